```python
import math
import jax, jax.numpy as jnp
from jax import lax
import numpy as np

D_MODEL = 1024
BATCH = 8
SEQ = 2048
DEPTH = 4

MIX_WIDTH = D_MODEL
POOL_WIDTH = MIX_WIDTH // 2
POOL_WINDOWS = (2, 4, 8, 16)
POOL_GROUPS = len(POOL_WINDOWS)
POOL_GC = POOL_WIDTH // POOL_GROUPS
N_HEADS = 4
QK_NOPE = 128
QK_ROPE = 64
V_HEAD = 128
QK_HEAD = QK_NOPE + QK_ROPE
MLA_WIDTH = N_HEADS * V_HEAD
Q_LORA = 384
KV_LORA = 256
ROPE_THETA = 10000.0
SOFTMAX_SCALE = 1.0 / math.sqrt(QK_HEAD)
Q_BLOCK = 128
IN_COLS = POOL_WIDTH + Q_LORA + KV_LORA + QK_ROPE
D_FF = 2816
N_SUBLAYERS = 3
EPS = 1e-6

kernel_name = "hybrid_macaron_pool_mla_adaln"


def rms_norm(x, g):
    xf = x.astype(jnp.float32)
    y = xf * lax.rsqrt(jnp.mean(xf * xf, axis=-1, keepdims=True) + EPS)
    return (y * g.astype(jnp.float32)).astype(x.dtype)


def modulate(h, shift, scale):
    return h * (1 + scale[:, None, :]) + shift[:, None, :]


def swiglu(h, w_gate, w_up, w_down):
    return (jax.nn.silu(h @ w_gate) * (h @ w_up)) @ w_down


def rotate_half(x):
    x1, x2 = jnp.split(x, 2, axis=-1)
    return jnp.concatenate([-x2, x1], axis=-1)


def apply_rope(x, cos, sin):
    return x * cos + rotate_half(x) * sin


def causal_multiscale_pool(u, pool_w, pool_scale):
    B, S, C = u.shape
    cs = jnp.cumsum(u.astype(jnp.float32), axis=1)
    pos = jnp.arange(S)
    means = []
    for g, w in enumerate(POOL_WINDOWS):
        csg = cs[..., g * POOL_GC:(g + 1) * POOL_GC]
        lag = jnp.pad(csg, ((0, 0), (w, 0), (0, 0)))[:, :S]
        cnt = jnp.minimum(pos + 1, w).astype(jnp.float32)[None, :, None]
        means.append((csg - lag) / cnt)
    pooled = jnp.stack(means, axis=2).astype(u.dtype)
    diff = pooled - u.reshape(B, S, POOL_GROUPS, POOL_GC)
    y = jnp.einsum('bsgc,gcd->bsgd', diff, pool_w).reshape(B, S, C)
    return y * pool_scale


def mla_attention(cq, ckv, kr, q_a_norm, w_q_b, kv_a_norm, w_kv_b, cos, sin):
    B, S, _ = cq.shape
    q = (rms_norm(cq, q_a_norm) @ w_q_b).reshape(B, S, N_HEADS, QK_HEAD)
    q_nope, q_rope = q[..., :QK_NOPE], q[..., QK_NOPE:]
    q_rope = apply_rope(q_rope, cos[:, :, None, :], sin[:, :, None, :])
    kv = (rms_norm(ckv, kv_a_norm) @ w_kv_b).reshape(B, S, N_HEADS, QK_NOPE + V_HEAD)
    k_nope, v = kv[..., :QK_NOPE], kv[..., QK_NOPE:]
    k_rope = apply_rope(kr, cos, sin)

    nb = S // Q_BLOCK
    qn = q_nope.reshape(B, nb, Q_BLOCK, N_HEADS, QK_NOPE).transpose(1, 0, 3, 2, 4)
    qr = q_rope.reshape(B, nb, Q_BLOCK, N_HEADS, QK_ROPE).transpose(1, 0, 3, 2, 4)
    kn = k_nope.transpose(0, 2, 1, 3)
    vv = v.transpose(0, 2, 1, 3)
    kpos = jnp.arange(S)

    def block(args):
        qn_b, qr_b, i = args
        s = (jnp.einsum('bhqd,bhkd->bhqk', qn_b, kn)
             + jnp.einsum('bhqd,bkd->bhqk', qr_b, k_rope)).astype(jnp.float32) * SOFTMAX_SCALE
        qpos = i * Q_BLOCK + jnp.arange(Q_BLOCK)
        s = jnp.where(qpos[:, None] >= kpos[None, :], s, -jnp.inf)
        p = jax.nn.softmax(s, axis=-1).astype(vv.dtype)
        return jnp.einsum('bhqk,bhkd->bhqd', p, vv)

    o = lax.map(block, (qn, qr, jnp.arange(nb)))
    return o.transpose(1, 0, 3, 2, 4).reshape(B, S, MLA_WIDTH)


def _fwd_setup_inputs(seed: int = 0) -> dict:
    key = jax.random.key(seed)
    ks = jax.random.split(key, 24)
    f32 = jnp.float32
    nrm = lambda k, shape, s: (jax.random.normal(k, shape, f32) * s)
    gain = lambda k, shape: 1.0 + 0.05 * jax.random.normal(k, shape, f32)
    D, F, L = D_MODEL, D_FF, DEPTH
    x = jax.random.normal(ks[0], (BATCH, SEQ, D), f32)
    c = jax.random.normal(ks[1], (BATCH, D), f32)
    positions = jnp.broadcast_to(jnp.arange(SEQ, dtype=jnp.int32)[None, :], (BATCH, SEQ))
    return {
        "x": x,
        "c": c,
        "positions": positions,
        "ada_w": nrm(ks[2], (L, D, 3 * N_SUBLAYERS * D), 0.5 * D ** -0.5),
        "ada_b": nrm(ks[3], (L, 3 * N_SUBLAYERS * D), 0.01),
        "ffn1_norm": gain(ks[4], (L, D)),
        "ffn1_w_gate": nrm(ks[5], (L, D, F), D ** -0.5),
        "ffn1_w_up": nrm(ks[6], (L, D, F), D ** -0.5),
        "ffn1_w_down": nrm(ks[7], (L, F, D), F ** -0.5),
        "mix_norm": gain(ks[8], (L, D)),
        "w_in": nrm(ks[9], (L, D, IN_COLS), D ** -0.5),
        "pool_w": nrm(ks[10], (L, POOL_GROUPS, POOL_GC, POOL_GC), POOL_GC ** -0.5),
        "pool_scale": gain(ks[11], (L, POOL_WIDTH)),
        "q_a_norm": gain(ks[12], (L, Q_LORA)),
        "w_q_b": nrm(ks[13], (L, Q_LORA, N_HEADS * QK_HEAD), Q_LORA ** -0.5),
        "kv_a_norm": gain(ks[14], (L, KV_LORA)),
        "w_kv_b": nrm(ks[15], (L, KV_LORA, N_HEADS * (QK_NOPE + V_HEAD)), KV_LORA ** -0.5),
        "w_out": nrm(ks[16], (L, MIX_WIDTH, D), MIX_WIDTH ** -0.5),
        "ffn2_norm": gain(ks[17], (L, D)),
        "ffn2_w_gate": nrm(ks[18], (L, D, F), D ** -0.5),
        "ffn2_w_up": nrm(ks[19], (L, D, F), D ** -0.5),
        "ffn2_w_down": nrm(ks[20], (L, F, D), F ** -0.5),
        "final_norm": gain(ks[21], (D,)),
    }


def _fwd_reference(x, c, positions, ada_w, ada_b, ffn1_norm, ffn1_w_gate, ffn1_w_up, ffn1_w_down,
              mix_norm, w_in, pool_w, pool_scale, q_a_norm, w_q_b, kv_a_norm, w_kv_b, w_out,
              ffn2_norm, ffn2_w_gate, ffn2_w_up, ffn2_w_down, final_norm):
    inv_freq = 1.0 / (ROPE_THETA ** (jnp.arange(0, QK_ROPE, 2, dtype=jnp.float32) / QK_ROPE))
    ang = positions.astype(jnp.float32)[..., None] * inv_freq
    ang = jnp.concatenate([ang, ang], axis=-1)
    cos = jnp.cos(ang).astype(x.dtype)
    sin = jnp.sin(ang).astype(x.dtype)
    c_act = jax.nn.silu(c)

    for l in range(DEPTH):
        mod = c_act @ ada_w[l] + ada_b[l]
        (sh1, sc1, g1, sh2, sc2, g2, sh3, sc3, g3) = jnp.split(mod, 3 * N_SUBLAYERS, axis=-1)

        h = modulate(rms_norm(x, ffn1_norm[l]), sh1, sc1)
        x = x + 0.5 * g1[:, None, :] * swiglu(h, ffn1_w_gate[l], ffn1_w_up[l], ffn1_w_down[l])

        h = modulate(rms_norm(x, mix_norm[l]), sh2, sc2)
        z = h @ w_in[l]
        o1 = POOL_WIDTH
        o2 = o1 + Q_LORA
        o3 = o2 + KV_LORA
        y_pool = causal_multiscale_pool(z[..., :o1], pool_w[l], pool_scale[l])
        y_mla = mla_attention(z[..., o1:o2], z[..., o2:o3], z[..., o3:], q_a_norm[l], w_q_b[l],
                              kv_a_norm[l], w_kv_b[l], cos, sin)
        y = jnp.concatenate([y_pool, y_mla], axis=-1) @ w_out[l]
        x = x + g2[:, None, :] * y

        h = modulate(rms_norm(x, ffn2_norm[l]), sh3, sc3)
        x = x + 0.5 * g3[:, None, :] * swiglu(h, ffn2_w_gate[l], ffn2_w_up[l], ffn2_w_down[l])

    return rms_norm(x, final_norm)


import jax as _jax
import jax.numpy as _jnp

TWIN_FORMAT = 'train_step'
FWD_PARAMS = ['x', 'c', 'positions', 'ada_w', 'ada_b', 'ffn1_norm', 'ffn1_w_gate', 'ffn1_w_up', 'ffn1_w_down', 'mix_norm', 'w_in', 'pool_w', 'pool_scale', 'q_a_norm', 'w_q_b', 'kv_a_norm', 'w_kv_b', 'w_out', 'ffn2_norm', 'ffn2_w_gate', 'ffn2_w_up', 'ffn2_w_down', 'final_norm']
TWIN_WEIGHTS = ['ada_w', 'ada_b', 'ffn1_norm', 'ffn1_w_gate', 'ffn1_w_up', 'ffn1_w_down', 'mix_norm', 'w_in', 'pool_w', 'pool_scale', 'q_a_norm', 'w_q_b', 'kv_a_norm', 'w_kv_b', 'w_out', 'ffn2_norm', 'ffn2_w_gate', 'ffn2_w_up', 'ffn2_w_down', 'final_norm']
TWIN_DIFF_INPUT = 'x'
TWIN_INPUTS = ['x', 'c', 'positions', 'ada_w', 'ada_b', 'ffn1_norm', 'ffn1_w_gate', 'ffn1_w_up', 'ffn1_w_down', 'mix_norm', 'w_in', 'pool_w', 'pool_scale', 'q_a_norm', 'w_q_b', 'kv_a_norm', 'w_kv_b', 'w_out', 'ffn2_norm', 'ffn2_w_gate', 'ffn2_w_up', 'ffn2_w_down', 'final_norm', 'loss_target', 'm_ada_w', 'm_ada_b', 'm_ffn1_norm', 'm_ffn1_w_gate', 'm_ffn1_w_up', 'm_ffn1_w_down', 'm_mix_norm', 'm_w_in', 'm_pool_w', 'm_pool_scale', 'm_q_a_norm', 'm_w_q_b', 'm_kv_a_norm', 'm_w_kv_b', 'm_w_out', 'm_ffn2_norm', 'm_ffn2_w_gate', 'm_ffn2_w_up', 'm_ffn2_w_down', 'm_final_norm', 'v_ada_w', 'v_ada_b', 'v_ffn1_norm', 'v_ffn1_w_gate', 'v_ffn1_w_up', 'v_ffn1_w_down', 'v_mix_norm', 'v_w_in', 'v_pool_w', 'v_pool_scale', 'v_q_a_norm', 'v_w_q_b', 'v_kv_a_norm', 'v_w_kv_b', 'v_w_out', 'v_ffn2_norm', 'v_ffn2_w_gate', 'v_ffn2_w_up', 'v_ffn2_w_down', 'v_final_norm']
TWIN_OUTPUTS = ['loss', 'grad_x', 'grad_ada_w', 'grad_ada_b', 'grad_ffn1_norm', 'grad_ffn1_w_gate', 'grad_ffn1_w_up', 'grad_ffn1_w_down', 'grad_mix_norm', 'grad_w_in', 'grad_pool_w', 'grad_pool_scale', 'grad_q_a_norm', 'grad_w_q_b', 'grad_kv_a_norm', 'grad_w_kv_b', 'grad_w_out', 'grad_ffn2_norm', 'grad_ffn2_w_gate', 'grad_ffn2_w_up', 'grad_ffn2_w_down', 'grad_final_norm', 'delta_ada_w', 'delta_ada_b', 'delta_ffn1_norm', 'delta_ffn1_w_gate', 'delta_ffn1_w_up', 'delta_ffn1_w_down', 'delta_mix_norm', 'delta_w_in', 'delta_pool_w', 'delta_pool_scale', 'delta_q_a_norm', 'delta_w_q_b', 'delta_kv_a_norm', 'delta_w_kv_b', 'delta_w_out', 'delta_ffn2_norm', 'delta_ffn2_w_gate', 'delta_ffn2_w_up', 'delta_ffn2_w_down', 'delta_final_norm', 'new_m_ada_w', 'new_m_ada_b', 'new_m_ffn1_norm', 'new_m_ffn1_w_gate', 'new_m_ffn1_w_up', 'new_m_ffn1_w_down', 'new_m_mix_norm', 'new_m_w_in', 'new_m_pool_w', 'new_m_pool_scale', 'new_m_q_a_norm', 'new_m_w_q_b', 'new_m_kv_a_norm', 'new_m_w_kv_b', 'new_m_w_out', 'new_m_ffn2_norm', 'new_m_ffn2_w_gate', 'new_m_ffn2_w_up', 'new_m_ffn2_w_down', 'new_m_final_norm', 'new_v_ada_w', 'new_v_ada_b', 'new_v_ffn1_norm', 'new_v_ffn1_w_gate', 'new_v_ffn1_w_up', 'new_v_ffn1_w_down', 'new_v_mix_norm', 'new_v_w_in', 'new_v_pool_w', 'new_v_pool_scale', 'new_v_q_a_norm', 'new_v_w_q_b', 'new_v_kv_a_norm', 'new_v_w_kv_b', 'new_v_w_out', 'new_v_ffn2_norm', 'new_v_ffn2_w_gate', 'new_v_ffn2_w_up', 'new_v_ffn2_w_down', 'new_v_final_norm']
TWIN_LEAF_KINDS = {'loss': 'loss', 'grad_x': 'grad_x', 'grad_ada_w': 'grad_w', 'grad_ada_b': 'grad_w', 'grad_ffn1_norm': 'grad_w', 'grad_ffn1_w_gate': 'grad_w', 'grad_ffn1_w_up': 'grad_w', 'grad_ffn1_w_down': 'grad_w', 'grad_mix_norm': 'grad_w', 'grad_w_in': 'grad_w', 'grad_pool_w': 'grad_w', 'grad_pool_scale': 'grad_w', 'grad_q_a_norm': 'grad_w', 'grad_w_q_b': 'grad_w', 'grad_kv_a_norm': 'grad_w', 'grad_w_kv_b': 'grad_w', 'grad_w_out': 'grad_w', 'grad_ffn2_norm': 'grad_w', 'grad_ffn2_w_gate': 'grad_w', 'grad_ffn2_w_up': 'grad_w', 'grad_ffn2_w_down': 'grad_w', 'grad_final_norm': 'grad_w', 'delta_ada_w': 'delta_w', 'delta_ada_b': 'delta_w', 'delta_ffn1_norm': 'delta_w', 'delta_ffn1_w_gate': 'delta_w', 'delta_ffn1_w_up': 'delta_w', 'delta_ffn1_w_down': 'delta_w', 'delta_mix_norm': 'delta_w', 'delta_w_in': 'delta_w', 'delta_pool_w': 'delta_w', 'delta_pool_scale': 'delta_w', 'delta_q_a_norm': 'delta_w', 'delta_w_q_b': 'delta_w', 'delta_kv_a_norm': 'delta_w', 'delta_w_kv_b': 'delta_w', 'delta_w_out': 'delta_w', 'delta_ffn2_norm': 'delta_w', 'delta_ffn2_w_gate': 'delta_w', 'delta_ffn2_w_up': 'delta_w', 'delta_ffn2_w_down': 'delta_w', 'delta_final_norm': 'delta_w', 'new_m_ada_w': 'new_m', 'new_m_ada_b': 'new_m', 'new_m_ffn1_norm': 'new_m', 'new_m_ffn1_w_gate': 'new_m', 'new_m_ffn1_w_up': 'new_m', 'new_m_ffn1_w_down': 'new_m', 'new_m_mix_norm': 'new_m', 'new_m_w_in': 'new_m', 'new_m_pool_w': 'new_m', 'new_m_pool_scale': 'new_m', 'new_m_q_a_norm': 'new_m', 'new_m_w_q_b': 'new_m', 'new_m_kv_a_norm': 'new_m', 'new_m_w_kv_b': 'new_m', 'new_m_w_out': 'new_m', 'new_m_ffn2_norm': 'new_m', 'new_m_ffn2_w_gate': 'new_m', 'new_m_ffn2_w_up': 'new_m', 'new_m_ffn2_w_down': 'new_m', 'new_m_final_norm': 'new_m', 'new_v_ada_w': 'new_v', 'new_v_ada_b': 'new_v', 'new_v_ffn1_norm': 'new_v', 'new_v_ffn1_w_gate': 'new_v', 'new_v_ffn1_w_up': 'new_v', 'new_v_ffn1_w_down': 'new_v', 'new_v_mix_norm': 'new_v', 'new_v_w_in': 'new_v', 'new_v_pool_w': 'new_v', 'new_v_pool_scale': 'new_v', 'new_v_q_a_norm': 'new_v', 'new_v_w_q_b': 'new_v', 'new_v_kv_a_norm': 'new_v', 'new_v_w_kv_b': 'new_v', 'new_v_w_out': 'new_v', 'new_v_ffn2_norm': 'new_v', 'new_v_ffn2_w_gate': 'new_v', 'new_v_ffn2_w_up': 'new_v', 'new_v_ffn2_w_down': 'new_v', 'new_v_final_norm': 'new_v'}


def _forward(args):
    return _fwd_reference(*[args[k] for k in FWD_PARAMS])


def _output_shape():
    out = _jax.eval_shape(lambda: _forward(_fwd_setup_inputs(0)))
    return out.shape, out.dtype

N_MICROBATCH = 1
ADAM_LR = 0.001
ADAM_B1 = 0.9
ADAM_B2 = 0.999
ADAM_EPS = 1e-08
ADAM_WD = 0.01
ADAM_STEP = 10
PER_EXAMPLE_BATCH_AXIS = {'x': 0, 'c': 0, 'positions': 0, 'loss_target': 0}
SHARED_INPUTS = []
_WEIGHT_DTYPES = {'ada_w': _jnp.float32, 'ada_b': _jnp.float32, 'ffn1_norm': _jnp.float32, 'ffn1_w_gate': _jnp.float32, 'ffn1_w_up': _jnp.float32, 'ffn1_w_down': _jnp.float32, 'mix_norm': _jnp.float32, 'w_in': _jnp.float32, 'pool_w': _jnp.float32, 'pool_scale': _jnp.float32, 'q_a_norm': _jnp.float32, 'w_q_b': _jnp.float32, 'kv_a_norm': _jnp.float32, 'w_kv_b': _jnp.float32, 'w_out': _jnp.float32, 'ffn2_norm': _jnp.float32, 'ffn2_w_gate': _jnp.float32, 'ffn2_w_up': _jnp.float32, 'ffn2_w_down': _jnp.float32, 'final_norm': _jnp.float32}
MOMENT_SCALE = {'ada_w': 2.730083e-02, 'ada_b': 4.593247e-02, 'ffn1_norm': 1.892342e-02, 'ffn1_w_gate': 8.467768e-03, 'ffn1_w_up': 8.218307e-03, 'ffn1_w_down': 1.360664e-02, 'mix_norm': 2.528930e-02, 'w_in': 2.417729e-02, 'pool_w': 3.411228e-02, 'pool_scale': 3.352073e-02, 'q_a_norm': 7.472133e-03, 'w_q_b': 5.190190e-03, 'kv_a_norm': 2.088794e-02, 'w_kv_b': 1.010985e-02, 'w_out': 2.553213e-02, 'ffn2_norm': 1.865820e-02, 'ffn2_w_gate': 8.296617e-03, 'ffn2_w_up': 8.041054e-03, 'ffn2_w_down': 1.335585e-02, 'final_norm': 1.608238e+01}


def _to_microbatches(a, axis):
    t = _jnp.moveaxis(a, axis, 0)
    t = t.reshape((N_MICROBATCH, t.shape[0] // N_MICROBATCH) + t.shape[1:])
    return _jnp.moveaxis(t, 1, axis + 1)


def setup_inputs(seed: int = 0) -> dict:
    inp = _fwd_setup_inputs(seed)
    key = _jax.random.fold_in(_jax.random.key(seed), 7919)
    shape, _ = _output_shape()
    out = dict(inp)
    out["loss_target"] = _jax.random.normal(_jax.random.fold_in(key, 0), shape, _jnp.float32)
    for i, name in enumerate(TWIN_WEIGHTS):
        w = inp[name].astype(_jnp.float32)
        if MOMENT_SCALE is None:
            s = _jnp.sqrt(_jnp.mean(_jnp.square(w)) + 1e-30)
        else:
            s = MOMENT_SCALE[name]
        km, kv = _jax.random.split(_jax.random.fold_in(key, i + 1))
        out[name] = w
        out["m_" + name] = s * _jax.random.normal(km, w.shape, _jnp.float32)
        out["v_" + name] = (s * s) * _jax.random.uniform(kv, w.shape, _jnp.float32, 0.5, 1.5)
    if N_MICROBATCH > 1:
        for name, axis in PER_EXAMPLE_BATCH_AXIS.items():
            out[name] = _to_microbatches(out[name], axis)
    return {'x': out['x'], 'c': out['c'], 'positions': out['positions'], 'ada_w': out['ada_w'], 'ada_b': out['ada_b'], 'ffn1_norm': out['ffn1_norm'], 'ffn1_w_gate': out['ffn1_w_gate'], 'ffn1_w_up': out['ffn1_w_up'], 'ffn1_w_down': out['ffn1_w_down'], 'mix_norm': out['mix_norm'], 'w_in': out['w_in'], 'pool_w': out['pool_w'], 'pool_scale': out['pool_scale'], 'q_a_norm': out['q_a_norm'], 'w_q_b': out['w_q_b'], 'kv_a_norm': out['kv_a_norm'], 'w_kv_b': out['w_kv_b'], 'w_out': out['w_out'], 'ffn2_norm': out['ffn2_norm'], 'ffn2_w_gate': out['ffn2_w_gate'], 'ffn2_w_up': out['ffn2_w_up'], 'ffn2_w_down': out['ffn2_w_down'], 'final_norm': out['final_norm'], 'loss_target': out['loss_target'], 'm_ada_w': out['m_ada_w'], 'm_ada_b': out['m_ada_b'], 'm_ffn1_norm': out['m_ffn1_norm'], 'm_ffn1_w_gate': out['m_ffn1_w_gate'], 'm_ffn1_w_up': out['m_ffn1_w_up'], 'm_ffn1_w_down': out['m_ffn1_w_down'], 'm_mix_norm': out['m_mix_norm'], 'm_w_in': out['m_w_in'], 'm_pool_w': out['m_pool_w'], 'm_pool_scale': out['m_pool_scale'], 'm_q_a_norm': out['m_q_a_norm'], 'm_w_q_b': out['m_w_q_b'], 'm_kv_a_norm': out['m_kv_a_norm'], 'm_w_kv_b': out['m_w_kv_b'], 'm_w_out': out['m_w_out'], 'm_ffn2_norm': out['m_ffn2_norm'], 'm_ffn2_w_gate': out['m_ffn2_w_gate'], 'm_ffn2_w_up': out['m_ffn2_w_up'], 'm_ffn2_w_down': out['m_ffn2_w_down'], 'm_final_norm': out['m_final_norm'], 'v_ada_w': out['v_ada_w'], 'v_ada_b': out['v_ada_b'], 'v_ffn1_norm': out['v_ffn1_norm'], 'v_ffn1_w_gate': out['v_ffn1_w_gate'], 'v_ffn1_w_up': out['v_ffn1_w_up'], 'v_ffn1_w_down': out['v_ffn1_w_down'], 'v_mix_norm': out['v_mix_norm'], 'v_w_in': out['v_w_in'], 'v_pool_w': out['v_pool_w'], 'v_pool_scale': out['v_pool_scale'], 'v_q_a_norm': out['v_q_a_norm'], 'v_w_q_b': out['v_w_q_b'], 'v_kv_a_norm': out['v_kv_a_norm'], 'v_w_kv_b': out['v_w_kv_b'], 'v_w_out': out['v_w_out'], 'v_ffn2_norm': out['v_ffn2_norm'], 'v_ffn2_w_gate': out['v_ffn2_w_gate'], 'v_ffn2_w_up': out['v_ffn2_w_up'], 'v_ffn2_w_down': out['v_ffn2_w_down'], 'v_final_norm': out['v_final_norm']}


def _loss(weights, diff, rest, loss_target):
    with _jax.named_scope("forward"):
        args = {**rest, TWIN_DIFF_INPUT: diff, **{k: w.astype(_WEIGHT_DTYPES[k]) for k, w in weights.items()}}
        y = _forward(args)
    with _jax.named_scope("loss_head"):
        err = _jnp.square(y.astype(_jnp.float32) - loss_target)
        return 0.5 * _jnp.sum(_jnp.mean(err, axis=-1)) if err.ndim else 0.5 * err


def _adamw(w, g, m, v):
    m = ADAM_B1 * m + (1.0 - ADAM_B1) * g
    v = ADAM_B2 * v + (1.0 - ADAM_B2) * _jnp.square(g)
    m_hat = m / (1.0 - ADAM_B1 ** ADAM_STEP)
    v_hat = v / (1.0 - ADAM_B2 ** ADAM_STEP)
    delta = -ADAM_LR * (m_hat / (_jnp.sqrt(v_hat) + ADAM_EPS) + ADAM_WD * w)
    return delta, m, v


def reference(x, c, positions, ada_w, ada_b, ffn1_norm, ffn1_w_gate, ffn1_w_up, ffn1_w_down, mix_norm, w_in, pool_w, pool_scale, q_a_norm, w_q_b, kv_a_norm, w_kv_b, w_out, ffn2_norm, ffn2_w_gate, ffn2_w_up, ffn2_w_down, final_norm, loss_target, m_ada_w, m_ada_b, m_ffn1_norm, m_ffn1_w_gate, m_ffn1_w_up, m_ffn1_w_down, m_mix_norm, m_w_in, m_pool_w, m_pool_scale, m_q_a_norm, m_w_q_b, m_kv_a_norm, m_w_kv_b, m_w_out, m_ffn2_norm, m_ffn2_w_gate, m_ffn2_w_up, m_ffn2_w_down, m_final_norm, v_ada_w, v_ada_b, v_ffn1_norm, v_ffn1_w_gate, v_ffn1_w_up, v_ffn1_w_down, v_mix_norm, v_w_in, v_pool_w, v_pool_scale, v_q_a_norm, v_w_q_b, v_kv_a_norm, v_w_kv_b, v_w_out, v_ffn2_norm, v_ffn2_w_gate, v_ffn2_w_up, v_ffn2_w_down, v_final_norm):
    given = dict(x=x, c=c, positions=positions, ada_w=ada_w, ada_b=ada_b, ffn1_norm=ffn1_norm, ffn1_w_gate=ffn1_w_gate, ffn1_w_up=ffn1_w_up, ffn1_w_down=ffn1_w_down, mix_norm=mix_norm, w_in=w_in, pool_w=pool_w, pool_scale=pool_scale, q_a_norm=q_a_norm, w_q_b=w_q_b, kv_a_norm=kv_a_norm, w_kv_b=w_kv_b, w_out=w_out, ffn2_norm=ffn2_norm, ffn2_w_gate=ffn2_w_gate, ffn2_w_up=ffn2_w_up, ffn2_w_down=ffn2_w_down, final_norm=final_norm, loss_target=loss_target, m_ada_w=m_ada_w, m_ada_b=m_ada_b, m_ffn1_norm=m_ffn1_norm, m_ffn1_w_gate=m_ffn1_w_gate, m_ffn1_w_up=m_ffn1_w_up, m_ffn1_w_down=m_ffn1_w_down, m_mix_norm=m_mix_norm, m_w_in=m_w_in, m_pool_w=m_pool_w, m_pool_scale=m_pool_scale, m_q_a_norm=m_q_a_norm, m_w_q_b=m_w_q_b, m_kv_a_norm=m_kv_a_norm, m_w_kv_b=m_w_kv_b, m_w_out=m_w_out, m_ffn2_norm=m_ffn2_norm, m_ffn2_w_gate=m_ffn2_w_gate, m_ffn2_w_up=m_ffn2_w_up, m_ffn2_w_down=m_ffn2_w_down, m_final_norm=m_final_norm, v_ada_w=v_ada_w, v_ada_b=v_ada_b, v_ffn1_norm=v_ffn1_norm, v_ffn1_w_gate=v_ffn1_w_gate, v_ffn1_w_up=v_ffn1_w_up, v_ffn1_w_down=v_ffn1_w_down, v_mix_norm=v_mix_norm, v_w_in=v_w_in, v_pool_w=v_pool_w, v_pool_scale=v_pool_scale, v_q_a_norm=v_q_a_norm, v_w_q_b=v_w_q_b, v_kv_a_norm=v_kv_a_norm, v_w_kv_b=v_w_kv_b, v_w_out=v_w_out, v_ffn2_norm=v_ffn2_norm, v_ffn2_w_gate=v_ffn2_w_gate, v_ffn2_w_up=v_ffn2_w_up, v_ffn2_w_down=v_ffn2_w_down, v_final_norm=v_final_norm)
    weights = {n: given[n] for n in TWIN_WEIGHTS}
    shared = {n: given[n] for n in SHARED_INPUTS}
    per_example = {n: given[n] for n in ['x', 'c', 'positions']}
    grad_fn = _jax.value_and_grad(_loss, argnums=(0, 1))

    def one_microbatch(ex, loss_target):
        ex = dict(ex)
        diff = ex.pop(TWIN_DIFF_INPUT)
        return grad_fn(weights, diff, {**shared, **ex}, loss_target)

    if N_MICROBATCH == 1:
        loss, (grad_w, grad_x) = one_microbatch(per_example, given["loss_target"])
    else:
        def body(carry, xs):
            loss_sum, grad_sum = carry
            l_k, (gw_k, gx_k) = one_microbatch(xs[0], xs[1])
            with _jax.named_scope("update"):
                return (loss_sum + l_k, _jax.tree.map(_jnp.add, grad_sum, gw_k)), gx_k

        init = (_jnp.zeros((), _jnp.float32), _jax.tree.map(_jnp.zeros_like, weights))
        (loss, grad_w), grad_x = _jax.lax.scan(body, init, (per_example, given["loss_target"]))
    with _jax.named_scope("update"):
        delta_w, new_m, new_v = {}, {}, {}
        for n in TWIN_WEIGHTS:
            delta_w[n], new_m[n], new_v[n] = _adamw(weights[n], grad_w[n], given["m_" + n], given["v_" + n])
    return (loss, grad_x, *[grad_w[n] for n in TWIN_WEIGHTS], *[delta_w[n] for n in TWIN_WEIGHTS],
            *[new_m[n] for n in TWIN_WEIGHTS], *[new_v[n] for n in TWIN_WEIGHTS])
```

```python
import math

import jax
import jax.numpy as jnp
from jax import lax
from jax.experimental import pallas as pl
from jax.experimental.pallas import tpu as pltpu

F32 = jnp.float32
BF16 = jnp.bfloat16
MESH = pl.DeviceIdType.MESH

EPS = 1e-6
ROPE_THETA = 10000.0
N_HEADS = 4
QK_NOPE = 128
QK_ROPE = 64
V_HEAD = 128
POOL_WINDOWS = (2, 4, 8, 16)
POOL_GC = 128
POOL_WIDTH = POOL_GC * len(POOL_WINDOWS)
Q_LORA = 384
KV_LORA = 256
SOFTMAX_SCALE = 1.0 / math.sqrt(QK_NOPE + QK_ROPE)
N_CHIPS = 4
N_DEV = 8

ADAM_LR = 0.001
ADAM_B1 = 0.9
ADAM_B2 = 0.999
ADAM_EPS = 1e-08
ADAM_WD = 0.01
ADAM_STEP = 10

ROW_TILE = 512
ATT_TILE = 256
VMEM_LIMIT = 56 * 1024 * 1024


def _params(sem=None, vmem=VMEM_LIMIT):
    return pltpu.CompilerParams(dimension_semantics=sem, vmem_limit_bytes=vmem)


def _dot(a, b):
    return jnp.dot(a, b, preferred_element_type=F32)


def _dot_nt(a, b):
    return lax.dot_general(a, b, (((1,), (1,)), ((), ())), preferred_element_type=F32)


def _dot_tn(a, b):
    return lax.dot_general(a, b, (((0,), (0,)), ((), ())), preferred_element_type=F32)


def _dot_exact(t, perm):
    t1 = t.astype(BF16)
    r1 = t - t1.astype(F32)
    t2 = r1.astype(BF16)
    t3 = (r1 - t2.astype(F32)).astype(BF16)
    return _dot(t1, perm) + _dot(t2, perm) + _dot(t3, perm)


def _sum0(a):
    return jnp.sum(a, axis=0, keepdims=True)


def _rms(xt):
    r = lax.rsqrt(jnp.mean(xt * xt, axis=-1, keepdims=True) + EPS)
    return xt * r, r


def _rms_bwd(dy, xt, g):
    xhat, r = _rms(xt)
    dxhat = dy * g
    dx = r * (dxhat - xhat * jnp.mean(dxhat * xhat, axis=-1, keepdims=True))
    return dx, _sum0(dy * xhat)


def _normmod_bwd(dh, xt, gn, sc):
    xhat, _ = _rms(xt)
    dn = dh * (1.0 + sc)
    dx, dgn = _rms_bwd(dn, xt, gn)
    return dx, _sum0(dh), _sum0(dh * (xhat * gn)), dgn


def _row_tile(s):
    return min(s, ROW_TILE)


def _full(shape):
    n = len(shape)
    return pl.BlockSpec(shape, lambda *_: (0,) * n)


def ffn_fwd(x, gn, sh, sc, gt, wg, wu, wd):
    s, d = x.shape
    k_chunks, _, fs = wg.shape
    tm = _row_tile(s)

    def body(x_ref, gn_ref, sh_ref, sc_ref, gt_ref, wg_ref, wu_ref, wd_ref,
             xo_ref, h_ref, gate_ref, up_ref, y_ref, acc_ref):
        k = pl.program_id(1)

        @pl.when(k == 0)
        def _():
            xhat, _ = _rms(x_ref[...])
            h = xhat * gn_ref[...] * (1.0 + sc_ref[...]) + sh_ref[...]
            h_ref[...] = h.astype(BF16)
            acc_ref[...] = jnp.zeros_like(acc_ref)

        h = h_ref[...]
        gate = _dot(h, wg_ref[...])
        up = _dot(h, wu_ref[...])
        gate_ref[...] = gate.astype(BF16)
        up_ref[...] = up.astype(BF16)
        a = (gate * jax.nn.sigmoid(gate) * up).astype(BF16)
        acc_ref[...] += _dot(a, wd_ref[...])

        @pl.when(k == k_chunks - 1)
        def _():
            y = acc_ref[...]
            y_ref[...] = y.astype(BF16)
            xo_ref[...] = x_ref[...] + 0.5 * gt_ref[...] * y

    row = pl.BlockSpec((tm, d), lambda i, k: (i, 0))
    vec = pl.BlockSpec((1, d), lambda i, k: (0, 0))
    wcol = pl.BlockSpec((None, d, fs), lambda i, k: (k, 0, 0))
    wrow = pl.BlockSpec((None, fs, d), lambda i, k: (k, 0, 0))
    act = pl.BlockSpec((None, tm, fs), lambda i, k: (k, i, 0))
    return pl.pallas_call(
        body, name="ffn_fwd",
        grid=(s // tm, k_chunks),
        in_specs=[row, vec, vec, vec, vec, wcol, wcol, wrow],
        out_specs=[row, row, act, act, row],
        out_shape=[jax.ShapeDtypeStruct((s, d), F32), jax.ShapeDtypeStruct((s, d), BF16),
                   jax.ShapeDtypeStruct((k_chunks, s, fs), BF16), jax.ShapeDtypeStruct((k_chunks, s, fs), BF16),
                   jax.ShapeDtypeStruct((s, d), BF16)],
        scratch_shapes=[pltpu.VMEM((tm, d), F32)],
        compiler_params=_params(("arbitrary", "arbitrary")),
    )(x, gn, sh, sc, gt, wg, wu, wd)


def ffn_bwd(dxn, x, y, gate, up, gn, sc, gt, wg, wu, wd):
    s, d = x.shape
    k_chunks, _, fs = wg.shape
    tm = _row_tile(s)

    def body(dxn_ref, x_ref, y_ref, gate_ref, up_ref, gn_ref, sc_ref, gt_ref, wg_ref, wu_ref, wd_ref,
             dx_ref, dy_ref, a_ref, dgate_ref, dup_ref, dvec_ref, acc_ref):
        i = pl.program_id(0)
        k = pl.program_id(1)

        @pl.when(k == 0)
        def _():
            dy_ref[...] = (0.5 * gt_ref[...] * dxn_ref[...]).astype(BF16)
            acc_ref[...] = jnp.zeros_like(acc_ref)

        da = _dot_nt(dy_ref[...], wd_ref[...])
        g = gate_ref[...].astype(F32)
        u = up_ref[...].astype(F32)
        sg = jax.nn.sigmoid(g)
        sl = g * sg
        a_ref[...] = (sl * u).astype(BF16)
        dg = (da * u * (sg * (1.0 + g * (1.0 - sg)))).astype(BF16)
        du = (da * sl).astype(BF16)
        dgate_ref[...] = dg
        dup_ref[...] = du
        acc_ref[...] += _dot_nt(dg, wg_ref[...]) + _dot_nt(du, wu_ref[...])

        @pl.when(k == k_chunks - 1)
        def _():
            @pl.when(i == 0)
            def _():
                dvec_ref[...] = jnp.zeros_like(dvec_ref)

            dxn_t = dxn_ref[...]
            dx, dsh, dsc, dgn = _normmod_bwd(acc_ref[...], x_ref[...], gn_ref[...], sc_ref[...])
            dx_ref[...] = dx + dxn_t
            dvec_ref[0:1, :] += dsh
            dvec_ref[1:2, :] += dsc
            dvec_ref[2:3, :] += _sum0(0.5 * dxn_t * y_ref[...].astype(F32))
            dvec_ref[3:4, :] += dgn

    row = pl.BlockSpec((tm, d), lambda i, k: (i, 0))
    vec = pl.BlockSpec((1, d), lambda i, k: (0, 0))
    wcol = pl.BlockSpec((None, d, fs), lambda i, k: (k, 0, 0))
    wrow = pl.BlockSpec((None, fs, d), lambda i, k: (k, 0, 0))
    act = pl.BlockSpec((None, tm, fs), lambda i, k: (k, i, 0))
    return pl.pallas_call(
        body, name="ffn_bwd",
        grid=(s // tm, k_chunks),
        in_specs=[row, row, row, act, act, vec, vec, vec, wcol, wcol, wrow],
        out_specs=[row, row, act, act, act, pl.BlockSpec((8, d), lambda i, k: (0, 0))],
        out_shape=[jax.ShapeDtypeStruct((s, d), F32), jax.ShapeDtypeStruct((s, d), BF16),
                   jax.ShapeDtypeStruct((k_chunks, s, fs), BF16), jax.ShapeDtypeStruct((k_chunks, s, fs), BF16),
                   jax.ShapeDtypeStruct((k_chunks, s, fs), BF16), jax.ShapeDtypeStruct((8, d), F32)],
        scratch_shapes=[pltpu.VMEM((tm, d), F32)],
        compiler_params=_params(("arbitrary", "arbitrary")),
    )(dxn, x, y, gate, up, gn, sc, gt, wg, wu, wd)


def tn_mm(a, b, layer, n_layers, acc=None):
    ga, s, m = a.shape
    gb, _, n = b.shape
    g = max(ga, gb)
    ts = _row_tile(s)

    def body(l_ref, a_ref, b_ref, *rest):
        o_ref = rest[-1]
        si = pl.program_id(1)

        @pl.when(si == 0)
        def _():
            o_ref[...] = jnp.zeros_like(o_ref)

        o_ref[...] += _dot_tn(a_ref[...], b_ref[...])

    a_spec = pl.BlockSpec((None, ts, m), (lambda gi, si, l: (gi, si, 0)) if ga > 1 else (lambda gi, si, l: (0, si, 0)))
    b_spec = pl.BlockSpec((None, ts, n), (lambda gi, si, l: (gi, si, 0)) if gb > 1 else (lambda gi, si, l: (0, si, 0)))
    o_spec = pl.BlockSpec((None, None, m, n), lambda gi, si, l: (l[0], gi, 0, 0))
    in_specs = [a_spec, b_spec]
    args = [jnp.full((1,), layer, jnp.int32), a, b]
    aliases = {}
    if acc is not None:
        in_specs.append(pl.BlockSpec(memory_space=pl.ANY))
        args.append(acc)
        aliases = {3: 0}
    return pl.pallas_call(
        body, name="tn_mm",
        grid_spec=pltpu.PrefetchScalarGridSpec(
            num_scalar_prefetch=1, grid=(g, s // ts), in_specs=in_specs, out_specs=o_spec),
        out_shape=jax.ShapeDtypeStruct((n_layers, g, m, n), F32),
        input_output_aliases=aliases,
        compiler_params=_params(("arbitrary", "arbitrary")),
    )(*args)


def mix_in_fwd(x, gn, sh, sc, w_main, w_kr):
    s, d = x.shape
    tm = _row_tile(s)
    o1, o2, o3 = POOL_WIDTH, POOL_WIDTH + Q_LORA, POOL_WIDTH + Q_LORA + KV_LORA

    def body(x_ref, gn_ref, sh_ref, sc_ref, w_ref, wkr_ref, h_ref, u_ref, cq_ref, ckv_ref, kr_ref):
        xhat, _ = _rms(x_ref[...])
        h = (xhat * gn_ref[...] * (1.0 + sc_ref[...]) + sh_ref[...]).astype(BF16)
        h_ref[...] = h
        z = _dot(h, w_ref[...])
        u_ref[...] = z[:, 0:o1]
        cq_ref[...] = z[:, o1:o2]
        ckv_ref[...] = z[:, o2:o3]
        kr_ref[...] = _dot(h, wkr_ref[...])

    row = lambda w: pl.BlockSpec((tm, w), lambda i: (i, 0))
    vec = pl.BlockSpec((1, d), lambda i: (0, 0))
    return pl.pallas_call(
        body, name="mix_in_fwd",
        grid=(s // tm,),
        in_specs=[row(d), vec, vec, vec, _full(w_main.shape), _full(w_kr.shape)],
        out_specs=[row(d), row(POOL_WIDTH), row(Q_LORA), row(KV_LORA), row(QK_ROPE)],
        out_shape=[jax.ShapeDtypeStruct((s, d), BF16), jax.ShapeDtypeStruct((s, POOL_WIDTH), F32),
                   jax.ShapeDtypeStruct((s, Q_LORA), F32), jax.ShapeDtypeStruct((s, KV_LORA), F32),
                   jax.ShapeDtypeStruct((s, QK_ROPE), F32)],
        compiler_params=_params(("arbitrary",)),
    )(x, gn, sh, sc, w_main, w_kr)


def mix_in_bwd(dxn, du, dcq, dckv, dkr, x, gn, sc, w_main, w_kr):
    s, d = x.shape
    tm = _row_tile(s)
    o1, o2, o3 = POOL_WIDTH, POOL_WIDTH + Q_LORA, POOL_WIDTH + Q_LORA + KV_LORA

    def body(dxn_ref, du_ref, dcq_ref, dckv_ref, dkr_ref, x_ref, gn_ref, sc_ref, w_ref, wkr_ref,
             dx_ref, dz_ref, dzkr_ref, dvec_ref):
        i = pl.program_id(0)

        @pl.when(i == 0)
        def _():
            dvec_ref[...] = jnp.zeros_like(dvec_ref)

        dub = du_ref[...].astype(BF16)
        dqb = dcq_ref[...].astype(BF16)
        dkb = dckv_ref[...].astype(BF16)
        drb = dkr_ref[...].astype(BF16)
        dz_ref[:, 0:o1] = dub
        dz_ref[:, o1:o2] = dqb
        dz_ref[:, o2:o3] = dkb
        dzkr_ref[...] = drb
        dh = (_dot_nt(dub, w_ref[:, 0:o1]) + _dot_nt(dqb, w_ref[:, o1:o2]) + _dot_nt(dkb, w_ref[:, o2:o3])
              + _dot_nt(drb, wkr_ref[...]))
        dx, dsh, dsc, dgn = _normmod_bwd(dh, x_ref[...], gn_ref[...], sc_ref[...])
        dx_ref[...] = dx + dxn_ref[...]
        dvec_ref[0:1, :] += dsh
        dvec_ref[1:2, :] += dsc
        dvec_ref[3:4, :] += dgn

    row = lambda w: pl.BlockSpec((tm, w), lambda i: (i, 0))
    vec = pl.BlockSpec((1, d), lambda i: (0, 0))
    return pl.pallas_call(
        body, name="mix_in_bwd",
        grid=(s // tm,),
        in_specs=[row(d), row(POOL_WIDTH), row(Q_LORA), row(KV_LORA), row(QK_ROPE), row(d), vec, vec,
                  _full(w_main.shape), _full(w_kr.shape)],
        out_specs=[row(d), row(o3), row(QK_ROPE), pl.BlockSpec((8, d), lambda i: (0, 0))],
        out_shape=[jax.ShapeDtypeStruct((s, d), F32), jax.ShapeDtypeStruct((s, o3), BF16),
                   jax.ShapeDtypeStruct((s, QK_ROPE), BF16), jax.ShapeDtypeStruct((8, d), F32)],
        compiler_params=_params(("arbitrary",)),
    )(dxn, du, dcq, dckv, dkr, x, gn, sc, w_main, w_kr)


def _window_sum(a, w, rows, forward):
    s = a.shape[0]
    step = 1
    while step < w:
        if forward:
            shifted = jnp.where(rows < s - step, pltpu.roll(a, s - step, 0), 0.0)
        else:
            shifted = jnp.where(rows >= step, pltpu.roll(a, step, 0), 0.0)
        a = a + shifted
        step *= 2
    return a


def pool_fwd(u, pool_w, pool_scale):
    s = u.shape[0]

    def body(u_ref, w_ref, sc_ref, y_ref, diff_ref):
        rows = lax.broadcasted_iota(jnp.int32, (s, POOL_GC), 0)
        for g, w in enumerate(POOL_WINDOWS):
            cols = slice(g * POOL_GC, (g + 1) * POOL_GC)
            ug = u_ref[:, cols]
            cnt = jnp.minimum(rows + 1, w).astype(F32)
            diff = (_window_sum(ug, w, rows, False) / cnt - ug).astype(BF16)
            diff_ref[:, cols] = diff
            y_ref[:, cols] = _dot(diff, w_ref[g].astype(BF16)) * sc_ref[:, cols]

    return pl.pallas_call(
        body, name="pool_fwd",
        out_shape=[jax.ShapeDtypeStruct(u.shape, F32), jax.ShapeDtypeStruct(u.shape, BF16)],
        compiler_params=_params(),
    )(u, pool_w, pool_scale)


def pool_bwd(dy, diff, pool_w, pool_scale):
    s = dy.shape[0]

    def body(dy_ref, diff_ref, w_ref, sc_ref, du_ref, dw_ref, dsc_ref):
        rows = lax.broadcasted_iota(jnp.int32, (s, POOL_GC), 0)
        for g, w in enumerate(POOL_WINDOWS):
            cols = slice(g * POOL_GC, (g + 1) * POOL_GC)
            dyg = dy_ref[:, cols]
            diff = diff_ref[:, cols]
            wb = w_ref[g].astype(BF16)
            dsc_ref[:, cols] = _sum0(dyg * _dot(diff, wb))
            dys = (dyg * sc_ref[:, cols]).astype(BF16)
            dw_ref[g] = _dot_tn(diff, dys)
            ddiff = _dot_nt(dys, wb)
            cnt = jnp.minimum(rows + 1, w).astype(F32)
            du_ref[:, cols] = _window_sum(ddiff / cnt, w, rows, True) - ddiff

    return pl.pallas_call(
        body, name="pool_bwd",
        out_shape=[jax.ShapeDtypeStruct(dy.shape, F32), jax.ShapeDtypeStruct(pool_w.shape, F32),
                   jax.ShapeDtypeStruct(pool_scale.shape, F32)],
        compiler_params=_params(),
    )(dy, diff, pool_w, pool_scale)


def mla_qkv_fwd(cq, ckv, kr, qan, kvan, wqn, wqr, wkv, cos, sin, rot):
    s = cq.shape[0]
    tm = _row_tile(s)

    def body(cq_ref, ckv_ref, kr_ref, qan_ref, kvan_ref, wqn_ref, wqr_ref, wkv_ref, cos_ref, sin_ref, rot_ref,
             qn_ref, qr_ref, kn_ref, krr_ref, v_ref, ql_ref, kvl_ref):
        cos_t = cos_ref[...]
        sin_t = sin_ref[...]
        perm = rot_ref[...]

        def rope(t):
            return t * cos_t + _dot_exact(t, perm) * sin_t

        qhat, _ = _rms(cq_ref[...])
        ql = (qhat * qan_ref[...]).astype(BF16)
        ql_ref[...] = ql
        khat, _ = _rms(ckv_ref[...])
        kvl = (khat * kvan_ref[...]).astype(BF16)
        kvl_ref[...] = kvl
        krr_ref[...] = rope(kr_ref[...]).astype(BF16)
        for h in range(N_HEADS):
            qn_ref[h] = _dot(ql, wqn_ref[h]).astype(BF16)
            qr_ref[h] = rope(_dot(ql, wqr_ref[h])).astype(BF16)
            kv = _dot(kvl, wkv_ref[h])
            kn_ref[h] = kv[:, 0:QK_NOPE].astype(BF16)
            v_ref[h] = kv[:, QK_NOPE:].astype(BF16)

    row = lambda w: pl.BlockSpec((tm, w), lambda i: (i, 0))
    hrow = lambda w: pl.BlockSpec((N_HEADS, tm, w), lambda i: (0, i, 0))
    return pl.pallas_call(
        body, name="mla_qkv_fwd",
        grid=(s // tm,),
        in_specs=[row(Q_LORA), row(KV_LORA), row(QK_ROPE), _full(qan.shape), _full(kvan.shape),
                  _full(wqn.shape), _full(wqr.shape), _full(wkv.shape), row(QK_ROPE), row(QK_ROPE), _full(rot.shape)],
        out_specs=[hrow(QK_NOPE), hrow(QK_ROPE), hrow(QK_NOPE), row(QK_ROPE), hrow(V_HEAD), row(Q_LORA), row(KV_LORA)],
        out_shape=[jax.ShapeDtypeStruct((N_HEADS, s, QK_NOPE), BF16), jax.ShapeDtypeStruct((N_HEADS, s, QK_ROPE), BF16),
                   jax.ShapeDtypeStruct((N_HEADS, s, QK_NOPE), BF16), jax.ShapeDtypeStruct((s, QK_ROPE), BF16),
                   jax.ShapeDtypeStruct((N_HEADS, s, V_HEAD), BF16), jax.ShapeDtypeStruct((s, Q_LORA), BF16),
                   jax.ShapeDtypeStruct((s, KV_LORA), BF16)],
        compiler_params=_params(("arbitrary",)),
    )(cq, ckv, kr, qan, kvan, wqn, wqr, wkv, cos, sin, rot)


def _attn_probs(qn_ref, qr_ref, kn_ref, kr_ref, qi, tq):
    n = (qi + 1) * tq
    rows = slice(qi * tq, n)
    sc = (_dot_nt(qn_ref[rows, :], kn_ref[0:n, :]) + _dot_nt(qr_ref[rows, :], kr_ref[0:n, :])) * SOFTMAX_SCALE
    qpos = qi * tq + lax.broadcasted_iota(jnp.int32, (tq, n), 0)
    kpos = lax.broadcasted_iota(jnp.int32, (tq, n), 1)
    sc = jnp.where(qpos >= kpos, sc, -1e30)
    e = jnp.exp(sc - jnp.max(sc, axis=-1, keepdims=True))
    return e / jnp.sum(e, axis=-1, keepdims=True)


def attn_fwd(qn, qr, kn, krr, v):
    nh, s, _ = qn.shape
    tq = min(s, ATT_TILE)

    def body(qn_ref, qr_ref, kn_ref, kr_ref, v_ref, o_ref):
        for qi in range(s // tq):
            n = (qi + 1) * tq
            p = _attn_probs(qn_ref, qr_ref, kn_ref, kr_ref, qi, tq).astype(BF16)
            o_ref[qi * tq:n, :] = _dot(p, v_ref[0:n, :])

    head = lambda w: pl.BlockSpec((None, s, w), lambda h: (h, 0, 0))
    return pl.pallas_call(
        body, name="attn_fwd",
        grid=(nh,),
        in_specs=[head(QK_NOPE), head(QK_ROPE), head(QK_NOPE), _full(krr.shape), head(V_HEAD)],
        out_specs=pl.BlockSpec((s, V_HEAD), lambda h: (0, h)),
        out_shape=jax.ShapeDtypeStruct((s, nh * V_HEAD), F32),
        compiler_params=_params(("arbitrary",)),
    )(qn, qr, kn, krr, v)


def attn_bwd(qn, qr, kn, krr, v, do):
    nh, s, _ = qn.shape
    tq = min(s, ATT_TILE)

    def body(qn_ref, qr_ref, kn_ref, kr_ref, v_ref, do_ref, dqn_ref, dqr_ref, dkn_ref, dkr_ref, dv_ref):
        dkn_ref[...] = jnp.zeros_like(dkn_ref)
        dkr_ref[...] = jnp.zeros_like(dkr_ref)
        dv_ref[...] = jnp.zeros_like(dv_ref)
        for qi in range(s // tq):
            n = (qi + 1) * tq
            rows = slice(qi * tq, n)
            p = _attn_probs(qn_ref, qr_ref, kn_ref, kr_ref, qi, tq)
            dob = do_ref[rows, :].astype(BF16)
            dp = _dot_nt(dob, v_ref[0:n, :])
            ds = (p * (dp - jnp.sum(p * dp, axis=-1, keepdims=True)) * SOFTMAX_SCALE).astype(BF16)
            dqn_ref[rows, :] = _dot(ds, kn_ref[0:n, :])
            dqr_ref[rows, :] = _dot(ds, kr_ref[0:n, :])
            dkn_ref[0:n, :] += _dot_tn(ds, qn_ref[rows, :])
            dkr_ref[0:n, :] += _dot_tn(ds, qr_ref[rows, :])
            dv_ref[0:n, :] += _dot_tn(p.astype(BF16), dob)

    head = lambda w: pl.BlockSpec((None, s, w), lambda h: (h, 0, 0))
    return pl.pallas_call(
        body, name="attn_bwd",
        grid=(nh,),
        in_specs=[head(QK_NOPE), head(QK_ROPE), head(QK_NOPE), _full(krr.shape), head(V_HEAD),
                  pl.BlockSpec((s, V_HEAD), lambda h: (0, h))],
        out_specs=[head(QK_NOPE), head(QK_ROPE), head(QK_NOPE), head(QK_ROPE), head(V_HEAD)],
        out_shape=[jax.ShapeDtypeStruct((nh, s, QK_NOPE), F32), jax.ShapeDtypeStruct((nh, s, QK_ROPE), F32),
                   jax.ShapeDtypeStruct((nh, s, QK_NOPE), F32), jax.ShapeDtypeStruct((nh, s, QK_ROPE), F32),
                   jax.ShapeDtypeStruct((nh, s, V_HEAD), F32)],
        compiler_params=_params(("arbitrary",)),
    )(qn, qr, kn, krr, v, do)


def mla_qkv_bwd(dqn, dqr, dkn, dkr, dv, cq, ckv, qan, kvan, wqn, wqr, wkv, cos, sin, rot_t):
    s = cq.shape[0]
    tm = _row_tile(s)

    def body(dqn_ref, dqr_ref, dkn_ref, dkr_ref, dv_ref, cq_ref, ckv_ref, qan_ref, kvan_ref,
             wqn_ref, wqr_ref, wkv_ref, cos_ref, sin_ref, rot_ref,
             dcq_ref, dckv_ref, dkro_ref, gqn_ref, gqr_ref, gkv_ref, dqan_ref, dkvan_ref):
        i = pl.program_id(0)

        @pl.when(i == 0)
        def _():
            dqan_ref[...] = jnp.zeros_like(dqan_ref)
            dkvan_ref[...] = jnp.zeros_like(dkvan_ref)

        cos_t = cos_ref[...]
        sin_t = sin_ref[...]
        perm_t = rot_ref[...]

        def unrope(t):
            return t * cos_t + _dot_exact(t * sin_t, perm_t)

        acc_q = jnp.zeros((tm, Q_LORA), F32)
        acc_kv = jnp.zeros((tm, KV_LORA), F32)
        dkr_sum = jnp.zeros((tm, QK_ROPE), F32)
        for h in range(N_HEADS):
            a = dqn_ref[h].astype(BF16)
            b = unrope(dqr_ref[h]).astype(BF16)
            gqn_ref[h] = a
            gqr_ref[h] = b
            acc_q += _dot_nt(a, wqn_ref[h]) + _dot_nt(b, wqr_ref[h])
            dk = dkn_ref[h].astype(BF16)
            dvv = dv_ref[h].astype(BF16)
            gkv_ref[h, :, 0:QK_NOPE] = dk
            gkv_ref[h, :, QK_NOPE:] = dvv
            wkv_h = wkv_ref[h]
            acc_kv += _dot_nt(dk, wkv_h[:, 0:QK_NOPE]) + _dot_nt(dvv, wkv_h[:, QK_NOPE:])
            dkr_sum += dkr_ref[h]
        dkro_ref[...] = unrope(dkr_sum)
        dcq, dqan = _rms_bwd(acc_q, cq_ref[...], qan_ref[...])
        dcq_ref[...] = dcq
        dqan_ref[...] += dqan
        dckv, dkvan = _rms_bwd(acc_kv, ckv_ref[...], kvan_ref[...])
        dckv_ref[...] = dckv
        dkvan_ref[...] += dkvan

    row = lambda w: pl.BlockSpec((tm, w), lambda i: (i, 0))
    hrow = lambda w: pl.BlockSpec((N_HEADS, tm, w), lambda i: (0, i, 0))
    return pl.pallas_call(
        body, name="mla_qkv_bwd",
        grid=(s // tm,),
        in_specs=[hrow(QK_NOPE), hrow(QK_ROPE), hrow(QK_NOPE), hrow(QK_ROPE), hrow(V_HEAD),
                  row(Q_LORA), row(KV_LORA), _full(qan.shape), _full(kvan.shape),
                  _full(wqn.shape), _full(wqr.shape), _full(wkv.shape), row(QK_ROPE), row(QK_ROPE), _full(rot_t.shape)],
        out_specs=[row(Q_LORA), row(KV_LORA), row(QK_ROPE), hrow(QK_NOPE), hrow(QK_ROPE), hrow(QK_NOPE + V_HEAD),
                   _full(qan.shape), _full(kvan.shape)],
        out_shape=[jax.ShapeDtypeStruct((s, Q_LORA), F32), jax.ShapeDtypeStruct((s, KV_LORA), F32),
                   jax.ShapeDtypeStruct((s, QK_ROPE), F32),
                   jax.ShapeDtypeStruct((N_HEADS, s, QK_NOPE), BF16), jax.ShapeDtypeStruct((N_HEADS, s, QK_ROPE), BF16),
                   jax.ShapeDtypeStruct((N_HEADS, s, QK_NOPE + V_HEAD), BF16),
                   jax.ShapeDtypeStruct(qan.shape, F32), jax.ShapeDtypeStruct(kvan.shape, F32)],
        compiler_params=_params(("arbitrary",)),
    )(dqn, dqr, dkn, dkr, dv, cq, ckv, qan, kvan, wqn, wqr, wkv, cos, sin, rot_t)


def out_proj_fwd(yp, om, w_out, x, gt):
    s, d = x.shape
    n_sh, rs, _ = w_out.shape
    tm = _row_tile(s)
    per = POOL_WIDTH // rs

    def body(yp_ref, om_ref, w_ref, x_ref, gt_ref, xo_ref, ycat_ref, y_ref):
        y = jnp.zeros((tm, d), F32)
        for j in range(n_sh):
            src = yp_ref if j < per else om_ref
            part = src[:, (j % per) * rs:(j % per + 1) * rs].astype(BF16)
            ycat_ref[j] = part
            y += _dot(part, w_ref[j])
        y_ref[...] = y.astype(BF16)
        xo_ref[...] = x_ref[...] + gt_ref[...] * y

    row = lambda w: pl.BlockSpec((tm, w), lambda i: (i, 0))
    return pl.pallas_call(
        body, name="out_proj_fwd",
        grid=(s // tm,),
        in_specs=[row(POOL_WIDTH), row(POOL_WIDTH), _full(w_out.shape), row(d), pl.BlockSpec((1, d), lambda i: (0, 0))],
        out_specs=[row(d), pl.BlockSpec((n_sh, tm, rs), lambda i: (0, i, 0)), row(d)],
        out_shape=[jax.ShapeDtypeStruct((s, d), F32), jax.ShapeDtypeStruct((n_sh, s, rs), BF16),
                   jax.ShapeDtypeStruct((s, d), BF16)],
        compiler_params=_params(("arbitrary",)),
    )(yp, om, w_out, x, gt)


def out_proj_bwd(dxn, y, gt, w_out):
    s, d = dxn.shape
    n_sh, rs, _ = w_out.shape
    tm = _row_tile(s)
    per = POOL_WIDTH // rs

    def body(dxn_ref, y_ref, gt_ref, w_ref, dy_ref, dyp_ref, dom_ref, dgt_ref):
        i = pl.program_id(0)

        @pl.when(i == 0)
        def _():
            dgt_ref[...] = jnp.zeros_like(dgt_ref)

        dxn_t = dxn_ref[...]
        dy = (gt_ref[...] * dxn_t).astype(BF16)
        dy_ref[...] = dy
        dgt_ref[...] += _sum0(dxn_t * y_ref[...].astype(F32))
        for j in range(n_sh):
            dst = dyp_ref if j < per else dom_ref
            dst[:, (j % per) * rs:(j % per + 1) * rs] = _dot_nt(dy, w_ref[j])

    row = lambda w: pl.BlockSpec((tm, w), lambda i: (i, 0))
    vec = pl.BlockSpec((1, d), lambda i: (0, 0))
    return pl.pallas_call(
        body, name="out_proj_bwd",
        grid=(s // tm,),
        in_specs=[row(d), row(d), vec, _full(w_out.shape)],
        out_specs=[row(d), row(POOL_WIDTH), row(POOL_WIDTH), vec],
        out_shape=[jax.ShapeDtypeStruct((s, d), BF16), jax.ShapeDtypeStruct((s, POOL_WIDTH), F32),
                   jax.ShapeDtypeStruct((s, POOL_WIDTH), F32), jax.ShapeDtypeStruct((1, d), F32)],
        compiler_params=_params(("arbitrary",)),
    )(dxn, y, gt, w_out)


def final_loss(x, gn, tgt):
    s, d = x.shape
    tm = _row_tile(s)

    def body(x_ref, gn_ref, t_ref, loss_ref, dx_ref, dgn_ref):
        i = pl.program_id(0)

        @pl.when(i == 0)
        def _():
            loss_ref[...] = jnp.zeros_like(loss_ref)
            dgn_ref[...] = jnp.zeros_like(dgn_ref)

        xt = x_ref[...]
        g = gn_ref[...]
        xhat, _ = _rms(xt)
        err = xhat * g - t_ref[...]
        per_tok = jnp.mean(err * err, axis=-1, keepdims=True)
        loss_ref[...] += jnp.broadcast_to(0.5 * _sum0(per_tok), loss_ref.shape)
        dx, dgn = _rms_bwd(err * (1.0 / d), xt, g)
        dx_ref[...] = dx
        dgn_ref[...] += dgn

    row = pl.BlockSpec((tm, d), lambda i: (i, 0))
    vec = pl.BlockSpec((1, d), lambda i: (0, 0))
    return pl.pallas_call(
        body, name="final_loss",
        grid=(s // tm,),
        in_specs=[row, vec, row],
        out_specs=[pl.BlockSpec((1, 128), lambda i: (0, 0)), row, vec],
        out_shape=[jax.ShapeDtypeStruct((1, 128), F32), jax.ShapeDtypeStruct((s, d), F32),
                   jax.ShapeDtypeStruct((1, d), F32)],
        compiler_params=_params(("arbitrary",)),
    )(x, gn, tgt)


def _col_tile(cols):
    return 768 if cols % 768 == 0 else cols


def ada_fwd(c16, ada_w, ada_b_loc):
    n_layers, d, cols = ada_w.shape
    tn = _col_tile(cols)

    def body(c_ref, w_ref, b_ref, o_ref):
        cv = c_ref[...]
        ca = (cv * jax.nn.sigmoid(cv)).astype(BF16)
        o_ref[...] = _dot(ca, w_ref[...].astype(BF16)) + b_ref[...]

    return pl.pallas_call(
        body, name="ada_fwd",
        grid=(n_layers, cols // tn),
        in_specs=[pl.BlockSpec((16, d), lambda l, j: (0, 0)), pl.BlockSpec((None, d, tn), lambda l, j: (l, 0, j)),
                  pl.BlockSpec((None, 1, tn), lambda l, j: (l, 0, j))],
        out_specs=pl.BlockSpec((None, 16, tn), lambda l, j: (l, 0, j)),
        out_shape=jax.ShapeDtypeStruct((n_layers, 16, cols), F32),
        compiler_params=_params(("arbitrary", "arbitrary")),
    )(c16, ada_w, ada_b_loc)


def ada_bwd(c16, dmod16):
    n_layers, _, cols = dmod16.shape
    d = c16.shape[1]
    tn = _col_tile(cols)

    def body(c_ref, g_ref, o_ref):
        cv = c_ref[...]
        ca = (cv * jax.nn.sigmoid(cv)).astype(BF16)
        o_ref[...] = _dot_tn(ca, g_ref[...].astype(BF16))

    return pl.pallas_call(
        body, name="ada_bwd",
        grid=(n_layers, cols // tn),
        in_specs=[pl.BlockSpec((16, d), lambda l, j: (0, 0)), pl.BlockSpec((None, 16, tn), lambda l, j: (l, 0, j))],
        out_specs=pl.BlockSpec((None, d, tn), lambda l, j: (l, 0, j)),
        out_shape=jax.ShapeDtypeStruct((n_layers, d, cols), F32),
        compiler_params=_params(("arbitrary", "arbitrary")),
    )(c16, dmod16)


def _as_rows(a):
    if a.ndim == 1:
        return a.reshape(1, a.shape[0])
    return a.reshape(-1, a.shape[-1])


def _rows_tile(r, c, itemsize=4, budget=2 * 1024 * 1024):
    if r * c * itemsize <= budget:
        return r
    best = None
    t = 16
    while t < r:
        if r % t == 0 and t * c * itemsize <= budget:
            best = t
        t += 16
    return best if best is not None else r


def cast_bf16(w):
    w2 = _as_rows(w)
    r, c = w2.shape
    tr = _rows_tile(r, c)

    def body(w_ref, o_ref):
        o_ref[...] = w_ref[...].astype(BF16)

    spec = pl.BlockSpec((tr, c), lambda i: (i, 0))
    out = pl.pallas_call(
        body, name="cast_bf16", grid=(r // tr,), in_specs=[spec], out_specs=spec,
        out_shape=jax.ShapeDtypeStruct((r, c), BF16), compiler_params=_params(("arbitrary",)),
    )(w2)
    return out.reshape(w.shape)


def adamw(w, g, m, v):
    shape = w.shape
    w2, g2, m2, v2 = (_as_rows(t) for t in (w, g, m, v))
    r, c = w2.shape
    tr = _rows_tile(r, c, budget=1024 * 1024)
    c1 = 1.0 - ADAM_B1 ** ADAM_STEP
    c2 = 1.0 - ADAM_B2 ** ADAM_STEP

    def body(w_ref, g_ref, m_ref, v_ref, d_ref, mo_ref, vo_ref):
        gv = g_ref[...]
        mn = ADAM_B1 * m_ref[...] + (1.0 - ADAM_B1) * gv
        vn = ADAM_B2 * v_ref[...] + (1.0 - ADAM_B2) * (gv * gv)
        mo_ref[...] = mn
        vo_ref[...] = vn
        d_ref[...] = -ADAM_LR * ((mn / c1) / (jnp.sqrt(vn / c2) + ADAM_EPS) + ADAM_WD * w_ref[...])

    spec = pl.BlockSpec((tr, c), lambda i: (i, 0))
    outs = pl.pallas_call(
        body, name="adamw", grid=(r // tr,), in_specs=[spec] * 4, out_specs=[spec] * 3,
        out_shape=[jax.ShapeDtypeStruct((r, c), F32)] * 3, compiler_params=_params(("arbitrary",)),
    )(w2, g2, m2, v2)
    return tuple(o.reshape(shape) for o in outs)


def sum_devices(a):
    n, r, c = a.shape
    tr = _rows_tile(r, c, budget=512 * 1024)

    def body(a_ref, o_ref):
        acc = a_ref[0]
        for j in range(1, n):
            acc = acc + a_ref[j]
        o_ref[...] = acc

    return pl.pallas_call(
        body, name="sum_devices", grid=(r // tr,),
        in_specs=[pl.BlockSpec((n, tr, c), lambda i: (0, i, 0))], out_specs=pl.BlockSpec((tr, c), lambda i: (i, 0)),
        out_shape=jax.ShapeDtypeStruct((r, c), F32), compiler_params=_params(("arbitrary",)),
    )(a)


def pair_add(g, ra, half):
    n_layers, n_sl, r, c = g.shape
    hr = r // 2

    def body(h_ref, g_ref, ra_ref, p_ref, pb_ref):
        p = g_ref[...] + ra_ref[...]
        p_ref[...] = p
        pb_ref[...] = p.astype(BF16)

    blk = lambda imap: pl.BlockSpec((None, None, hr, c), imap)
    mine = blk(lambda l, k, h: (l, k, 0, 0))
    return pl.pallas_call(
        body, name="pair_add",
        grid_spec=pltpu.PrefetchScalarGridSpec(
            num_scalar_prefetch=1, grid=(n_layers, n_sl),
            in_specs=[blk(lambda l, k, h: (l, k, h[0], 0)), mine], out_specs=[mine, mine]),
        out_shape=[jax.ShapeDtypeStruct((n_layers, n_sl, hr, c), F32), jax.ShapeDtypeStruct((n_layers, n_sl, hr, c), BF16)],
        compiler_params=_params(("arbitrary", "arbitrary")),
    )(half, g, ra)


def chip_sum(p32, rb, half, chip):
    n_layers, _, hr, c = p32.shape

    def body(s_ref, p_ref, rb_ref, o_ref):
        acc = p_ref[...]
        for j in range(3):
            acc = acc + rb_ref[j].astype(F32)
        o_ref[...] = acc

    return pl.pallas_call(
        body, name="chip_sum",
        grid_spec=pltpu.PrefetchScalarGridSpec(
            num_scalar_prefetch=1, grid=(n_layers,),
            in_specs=[pl.BlockSpec((None, None, hr, c), lambda l, sr: (l, sr[1], 0, 0)),
                      pl.BlockSpec((3, None, hr, c), lambda l, sr: (0, l, 0, 0))],
            out_specs=pl.BlockSpec((None, hr, c), lambda l, sr: (l, sr[0], 0))),
        out_shape=jax.ShapeDtypeStruct((n_layers, 2 * hr, c), F32),
        compiler_params=_params(("arbitrary",)),
    )(jnp.concatenate([half, chip]), p32, rb)


def _me():
    return lax.axis_index("x"), lax.axis_index("y"), lax.axis_index("c")


def _flip(v, bit):
    return 1 - v if bit else v


def exchange8(xs, bcast):
    blk = xs.shape if bcast else xs.shape[1:]

    def body(x_ref, o_ref, send_sems, recv_sems, loc_sem):
        mx, my, mc = _me()
        me = 4 * mx + 2 * my + mc
        src = (lambda j: x_ref) if bcast else (lambda j: x_ref.at[j])
        loc = pltpu.make_async_copy(src(me), o_ref.at[me], loc_sem)
        loc.start()
        copies = []
        for o in range(1, N_DEV):
            px, py, pc = _flip(mx, o & 4), _flip(my, o & 2), _flip(mc, o & 1)
            cp = pltpu.make_async_remote_copy(
                src_ref=src(4 * px + 2 * py + pc), dst_ref=o_ref.at[me],
                send_sem=send_sems.at[o - 1], recv_sem=recv_sems.at[o - 1],
                device_id=(px, py, pc), device_id_type=MESH)
            cp.start()
            copies.append(cp)
        for cp in copies:
            cp.wait()
        loc.wait()

    return pl.pallas_call(
        body, name="exchange8_gather" if bcast else "exchange8_a2a",
        in_specs=[pl.BlockSpec(memory_space=pltpu.VMEM)], out_specs=pl.BlockSpec(memory_space=pltpu.VMEM),
        out_shape=jax.ShapeDtypeStruct((N_DEV,) + tuple(blk), xs.dtype),
        scratch_shapes=[pltpu.SemaphoreType.DMA((N_DEV - 1,)), pltpu.SemaphoreType.DMA((N_DEV - 1,)), pltpu.SemaphoreType.DMA],
        compiler_params=_params(),
    )(xs)


def weight_gather(shards):
    n = len(shards)
    n_layers = shards[0].shape[0]
    hl = n_layers // 2

    def body(*refs):
        src = refs[:n]
        dst = refs[n:2 * n]
        ici_send, ici_recv, d2d_send, d2d_recv, loc_sem = refs[2 * n:]
        mx, my, mc = _me()
        chip = 2 * mx + my
        mine = pl.ds(mc * hl, hl)
        locs = []
        for t in range(n):
            cp = pltpu.make_async_copy(src[t], dst[t].at[chip], loc_sem.at[t])
            cp.start()
            locs.append(cp)
        sends = []
        for t in range(n):
            for o in range(1, N_CHIPS):
                px, py = _flip(mx, o & 2), _flip(my, o & 1)
                cp = pltpu.make_async_remote_copy(
                    src_ref=src[t].at[mine], dst_ref=dst[t].at[chip, mine],
                    send_sem=ici_send.at[t, o - 1], recv_sem=ici_recv.at[t, o - 1],
                    device_id=(px, py, mc), device_id_type=MESH)
                cp.start()
                sends.append(cp)
        fwds = []
        for t in range(n):
            for o in range(1, N_CHIPS):
                px, py = _flip(mx, o & 2), _flip(my, o & 1)
                got = dst[t].at[2 * px + py, mine]
                pltpu.make_async_remote_copy(
                    src_ref=got, dst_ref=got, send_sem=ici_send.at[t, o - 1], recv_sem=ici_recv.at[t, o - 1],
                    device_id=(px, py, mc), device_id_type=MESH).wait_recv()
                cp = pltpu.make_async_remote_copy(
                    src_ref=got, dst_ref=got, send_sem=d2d_send.at[t, o - 1], recv_sem=d2d_recv.at[t, o - 1],
                    device_id=(mx, my, 1 - mc), device_id_type=MESH)
                cp.start()
                fwds.append(cp)
        for cp in fwds:
            cp.wait()
        for cp in sends:
            cp.wait_send()
        for cp in locs:
            cp.wait()

    any_spec = pl.BlockSpec(memory_space=pl.ANY)
    return pl.pallas_call(
        body, name="weight_gather",
        in_specs=[any_spec] * n, out_specs=[any_spec] * n,
        out_shape=[jax.ShapeDtypeStruct((N_CHIPS,) + t.shape, t.dtype) for t in shards],
        scratch_shapes=[pltpu.SemaphoreType.DMA((n, N_CHIPS - 1))] * 4 + [pltpu.SemaphoreType.DMA((n,))],
        compiler_params=_params(),
    )(*shards)


def pair_send_halves(gs):
    n = len(gs)

    def body(*refs):
        src = refs[:n]
        dst = refs[n:2 * n]
        send_sems, recv_sems = refs[2 * n:]
        mx, my, mc = _me()
        copies = []
        for t in range(n):
            hr = src[t].shape[2] // 2
            cp = pltpu.make_async_remote_copy(
                src_ref=src[t].at[:, :, pl.ds((1 - mc) * hr, hr), :], dst_ref=dst[t],
                send_sem=send_sems.at[t], recv_sem=recv_sems.at[t],
                device_id=(mx, my, 1 - mc), device_id_type=MESH)
            cp.start()
            copies.append(cp)
        for cp in copies:
            cp.wait()

    any_spec = pl.BlockSpec(memory_space=pl.ANY)
    return pl.pallas_call(
        body, name="pair_send_halves",
        in_specs=[any_spec] * n, out_specs=[any_spec] * n,
        out_shape=[jax.ShapeDtypeStruct(g.shape[:2] + (g.shape[2] // 2, g.shape[3]), g.dtype) for g in gs],
        scratch_shapes=[pltpu.SemaphoreType.DMA((n,)), pltpu.SemaphoreType.DMA((n,))],
        compiler_params=_params(),
    )(*gs)


def chip_scatter(pbs):
    n = len(pbs)

    def body(*refs):
        src = refs[:n]
        dst = refs[n:2 * n]
        send_sems, recv_sems = refs[2 * n:]
        mx, my, mc = _me()
        copies = []
        for t in range(n):
            for o in range(1, N_CHIPS):
                px, py = _flip(mx, o & 2), _flip(my, o & 1)
                cp = pltpu.make_async_remote_copy(
                    src_ref=src[t].at[:, 2 * px + py], dst_ref=dst[t].at[o - 1],
                    send_sem=send_sems.at[t, o - 1], recv_sem=recv_sems.at[t, o - 1],
                    device_id=(px, py, mc), device_id_type=MESH)
                cp.start()
                copies.append(cp)
        for cp in copies:
            cp.wait()

    any_spec = pl.BlockSpec(memory_space=pl.ANY)
    return pl.pallas_call(
        body, name="chip_scatter",
        in_specs=[any_spec] * n, out_specs=[any_spec] * n,
        out_shape=[jax.ShapeDtypeStruct((N_CHIPS - 1, p.shape[0]) + p.shape[2:], p.dtype) for p in pbs],
        scratch_shapes=[pltpu.SemaphoreType.DMA((n, N_CHIPS - 1)), pltpu.SemaphoreType.DMA((n, N_CHIPS - 1))],
        compiler_params=_params(),
    )(*pbs)


def pair_fill_halves(fs):
    n = len(fs)

    def body(*refs):
        dst = refs[n:2 * n]
        send_sems, recv_sems = refs[2 * n:]
        mx, my, mc = _me()
        copies = []
        for t in range(n):
            hr = dst[t].shape[1] // 2
            rows = dst[t].at[:, pl.ds(mc * hr, hr), :]
            theirs = dst[t].at[:, pl.ds((1 - mc) * hr, hr), :]
            cp = pltpu.make_async_remote_copy(
                src_ref=rows, dst_ref=rows, send_sem=send_sems.at[t], recv_sem=recv_sems.at[t],
                device_id=(mx, my, 1 - mc), device_id_type=MESH)
            cp.start()
            copies.append((cp, pltpu.make_async_remote_copy(
                src_ref=theirs, dst_ref=theirs, send_sem=send_sems.at[t], recv_sem=recv_sems.at[t],
                device_id=(mx, my, 1 - mc), device_id_type=MESH)))
        for cp, arrival in copies:
            cp.wait_send()
            arrival.wait_recv()

    any_spec = pl.BlockSpec(memory_space=pl.ANY)
    return pl.pallas_call(
        body, name="pair_fill_halves",
        in_specs=[any_spec] * n, out_specs=[any_spec] * n,
        out_shape=[jax.ShapeDtypeStruct(f.shape, f.dtype) for f in fs],
        input_output_aliases={t: t for t in range(n)},
        scratch_shapes=[pltpu.SemaphoreType.DMA((n,)), pltpu.SemaphoreType.DMA((n,))],
        compiler_params=_params(),
    )(*fs)


def _pack_rows(parts, d):
    rows, spans = [], []
    at = 0
    for p in parts:
        flat = p.reshape(-1)
        n_rows = -(-flat.shape[0] // (8 * d)) * 8
        flat = jnp.pad(flat, (0, n_rows * d - flat.shape[0]))
        rows.append(flat.reshape(n_rows, d))
        spans.append((at, p.shape))
        at += n_rows
    return jnp.concatenate(rows, axis=0), spans


def _unpack_rows(packed, spans):
    out = []
    for at, shape in spans:
        n = math.prod(shape)
        d = packed.shape[1]
        n_rows = -(-n // d)
        out.append(packed[at:at + n_rows].reshape(-1)[:n].reshape(shape))
    return out


def _rotate_half_matrix():
    half = QK_ROPE // 2
    idx = jnp.arange(QK_ROPE)
    src = jnp.where(idx < half, idx + half, idx - half)
    sign = jnp.where(idx < half, -1.0, 1.0)
    return (jnp.zeros((QK_ROPE, QK_ROPE), F32).at[src, idx].set(sign)).astype(BF16)


def kernel(x, c, positions, ada_w, ada_b, ffn1_norm, ffn1_w_gate, ffn1_w_up, ffn1_w_down, mix_norm, w_in, pool_w, pool_scale, q_a_norm, w_q_b, kv_a_norm, w_kv_b, w_out, ffn2_norm, ffn2_w_gate, ffn2_w_up, ffn2_w_down, final_norm, loss_target, m_ada_w, m_ada_b, m_ffn1_norm, m_ffn1_w_gate, m_ffn1_w_up, m_ffn1_w_down, m_mix_norm, m_w_in, m_pool_w, m_pool_scale, m_q_a_norm, m_w_q_b, m_kv_a_norm, m_w_kv_b, m_w_out, m_ffn2_norm, m_ffn2_w_gate, m_ffn2_w_up, m_ffn2_w_down, m_final_norm, v_ada_w, v_ada_b, v_ffn1_norm, v_ffn1_w_gate, v_ffn1_w_up, v_ffn1_w_down, v_mix_norm, v_w_in, v_pool_w, v_pool_scale, v_q_a_norm, v_w_q_b, v_kv_a_norm, v_w_kv_b, v_w_out, v_ffn2_norm, v_ffn2_w_gate, v_ffn2_w_up, v_ffn2_w_down, v_final_norm):
    mx, my, mc = _me()
    chip = 2 * mx + my
    half = jnp.reshape(mc, (1,)).astype(jnp.int32)
    chip1 = jnp.reshape(chip, (1,)).astype(jnp.int32)
    n_layers, d, ada_cols = ada_w.shape
    s = x.shape[1]
    xt = x[0]
    tgt = loss_target[0]

    inv_freq = 1.0 / (ROPE_THETA ** (jnp.arange(0, QK_ROPE, 2, dtype=F32) / QK_ROPE))
    ang = positions[0].astype(F32)[:, None] * inv_freq
    ang = jnp.concatenate([ang, ang], axis=-1)
    cos, sin = jnp.cos(ang), jnp.sin(ang)
    rot = _rotate_half_matrix()
    rot_t = rot.T

    c_all = exchange8(c, True).reshape(N_DEV, d)
    c16 = jnp.pad(c_all, ((0, 8), (0, 0)))
    ada_b_loc = lax.dynamic_slice_in_dim(ada_b, chip * ada_cols, ada_cols, axis=1).reshape(n_layers, 1, ada_cols)
    mod_part = ada_fwd(c16, ada_w, ada_b_loc)[:, :N_DEV]
    mod_got = exchange8(jnp.transpose(mod_part, (1, 0, 2)), False)
    mod = jnp.transpose(mod_got.reshape(N_CHIPS, 2, n_layers, ada_cols)[:, 0], (1, 0, 2))
    mod = mod.reshape(n_layers, 9, 1, d)

    local = [ffn1_w_gate, ffn1_w_up, ffn1_w_down, w_in, w_q_b, w_kv_b, w_out, ffn2_w_gate, ffn2_w_up, ffn2_w_down]
    g1w, u1w, d1w, winw, wqw, wkvw, woutw, g2w, u2w, d2w = weight_gather([cast_bf16(w) for w in local])
    in_sh = winw.shape[-1]
    main_cols = POOL_WIDTH + Q_LORA + KV_LORA

    def layer_weights(l):
        w_in_full = jnp.transpose(winw[:, l], (1, 0, 2)).reshape(d, N_CHIPS * in_sh)
        wq = wqw[:, l]
        return dict(
            g1=g1w[:, l], u1=u1w[:, l], d1=d1w[:, l], g2=g2w[:, l], u2=u2w[:, l], d2=d2w[:, l],
            w_main=w_in_full[:, :main_cols], w_kr=w_in_full[:, main_cols:],
            wqn=wq[:, :, :QK_NOPE], wqr=wq[:, :, QK_NOPE:], wkv=wkvw[:, l], wout=woutw[:, l])

    row = lambda a, l: a[l].reshape(1, -1)
    saved = []
    for l in range(n_layers):
        w = layer_weights(l)
        sv = dict(w=w, x0=xt)
        xt, sv["h1"], sv["gate1"], sv["up1"], sv["y1"] = ffn_fwd(
            xt, row(ffn1_norm, l), mod[l, 0], mod[l, 1], mod[l, 2], w["g1"], w["u1"], w["d1"])
        sv["x1"] = xt
        sv["h2"], u, cq, ckv, kr = mix_in_fwd(xt, row(mix_norm, l), mod[l, 3], mod[l, 4], w["w_main"], w["w_kr"])
        sv["cq"], sv["ckv"] = cq, ckv
        yp, sv["diff"] = pool_fwd(u, pool_w[l], row(pool_scale, l))
        qn, qr, kn, krr, vv, sv["ql"], sv["kvl"] = mla_qkv_fwd(
            cq, ckv, kr, row(q_a_norm, l), row(kv_a_norm, l), w["wqn"], w["wqr"], w["wkv"], cos, sin, rot)
        sv["qkv"] = (qn, qr, kn, krr, vv)
        om = attn_fwd(qn, qr, kn, krr, vv)
        xt, sv["ycat"], sv["y2"] = out_proj_fwd(yp, om, w["wout"], xt, mod[l, 5])
        sv["x2"] = xt
        xt, sv["h3"], sv["gate3"], sv["up3"], sv["y3"] = ffn_fwd(
            xt, row(ffn2_norm, l), mod[l, 6], mod[l, 7], mod[l, 8], w["g2"], w["u2"], w["d2"])
        saved.append(sv)

    loss_vec, dx, d_final_norm = final_loss(xt, final_norm.reshape(1, d), tgt)
    loss = lax.psum(loss_vec[0, 0], ("x", "y", "c"))

    big = {}

    def put(name, a, b, l):
        big[name] = tn_mm(a, b, l, n_layers, big.get(name))

    dmods, dnorm1, dnorm2, dnorm3 = [None] * n_layers, [None] * n_layers, [None] * n_layers, [None] * n_layers
    dpw, dps, dqan_l, dkvan_l = [None] * n_layers, [None] * n_layers, [None] * n_layers, [None] * n_layers
    dwin_l, dwq_l = [None] * n_layers, [None] * n_layers
    for l in reversed(range(n_layers)):
        sv = saved[l]
        w = sv["w"]
        dx, dy, a, dgt, dup, dvec3 = ffn_bwd(dx, sv["x2"], sv["y3"], sv["gate3"], sv["up3"],
                                             row(ffn2_norm, l), mod[l, 7], mod[l, 8], w["g2"], w["u2"], w["d2"])
        put("g2", sv["h3"][None], dgt, l)
        put("u2", sv["h3"][None], dup, l)
        put("d2", a, dy[None], l)
        dy2, dyp, dom, dg2 = out_proj_bwd(dx, sv["y2"], mod[l, 5], w["wout"])
        put("wout", sv["ycat"], dy2[None], l)
        qn, qr, kn, krr, vv = sv["qkv"]
        dqn, dqr, dkn, dkr, dvv = attn_bwd(qn, qr, kn, krr, vv, dom)
        dcq, dckv, dkr_in, gqn, gqr, gkv, dqan_l[l], dkvan_l[l] = mla_qkv_bwd(
            dqn, dqr, dkn, dkr, dvv, sv["cq"], sv["ckv"], row(q_a_norm, l), row(kv_a_norm, l),
            w["wqn"], w["wqr"], w["wkv"], cos, sin, rot_t)
        put("wqn", sv["ql"][None], gqn, l)
        put("wqr", sv["ql"][None], gqr, l)
        put("wkv", sv["kvl"][None], gkv, l)
        du, dpw[l], dps[l] = pool_bwd(dyp, sv["diff"], pool_w[l], row(pool_scale, l))
        dx, dz, dzkr, dvec2 = mix_in_bwd(dx, du, dcq, dckv, dkr_in, sv["x1"], row(mix_norm, l), mod[l, 4],
                                         w["w_main"], w["w_kr"])
        put("win_main", sv["h2"][None], dz[None], l)
        put("win_kr", sv["h2"][None], dzkr[None], l)
        dx, dy, a, dgt, dup, dvec1 = ffn_bwd(dx, sv["x0"], sv["y1"], sv["gate1"], sv["up1"],
                                             row(ffn1_norm, l), mod[l, 1], mod[l, 2], w["g1"], w["u1"], w["d1"])
        put("g1", sv["h1"][None], dgt, l)
        put("u1", sv["h1"][None], dup, l)
        put("d1", a, dy[None], l)
        dmods[l] = jnp.concatenate([dvec1[0:3], dvec2[0:2], dg2, dvec3[0:3]], axis=0)
        dnorm1[l], dnorm2[l], dnorm3[l] = dvec1[3], dvec2[3], dvec3[3]

    win_full = jnp.concatenate([big["win_main"][:, 0], big["win_kr"][:, 0]], axis=-1)
    g_win = jnp.transpose(win_full.reshape(n_layers, d, N_CHIPS, in_sh), (0, 2, 1, 3))
    g_wq = jnp.concatenate([big["wqn"], big["wqr"]], axis=-1)
    full = [big["g1"], big["u1"], big["d1"], g_win, g_wq, big["wkv"], big["wout"], big["g2"], big["u2"], big["d2"]]

    got = pair_send_halves(full)
    sums = [pair_add(g, ra, half) for g, ra in zip(full, got)]
    parts = chip_scatter([pb for _, pb in sums])
    halves = [chip_sum(p32, rb, half, chip1) for (p32, _), rb in zip(sums, parts)]
    g_local = pair_fill_halves(halves)

    small_parts = [jnp.stack(dmods), jnp.stack(dnorm1), jnp.stack(dnorm2), jnp.stack(dnorm3), d_final_norm,
                   jnp.stack(dps), jnp.stack(dqan_l), jnp.stack(dkvan_l), jnp.stack(dpw)]
    packed, spans = _pack_rows(small_parts, d)
    gathered = exchange8(packed, True)
    total = sum_devices(gathered)
    (g_ada_b, g_n1, g_n2, g_n3, g_fn, g_ps, g_qan, g_kvan, g_pw) = _unpack_rows(total, spans)
    g_ada_b = g_ada_b.reshape(n_layers, 9 * d)
    dmod_all = gathered[:, :9 * n_layers].reshape(N_DEV, n_layers, 9 * d)
    dmod_loc = lax.dynamic_slice_in_dim(dmod_all, chip * ada_cols, ada_cols, axis=2)
    dmod16 = jnp.pad(jnp.transpose(dmod_loc, (1, 0, 2)), ((0, 0), (0, 8), (0, 0)))
    g_ada_w = ada_bwd(c16, dmod16)

    grads = [g_ada_w, g_ada_b, g_n1, g_local[0], g_local[1], g_local[2], g_n2, g_local[3],
             g_pw.reshape(pool_w.shape), g_ps.reshape(pool_scale.shape), g_qan.reshape(q_a_norm.shape), g_local[4],
             g_kvan.reshape(kv_a_norm.shape), g_local[5], g_local[6], g_n3, g_local[7], g_local[8], g_local[9],
             g_fn.reshape(final_norm.shape)]
    weights = [ada_w, ada_b, ffn1_norm, ffn1_w_gate, ffn1_w_up, ffn1_w_down, mix_norm, w_in, pool_w, pool_scale,
               q_a_norm, w_q_b, kv_a_norm, w_kv_b, w_out, ffn2_norm, ffn2_w_gate, ffn2_w_up, ffn2_w_down, final_norm]
    ms = [m_ada_w, m_ada_b, m_ffn1_norm, m_ffn1_w_gate, m_ffn1_w_up, m_ffn1_w_down, m_mix_norm, m_w_in, m_pool_w,
          m_pool_scale, m_q_a_norm, m_w_q_b, m_kv_a_norm, m_w_kv_b, m_w_out, m_ffn2_norm, m_ffn2_w_gate, m_ffn2_w_up,
          m_ffn2_w_down, m_final_norm]
    vs = [v_ada_w, v_ada_b, v_ffn1_norm, v_ffn1_w_gate, v_ffn1_w_up, v_ffn1_w_down, v_mix_norm, v_w_in, v_pool_w,
          v_pool_scale, v_q_a_norm, v_w_q_b, v_kv_a_norm, v_w_kv_b, v_w_out, v_ffn2_norm, v_ffn2_w_gate, v_ffn2_w_up,
          v_ffn2_w_down, v_final_norm]
    grads = [g.reshape(w.shape) for g, w in zip(grads, weights)]
    steps = [adamw(w, g, m, v) for w, g, m, v in zip(weights, grads, ms, vs)]
    return (loss, dx.reshape(x.shape), *grads, *[t[0] for t in steps], *[t[1] for t in steps], *[t[2] for t in steps])
```

```python
import math

import jax
import jax.numpy as jnp
from jax import lax
from jax.experimental import pallas as pl
from jax.experimental.pallas import tpu as pltpu

F32 = jnp.float32
BF16 = jnp.bfloat16
MESH = pl.DeviceIdType.MESH

EPS = 1e-6
ROPE_THETA = 10000.0
N_HEADS = 4
QK_NOPE = 128
QK_ROPE = 64
V_HEAD = 128
POOL_WINDOWS = (2, 4, 8, 16)
POOL_GC = 128
POOL_WIDTH = POOL_GC * len(POOL_WINDOWS)
Q_LORA = 384
KV_LORA = 256
SOFTMAX_SCALE = 1.0 / math.sqrt(QK_NOPE + QK_ROPE)
N_CHIPS = 4
N_DEV = 8

ADAM_LR = 0.001
ADAM_B1 = 0.9
ADAM_B2 = 0.999
ADAM_EPS = 1e-08
ADAM_WD = 0.01
ADAM_STEP = 10

ROW_TILE = 512
ATT_TILE = 256
VMEM_LIMIT = 56 * 1024 * 1024
BF16_ROWS = 16
LANES = 128


def _params(sem=None, vmem=VMEM_LIMIT):
    return pltpu.CompilerParams(dimension_semantics=sem, vmem_limit_bytes=vmem)


def _dot(a, b):
    return jnp.dot(a, b, preferred_element_type=F32)


def _dot_nt(a, b):
    return lax.dot_general(a, b, (((1,), (1,)), ((), ())), preferred_element_type=F32)


def _dot_tn(a, b):
    return lax.dot_general(a, b, (((0,), (0,)), ((), ())), preferred_element_type=F32)


def _dot_exact(t, perm):
    t1 = t.astype(BF16)
    r1 = t - t1.astype(F32)
    t2 = r1.astype(BF16)
    t3 = (r1 - t2.astype(F32)).astype(BF16)
    return _dot(t1, perm) + _dot(t2, perm) + _dot(t3, perm)


def _sum0(a):
    return jnp.sum(a, axis=0, keepdims=True)


def _rms(xt):
    r = lax.rsqrt(jnp.mean(xt * xt, axis=-1, keepdims=True) + EPS)
    return xt * r, r


def _rms_bwd(dy, xt, g):
    xhat, r = _rms(xt)
    dxhat = dy * g
    dx = r * (dxhat - xhat * jnp.mean(dxhat * xhat, axis=-1, keepdims=True))
    return dx, _sum0(dy * xhat)


def _normmod_bwd(dh, xt, gn, sc):
    xhat, _ = _rms(xt)
    dn = dh * (1.0 + sc)
    dx, dgn = _rms_bwd(dn, xt, gn)
    return dx, _sum0(dh), _sum0(dh * (xhat * gn)), dgn


def _row_tile(s):
    return min(s, ROW_TILE)


def _full(shape):
    n = len(shape)
    return pl.BlockSpec(shape, lambda *_: (0,) * n)


def _resident(shape):
    n = len(shape)
    return pl.BlockSpec(shape, lambda *_: (0,) * n, pipeline_mode=pl.Buffered(1))


def ffn_fwd(x, gn, sh, sc, gt, wg, wu, wd):
    s, d = x.shape
    k_chunks, _, fs = wg.shape
    tm = _row_tile(s)

    def body(x_ref, gn_ref, sh_ref, sc_ref, gt_ref, wg_ref, wu_ref, wd_ref,
             xo_ref, h_ref, gate_ref, up_ref, y_ref):
        xt = x_ref[...]
        xhat, _ = _rms(xt)
        h = (xhat * gn_ref[...] * (1.0 + sc_ref[...]) + sh_ref[...]).astype(BF16)
        h_ref[...] = h
        y = jnp.zeros((tm, d), F32)
        for k in range(k_chunks):
            gate = _dot(h, wg_ref[k])
            up = _dot(h, wu_ref[k])
            gate_ref[k] = gate.astype(BF16)
            up_ref[k] = up.astype(BF16)
            y += _dot((gate * jax.nn.sigmoid(gate) * up).astype(BF16), wd_ref[k])
        y_ref[...] = y.astype(BF16)
        xo_ref[...] = xt + 0.5 * gt_ref[...] * y

    row = pl.BlockSpec((tm, d), lambda i: (i, 0))
    vec = pl.BlockSpec((1, d), lambda i: (0, 0))
    act = pl.BlockSpec((k_chunks, tm, fs), lambda i: (0, i, 0))
    return pl.pallas_call(
        body, name="ffn_fwd",
        grid=(s // tm,),
        in_specs=[row, vec, vec, vec, vec, _resident(wg.shape), _resident(wu.shape), _resident(wd.shape)],
        out_specs=[row, row, act, act, row],
        out_shape=[jax.ShapeDtypeStruct((s, d), F32), jax.ShapeDtypeStruct((s, d), BF16),
                   jax.ShapeDtypeStruct((k_chunks, s, fs), BF16), jax.ShapeDtypeStruct((k_chunks, s, fs), BF16),
                   jax.ShapeDtypeStruct((s, d), BF16)],
        compiler_params=_params(("arbitrary",)),
    )(x, gn, sh, sc, gt, wg, wu, wd)


def ffn_bwd_act(dxn, gate, up, gt, wd):
    s, d = dxn.shape
    k_chunks, fs, _ = wd.shape
    tm = _row_tile(s)

    def body(dxn_ref, gate_ref, up_ref, gt_ref, wd_ref, dy_ref, a_ref, dgate_ref, dup_ref):
        dy = (0.5 * gt_ref[...] * dxn_ref[...]).astype(BF16)
        dy_ref[...] = dy
        for k in range(k_chunks):
            da = _dot_nt(dy, wd_ref[k])
            g = gate_ref[k].astype(F32)
            u = up_ref[k].astype(F32)
            sg = jax.nn.sigmoid(g)
            sl = g * sg
            a_ref[k] = (sl * u).astype(BF16)
            dgate_ref[k] = (da * u * (sg * (1.0 + g * (1.0 - sg)))).astype(BF16)
            dup_ref[k] = (da * sl).astype(BF16)

    row = pl.BlockSpec((tm, d), lambda i: (i, 0))
    act = pl.BlockSpec((k_chunks, tm, fs), lambda i: (0, i, 0))
    act_shape = jax.ShapeDtypeStruct((k_chunks, s, fs), BF16)
    return pl.pallas_call(
        body, name="ffn_bwd_act",
        grid=(s // tm,),
        in_specs=[row, act, act, pl.BlockSpec((1, d), lambda i: (0, 0)), _resident(wd.shape)],
        out_specs=[row, act, act, act],
        out_shape=[jax.ShapeDtypeStruct((s, d), BF16), act_shape, act_shape, act_shape],
        compiler_params=_params(("arbitrary",)),
    )(dxn, gate, up, gt, wd)


def ffn_bwd_in(dxn, x, y, dgate, dup, gn, sc, wg, wu):
    s, d = x.shape
    k_chunks, _, fs = wg.shape
    tm = _row_tile(s)

    def body(dxn_ref, x_ref, y_ref, dgate_ref, dup_ref, gn_ref, sc_ref, wg_ref, wu_ref, dx_ref, dvec_ref):
        i = pl.program_id(0)

        @pl.when(i == 0)
        def _():
            dvec_ref[...] = jnp.zeros_like(dvec_ref)

        dh = jnp.zeros((tm, d), F32)
        for k in range(k_chunks):
            dh += _dot_nt(dgate_ref[k], wg_ref[k]) + _dot_nt(dup_ref[k], wu_ref[k])
        dxn_t = dxn_ref[...]
        dx, dsh, dsc, dgn = _normmod_bwd(dh, x_ref[...], gn_ref[...], sc_ref[...])
        dx_ref[...] = dx + dxn_t
        dvec_ref[0:1, :] += dsh
        dvec_ref[1:2, :] += dsc
        dvec_ref[2:3, :] += _sum0(0.5 * dxn_t * y_ref[...].astype(F32))
        dvec_ref[3:4, :] += dgn

    row = pl.BlockSpec((tm, d), lambda i: (i, 0))
    vec = pl.BlockSpec((1, d), lambda i: (0, 0))
    act = pl.BlockSpec((k_chunks, tm, fs), lambda i: (0, i, 0))
    return pl.pallas_call(
        body, name="ffn_bwd_in",
        grid=(s // tm,),
        in_specs=[row, row, row, act, act, vec, vec, _resident(wg.shape), _resident(wu.shape)],
        out_specs=[row, pl.BlockSpec((8, d), lambda i: (0, 0))],
        out_shape=[jax.ShapeDtypeStruct((s, d), F32), jax.ShapeDtypeStruct((8, d), F32)],
        compiler_params=_params(("arbitrary",)),
    )(dxn, x, y, dgate, dup, gn, sc, wg, wu)


def tn_mm(a, b):
    ga, s, m = a.shape
    gb, _, n = b.shape
    g = max(ga, gb)

    def body(a_ref, b_ref, o_ref):
        o_ref[...] = _dot_tn(a_ref[...], b_ref[...])

    a_spec = pl.BlockSpec((None, s, m), (lambda gi: (gi, 0, 0)) if ga > 1 else (lambda gi: (0, 0, 0)))
    b_spec = pl.BlockSpec((None, s, n), (lambda gi: (gi, 0, 0)) if gb > 1 else (lambda gi: (0, 0, 0)))
    return pl.pallas_call(
        body, name="tn_mm",
        grid=(g,), in_specs=[a_spec, b_spec], out_specs=pl.BlockSpec((None, m, n), lambda gi: (gi, 0, 0)),
        out_shape=jax.ShapeDtypeStruct((g, m, n), F32),
        compiler_params=_params(("arbitrary",)),
    )(a, b)


def mix_in_fwd(x, gn, sh, sc, w_in_t):
    s, d = x.shape
    tm = _row_tile(s)
    o1, o2, o3 = POOL_WIDTH, POOL_WIDTH + Q_LORA, POOL_WIDTH + Q_LORA + KV_LORA

    def body(x_ref, gn_ref, sh_ref, sc_ref, w_ref, h_ref, u_ref, cq_ref, ckv_ref, kr_ref):
        xhat, _ = _rms(x_ref[...])
        h = (xhat * gn_ref[...] * (1.0 + sc_ref[...]) + sh_ref[...]).astype(BF16)
        h_ref[...] = h
        z = _dot_nt(h, w_ref[0:o3, :])
        u_ref[...] = z[:, 0:o1]
        cq_ref[...] = z[:, o1:o2]
        ckv_ref[...] = z[:, o2:o3]
        kr_ref[...] = _dot_nt(h, w_ref[o3:, :])

    row = lambda w: pl.BlockSpec((tm, w), lambda i: (i, 0))
    vec = pl.BlockSpec((1, d), lambda i: (0, 0))
    return pl.pallas_call(
        body, name="mix_in_fwd",
        grid=(s // tm,),
        in_specs=[row(d), vec, vec, vec, _full(w_in_t.shape)],
        out_specs=[row(d), row(POOL_WIDTH), row(Q_LORA), row(KV_LORA), row(QK_ROPE)],
        out_shape=[jax.ShapeDtypeStruct((s, d), BF16), jax.ShapeDtypeStruct((s, POOL_WIDTH), F32),
                   jax.ShapeDtypeStruct((s, Q_LORA), F32), jax.ShapeDtypeStruct((s, KV_LORA), F32),
                   jax.ShapeDtypeStruct((s, QK_ROPE), F32)],
        compiler_params=_params(("arbitrary",)),
    )(x, gn, sh, sc, w_in_t)


def mix_in_bwd(dxn, du, dcq, dckv, dkr, x, gn, sc, w_in_t):
    s, d = x.shape
    tm = _row_tile(s)
    o1, o2, o3 = POOL_WIDTH, POOL_WIDTH + Q_LORA, POOL_WIDTH + Q_LORA + KV_LORA
    n_z = w_in_t.shape[0]

    def body(dxn_ref, du_ref, dcq_ref, dckv_ref, dkr_ref, x_ref, gn_ref, sc_ref, w_ref, dx_ref, dz_ref, dvec_ref):
        i = pl.program_id(0)

        @pl.when(i == 0)
        def _():
            dvec_ref[...] = jnp.zeros_like(dvec_ref)

        dub = du_ref[...].astype(BF16)
        dqb = dcq_ref[...].astype(BF16)
        dkb = dckv_ref[...].astype(BF16)
        drb = dkr_ref[...].astype(BF16)
        dz_ref[:, 0:o1] = dub
        dz_ref[:, o1:o2] = dqb
        dz_ref[:, o2:o3] = dkb
        dz_ref[:, o3:] = drb
        dh = (_dot(dub, w_ref[0:o1, :]) + _dot(dqb, w_ref[o1:o2, :]) + _dot(dkb, w_ref[o2:o3, :])
              + _dot(drb, w_ref[o3:, :]))
        dx, dsh, dsc, dgn = _normmod_bwd(dh, x_ref[...], gn_ref[...], sc_ref[...])
        dx_ref[...] = dx + dxn_ref[...]
        dvec_ref[0:1, :] += dsh
        dvec_ref[1:2, :] += dsc
        dvec_ref[3:4, :] += dgn

    row = lambda w: pl.BlockSpec((tm, w), lambda i: (i, 0))
    vec = pl.BlockSpec((1, d), lambda i: (0, 0))
    return pl.pallas_call(
        body, name="mix_in_bwd",
        grid=(s // tm,),
        in_specs=[row(d), row(POOL_WIDTH), row(Q_LORA), row(KV_LORA), row(QK_ROPE), row(d), vec, vec,
                  _full(w_in_t.shape)],
        out_specs=[row(d), row(n_z), pl.BlockSpec((8, d), lambda i: (0, 0))],
        out_shape=[jax.ShapeDtypeStruct((s, d), F32), jax.ShapeDtypeStruct((s, n_z), BF16),
                   jax.ShapeDtypeStruct((8, d), F32)],
        compiler_params=_params(("arbitrary",)),
    )(dxn, du, dcq, dckv, dkr, x, gn, sc, w_in_t)


def _window_sum(a, w, rows, forward):
    s = a.shape[0]
    step = 1
    while step < w:
        if forward:
            shifted = jnp.where(rows < s - step, pltpu.roll(a, s - step, 0), 0.0)
        else:
            shifted = jnp.where(rows >= step, pltpu.roll(a, step, 0), 0.0)
        a = a + shifted
        step *= 2
    return a


def pool_fwd(u, pool_w, pool_scale):
    s = u.shape[0]

    def body(u_ref, w_ref, sc_ref, y_ref, diff_ref):
        rows = lax.broadcasted_iota(jnp.int32, (s, POOL_GC), 0)
        for g, w in enumerate(POOL_WINDOWS):
            cols = slice(g * POOL_GC, (g + 1) * POOL_GC)
            ug = u_ref[:, cols]
            cnt = jnp.minimum(rows + 1, w).astype(F32)
            diff = (_window_sum(ug, w, rows, False) / cnt - ug).astype(BF16)
            diff_ref[:, cols] = diff
            y_ref[:, cols] = _dot(diff, w_ref[g].astype(BF16)) * sc_ref[:, cols]

    return pl.pallas_call(
        body, name="pool_fwd",
        out_shape=[jax.ShapeDtypeStruct(u.shape, F32), jax.ShapeDtypeStruct(u.shape, BF16)],
        compiler_params=_params(),
    )(u, pool_w, pool_scale)


def pool_bwd(dy, diff, pool_w, pool_scale):
    s = dy.shape[0]

    def body(dy_ref, diff_ref, w_ref, sc_ref, du_ref, dw_ref, dsc_ref):
        rows = lax.broadcasted_iota(jnp.int32, (s, POOL_GC), 0)
        for g, w in enumerate(POOL_WINDOWS):
            cols = slice(g * POOL_GC, (g + 1) * POOL_GC)
            dyg = dy_ref[:, cols]
            diff = diff_ref[:, cols]
            wb = w_ref[g].astype(BF16)
            dsc_ref[:, cols] = _sum0(dyg * _dot(diff, wb))
            dys = (dyg * sc_ref[:, cols]).astype(BF16)
            dw_ref[g] = _dot_tn(diff, dys)
            ddiff = _dot_nt(dys, wb)
            cnt = jnp.minimum(rows + 1, w).astype(F32)
            du_ref[:, cols] = _window_sum(ddiff / cnt, w, rows, True) - ddiff

    return pl.pallas_call(
        body, name="pool_bwd",
        out_shape=[jax.ShapeDtypeStruct(dy.shape, F32), jax.ShapeDtypeStruct(pool_w.shape, F32),
                   jax.ShapeDtypeStruct(pool_scale.shape, F32)],
        compiler_params=_params(),
    )(dy, diff, pool_w, pool_scale)


def mla_qkv_fwd(cq, ckv, kr, qan, kvan, wq, wkv, cos, sin, rot):
    s = cq.shape[0]
    tm = _row_tile(s)

    def body(cq_ref, ckv_ref, kr_ref, qan_ref, kvan_ref, wq_ref, wkv_ref, cos_ref, sin_ref, rot_ref,
             qn_ref, qr_ref, kn_ref, krr_ref, v_ref, ql_ref, kvl_ref):
        cos_t = cos_ref[...]
        sin_t = sin_ref[...]
        perm = rot_ref[...]

        def rope(t):
            return t * cos_t + _dot_exact(t, perm) * sin_t

        qhat, _ = _rms(cq_ref[...])
        ql = (qhat * qan_ref[...]).astype(BF16)
        ql_ref[...] = ql
        khat, _ = _rms(ckv_ref[...])
        kvl = (khat * kvan_ref[...]).astype(BF16)
        kvl_ref[...] = kvl
        krr_ref[...] = rope(kr_ref[...]).astype(BF16)
        for h in range(N_HEADS):
            q = _dot(ql, wq_ref[h])
            qn_ref[h] = q[:, 0:QK_NOPE].astype(BF16)
            qr_ref[h] = rope(q[:, QK_NOPE:]).astype(BF16)
            kv = _dot(kvl, wkv_ref[h])
            kn_ref[h] = kv[:, 0:QK_NOPE].astype(BF16)
            v_ref[h] = kv[:, QK_NOPE:].astype(BF16)

    row = lambda w: pl.BlockSpec((tm, w), lambda i: (i, 0))
    hrow = lambda w: pl.BlockSpec((N_HEADS, tm, w), lambda i: (0, i, 0))
    return pl.pallas_call(
        body, name="mla_qkv_fwd",
        grid=(s // tm,),
        in_specs=[row(Q_LORA), row(KV_LORA), row(QK_ROPE), _full(qan.shape), _full(kvan.shape),
                  _full(wq.shape), _full(wkv.shape), row(QK_ROPE), row(QK_ROPE), _full(rot.shape)],
        out_specs=[hrow(QK_NOPE), hrow(QK_ROPE), hrow(QK_NOPE), row(QK_ROPE), hrow(V_HEAD), row(Q_LORA), row(KV_LORA)],
        out_shape=[jax.ShapeDtypeStruct((N_HEADS, s, QK_NOPE), BF16), jax.ShapeDtypeStruct((N_HEADS, s, QK_ROPE), BF16),
                   jax.ShapeDtypeStruct((N_HEADS, s, QK_NOPE), BF16), jax.ShapeDtypeStruct((s, QK_ROPE), BF16),
                   jax.ShapeDtypeStruct((N_HEADS, s, V_HEAD), BF16), jax.ShapeDtypeStruct((s, Q_LORA), BF16),
                   jax.ShapeDtypeStruct((s, KV_LORA), BF16)],
        compiler_params=_params(("arbitrary",)),
    )(cq, ckv, kr, qan, kvan, wq, wkv, cos, sin, rot)


def _attn_probs(qn_ref, qr_ref, kn_ref, kr_ref, qi, tq):
    n = (qi + 1) * tq
    rows = slice(qi * tq, n)
    sc = (_dot_nt(qn_ref[rows, :], kn_ref[0:n, :]) + _dot_nt(qr_ref[rows, :], kr_ref[0:n, :])) * SOFTMAX_SCALE
    qpos = qi * tq + lax.broadcasted_iota(jnp.int32, (tq, n), 0)
    kpos = lax.broadcasted_iota(jnp.int32, (tq, n), 1)
    sc = jnp.where(qpos >= kpos, sc, -1e30)
    e = jnp.exp(sc - jnp.max(sc, axis=-1, keepdims=True))
    return e / jnp.sum(e, axis=-1, keepdims=True)


def attn_fwd(qn, qr, kn, krr, v):
    nh, s, _ = qn.shape
    tq = min(s, ATT_TILE)

    def body(qn_ref, qr_ref, kn_ref, kr_ref, v_ref, o_ref):
        for qi in range(s // tq):
            n = (qi + 1) * tq
            p = _attn_probs(qn_ref, qr_ref, kn_ref, kr_ref, qi, tq).astype(BF16)
            o_ref[qi * tq:n, :] = _dot(p, v_ref[0:n, :])

    head = lambda w: pl.BlockSpec((None, s, w), lambda h: (h, 0, 0))
    return pl.pallas_call(
        body, name="attn_fwd",
        grid=(nh,),
        in_specs=[head(QK_NOPE), head(QK_ROPE), head(QK_NOPE), _full(krr.shape), head(V_HEAD)],
        out_specs=pl.BlockSpec((s, V_HEAD), lambda h: (0, h)),
        out_shape=jax.ShapeDtypeStruct((s, nh * V_HEAD), F32),
        compiler_params=_params(("arbitrary",)),
    )(qn, qr, kn, krr, v)


def attn_bwd(qn, qr, kn, krr, v, do):
    nh, s, _ = qn.shape
    tq = min(s, ATT_TILE)

    def body(qn_ref, qr_ref, kn_ref, kr_ref, v_ref, do_ref, dqn_ref, dqr_ref, dkn_ref, dkr_ref, dv_ref):
        dkn_ref[...] = jnp.zeros_like(dkn_ref)
        dkr_ref[...] = jnp.zeros_like(dkr_ref)
        dv_ref[...] = jnp.zeros_like(dv_ref)
        for qi in range(s // tq):
            n = (qi + 1) * tq
            rows = slice(qi * tq, n)
            p = _attn_probs(qn_ref, qr_ref, kn_ref, kr_ref, qi, tq)
            dob = do_ref[rows, :].astype(BF16)
            dp = _dot_nt(dob, v_ref[0:n, :])
            ds = (p * (dp - jnp.sum(p * dp, axis=-1, keepdims=True)) * SOFTMAX_SCALE).astype(BF16)
            dqn_ref[rows, :] = _dot(ds, kn_ref[0:n, :])
            dqr_ref[rows, :] = _dot(ds, kr_ref[0:n, :])
            dkn_ref[0:n, :] += _dot_tn(ds, qn_ref[rows, :])
            dkr_ref[0:n, :] += _dot_tn(ds, qr_ref[rows, :])
            dv_ref[0:n, :] += _dot_tn(p.astype(BF16), dob)

    head = lambda w: pl.BlockSpec((None, s, w), lambda h: (h, 0, 0))
    return pl.pallas_call(
        body, name="attn_bwd",
        grid=(nh,),
        in_specs=[head(QK_NOPE), head(QK_ROPE), head(QK_NOPE), _full(krr.shape), head(V_HEAD),
                  pl.BlockSpec((s, V_HEAD), lambda h: (0, h))],
        out_specs=[head(QK_NOPE), head(QK_ROPE), head(QK_NOPE), head(QK_ROPE), head(V_HEAD)],
        out_shape=[jax.ShapeDtypeStruct((nh, s, QK_NOPE), F32), jax.ShapeDtypeStruct((nh, s, QK_ROPE), F32),
                   jax.ShapeDtypeStruct((nh, s, QK_NOPE), F32), jax.ShapeDtypeStruct((nh, s, QK_ROPE), F32),
                   jax.ShapeDtypeStruct((nh, s, V_HEAD), F32)],
        compiler_params=_params(("arbitrary",)),
    )(qn, qr, kn, krr, v, do)


def mla_qkv_bwd(dqn, dqr, dkn, dkr, dv, cq, ckv, qan, kvan, wq, wkv, cos, sin, rot_t):
    s = cq.shape[0]
    tm = _row_tile(s)

    def body(dqn_ref, dqr_ref, dkn_ref, dkr_ref, dv_ref, cq_ref, ckv_ref, qan_ref, kvan_ref,
             wq_ref, wkv_ref, cos_ref, sin_ref, rot_ref,
             dcq_ref, dckv_ref, dkro_ref, gq_ref, gkv_ref, dqan_ref, dkvan_ref):
        i = pl.program_id(0)

        @pl.when(i == 0)
        def _():
            dqan_ref[...] = jnp.zeros_like(dqan_ref)
            dkvan_ref[...] = jnp.zeros_like(dkvan_ref)

        cos_t = cos_ref[...]
        sin_t = sin_ref[...]
        perm_t = rot_ref[...]

        def unrope(t):
            return t * cos_t + _dot_exact(t * sin_t, perm_t)

        acc_q = jnp.zeros((tm, Q_LORA), F32)
        acc_kv = jnp.zeros((tm, KV_LORA), F32)
        dkr_sum = jnp.zeros((tm, QK_ROPE), F32)
        for h in range(N_HEADS):
            a = dqn_ref[h].astype(BF16)
            b = unrope(dqr_ref[h]).astype(BF16)
            gq_ref[h, :, 0:QK_NOPE] = a
            gq_ref[h, :, QK_NOPE:] = b
            wq_h = wq_ref[h]
            acc_q += _dot_nt(a, wq_h[:, 0:QK_NOPE]) + _dot_nt(b, wq_h[:, QK_NOPE:])
            dk = dkn_ref[h].astype(BF16)
            dvv = dv_ref[h].astype(BF16)
            gkv_ref[h, :, 0:QK_NOPE] = dk
            gkv_ref[h, :, QK_NOPE:] = dvv
            wkv_h = wkv_ref[h]
            acc_kv += _dot_nt(dk, wkv_h[:, 0:QK_NOPE]) + _dot_nt(dvv, wkv_h[:, QK_NOPE:])
            dkr_sum += dkr_ref[h]
        dkro_ref[...] = unrope(dkr_sum)
        dcq, dqan = _rms_bwd(acc_q, cq_ref[...], qan_ref[...])
        dcq_ref[...] = dcq
        dqan_ref[...] += dqan
        dckv, dkvan = _rms_bwd(acc_kv, ckv_ref[...], kvan_ref[...])
        dckv_ref[...] = dckv
        dkvan_ref[...] += dkvan

    row = lambda w: pl.BlockSpec((tm, w), lambda i: (i, 0))
    hrow = lambda w: pl.BlockSpec((N_HEADS, tm, w), lambda i: (0, i, 0))
    return pl.pallas_call(
        body, name="mla_qkv_bwd",
        grid=(s // tm,),
        in_specs=[hrow(QK_NOPE), hrow(QK_ROPE), hrow(QK_NOPE), hrow(QK_ROPE), hrow(V_HEAD),
                  row(Q_LORA), row(KV_LORA), _full(qan.shape), _full(kvan.shape),
                  _full(wq.shape), _full(wkv.shape), row(QK_ROPE), row(QK_ROPE), _full(rot_t.shape)],
        out_specs=[row(Q_LORA), row(KV_LORA), row(QK_ROPE), hrow(QK_NOPE + QK_ROPE), hrow(QK_NOPE + V_HEAD),
                   _full(qan.shape), _full(kvan.shape)],
        out_shape=[jax.ShapeDtypeStruct((s, Q_LORA), F32), jax.ShapeDtypeStruct((s, KV_LORA), F32),
                   jax.ShapeDtypeStruct((s, QK_ROPE), F32),
                   jax.ShapeDtypeStruct((N_HEADS, s, QK_NOPE + QK_ROPE), BF16),
                   jax.ShapeDtypeStruct((N_HEADS, s, QK_NOPE + V_HEAD), BF16),
                   jax.ShapeDtypeStruct(qan.shape, F32), jax.ShapeDtypeStruct(kvan.shape, F32)],
        compiler_params=_params(("arbitrary",)),
    )(dqn, dqr, dkn, dkr, dv, cq, ckv, qan, kvan, wq, wkv, cos, sin, rot_t)


def out_proj_fwd(yp, om, w_out, x, gt):
    s, d = x.shape
    n_sh, rs, _ = w_out.shape
    tm = _row_tile(s)
    per = POOL_WIDTH // rs

    def body(yp_ref, om_ref, w_ref, x_ref, gt_ref, xo_ref, ycat_ref, y_ref):
        y = jnp.zeros((tm, d), F32)
        for j in range(n_sh):
            src = yp_ref if j < per else om_ref
            part = src[:, (j % per) * rs:(j % per + 1) * rs].astype(BF16)
            ycat_ref[j] = part
            y += _dot(part, w_ref[j])
        y_ref[...] = y.astype(BF16)
        xo_ref[...] = x_ref[...] + gt_ref[...] * y

    row = lambda w: pl.BlockSpec((tm, w), lambda i: (i, 0))
    return pl.pallas_call(
        body, name="out_proj_fwd",
        grid=(s // tm,),
        in_specs=[row(POOL_WIDTH), row(POOL_WIDTH), _full(w_out.shape), row(d), pl.BlockSpec((1, d), lambda i: (0, 0))],
        out_specs=[row(d), pl.BlockSpec((n_sh, tm, rs), lambda i: (0, i, 0)), row(d)],
        out_shape=[jax.ShapeDtypeStruct((s, d), F32), jax.ShapeDtypeStruct((n_sh, s, rs), BF16),
                   jax.ShapeDtypeStruct((s, d), BF16)],
        compiler_params=_params(("arbitrary",)),
    )(yp, om, w_out, x, gt)


def out_proj_bwd(dxn, y, gt, w_out):
    s, d = dxn.shape
    n_sh, rs, _ = w_out.shape
    tm = _row_tile(s)
    per = POOL_WIDTH // rs

    def body(dxn_ref, y_ref, gt_ref, w_ref, dy_ref, dyp_ref, dom_ref, dgt_ref):
        i = pl.program_id(0)

        @pl.when(i == 0)
        def _():
            dgt_ref[...] = jnp.zeros_like(dgt_ref)

        dxn_t = dxn_ref[...]
        dy = (gt_ref[...] * dxn_t).astype(BF16)
        dy_ref[...] = dy
        dgt_ref[...] += _sum0(dxn_t * y_ref[...].astype(F32))
        for j in range(n_sh):
            dst = dyp_ref if j < per else dom_ref
            dst[:, (j % per) * rs:(j % per + 1) * rs] = _dot_nt(dy, w_ref[j])

    row = lambda w: pl.BlockSpec((tm, w), lambda i: (i, 0))
    vec = pl.BlockSpec((1, d), lambda i: (0, 0))
    return pl.pallas_call(
        body, name="out_proj_bwd",
        grid=(s // tm,),
        in_specs=[row(d), row(d), vec, _full(w_out.shape)],
        out_specs=[row(d), row(POOL_WIDTH), row(POOL_WIDTH), vec],
        out_shape=[jax.ShapeDtypeStruct((s, d), BF16), jax.ShapeDtypeStruct((s, POOL_WIDTH), F32),
                   jax.ShapeDtypeStruct((s, POOL_WIDTH), F32), jax.ShapeDtypeStruct((1, d), F32)],
        compiler_params=_params(("arbitrary",)),
    )(dxn, y, gt, w_out)


def final_loss(x, gn, tgt):
    s, d = x.shape
    tm = _row_tile(s)

    def body(x_ref, gn_ref, t_ref, loss_ref, dx_ref, dgn_ref):
        i = pl.program_id(0)

        @pl.when(i == 0)
        def _():
            loss_ref[...] = jnp.zeros_like(loss_ref)
            dgn_ref[...] = jnp.zeros_like(dgn_ref)

        xt = x_ref[...]
        g = gn_ref[...]
        xhat, _ = _rms(xt)
        err = xhat * g - t_ref[...]
        per_tok = jnp.mean(err * err, axis=-1, keepdims=True)
        loss_ref[...] += jnp.broadcast_to(0.5 * _sum0(per_tok), loss_ref.shape)
        dx, dgn = _rms_bwd(err * (1.0 / d), xt, g)
        dx_ref[...] = dx
        dgn_ref[...] += dgn

    row = pl.BlockSpec((tm, d), lambda i: (i, 0))
    vec = pl.BlockSpec((1, d), lambda i: (0, 0))
    return pl.pallas_call(
        body, name="final_loss",
        grid=(s // tm,),
        in_specs=[row, vec, row],
        out_specs=[pl.BlockSpec((1, LANES), lambda i: (0, 0)), row, vec],
        out_shape=[jax.ShapeDtypeStruct((1, LANES), F32), jax.ShapeDtypeStruct((s, d), F32),
                   jax.ShapeDtypeStruct((1, d), F32)],
        compiler_params=_params(("arbitrary",)),
    )(x, gn, tgt)


def _col_tile(cols):
    return 768 if cols % 768 == 0 else cols


def ada_fwd(c16, ada_w, ada_b_loc):
    n_layers, d, cols = ada_w.shape
    tn = _col_tile(cols)

    def body(c_ref, w_ref, b_ref, o_ref):
        cv = c_ref[...]
        ca = (cv * jax.nn.sigmoid(cv)).astype(BF16)
        o_ref[...] = _dot(ca, w_ref[...].astype(BF16)) + b_ref[...]

    return pl.pallas_call(
        body, name="ada_fwd",
        grid=(n_layers, cols // tn),
        in_specs=[pl.BlockSpec((16, d), lambda l, j: (0, 0)), pl.BlockSpec((None, d, tn), lambda l, j: (l, 0, j)),
                  pl.BlockSpec((None, 1, tn), lambda l, j: (l, 0, j))],
        out_specs=pl.BlockSpec((None, 16, tn), lambda l, j: (l, 0, j)),
        out_shape=jax.ShapeDtypeStruct((n_layers, 16, cols), F32),
        compiler_params=_params(("arbitrary", "arbitrary")),
    )(c16, ada_w, ada_b_loc)


def ada_bwd(c16, dmod16):
    n_layers, _, cols = dmod16.shape
    d = c16.shape[1]
    tn = _col_tile(cols)

    def body(c_ref, g_ref, o_ref):
        cv = c_ref[...]
        ca = (cv * jax.nn.sigmoid(cv)).astype(BF16)
        o_ref[...] = _dot_tn(ca, g_ref[...].astype(BF16))

    return pl.pallas_call(
        body, name="ada_bwd",
        grid=(n_layers, cols // tn),
        in_specs=[pl.BlockSpec((16, d), lambda l, j: (0, 0)), pl.BlockSpec((None, 16, tn), lambda l, j: (l, 0, j))],
        out_specs=pl.BlockSpec((None, d, tn), lambda l, j: (l, 0, j)),
        out_shape=jax.ShapeDtypeStruct((n_layers, d, cols), F32),
        compiler_params=_params(("arbitrary", "arbitrary")),
    )(c16, dmod16)


def _as_rows(a):
    if a.ndim == 1:
        return a.reshape(1, a.shape[0])
    return a.reshape(-1, a.shape[-1])


def _rows_tile(r, c, itemsize=4, budget=2 * 1024 * 1024):
    if r * c * itemsize <= budget:
        return r
    best = None
    t = BF16_ROWS
    while t < r:
        if r % t == 0 and t * c * itemsize <= budget:
            best = t
        t += BF16_ROWS
    return best if best is not None else r


def cast_bf16(w):
    w2 = _as_rows(w)
    r, c = w2.shape
    tr = _rows_tile(r, c)

    def body(w_ref, o_ref):
        o_ref[...] = w_ref[...].astype(BF16)

    spec = pl.BlockSpec((tr, c), lambda i: (i, 0))
    out = pl.pallas_call(
        body, name="cast_bf16", grid=(r // tr,), in_specs=[spec], out_specs=spec,
        out_shape=jax.ShapeDtypeStruct((r, c), BF16), compiler_params=_params(("arbitrary",)),
    )(w2)
    return out.reshape(w.shape)


def adamw(w, g, m, v):
    shape = w.shape
    w2, g2, m2, v2 = (_as_rows(t) for t in (w, g, m, v))
    r, c = w2.shape
    tr = _rows_tile(r, c, budget=1024 * 1024)
    c1 = 1.0 - ADAM_B1 ** ADAM_STEP
    c2 = 1.0 - ADAM_B2 ** ADAM_STEP

    def body(w_ref, g_ref, m_ref, v_ref, d_ref, mo_ref, vo_ref):
        gv = g_ref[...]
        mn = ADAM_B1 * m_ref[...] + (1.0 - ADAM_B1) * gv
        vn = ADAM_B2 * v_ref[...] + (1.0 - ADAM_B2) * (gv * gv)
        mo_ref[...] = mn
        vo_ref[...] = vn
        d_ref[...] = -ADAM_LR * ((mn / c1) / (jnp.sqrt(vn / c2) + ADAM_EPS) + ADAM_WD * w_ref[...])

    spec = pl.BlockSpec((tr, c), lambda i: (i, 0))
    outs = pl.pallas_call(
        body, name="adamw", grid=(r // tr,), in_specs=[spec] * 4, out_specs=[spec] * 3,
        out_shape=[jax.ShapeDtypeStruct((r, c), F32)] * 3, compiler_params=_params(("arbitrary",)),
    )(w2, g2, m2, v2)
    return tuple(o.reshape(shape) for o in outs)


def sum_devices(a):
    n, r, c = a.shape
    tr = _rows_tile(r, c, budget=512 * 1024)

    def body(a_ref, o_ref):
        acc = a_ref[0]
        for j in range(1, n):
            acc = acc + a_ref[j]
        o_ref[...] = acc

    return pl.pallas_call(
        body, name="sum_devices", grid=(r // tr,),
        in_specs=[pl.BlockSpec((n, tr, c), lambda i: (0, i, 0))], out_specs=pl.BlockSpec((tr, c), lambda i: (i, 0)),
        out_shape=jax.ShapeDtypeStruct((r, c), F32), compiler_params=_params(("arbitrary",)),
    )(a)


def _split_axis(r, c):
    if (r // 2) % BF16_ROWS == 0 and r % 2 == 0:
        return 0
    assert c % (2 * LANES) == 0, (r, c)
    return 1


def _half_shape(r, c):
    return (r // 2, c) if _split_axis(r, c) == 0 else (r, c // 2)


def _half_at(ref, lead, which):
    r, c = ref.shape[-2:]
    if _split_axis(r, c) == 0:
        return ref.at[(*lead, pl.ds(which * (r // 2), r // 2), slice(None))]
    return ref.at[(*lead, slice(None), pl.ds(which * (c // 2), c // 2))]


def _half_spec(r, c, lead_block, imap):
    hr, hc = _half_shape(r, c)
    if _split_axis(r, c) == 0:
        return pl.BlockSpec((*lead_block, hr, hc), lambda *a: (*imap(*a)[0], imap(*a)[1], 0))
    return pl.BlockSpec((*lead_block, hr, hc), lambda *a: (*imap(*a)[0], 0, imap(*a)[1]))


def pair_add(g, ra, half):
    n_sl, r, c = g.shape
    hr, hc = _half_shape(r, c)

    def body(h_ref, g_ref, ra_ref, p_ref, pb_ref):
        p = g_ref[...] + ra_ref[...]
        p_ref[...] = p
        pb_ref[...] = p.astype(BF16)

    mine = pl.BlockSpec((None, hr, hc), lambda k, h: (k, 0, 0))
    return pl.pallas_call(
        body, name="pair_add",
        grid_spec=pltpu.PrefetchScalarGridSpec(
            num_scalar_prefetch=1, grid=(n_sl,),
            in_specs=[_half_spec(r, c, (None,), lambda k, h: ((k,), h[0])), mine], out_specs=[mine, mine]),
        out_shape=[jax.ShapeDtypeStruct((n_sl, hr, hc), F32), jax.ShapeDtypeStruct((n_sl, hr, hc), BF16)],
        compiler_params=_params(("arbitrary",)),
    )(half, g, ra)


def chip_sum(p32, rb, sel, shape, acc):
    n_layers, r, c = shape
    hr, hc = _half_shape(r, c)

    def body(s_ref, p_ref, rb_ref, *rest):
        o_ref = rest[-1]
        acc_v = p_ref[...]
        for j in range(N_CHIPS - 1):
            acc_v = acc_v + rb_ref[j].astype(F32)
        o_ref[...] = acc_v

    in_specs = [pl.BlockSpec((None, hr, hc), lambda i, sr: (sr[1], 0, 0)),
                pl.BlockSpec((N_CHIPS - 1, hr, hc), lambda i, sr: (0, 0, 0))]
    args = [sel, p32, rb]
    aliases = {}
    if acc is not None:
        in_specs.append(pl.BlockSpec(memory_space=pl.ANY))
        args.append(acc)
        aliases = {3: 0}
    return pl.pallas_call(
        body, name="chip_sum",
        grid_spec=pltpu.PrefetchScalarGridSpec(
            num_scalar_prefetch=1, grid=(1,), in_specs=in_specs,
            out_specs=_half_spec(r, c, (None,), lambda i, sr: ((sr[2],), sr[0]))),
        out_shape=jax.ShapeDtypeStruct((n_layers, r, c), F32),
        input_output_aliases=aliases,
        compiler_params=_params(("arbitrary",)),
    )(*args)


def _me():
    return lax.axis_index("x"), lax.axis_index("y"), lax.axis_index("c")


def _flip(v, bit):
    return 1 - v if bit else v


def exchange8(xs, bcast):
    blk = xs.shape if bcast else xs.shape[1:]

    def body(x_ref, o_ref, send_sems, recv_sems, loc_sem):
        mx, my, mc = _me()
        me = 4 * mx + 2 * my + mc
        src = (lambda j: x_ref) if bcast else (lambda j: x_ref.at[j])
        loc = pltpu.make_async_copy(src(me), o_ref.at[me], loc_sem)
        loc.start()
        copies = []
        for o in range(1, N_DEV):
            px, py, pc = _flip(mx, o & 4), _flip(my, o & 2), _flip(mc, o & 1)
            cp = pltpu.make_async_remote_copy(
                src_ref=src(4 * px + 2 * py + pc), dst_ref=o_ref.at[me],
                send_sem=send_sems.at[o - 1], recv_sem=recv_sems.at[o - 1],
                device_id=(px, py, pc), device_id_type=MESH)
            cp.start()
            copies.append(cp)
        for cp in copies:
            cp.wait()
        loc.wait()

    return pl.pallas_call(
        body, name="exchange8_gather" if bcast else "exchange8_a2a",
        in_specs=[pl.BlockSpec(memory_space=pltpu.VMEM)], out_specs=pl.BlockSpec(memory_space=pltpu.VMEM),
        out_shape=jax.ShapeDtypeStruct((N_DEV,) + tuple(blk), xs.dtype),
        scratch_shapes=[pltpu.SemaphoreType.DMA((N_DEV - 1,)), pltpu.SemaphoreType.DMA((N_DEV - 1,)), pltpu.SemaphoreType.DMA],
        compiler_params=_params(),
    )(xs)


def weight_gather(shards, layer):
    n = len(shards)

    def body(*refs):
        src = refs[:n]
        dst = refs[n:2 * n]
        ici_send, ici_recv, d2d_send, d2d_recv, loc_sem = refs[2 * n:]
        mx, my, mc = _me()
        chip = 2 * mx + my
        locs = []
        for t in range(n):
            cp = pltpu.make_async_copy(src[t].at[layer], dst[t].at[chip], loc_sem.at[t])
            cp.start()
            locs.append(cp)
        sends = []
        for t in range(n):
            for o in range(1, N_CHIPS):
                px, py = _flip(mx, o & 2), _flip(my, o & 1)
                cp = pltpu.make_async_remote_copy(
                    src_ref=_half_at(src[t], (layer,), mc), dst_ref=_half_at(dst[t], (chip,), mc),
                    send_sem=ici_send.at[t, o - 1], recv_sem=ici_recv.at[t, o - 1],
                    device_id=(px, py, mc), device_id_type=MESH)
                cp.start()
                sends.append(cp)
        fwds = []
        for t in range(n):
            for o in range(1, N_CHIPS):
                px, py = _flip(mx, o & 2), _flip(my, o & 1)
                got = _half_at(dst[t], (2 * px + py,), mc)
                pltpu.make_async_remote_copy(
                    src_ref=got, dst_ref=got, send_sem=ici_send.at[t, o - 1], recv_sem=ici_recv.at[t, o - 1],
                    device_id=(px, py, mc), device_id_type=MESH).wait_recv()
                cp = pltpu.make_async_remote_copy(
                    src_ref=got, dst_ref=got, send_sem=d2d_send.at[t, o - 1], recv_sem=d2d_recv.at[t, o - 1],
                    device_id=(mx, my, 1 - mc), device_id_type=MESH)
                cp.start()
                fwds.append(cp)
        for cp in fwds:
            cp.wait()
        for cp in sends:
            cp.wait_send()
        for cp in locs:
            cp.wait()

    any_spec = pl.BlockSpec(memory_space=pl.ANY)
    return pl.pallas_call(
        body, name="weight_gather",
        in_specs=[any_spec] * n, out_specs=[any_spec] * n,
        out_shape=[jax.ShapeDtypeStruct((N_CHIPS,) + t.shape[1:], t.dtype) for t in shards],
        scratch_shapes=[pltpu.SemaphoreType.DMA((n, N_CHIPS - 1))] * 4 + [pltpu.SemaphoreType.DMA((n,))],
        compiler_params=_params(),
    )(*shards)


def pair_send_halves(gs):
    n = len(gs)

    def body(*refs):
        src = refs[:n]
        dst = refs[n:2 * n]
        send_sems, recv_sems = refs[2 * n:]
        mx, my, mc = _me()
        copies = []
        for t in range(n):
            cp = pltpu.make_async_remote_copy(
                src_ref=_half_at(src[t], (slice(None),), 1 - mc), dst_ref=dst[t],
                send_sem=send_sems.at[t], recv_sem=recv_sems.at[t],
                device_id=(mx, my, 1 - mc), device_id_type=MESH)
            cp.start()
            copies.append(cp)
        for cp in copies:
            cp.wait()

    any_spec = pl.BlockSpec(memory_space=pl.ANY)
    return pl.pallas_call(
        body, name="pair_send_halves",
        in_specs=[any_spec] * n, out_specs=[any_spec] * n,
        out_shape=[jax.ShapeDtypeStruct((g.shape[0],) + _half_shape(*g.shape[1:]), g.dtype) for g in gs],
        scratch_shapes=[pltpu.SemaphoreType.DMA((n,)), pltpu.SemaphoreType.DMA((n,))],
        compiler_params=_params(),
    )(*gs)


def chip_scatter(pbs):
    n = len(pbs)

    def body(*refs):
        src = refs[:n]
        dst = refs[n:2 * n]
        send_sems, recv_sems = refs[2 * n:]
        mx, my, mc = _me()
        copies = []
        for t in range(n):
            for o in range(1, N_CHIPS):
                px, py = _flip(mx, o & 2), _flip(my, o & 1)
                cp = pltpu.make_async_remote_copy(
                    src_ref=src[t].at[2 * px + py], dst_ref=dst[t].at[o - 1],
                    send_sem=send_sems.at[t, o - 1], recv_sem=recv_sems.at[t, o - 1],
                    device_id=(px, py, mc), device_id_type=MESH)
                cp.start()
                copies.append(cp)
        for cp in copies:
            cp.wait()

    any_spec = pl.BlockSpec(memory_space=pl.ANY)
    return pl.pallas_call(
        body, name="chip_scatter",
        in_specs=[any_spec] * n, out_specs=[any_spec] * n,
        out_shape=[jax.ShapeDtypeStruct((N_CHIPS - 1,) + p.shape[1:], p.dtype) for p in pbs],
        scratch_shapes=[pltpu.SemaphoreType.DMA((n, N_CHIPS - 1)), pltpu.SemaphoreType.DMA((n, N_CHIPS - 1))],
        compiler_params=_params(),
    )(*pbs)


def pair_fill_halves(fs):
    n = len(fs)

    def body(*refs):
        dst = refs[n:2 * n]
        send_sems, recv_sems = refs[2 * n:]
        mx, my, mc = _me()
        copies = []
        for t in range(n):
            mine = _half_at(dst[t], (slice(None),), mc)
            theirs = _half_at(dst[t], (slice(None),), 1 - mc)
            cp = pltpu.make_async_remote_copy(
                src_ref=mine, dst_ref=mine, send_sem=send_sems.at[t], recv_sem=recv_sems.at[t],
                device_id=(mx, my, 1 - mc), device_id_type=MESH)
            cp.start()
            copies.append((cp, pltpu.make_async_remote_copy(
                src_ref=theirs, dst_ref=theirs, send_sem=send_sems.at[t], recv_sem=recv_sems.at[t],
                device_id=(mx, my, 1 - mc), device_id_type=MESH)))
        for cp, arrival in copies:
            cp.wait_send()
            arrival.wait_recv()

    any_spec = pl.BlockSpec(memory_space=pl.ANY)
    return pl.pallas_call(
        body, name="pair_fill_halves",
        in_specs=[any_spec] * n, out_specs=[any_spec] * n,
        out_shape=[jax.ShapeDtypeStruct(f.shape, f.dtype) for f in fs],
        input_output_aliases={t: t for t in range(n)},
        scratch_shapes=[pltpu.SemaphoreType.DMA((n,)), pltpu.SemaphoreType.DMA((n,))],
        compiler_params=_params(),
    )(*fs)


def _pack_rows(parts, d):
    rows, spans = [], []
    at = 0
    for p in parts:
        flat = p.reshape(-1)
        n_rows = -(-flat.shape[0] // (8 * d)) * 8
        flat = jnp.pad(flat, (0, n_rows * d - flat.shape[0]))
        rows.append(flat.reshape(n_rows, d))
        spans.append((at, p.shape))
        at += n_rows
    return jnp.concatenate(rows, axis=0), spans


def _unpack_rows(packed, spans):
    out = []
    for at, shape in spans:
        n = math.prod(shape)
        d = packed.shape[1]
        n_rows = -(-n // d)
        out.append(packed[at:at + n_rows].reshape(-1)[:n].reshape(shape))
    return out


def _rotate_half_matrix():
    half = QK_ROPE // 2
    idx = jnp.arange(QK_ROPE)
    src = jnp.where(idx < half, idx + half, idx - half)
    sign = jnp.where(idx < half, -1.0, 1.0)
    return (jnp.zeros((QK_ROPE, QK_ROPE), F32).at[src, idx].set(sign)).astype(BF16)


def kernel(x, c, positions, ada_w, ada_b, ffn1_norm, ffn1_w_gate, ffn1_w_up, ffn1_w_down, mix_norm, w_in, pool_w, pool_scale, q_a_norm, w_q_b, kv_a_norm, w_kv_b, w_out, ffn2_norm, ffn2_w_gate, ffn2_w_up, ffn2_w_down, final_norm, loss_target, m_ada_w, m_ada_b, m_ffn1_norm, m_ffn1_w_gate, m_ffn1_w_up, m_ffn1_w_down, m_mix_norm, m_w_in, m_pool_w, m_pool_scale, m_q_a_norm, m_w_q_b, m_kv_a_norm, m_w_kv_b, m_w_out, m_ffn2_norm, m_ffn2_w_gate, m_ffn2_w_up, m_ffn2_w_down, m_final_norm, v_ada_w, v_ada_b, v_ffn1_norm, v_ffn1_w_gate, v_ffn1_w_up, v_ffn1_w_down, v_mix_norm, v_w_in, v_pool_w, v_pool_scale, v_q_a_norm, v_w_q_b, v_kv_a_norm, v_w_kv_b, v_w_out, v_ffn2_norm, v_ffn2_w_gate, v_ffn2_w_up, v_ffn2_w_down, v_final_norm):
    mx, my, mc = _me()
    chip = 2 * mx + my
    half = jnp.reshape(mc, (1,)).astype(jnp.int32)
    n_layers, d, ada_cols = ada_w.shape
    xt = x[0]
    tgt = loss_target[0]

    inv_freq = 1.0 / (ROPE_THETA ** (jnp.arange(0, QK_ROPE, 2, dtype=F32) / QK_ROPE))
    ang = positions[0].astype(F32)[:, None] * inv_freq
    ang = jnp.concatenate([ang, ang], axis=-1)
    cos, sin = jnp.cos(ang), jnp.sin(ang)
    rot = _rotate_half_matrix()
    rot_t = rot.T

    c_all = exchange8(c, True).reshape(N_DEV, d)
    c16 = jnp.pad(c_all, ((0, 8), (0, 0)))
    ada_b_loc = lax.dynamic_slice_in_dim(ada_b, chip * ada_cols, ada_cols, axis=1).reshape(n_layers, 1, ada_cols)
    mod_part = ada_fwd(c16, ada_w, ada_b_loc)[:, :N_DEV]
    mod_got = exchange8(jnp.transpose(mod_part, (1, 0, 2)), False)
    mod = jnp.transpose(mod_got.reshape(N_CHIPS, 2, n_layers, ada_cols)[:, 0], (1, 0, 2))
    mod = mod.reshape(n_layers, 9, 1, d)

    local = [ffn1_w_gate, ffn1_w_up, ffn1_w_down, jnp.transpose(w_in, (0, 2, 1)), w_q_b, w_kv_b, w_out,
             ffn2_w_gate, ffn2_w_up, ffn2_w_down]
    local_bf = [cast_bf16(w) for w in local]
    gathered = [weight_gather(local_bf, l) for l in range(n_layers)]

    row = lambda a, l: a[l].reshape(1, -1)
    saved = []
    for l in range(n_layers):
        g1, u1, d1, win, wq, wkv, wout, g2, u2, d2 = gathered[l]
        win = win.reshape(-1, d)
        sv = dict(x0=xt)
        xt, sv["h1"], sv["gate1"], sv["up1"], sv["y1"] = ffn_fwd(
            xt, row(ffn1_norm, l), mod[l, 0], mod[l, 1], mod[l, 2], g1, u1, d1)
        sv["x1"] = xt
        sv["h2"], u, cq, ckv, kr = mix_in_fwd(xt, row(mix_norm, l), mod[l, 3], mod[l, 4], win)
        sv["cq"], sv["ckv"] = cq, ckv
        yp, sv["diff"] = pool_fwd(u, pool_w[l], row(pool_scale, l))
        qn, qr, kn, krr, vv, sv["ql"], sv["kvl"] = mla_qkv_fwd(
            cq, ckv, kr, row(q_a_norm, l), row(kv_a_norm, l), wq, wkv, cos, sin, rot)
        sv["qkv"] = (qn, qr, kn, krr, vv)
        om = attn_fwd(qn, qr, kn, krr, vv)
        xt, sv["ycat"], sv["y2"] = out_proj_fwd(yp, om, wout, xt, mod[l, 5])
        sv["x2"] = xt
        xt, sv["h3"], sv["gate3"], sv["up3"], sv["y3"] = ffn_fwd(
            xt, row(ffn2_norm, l), mod[l, 6], mod[l, 7], mod[l, 8], g2, u2, d2)
        saved.append(sv)

    loss_vec, dx, d_final_norm = final_loss(xt, final_norm.reshape(1, d), tgt)
    loss = lax.psum(loss_vec[0, 0], ("x", "y", "c"))

    none = [None] * n_layers
    dmods, dnorm1, dnorm2, dnorm3 = list(none), list(none), list(none), list(none)
    dpw, dps, dqan_l, dkvan_l = list(none), list(none), list(none), list(none)
    reduced = [None] * len(local)
    for l in reversed(range(n_layers)):
        sv = saved[l]
        g1, u1, d1, win, wq, wkv, wout, g2, u2, d2 = gathered[l]
        win = win.reshape(-1, d)
        dy, a, dgt, dup = ffn_bwd_act(dx, sv["gate3"], sv["up3"], mod[l, 8], d2)
        dx, dvec3 = ffn_bwd_in(dx, sv["x2"], sv["y3"], dgt, dup, row(ffn2_norm, l), mod[l, 7], g2, u2)
        g_g2, g_u2, g_d2 = tn_mm(sv["h3"][None], dgt), tn_mm(sv["h3"][None], dup), tn_mm(a, dy[None])
        dy2, dyp, dom, dg2 = out_proj_bwd(dx, sv["y2"], mod[l, 5], wout)
        g_wout = tn_mm(sv["ycat"], dy2[None])
        qn, qr, kn, krr, vv = sv["qkv"]
        dqn, dqr, dkn, dkr, dvv = attn_bwd(qn, qr, kn, krr, vv, dom)
        dcq, dckv, dkr_in, gq, gkv, dqan_l[l], dkvan_l[l] = mla_qkv_bwd(
            dqn, dqr, dkn, dkr, dvv, sv["cq"], sv["ckv"], row(q_a_norm, l), row(kv_a_norm, l),
            wq, wkv, cos, sin, rot_t)
        g_wq, g_wkv = tn_mm(sv["ql"][None], gq), tn_mm(sv["kvl"][None], gkv)
        du, dpw[l], dps[l] = pool_bwd(dyp, sv["diff"], pool_w[l], row(pool_scale, l))
        dx, dz, dvec2 = mix_in_bwd(dx, du, dcq, dckv, dkr_in, sv["x1"], row(mix_norm, l), mod[l, 4], win)
        g_win = tn_mm(dz[None], sv["h2"][None]).reshape(N_CHIPS, -1, d)
        dy, a, dgt, dup = ffn_bwd_act(dx, sv["gate1"], sv["up1"], mod[l, 2], d1)
        dx, dvec1 = ffn_bwd_in(dx, sv["x0"], sv["y1"], dgt, dup, row(ffn1_norm, l), mod[l, 1], g1, u1)
        g_g1, g_u1, g_d1 = tn_mm(sv["h1"][None], dgt), tn_mm(sv["h1"][None], dup), tn_mm(a, dy[None])
        dmods[l] = jnp.concatenate([dvec1[0:3], dvec2[0:2], dg2, dvec3[0:3]], axis=0)
        dnorm1[l], dnorm2[l], dnorm3[l] = dvec1[3], dvec2[3], dvec3[3]

        full = [g_g1, g_u1, g_d1, g_win, g_wq, g_wkv, g_wout, g_g2, g_u2, g_d2]
        got = pair_send_halves(full)
        sums = [pair_add(g, ra, half) for g, ra in zip(full, got)]
        parts = chip_scatter([pb for _, pb in sums])
        sel = jnp.stack([mc, chip, jnp.asarray(l, mc.dtype)]).astype(jnp.int32)
        reduced = [chip_sum(p32, rb, sel, (n_layers,) + g.shape[1:], acc)
                   for (p32, _), rb, g, acc in zip(sums, parts, full, reduced)]

    g_local = list(pair_fill_halves(reduced))
    g_local[3] = jnp.transpose(g_local[3], (0, 2, 1))

    small_parts = [jnp.stack(dmods), jnp.stack(dnorm1), jnp.stack(dnorm2), jnp.stack(dnorm3), d_final_norm,
                   jnp.stack(dps), jnp.stack(dqan_l), jnp.stack(dkvan_l), jnp.stack(dpw)]
    packed, spans = _pack_rows(small_parts, d)
    gathered_small = exchange8(packed, True)
    total = sum_devices(gathered_small)
    (g_ada_b, g_n1, g_n2, g_n3, g_fn, g_ps, g_qan, g_kvan, g_pw) = _unpack_rows(total, spans)
    dmod_all = gathered_small[:, :9 * n_layers].reshape(N_DEV, n_layers, 9 * d)
    dmod_loc = lax.dynamic_slice_in_dim(dmod_all, chip * ada_cols, ada_cols, axis=2)
    dmod16 = jnp.pad(jnp.transpose(dmod_loc, (1, 0, 2)), ((0, 0), (0, 8), (0, 0)))
    g_ada_w = ada_bwd(c16, dmod16)

    grads = [g_ada_w, g_ada_b, g_n1, g_local[0], g_local[1], g_local[2], g_n2, g_local[3],
             g_pw, g_ps, g_qan, g_local[4], g_kvan, g_local[5], g_local[6], g_n3,
             g_local[7], g_local[8], g_local[9], g_fn]
    weights = [ada_w, ada_b, ffn1_norm, ffn1_w_gate, ffn1_w_up, ffn1_w_down, mix_norm, w_in, pool_w, pool_scale,
               q_a_norm, w_q_b, kv_a_norm, w_kv_b, w_out, ffn2_norm, ffn2_w_gate, ffn2_w_up, ffn2_w_down, final_norm]
    ms = [m_ada_w, m_ada_b, m_ffn1_norm, m_ffn1_w_gate, m_ffn1_w_up, m_ffn1_w_down, m_mix_norm, m_w_in, m_pool_w,
          m_pool_scale, m_q_a_norm, m_w_q_b, m_kv_a_norm, m_w_kv_b, m_w_out, m_ffn2_norm, m_ffn2_w_gate, m_ffn2_w_up,
          m_ffn2_w_down, m_final_norm]
    vs = [v_ada_w, v_ada_b, v_ffn1_norm, v_ffn1_w_gate, v_ffn1_w_up, v_ffn1_w_down, v_mix_norm, v_w_in, v_pool_w,
          v_pool_scale, v_q_a_norm, v_w_q_b, v_kv_a_norm, v_w_kv_b, v_w_out, v_ffn2_norm, v_ffn2_w_gate, v_ffn2_w_up,
          v_ffn2_w_down, v_final_norm]
    grads = [g.reshape(w.shape) for g, w in zip(grads, weights)]
    steps = [adamw(w, g, m, v) for w, g, m, v in zip(weights, grads, ms, vs)]
    return (loss, dx.reshape(x.shape), *grads, *[t[0] for t in steps], *[t[1] for t in steps], *[t[2] for t in steps])
```

```python
import math

import jax
import jax.numpy as jnp
from jax import lax
from jax.experimental import pallas as pl
from jax.experimental.pallas import tpu as pltpu

F32 = jnp.float32
BF16 = jnp.bfloat16
MESH = pl.DeviceIdType.MESH

EPS = 1e-6
ROPE_THETA = 10000.0
N_HEADS = 4
QK_NOPE = 128
QK_ROPE = 64
V_HEAD = 128
POOL_WINDOWS = (2, 4, 8, 16)
POOL_GC = 128
POOL_WIDTH = POOL_GC * len(POOL_WINDOWS)
Q_LORA = 384
KV_LORA = 256
SOFTMAX_SCALE = 1.0 / math.sqrt(QK_NOPE + QK_ROPE)
N_CHIPS = 4
N_DEV = 8

ADAM_LR = 0.001
ADAM_B1 = 0.9
ADAM_B2 = 0.999
ADAM_EPS = 1e-08
ADAM_WD = 0.01
ADAM_STEP = 10

ROW_TILE = 512
ATT_TILE = 256
VMEM_LIMIT = 56 * 1024 * 1024
BF16_ROWS = 16
LANES = 128


def _params(sem=None, vmem=VMEM_LIMIT):
    return pltpu.CompilerParams(dimension_semantics=sem, vmem_limit_bytes=vmem)


def _dot(a, b):
    return jnp.dot(a, b, preferred_element_type=F32)


def _dot_nt(a, b):
    return lax.dot_general(a, b, (((1,), (1,)), ((), ())), preferred_element_type=F32)


def _dot_tn(a, b):
    return lax.dot_general(a, b, (((0,), (0,)), ((), ())), preferred_element_type=F32)


def _dot_exact(t, perm):
    t1 = t.astype(BF16)
    r1 = t - t1.astype(F32)
    t2 = r1.astype(BF16)
    t3 = (r1 - t2.astype(F32)).astype(BF16)
    return _dot(t1, perm) + _dot(t2, perm) + _dot(t3, perm)


def _sum0(a):
    return jnp.sum(a, axis=0, keepdims=True)


def _rms(xt):
    r = lax.rsqrt(jnp.mean(xt * xt, axis=-1, keepdims=True) + EPS)
    return xt * r, r


def _rms_bwd(dy, xt, g):
    xhat, r = _rms(xt)
    dxhat = dy * g
    dx = r * (dxhat - xhat * jnp.mean(dxhat * xhat, axis=-1, keepdims=True))
    return dx, _sum0(dy * xhat)


def _normmod_bwd(dh, xt, gn, sc):
    xhat, _ = _rms(xt)
    dn = dh * (1.0 + sc)
    dx, dgn = _rms_bwd(dn, xt, gn)
    return dx, _sum0(dh), _sum0(dh * (xhat * gn)), dgn


def _row_tile(s):
    return min(s, ROW_TILE)


def _full(shape):
    n = len(shape)
    return pl.BlockSpec(shape, lambda *_: (0,) * n)


def _resident(shape):
    n = len(shape)
    return pl.BlockSpec(shape, lambda *_: (0,) * n, pipeline_mode=pl.Buffered(1))


def ffn_fwd(x, gn, sh, sc, gt, wg, wu, wd):
    s, d = x.shape
    k_chunks, fs, _ = wg.shape
    tm = _row_tile(s)

    def body(x_ref, gn_ref, sh_ref, sc_ref, gt_ref, wg_ref, wu_ref, wd_ref,
             xo_ref, h_ref, gate_ref, up_ref, y_ref):
        xt = x_ref[...]
        xhat, _ = _rms(xt)
        h = (xhat * gn_ref[...] * (1.0 + sc_ref[...]) + sh_ref[...]).astype(BF16)
        h_ref[...] = h
        y = jnp.zeros((tm, d), F32)
        for k in range(k_chunks):
            gate = _dot_nt(h, wg_ref[k])
            up = _dot_nt(h, wu_ref[k])
            gate_ref[k] = gate.astype(BF16)
            up_ref[k] = up.astype(BF16)
            y += _dot((gate * jax.nn.sigmoid(gate) * up).astype(BF16), wd_ref[k])
        y_ref[...] = y.astype(BF16)
        xo_ref[...] = xt + 0.5 * gt_ref[...] * y

    row = pl.BlockSpec((tm, d), lambda i: (i, 0))
    vec = pl.BlockSpec((1, d), lambda i: (0, 0))
    act = pl.BlockSpec((k_chunks, tm, fs), lambda i: (0, i, 0))
    return pl.pallas_call(
        body, name="ffn_fwd",
        grid=(s // tm,),
        in_specs=[row, vec, vec, vec, vec, _resident(wg.shape), _resident(wu.shape), _resident(wd.shape)],
        out_specs=[row, row, act, act, row],
        out_shape=[jax.ShapeDtypeStruct((s, d), F32), jax.ShapeDtypeStruct((s, d), BF16),
                   jax.ShapeDtypeStruct((k_chunks, s, fs), BF16), jax.ShapeDtypeStruct((k_chunks, s, fs), BF16),
                   jax.ShapeDtypeStruct((s, d), BF16)],
        compiler_params=_params(("arbitrary",)),
    )(x, gn, sh, sc, gt, wg, wu, wd)


def ffn_bwd_act(dxn, gate, up, gt, wd):
    s, d = dxn.shape
    k_chunks, fs, _ = wd.shape
    tm = _row_tile(s)

    def body(dxn_ref, gate_ref, up_ref, gt_ref, wd_ref, dy_ref, a_ref, dgate_ref, dup_ref):
        dy = (0.5 * gt_ref[...] * dxn_ref[...]).astype(BF16)
        dy_ref[...] = dy
        for k in range(k_chunks):
            da = _dot_nt(dy, wd_ref[k])
            g = gate_ref[k].astype(F32)
            u = up_ref[k].astype(F32)
            sg = jax.nn.sigmoid(g)
            sl = g * sg
            a_ref[k] = (sl * u).astype(BF16)
            dgate_ref[k] = (da * u * (sg * (1.0 + g * (1.0 - sg)))).astype(BF16)
            dup_ref[k] = (da * sl).astype(BF16)

    row = pl.BlockSpec((tm, d), lambda i: (i, 0))
    act = pl.BlockSpec((k_chunks, tm, fs), lambda i: (0, i, 0))
    act_shape = jax.ShapeDtypeStruct((k_chunks, s, fs), BF16)
    return pl.pallas_call(
        body, name="ffn_bwd_act",
        grid=(s // tm,),
        in_specs=[row, act, act, pl.BlockSpec((1, d), lambda i: (0, 0)), _resident(wd.shape)],
        out_specs=[row, act, act, act],
        out_shape=[jax.ShapeDtypeStruct((s, d), BF16), act_shape, act_shape, act_shape],
        compiler_params=_params(("arbitrary",)),
    )(dxn, gate, up, gt, wd)


def ffn_bwd_in(dxn, x, y, dgate, dup, gn, sc, wg, wu):
    s, d = x.shape
    k_chunks, fs, _ = wg.shape
    tm = _row_tile(s)

    def body(dxn_ref, x_ref, y_ref, dgate_ref, dup_ref, gn_ref, sc_ref, wg_ref, wu_ref, dx_ref, dvec_ref):
        i = pl.program_id(0)

        @pl.when(i == 0)
        def _():
            dvec_ref[...] = jnp.zeros_like(dvec_ref)

        dh = jnp.zeros((tm, d), F32)
        for k in range(k_chunks):
            dh += _dot(dgate_ref[k], wg_ref[k]) + _dot(dup_ref[k], wu_ref[k])
        dxn_t = dxn_ref[...]
        dx, dsh, dsc, dgn = _normmod_bwd(dh, x_ref[...], gn_ref[...], sc_ref[...])
        dx_ref[...] = dx + dxn_t
        dvec_ref[0:1, :] += dsh
        dvec_ref[1:2, :] += dsc
        dvec_ref[2:3, :] += _sum0(0.5 * dxn_t * y_ref[...].astype(F32))
        dvec_ref[3:4, :] += dgn

    row = pl.BlockSpec((tm, d), lambda i: (i, 0))
    vec = pl.BlockSpec((1, d), lambda i: (0, 0))
    act = pl.BlockSpec((k_chunks, tm, fs), lambda i: (0, i, 0))
    return pl.pallas_call(
        body, name="ffn_bwd_in",
        grid=(s // tm,),
        in_specs=[row, row, row, act, act, vec, vec, _resident(wg.shape), _resident(wu.shape)],
        out_specs=[row, pl.BlockSpec((8, d), lambda i: (0, 0))],
        out_shape=[jax.ShapeDtypeStruct((s, d), F32), jax.ShapeDtypeStruct((8, d), F32)],
        compiler_params=_params(("arbitrary",)),
    )(dxn, x, y, dgate, dup, gn, sc, wg, wu)


def tn_mm(a, b):
    ga, s, m = a.shape
    gb, _, n = b.shape
    g = max(ga, gb)

    def body(a_ref, b_ref, o_ref):
        o_ref[...] = _dot_tn(a_ref[...], b_ref[...])

    a_spec = pl.BlockSpec((None, s, m), (lambda gi: (gi, 0, 0)) if ga > 1 else (lambda gi: (0, 0, 0)))
    b_spec = pl.BlockSpec((None, s, n), (lambda gi: (gi, 0, 0)) if gb > 1 else (lambda gi: (0, 0, 0)))
    return pl.pallas_call(
        body, name="tn_mm",
        grid=(g,), in_specs=[a_spec, b_spec], out_specs=pl.BlockSpec((None, m, n), lambda gi: (gi, 0, 0)),
        out_shape=jax.ShapeDtypeStruct((g, m, n), F32),
        compiler_params=_params(("arbitrary",)),
    )(a, b)


def mix_in_fwd(x, gn, sh, sc, w_in_t):
    s, d = x.shape
    tm = _row_tile(s)
    o1, o2, o3 = POOL_WIDTH, POOL_WIDTH + Q_LORA, POOL_WIDTH + Q_LORA + KV_LORA

    def body(x_ref, gn_ref, sh_ref, sc_ref, w_ref, h_ref, u_ref, cq_ref, ckv_ref, kr_ref):
        xhat, _ = _rms(x_ref[...])
        h = (xhat * gn_ref[...] * (1.0 + sc_ref[...]) + sh_ref[...]).astype(BF16)
        h_ref[...] = h
        z = _dot_nt(h, w_ref[0:o3, :])
        u_ref[...] = z[:, 0:o1]
        cq_ref[...] = z[:, o1:o2]
        ckv_ref[...] = z[:, o2:o3]
        kr_ref[...] = _dot_nt(h, w_ref[o3:, :])

    row = lambda w: pl.BlockSpec((tm, w), lambda i: (i, 0))
    vec = pl.BlockSpec((1, d), lambda i: (0, 0))
    return pl.pallas_call(
        body, name="mix_in_fwd",
        grid=(s // tm,),
        in_specs=[row(d), vec, vec, vec, _full(w_in_t.shape)],
        out_specs=[row(d), row(POOL_WIDTH), row(Q_LORA), row(KV_LORA), row(QK_ROPE)],
        out_shape=[jax.ShapeDtypeStruct((s, d), BF16), jax.ShapeDtypeStruct((s, POOL_WIDTH), F32),
                   jax.ShapeDtypeStruct((s, Q_LORA), F32), jax.ShapeDtypeStruct((s, KV_LORA), F32),
                   jax.ShapeDtypeStruct((s, QK_ROPE), F32)],
        compiler_params=_params(("arbitrary",)),
    )(x, gn, sh, sc, w_in_t)


def mix_in_bwd(dxn, du, dcq, dckv, dkr, x, gn, sc, w_in_t):
    s, d = x.shape
    tm = _row_tile(s)
    o1, o2, o3 = POOL_WIDTH, POOL_WIDTH + Q_LORA, POOL_WIDTH + Q_LORA + KV_LORA
    n_z = w_in_t.shape[0]

    def body(dxn_ref, du_ref, dcq_ref, dckv_ref, dkr_ref, x_ref, gn_ref, sc_ref, w_ref, dx_ref, dz_ref, dvec_ref):
        i = pl.program_id(0)

        @pl.when(i == 0)
        def _():
            dvec_ref[...] = jnp.zeros_like(dvec_ref)

        dub = du_ref[...].astype(BF16)
        dqb = dcq_ref[...].astype(BF16)
        dkb = dckv_ref[...].astype(BF16)
        drb = dkr_ref[...].astype(BF16)
        dz_ref[:, 0:o1] = dub
        dz_ref[:, o1:o2] = dqb
        dz_ref[:, o2:o3] = dkb
        dz_ref[:, o3:] = drb
        dh = (_dot(dub, w_ref[0:o1, :]) + _dot(dqb, w_ref[o1:o2, :]) + _dot(dkb, w_ref[o2:o3, :])
              + _dot(drb, w_ref[o3:, :]))
        dx, dsh, dsc, dgn = _normmod_bwd(dh, x_ref[...], gn_ref[...], sc_ref[...])
        dx_ref[...] = dx + dxn_ref[...]
        dvec_ref[0:1, :] += dsh
        dvec_ref[1:2, :] += dsc
        dvec_ref[3:4, :] += dgn

    row = lambda w: pl.BlockSpec((tm, w), lambda i: (i, 0))
    vec = pl.BlockSpec((1, d), lambda i: (0, 0))
    return pl.pallas_call(
        body, name="mix_in_bwd",
        grid=(s // tm,),
        in_specs=[row(d), row(POOL_WIDTH), row(Q_LORA), row(KV_LORA), row(QK_ROPE), row(d), vec, vec,
                  _full(w_in_t.shape)],
        out_specs=[row(d), row(n_z), pl.BlockSpec((8, d), lambda i: (0, 0))],
        out_shape=[jax.ShapeDtypeStruct((s, d), F32), jax.ShapeDtypeStruct((s, n_z), BF16),
                   jax.ShapeDtypeStruct((8, d), F32)],
        compiler_params=_params(("arbitrary",)),
    )(dxn, du, dcq, dckv, dkr, x, gn, sc, w_in_t)


def _window_sum(a, w, rows, forward):
    s = a.shape[0]
    step = 1
    while step < w:
        if forward:
            shifted = jnp.where(rows < s - step, pltpu.roll(a, s - step, 0), 0.0)
        else:
            shifted = jnp.where(rows >= step, pltpu.roll(a, step, 0), 0.0)
        a = a + shifted
        step *= 2
    return a


def pool_fwd(u, pool_w, pool_scale):
    s = u.shape[0]

    def body(u_ref, w_ref, sc_ref, y_ref, diff_ref):
        rows = lax.broadcasted_iota(jnp.int32, (s, POOL_GC), 0)
        for g, w in enumerate(POOL_WINDOWS):
            cols = slice(g * POOL_GC, (g + 1) * POOL_GC)
            ug = u_ref[:, cols]
            cnt = jnp.minimum(rows + 1, w).astype(F32)
            diff = (_window_sum(ug, w, rows, False) / cnt - ug).astype(BF16)
            diff_ref[:, cols] = diff
            y_ref[:, cols] = _dot(diff, w_ref[g].astype(BF16)) * sc_ref[:, cols]

    return pl.pallas_call(
        body, name="pool_fwd",
        out_shape=[jax.ShapeDtypeStruct(u.shape, F32), jax.ShapeDtypeStruct(u.shape, BF16)],
        compiler_params=_params(),
    )(u, pool_w, pool_scale)


def pool_bwd(dy, diff, pool_w, pool_scale):
    s = dy.shape[0]

    def body(dy_ref, diff_ref, w_ref, sc_ref, du_ref, dw_ref, dsc_ref):
        rows = lax.broadcasted_iota(jnp.int32, (s, POOL_GC), 0)
        for g, w in enumerate(POOL_WINDOWS):
            cols = slice(g * POOL_GC, (g + 1) * POOL_GC)
            dyg = dy_ref[:, cols]
            diff = diff_ref[:, cols]
            wb = w_ref[g].astype(BF16)
            dsc_ref[:, cols] = _sum0(dyg * _dot(diff, wb))
            dys = (dyg * sc_ref[:, cols]).astype(BF16)
            dw_ref[g] = _dot_tn(diff, dys)
            ddiff = _dot_nt(dys, wb)
            cnt = jnp.minimum(rows + 1, w).astype(F32)
            du_ref[:, cols] = _window_sum(ddiff / cnt, w, rows, True) - ddiff

    return pl.pallas_call(
        body, name="pool_bwd",
        out_shape=[jax.ShapeDtypeStruct(dy.shape, F32), jax.ShapeDtypeStruct(pool_w.shape, F32),
                   jax.ShapeDtypeStruct(pool_scale.shape, F32)],
        compiler_params=_params(),
    )(dy, diff, pool_w, pool_scale)


def mla_qkv_fwd(cq, ckv, kr, qan, kvan, wq, wkv, cos, sin, rot):
    s = cq.shape[0]
    tm = _row_tile(s)

    def body(cq_ref, ckv_ref, kr_ref, qan_ref, kvan_ref, wq_ref, wkv_ref, cos_ref, sin_ref, rot_ref,
             qn_ref, qr_ref, kn_ref, krr_ref, v_ref, ql_ref, kvl_ref):
        cos_t = cos_ref[...]
        sin_t = sin_ref[...]
        perm = rot_ref[...]

        def rope(t):
            return t * cos_t + _dot_exact(t, perm) * sin_t

        qhat, _ = _rms(cq_ref[...])
        ql = (qhat * qan_ref[...]).astype(BF16)
        ql_ref[...] = ql
        khat, _ = _rms(ckv_ref[...])
        kvl = (khat * kvan_ref[...]).astype(BF16)
        kvl_ref[...] = kvl
        krr_ref[...] = rope(kr_ref[...]).astype(BF16)
        for h in range(N_HEADS):
            q = _dot_nt(ql, wq_ref[h])
            qn_ref[h] = q[:, 0:QK_NOPE].astype(BF16)
            qr_ref[h] = rope(q[:, QK_NOPE:]).astype(BF16)
            kv = _dot(kvl, wkv_ref[h])
            kn_ref[h] = kv[:, 0:QK_NOPE].astype(BF16)
            v_ref[h] = kv[:, QK_NOPE:].astype(BF16)

    row = lambda w: pl.BlockSpec((tm, w), lambda i: (i, 0))
    hrow = lambda w: pl.BlockSpec((N_HEADS, tm, w), lambda i: (0, i, 0))
    return pl.pallas_call(
        body, name="mla_qkv_fwd",
        grid=(s // tm,),
        in_specs=[row(Q_LORA), row(KV_LORA), row(QK_ROPE), _full(qan.shape), _full(kvan.shape),
                  _full(wq.shape), _full(wkv.shape), row(QK_ROPE), row(QK_ROPE), _full(rot.shape)],
        out_specs=[hrow(QK_NOPE), hrow(QK_ROPE), hrow(QK_NOPE), row(QK_ROPE), hrow(V_HEAD), row(Q_LORA), row(KV_LORA)],
        out_shape=[jax.ShapeDtypeStruct((N_HEADS, s, QK_NOPE), BF16), jax.ShapeDtypeStruct((N_HEADS, s, QK_ROPE), BF16),
                   jax.ShapeDtypeStruct((N_HEADS, s, QK_NOPE), BF16), jax.ShapeDtypeStruct((s, QK_ROPE), BF16),
                   jax.ShapeDtypeStruct((N_HEADS, s, V_HEAD), BF16), jax.ShapeDtypeStruct((s, Q_LORA), BF16),
                   jax.ShapeDtypeStruct((s, KV_LORA), BF16)],
        compiler_params=_params(("arbitrary",)),
    )(cq, ckv, kr, qan, kvan, wq, wkv, cos, sin, rot)


def _attn_probs(qn_ref, qr_ref, kn_ref, kr_ref, qi, tq):
    n = (qi + 1) * tq
    rows = slice(qi * tq, n)
    sc = (_dot_nt(qn_ref[rows, :], kn_ref[0:n, :]) + _dot_nt(qr_ref[rows, :], kr_ref[0:n, :])) * SOFTMAX_SCALE
    qpos = qi * tq + lax.broadcasted_iota(jnp.int32, (tq, n), 0)
    kpos = lax.broadcasted_iota(jnp.int32, (tq, n), 1)
    sc = jnp.where(qpos >= kpos, sc, -1e30)
    e = jnp.exp(sc - jnp.max(sc, axis=-1, keepdims=True))
    return e / jnp.sum(e, axis=-1, keepdims=True)


def attn_fwd(qn, qr, kn, krr, v):
    nh, s, _ = qn.shape
    tq = min(s, ATT_TILE)

    def body(qn_ref, qr_ref, kn_ref, kr_ref, v_ref, o_ref):
        for qi in range(s // tq):
            n = (qi + 1) * tq
            p = _attn_probs(qn_ref, qr_ref, kn_ref, kr_ref, qi, tq).astype(BF16)
            o_ref[qi * tq:n, :] = _dot(p, v_ref[0:n, :])

    head = lambda w: pl.BlockSpec((None, s, w), lambda h: (h, 0, 0))
    return pl.pallas_call(
        body, name="attn_fwd",
        grid=(nh,),
        in_specs=[head(QK_NOPE), head(QK_ROPE), head(QK_NOPE), _full(krr.shape), head(V_HEAD)],
        out_specs=pl.BlockSpec((s, V_HEAD), lambda h: (0, h)),
        out_shape=jax.ShapeDtypeStruct((s, nh * V_HEAD), F32),
        compiler_params=_params(("arbitrary",)),
    )(qn, qr, kn, krr, v)


def attn_bwd(qn, qr, kn, krr, v, do):
    nh, s, _ = qn.shape
    tq = min(s, ATT_TILE)

    def body(qn_ref, qr_ref, kn_ref, kr_ref, v_ref, do_ref, dqn_ref, dqr_ref, dkn_ref, dkr_ref, dv_ref):
        dkn_ref[...] = jnp.zeros_like(dkn_ref)
        dkr_ref[...] = jnp.zeros_like(dkr_ref)
        dv_ref[...] = jnp.zeros_like(dv_ref)
        for qi in range(s // tq):
            n = (qi + 1) * tq
            rows = slice(qi * tq, n)
            p = _attn_probs(qn_ref, qr_ref, kn_ref, kr_ref, qi, tq)
            dob = do_ref[rows, :].astype(BF16)
            dp = _dot_nt(dob, v_ref[0:n, :])
            ds = (p * (dp - jnp.sum(p * dp, axis=-1, keepdims=True)) * SOFTMAX_SCALE).astype(BF16)
            dqn_ref[rows, :] = _dot(ds, kn_ref[0:n, :])
            dqr_ref[rows, :] = _dot(ds, kr_ref[0:n, :])
            dkn_ref[0:n, :] += _dot_tn(ds, qn_ref[rows, :])
            dkr_ref[0:n, :] += _dot_tn(ds, qr_ref[rows, :])
            dv_ref[0:n, :] += _dot_tn(p.astype(BF16), dob)

    head = lambda w: pl.BlockSpec((None, s, w), lambda h: (h, 0, 0))
    return pl.pallas_call(
        body, name="attn_bwd",
        grid=(nh,),
        in_specs=[head(QK_NOPE), head(QK_ROPE), head(QK_NOPE), _full(krr.shape), head(V_HEAD),
                  pl.BlockSpec((s, V_HEAD), lambda h: (0, h))],
        out_specs=[head(QK_NOPE), head(QK_ROPE), head(QK_NOPE), head(QK_ROPE), head(V_HEAD)],
        out_shape=[jax.ShapeDtypeStruct((nh, s, QK_NOPE), F32), jax.ShapeDtypeStruct((nh, s, QK_ROPE), F32),
                   jax.ShapeDtypeStruct((nh, s, QK_NOPE), F32), jax.ShapeDtypeStruct((nh, s, QK_ROPE), F32),
                   jax.ShapeDtypeStruct((nh, s, V_HEAD), F32)],
        compiler_params=_params(("arbitrary",)),
    )(qn, qr, kn, krr, v, do)


def mla_qkv_bwd(dqn, dqr, dkn, dkr, dv, cq, ckv, qan, kvan, wq, wkv, cos, sin, rot_t):
    s = cq.shape[0]
    tm = _row_tile(s)

    def body(dqn_ref, dqr_ref, dkn_ref, dkr_ref, dv_ref, cq_ref, ckv_ref, qan_ref, kvan_ref,
             wq_ref, wkv_ref, cos_ref, sin_ref, rot_ref,
             dcq_ref, dckv_ref, dkro_ref, gq_ref, gkv_ref, dqan_ref, dkvan_ref):
        i = pl.program_id(0)

        @pl.when(i == 0)
        def _():
            dqan_ref[...] = jnp.zeros_like(dqan_ref)
            dkvan_ref[...] = jnp.zeros_like(dkvan_ref)

        cos_t = cos_ref[...]
        sin_t = sin_ref[...]
        perm_t = rot_ref[...]

        def unrope(t):
            return t * cos_t + _dot_exact(t * sin_t, perm_t)

        acc_q = jnp.zeros((tm, Q_LORA), F32)
        acc_kv = jnp.zeros((tm, KV_LORA), F32)
        dkr_sum = jnp.zeros((tm, QK_ROPE), F32)
        for h in range(N_HEADS):
            a = dqn_ref[h].astype(BF16)
            b = unrope(dqr_ref[h]).astype(BF16)
            gq_ref[h, :, 0:QK_NOPE] = a
            gq_ref[h, :, QK_NOPE:] = b
            wq_h = wq_ref[h]
            acc_q += _dot(a, wq_h[0:QK_NOPE, :]) + _dot(b, wq_h[QK_NOPE:, :])
            dk = dkn_ref[h].astype(BF16)
            dvv = dv_ref[h].astype(BF16)
            gkv_ref[h, :, 0:QK_NOPE] = dk
            gkv_ref[h, :, QK_NOPE:] = dvv
            wkv_h = wkv_ref[h]
            acc_kv += _dot_nt(dk, wkv_h[:, 0:QK_NOPE]) + _dot_nt(dvv, wkv_h[:, QK_NOPE:])
            dkr_sum += dkr_ref[h]
        dkro_ref[...] = unrope(dkr_sum)
        dcq, dqan = _rms_bwd(acc_q, cq_ref[...], qan_ref[...])
        dcq_ref[...] = dcq
        dqan_ref[...] += dqan
        dckv, dkvan = _rms_bwd(acc_kv, ckv_ref[...], kvan_ref[...])
        dckv_ref[...] = dckv
        dkvan_ref[...] += dkvan

    row = lambda w: pl.BlockSpec((tm, w), lambda i: (i, 0))
    hrow = lambda w: pl.BlockSpec((N_HEADS, tm, w), lambda i: (0, i, 0))
    return pl.pallas_call(
        body, name="mla_qkv_bwd",
        grid=(s // tm,),
        in_specs=[hrow(QK_NOPE), hrow(QK_ROPE), hrow(QK_NOPE), hrow(QK_ROPE), hrow(V_HEAD),
                  row(Q_LORA), row(KV_LORA), _full(qan.shape), _full(kvan.shape),
                  _full(wq.shape), _full(wkv.shape), row(QK_ROPE), row(QK_ROPE), _full(rot_t.shape)],
        out_specs=[row(Q_LORA), row(KV_LORA), row(QK_ROPE), hrow(QK_NOPE + QK_ROPE), hrow(QK_NOPE + V_HEAD),
                   _full(qan.shape), _full(kvan.shape)],
        out_shape=[jax.ShapeDtypeStruct((s, Q_LORA), F32), jax.ShapeDtypeStruct((s, KV_LORA), F32),
                   jax.ShapeDtypeStruct((s, QK_ROPE), F32),
                   jax.ShapeDtypeStruct((N_HEADS, s, QK_NOPE + QK_ROPE), BF16),
                   jax.ShapeDtypeStruct((N_HEADS, s, QK_NOPE + V_HEAD), BF16),
                   jax.ShapeDtypeStruct(qan.shape, F32), jax.ShapeDtypeStruct(kvan.shape, F32)],
        compiler_params=_params(("arbitrary",)),
    )(dqn, dqr, dkn, dkr, dv, cq, ckv, qan, kvan, wq, wkv, cos, sin, rot_t)


def out_proj_fwd(yp, om, w_out, x, gt):
    s, d = x.shape
    n_sh, rs, _ = w_out.shape
    tm = _row_tile(s)
    per = POOL_WIDTH // rs

    def body(yp_ref, om_ref, w_ref, x_ref, gt_ref, xo_ref, ycat_ref, y_ref):
        y = jnp.zeros((tm, d), F32)
        for j in range(n_sh):
            src = yp_ref if j < per else om_ref
            part = src[:, (j % per) * rs:(j % per + 1) * rs].astype(BF16)
            ycat_ref[j] = part
            y += _dot(part, w_ref[j])
        y_ref[...] = y.astype(BF16)
        xo_ref[...] = x_ref[...] + gt_ref[...] * y

    row = lambda w: pl.BlockSpec((tm, w), lambda i: (i, 0))
    return pl.pallas_call(
        body, name="out_proj_fwd",
        grid=(s // tm,),
        in_specs=[row(POOL_WIDTH), row(POOL_WIDTH), _full(w_out.shape), row(d), pl.BlockSpec((1, d), lambda i: (0, 0))],
        out_specs=[row(d), pl.BlockSpec((n_sh, tm, rs), lambda i: (0, i, 0)), row(d)],
        out_shape=[jax.ShapeDtypeStruct((s, d), F32), jax.ShapeDtypeStruct((n_sh, s, rs), BF16),
                   jax.ShapeDtypeStruct((s, d), BF16)],
        compiler_params=_params(("arbitrary",)),
    )(yp, om, w_out, x, gt)


def out_proj_bwd(dxn, y, gt, w_out):
    s, d = dxn.shape
    n_sh, rs, _ = w_out.shape
    tm = _row_tile(s)
    per = POOL_WIDTH // rs

    def body(dxn_ref, y_ref, gt_ref, w_ref, dy_ref, dyp_ref, dom_ref, dgt_ref):
        i = pl.program_id(0)

        @pl.when(i == 0)
        def _():
            dgt_ref[...] = jnp.zeros_like(dgt_ref)

        dxn_t = dxn_ref[...]
        dy = (gt_ref[...] * dxn_t).astype(BF16)
        dy_ref[...] = dy
        dgt_ref[...] += _sum0(dxn_t * y_ref[...].astype(F32))
        for j in range(n_sh):
            dst = dyp_ref if j < per else dom_ref
            dst[:, (j % per) * rs:(j % per + 1) * rs] = _dot_nt(dy, w_ref[j])

    row = lambda w: pl.BlockSpec((tm, w), lambda i: (i, 0))
    vec = pl.BlockSpec((1, d), lambda i: (0, 0))
    return pl.pallas_call(
        body, name="out_proj_bwd",
        grid=(s // tm,),
        in_specs=[row(d), row(d), vec, _full(w_out.shape)],
        out_specs=[row(d), row(POOL_WIDTH), row(POOL_WIDTH), vec],
        out_shape=[jax.ShapeDtypeStruct((s, d), BF16), jax.ShapeDtypeStruct((s, POOL_WIDTH), F32),
                   jax.ShapeDtypeStruct((s, POOL_WIDTH), F32), jax.ShapeDtypeStruct((1, d), F32)],
        compiler_params=_params(("arbitrary",)),
    )(dxn, y, gt, w_out)


def final_loss(x, gn, tgt):
    s, d = x.shape
    tm = _row_tile(s)

    def body(x_ref, gn_ref, t_ref, loss_ref, dx_ref, dgn_ref):
        i = pl.program_id(0)

        @pl.when(i == 0)
        def _():
            loss_ref[...] = jnp.zeros_like(loss_ref)
            dgn_ref[...] = jnp.zeros_like(dgn_ref)

        xt = x_ref[...]
        g = gn_ref[...]
        xhat, _ = _rms(xt)
        err = xhat * g - t_ref[...]
        per_tok = jnp.mean(err * err, axis=-1, keepdims=True)
        loss_ref[...] += jnp.broadcast_to(0.5 * _sum0(per_tok), loss_ref.shape)
        dx, dgn = _rms_bwd(err * (1.0 / d), xt, g)
        dx_ref[...] = dx
        dgn_ref[...] += dgn

    row = pl.BlockSpec((tm, d), lambda i: (i, 0))
    vec = pl.BlockSpec((1, d), lambda i: (0, 0))
    return pl.pallas_call(
        body, name="final_loss",
        grid=(s // tm,),
        in_specs=[row, vec, row],
        out_specs=[pl.BlockSpec((1, LANES), lambda i: (0, 0)), row, vec],
        out_shape=[jax.ShapeDtypeStruct((1, LANES), F32), jax.ShapeDtypeStruct((s, d), F32),
                   jax.ShapeDtypeStruct((1, d), F32)],
        compiler_params=_params(("arbitrary",)),
    )(x, gn, tgt)


def _col_tile(cols):
    return 768 if cols % 768 == 0 else cols


def ada_fwd(c16, ada_w, ada_b_loc):
    n_layers, d, cols = ada_w.shape
    tn = _col_tile(cols)

    def body(c_ref, w_ref, b_ref, o_ref):
        cv = c_ref[...]
        ca = (cv * jax.nn.sigmoid(cv)).astype(BF16)
        o_ref[...] = _dot(ca, w_ref[...].astype(BF16)) + b_ref[...]

    return pl.pallas_call(
        body, name="ada_fwd",
        grid=(n_layers, cols // tn),
        in_specs=[pl.BlockSpec((16, d), lambda l, j: (0, 0)), pl.BlockSpec((None, d, tn), lambda l, j: (l, 0, j)),
                  pl.BlockSpec((None, 1, tn), lambda l, j: (l, 0, j))],
        out_specs=pl.BlockSpec((None, 16, tn), lambda l, j: (l, 0, j)),
        out_shape=jax.ShapeDtypeStruct((n_layers, 16, cols), F32),
        compiler_params=_params(("arbitrary", "arbitrary")),
    )(c16, ada_w, ada_b_loc)


def ada_bwd(c16, dmod16):
    n_layers, _, cols = dmod16.shape
    d = c16.shape[1]
    tn = _col_tile(cols)

    def body(c_ref, g_ref, o_ref):
        cv = c_ref[...]
        ca = (cv * jax.nn.sigmoid(cv)).astype(BF16)
        o_ref[...] = _dot_tn(ca, g_ref[...].astype(BF16))

    return pl.pallas_call(
        body, name="ada_bwd",
        grid=(n_layers, cols // tn),
        in_specs=[pl.BlockSpec((16, d), lambda l, j: (0, 0)), pl.BlockSpec((None, 16, tn), lambda l, j: (l, 0, j))],
        out_specs=pl.BlockSpec((None, d, tn), lambda l, j: (l, 0, j)),
        out_shape=jax.ShapeDtypeStruct((n_layers, d, cols), F32),
        compiler_params=_params(("arbitrary", "arbitrary")),
    )(c16, dmod16)


def _as_rows(a):
    if a.ndim == 1:
        return a.reshape(1, a.shape[0])
    return a.reshape(-1, a.shape[-1])


def _rows_tile(r, c, itemsize=4, budget=2 * 1024 * 1024):
    if r * c * itemsize <= budget:
        return r
    best = None
    t = BF16_ROWS
    while t < r:
        if r % t == 0 and t * c * itemsize <= budget:
            best = t
        t += BF16_ROWS
    return best if best is not None else r


def cast_bf16(w):
    w2 = _as_rows(w)
    r, c = w2.shape
    tr = _rows_tile(r, c)

    def body(w_ref, o_ref):
        o_ref[...] = w_ref[...].astype(BF16)

    spec = pl.BlockSpec((tr, c), lambda i: (i, 0))
    out = pl.pallas_call(
        body, name="cast_bf16", grid=(r // tr,), in_specs=[spec], out_specs=spec,
        out_shape=jax.ShapeDtypeStruct((r, c), BF16), compiler_params=_params(("arbitrary",)),
    )(w2)
    return out.reshape(w.shape)


def adamw(w, g, m, v):
    shape = w.shape
    w2, g2, m2, v2 = (_as_rows(t) for t in (w, g, m, v))
    r, c = w2.shape
    tr = _rows_tile(r, c, budget=1024 * 1024)
    c1 = 1.0 - ADAM_B1 ** ADAM_STEP
    c2 = 1.0 - ADAM_B2 ** ADAM_STEP

    def body(w_ref, g_ref, m_ref, v_ref, d_ref, mo_ref, vo_ref):
        gv = g_ref[...]
        mn = ADAM_B1 * m_ref[...] + (1.0 - ADAM_B1) * gv
        vn = ADAM_B2 * v_ref[...] + (1.0 - ADAM_B2) * (gv * gv)
        mo_ref[...] = mn
        vo_ref[...] = vn
        d_ref[...] = -ADAM_LR * ((mn / c1) / (jnp.sqrt(vn / c2) + ADAM_EPS) + ADAM_WD * w_ref[...])

    spec = pl.BlockSpec((tr, c), lambda i: (i, 0))
    outs = pl.pallas_call(
        body, name="adamw", grid=(r // tr,), in_specs=[spec] * 4, out_specs=[spec] * 3,
        out_shape=[jax.ShapeDtypeStruct((r, c), F32)] * 3, compiler_params=_params(("arbitrary",)),
    )(w2, g2, m2, v2)
    return tuple(o.reshape(shape) for o in outs)


def sum_devices(a):
    n, r, c = a.shape
    tr = _rows_tile(r, c, budget=512 * 1024)

    def body(a_ref, o_ref):
        acc = a_ref[0]
        for j in range(1, n):
            acc = acc + a_ref[j]
        o_ref[...] = acc

    return pl.pallas_call(
        body, name="sum_devices", grid=(r // tr,),
        in_specs=[pl.BlockSpec((n, tr, c), lambda i: (0, i, 0))], out_specs=pl.BlockSpec((tr, c), lambda i: (i, 0)),
        out_shape=jax.ShapeDtypeStruct((r, c), F32), compiler_params=_params(("arbitrary",)),
    )(a)


def _split_axis(r, c):
    if (r // 2) % BF16_ROWS == 0 and r % 2 == 0:
        return 0
    assert c % (2 * LANES) == 0, (r, c)
    return 1


def _half_shape(r, c):
    return (r // 2, c) if _split_axis(r, c) == 0 else (r, c // 2)


def _half_at(ref, lead, which):
    r, c = ref.shape[-2:]
    if _split_axis(r, c) == 0:
        return ref.at[(*lead, pl.ds(which * (r // 2), r // 2), slice(None))]
    return ref.at[(*lead, slice(None), pl.ds(which * (c // 2), c // 2))]


def _half_spec(r, c, lead_block, imap):
    hr, hc = _half_shape(r, c)
    if _split_axis(r, c) == 0:
        return pl.BlockSpec((*lead_block, hr, hc), lambda *a: (*imap(*a)[0], imap(*a)[1], 0))
    return pl.BlockSpec((*lead_block, hr, hc), lambda *a: (*imap(*a)[0], 0, imap(*a)[1]))


def pair_add(g, ra, half):
    n_sl, r, c = g.shape
    hr, hc = _half_shape(r, c)

    def body(h_ref, g_ref, ra_ref, p_ref, pb_ref):
        p = g_ref[...] + ra_ref[...]
        p_ref[...] = p
        pb_ref[...] = p.astype(BF16)

    mine = pl.BlockSpec((None, hr, hc), lambda k, h: (k, 0, 0))
    return pl.pallas_call(
        body, name="pair_add",
        grid_spec=pltpu.PrefetchScalarGridSpec(
            num_scalar_prefetch=1, grid=(n_sl,),
            in_specs=[_half_spec(r, c, (None,), lambda k, h: ((k,), h[0])), mine], out_specs=[mine, mine]),
        out_shape=[jax.ShapeDtypeStruct((n_sl, hr, hc), F32), jax.ShapeDtypeStruct((n_sl, hr, hc), BF16)],
        compiler_params=_params(("arbitrary",)),
    )(half, g, ra)


def chip_sum(p32, rb, sel, shape, acc):
    n_layers, r, c = shape
    hr, hc = _half_shape(r, c)

    def body(s_ref, p_ref, rb_ref, *rest):
        o_ref = rest[-1]
        acc_v = p_ref[...]
        for j in range(N_CHIPS - 1):
            acc_v = acc_v + rb_ref[j].astype(F32)
        o_ref[...] = acc_v

    in_specs = [pl.BlockSpec((None, hr, hc), lambda i, sr: (sr[1], 0, 0)),
                pl.BlockSpec((N_CHIPS - 1, hr, hc), lambda i, sr: (0, 0, 0))]
    args = [sel, p32, rb]
    aliases = {}
    if acc is not None:
        in_specs.append(pl.BlockSpec(memory_space=pl.ANY))
        args.append(acc)
        aliases = {3: 0}
    return pl.pallas_call(
        body, name="chip_sum",
        grid_spec=pltpu.PrefetchScalarGridSpec(
            num_scalar_prefetch=1, grid=(1,), in_specs=in_specs,
            out_specs=_half_spec(r, c, (None,), lambda i, sr: ((sr[2],), sr[0]))),
        out_shape=jax.ShapeDtypeStruct((n_layers, r, c), F32),
        input_output_aliases=aliases,
        compiler_params=_params(("arbitrary",)),
    )(*args)


def _me():
    return lax.axis_index("x"), lax.axis_index("y"), lax.axis_index("c")


def _flip(v, bit):
    return 1 - v if bit else v


def exchange8(xs, bcast):
    blk = xs.shape if bcast else xs.shape[1:]

    def body(x_ref, o_ref, send_sems, recv_sems, loc_sem):
        mx, my, mc = _me()
        me = 4 * mx + 2 * my + mc
        src = (lambda j: x_ref) if bcast else (lambda j: x_ref.at[j])
        loc = pltpu.make_async_copy(src(me), o_ref.at[me], loc_sem)
        loc.start()
        copies = []
        for o in range(1, N_DEV):
            px, py, pc = _flip(mx, o & 4), _flip(my, o & 2), _flip(mc, o & 1)
            cp = pltpu.make_async_remote_copy(
                src_ref=src(4 * px + 2 * py + pc), dst_ref=o_ref.at[me],
                send_sem=send_sems.at[o - 1], recv_sem=recv_sems.at[o - 1],
                device_id=(px, py, pc), device_id_type=MESH)
            cp.start()
            copies.append(cp)
        for cp in copies:
            cp.wait()
        loc.wait()

    return pl.pallas_call(
        body, name="exchange8_gather" if bcast else "exchange8_a2a",
        in_specs=[pl.BlockSpec(memory_space=pltpu.VMEM)], out_specs=pl.BlockSpec(memory_space=pltpu.VMEM),
        out_shape=jax.ShapeDtypeStruct((N_DEV,) + tuple(blk), xs.dtype),
        scratch_shapes=[pltpu.SemaphoreType.DMA((N_DEV - 1,)), pltpu.SemaphoreType.DMA((N_DEV - 1,)), pltpu.SemaphoreType.DMA],
        compiler_params=_params(),
    )(xs)


def weight_gather(shards, layer):
    n = len(shards)

    def body(*refs):
        src = refs[:n]
        dst = refs[n:2 * n]
        ici_send, ici_recv, d2d_send, d2d_recv, loc_sem = refs[2 * n:]
        mx, my, mc = _me()
        chip = 2 * mx + my
        locs = []
        for t in range(n):
            cp = pltpu.make_async_copy(src[t].at[layer], dst[t].at[chip], loc_sem.at[t])
            cp.start()
            locs.append(cp)
        sends = []
        for t in range(n):
            for o in range(1, N_CHIPS):
                px, py = _flip(mx, o & 2), _flip(my, o & 1)
                cp = pltpu.make_async_remote_copy(
                    src_ref=_half_at(src[t], (layer,), mc), dst_ref=_half_at(dst[t], (chip,), mc),
                    send_sem=ici_send.at[t, o - 1], recv_sem=ici_recv.at[t, o - 1],
                    device_id=(px, py, mc), device_id_type=MESH)
                cp.start()
                sends.append(cp)
        fwds = []
        for t in range(n):
            for o in range(1, N_CHIPS):
                px, py = _flip(mx, o & 2), _flip(my, o & 1)
                got = _half_at(dst[t], (2 * px + py,), mc)
                pltpu.make_async_remote_copy(
                    src_ref=got, dst_ref=got, send_sem=ici_send.at[t, o - 1], recv_sem=ici_recv.at[t, o - 1],
                    device_id=(px, py, mc), device_id_type=MESH).wait_recv()
                cp = pltpu.make_async_remote_copy(
                    src_ref=got, dst_ref=got, send_sem=d2d_send.at[t, o - 1], recv_sem=d2d_recv.at[t, o - 1],
                    device_id=(mx, my, 1 - mc), device_id_type=MESH)
                cp.start()
                fwds.append(cp)
        for cp in fwds:
            cp.wait()
        for cp in sends:
            cp.wait_send()
        for cp in locs:
            cp.wait()

    any_spec = pl.BlockSpec(memory_space=pl.ANY)
    return pl.pallas_call(
        body, name="weight_gather",
        in_specs=[any_spec] * n, out_specs=[any_spec] * n,
        out_shape=[jax.ShapeDtypeStruct((N_CHIPS,) + t.shape[1:], t.dtype) for t in shards],
        scratch_shapes=[pltpu.SemaphoreType.DMA((n, N_CHIPS - 1))] * 4 + [pltpu.SemaphoreType.DMA((n,))],
        compiler_params=_params(),
    )(*shards)


HBM = pl.BlockSpec(memory_space=pltpu.HBM)
SEM = pl.BlockSpec(memory_space=pltpu.SEMAPHORE)
EFFECT = pltpu.SideEffectType.DATAFLOW_SIDE_EFFECTING


def _hbm(a):
    return pltpu.with_memory_space_constraint(a, pltpu.HBM)


def place_local(shards):
    n = len(shards)
    n_layers = shards[0].shape[0]

    def body(*refs):
        src = refs[:n]
        dst = refs[n:n + n * n_layers]
        sems = refs[-1]
        mx, my, _ = _me()
        chip = 2 * mx + my
        copies = []
        for l in range(n_layers):
            for t in range(n):
                cp = pltpu.make_async_copy(src[t].at[l], dst[l * n + t].at[chip], sems.at[l, t])
                cp.start()
                copies.append(cp)
        for cp in copies:
            cp.wait()

    any_spec = pl.BlockSpec(memory_space=pl.ANY)
    outs = pl.pallas_call(
        body, name="place_local",
        in_specs=[any_spec] * n, out_specs=[any_spec] * (n * n_layers),
        out_shape=[jax.ShapeDtypeStruct((N_CHIPS,) + t.shape[1:], t.dtype) for _ in range(n_layers) for t in shards],
        scratch_shapes=[pltpu.SemaphoreType.DMA((n_layers, n))],
        compiler_params=_params(),
    )(*shards)
    return [list(outs[l * n:(l + 1) * n]) for l in range(n_layers)]


def _ici_copy(src, land, layer, o, send_sem, recv_sem, sending):
    mx, my, mc = _me()
    px, py = _flip(mx, o & 2), _flip(my, o & 1)
    slot = 2 * mx + my if sending else 2 * px + py
    return pltpu.make_async_remote_copy(
        src_ref=_half_at(src, (layer,), mc), dst_ref=_half_at(land, (slot,), mc),
        send_sem=send_sem, recv_sem=recv_sem, device_id=(px, py, mc), device_id_type=MESH)


N_PEERS = N_CHIPS - 1
DMA_SEM = pltpu.SemaphoreType.DMA(())


def gather_start(shards, lands):
    n = len(shards)
    n_layers = shards[0].shape[0]
    flat = [a for layer in lands for a in layer]

    def body(*refs):
        src = refs[:n]
        land = refs[n:n + n * n_layers]
        n_in = n + n * n_layers
        sems = refs[n_in:n_in + n_sem]
        token = refs[-1]
        for l in range(n_layers):
            for t in range(n):
                for o in range(1, N_CHIPS):
                    send_sem = sems[(2 * l) * N_PEERS + o - 1]
                    recv_sem = sems[(2 * l + 1) * N_PEERS + o - 1]
                    _ici_copy(src[t], land[l * n + t], l, o, send_sem, recv_sem, True).start()
        token[...] = jnp.zeros_like(token)

    n_sem = 2 * n_layers * N_PEERS
    n_in = n + n * n_layers
    outs = pl.pallas_call(
        body, name="gather_start",
        in_specs=[HBM] * n_in,
        out_specs=[SEM] * n_sem + [HBM] * n_in + [pl.BlockSpec(memory_space=pltpu.VMEM)],
        out_shape=[DMA_SEM] * n_sem + [pltpu.HBM(a.shape, a.dtype) for a in list(shards) + flat]
        + [jax.ShapeDtypeStruct((8, LANES), F32)],
        input_output_aliases={i: i + n_sem for i in range(n_in)},
        compiler_params=pltpu.CompilerParams(has_side_effects=EFFECT),
    )(*[_hbm(a) for a in list(shards) + flat])
    sems = [(list(outs[(2 * l) * N_PEERS:(2 * l + 1) * N_PEERS]), list(outs[(2 * l + 1) * N_PEERS:(2 * l + 2) * N_PEERS]))
            for l in range(n_layers)]
    shards_thru = list(outs[n_sem:n_sem + n])
    lands_thru = [list(outs[n_sem + n + l * n:n_sem + n + (l + 1) * n]) for l in range(n_layers)]
    return sems, shards_thru, lands_thru, outs[-1]


def gather_wait(layer, sems, shards, lands, after):
    n = len(shards)
    send_sems, recv_sems = sems

    def body(*refs):
        src = refs[:n]
        land = refs[n:2 * n]
        send_r = refs[2 * n:2 * n + N_PEERS]
        recv_r = refs[2 * n + N_PEERS:2 * n + 2 * N_PEERS]
        for t in range(n):
            for o in range(1, N_CHIPS):
                _ici_copy(src[t], land[t], layer, o, send_r[o - 1], recv_r[o - 1], True).wait_send()
                _ici_copy(src[t], land[t], layer, o, send_r[o - 1], recv_r[o - 1], False).wait_recv()

    arrs = list(shards) + list(lands)
    outs = pl.pallas_call(
        body, name=f"gather_wait_{layer}",
        in_specs=[HBM] * (2 * n) + [SEM] * (2 * N_PEERS) + [pl.BlockSpec(memory_space=pl.ANY)],
        out_specs=[HBM] * (2 * n),
        out_shape=[pltpu.HBM(a.shape, a.dtype) for a in arrs],
        input_output_aliases={i: i for i in range(2 * n)},
        compiler_params=pltpu.CompilerParams(has_side_effects=EFFECT),
    )(*arrs, *send_sems, *recv_sems, after)
    return list(outs[:n]), list(outs[n:])


def gather_forward(lands):
    n = len(lands)

    def body(*refs):
        dst = refs[n:2 * n]
        send_sems, recv_sems = refs[2 * n:]
        mx, my, mc = _me()
        fwds = []
        for t in range(n):
            for o in range(1, N_CHIPS):
                slot = 2 * _flip(mx, o & 2) + _flip(my, o & 1)
                mine = _half_at(dst[t], (slot,), mc)
                theirs = _half_at(dst[t], (slot,), 1 - mc)
                cp = pltpu.make_async_remote_copy(
                    src_ref=mine, dst_ref=mine, send_sem=send_sems.at[t, o - 1], recv_sem=recv_sems.at[t, o - 1],
                    device_id=(mx, my, 1 - mc), device_id_type=MESH)
                cp.start()
                fwds.append((cp, pltpu.make_async_remote_copy(
                    src_ref=theirs, dst_ref=theirs, send_sem=send_sems.at[t, o - 1], recv_sem=recv_sems.at[t, o - 1],
                    device_id=(mx, my, 1 - mc), device_id_type=MESH)))
        for cp, arrival in fwds:
            cp.wait_send()
            arrival.wait_recv()

    any_spec = pl.BlockSpec(memory_space=pl.ANY)
    return list(pl.pallas_call(
        body, name="gather_forward",
        in_specs=[any_spec] * n, out_specs=[any_spec] * n,
        out_shape=[jax.ShapeDtypeStruct(a.shape, a.dtype) for a in lands],
        input_output_aliases={t: t for t in range(n)},
        scratch_shapes=[pltpu.SemaphoreType.DMA((n, N_CHIPS - 1)), pltpu.SemaphoreType.DMA((n, N_CHIPS - 1))],
        compiler_params=_params(),
    )(*lands))


def _scatter_copy(src, land, o, send_sem, recv_sem):
    mx, my, mc = _me()
    px, py = _flip(mx, o & 2), _flip(my, o & 1)
    return pltpu.make_async_remote_copy(
        src_ref=src.at[2 * px + py], dst_ref=land.at[o - 1],
        send_sem=send_sem, recv_sem=recv_sem, device_id=(px, py, mc), device_id_type=MESH)


def scatter_start(pbs, layer):
    n = len(pbs)
    lands = [lax.empty((N_CHIPS - 1,) + p.shape[1:], p.dtype) for p in pbs]

    def body(*refs):
        src = refs[:n]
        land = refs[n:2 * n]
        send_sems = refs[2 * n:2 * n + N_PEERS]
        recv_sems = refs[2 * n + N_PEERS:2 * n + 2 * N_PEERS]
        token = refs[-1]
        for t in range(n):
            for o in range(1, N_CHIPS):
                _scatter_copy(src[t], land[t], o, send_sems[o - 1], recv_sems[o - 1]).start()
        token[...] = jnp.zeros_like(token)

    n_sem = 2 * N_PEERS
    arrs = list(pbs) + lands
    outs = pl.pallas_call(
        body, name=f"scatter_start_{layer}",
        in_specs=[HBM] * (2 * n),
        out_specs=[SEM] * n_sem + [HBM] * (2 * n) + [pl.BlockSpec(memory_space=pltpu.VMEM)],
        out_shape=[DMA_SEM] * n_sem + [pltpu.HBM(a.shape, a.dtype) for a in arrs]
        + [jax.ShapeDtypeStruct((8, LANES), F32)],
        input_output_aliases={i: i + n_sem for i in range(2 * n)},
        compiler_params=pltpu.CompilerParams(has_side_effects=EFFECT),
    )(*[_hbm(a) for a in arrs])
    return (list(outs[:N_PEERS]), list(outs[N_PEERS:n_sem]), list(outs[n_sem:n_sem + n]),
            list(outs[n_sem + n:n_sem + 2 * n]), outs[-1])


def scatter_wait(layer, send_sems, recv_sems, pbs, lands, after):
    n = len(pbs)

    def body(*refs):
        src = refs[:n]
        land = refs[n:2 * n]
        send_r = refs[2 * n:2 * n + N_PEERS]
        recv_r = refs[2 * n + N_PEERS:2 * n + 2 * N_PEERS]
        for t in range(n):
            for o in range(1, N_CHIPS):
                cp = _scatter_copy(src[t], land[t], o, send_r[o - 1], recv_r[o - 1])
                cp.wait_send()
                cp.wait_recv()

    arrs = list(pbs) + list(lands)
    outs = pl.pallas_call(
        body, name=f"scatter_wait_{layer}",
        in_specs=[HBM] * (2 * n) + [SEM] * (2 * N_PEERS) + [pl.BlockSpec(memory_space=pl.ANY)],
        out_specs=[HBM] * (2 * n),
        out_shape=[pltpu.HBM(a.shape, a.dtype) for a in arrs],
        input_output_aliases={i: i for i in range(2 * n)},
        compiler_params=pltpu.CompilerParams(has_side_effects=EFFECT),
    )(*arrs, *send_sems, *recv_sems, after)
    return list(outs[n:])


def pair_send_halves(gs):
    n = len(gs)

    def body(*refs):
        src = refs[:n]
        dst = refs[n:2 * n]
        send_sems, recv_sems = refs[2 * n:]
        mx, my, mc = _me()
        copies = []
        for t in range(n):
            cp = pltpu.make_async_remote_copy(
                src_ref=_half_at(src[t], (slice(None),), 1 - mc), dst_ref=dst[t],
                send_sem=send_sems.at[t], recv_sem=recv_sems.at[t],
                device_id=(mx, my, 1 - mc), device_id_type=MESH)
            cp.start()
            copies.append(cp)
        for cp in copies:
            cp.wait()

    any_spec = pl.BlockSpec(memory_space=pl.ANY)
    return pl.pallas_call(
        body, name="pair_send_halves",
        in_specs=[any_spec] * n, out_specs=[any_spec] * n,
        out_shape=[jax.ShapeDtypeStruct((g.shape[0],) + _half_shape(*g.shape[1:]), g.dtype) for g in gs],
        scratch_shapes=[pltpu.SemaphoreType.DMA((n,)), pltpu.SemaphoreType.DMA((n,))],
        compiler_params=_params(),
    )(*gs)


def chip_scatter(pbs):
    n = len(pbs)

    def body(*refs):
        src = refs[:n]
        dst = refs[n:2 * n]
        send_sems, recv_sems = refs[2 * n:]
        mx, my, mc = _me()
        copies = []
        for t in range(n):
            for o in range(1, N_CHIPS):
                px, py = _flip(mx, o & 2), _flip(my, o & 1)
                cp = pltpu.make_async_remote_copy(
                    src_ref=src[t].at[2 * px + py], dst_ref=dst[t].at[o - 1],
                    send_sem=send_sems.at[t, o - 1], recv_sem=recv_sems.at[t, o - 1],
                    device_id=(px, py, mc), device_id_type=MESH)
                cp.start()
                copies.append(cp)
        for cp in copies:
            cp.wait()

    any_spec = pl.BlockSpec(memory_space=pl.ANY)
    return pl.pallas_call(
        body, name="chip_scatter",
        in_specs=[any_spec] * n, out_specs=[any_spec] * n,
        out_shape=[jax.ShapeDtypeStruct((N_CHIPS - 1,) + p.shape[1:], p.dtype) for p in pbs],
        scratch_shapes=[pltpu.SemaphoreType.DMA((n, N_CHIPS - 1)), pltpu.SemaphoreType.DMA((n, N_CHIPS - 1))],
        compiler_params=_params(),
    )(*pbs)


def pair_fill_halves(fs):
    n = len(fs)

    def body(*refs):
        dst = refs[n:2 * n]
        send_sems, recv_sems = refs[2 * n:]
        mx, my, mc = _me()
        copies = []
        for t in range(n):
            mine = _half_at(dst[t], (slice(None),), mc)
            theirs = _half_at(dst[t], (slice(None),), 1 - mc)
            cp = pltpu.make_async_remote_copy(
                src_ref=mine, dst_ref=mine, send_sem=send_sems.at[t], recv_sem=recv_sems.at[t],
                device_id=(mx, my, 1 - mc), device_id_type=MESH)
            cp.start()
            copies.append((cp, pltpu.make_async_remote_copy(
                src_ref=theirs, dst_ref=theirs, send_sem=send_sems.at[t], recv_sem=recv_sems.at[t],
                device_id=(mx, my, 1 - mc), device_id_type=MESH)))
        for cp, arrival in copies:
            cp.wait_send()
            arrival.wait_recv()

    any_spec = pl.BlockSpec(memory_space=pl.ANY)
    return pl.pallas_call(
        body, name="pair_fill_halves",
        in_specs=[any_spec] * n, out_specs=[any_spec] * n,
        out_shape=[jax.ShapeDtypeStruct(f.shape, f.dtype) for f in fs],
        input_output_aliases={t: t for t in range(n)},
        scratch_shapes=[pltpu.SemaphoreType.DMA((n,)), pltpu.SemaphoreType.DMA((n,))],
        compiler_params=_params(),
    )(*fs)


def _pack_rows(parts, d):
    rows, spans = [], []
    at = 0
    for p in parts:
        flat = p.reshape(-1)
        n_rows = -(-flat.shape[0] // (8 * d)) * 8
        flat = jnp.pad(flat, (0, n_rows * d - flat.shape[0]))
        rows.append(flat.reshape(n_rows, d))
        spans.append((at, p.shape))
        at += n_rows
    return jnp.concatenate(rows, axis=0), spans


def _unpack_rows(packed, spans):
    out = []
    for at, shape in spans:
        n = math.prod(shape)
        d = packed.shape[1]
        n_rows = -(-n // d)
        out.append(packed[at:at + n_rows].reshape(-1)[:n].reshape(shape))
    return out


def _rotate_half_matrix():
    half = QK_ROPE // 2
    idx = jnp.arange(QK_ROPE)
    src = jnp.where(idx < half, idx + half, idx - half)
    sign = jnp.where(idx < half, -1.0, 1.0)
    return (jnp.zeros((QK_ROPE, QK_ROPE), F32).at[src, idx].set(sign)).astype(BF16)


def kernel(x, c, positions, ada_w, ada_b, ffn1_norm, ffn1_w_gate, ffn1_w_up, ffn1_w_down, mix_norm, w_in, pool_w, pool_scale, q_a_norm, w_q_b, kv_a_norm, w_kv_b, w_out, ffn2_norm, ffn2_w_gate, ffn2_w_up, ffn2_w_down, final_norm, loss_target, m_ada_w, m_ada_b, m_ffn1_norm, m_ffn1_w_gate, m_ffn1_w_up, m_ffn1_w_down, m_mix_norm, m_w_in, m_pool_w, m_pool_scale, m_q_a_norm, m_w_q_b, m_kv_a_norm, m_w_kv_b, m_w_out, m_ffn2_norm, m_ffn2_w_gate, m_ffn2_w_up, m_ffn2_w_down, m_final_norm, v_ada_w, v_ada_b, v_ffn1_norm, v_ffn1_w_gate, v_ffn1_w_up, v_ffn1_w_down, v_mix_norm, v_w_in, v_pool_w, v_pool_scale, v_q_a_norm, v_w_q_b, v_kv_a_norm, v_w_kv_b, v_w_out, v_ffn2_norm, v_ffn2_w_gate, v_ffn2_w_up, v_ffn2_w_down, v_final_norm):
    mx, my, mc = _me()
    chip = 2 * mx + my
    half = jnp.reshape(mc, (1,)).astype(jnp.int32)
    n_layers, d, ada_cols = ada_w.shape
    xt = x[0]
    tgt = loss_target[0]

    inv_freq = 1.0 / (ROPE_THETA ** (jnp.arange(0, QK_ROPE, 2, dtype=F32) / QK_ROPE))
    ang = positions[0].astype(F32)[:, None] * inv_freq
    ang = jnp.concatenate([ang, ang], axis=-1)
    cos, sin = jnp.cos(ang), jnp.sin(ang)
    rot = _rotate_half_matrix()
    rot_t = rot.T

    c_all = exchange8(c, True).reshape(N_DEV, d)
    c16 = jnp.pad(c_all, ((0, 8), (0, 0)))
    ada_b_loc = lax.dynamic_slice_in_dim(ada_b, chip * ada_cols, ada_cols, axis=1).reshape(n_layers, 1, ada_cols)
    mod_part = ada_fwd(c16, ada_w, ada_b_loc)[:, :N_DEV]
    mod_got = exchange8(jnp.transpose(mod_part, (1, 0, 2)), False)
    mod = jnp.transpose(mod_got.reshape(N_CHIPS, 2, n_layers, ada_cols)[:, 0], (1, 0, 2))
    mod = mod.reshape(n_layers, 9, 1, d)

    tr = lambda a: jnp.transpose(a, (0, 2, 1))
    local = [tr(ffn1_w_gate), tr(ffn1_w_up), ffn1_w_down, tr(w_in), tr(w_q_b), w_kv_b, w_out,
             tr(ffn2_w_gate), tr(ffn2_w_up), ffn2_w_down]
    local_bf = [cast_bf16(w) for w in local]
    g_sems, shards_fly, lands_fly, g_token = gather_start(local_bf, place_local(local_bf))
    gathered = []

    row = lambda a, l: a[l].reshape(1, -1)
    saved = []
    for l in range(n_layers):
        shards_fly, landed = gather_wait(l, g_sems[l], shards_fly, lands_fly[l], xt if l else g_token)
        gathered.append(gather_forward(landed))
        g1, u1, d1, win, wq, wkv, wout, g2, u2, d2 = gathered[l]
        win = win.reshape(-1, d)
        sv = dict(x0=xt)
        xt, sv["h1"], sv["gate1"], sv["up1"], sv["y1"] = ffn_fwd(
            xt, row(ffn1_norm, l), mod[l, 0], mod[l, 1], mod[l, 2], g1, u1, d1)
        sv["x1"] = xt
        sv["h2"], u, cq, ckv, kr = mix_in_fwd(xt, row(mix_norm, l), mod[l, 3], mod[l, 4], win)
        sv["cq"], sv["ckv"] = cq, ckv
        yp, sv["diff"] = pool_fwd(u, pool_w[l], row(pool_scale, l))
        qn, qr, kn, krr, vv, sv["ql"], sv["kvl"] = mla_qkv_fwd(
            cq, ckv, kr, row(q_a_norm, l), row(kv_a_norm, l), wq, wkv, cos, sin, rot)
        sv["qkv"] = (qn, qr, kn, krr, vv)
        om = attn_fwd(qn, qr, kn, krr, vv)
        xt, sv["ycat"], sv["y2"] = out_proj_fwd(yp, om, wout, xt, mod[l, 5])
        sv["x2"] = xt
        xt, sv["h3"], sv["gate3"], sv["up3"], sv["y3"] = ffn_fwd(
            xt, row(ffn2_norm, l), mod[l, 6], mod[l, 7], mod[l, 8], g2, u2, d2)
        saved.append(sv)

    loss_vec, dx, d_final_norm = final_loss(xt, final_norm.reshape(1, d), tgt)
    loss = lax.psum(loss_vec[0, 0], ("x", "y", "c"))

    none = [None] * n_layers
    dmods, dnorm1, dnorm2, dnorm3 = list(none), list(none), list(none), list(none)
    dpw, dps, dqan_l, dkvan_l = list(none), list(none), list(none), list(none)
    reduced = [None] * len(local)
    in_flight = None
    sel_of = lambda l: jnp.stack([mc, chip, jnp.asarray(l, mc.dtype)]).astype(jnp.int32)

    def finish(job, after):
        l_j, shapes, sums, (s_send, s_recv, pbs_fly, lands, _) = job
        parts = scatter_wait(l_j, s_send, s_recv, pbs_fly, lands, after)
        return [chip_sum(p32, rb, sel_of(l_j), (n_layers,) + shp, acc)
                for (p32, _), rb, shp, acc in zip(sums, parts, shapes, reduced)]

    for l in reversed(range(n_layers)):
        sv = saved[l]
        g1, u1, d1, win, wq, wkv, wout, g2, u2, d2 = gathered[l]
        win = win.reshape(-1, d)
        gt3 = mod[l, 8] if in_flight is None else mod[l, 8] + in_flight[3][4][0, 0]
        dy, a, dgt, dup = ffn_bwd_act(dx, sv["gate3"], sv["up3"], gt3, d2)
        dx, dvec3 = ffn_bwd_in(dx, sv["x2"], sv["y3"], dgt, dup, row(ffn2_norm, l), mod[l, 7], g2, u2)
        g_g2, g_u2, g_d2 = tn_mm(dgt, sv["h3"][None]), tn_mm(dup, sv["h3"][None]), tn_mm(a, dy[None])
        dy2, dyp, dom, dg2 = out_proj_bwd(dx, sv["y2"], mod[l, 5], wout)
        g_wout = tn_mm(sv["ycat"], dy2[None])
        qn, qr, kn, krr, vv = sv["qkv"]
        dqn, dqr, dkn, dkr, dvv = attn_bwd(qn, qr, kn, krr, vv, dom)
        dcq, dckv, dkr_in, gq, gkv, dqan_l[l], dkvan_l[l] = mla_qkv_bwd(
            dqn, dqr, dkn, dkr, dvv, sv["cq"], sv["ckv"], row(q_a_norm, l), row(kv_a_norm, l),
            wq, wkv, cos, sin, rot_t)
        g_wq, g_wkv = tn_mm(gq, sv["ql"][None]), tn_mm(sv["kvl"][None], gkv)
        du, dpw[l], dps[l] = pool_bwd(dyp, sv["diff"], pool_w[l], row(pool_scale, l))
        dx, dz, dvec2 = mix_in_bwd(dx, du, dcq, dckv, dkr_in, sv["x1"], row(mix_norm, l), mod[l, 4], win)
        g_win = tn_mm(dz[None], sv["h2"][None]).reshape(N_CHIPS, -1, d)
        dy, a, dgt, dup = ffn_bwd_act(dx, sv["gate1"], sv["up1"], mod[l, 2], d1)
        dx, dvec1 = ffn_bwd_in(dx, sv["x0"], sv["y1"], dgt, dup, row(ffn1_norm, l), mod[l, 1], g1, u1)
        g_g1, g_u1, g_d1 = tn_mm(dgt, sv["h1"][None]), tn_mm(dup, sv["h1"][None]), tn_mm(a, dy[None])
        dmods[l] = jnp.concatenate([dvec1[0:3], dvec2[0:2], dg2, dvec3[0:3]], axis=0)
        dnorm1[l], dnorm2[l], dnorm3[l] = dvec1[3], dvec2[3], dvec3[3]

        full = [g_g1, g_u1, g_d1, g_win, g_wq, g_wkv, g_wout, g_g2, g_u2, g_d2]
        got = pair_send_halves(full)
        sums = [pair_add(g, ra, half) for g, ra in zip(full, got)]
        started = scatter_start([pb for _, pb in sums], l)
        if in_flight is not None:
            reduced = finish(in_flight, dx)
        in_flight = (l, [g.shape[1:] for g in full], sums, started)

    small_parts = [jnp.stack(dmods), jnp.stack(dnorm1), jnp.stack(dnorm2), jnp.stack(dnorm3), d_final_norm,
                   jnp.stack(dps), jnp.stack(dqan_l), jnp.stack(dkvan_l), jnp.stack(dpw)]
    packed, spans = _pack_rows(small_parts, d)
    gathered_small = exchange8(packed, True)
    total = sum_devices(gathered_small)
    (g_ada_b, g_n1, g_n2, g_n3, g_fn, g_ps, g_qan, g_kvan, g_pw) = _unpack_rows(total, spans)
    dmod_all = gathered_small[:, :9 * n_layers].reshape(N_DEV, n_layers, 9 * d)
    dmod_loc = lax.dynamic_slice_in_dim(dmod_all, chip * ada_cols, ada_cols, axis=2)
    dmod16 = jnp.pad(jnp.transpose(dmod_loc, (1, 0, 2)), ((0, 0), (0, 8), (0, 0)))
    g_ada_w = ada_bwd(c16, dmod16)
    reduced = finish(in_flight, g_ada_w)
    g_local = list(pair_fill_halves(reduced))

    grads = [g_ada_w, g_ada_b, g_n1, g_local[0], g_local[1], g_local[2], g_n2, g_local[3],
             g_pw, g_ps, g_qan, g_local[4], g_kvan, g_local[5], g_local[6], g_n3,
             g_local[7], g_local[8], g_local[9], g_fn]
    weights = [ada_w, ada_b, ffn1_norm, ffn1_w_gate, ffn1_w_up, ffn1_w_down, mix_norm, w_in, pool_w, pool_scale,
               q_a_norm, w_q_b, kv_a_norm, w_kv_b, w_out, ffn2_norm, ffn2_w_gate, ffn2_w_up, ffn2_w_down, final_norm]
    ms = [m_ada_w, m_ada_b, m_ffn1_norm, m_ffn1_w_gate, m_ffn1_w_up, m_ffn1_w_down, m_mix_norm, m_w_in, m_pool_w,
          m_pool_scale, m_q_a_norm, m_w_q_b, m_kv_a_norm, m_w_kv_b, m_w_out, m_ffn2_norm, m_ffn2_w_gate, m_ffn2_w_up,
          m_ffn2_w_down, m_final_norm]
    vs = [v_ada_w, v_ada_b, v_ffn1_norm, v_ffn1_w_gate, v_ffn1_w_up, v_ffn1_w_down, v_mix_norm, v_w_in, v_pool_w,
          v_pool_scale, v_q_a_norm, v_w_q_b, v_kv_a_norm, v_w_kv_b, v_w_out, v_ffn2_norm, v_ffn2_w_gate, v_ffn2_w_up,
          v_ffn2_w_down, v_final_norm]
    transposed = (3, 4, 7, 11, 16, 17)
    outs = []
    for i, (w, g, m, v) in enumerate(zip(weights, grads, ms, vs)):
        if i in transposed:
            outs.append(tuple(tr(t) for t in (g,) + adamw(tr(w), g, tr(m), tr(v))))
        else:
            g = g.reshape(w.shape)
            outs.append((g,) + adamw(w, g, m, v))
    return (loss, dx.reshape(x.shape), *[t[0] for t in outs], *[t[1] for t in outs], *[t[2] for t in outs],
            *[t[3] for t in outs])
```

```python
import math

import jax
import jax.numpy as jnp
from jax import lax
from jax.experimental import pallas as pl
from jax.experimental.pallas import tpu as pltpu

F32 = jnp.float32
BF16 = jnp.bfloat16
MESH = pl.DeviceIdType.MESH

EPS = 1e-6
ROPE_THETA = 10000.0
N_HEADS = 4
QK_NOPE = 128
QK_ROPE = 64
V_HEAD = 128
POOL_WINDOWS = (2, 4, 8, 16)
POOL_GC = 128
POOL_WIDTH = POOL_GC * len(POOL_WINDOWS)
Q_LORA = 384
KV_LORA = 256
SOFTMAX_SCALE = 1.0 / math.sqrt(QK_NOPE + QK_ROPE)
N_CHIPS = 4
N_DEV = 8

ADAM_LR = 0.001
ADAM_B1 = 0.9
ADAM_B2 = 0.999
ADAM_EPS = 1e-08
ADAM_WD = 0.01
ADAM_STEP = 10

ROW_TILE = 512
ATT_TILE = 256
VMEM_LIMIT = 56 * 1024 * 1024
BF16_ROWS = 16
LANES = 128


def _params(sem=None, vmem=VMEM_LIMIT):
    return pltpu.CompilerParams(dimension_semantics=sem, vmem_limit_bytes=vmem)


def _dot(a, b):
    return jnp.dot(a, b, preferred_element_type=F32)


def _dot_nt(a, b):
    return lax.dot_general(a, b, (((1,), (1,)), ((), ())), preferred_element_type=F32)


def _dot_tn(a, b):
    return lax.dot_general(a, b, (((0,), (0,)), ((), ())), preferred_element_type=F32)


def _dot_exact(t, perm):
    t1 = t.astype(BF16)
    r1 = t - t1.astype(F32)
    t2 = r1.astype(BF16)
    t3 = (r1 - t2.astype(F32)).astype(BF16)
    return _dot(t1, perm) + _dot(t2, perm) + _dot(t3, perm)


def _sum0(a):
    return jnp.sum(a, axis=0, keepdims=True)


def _rms(xt):
    r = lax.rsqrt(jnp.mean(xt * xt, axis=-1, keepdims=True) + EPS)
    return xt * r, r


def _rms_bwd(dy, xt, g):
    xhat, r = _rms(xt)
    dxhat = dy * g
    dx = r * (dxhat - xhat * jnp.mean(dxhat * xhat, axis=-1, keepdims=True))
    return dx, _sum0(dy * xhat)


def _normmod_bwd(dh, xt, gn, sc):
    xhat, _ = _rms(xt)
    dn = dh * (1.0 + sc)
    dx, dgn = _rms_bwd(dn, xt, gn)
    return dx, _sum0(dh), _sum0(dh * (xhat * gn)), dgn


def _row_tile(s):
    return min(s, ROW_TILE)


def _full(shape):
    n = len(shape)
    return pl.BlockSpec(shape, lambda *_: (0,) * n)


def _resident(shape):
    n = len(shape)
    return pl.BlockSpec(shape, lambda *_: (0,) * n, pipeline_mode=pl.Buffered(1))


def ffn_fwd(x, gn, sh, sc, gt, wg, wu, wd):
    s, d = x.shape
    k_chunks, fs, _ = wg.shape
    tm = _row_tile(s)

    def body(x_ref, gn_ref, sh_ref, sc_ref, gt_ref, wg_ref, wu_ref, wd_ref,
             xo_ref, h_ref, gate_ref, up_ref, y_ref):
        xt = x_ref[...]
        xhat, _ = _rms(xt)
        h = (xhat * gn_ref[...] * (1.0 + sc_ref[...]) + sh_ref[...]).astype(BF16)
        h_ref[...] = h
        y = jnp.zeros((tm, d), F32)
        for k in range(k_chunks):
            gate = _dot_nt(h, wg_ref[k])
            up = _dot_nt(h, wu_ref[k])
            gate_ref[k] = gate.astype(BF16)
            up_ref[k] = up.astype(BF16)
            y += _dot((gate * jax.nn.sigmoid(gate) * up).astype(BF16), wd_ref[k])
        y_ref[...] = y.astype(BF16)
        xo_ref[...] = xt + 0.5 * gt_ref[...] * y

    row = pl.BlockSpec((tm, d), lambda i: (i, 0))
    vec = pl.BlockSpec((1, d), lambda i: (0, 0))
    act = pl.BlockSpec((k_chunks, tm, fs), lambda i: (0, i, 0))
    return pl.pallas_call(
        body, name="ffn_fwd",
        grid=(s // tm,),
        in_specs=[row, vec, vec, vec, vec, _resident(wg.shape), _resident(wu.shape), _resident(wd.shape)],
        out_specs=[row, row, act, act, row],
        out_shape=[jax.ShapeDtypeStruct((s, d), F32), jax.ShapeDtypeStruct((s, d), BF16),
                   jax.ShapeDtypeStruct((k_chunks, s, fs), BF16), jax.ShapeDtypeStruct((k_chunks, s, fs), BF16),
                   jax.ShapeDtypeStruct((s, d), BF16)],
        compiler_params=_params(("arbitrary",)),
    )(x, gn, sh, sc, gt, wg, wu, wd)


def ffn_bwd_act(dxn, gate, up, gt, wd):
    s, d = dxn.shape
    k_chunks, fs, _ = wd.shape
    tm = _row_tile(s)

    def body(dxn_ref, gate_ref, up_ref, gt_ref, wd_ref, dy_ref, a_ref, dgate_ref, dup_ref):
        dy = (0.5 * gt_ref[...] * dxn_ref[...]).astype(BF16)
        dy_ref[...] = dy
        for k in range(k_chunks):
            da = _dot_nt(dy, wd_ref[k])
            g = gate_ref[k].astype(F32)
            u = up_ref[k].astype(F32)
            sg = jax.nn.sigmoid(g)
            sl = g * sg
            a_ref[k] = (sl * u).astype(BF16)
            dgate_ref[k] = (da * u * (sg * (1.0 + g * (1.0 - sg)))).astype(BF16)
            dup_ref[k] = (da * sl).astype(BF16)

    row = pl.BlockSpec((tm, d), lambda i: (i, 0))
    act = pl.BlockSpec((k_chunks, tm, fs), lambda i: (0, i, 0))
    act_shape = jax.ShapeDtypeStruct((k_chunks, s, fs), BF16)
    return pl.pallas_call(
        body, name="ffn_bwd_act",
        grid=(s // tm,),
        in_specs=[row, act, act, pl.BlockSpec((1, d), lambda i: (0, 0)), _resident(wd.shape)],
        out_specs=[row, act, act, act],
        out_shape=[jax.ShapeDtypeStruct((s, d), BF16), act_shape, act_shape, act_shape],
        compiler_params=_params(("arbitrary",)),
    )(dxn, gate, up, gt, wd)


def ffn_bwd_in(dxn, x, y, dgate, dup, gn, sc, wg, wu):
    s, d = x.shape
    k_chunks, fs, _ = wg.shape
    tm = _row_tile(s)

    def body(dxn_ref, x_ref, y_ref, dgate_ref, dup_ref, gn_ref, sc_ref, wg_ref, wu_ref, dx_ref, dvec_ref):
        i = pl.program_id(0)

        @pl.when(i == 0)
        def _():
            dvec_ref[...] = jnp.zeros_like(dvec_ref)

        dh = jnp.zeros((tm, d), F32)
        for k in range(k_chunks):
            dh += _dot(dgate_ref[k], wg_ref[k]) + _dot(dup_ref[k], wu_ref[k])
        dxn_t = dxn_ref[...]
        dx, dsh, dsc, dgn = _normmod_bwd(dh, x_ref[...], gn_ref[...], sc_ref[...])
        dx_ref[...] = dx + dxn_t
        dvec_ref[0:1, :] += dsh
        dvec_ref[1:2, :] += dsc
        dvec_ref[2:3, :] += _sum0(0.5 * dxn_t * y_ref[...].astype(F32))
        dvec_ref[3:4, :] += dgn

    row = pl.BlockSpec((tm, d), lambda i: (i, 0))
    vec = pl.BlockSpec((1, d), lambda i: (0, 0))
    act = pl.BlockSpec((k_chunks, tm, fs), lambda i: (0, i, 0))
    return pl.pallas_call(
        body, name="ffn_bwd_in",
        grid=(s // tm,),
        in_specs=[row, row, row, act, act, vec, vec, _resident(wg.shape), _resident(wu.shape)],
        out_specs=[row, pl.BlockSpec((8, d), lambda i: (0, 0))],
        out_shape=[jax.ShapeDtypeStruct((s, d), F32), jax.ShapeDtypeStruct((8, d), F32)],
        compiler_params=_params(("arbitrary",)),
    )(dxn, x, y, dgate, dup, gn, sc, wg, wu)


def tn_mm(a, b):
    ga, s, m = a.shape
    gb, _, n = b.shape
    g = max(ga, gb)

    def body(a_ref, b_ref, o_ref):
        o_ref[...] = _dot_tn(a_ref[...], b_ref[...])

    a_spec = pl.BlockSpec((None, s, m), (lambda gi: (gi, 0, 0)) if ga > 1 else (lambda gi: (0, 0, 0)))
    b_spec = pl.BlockSpec((None, s, n), (lambda gi: (gi, 0, 0)) if gb > 1 else (lambda gi: (0, 0, 0)))
    return pl.pallas_call(
        body, name="tn_mm",
        grid=(g,), in_specs=[a_spec, b_spec], out_specs=pl.BlockSpec((None, m, n), lambda gi: (gi, 0, 0)),
        out_shape=jax.ShapeDtypeStruct((g, m, n), F32),
        compiler_params=_params(("arbitrary",)),
    )(a, b)


def mix_in_fwd(x, gn, sh, sc, w_in_t):
    s, d = x.shape
    tm = _row_tile(s)
    o1, o2, o3 = POOL_WIDTH, POOL_WIDTH + Q_LORA, POOL_WIDTH + Q_LORA + KV_LORA

    def body(x_ref, gn_ref, sh_ref, sc_ref, w_ref, h_ref, u_ref, cq_ref, ckv_ref, kr_ref):
        xhat, _ = _rms(x_ref[...])
        h = (xhat * gn_ref[...] * (1.0 + sc_ref[...]) + sh_ref[...]).astype(BF16)
        h_ref[...] = h
        z = _dot_nt(h, w_ref[0:o3, :])
        u_ref[...] = z[:, 0:o1]
        cq_ref[...] = z[:, o1:o2]
        ckv_ref[...] = z[:, o2:o3]
        kr_ref[...] = _dot_nt(h, w_ref[o3:, :])

    row = lambda w: pl.BlockSpec((tm, w), lambda i: (i, 0))
    vec = pl.BlockSpec((1, d), lambda i: (0, 0))
    return pl.pallas_call(
        body, name="mix_in_fwd",
        grid=(s // tm,),
        in_specs=[row(d), vec, vec, vec, _full(w_in_t.shape)],
        out_specs=[row(d), row(POOL_WIDTH), row(Q_LORA), row(KV_LORA), row(QK_ROPE)],
        out_shape=[jax.ShapeDtypeStruct((s, d), BF16), jax.ShapeDtypeStruct((s, POOL_WIDTH), F32),
                   jax.ShapeDtypeStruct((s, Q_LORA), F32), jax.ShapeDtypeStruct((s, KV_LORA), F32),
                   jax.ShapeDtypeStruct((s, QK_ROPE), F32)],
        compiler_params=_params(("arbitrary",)),
    )(x, gn, sh, sc, w_in_t)


def mix_in_bwd(dxn, du, dcq, dckv, dkr, x, gn, sc, w_in_t):
    s, d = x.shape
    tm = _row_tile(s)
    o1, o2, o3 = POOL_WIDTH, POOL_WIDTH + Q_LORA, POOL_WIDTH + Q_LORA + KV_LORA
    n_z = w_in_t.shape[0]

    def body(dxn_ref, du_ref, dcq_ref, dckv_ref, dkr_ref, x_ref, gn_ref, sc_ref, w_ref, dx_ref, dz_ref, dvec_ref):
        i = pl.program_id(0)

        @pl.when(i == 0)
        def _():
            dvec_ref[...] = jnp.zeros_like(dvec_ref)

        dub = du_ref[...].astype(BF16)
        dqb = dcq_ref[...].astype(BF16)
        dkb = dckv_ref[...].astype(BF16)
        drb = dkr_ref[...].astype(BF16)
        dz_ref[:, 0:o1] = dub
        dz_ref[:, o1:o2] = dqb
        dz_ref[:, o2:o3] = dkb
        dz_ref[:, o3:] = drb
        dh = (_dot(dub, w_ref[0:o1, :]) + _dot(dqb, w_ref[o1:o2, :]) + _dot(dkb, w_ref[o2:o3, :])
              + _dot(drb, w_ref[o3:, :]))
        dx, dsh, dsc, dgn = _normmod_bwd(dh, x_ref[...], gn_ref[...], sc_ref[...])
        dx_ref[...] = dx + dxn_ref[...]
        dvec_ref[0:1, :] += dsh
        dvec_ref[1:2, :] += dsc
        dvec_ref[3:4, :] += dgn

    row = lambda w: pl.BlockSpec((tm, w), lambda i: (i, 0))
    vec = pl.BlockSpec((1, d), lambda i: (0, 0))
    return pl.pallas_call(
        body, name="mix_in_bwd",
        grid=(s // tm,),
        in_specs=[row(d), row(POOL_WIDTH), row(Q_LORA), row(KV_LORA), row(QK_ROPE), row(d), vec, vec,
                  _full(w_in_t.shape)],
        out_specs=[row(d), row(n_z), pl.BlockSpec((8, d), lambda i: (0, 0))],
        out_shape=[jax.ShapeDtypeStruct((s, d), F32), jax.ShapeDtypeStruct((s, n_z), BF16),
                   jax.ShapeDtypeStruct((8, d), F32)],
        compiler_params=_params(("arbitrary",)),
    )(dxn, du, dcq, dckv, dkr, x, gn, sc, w_in_t)


def _window_sum(a, w, rows, forward):
    s = a.shape[0]
    step = 1
    while step < w:
        if forward:
            shifted = jnp.where(rows < s - step, pltpu.roll(a, s - step, 0), 0.0)
        else:
            shifted = jnp.where(rows >= step, pltpu.roll(a, step, 0), 0.0)
        a = a + shifted
        step *= 2
    return a


def pool_fwd(u, pool_w, pool_scale):
    s = u.shape[0]

    def body(u_ref, w_ref, sc_ref, y_ref, diff_ref):
        rows = lax.broadcasted_iota(jnp.int32, (s, POOL_GC), 0)
        for g, w in enumerate(POOL_WINDOWS):
            cols = slice(g * POOL_GC, (g + 1) * POOL_GC)
            ug = u_ref[:, cols]
            cnt = jnp.minimum(rows + 1, w).astype(F32)
            diff = (_window_sum(ug, w, rows, False) / cnt - ug).astype(BF16)
            diff_ref[:, cols] = diff
            y_ref[:, cols] = _dot(diff, w_ref[g].astype(BF16)) * sc_ref[:, cols]

    return pl.pallas_call(
        body, name="pool_fwd",
        out_shape=[jax.ShapeDtypeStruct(u.shape, F32), jax.ShapeDtypeStruct(u.shape, BF16)],
        compiler_params=_params(),
    )(u, pool_w, pool_scale)


def pool_bwd(dy, diff, pool_w, pool_scale):
    s = dy.shape[0]

    def body(dy_ref, diff_ref, w_ref, sc_ref, du_ref, dw_ref, dsc_ref):
        rows = lax.broadcasted_iota(jnp.int32, (s, POOL_GC), 0)
        for g, w in enumerate(POOL_WINDOWS):
            cols = slice(g * POOL_GC, (g + 1) * POOL_GC)
            dyg = dy_ref[:, cols]
            diff = diff_ref[:, cols]
            wb = w_ref[g].astype(BF16)
            dsc_ref[:, cols] = _sum0(dyg * _dot(diff, wb))
            dys = (dyg * sc_ref[:, cols]).astype(BF16)
            dw_ref[g] = _dot_tn(diff, dys)
            ddiff = _dot_nt(dys, wb)
            cnt = jnp.minimum(rows + 1, w).astype(F32)
            du_ref[:, cols] = _window_sum(ddiff / cnt, w, rows, True) - ddiff

    return pl.pallas_call(
        body, name="pool_bwd",
        out_shape=[jax.ShapeDtypeStruct(dy.shape, F32), jax.ShapeDtypeStruct(pool_w.shape, F32),
                   jax.ShapeDtypeStruct(pool_scale.shape, F32)],
        compiler_params=_params(),
    )(dy, diff, pool_w, pool_scale)


def mla_qkv_fwd(cq, ckv, kr, qan, kvan, wq, wkv, cos, sin, rot):
    s = cq.shape[0]
    tm = _row_tile(s)

    def body(cq_ref, ckv_ref, kr_ref, qan_ref, kvan_ref, wq_ref, wkv_ref, cos_ref, sin_ref, rot_ref,
             qn_ref, qr_ref, kn_ref, krr_ref, v_ref, ql_ref, kvl_ref):
        cos_t = cos_ref[...]
        sin_t = sin_ref[...]
        perm = rot_ref[...]

        def rope(t):
            return t * cos_t + _dot_exact(t, perm) * sin_t

        qhat, _ = _rms(cq_ref[...])
        ql = (qhat * qan_ref[...]).astype(BF16)
        ql_ref[...] = ql
        khat, _ = _rms(ckv_ref[...])
        kvl = (khat * kvan_ref[...]).astype(BF16)
        kvl_ref[...] = kvl
        krr_ref[...] = rope(kr_ref[...]).astype(BF16)
        for h in range(N_HEADS):
            q = _dot_nt(ql, wq_ref[h])
            qn_ref[h] = q[:, 0:QK_NOPE].astype(BF16)
            qr_ref[h] = rope(q[:, QK_NOPE:]).astype(BF16)
            kv = _dot(kvl, wkv_ref[h])
            kn_ref[h] = kv[:, 0:QK_NOPE].astype(BF16)
            v_ref[h] = kv[:, QK_NOPE:].astype(BF16)

    row = lambda w: pl.BlockSpec((tm, w), lambda i: (i, 0))
    hrow = lambda w: pl.BlockSpec((N_HEADS, tm, w), lambda i: (0, i, 0))
    return pl.pallas_call(
        body, name="mla_qkv_fwd",
        grid=(s // tm,),
        in_specs=[row(Q_LORA), row(KV_LORA), row(QK_ROPE), _full(qan.shape), _full(kvan.shape),
                  _full(wq.shape), _full(wkv.shape), row(QK_ROPE), row(QK_ROPE), _full(rot.shape)],
        out_specs=[hrow(QK_NOPE), hrow(QK_ROPE), hrow(QK_NOPE), row(QK_ROPE), hrow(V_HEAD), row(Q_LORA), row(KV_LORA)],
        out_shape=[jax.ShapeDtypeStruct((N_HEADS, s, QK_NOPE), BF16), jax.ShapeDtypeStruct((N_HEADS, s, QK_ROPE), BF16),
                   jax.ShapeDtypeStruct((N_HEADS, s, QK_NOPE), BF16), jax.ShapeDtypeStruct((s, QK_ROPE), BF16),
                   jax.ShapeDtypeStruct((N_HEADS, s, V_HEAD), BF16), jax.ShapeDtypeStruct((s, Q_LORA), BF16),
                   jax.ShapeDtypeStruct((s, KV_LORA), BF16)],
        compiler_params=_params(("arbitrary",)),
    )(cq, ckv, kr, qan, kvan, wq, wkv, cos, sin, rot)


def _attn_probs(qn_ref, qr_ref, kn_ref, kr_ref, qi, tq):
    n = (qi + 1) * tq
    rows = slice(qi * tq, n)
    sc = (_dot_nt(qn_ref[rows, :], kn_ref[0:n, :]) + _dot_nt(qr_ref[rows, :], kr_ref[0:n, :])) * SOFTMAX_SCALE
    qpos = qi * tq + lax.broadcasted_iota(jnp.int32, (tq, n), 0)
    kpos = lax.broadcasted_iota(jnp.int32, (tq, n), 1)
    sc = jnp.where(qpos >= kpos, sc, -1e30)
    e = jnp.exp(sc - jnp.max(sc, axis=-1, keepdims=True))
    return e / jnp.sum(e, axis=-1, keepdims=True)


def attn_fwd(qn, qr, kn, krr, v):
    nh, s, _ = qn.shape
    tq = min(s, ATT_TILE)

    def body(qn_ref, qr_ref, kn_ref, kr_ref, v_ref, o_ref):
        for qi in range(s // tq):
            n = (qi + 1) * tq
            p = _attn_probs(qn_ref, qr_ref, kn_ref, kr_ref, qi, tq).astype(BF16)
            o_ref[qi * tq:n, :] = _dot(p, v_ref[0:n, :])

    head = lambda w: pl.BlockSpec((None, s, w), lambda h: (h, 0, 0))
    return pl.pallas_call(
        body, name="attn_fwd",
        grid=(nh,),
        in_specs=[head(QK_NOPE), head(QK_ROPE), head(QK_NOPE), _full(krr.shape), head(V_HEAD)],
        out_specs=pl.BlockSpec((s, V_HEAD), lambda h: (0, h)),
        out_shape=jax.ShapeDtypeStruct((s, nh * V_HEAD), F32),
        compiler_params=_params(("arbitrary",)),
    )(qn, qr, kn, krr, v)


def attn_bwd(qn, qr, kn, krr, v, do):
    nh, s, _ = qn.shape
    tq = min(s, ATT_TILE)

    def body(qn_ref, qr_ref, kn_ref, kr_ref, v_ref, do_ref, dqn_ref, dqr_ref, dkn_ref, dkr_ref, dv_ref):
        dkn_ref[...] = jnp.zeros_like(dkn_ref)
        dkr_ref[...] = jnp.zeros_like(dkr_ref)
        dv_ref[...] = jnp.zeros_like(dv_ref)
        for qi in range(s // tq):
            n = (qi + 1) * tq
            rows = slice(qi * tq, n)
            p = _attn_probs(qn_ref, qr_ref, kn_ref, kr_ref, qi, tq)
            dob = do_ref[rows, :].astype(BF16)
            dp = _dot_nt(dob, v_ref[0:n, :])
            ds = (p * (dp - jnp.sum(p * dp, axis=-1, keepdims=True)) * SOFTMAX_SCALE).astype(BF16)
            dqn_ref[rows, :] = _dot(ds, kn_ref[0:n, :])
            dqr_ref[rows, :] = _dot(ds, kr_ref[0:n, :])
            dkn_ref[0:n, :] += _dot_tn(ds, qn_ref[rows, :])
            dkr_ref[0:n, :] += _dot_tn(ds, qr_ref[rows, :])
            dv_ref[0:n, :] += _dot_tn(p.astype(BF16), dob)

    head = lambda w: pl.BlockSpec((None, s, w), lambda h: (h, 0, 0))
    return pl.pallas_call(
        body, name="attn_bwd",
        grid=(nh,),
        in_specs=[head(QK_NOPE), head(QK_ROPE), head(QK_NOPE), _full(krr.shape), head(V_HEAD),
                  pl.BlockSpec((s, V_HEAD), lambda h: (0, h))],
        out_specs=[head(QK_NOPE), head(QK_ROPE), head(QK_NOPE), head(QK_ROPE), head(V_HEAD)],
        out_shape=[jax.ShapeDtypeStruct((nh, s, QK_NOPE), F32), jax.ShapeDtypeStruct((nh, s, QK_ROPE), F32),
                   jax.ShapeDtypeStruct((nh, s, QK_NOPE), F32), jax.ShapeDtypeStruct((nh, s, QK_ROPE), F32),
                   jax.ShapeDtypeStruct((nh, s, V_HEAD), F32)],
        compiler_params=_params(("arbitrary",)),
    )(qn, qr, kn, krr, v, do)


def mla_qkv_bwd(dqn, dqr, dkn, dkr, dv, cq, ckv, qan, kvan, wq, wkv, cos, sin, rot_t):
    s = cq.shape[0]
    tm = _row_tile(s)

    def body(dqn_ref, dqr_ref, dkn_ref, dkr_ref, dv_ref, cq_ref, ckv_ref, qan_ref, kvan_ref,
             wq_ref, wkv_ref, cos_ref, sin_ref, rot_ref,
             dcq_ref, dckv_ref, dkro_ref, gq_ref, gkv_ref, dqan_ref, dkvan_ref):
        i = pl.program_id(0)

        @pl.when(i == 0)
        def _():
            dqan_ref[...] = jnp.zeros_like(dqan_ref)
            dkvan_ref[...] = jnp.zeros_like(dkvan_ref)

        cos_t = cos_ref[...]
        sin_t = sin_ref[...]
        perm_t = rot_ref[...]

        def unrope(t):
            return t * cos_t + _dot_exact(t * sin_t, perm_t)

        acc_q = jnp.zeros((tm, Q_LORA), F32)
        acc_kv = jnp.zeros((tm, KV_LORA), F32)
        dkr_sum = jnp.zeros((tm, QK_ROPE), F32)
        for h in range(N_HEADS):
            a = dqn_ref[h].astype(BF16)
            b = unrope(dqr_ref[h]).astype(BF16)
            gq_ref[h, :, 0:QK_NOPE] = a
            gq_ref[h, :, QK_NOPE:] = b
            wq_h = wq_ref[h]
            acc_q += _dot(a, wq_h[0:QK_NOPE, :]) + _dot(b, wq_h[QK_NOPE:, :])
            dk = dkn_ref[h].astype(BF16)
            dvv = dv_ref[h].astype(BF16)
            gkv_ref[h, :, 0:QK_NOPE] = dk
            gkv_ref[h, :, QK_NOPE:] = dvv
            wkv_h = wkv_ref[h]
            acc_kv += _dot_nt(dk, wkv_h[:, 0:QK_NOPE]) + _dot_nt(dvv, wkv_h[:, QK_NOPE:])
            dkr_sum += dkr_ref[h]
        dkro_ref[...] = unrope(dkr_sum)
        dcq, dqan = _rms_bwd(acc_q, cq_ref[...], qan_ref[...])
        dcq_ref[...] = dcq
        dqan_ref[...] += dqan
        dckv, dkvan = _rms_bwd(acc_kv, ckv_ref[...], kvan_ref[...])
        dckv_ref[...] = dckv
        dkvan_ref[...] += dkvan

    row = lambda w: pl.BlockSpec((tm, w), lambda i: (i, 0))
    hrow = lambda w: pl.BlockSpec((N_HEADS, tm, w), lambda i: (0, i, 0))
    return pl.pallas_call(
        body, name="mla_qkv_bwd",
        grid=(s // tm,),
        in_specs=[hrow(QK_NOPE), hrow(QK_ROPE), hrow(QK_NOPE), hrow(QK_ROPE), hrow(V_HEAD),
                  row(Q_LORA), row(KV_LORA), _full(qan.shape), _full(kvan.shape),
                  _full(wq.shape), _full(wkv.shape), row(QK_ROPE), row(QK_ROPE), _full(rot_t.shape)],
        out_specs=[row(Q_LORA), row(KV_LORA), row(QK_ROPE), hrow(QK_NOPE + QK_ROPE), hrow(QK_NOPE + V_HEAD),
                   _full(qan.shape), _full(kvan.shape)],
        out_shape=[jax.ShapeDtypeStruct((s, Q_LORA), F32), jax.ShapeDtypeStruct((s, KV_LORA), F32),
                   jax.ShapeDtypeStruct((s, QK_ROPE), F32),
                   jax.ShapeDtypeStruct((N_HEADS, s, QK_NOPE + QK_ROPE), BF16),
                   jax.ShapeDtypeStruct((N_HEADS, s, QK_NOPE + V_HEAD), BF16),
                   jax.ShapeDtypeStruct(qan.shape, F32), jax.ShapeDtypeStruct(kvan.shape, F32)],
        compiler_params=_params(("arbitrary",)),
    )(dqn, dqr, dkn, dkr, dv, cq, ckv, qan, kvan, wq, wkv, cos, sin, rot_t)


def out_proj_fwd(yp, om, w_out, x, gt):
    s, d = x.shape
    n_sh, rs, _ = w_out.shape
    tm = _row_tile(s)
    per = POOL_WIDTH // rs

    def body(yp_ref, om_ref, w_ref, x_ref, gt_ref, xo_ref, ycat_ref, y_ref):
        y = jnp.zeros((tm, d), F32)
        for j in range(n_sh):
            src = yp_ref if j < per else om_ref
            part = src[:, (j % per) * rs:(j % per + 1) * rs].astype(BF16)
            ycat_ref[j] = part
            y += _dot(part, w_ref[j])
        y_ref[...] = y.astype(BF16)
        xo_ref[...] = x_ref[...] + gt_ref[...] * y

    row = lambda w: pl.BlockSpec((tm, w), lambda i: (i, 0))
    return pl.pallas_call(
        body, name="out_proj_fwd",
        grid=(s // tm,),
        in_specs=[row(POOL_WIDTH), row(POOL_WIDTH), _full(w_out.shape), row(d), pl.BlockSpec((1, d), lambda i: (0, 0))],
        out_specs=[row(d), pl.BlockSpec((n_sh, tm, rs), lambda i: (0, i, 0)), row(d)],
        out_shape=[jax.ShapeDtypeStruct((s, d), F32), jax.ShapeDtypeStruct((n_sh, s, rs), BF16),
                   jax.ShapeDtypeStruct((s, d), BF16)],
        compiler_params=_params(("arbitrary",)),
    )(yp, om, w_out, x, gt)


def out_proj_bwd(dxn, y, gt, w_out):
    s, d = dxn.shape
    n_sh, rs, _ = w_out.shape
    tm = _row_tile(s)
    per = POOL_WIDTH // rs

    def body(dxn_ref, y_ref, gt_ref, w_ref, dy_ref, dyp_ref, dom_ref, dgt_ref):
        i = pl.program_id(0)

        @pl.when(i == 0)
        def _():
            dgt_ref[...] = jnp.zeros_like(dgt_ref)

        dxn_t = dxn_ref[...]
        dy = (gt_ref[...] * dxn_t).astype(BF16)
        dy_ref[...] = dy
        dgt_ref[...] += _sum0(dxn_t * y_ref[...].astype(F32))
        for j in range(n_sh):
            dst = dyp_ref if j < per else dom_ref
            dst[:, (j % per) * rs:(j % per + 1) * rs] = _dot_nt(dy, w_ref[j])

    row = lambda w: pl.BlockSpec((tm, w), lambda i: (i, 0))
    vec = pl.BlockSpec((1, d), lambda i: (0, 0))
    return pl.pallas_call(
        body, name="out_proj_bwd",
        grid=(s // tm,),
        in_specs=[row(d), row(d), vec, _full(w_out.shape)],
        out_specs=[row(d), row(POOL_WIDTH), row(POOL_WIDTH), vec],
        out_shape=[jax.ShapeDtypeStruct((s, d), BF16), jax.ShapeDtypeStruct((s, POOL_WIDTH), F32),
                   jax.ShapeDtypeStruct((s, POOL_WIDTH), F32), jax.ShapeDtypeStruct((1, d), F32)],
        compiler_params=_params(("arbitrary",)),
    )(dxn, y, gt, w_out)


def final_loss(x, gn, tgt):
    s, d = x.shape
    tm = _row_tile(s)

    def body(x_ref, gn_ref, t_ref, loss_ref, dx_ref, dgn_ref):
        i = pl.program_id(0)

        @pl.when(i == 0)
        def _():
            loss_ref[...] = jnp.zeros_like(loss_ref)
            dgn_ref[...] = jnp.zeros_like(dgn_ref)

        xt = x_ref[...]
        g = gn_ref[...]
        xhat, _ = _rms(xt)
        err = xhat * g - t_ref[...]
        per_tok = jnp.mean(err * err, axis=-1, keepdims=True)
        loss_ref[...] += jnp.broadcast_to(0.5 * _sum0(per_tok), loss_ref.shape)
        dx, dgn = _rms_bwd(err * (1.0 / d), xt, g)
        dx_ref[...] = dx
        dgn_ref[...] += dgn

    row = pl.BlockSpec((tm, d), lambda i: (i, 0))
    vec = pl.BlockSpec((1, d), lambda i: (0, 0))
    return pl.pallas_call(
        body, name="final_loss",
        grid=(s // tm,),
        in_specs=[row, vec, row],
        out_specs=[pl.BlockSpec((1, LANES), lambda i: (0, 0)), row, vec],
        out_shape=[jax.ShapeDtypeStruct((1, LANES), F32), jax.ShapeDtypeStruct((s, d), F32),
                   jax.ShapeDtypeStruct((1, d), F32)],
        compiler_params=_params(("arbitrary",)),
    )(x, gn, tgt)


def _col_tile(cols):
    return 768 if cols % 768 == 0 else cols


def ada_fwd(c16, ada_w, ada_b_loc):
    n_layers, d, cols = ada_w.shape
    tn = _col_tile(cols)

    def body(c_ref, w_ref, b_ref, o_ref):
        cv = c_ref[...]
        ca = (cv * jax.nn.sigmoid(cv)).astype(BF16)
        o_ref[...] = _dot(ca, w_ref[...].astype(BF16)) + b_ref[...]

    return pl.pallas_call(
        body, name="ada_fwd",
        grid=(n_layers, cols // tn),
        in_specs=[pl.BlockSpec((16, d), lambda l, j: (0, 0)), pl.BlockSpec((None, d, tn), lambda l, j: (l, 0, j)),
                  pl.BlockSpec((None, 1, tn), lambda l, j: (l, 0, j))],
        out_specs=pl.BlockSpec((None, 16, tn), lambda l, j: (l, 0, j)),
        out_shape=jax.ShapeDtypeStruct((n_layers, 16, cols), F32),
        compiler_params=_params(("arbitrary", "arbitrary")),
    )(c16, ada_w, ada_b_loc)


def ada_bwd(c16, dmod16):
    n_layers, _, cols = dmod16.shape
    d = c16.shape[1]
    tn = _col_tile(cols)

    def body(c_ref, g_ref, o_ref):
        cv = c_ref[...]
        ca = (cv * jax.nn.sigmoid(cv)).astype(BF16)
        o_ref[...] = _dot_tn(ca, g_ref[...].astype(BF16))

    return pl.pallas_call(
        body, name="ada_bwd",
        grid=(n_layers, cols // tn),
        in_specs=[pl.BlockSpec((16, d), lambda l, j: (0, 0)), pl.BlockSpec((None, 16, tn), lambda l, j: (l, 0, j))],
        out_specs=pl.BlockSpec((None, d, tn), lambda l, j: (l, 0, j)),
        out_shape=jax.ShapeDtypeStruct((n_layers, d, cols), F32),
        compiler_params=_params(("arbitrary", "arbitrary")),
    )(c16, dmod16)


def _as_rows(a):
    if a.ndim == 1:
        return a.reshape(1, a.shape[0])
    return a.reshape(-1, a.shape[-1])


def _rows_tile(r, c, itemsize=4, budget=2 * 1024 * 1024):
    if r * c * itemsize <= budget:
        return r
    best = None
    t = BF16_ROWS
    while t < r:
        if r % t == 0 and t * c * itemsize <= budget:
            best = t
        t += BF16_ROWS
    return best if best is not None else r


def cast_place(w, chip):
    n_layers, r, c = w.shape
    tr = _rows_tile(r, c, budget=2 * 1024 * 1024 // n_layers)

    def body(chip_ref, w_ref, *o_refs):
        for l in range(n_layers):
            o_refs[l][...] = w_ref[l].astype(BF16)

    return list(pl.pallas_call(
        body, name="cast_place",
        grid_spec=pltpu.PrefetchScalarGridSpec(
            num_scalar_prefetch=1, grid=(r // tr,),
            in_specs=[pl.BlockSpec((n_layers, tr, c), lambda i, ch: (0, i, 0))],
            out_specs=[pl.BlockSpec((None, tr, c), lambda i, ch: (ch[0], i, 0))] * n_layers),
        out_shape=[jax.ShapeDtypeStruct((N_CHIPS, r, c), BF16)] * n_layers,
        compiler_params=_params(("arbitrary",)),
    )(chip, w))


def adamw(w, g, m, v):
    shape = w.shape
    w2, g2, m2, v2 = (_as_rows(t) for t in (w, g, m, v))
    r, c = w2.shape
    tr = _rows_tile(r, c, budget=1024 * 1024)
    c1 = 1.0 - ADAM_B1 ** ADAM_STEP
    c2 = 1.0 - ADAM_B2 ** ADAM_STEP

    def body(w_ref, g_ref, m_ref, v_ref, d_ref, mo_ref, vo_ref):
        gv = g_ref[...]
        mn = ADAM_B1 * m_ref[...] + (1.0 - ADAM_B1) * gv
        vn = ADAM_B2 * v_ref[...] + (1.0 - ADAM_B2) * (gv * gv)
        mo_ref[...] = mn
        vo_ref[...] = vn
        d_ref[...] = -ADAM_LR * ((mn / c1) / (jnp.sqrt(vn / c2) + ADAM_EPS) + ADAM_WD * w_ref[...])

    spec = pl.BlockSpec((tr, c), lambda i: (i, 0))
    outs = pl.pallas_call(
        body, name="adamw", grid=(r // tr,), in_specs=[spec] * 4, out_specs=[spec] * 3,
        out_shape=[jax.ShapeDtypeStruct((r, c), F32)] * 3, compiler_params=_params(("arbitrary",)),
    )(w2, g2, m2, v2)
    return tuple(o.reshape(shape) for o in outs)


def sum_devices(a):
    n, r, c = a.shape
    tr = _rows_tile(r, c, budget=512 * 1024)

    def body(a_ref, o_ref):
        acc = a_ref[0]
        for j in range(1, n):
            acc = acc + a_ref[j]
        o_ref[...] = acc

    return pl.pallas_call(
        body, name="sum_devices", grid=(r // tr,),
        in_specs=[pl.BlockSpec((n, tr, c), lambda i: (0, i, 0))], out_specs=pl.BlockSpec((tr, c), lambda i: (i, 0)),
        out_shape=jax.ShapeDtypeStruct((r, c), F32), compiler_params=_params(("arbitrary",)),
    )(a)


def _split_axis(r, c):
    if (r // 2) % BF16_ROWS == 0 and r % 2 == 0:
        return 0
    assert c % (2 * LANES) == 0, (r, c)
    return 1


def _half_shape(r, c):
    return (r // 2, c) if _split_axis(r, c) == 0 else (r, c // 2)


def _half_at(ref, lead, which):
    r, c = ref.shape[-2:]
    if _split_axis(r, c) == 0:
        return ref.at[(*lead, pl.ds(which * (r // 2), r // 2), slice(None))]
    return ref.at[(*lead, slice(None), pl.ds(which * (c // 2), c // 2))]


def _half_spec(r, c, lead_block, imap):
    hr, hc = _half_shape(r, c)
    if _split_axis(r, c) == 0:
        return pl.BlockSpec((*lead_block, hr, hc), lambda *a: (*imap(*a)[0], imap(*a)[1], 0))
    return pl.BlockSpec((*lead_block, hr, hc), lambda *a: (*imap(*a)[0], 0, imap(*a)[1]))


def pair_add(g, ra, half):
    n_sl, r, c = g.shape
    hr, hc = _half_shape(r, c)

    def body(h_ref, g_ref, ra_ref, p_ref, pb_ref):
        p = g_ref[...] + ra_ref[...]
        p_ref[...] = p
        pb_ref[...] = p.astype(BF16)

    mine = pl.BlockSpec((None, hr, hc), lambda k, h: (k, 0, 0))
    return pl.pallas_call(
        body, name="pair_add",
        grid_spec=pltpu.PrefetchScalarGridSpec(
            num_scalar_prefetch=1, grid=(n_sl,),
            in_specs=[_half_spec(r, c, (None,), lambda k, h: ((k,), h[0])), mine], out_specs=[mine, mine]),
        out_shape=[jax.ShapeDtypeStruct((n_sl, hr, hc), F32), jax.ShapeDtypeStruct((n_sl, hr, hc), BF16)],
        compiler_params=_params(("arbitrary",)),
    )(half, g, ra)


def chip_sum(p32, rb, sel, shape, acc):
    n_layers, r, c = shape
    hr, hc = _half_shape(r, c)

    def body(s_ref, p_ref, rb_ref, *rest):
        o_ref = rest[-1]
        acc_v = p_ref[...]
        for j in range(N_CHIPS - 1):
            acc_v = acc_v + rb_ref[j].astype(F32)
        o_ref[...] = acc_v

    in_specs = [pl.BlockSpec((None, hr, hc), lambda i, sr: (sr[1], 0, 0)),
                pl.BlockSpec((N_CHIPS - 1, hr, hc), lambda i, sr: (0, 0, 0))]
    args = [sel, p32, rb]
    aliases = {}
    if acc is not None:
        in_specs.append(pl.BlockSpec(memory_space=pl.ANY))
        args.append(acc)
        aliases = {3: 0}
    return pl.pallas_call(
        body, name="chip_sum",
        grid_spec=pltpu.PrefetchScalarGridSpec(
            num_scalar_prefetch=1, grid=(1,), in_specs=in_specs,
            out_specs=_half_spec(r, c, (None,), lambda i, sr: ((sr[2],), sr[0]))),
        out_shape=jax.ShapeDtypeStruct((n_layers, r, c), F32),
        input_output_aliases=aliases,
        compiler_params=_params(("arbitrary",)),
    )(*args)


def _me():
    return lax.axis_index("x"), lax.axis_index("y"), lax.axis_index("c")


def _flip(v, bit):
    return 1 - v if bit else v


def exchange8(xs, bcast):
    blk = xs.shape if bcast else xs.shape[1:]

    def body(x_ref, o_ref, send_sems, recv_sems, loc_sem):
        mx, my, mc = _me()
        me = 4 * mx + 2 * my + mc
        src = (lambda j: x_ref) if bcast else (lambda j: x_ref.at[j])
        loc = pltpu.make_async_copy(src(me), o_ref.at[me], loc_sem)
        loc.start()
        copies = []
        for o in range(1, N_DEV):
            px, py, pc = _flip(mx, o & 4), _flip(my, o & 2), _flip(mc, o & 1)
            cp = pltpu.make_async_remote_copy(
                src_ref=src(4 * px + 2 * py + pc), dst_ref=o_ref.at[me],
                send_sem=send_sems.at[o - 1], recv_sem=recv_sems.at[o - 1],
                device_id=(px, py, pc), device_id_type=MESH)
            cp.start()
            copies.append(cp)
        for cp in copies:
            cp.wait()
        loc.wait()

    return pl.pallas_call(
        body, name="exchange8_gather" if bcast else "exchange8_a2a",
        in_specs=[pl.BlockSpec(memory_space=pltpu.VMEM)], out_specs=pl.BlockSpec(memory_space=pltpu.VMEM),
        out_shape=jax.ShapeDtypeStruct((N_DEV,) + tuple(blk), xs.dtype),
        scratch_shapes=[pltpu.SemaphoreType.DMA((N_DEV - 1,)), pltpu.SemaphoreType.DMA((N_DEV - 1,)), pltpu.SemaphoreType.DMA],
        compiler_params=_params(),
    )(xs)


HBM = pl.BlockSpec(memory_space=pltpu.HBM)
SEM = pl.BlockSpec(memory_space=pltpu.SEMAPHORE)
EFFECT = pltpu.SideEffectType.DATAFLOW_SIDE_EFFECTING


def _hbm(a):
    return pltpu.with_memory_space_constraint(a, pltpu.HBM)


def _ici_copy(land, o, send_sem, recv_sem, sending):
    mx, my, mc = _me()
    px, py = _flip(mx, o & 2), _flip(my, o & 1)
    mine = _half_at(land, (2 * mx + my,), mc)
    return pltpu.make_async_remote_copy(
        src_ref=mine, dst_ref=mine if sending else _half_at(land, (2 * px + py,), mc),
        send_sem=send_sem, recv_sem=recv_sem, device_id=(px, py, mc), device_id_type=MESH)


N_PEERS = N_CHIPS - 1
DMA_SEM = pltpu.SemaphoreType.DMA(())


def gather_start(lands, after):
    n_layers, n = len(lands), len(lands[0])
    flat = [a for layer in lands for a in layer]
    n_in = n * n_layers
    n_sem = 2 * n_layers * N_PEERS

    def body(*refs):
        land = refs[:n_in]
        sems = refs[n_in + 1:n_in + 1 + n_sem]
        token = refs[-1]
        for l in range(n_layers):
            for t in range(n):
                for o in range(1, N_CHIPS):
                    send_sem = sems[(2 * l) * N_PEERS + o - 1]
                    recv_sem = sems[(2 * l + 1) * N_PEERS + o - 1]
                    _ici_copy(land[l * n + t], o, send_sem, recv_sem, True).start()
        token[...] = jnp.zeros_like(token)

    outs = pl.pallas_call(
        body, name="gather_start",
        in_specs=[HBM] * n_in + [pl.BlockSpec(memory_space=pl.ANY)],
        out_specs=[SEM] * n_sem + [HBM] * n_in + [pl.BlockSpec(memory_space=pltpu.VMEM)],
        out_shape=[DMA_SEM] * n_sem + [pltpu.HBM(a.shape, a.dtype) for a in flat]
        + [jax.ShapeDtypeStruct((8, LANES), F32)],
        input_output_aliases={i: i + n_sem for i in range(n_in)},
        compiler_params=pltpu.CompilerParams(has_side_effects=EFFECT),
    )(*[_hbm(a) for a in flat], after)
    sems = [(list(outs[(2 * l) * N_PEERS:(2 * l + 1) * N_PEERS]), list(outs[(2 * l + 1) * N_PEERS:(2 * l + 2) * N_PEERS]))
            for l in range(n_layers)]
    lands_thru = [list(outs[n_sem + l * n:n_sem + (l + 1) * n]) for l in range(n_layers)]
    return sems, lands_thru, outs[-1]


def gather_wait(layer, sems, lands, after):
    n = len(lands)
    send_sems, recv_sems = sems

    def body(*refs):
        land = refs[:n]
        send_r = refs[n:n + N_PEERS]
        recv_r = refs[n + N_PEERS:n + 2 * N_PEERS]
        for t in range(n):
            for o in range(1, N_CHIPS):
                _ici_copy(land[t], o, send_r[o - 1], recv_r[o - 1], True).wait_send()
                _ici_copy(land[t], o, send_r[o - 1], recv_r[o - 1], False).wait_recv()

    return list(pl.pallas_call(
        body, name=f"gather_wait_{layer}",
        in_specs=[HBM] * n + [SEM] * (2 * N_PEERS) + [pl.BlockSpec(memory_space=pl.ANY)],
        out_specs=[HBM] * n,
        out_shape=[pltpu.HBM(a.shape, a.dtype) for a in lands],
        input_output_aliases={i: i for i in range(n)},
        compiler_params=pltpu.CompilerParams(has_side_effects=EFFECT),
    )(*lands, *send_sems, *recv_sems, after))


def gather_forward(lands):
    n = len(lands)

    def body(*refs):
        dst = refs[n:2 * n]
        send_sems, recv_sems = refs[2 * n:]
        mx, my, mc = _me()
        fwds = []
        for t in range(n):
            for o in range(1, N_CHIPS):
                slot = 2 * _flip(mx, o & 2) + _flip(my, o & 1)
                mine = _half_at(dst[t], (slot,), mc)
                theirs = _half_at(dst[t], (slot,), 1 - mc)
                cp = pltpu.make_async_remote_copy(
                    src_ref=mine, dst_ref=mine, send_sem=send_sems.at[t, o - 1], recv_sem=recv_sems.at[t, o - 1],
                    device_id=(mx, my, 1 - mc), device_id_type=MESH)
                cp.start()
                fwds.append((cp, pltpu.make_async_remote_copy(
                    src_ref=theirs, dst_ref=theirs, send_sem=send_sems.at[t, o - 1], recv_sem=recv_sems.at[t, o - 1],
                    device_id=(mx, my, 1 - mc), device_id_type=MESH)))
        for cp, arrival in fwds:
            cp.wait_send()
            arrival.wait_recv()

    any_spec = pl.BlockSpec(memory_space=pl.ANY)
    return list(pl.pallas_call(
        body, name="gather_forward",
        in_specs=[any_spec] * n, out_specs=[any_spec] * n,
        out_shape=[jax.ShapeDtypeStruct(a.shape, a.dtype) for a in lands],
        input_output_aliases={t: t for t in range(n)},
        scratch_shapes=[pltpu.SemaphoreType.DMA((n, N_CHIPS - 1)), pltpu.SemaphoreType.DMA((n, N_CHIPS - 1))],
        compiler_params=_params(),
    )(*lands))


def _scatter_copy(src, land, o, send_sem, recv_sem):
    mx, my, mc = _me()
    px, py = _flip(mx, o & 2), _flip(my, o & 1)
    return pltpu.make_async_remote_copy(
        src_ref=src.at[2 * px + py], dst_ref=land.at[o - 1],
        send_sem=send_sem, recv_sem=recv_sem, device_id=(px, py, mc), device_id_type=MESH)


def scatter_start(pbs, layer, after):
    n = len(pbs)
    lands = [lax.empty((N_CHIPS - 1,) + p.shape[1:], p.dtype) for p in pbs]

    def body(*refs):
        src = refs[:n]
        land = refs[n:2 * n]
        send_sems = refs[2 * n + 1:2 * n + 1 + N_PEERS]
        recv_sems = refs[2 * n + 1 + N_PEERS:2 * n + 1 + 2 * N_PEERS]
        token = refs[-1]
        for t in range(n):
            for o in range(1, N_CHIPS):
                _scatter_copy(src[t], land[t], o, send_sems[o - 1], recv_sems[o - 1]).start()
        token[...] = jnp.zeros_like(token)

    n_sem = 2 * N_PEERS
    arrs = list(pbs) + lands
    outs = pl.pallas_call(
        body, name=f"scatter_start_{layer}",
        in_specs=[HBM] * (2 * n) + [pl.BlockSpec(memory_space=pl.ANY)],
        out_specs=[SEM] * n_sem + [HBM] * (2 * n) + [pl.BlockSpec(memory_space=pltpu.VMEM)],
        out_shape=[DMA_SEM] * n_sem + [pltpu.HBM(a.shape, a.dtype) for a in arrs]
        + [jax.ShapeDtypeStruct((8, LANES), F32)],
        input_output_aliases={i: i + n_sem for i in range(2 * n)},
        compiler_params=pltpu.CompilerParams(has_side_effects=EFFECT),
    )(*[_hbm(a) for a in arrs], after)
    return (list(outs[:N_PEERS]), list(outs[N_PEERS:n_sem]), list(outs[n_sem:n_sem + n]),
            list(outs[n_sem + n:n_sem + 2 * n]), outs[-1])


def scatter_wait(layer, send_sems, recv_sems, pbs, lands, after):
    n = len(pbs)

    def body(*refs):
        src = refs[:n]
        land = refs[n:2 * n]
        send_r = refs[2 * n:2 * n + N_PEERS]
        recv_r = refs[2 * n + N_PEERS:2 * n + 2 * N_PEERS]
        for t in range(n):
            for o in range(1, N_CHIPS):
                cp = _scatter_copy(src[t], land[t], o, send_r[o - 1], recv_r[o - 1])
                cp.wait_send()
                cp.wait_recv()

    arrs = list(pbs) + list(lands)
    outs = pl.pallas_call(
        body, name=f"scatter_wait_{layer}",
        in_specs=[HBM] * (2 * n) + [SEM] * (2 * N_PEERS) + [pl.BlockSpec(memory_space=pl.ANY)],
        out_specs=[HBM] * (2 * n),
        out_shape=[pltpu.HBM(a.shape, a.dtype) for a in arrs],
        input_output_aliases={i: i for i in range(2 * n)},
        compiler_params=pltpu.CompilerParams(has_side_effects=EFFECT),
    )(*arrs, *send_sems, *recv_sems, after)
    return list(outs[n:])


def pair_send_halves(gs):
    n = len(gs)

    def body(*refs):
        src = refs[:n]
        dst = refs[n:2 * n]
        send_sems, recv_sems = refs[2 * n:]
        mx, my, mc = _me()
        copies = []
        for t in range(n):
            cp = pltpu.make_async_remote_copy(
                src_ref=_half_at(src[t], (slice(None),), 1 - mc), dst_ref=dst[t],
                send_sem=send_sems.at[t], recv_sem=recv_sems.at[t],
                device_id=(mx, my, 1 - mc), device_id_type=MESH)
            cp.start()
            copies.append(cp)
        for cp in copies:
            cp.wait()

    any_spec = pl.BlockSpec(memory_space=pl.ANY)
    return pl.pallas_call(
        body, name="pair_send_halves",
        in_specs=[any_spec] * n, out_specs=[any_spec] * n,
        out_shape=[jax.ShapeDtypeStruct((g.shape[0],) + _half_shape(*g.shape[1:]), g.dtype) for g in gs],
        scratch_shapes=[pltpu.SemaphoreType.DMA((n,)), pltpu.SemaphoreType.DMA((n,))],
        compiler_params=_params(),
    )(*gs)


def pair_fill_halves(fs):
    n = len(fs)

    def body(*refs):
        dst = refs[n:2 * n]
        send_sems, recv_sems = refs[2 * n:]
        mx, my, mc = _me()
        copies = []
        for t in range(n):
            mine = _half_at(dst[t], (slice(None),), mc)
            theirs = _half_at(dst[t], (slice(None),), 1 - mc)
            cp = pltpu.make_async_remote_copy(
                src_ref=mine, dst_ref=mine, send_sem=send_sems.at[t], recv_sem=recv_sems.at[t],
                device_id=(mx, my, 1 - mc), device_id_type=MESH)
            cp.start()
            copies.append((cp, pltpu.make_async_remote_copy(
                src_ref=theirs, dst_ref=theirs, send_sem=send_sems.at[t], recv_sem=recv_sems.at[t],
                device_id=(mx, my, 1 - mc), device_id_type=MESH)))
        for cp, arrival in copies:
            cp.wait_send()
            arrival.wait_recv()

    any_spec = pl.BlockSpec(memory_space=pl.ANY)
    return pl.pallas_call(
        body, name="pair_fill_halves",
        in_specs=[any_spec] * n, out_specs=[any_spec] * n,
        out_shape=[jax.ShapeDtypeStruct(f.shape, f.dtype) for f in fs],
        input_output_aliases={t: t for t in range(n)},
        scratch_shapes=[pltpu.SemaphoreType.DMA((n,)), pltpu.SemaphoreType.DMA((n,))],
        compiler_params=_params(),
    )(*fs)


def _pack_rows(parts, d):
    rows, spans = [], []
    at = 0
    for p in parts:
        flat = p.reshape(-1)
        n_rows = -(-flat.shape[0] // (8 * d)) * 8
        flat = jnp.pad(flat, (0, n_rows * d - flat.shape[0]))
        rows.append(flat.reshape(n_rows, d))
        spans.append((at, p.shape))
        at += n_rows
    return jnp.concatenate(rows, axis=0), spans


def _unpack_rows(packed, spans):
    out = []
    for at, shape in spans:
        n = math.prod(shape)
        d = packed.shape[1]
        n_rows = -(-n // d)
        out.append(packed[at:at + n_rows].reshape(-1)[:n].reshape(shape))
    return out


def _rotate_half_matrix():
    half = QK_ROPE // 2
    idx = jnp.arange(QK_ROPE)
    src = jnp.where(idx < half, idx + half, idx - half)
    sign = jnp.where(idx < half, -1.0, 1.0)
    return (jnp.zeros((QK_ROPE, QK_ROPE), F32).at[src, idx].set(sign)).astype(BF16)


def kernel(x, c, positions, ada_w, ada_b, ffn1_norm, ffn1_w_gate, ffn1_w_up, ffn1_w_down, mix_norm, w_in, pool_w, pool_scale, q_a_norm, w_q_b, kv_a_norm, w_kv_b, w_out, ffn2_norm, ffn2_w_gate, ffn2_w_up, ffn2_w_down, final_norm, loss_target, m_ada_w, m_ada_b, m_ffn1_norm, m_ffn1_w_gate, m_ffn1_w_up, m_ffn1_w_down, m_mix_norm, m_w_in, m_pool_w, m_pool_scale, m_q_a_norm, m_w_q_b, m_kv_a_norm, m_w_kv_b, m_w_out, m_ffn2_norm, m_ffn2_w_gate, m_ffn2_w_up, m_ffn2_w_down, m_final_norm, v_ada_w, v_ada_b, v_ffn1_norm, v_ffn1_w_gate, v_ffn1_w_up, v_ffn1_w_down, v_mix_norm, v_w_in, v_pool_w, v_pool_scale, v_q_a_norm, v_w_q_b, v_kv_a_norm, v_w_kv_b, v_w_out, v_ffn2_norm, v_ffn2_w_gate, v_ffn2_w_up, v_ffn2_w_down, v_final_norm):
    mx, my, mc = _me()
    chip = 2 * mx + my
    half = jnp.reshape(mc, (1,)).astype(jnp.int32)
    chip1 = jnp.reshape(chip, (1,)).astype(jnp.int32)
    n_layers, d, ada_cols = ada_w.shape
    xt = x[0]
    tgt = loss_target[0]

    inv_freq = 1.0 / (ROPE_THETA ** (jnp.arange(0, QK_ROPE, 2, dtype=F32) / QK_ROPE))
    ang = positions[0].astype(F32)[:, None] * inv_freq
    ang = jnp.concatenate([ang, ang], axis=-1)
    cos, sin = jnp.cos(ang), jnp.sin(ang)
    rot = _rotate_half_matrix()
    rot_t = rot.T

    c_all = exchange8(c, True).reshape(N_DEV, d)
    c16 = jnp.pad(c_all, ((0, 8), (0, 0)))
    ada_b_loc = lax.dynamic_slice_in_dim(ada_b, chip * ada_cols, ada_cols, axis=1).reshape(n_layers, 1, ada_cols)
    mod_part = ada_fwd(c16, ada_w, ada_b_loc)[:, :N_DEV]
    mod_got = exchange8(jnp.transpose(mod_part, (1, 0, 2)), False)
    mod = jnp.transpose(mod_got.reshape(N_CHIPS, 2, n_layers, ada_cols)[:, 0], (1, 0, 2))
    mod = mod.reshape(n_layers, 9, 1, d)

    tr = lambda a: jnp.transpose(a, (0, 2, 1))
    local = [tr(ffn1_w_gate), tr(ffn1_w_up), ffn1_w_down, tr(w_in), tr(w_q_b), w_kv_b, w_out,
             tr(ffn2_w_gate), tr(ffn2_w_up), ffn2_w_down]
    placed = [cast_place(w, chip1) for w in local]
    lands = [[placed[t][l] for t in range(len(local))] for l in range(n_layers)]
    g_sems, lands_fly, g_token = gather_start(lands, mod)
    gathered = []

    row = lambda a, l: a[l].reshape(1, -1)
    saved = []
    for l in range(n_layers):
        landed = gather_wait(l, g_sems[l], lands_fly[l], xt if l else g_token)
        gathered.append(gather_forward(landed))
        g1, u1, d1, win, wq, wkv, wout, g2, u2, d2 = gathered[l]
        win = win.reshape(-1, d)
        sv = dict(x0=xt)
        xt, sv["h1"], sv["gate1"], sv["up1"], sv["y1"] = ffn_fwd(
            xt, row(ffn1_norm, l), mod[l, 0], mod[l, 1], mod[l, 2], g1, u1, d1)
        sv["x1"] = xt
        sv["h2"], u, cq, ckv, kr = mix_in_fwd(xt, row(mix_norm, l), mod[l, 3], mod[l, 4], win)
        sv["cq"], sv["ckv"] = cq, ckv
        yp, sv["diff"] = pool_fwd(u, pool_w[l], row(pool_scale, l))
        qn, qr, kn, krr, vv, sv["ql"], sv["kvl"] = mla_qkv_fwd(
            cq, ckv, kr, row(q_a_norm, l), row(kv_a_norm, l), wq, wkv, cos, sin, rot)
        sv["qkv"] = (qn, qr, kn, krr, vv)
        om = attn_fwd(qn, qr, kn, krr, vv)
        xt, sv["ycat"], sv["y2"] = out_proj_fwd(yp, om, wout, xt, mod[l, 5])
        sv["x2"] = xt
        xt, sv["h3"], sv["gate3"], sv["up3"], sv["y3"] = ffn_fwd(
            xt, row(ffn2_norm, l), mod[l, 6], mod[l, 7], mod[l, 8], g2, u2, d2)
        saved.append(sv)

    loss_vec, dx, d_final_norm = final_loss(xt, final_norm.reshape(1, d), tgt)
    loss = lax.psum(loss_vec[0, 0], ("x", "y", "c"))

    none = [None] * n_layers
    dmods, dnorm1, dnorm2, dnorm3 = list(none), list(none), list(none), list(none)
    dpw, dps, dqan_l, dkvan_l = list(none), list(none), list(none), list(none)
    reduced = [None] * len(local)
    in_flight = None
    sel_of = lambda l: jnp.stack([mc, chip, jnp.asarray(l, mc.dtype)]).astype(jnp.int32)

    def finish(job, after):
        l_j, shapes, sums, (s_send, s_recv, pbs_fly, lands, _) = job
        parts = scatter_wait(l_j, s_send, s_recv, pbs_fly, lands, after)
        return [chip_sum(p32, rb, sel_of(l_j), (n_layers,) + shp, acc)
                for (p32, _), rb, shp, acc in zip(sums, parts, shapes, reduced)]

    for l in reversed(range(n_layers)):
        sv = saved[l]
        g1, u1, d1, win, wq, wkv, wout, g2, u2, d2 = gathered[l]
        win = win.reshape(-1, d)
        gt3 = mod[l, 8] if in_flight is None else mod[l, 8] + in_flight[3][4][0, 0]
        dy, a, dgt, dup = ffn_bwd_act(dx, sv["gate3"], sv["up3"], gt3, d2)
        dx, dvec3 = ffn_bwd_in(dx, sv["x2"], sv["y3"], dgt, dup, row(ffn2_norm, l), mod[l, 7], g2, u2)
        g_g2, g_u2, g_d2 = tn_mm(dgt, sv["h3"][None]), tn_mm(dup, sv["h3"][None]), tn_mm(a, dy[None])
        dy2, dyp, dom, dg2 = out_proj_bwd(dx, sv["y2"], mod[l, 5], wout)
        g_wout = tn_mm(sv["ycat"], dy2[None])
        qn, qr, kn, krr, vv = sv["qkv"]
        dqn, dqr, dkn, dkr, dvv = attn_bwd(qn, qr, kn, krr, vv, dom)
        dcq, dckv, dkr_in, gq, gkv, dqan_l[l], dkvan_l[l] = mla_qkv_bwd(
            dqn, dqr, dkn, dkr, dvv, sv["cq"], sv["ckv"], row(q_a_norm, l), row(kv_a_norm, l),
            wq, wkv, cos, sin, rot_t)
        g_wq, g_wkv = tn_mm(gq, sv["ql"][None]), tn_mm(sv["kvl"][None], gkv)
        du, dpw[l], dps[l] = pool_bwd(dyp, sv["diff"], pool_w[l], row(pool_scale, l))
        dx, dz, dvec2 = mix_in_bwd(dx, du, dcq, dckv, dkr_in, sv["x1"], row(mix_norm, l), mod[l, 4], win)
        g_win = tn_mm(dz[None], sv["h2"][None]).reshape(N_CHIPS, -1, d)
        dy, a, dgt, dup = ffn_bwd_act(dx, sv["gate1"], sv["up1"], mod[l, 2], d1)
        dx, dvec1 = ffn_bwd_in(dx, sv["x0"], sv["y1"], dgt, dup, row(ffn1_norm, l), mod[l, 1], g1, u1)
        g_g1, g_u1, g_d1 = tn_mm(dgt, sv["h1"][None]), tn_mm(dup, sv["h1"][None]), tn_mm(a, dy[None])
        dmods[l] = jnp.concatenate([dvec1[0:3], dvec2[0:2], dg2, dvec3[0:3]], axis=0)
        dnorm1[l], dnorm2[l], dnorm3[l] = dvec1[3], dvec2[3], dvec3[3]
        if l == 0:
            small_parts = [jnp.stack(dmods), jnp.stack(dnorm1), jnp.stack(dnorm2), jnp.stack(dnorm3), d_final_norm,
                           jnp.stack(dps), jnp.stack(dqan_l), jnp.stack(dkvan_l), jnp.stack(dpw)]
            packed, spans = _pack_rows(small_parts, d)
            gathered_small = exchange8(packed, True)

        full = [g_g1, g_u1, g_d1, g_win, g_wq, g_wkv, g_wout, g_g2, g_u2, g_d2]
        got = pair_send_halves(full)
        sums = [pair_add(g, ra, half) for g, ra in zip(full, got)]
        started = scatter_start([pb for _, pb in sums], l, gathered_small if l == 0 else dx)
        if in_flight is not None:
            reduced = finish(in_flight, dx)
        in_flight = (l, [g.shape[1:] for g in full], sums, started)

    total = sum_devices(gathered_small)
    (g_ada_b, g_n1, g_n2, g_n3, g_fn, g_ps, g_qan, g_kvan, g_pw) = _unpack_rows(total, spans)
    dmod_all = gathered_small[:, :9 * n_layers].reshape(N_DEV, n_layers, 9 * d)
    dmod_loc = lax.dynamic_slice_in_dim(dmod_all, chip * ada_cols, ada_cols, axis=2)
    dmod16 = jnp.pad(jnp.transpose(dmod_loc, (1, 0, 2)), ((0, 0), (0, 8), (0, 0)))
    g_ada_w = ada_bwd(c16, dmod16)

    grads = [g_ada_w, g_ada_b, g_n1, None, None, None, g_n2, None, g_pw, g_ps, g_qan, None, g_kvan, None, None, g_n3,
             None, None, None, g_fn]
    weights = [ada_w, ada_b, ffn1_norm, ffn1_w_gate, ffn1_w_up, ffn1_w_down, mix_norm, w_in, pool_w, pool_scale,
               q_a_norm, w_q_b, kv_a_norm, w_kv_b, w_out, ffn2_norm, ffn2_w_gate, ffn2_w_up, ffn2_w_down, final_norm]
    ms = [m_ada_w, m_ada_b, m_ffn1_norm, m_ffn1_w_gate, m_ffn1_w_up, m_ffn1_w_down, m_mix_norm, m_w_in, m_pool_w,
          m_pool_scale, m_q_a_norm, m_w_q_b, m_kv_a_norm, m_w_kv_b, m_w_out, m_ffn2_norm, m_ffn2_w_gate, m_ffn2_w_up,
          m_ffn2_w_down, m_final_norm]
    vs = [v_ada_w, v_ada_b, v_ffn1_norm, v_ffn1_w_gate, v_ffn1_w_up, v_ffn1_w_down, v_mix_norm, v_w_in, v_pool_w,
          v_pool_scale, v_q_a_norm, v_w_q_b, v_kv_a_norm, v_w_kv_b, v_w_out, v_ffn2_norm, v_ffn2_w_gate, v_ffn2_w_up,
          v_ffn2_w_down, v_final_norm]
    transposed = (3, 4, 7, 11, 16, 17)
    outs = [None] * len(weights)
    for i, (w, g, m, v) in enumerate(zip(weights, grads, ms, vs)):
        if g is not None:
            g = g.reshape(w.shape)
            outs[i] = (g,) + adamw(w, g, m, v)
    reduced = finish(in_flight, outs[0][1])
    g_local = iter(pair_fill_halves(reduced))
    for i, (w, g, m, v) in enumerate(zip(weights, grads, ms, vs)):
        if g is None:
            g = next(g_local)
            if i in transposed:
                outs[i] = tuple(tr(t) for t in (g,) + adamw(tr(w), g, tr(m), tr(v)))
            else:
                outs[i] = (g,) + adamw(w, g, m, v)
    return (loss, dx.reshape(x.shape), *[t[0] for t in outs], *[t[1] for t in outs], *[t[2] for t in outs],
            *[t[3] for t in outs])
```

```python
import math

import jax
import jax.numpy as jnp
from jax import lax
from jax.experimental import pallas as pl
from jax.experimental.pallas import tpu as pltpu

F32 = jnp.float32
BF16 = jnp.bfloat16
MESH = pl.DeviceIdType.MESH

EPS = 1e-6
ROPE_THETA = 10000.0
N_HEADS = 4
QK_NOPE = 128
QK_ROPE = 64
V_HEAD = 128
POOL_WINDOWS = (2, 4, 8, 16)
POOL_GC = 128
POOL_WIDTH = POOL_GC * len(POOL_WINDOWS)
Q_LORA = 384
KV_LORA = 256
SOFTMAX_SCALE = 1.0 / math.sqrt(QK_NOPE + QK_ROPE)
N_CHIPS = 4
N_DEV = 8

ADAM_LR = 0.001
ADAM_B1 = 0.9
ADAM_B2 = 0.999
ADAM_EPS = 1e-08
ADAM_WD = 0.01
ADAM_STEP = 10

ROW_TILE = 512
ATT_TILE = 256
VMEM_LIMIT = 56 * 1024 * 1024
BF16_ROWS = 16
LANES = 128


def _params(sem=None, vmem=VMEM_LIMIT):
    return pltpu.CompilerParams(dimension_semantics=sem, vmem_limit_bytes=vmem)


def _dot(a, b):
    return jnp.dot(a, b, preferred_element_type=F32)


def _dot_nt(a, b):
    return lax.dot_general(a, b, (((1,), (1,)), ((), ())), preferred_element_type=F32)


def _dot_tn(a, b):
    return lax.dot_general(a, b, (((0,), (0,)), ((), ())), preferred_element_type=F32)


def _dot_exact(t, perm):
    t1 = t.astype(BF16)
    r1 = t - t1.astype(F32)
    t2 = r1.astype(BF16)
    t3 = (r1 - t2.astype(F32)).astype(BF16)
    return _dot(t1, perm) + _dot(t2, perm) + _dot(t3, perm)


def _sum0(a):
    return jnp.sum(a, axis=0, keepdims=True)


def _rms(xt):
    r = lax.rsqrt(jnp.mean(xt * xt, axis=-1, keepdims=True) + EPS)
    return xt * r, r


def _rms_bwd(dy, xt, g):
    xhat, r = _rms(xt)
    dxhat = dy * g
    dx = r * (dxhat - xhat * jnp.mean(dxhat * xhat, axis=-1, keepdims=True))
    return dx, _sum0(dy * xhat)


def _normmod_bwd(dh, xt, gn, sc):
    xhat, _ = _rms(xt)
    dn = dh * (1.0 + sc)
    dx, dgn = _rms_bwd(dn, xt, gn)
    return dx, _sum0(dh), _sum0(dh * (xhat * gn)), dgn


def _row_tile(s):
    return min(s, ROW_TILE)


def _full(shape):
    n = len(shape)
    return pl.BlockSpec(shape, lambda *_: (0,) * n)


def _resident(shape):
    n = len(shape)
    return pl.BlockSpec(shape, lambda *_: (0,) * n, pipeline_mode=pl.Buffered(1))


def ffn_fwd(x, gn, sh, sc, gt, wg, wu, wd):
    s, d = x.shape
    k_chunks, fs, _ = wg.shape
    tm = _row_tile(s)

    def body(x_ref, gn_ref, sh_ref, sc_ref, gt_ref, wg_ref, wu_ref, wd_ref,
             xo_ref, h_ref, gate_ref, up_ref, y_ref):
        xt = x_ref[...]
        xhat, _ = _rms(xt)
        h = (xhat * gn_ref[...] * (1.0 + sc_ref[...]) + sh_ref[...]).astype(BF16)
        h_ref[...] = h
        y = jnp.zeros((tm, d), F32)
        for k in range(k_chunks):
            gate = _dot_nt(h, wg_ref[k])
            up = _dot_nt(h, wu_ref[k])
            gate_ref[k] = gate.astype(BF16)
            up_ref[k] = up.astype(BF16)
            y += _dot((gate * jax.nn.sigmoid(gate) * up).astype(BF16), wd_ref[k])
        y_ref[...] = y.astype(BF16)
        xo_ref[...] = xt + 0.5 * gt_ref[...] * y

    row = pl.BlockSpec((tm, d), lambda i: (i, 0))
    vec = pl.BlockSpec((1, d), lambda i: (0, 0))
    act = pl.BlockSpec((k_chunks, tm, fs), lambda i: (0, i, 0))
    return pl.pallas_call(
        body, name="ffn_fwd",
        grid=(s // tm,),
        in_specs=[row, vec, vec, vec, vec, _resident(wg.shape), _resident(wu.shape), _resident(wd.shape)],
        out_specs=[row, row, act, act, row],
        out_shape=[jax.ShapeDtypeStruct((s, d), F32), jax.ShapeDtypeStruct((s, d), BF16),
                   jax.ShapeDtypeStruct((k_chunks, s, fs), BF16), jax.ShapeDtypeStruct((k_chunks, s, fs), BF16),
                   jax.ShapeDtypeStruct((s, d), BF16)],
        compiler_params=_params(("arbitrary",)),
    )(x, gn, sh, sc, gt, wg, wu, wd)


def ffn_bwd_act(dxn, gate, up, gt, wd):
    s, d = dxn.shape
    k_chunks, fs, _ = wd.shape
    tm = _row_tile(s)

    def body(dxn_ref, gate_ref, up_ref, gt_ref, wd_ref, dy_ref, a_ref, dgate_ref, dup_ref):
        dy = (0.5 * gt_ref[...] * dxn_ref[...]).astype(BF16)
        dy_ref[...] = dy
        for k in range(k_chunks):
            da = _dot_nt(dy, wd_ref[k])
            g = gate_ref[k].astype(F32)
            u = up_ref[k].astype(F32)
            sg = jax.nn.sigmoid(g)
            sl = g * sg
            a_ref[k] = (sl * u).astype(BF16)
            dgate_ref[k] = (da * u * (sg * (1.0 + g * (1.0 - sg)))).astype(BF16)
            dup_ref[k] = (da * sl).astype(BF16)

    row = pl.BlockSpec((tm, d), lambda i: (i, 0))
    act = pl.BlockSpec((k_chunks, tm, fs), lambda i: (0, i, 0))
    act_shape = jax.ShapeDtypeStruct((k_chunks, s, fs), BF16)
    return pl.pallas_call(
        body, name="ffn_bwd_act",
        grid=(s // tm,),
        in_specs=[row, act, act, pl.BlockSpec((1, d), lambda i: (0, 0)), _resident(wd.shape)],
        out_specs=[row, act, act, act],
        out_shape=[jax.ShapeDtypeStruct((s, d), BF16), act_shape, act_shape, act_shape],
        compiler_params=_params(("arbitrary",)),
    )(dxn, gate, up, gt, wd)


def ffn_bwd_in(dxn, x, y, dgate, dup, gn, sc, wg, wu):
    s, d = x.shape
    k_chunks, fs, _ = wg.shape
    tm = _row_tile(s)

    def body(dxn_ref, x_ref, y_ref, dgate_ref, dup_ref, gn_ref, sc_ref, wg_ref, wu_ref, dx_ref, dvec_ref):
        i = pl.program_id(0)

        @pl.when(i == 0)
        def _():
            dvec_ref[...] = jnp.zeros_like(dvec_ref)

        dh = jnp.zeros((tm, d), F32)
        for k in range(k_chunks):
            dh += _dot(dgate_ref[k], wg_ref[k]) + _dot(dup_ref[k], wu_ref[k])
        dxn_t = dxn_ref[...]
        dx, dsh, dsc, dgn = _normmod_bwd(dh, x_ref[...], gn_ref[...], sc_ref[...])
        dx_ref[...] = dx + dxn_t
        dvec_ref[0:1, :] += dsh
        dvec_ref[1:2, :] += dsc
        dvec_ref[2:3, :] += _sum0(0.5 * dxn_t * y_ref[...].astype(F32))
        dvec_ref[3:4, :] += dgn

    row = pl.BlockSpec((tm, d), lambda i: (i, 0))
    vec = pl.BlockSpec((1, d), lambda i: (0, 0))
    act = pl.BlockSpec((k_chunks, tm, fs), lambda i: (0, i, 0))
    return pl.pallas_call(
        body, name="ffn_bwd_in",
        grid=(s // tm,),
        in_specs=[row, row, row, act, act, vec, vec, _resident(wg.shape), _resident(wu.shape)],
        out_specs=[row, pl.BlockSpec((8, d), lambda i: (0, 0))],
        out_shape=[jax.ShapeDtypeStruct((s, d), F32), jax.ShapeDtypeStruct((8, d), F32)],
        compiler_params=_params(("arbitrary",)),
    )(dxn, x, y, dgate, dup, gn, sc, wg, wu)


def tn_mm(a, b):
    ga, s, m = a.shape
    gb, _, n = b.shape
    g = max(ga, gb)

    def body(a_ref, b_ref, o_ref):
        o_ref[...] = _dot_tn(a_ref[...], b_ref[...])

    a_spec = pl.BlockSpec((None, s, m), (lambda gi: (gi, 0, 0)) if ga > 1 else (lambda gi: (0, 0, 0)))
    b_spec = pl.BlockSpec((None, s, n), (lambda gi: (gi, 0, 0)) if gb > 1 else (lambda gi: (0, 0, 0)))
    return pl.pallas_call(
        body, name="tn_mm",
        grid=(g,), in_specs=[a_spec, b_spec], out_specs=pl.BlockSpec((None, m, n), lambda gi: (gi, 0, 0)),
        out_shape=jax.ShapeDtypeStruct((g, m, n), F32),
        compiler_params=_params(("arbitrary",)),
    )(a, b)


def mix_in_fwd(x, gn, sh, sc, w_in_t):
    s, d = x.shape
    tm = _row_tile(s)
    o1, o2, o3 = POOL_WIDTH, POOL_WIDTH + Q_LORA, POOL_WIDTH + Q_LORA + KV_LORA

    def body(x_ref, gn_ref, sh_ref, sc_ref, w_ref, h_ref, u_ref, cq_ref, ckv_ref, kr_ref):
        xhat, _ = _rms(x_ref[...])
        h = (xhat * gn_ref[...] * (1.0 + sc_ref[...]) + sh_ref[...]).astype(BF16)
        h_ref[...] = h
        z = _dot_nt(h, w_ref[0:o3, :])
        u_ref[...] = z[:, 0:o1]
        cq_ref[...] = z[:, o1:o2]
        ckv_ref[...] = z[:, o2:o3]
        kr_ref[...] = _dot_nt(h, w_ref[o3:, :])

    row = lambda w: pl.BlockSpec((tm, w), lambda i: (i, 0))
    vec = pl.BlockSpec((1, d), lambda i: (0, 0))
    return pl.pallas_call(
        body, name="mix_in_fwd",
        grid=(s // tm,),
        in_specs=[row(d), vec, vec, vec, _full(w_in_t.shape)],
        out_specs=[row(d), row(POOL_WIDTH), row(Q_LORA), row(KV_LORA), row(QK_ROPE)],
        out_shape=[jax.ShapeDtypeStruct((s, d), BF16), jax.ShapeDtypeStruct((s, POOL_WIDTH), F32),
                   jax.ShapeDtypeStruct((s, Q_LORA), F32), jax.ShapeDtypeStruct((s, KV_LORA), F32),
                   jax.ShapeDtypeStruct((s, QK_ROPE), F32)],
        compiler_params=_params(("arbitrary",)),
    )(x, gn, sh, sc, w_in_t)


def mix_in_bwd(dxn, du, dcq, dckv, dkr, x, gn, sc, w_in_t):
    s, d = x.shape
    tm = _row_tile(s)
    o1, o2, o3 = POOL_WIDTH, POOL_WIDTH + Q_LORA, POOL_WIDTH + Q_LORA + KV_LORA
    n_z = w_in_t.shape[0]

    def body(dxn_ref, du_ref, dcq_ref, dckv_ref, dkr_ref, x_ref, gn_ref, sc_ref, w_ref, dx_ref, dz_ref, dvec_ref):
        i = pl.program_id(0)

        @pl.when(i == 0)
        def _():
            dvec_ref[...] = jnp.zeros_like(dvec_ref)

        dub = du_ref[...].astype(BF16)
        dqb = dcq_ref[...].astype(BF16)
        dkb = dckv_ref[...].astype(BF16)
        drb = dkr_ref[...].astype(BF16)
        dz_ref[:, 0:o1] = dub
        dz_ref[:, o1:o2] = dqb
        dz_ref[:, o2:o3] = dkb
        dz_ref[:, o3:] = drb
        dh = (_dot(dub, w_ref[0:o1, :]) + _dot(dqb, w_ref[o1:o2, :]) + _dot(dkb, w_ref[o2:o3, :])
              + _dot(drb, w_ref[o3:, :]))
        dx, dsh, dsc, dgn = _normmod_bwd(dh, x_ref[...], gn_ref[...], sc_ref[...])
        dx_ref[...] = dx + dxn_ref[...]
        dvec_ref[0:1, :] += dsh
        dvec_ref[1:2, :] += dsc
        dvec_ref[3:4, :] += dgn

    row = lambda w: pl.BlockSpec((tm, w), lambda i: (i, 0))
    vec = pl.BlockSpec((1, d), lambda i: (0, 0))
    return pl.pallas_call(
        body, name="mix_in_bwd",
        grid=(s // tm,),
        in_specs=[row(d), row(POOL_WIDTH), row(Q_LORA), row(KV_LORA), row(QK_ROPE), row(d), vec, vec,
                  _full(w_in_t.shape)],
        out_specs=[row(d), row(n_z), pl.BlockSpec((8, d), lambda i: (0, 0))],
        out_shape=[jax.ShapeDtypeStruct((s, d), F32), jax.ShapeDtypeStruct((s, n_z), BF16),
                   jax.ShapeDtypeStruct((8, d), F32)],
        compiler_params=_params(("arbitrary",)),
    )(dxn, du, dcq, dckv, dkr, x, gn, sc, w_in_t)


def _window_sum(a, w, rows, forward):
    s = a.shape[0]
    step = 1
    while step < w:
        if forward:
            shifted = jnp.where(rows < s - step, pltpu.roll(a, s - step, 0), 0.0)
        else:
            shifted = jnp.where(rows >= step, pltpu.roll(a, step, 0), 0.0)
        a = a + shifted
        step *= 2
    return a


def pool_fwd(u, pool_w, pool_scale):
    s = u.shape[0]

    def body(u_ref, w_ref, sc_ref, y_ref, diff_ref):
        rows = lax.broadcasted_iota(jnp.int32, (s, POOL_GC), 0)
        for g, w in enumerate(POOL_WINDOWS):
            cols = slice(g * POOL_GC, (g + 1) * POOL_GC)
            ug = u_ref[:, cols]
            cnt = jnp.minimum(rows + 1, w).astype(F32)
            diff = (_window_sum(ug, w, rows, False) / cnt - ug).astype(BF16)
            diff_ref[:, cols] = diff
            y_ref[:, cols] = _dot(diff, w_ref[g].astype(BF16)) * sc_ref[:, cols]

    return pl.pallas_call(
        body, name="pool_fwd",
        out_shape=[jax.ShapeDtypeStruct(u.shape, F32), jax.ShapeDtypeStruct(u.shape, BF16)],
        compiler_params=_params(),
    )(u, pool_w, pool_scale)


def pool_bwd(dy, diff, pool_w, pool_scale):
    s = dy.shape[0]

    def body(dy_ref, diff_ref, w_ref, sc_ref, du_ref, dw_ref, dsc_ref):
        rows = lax.broadcasted_iota(jnp.int32, (s, POOL_GC), 0)
        for g, w in enumerate(POOL_WINDOWS):
            cols = slice(g * POOL_GC, (g + 1) * POOL_GC)
            dyg = dy_ref[:, cols]
            diff = diff_ref[:, cols]
            wb = w_ref[g].astype(BF16)
            dsc_ref[:, cols] = _sum0(dyg * _dot(diff, wb))
            dys = (dyg * sc_ref[:, cols]).astype(BF16)
            dw_ref[g] = _dot_tn(diff, dys)
            ddiff = _dot_nt(dys, wb)
            cnt = jnp.minimum(rows + 1, w).astype(F32)
            du_ref[:, cols] = _window_sum(ddiff / cnt, w, rows, True) - ddiff

    return pl.pallas_call(
        body, name="pool_bwd",
        out_shape=[jax.ShapeDtypeStruct(dy.shape, F32), jax.ShapeDtypeStruct(pool_w.shape, F32),
                   jax.ShapeDtypeStruct(pool_scale.shape, F32)],
        compiler_params=_params(),
    )(dy, diff, pool_w, pool_scale)


def mla_qkv_fwd(cq, ckv, kr, qan, kvan, wq, wkv, cos, sin, rot):
    s = cq.shape[0]
    tm = _row_tile(s)

    def body(cq_ref, ckv_ref, kr_ref, qan_ref, kvan_ref, wq_ref, wkv_ref, cos_ref, sin_ref, rot_ref,
             q_ref, k_ref, v_ref, ql_ref, kvl_ref):
        cos_t = cos_ref[...]
        sin_t = sin_ref[...]
        perm = rot_ref[...]

        def rope(t):
            return t * cos_t + _dot_exact(t, perm) * sin_t

        qhat, _ = _rms(cq_ref[...])
        ql = (qhat * qan_ref[...]).astype(BF16)
        ql_ref[...] = ql
        khat, _ = _rms(ckv_ref[...])
        kvl = (khat * kvan_ref[...]).astype(BF16)
        kvl_ref[...] = kvl
        krr = rope(kr_ref[...]).astype(BF16)
        for h in range(N_HEADS):
            q = _dot_nt(ql, wq_ref[h])
            q_ref[h, :, 0:QK_NOPE] = q[:, 0:QK_NOPE].astype(BF16)
            q_ref[h, :, QK_NOPE:] = rope(q[:, QK_NOPE:]).astype(BF16)
            kv = _dot(kvl, wkv_ref[h])
            k_ref[h, :, 0:QK_NOPE] = kv[:, 0:QK_NOPE].astype(BF16)
            k_ref[h, :, QK_NOPE:] = krr
            v_ref[h] = kv[:, QK_NOPE:].astype(BF16)

    row = lambda w: pl.BlockSpec((tm, w), lambda i: (i, 0))
    hrow = lambda w: pl.BlockSpec((N_HEADS, tm, w), lambda i: (0, i, 0))
    qk = QK_NOPE + QK_ROPE
    return pl.pallas_call(
        body, name="mla_qkv_fwd",
        grid=(s // tm,),
        in_specs=[row(Q_LORA), row(KV_LORA), row(QK_ROPE), _full(qan.shape), _full(kvan.shape),
                  _full(wq.shape), _full(wkv.shape), row(QK_ROPE), row(QK_ROPE), _full(rot.shape)],
        out_specs=[hrow(qk), hrow(qk), hrow(V_HEAD), row(Q_LORA), row(KV_LORA)],
        out_shape=[jax.ShapeDtypeStruct((N_HEADS, s, qk), BF16), jax.ShapeDtypeStruct((N_HEADS, s, qk), BF16),
                   jax.ShapeDtypeStruct((N_HEADS, s, V_HEAD), BF16), jax.ShapeDtypeStruct((s, Q_LORA), BF16),
                   jax.ShapeDtypeStruct((s, KV_LORA), BF16)],
        compiler_params=_params(("arbitrary",)),
    )(cq, ckv, kr, qan, kvan, wq, wkv, cos, sin, rot)


def _attn_probs(q_ref, k_ref, qi, tq):
    n = (qi + 1) * tq
    rows = slice(qi * tq, n)
    sc = _dot_nt(q_ref[rows, :], k_ref[0:n, :]) * SOFTMAX_SCALE
    qpos = qi * tq + lax.broadcasted_iota(jnp.int32, (tq, n), 0)
    kpos = lax.broadcasted_iota(jnp.int32, (tq, n), 1)
    sc = jnp.where(qpos >= kpos, sc, -1e30)
    e = jnp.exp(sc - jnp.max(sc, axis=-1, keepdims=True))
    return e / jnp.sum(e, axis=-1, keepdims=True)


def attn_fwd(q, k, v):
    nh, s, qk = q.shape
    tq = min(s, ATT_TILE)

    def body(q_ref, k_ref, v_ref, o_ref):
        for qi in range(s // tq):
            n = (qi + 1) * tq
            p = _attn_probs(q_ref, k_ref, qi, tq).astype(BF16)
            o_ref[qi * tq:n, :] = _dot(p, v_ref[0:n, :])

    head = lambda w: pl.BlockSpec((None, s, w), lambda h: (h, 0, 0))
    return pl.pallas_call(
        body, name="attn_fwd",
        grid=(nh,),
        in_specs=[head(qk), head(qk), head(V_HEAD)],
        out_specs=pl.BlockSpec((s, V_HEAD), lambda h: (0, h)),
        out_shape=jax.ShapeDtypeStruct((s, nh * V_HEAD), F32),
        compiler_params=_params(("arbitrary",)),
    )(q, k, v)


def attn_bwd(q, k, v, do):
    nh, s, qk = q.shape
    tq = min(s, ATT_TILE)

    def body(q_ref, k_ref, v_ref, do_ref, dq_ref, dk_ref, dv_ref):
        dk_ref[...] = jnp.zeros_like(dk_ref)
        dv_ref[...] = jnp.zeros_like(dv_ref)
        for qi in range(s // tq):
            n = (qi + 1) * tq
            rows = slice(qi * tq, n)
            p = _attn_probs(q_ref, k_ref, qi, tq)
            dob = do_ref[rows, :].astype(BF16)
            dp = _dot_nt(dob, v_ref[0:n, :])
            ds = (p * (dp - jnp.sum(p * dp, axis=-1, keepdims=True)) * SOFTMAX_SCALE).astype(BF16)
            dq_ref[rows, :] = _dot(ds, k_ref[0:n, :])
            dk_ref[0:n, :] += _dot_tn(ds, q_ref[rows, :])
            dv_ref[0:n, :] += _dot_tn(p.astype(BF16), dob)

    head = lambda w: pl.BlockSpec((None, s, w), lambda h: (h, 0, 0))
    return pl.pallas_call(
        body, name="attn_bwd",
        grid=(nh,),
        in_specs=[head(qk), head(qk), head(V_HEAD), pl.BlockSpec((s, V_HEAD), lambda h: (0, h))],
        out_specs=[head(qk), head(qk), head(V_HEAD)],
        out_shape=[jax.ShapeDtypeStruct((nh, s, qk), F32), jax.ShapeDtypeStruct((nh, s, qk), F32),
                   jax.ShapeDtypeStruct((nh, s, V_HEAD), F32)],
        compiler_params=_params(("arbitrary",)),
    )(q, k, v, do)


def mla_qkv_bwd(dq, dk, dv, cq, ckv, qan, kvan, wq, wkv, cos, sin, rot_t):
    s = cq.shape[0]
    tm = _row_tile(s)

    def body(dq_ref, dk_ref, dv_ref, cq_ref, ckv_ref, qan_ref, kvan_ref,
             wq_ref, wkv_ref, cos_ref, sin_ref, rot_ref,
             dcq_ref, dckv_ref, dkro_ref, gq_ref, gkv_ref, dqan_ref, dkvan_ref):
        i = pl.program_id(0)

        @pl.when(i == 0)
        def _():
            dqan_ref[...] = jnp.zeros_like(dqan_ref)
            dkvan_ref[...] = jnp.zeros_like(dkvan_ref)

        cos_t = cos_ref[...]
        sin_t = sin_ref[...]
        perm_t = rot_ref[...]

        def unrope(t):
            return t * cos_t + _dot_exact(t * sin_t, perm_t)

        acc_q = jnp.zeros((tm, Q_LORA), F32)
        acc_kv = jnp.zeros((tm, KV_LORA), F32)
        dkr_sum = jnp.zeros((tm, QK_ROPE), F32)
        for h in range(N_HEADS):
            dq_h = dq_ref[h]
            a = dq_h[:, 0:QK_NOPE].astype(BF16)
            b = unrope(dq_h[:, QK_NOPE:]).astype(BF16)
            gq_ref[h, :, 0:QK_NOPE] = a
            gq_ref[h, :, QK_NOPE:] = b
            wq_h = wq_ref[h]
            acc_q += _dot(a, wq_h[0:QK_NOPE, :]) + _dot(b, wq_h[QK_NOPE:, :])
            dk_h = dk_ref[h]
            dk = dk_h[:, 0:QK_NOPE].astype(BF16)
            dvv = dv_ref[h].astype(BF16)
            gkv_ref[h, :, 0:QK_NOPE] = dk
            gkv_ref[h, :, QK_NOPE:] = dvv
            wkv_h = wkv_ref[h]
            acc_kv += _dot_nt(dk, wkv_h[:, 0:QK_NOPE]) + _dot_nt(dvv, wkv_h[:, QK_NOPE:])
            dkr_sum += dk_h[:, QK_NOPE:]
        dkro_ref[...] = unrope(dkr_sum)
        dcq, dqan = _rms_bwd(acc_q, cq_ref[...], qan_ref[...])
        dcq_ref[...] = dcq
        dqan_ref[...] += dqan
        dckv, dkvan = _rms_bwd(acc_kv, ckv_ref[...], kvan_ref[...])
        dckv_ref[...] = dckv
        dkvan_ref[...] += dkvan

    row = lambda w: pl.BlockSpec((tm, w), lambda i: (i, 0))
    hrow = lambda w: pl.BlockSpec((N_HEADS, tm, w), lambda i: (0, i, 0))
    return pl.pallas_call(
        body, name="mla_qkv_bwd",
        grid=(s // tm,),
        in_specs=[hrow(QK_NOPE + QK_ROPE), hrow(QK_NOPE + QK_ROPE), hrow(V_HEAD),
                  row(Q_LORA), row(KV_LORA), _full(qan.shape), _full(kvan.shape),
                  _full(wq.shape), _full(wkv.shape), row(QK_ROPE), row(QK_ROPE), _full(rot_t.shape)],
        out_specs=[row(Q_LORA), row(KV_LORA), row(QK_ROPE), hrow(QK_NOPE + QK_ROPE), hrow(QK_NOPE + V_HEAD),
                   _full(qan.shape), _full(kvan.shape)],
        out_shape=[jax.ShapeDtypeStruct((s, Q_LORA), F32), jax.ShapeDtypeStruct((s, KV_LORA), F32),
                   jax.ShapeDtypeStruct((s, QK_ROPE), F32),
                   jax.ShapeDtypeStruct((N_HEADS, s, QK_NOPE + QK_ROPE), BF16),
                   jax.ShapeDtypeStruct((N_HEADS, s, QK_NOPE + V_HEAD), BF16),
                   jax.ShapeDtypeStruct(qan.shape, F32), jax.ShapeDtypeStruct(kvan.shape, F32)],
        compiler_params=_params(("arbitrary",)),
    )(dq, dk, dv, cq, ckv, qan, kvan, wq, wkv, cos, sin, rot_t)


def out_proj_fwd(yp, om, w_out, x, gt):
    s, d = x.shape
    n_sh, rs, _ = w_out.shape
    tm = _row_tile(s)
    per = POOL_WIDTH // rs

    def body(yp_ref, om_ref, w_ref, x_ref, gt_ref, xo_ref, ycat_ref, y_ref):
        y = jnp.zeros((tm, d), F32)
        for j in range(n_sh):
            src = yp_ref if j < per else om_ref
            part = src[:, (j % per) * rs:(j % per + 1) * rs].astype(BF16)
            ycat_ref[j] = part
            y += _dot(part, w_ref[j])
        y_ref[...] = y.astype(BF16)
        xo_ref[...] = x_ref[...] + gt_ref[...] * y

    row = lambda w: pl.BlockSpec((tm, w), lambda i: (i, 0))
    return pl.pallas_call(
        body, name="out_proj_fwd",
        grid=(s // tm,),
        in_specs=[row(POOL_WIDTH), row(POOL_WIDTH), _full(w_out.shape), row(d), pl.BlockSpec((1, d), lambda i: (0, 0))],
        out_specs=[row(d), pl.BlockSpec((n_sh, tm, rs), lambda i: (0, i, 0)), row(d)],
        out_shape=[jax.ShapeDtypeStruct((s, d), F32), jax.ShapeDtypeStruct((n_sh, s, rs), BF16),
                   jax.ShapeDtypeStruct((s, d), BF16)],
        compiler_params=_params(("arbitrary",)),
    )(yp, om, w_out, x, gt)


def out_proj_bwd(dxn, y, gt, w_out):
    s, d = dxn.shape
    n_sh, rs, _ = w_out.shape
    tm = _row_tile(s)
    per = POOL_WIDTH // rs

    def body(dxn_ref, y_ref, gt_ref, w_ref, dy_ref, dyp_ref, dom_ref, dgt_ref):
        i = pl.program_id(0)

        @pl.when(i == 0)
        def _():
            dgt_ref[...] = jnp.zeros_like(dgt_ref)

        dxn_t = dxn_ref[...]
        dy = (gt_ref[...] * dxn_t).astype(BF16)
        dy_ref[...] = dy
        dgt_ref[...] += _sum0(dxn_t * y_ref[...].astype(F32))
        for j in range(n_sh):
            dst = dyp_ref if j < per else dom_ref
            dst[:, (j % per) * rs:(j % per + 1) * rs] = _dot_nt(dy, w_ref[j])

    row = lambda w: pl.BlockSpec((tm, w), lambda i: (i, 0))
    vec = pl.BlockSpec((1, d), lambda i: (0, 0))
    return pl.pallas_call(
        body, name="out_proj_bwd",
        grid=(s // tm,),
        in_specs=[row(d), row(d), vec, _full(w_out.shape)],
        out_specs=[row(d), row(POOL_WIDTH), row(POOL_WIDTH), vec],
        out_shape=[jax.ShapeDtypeStruct((s, d), BF16), jax.ShapeDtypeStruct((s, POOL_WIDTH), F32),
                   jax.ShapeDtypeStruct((s, POOL_WIDTH), F32), jax.ShapeDtypeStruct((1, d), F32)],
        compiler_params=_params(("arbitrary",)),
    )(dxn, y, gt, w_out)


def final_loss(x, gn, tgt):
    s, d = x.shape
    tm = _row_tile(s)

    def body(x_ref, gn_ref, t_ref, loss_ref, dx_ref, dgn_ref):
        i = pl.program_id(0)

        @pl.when(i == 0)
        def _():
            loss_ref[...] = jnp.zeros_like(loss_ref)
            dgn_ref[...] = jnp.zeros_like(dgn_ref)

        xt = x_ref[...]
        g = gn_ref[...]
        xhat, _ = _rms(xt)
        err = xhat * g - t_ref[...]
        per_tok = jnp.mean(err * err, axis=-1, keepdims=True)
        loss_ref[...] += jnp.broadcast_to(0.5 * _sum0(per_tok), loss_ref.shape)
        dx, dgn = _rms_bwd(err * (1.0 / d), xt, g)
        dx_ref[...] = dx
        dgn_ref[...] += dgn

    row = pl.BlockSpec((tm, d), lambda i: (i, 0))
    vec = pl.BlockSpec((1, d), lambda i: (0, 0))
    return pl.pallas_call(
        body, name="final_loss",
        grid=(s // tm,),
        in_specs=[row, vec, row],
        out_specs=[pl.BlockSpec((1, LANES), lambda i: (0, 0)), row, vec],
        out_shape=[jax.ShapeDtypeStruct((1, LANES), F32), jax.ShapeDtypeStruct((s, d), F32),
                   jax.ShapeDtypeStruct((1, d), F32)],
        compiler_params=_params(("arbitrary",)),
    )(x, gn, tgt)


def _col_tile(cols):
    return 768 if cols % 768 == 0 else cols


def ada_fwd(c16, ada_w, ada_b_loc):
    n_layers, d, cols = ada_w.shape
    tn = _col_tile(cols)

    def body(c_ref, w_ref, b_ref, o_ref):
        cv = c_ref[...]
        ca = (cv * jax.nn.sigmoid(cv)).astype(BF16)
        o_ref[...] = _dot(ca, w_ref[...].astype(BF16)) + b_ref[...]

    return pl.pallas_call(
        body, name="ada_fwd",
        grid=(n_layers, cols // tn),
        in_specs=[pl.BlockSpec((16, d), lambda l, j: (0, 0)), pl.BlockSpec((None, d, tn), lambda l, j: (l, 0, j)),
                  pl.BlockSpec((None, 1, tn), lambda l, j: (l, 0, j))],
        out_specs=pl.BlockSpec((None, 16, tn), lambda l, j: (l, 0, j)),
        out_shape=jax.ShapeDtypeStruct((n_layers, 16, cols), F32),
        compiler_params=_params(("arbitrary", "arbitrary")),
    )(c16, ada_w, ada_b_loc)


def ada_bwd(c16, dmod16):
    n_layers, _, cols = dmod16.shape
    d = c16.shape[1]
    tn = _col_tile(cols)

    def body(c_ref, g_ref, o_ref):
        cv = c_ref[...]
        ca = (cv * jax.nn.sigmoid(cv)).astype(BF16)
        o_ref[...] = _dot_tn(ca, g_ref[...].astype(BF16))

    return pl.pallas_call(
        body, name="ada_bwd",
        grid=(n_layers, cols // tn),
        in_specs=[pl.BlockSpec((16, d), lambda l, j: (0, 0)), pl.BlockSpec((None, 16, tn), lambda l, j: (l, 0, j))],
        out_specs=pl.BlockSpec((None, d, tn), lambda l, j: (l, 0, j)),
        out_shape=jax.ShapeDtypeStruct((n_layers, d, cols), F32),
        compiler_params=_params(("arbitrary", "arbitrary")),
    )(c16, dmod16)


def _as_rows(a):
    if a.ndim == 1:
        return a.reshape(1, a.shape[0])
    return a.reshape(-1, a.shape[-1])


def _rows_tile(r, c, itemsize=4, budget=2 * 1024 * 1024):
    if r * c * itemsize <= budget:
        return r
    best = None
    t = BF16_ROWS
    while t < r:
        if r % t == 0 and t * c * itemsize <= budget:
            best = t
        t += BF16_ROWS
    return best if best is not None else r


def cast_place(w, chip):
    n_layers, r, c = w.shape
    tr = _rows_tile(r, c, budget=2 * 1024 * 1024 // n_layers)

    def body(chip_ref, w_ref, *o_refs):
        for l in range(n_layers):
            o_refs[l][...] = w_ref[l].astype(BF16)

    return list(pl.pallas_call(
        body, name="cast_place",
        grid_spec=pltpu.PrefetchScalarGridSpec(
            num_scalar_prefetch=1, grid=(r // tr,),
            in_specs=[pl.BlockSpec((n_layers, tr, c), lambda i, ch: (0, i, 0))],
            out_specs=[pl.BlockSpec((None, tr, c), lambda i, ch: (ch[0], i, 0))] * n_layers),
        out_shape=[jax.ShapeDtypeStruct((N_CHIPS, r, c), BF16)] * n_layers,
        compiler_params=_params(("arbitrary",)),
    )(chip, w))


def adamw(w, g, m, v):
    shape = w.shape
    w2, g2, m2, v2 = (_as_rows(t) for t in (w, g, m, v))
    r, c = w2.shape
    tr = _rows_tile(r, c, budget=1024 * 1024)
    c1 = 1.0 - ADAM_B1 ** ADAM_STEP
    c2 = 1.0 - ADAM_B2 ** ADAM_STEP

    def body(w_ref, g_ref, m_ref, v_ref, d_ref, mo_ref, vo_ref):
        gv = g_ref[...]
        mn = ADAM_B1 * m_ref[...] + (1.0 - ADAM_B1) * gv
        vn = ADAM_B2 * v_ref[...] + (1.0 - ADAM_B2) * (gv * gv)
        mo_ref[...] = mn
        vo_ref[...] = vn
        d_ref[...] = -ADAM_LR * ((mn / c1) / (jnp.sqrt(vn / c2) + ADAM_EPS) + ADAM_WD * w_ref[...])

    spec = pl.BlockSpec((tr, c), lambda i: (i, 0))
    outs = pl.pallas_call(
        body, name="adamw", grid=(r // tr,), in_specs=[spec] * 4, out_specs=[spec] * 3,
        out_shape=[jax.ShapeDtypeStruct((r, c), F32)] * 3, compiler_params=_params(("arbitrary",)),
    )(w2, g2, m2, v2)
    return tuple(o.reshape(shape) for o in outs)


def sum_devices(a):
    n, r, c = a.shape
    tr = _rows_tile(r, c, budget=512 * 1024)

    def body(a_ref, o_ref):
        acc = a_ref[0]
        for j in range(1, n):
            acc = acc + a_ref[j]
        o_ref[...] = acc

    return pl.pallas_call(
        body, name="sum_devices", grid=(r // tr,),
        in_specs=[pl.BlockSpec((n, tr, c), lambda i: (0, i, 0))], out_specs=pl.BlockSpec((tr, c), lambda i: (i, 0)),
        out_shape=jax.ShapeDtypeStruct((r, c), F32), compiler_params=_params(("arbitrary",)),
    )(a)


def _split_axis(r, c):
    if (r // 2) % BF16_ROWS == 0 and r % 2 == 0:
        return 0
    assert c % (2 * LANES) == 0, (r, c)
    return 1


def _half_shape(r, c):
    return (r // 2, c) if _split_axis(r, c) == 0 else (r, c // 2)


def _half_at(ref, lead, which):
    r, c = ref.shape[-2:]
    if _split_axis(r, c) == 0:
        return ref.at[(*lead, pl.ds(which * (r // 2), r // 2), slice(None))]
    return ref.at[(*lead, slice(None), pl.ds(which * (c // 2), c // 2))]


def _half_spec(r, c, lead_block, imap):
    hr, hc = _half_shape(r, c)
    if _split_axis(r, c) == 0:
        return pl.BlockSpec((*lead_block, hr, hc), lambda *a: (*imap(*a)[0], imap(*a)[1], 0))
    return pl.BlockSpec((*lead_block, hr, hc), lambda *a: (*imap(*a)[0], 0, imap(*a)[1]))


def pair_add(g, ra, half):
    n_sl, r, c = g.shape
    hr, hc = _half_shape(r, c)

    def body(h_ref, g_ref, ra_ref, p_ref, pb_ref):
        p = g_ref[...] + ra_ref[...]
        p_ref[...] = p
        pb_ref[...] = p.astype(BF16)

    mine = pl.BlockSpec((None, hr, hc), lambda k, h: (k, 0, 0))
    return pl.pallas_call(
        body, name="pair_add",
        grid_spec=pltpu.PrefetchScalarGridSpec(
            num_scalar_prefetch=1, grid=(n_sl,),
            in_specs=[_half_spec(r, c, (None,), lambda k, h: ((k,), h[0])), mine], out_specs=[mine, mine]),
        out_shape=[jax.ShapeDtypeStruct((n_sl, hr, hc), F32), jax.ShapeDtypeStruct((n_sl, hr, hc), BF16)],
        compiler_params=_params(("arbitrary",)),
    )(half, g, ra)


def chip_sum(p32, rb, sel, shape, acc):
    n_layers, r, c = shape
    hr, hc = _half_shape(r, c)

    def body(s_ref, p_ref, rb_ref, *rest):
        o_ref = rest[-1]
        acc_v = p_ref[...]
        for j in range(N_CHIPS - 1):
            acc_v = acc_v + rb_ref[j].astype(F32)
        o_ref[...] = acc_v

    in_specs = [pl.BlockSpec((None, hr, hc), lambda i, sr: (sr[1], 0, 0)),
                pl.BlockSpec((N_CHIPS - 1, hr, hc), lambda i, sr: (0, 0, 0))]
    args = [sel, p32, rb]
    aliases = {}
    if acc is not None:
        in_specs.append(pl.BlockSpec(memory_space=pl.ANY))
        args.append(acc)
        aliases = {3: 0}
    return pl.pallas_call(
        body, name="chip_sum",
        grid_spec=pltpu.PrefetchScalarGridSpec(
            num_scalar_prefetch=1, grid=(1,), in_specs=in_specs,
            out_specs=_half_spec(r, c, (None,), lambda i, sr: ((sr[2],), sr[0]))),
        out_shape=jax.ShapeDtypeStruct((n_layers, r, c), F32),
        input_output_aliases=aliases,
        compiler_params=_params(("arbitrary",)),
    )(*args)


def _me():
    return lax.axis_index("x"), lax.axis_index("y"), lax.axis_index("c")


def _flip(v, bit):
    return 1 - v if bit else v


def exchange8(xs, bcast):
    blk = xs.shape if bcast else xs.shape[1:]

    def body(x_ref, o_ref, send_sems, recv_sems, loc_sem):
        mx, my, mc = _me()
        me = 4 * mx + 2 * my + mc
        src = (lambda j: x_ref) if bcast else (lambda j: x_ref.at[j])
        loc = pltpu.make_async_copy(src(me), o_ref.at[me], loc_sem)
        loc.start()
        copies = []
        for o in range(1, N_DEV):
            px, py, pc = _flip(mx, o & 4), _flip(my, o & 2), _flip(mc, o & 1)
            cp = pltpu.make_async_remote_copy(
                src_ref=src(4 * px + 2 * py + pc), dst_ref=o_ref.at[me],
                send_sem=send_sems.at[o - 1], recv_sem=recv_sems.at[o - 1],
                device_id=(px, py, pc), device_id_type=MESH)
            cp.start()
            copies.append(cp)
        for cp in copies:
            cp.wait()
        loc.wait()

    return pl.pallas_call(
        body, name="exchange8_gather" if bcast else "exchange8_a2a",
        in_specs=[pl.BlockSpec(memory_space=pltpu.VMEM)], out_specs=pl.BlockSpec(memory_space=pltpu.VMEM),
        out_shape=jax.ShapeDtypeStruct((N_DEV,) + tuple(blk), xs.dtype),
        scratch_shapes=[pltpu.SemaphoreType.DMA((N_DEV - 1,)), pltpu.SemaphoreType.DMA((N_DEV - 1,)), pltpu.SemaphoreType.DMA],
        compiler_params=_params(),
    )(xs)


HBM = pl.BlockSpec(memory_space=pltpu.HBM)
SEM = pl.BlockSpec(memory_space=pltpu.SEMAPHORE)
EFFECT = pltpu.SideEffectType.DATAFLOW_SIDE_EFFECTING


def _hbm(a):
    return pltpu.with_memory_space_constraint(a, pltpu.HBM)


def _ici_copy(land, o, send_sem, recv_sem, sending):
    mx, my, mc = _me()
    px, py = _flip(mx, o & 2), _flip(my, o & 1)
    mine = _half_at(land, (2 * mx + my,), mc)
    return pltpu.make_async_remote_copy(
        src_ref=mine, dst_ref=mine if sending else _half_at(land, (2 * px + py,), mc),
        send_sem=send_sem, recv_sem=recv_sem, device_id=(px, py, mc), device_id_type=MESH)


N_PEERS = N_CHIPS - 1
DMA_SEM = pltpu.SemaphoreType.DMA(())


def gather_start(lands, groups, after):
    n_layers, n = len(lands), len(lands[0])
    flat = [a for layer in lands for a in layer]
    n_in = n * n_layers
    n_grp = len(groups)
    n_sem = 2 * n_layers * n_grp * N_PEERS
    first = lambda l, g, recv: ((l * n_grp + g) * 2 + recv) * N_PEERS

    def body(*refs):
        land = refs[:n_in]
        sems = refs[n_in + 1:n_in + 1 + n_sem]
        token = refs[-1]
        for l in range(n_layers):
            for g, members in enumerate(groups):
                for t in members:
                    for o in range(1, N_CHIPS):
                        _ici_copy(land[l * n + t], o, sems[first(l, g, 0) + o - 1], sems[first(l, g, 1) + o - 1],
                                  True).start()
        token[...] = jnp.zeros_like(token)

    outs = pl.pallas_call(
        body, name="gather_start",
        in_specs=[HBM] * n_in + [pl.BlockSpec(memory_space=pl.ANY)],
        out_specs=[SEM] * n_sem + [HBM] * n_in + [pl.BlockSpec(memory_space=pltpu.VMEM)],
        out_shape=[DMA_SEM] * n_sem + [pltpu.HBM(a.shape, a.dtype) for a in flat]
        + [jax.ShapeDtypeStruct((8, LANES), F32)],
        input_output_aliases={i: i + n_sem for i in range(n_in)},
        compiler_params=pltpu.CompilerParams(has_side_effects=EFFECT),
    )(*[_hbm(a) for a in flat], after)
    sems = [[(list(outs[first(l, g, 0):first(l, g, 0) + N_PEERS]), list(outs[first(l, g, 1):first(l, g, 1) + N_PEERS]))
             for g in range(n_grp)] for l in range(n_layers)]
    lands_thru = [list(outs[n_sem + l * n:n_sem + (l + 1) * n]) for l in range(n_layers)]
    return sems, lands_thru, outs[-1]


def gather_wait(tag, sems, lands, after):
    n = len(lands)
    send_sems, recv_sems = sems

    def body(*refs):
        land = refs[:n]
        send_r = refs[n:n + N_PEERS]
        recv_r = refs[n + N_PEERS:n + 2 * N_PEERS]
        for t in range(n):
            for o in range(1, N_CHIPS):
                _ici_copy(land[t], o, send_r[o - 1], recv_r[o - 1], True).wait_send()
                _ici_copy(land[t], o, send_r[o - 1], recv_r[o - 1], False).wait_recv()

    return list(pl.pallas_call(
        body, name=f"gather_wait_{tag}",
        in_specs=[HBM] * n + [SEM] * (2 * N_PEERS) + [pl.BlockSpec(memory_space=pl.ANY)],
        out_specs=[HBM] * n,
        out_shape=[pltpu.HBM(a.shape, a.dtype) for a in lands],
        input_output_aliases={i: i for i in range(n)},
        compiler_params=pltpu.CompilerParams(has_side_effects=EFFECT),
    )(*lands, *send_sems, *recv_sems, after))


def gather_forward(lands):
    n = len(lands)

    def body(*refs):
        dst = refs[n:2 * n]
        send_sems, recv_sems = refs[2 * n:]
        mx, my, mc = _me()
        fwds = []
        for t in range(n):
            for o in range(1, N_CHIPS):
                slot = 2 * _flip(mx, o & 2) + _flip(my, o & 1)
                mine = _half_at(dst[t], (slot,), mc)
                theirs = _half_at(dst[t], (slot,), 1 - mc)
                cp = pltpu.make_async_remote_copy(
                    src_ref=mine, dst_ref=mine, send_sem=send_sems.at[t, o - 1], recv_sem=recv_sems.at[t, o - 1],
                    device_id=(mx, my, 1 - mc), device_id_type=MESH)
                cp.start()
                fwds.append((cp, pltpu.make_async_remote_copy(
                    src_ref=theirs, dst_ref=theirs, send_sem=send_sems.at[t, o - 1], recv_sem=recv_sems.at[t, o - 1],
                    device_id=(mx, my, 1 - mc), device_id_type=MESH)))
        for cp, arrival in fwds:
            cp.wait_send()
            arrival.wait_recv()

    any_spec = pl.BlockSpec(memory_space=pl.ANY)
    return list(pl.pallas_call(
        body, name="gather_forward",
        in_specs=[any_spec] * n, out_specs=[any_spec] * n,
        out_shape=[jax.ShapeDtypeStruct(a.shape, a.dtype) for a in lands],
        input_output_aliases={t: t for t in range(n)},
        scratch_shapes=[pltpu.SemaphoreType.DMA((n, N_CHIPS - 1)), pltpu.SemaphoreType.DMA((n, N_CHIPS - 1))],
        compiler_params=_params(),
    )(*lands))


def _scatter_copy(src, land, o, send_sem, recv_sem):
    mx, my, mc = _me()
    px, py = _flip(mx, o & 2), _flip(my, o & 1)
    return pltpu.make_async_remote_copy(
        src_ref=src.at[2 * px + py], dst_ref=land.at[o - 1],
        send_sem=send_sem, recv_sem=recv_sem, device_id=(px, py, mc), device_id_type=MESH)


def scatter_start(pbs, tag, after):
    n = len(pbs)
    lands = [lax.empty((N_CHIPS - 1,) + p.shape[1:], p.dtype) for p in pbs]

    def body(*refs):
        src = refs[:n]
        land = refs[n:2 * n]
        send_sems = refs[2 * n + 1:2 * n + 1 + N_PEERS]
        recv_sems = refs[2 * n + 1 + N_PEERS:2 * n + 1 + 2 * N_PEERS]
        token = refs[-1]
        for t in range(n):
            for o in range(1, N_CHIPS):
                _scatter_copy(src[t], land[t], o, send_sems[o - 1], recv_sems[o - 1]).start()
        token[...] = jnp.zeros_like(token)

    n_sem = 2 * N_PEERS
    arrs = list(pbs) + lands
    outs = pl.pallas_call(
        body, name=f"scatter_start_{tag}",
        in_specs=[HBM] * (2 * n) + [pl.BlockSpec(memory_space=pl.ANY)],
        out_specs=[SEM] * n_sem + [HBM] * (2 * n) + [pl.BlockSpec(memory_space=pltpu.VMEM)],
        out_shape=[DMA_SEM] * n_sem + [pltpu.HBM(a.shape, a.dtype) for a in arrs]
        + [jax.ShapeDtypeStruct((8, LANES), F32)],
        input_output_aliases={i: i + n_sem for i in range(2 * n)},
        compiler_params=pltpu.CompilerParams(has_side_effects=EFFECT),
    )(*[_hbm(a) for a in arrs], after)
    return (list(outs[:N_PEERS]), list(outs[N_PEERS:n_sem]), list(outs[n_sem:n_sem + n]),
            list(outs[n_sem + n:n_sem + 2 * n]), outs[-1])


def scatter_wait(tag, send_sems, recv_sems, pbs, lands, after):
    n = len(pbs)

    def body(*refs):
        src = refs[:n]
        land = refs[n:2 * n]
        send_r = refs[2 * n:2 * n + N_PEERS]
        recv_r = refs[2 * n + N_PEERS:2 * n + 2 * N_PEERS]
        for t in range(n):
            for o in range(1, N_CHIPS):
                cp = _scatter_copy(src[t], land[t], o, send_r[o - 1], recv_r[o - 1])
                cp.wait_send()
                cp.wait_recv()

    arrs = list(pbs) + list(lands)
    outs = pl.pallas_call(
        body, name=f"scatter_wait_{tag}",
        in_specs=[HBM] * (2 * n) + [SEM] * (2 * N_PEERS) + [pl.BlockSpec(memory_space=pl.ANY)],
        out_specs=[HBM] * (2 * n),
        out_shape=[pltpu.HBM(a.shape, a.dtype) for a in arrs],
        input_output_aliases={i: i for i in range(2 * n)},
        compiler_params=pltpu.CompilerParams(has_side_effects=EFFECT),
    )(*arrs, *send_sems, *recv_sems, after)
    return list(outs[n:])


def pair_send_halves(gs):
    n = len(gs)

    def body(*refs):
        src = refs[:n]
        dst = refs[n:2 * n]
        send_sems, recv_sems = refs[2 * n:]
        mx, my, mc = _me()
        copies = []
        for t in range(n):
            cp = pltpu.make_async_remote_copy(
                src_ref=_half_at(src[t], (slice(None),), 1 - mc), dst_ref=dst[t],
                send_sem=send_sems.at[t], recv_sem=recv_sems.at[t],
                device_id=(mx, my, 1 - mc), device_id_type=MESH)
            cp.start()
            copies.append(cp)
        for cp in copies:
            cp.wait()

    any_spec = pl.BlockSpec(memory_space=pl.ANY)
    return pl.pallas_call(
        body, name="pair_send_halves",
        in_specs=[any_spec] * n, out_specs=[any_spec] * n,
        out_shape=[jax.ShapeDtypeStruct((g.shape[0],) + _half_shape(*g.shape[1:]), g.dtype) for g in gs],
        scratch_shapes=[pltpu.SemaphoreType.DMA((n,)), pltpu.SemaphoreType.DMA((n,))],
        compiler_params=_params(),
    )(*gs)


def pair_fill_halves(fs):
    n = len(fs)

    def body(*refs):
        dst = refs[n:2 * n]
        send_sems, recv_sems = refs[2 * n:]
        mx, my, mc = _me()
        copies = []
        for t in range(n):
            mine = _half_at(dst[t], (slice(None),), mc)
            theirs = _half_at(dst[t], (slice(None),), 1 - mc)
            cp = pltpu.make_async_remote_copy(
                src_ref=mine, dst_ref=mine, send_sem=send_sems.at[t], recv_sem=recv_sems.at[t],
                device_id=(mx, my, 1 - mc), device_id_type=MESH)
            cp.start()
            copies.append((cp, pltpu.make_async_remote_copy(
                src_ref=theirs, dst_ref=theirs, send_sem=send_sems.at[t], recv_sem=recv_sems.at[t],
                device_id=(mx, my, 1 - mc), device_id_type=MESH)))
        for cp, arrival in copies:
            cp.wait_send()
            arrival.wait_recv()

    any_spec = pl.BlockSpec(memory_space=pl.ANY)
    return pl.pallas_call(
        body, name="pair_fill_halves",
        in_specs=[any_spec] * n, out_specs=[any_spec] * n,
        out_shape=[jax.ShapeDtypeStruct(f.shape, f.dtype) for f in fs],
        input_output_aliases={t: t for t in range(n)},
        scratch_shapes=[pltpu.SemaphoreType.DMA((n,)), pltpu.SemaphoreType.DMA((n,))],
        compiler_params=_params(),
    )(*fs)


def _pack_rows(parts, d):
    rows, spans = [], []
    at = 0
    for p in parts:
        flat = p.reshape(-1)
        n_rows = -(-flat.shape[0] // (8 * d)) * 8
        flat = jnp.pad(flat, (0, n_rows * d - flat.shape[0]))
        rows.append(flat.reshape(n_rows, d))
        spans.append((at, p.shape))
        at += n_rows
    return jnp.concatenate(rows, axis=0), spans


def _unpack_rows(packed, spans):
    out = []
    for at, shape in spans:
        n = math.prod(shape)
        d = packed.shape[1]
        n_rows = -(-n // d)
        out.append(packed[at:at + n_rows].reshape(-1)[:n].reshape(shape))
    return out


def _rotate_half_matrix():
    half = QK_ROPE // 2
    idx = jnp.arange(QK_ROPE)
    src = jnp.where(idx < half, idx + half, idx - half)
    sign = jnp.where(idx < half, -1.0, 1.0)
    return (jnp.zeros((QK_ROPE, QK_ROPE), F32).at[src, idx].set(sign)).astype(BF16)


def kernel(x, c, positions, ada_w, ada_b, ffn1_norm, ffn1_w_gate, ffn1_w_up, ffn1_w_down, mix_norm, w_in, pool_w, pool_scale, q_a_norm, w_q_b, kv_a_norm, w_kv_b, w_out, ffn2_norm, ffn2_w_gate, ffn2_w_up, ffn2_w_down, final_norm, loss_target, m_ada_w, m_ada_b, m_ffn1_norm, m_ffn1_w_gate, m_ffn1_w_up, m_ffn1_w_down, m_mix_norm, m_w_in, m_pool_w, m_pool_scale, m_q_a_norm, m_w_q_b, m_kv_a_norm, m_w_kv_b, m_w_out, m_ffn2_norm, m_ffn2_w_gate, m_ffn2_w_up, m_ffn2_w_down, m_final_norm, v_ada_w, v_ada_b, v_ffn1_norm, v_ffn1_w_gate, v_ffn1_w_up, v_ffn1_w_down, v_mix_norm, v_w_in, v_pool_w, v_pool_scale, v_q_a_norm, v_w_q_b, v_kv_a_norm, v_w_kv_b, v_w_out, v_ffn2_norm, v_ffn2_w_gate, v_ffn2_w_up, v_ffn2_w_down, v_final_norm):
    mx, my, mc = _me()
    chip = 2 * mx + my
    half = jnp.reshape(mc, (1,)).astype(jnp.int32)
    chip1 = jnp.reshape(chip, (1,)).astype(jnp.int32)
    n_layers, d, ada_cols = ada_w.shape
    xt = x[0]
    tgt = loss_target[0]

    inv_freq = 1.0 / (ROPE_THETA ** (jnp.arange(0, QK_ROPE, 2, dtype=F32) / QK_ROPE))
    ang = positions[0].astype(F32)[:, None] * inv_freq
    ang = jnp.concatenate([ang, ang], axis=-1)
    cos, sin = jnp.cos(ang), jnp.sin(ang)
    rot = _rotate_half_matrix()
    rot_t = rot.T

    c_all = exchange8(c, True).reshape(N_DEV, d)
    c16 = jnp.pad(c_all, ((0, 8), (0, 0)))
    ada_b_loc = lax.dynamic_slice_in_dim(ada_b, chip * ada_cols, ada_cols, axis=1).reshape(n_layers, 1, ada_cols)
    mod_part = ada_fwd(c16, ada_w, ada_b_loc)[:, :N_DEV]
    mod_got = exchange8(jnp.transpose(mod_part, (1, 0, 2)), False)
    mod = jnp.transpose(mod_got.reshape(N_CHIPS, 2, n_layers, ada_cols)[:, 0], (1, 0, 2))
    mod = mod.reshape(n_layers, 9, 1, d)

    tr = lambda a: jnp.transpose(a, (0, 2, 1))
    local = [tr(ffn1_w_gate), tr(ffn1_w_up), ffn1_w_down, tr(w_in), tr(w_q_b), w_kv_b, w_out,
             tr(ffn2_w_gate), tr(ffn2_w_up), ffn2_w_down]
    ffn1_pos, rest_pos = (0, 1, 2), tuple(range(3, len(local)))
    placed = [cast_place(w, chip1) for w in local]
    lands = [[placed[t][l] for t in range(len(local))] for l in range(n_layers)]
    g_sems, lands_fly, g_token = gather_start(lands, (ffn1_pos, rest_pos), mod)
    gathered = []

    row = lambda a, l: a[l].reshape(1, -1)
    saved = []
    for l in range(n_layers):
        g1, u1, d1 = gather_forward(gather_wait(
            f"{l}a", g_sems[l][0], [lands_fly[l][t] for t in ffn1_pos], xt if l else g_token))
        sv = dict(x0=xt)
        xt, sv["h1"], sv["gate1"], sv["up1"], sv["y1"] = ffn_fwd(
            xt, row(ffn1_norm, l), mod[l, 0], mod[l, 1], mod[l, 2], g1, u1, d1)
        sv["x1"] = xt
        win, wq, wkv, wout, g2, u2, d2 = gather_forward(gather_wait(
            f"{l}b", g_sems[l][1], [lands_fly[l][t] for t in rest_pos], xt))
        gathered.append([g1, u1, d1, win, wq, wkv, wout, g2, u2, d2])
        win = win.reshape(-1, d)
        sv["h2"], u, cq, ckv, kr = mix_in_fwd(xt, row(mix_norm, l), mod[l, 3], mod[l, 4], win)
        sv["cq"], sv["ckv"] = cq, ckv
        yp, sv["diff"] = pool_fwd(u, pool_w[l], row(pool_scale, l))
        qh, kh, vh, sv["ql"], sv["kvl"] = mla_qkv_fwd(
            cq, ckv, kr, row(q_a_norm, l), row(kv_a_norm, l), wq, wkv, cos, sin, rot)
        sv["qkv"] = (qh, kh, vh)
        om = attn_fwd(qh, kh, vh)
        xt, sv["ycat"], sv["y2"] = out_proj_fwd(yp, om, wout, xt, mod[l, 5])
        sv["x2"] = xt
        xt, sv["h3"], sv["gate3"], sv["up3"], sv["y3"] = ffn_fwd(
            xt, row(ffn2_norm, l), mod[l, 6], mod[l, 7], mod[l, 8], g2, u2, d2)
        saved.append(sv)

    loss_vec, dx, d_final_norm = final_loss(xt, final_norm.reshape(1, d), tgt)
    loss = lax.psum(loss_vec[0, 0], ("x", "y", "c"))

    none = [None] * n_layers
    dmods, dnorm1, dnorm2, dnorm3 = list(none), list(none), list(none), list(none)
    dpw, dps, dqan_l, dkvan_l = list(none), list(none), list(none), list(none)
    reduced = [None] * len(local)
    pending = []
    sel_of = lambda l: jnp.stack([mc, chip, jnp.asarray(l, mc.dtype)]).astype(jnp.int32)

    def start(tag, l, positions, grads_, after):
        got = pair_send_halves(grads_)
        sums = [pair_add(g, ra, half) for g, ra in zip(grads_, got)]
        return tag, l, positions, [g.shape[1:] for g in grads_], sums, scatter_start([pb for _, pb in sums], tag, after)

    def finish(job, after):
        tag, l_j, positions, shapes, sums, (s_send, s_recv, pbs_fly, lands_j, _) = job
        parts = scatter_wait(tag, s_send, s_recv, pbs_fly, lands_j, after)
        for t, (p32, _), rb, shp in zip(positions, sums, parts, shapes):
            reduced[t] = chip_sum(p32, rb, sel_of(l_j), (n_layers,) + shp, reduced[t])

    for l in reversed(range(n_layers)):
        sv = saved[l]
        g1, u1, d1, win, wq, wkv, wout, g2, u2, d2 = gathered[l]
        win = win.reshape(-1, d)
        gt3 = mod[l, 8] + pending[-1][5][4][0, 0] if pending else mod[l, 8]
        dy, a, dgt, dup = ffn_bwd_act(dx, sv["gate3"], sv["up3"], gt3, d2)
        dx, dvec3 = ffn_bwd_in(dx, sv["x2"], sv["y3"], dgt, dup, row(ffn2_norm, l), mod[l, 7], g2, u2)
        g_g2, g_u2, g_d2 = tn_mm(dgt, sv["h3"][None]), tn_mm(dup, sv["h3"][None]), tn_mm(a, dy[None])
        dy2, dyp, dom, dg2 = out_proj_bwd(dx, sv["y2"], mod[l, 5], wout)
        g_wout = tn_mm(sv["ycat"], dy2[None])
        qh, kh, vh = sv["qkv"]
        dqh, dkh, dvh = attn_bwd(qh, kh, vh, dom)
        dcq, dckv, dkr_in, gq, gkv, dqan_l[l], dkvan_l[l] = mla_qkv_bwd(
            dqh, dkh, dvh, sv["cq"], sv["ckv"], row(q_a_norm, l), row(kv_a_norm, l), wq, wkv, cos, sin, rot_t)
        g_wq, g_wkv = tn_mm(gq, sv["ql"][None]), tn_mm(sv["kvl"][None], gkv)
        du, dpw[l], dps[l] = pool_bwd(dyp, sv["diff"], pool_w[l], row(pool_scale, l))
        dx, dz, dvec2 = mix_in_bwd(dx, du, dcq, dckv, dkr_in, sv["x1"], row(mix_norm, l), mod[l, 4], win)
        g_win = tn_mm(dz[None], sv["h2"][None]).reshape(N_CHIPS, -1, d)
        stage_a = start(f"{l}a", l, rest_pos, [g_win, g_wq, g_wkv, g_wout, g_g2, g_u2, g_d2], dx)
        dy, a, dgt, dup = ffn_bwd_act(dx, sv["gate1"], sv["up1"], mod[l, 2] + stage_a[5][4][0, 0], d1)
        dx, dvec1 = ffn_bwd_in(dx, sv["x0"], sv["y1"], dgt, dup, row(ffn1_norm, l), mod[l, 1], g1, u1)
        g_g1, g_u1, g_d1 = tn_mm(dgt, sv["h1"][None]), tn_mm(dup, sv["h1"][None]), tn_mm(a, dy[None])
        dmods[l] = jnp.concatenate([dvec1[0:3], dvec2[0:2], dg2, dvec3[0:3]], axis=0)
        dnorm1[l], dnorm2[l], dnorm3[l] = dvec1[3], dvec2[3], dvec3[3]
        if l == 0:
            small_parts = [jnp.stack(dmods), jnp.stack(dnorm1), jnp.stack(dnorm2), jnp.stack(dnorm3), d_final_norm,
                           jnp.stack(dps), jnp.stack(dqan_l), jnp.stack(dkvan_l), jnp.stack(dpw)]
            packed, spans = _pack_rows(small_parts, d)
            gathered_small = exchange8(packed, True)

        stage_b = start(f"{l}b", l, ffn1_pos, [g_g1, g_u1, g_d1], gathered_small if l == 0 else dx)
        for job in pending:
            finish(job, dx)
        pending = [stage_a, stage_b]

    total = sum_devices(gathered_small)
    (g_ada_b, g_n1, g_n2, g_n3, g_fn, g_ps, g_qan, g_kvan, g_pw) = _unpack_rows(total, spans)
    dmod_all = gathered_small[:, :9 * n_layers].reshape(N_DEV, n_layers, 9 * d)
    dmod_loc = lax.dynamic_slice_in_dim(dmod_all, chip * ada_cols, ada_cols, axis=2)
    dmod16 = jnp.pad(jnp.transpose(dmod_loc, (1, 0, 2)), ((0, 0), (0, 8), (0, 0)))
    g_ada_w = ada_bwd(c16, dmod16)

    grads = [g_ada_w, g_ada_b, g_n1, None, None, None, g_n2, None, g_pw, g_ps, g_qan, None, g_kvan, None, None, g_n3,
             None, None, None, g_fn]
    weights = [ada_w, ada_b, ffn1_norm, ffn1_w_gate, ffn1_w_up, ffn1_w_down, mix_norm, w_in, pool_w, pool_scale,
               q_a_norm, w_q_b, kv_a_norm, w_kv_b, w_out, ffn2_norm, ffn2_w_gate, ffn2_w_up, ffn2_w_down, final_norm]
    ms = [m_ada_w, m_ada_b, m_ffn1_norm, m_ffn1_w_gate, m_ffn1_w_up, m_ffn1_w_down, m_mix_norm, m_w_in, m_pool_w,
          m_pool_scale, m_q_a_norm, m_w_q_b, m_kv_a_norm, m_w_kv_b, m_w_out, m_ffn2_norm, m_ffn2_w_gate, m_ffn2_w_up,
          m_ffn2_w_down, m_final_norm]
    vs = [v_ada_w, v_ada_b, v_ffn1_norm, v_ffn1_w_gate, v_ffn1_w_up, v_ffn1_w_down, v_mix_norm, v_w_in, v_pool_w,
          v_pool_scale, v_q_a_norm, v_w_q_b, v_kv_a_norm, v_w_kv_b, v_w_out, v_ffn2_norm, v_ffn2_w_gate, v_ffn2_w_up,
          v_ffn2_w_down, v_final_norm]
    transposed = (3, 4, 7, 11, 16, 17)
    outs = [None] * len(weights)
    for i, (w, g, m, v) in enumerate(zip(weights, grads, ms, vs)):
        if g is not None:
            g = g.reshape(w.shape)
            outs[i] = (g,) + adamw(w, g, m, v)
    for job in pending:
        finish(job, outs[0][1])
    g_local = iter(pair_fill_halves(reduced))
    for i, (w, g, m, v) in enumerate(zip(weights, grads, ms, vs)):
        if g is None:
            g = next(g_local)
            if i in transposed:
                outs[i] = tuple(tr(t) for t in (g,) + adamw(tr(w), g, tr(m), tr(v)))
            else:
                outs[i] = (g,) + adamw(w, g, m, v)
    return (loss, dx.reshape(x.shape), *[t[0] for t in outs], *[t[1] for t in outs], *[t[2] for t in outs],
            *[t[3] for t in outs])
```

```python
import math

import jax
import jax.numpy as jnp
from jax import lax
from jax.experimental import pallas as pl
from jax.experimental.pallas import tpu as pltpu

F32 = jnp.float32
BF16 = jnp.bfloat16
MESH = pl.DeviceIdType.MESH

EPS = 1e-6
ROPE_THETA = 10000.0
N_HEADS = 4
QK_NOPE = 128
QK_ROPE = 64
V_HEAD = 128
POOL_WINDOWS = (2, 4, 8, 16)
POOL_GC = 128
POOL_WIDTH = POOL_GC * len(POOL_WINDOWS)
Q_LORA = 384
KV_LORA = 256
SOFTMAX_SCALE = 1.0 / math.sqrt(QK_NOPE + QK_ROPE)
N_CHIPS = 4
N_DEV = 8

ADAM_LR = 0.001
ADAM_B1 = 0.9
ADAM_B2 = 0.999
ADAM_EPS = 1e-08
ADAM_WD = 0.01
ADAM_STEP = 10

ROW_TILE = 512
ATT_TILE = 256
VMEM_LIMIT = 56 * 1024 * 1024
BF16_ROWS = 16
LANES = 128


def _params(sem=None, vmem=VMEM_LIMIT):
    return pltpu.CompilerParams(dimension_semantics=sem, vmem_limit_bytes=vmem)


def _dot(a, b):
    return jnp.dot(a, b, preferred_element_type=F32)


def _dot_nt(a, b):
    return lax.dot_general(a, b, (((1,), (1,)), ((), ())), preferred_element_type=F32)


def _dot_tn(a, b):
    return lax.dot_general(a, b, (((0,), (0,)), ((), ())), preferred_element_type=F32)


def _dot_exact(t, perm):
    t1 = t.astype(BF16)
    r1 = t - t1.astype(F32)
    t2 = r1.astype(BF16)
    t3 = (r1 - t2.astype(F32)).astype(BF16)
    return _dot(t1, perm) + _dot(t2, perm) + _dot(t3, perm)


def _sum0(a):
    return jnp.sum(a, axis=0, keepdims=True)


def _rms(xt):
    r = lax.rsqrt(jnp.mean(xt * xt, axis=-1, keepdims=True) + EPS)
    return xt * r, r


def _rms_bwd(dy, xt, g):
    xhat, r = _rms(xt)
    dxhat = dy * g
    dx = r * (dxhat - xhat * jnp.mean(dxhat * xhat, axis=-1, keepdims=True))
    return dx, _sum0(dy * xhat)


def _normmod_bwd(dh, xt, gn, sc):
    xhat, _ = _rms(xt)
    dn = dh * (1.0 + sc)
    dx, dgn = _rms_bwd(dn, xt, gn)
    return dx, _sum0(dh), _sum0(dh * (xhat * gn)), dgn


def _row_tile(s):
    return min(s, ROW_TILE)


def _full(shape):
    n = len(shape)
    return pl.BlockSpec(shape, lambda *_: (0,) * n)


def _resident(shape):
    n = len(shape)
    return pl.BlockSpec(shape, lambda *_: (0,) * n, pipeline_mode=pl.Buffered(1))


def ffn_fwd(x, gn, sh, sc, gt, wg, wu, wd):
    s, d = x.shape
    k_chunks, fs, _ = wg.shape
    tm = _row_tile(s)

    def body(x_ref, gn_ref, sh_ref, sc_ref, gt_ref, wg_ref, wu_ref, wd_ref,
             xo_ref, h_ref, gate_ref, up_ref, y_ref):
        xt = x_ref[...]
        xhat, _ = _rms(xt)
        h = (xhat * gn_ref[...] * (1.0 + sc_ref[...]) + sh_ref[...]).astype(BF16)
        h_ref[...] = h
        y = jnp.zeros((tm, d), F32)
        for k in range(k_chunks):
            gate = _dot_nt(h, wg_ref[k])
            up = _dot_nt(h, wu_ref[k])
            gate_ref[k] = gate.astype(BF16)
            up_ref[k] = up.astype(BF16)
            y += _dot((gate * jax.nn.sigmoid(gate) * up).astype(BF16), wd_ref[k])
        y_ref[...] = y.astype(BF16)
        xo_ref[...] = xt + 0.5 * gt_ref[...] * y

    row = pl.BlockSpec((tm, d), lambda i: (i, 0))
    vec = pl.BlockSpec((1, d), lambda i: (0, 0))
    act = pl.BlockSpec((k_chunks, tm, fs), lambda i: (0, i, 0))
    return pl.pallas_call(
        body, name="ffn_fwd",
        grid=(s // tm,),
        in_specs=[row, vec, vec, vec, vec, _resident(wg.shape), _resident(wu.shape), _resident(wd.shape)],
        out_specs=[row, row, act, act, row],
        out_shape=[jax.ShapeDtypeStruct((s, d), F32), jax.ShapeDtypeStruct((s, d), BF16),
                   jax.ShapeDtypeStruct((k_chunks, s, fs), BF16), jax.ShapeDtypeStruct((k_chunks, s, fs), BF16),
                   jax.ShapeDtypeStruct((s, d), BF16)],
        compiler_params=_params(("arbitrary",)),
    )(x, gn, sh, sc, gt, wg, wu, wd)


def ffn_bwd_act(dxn, gate, up, gt, wd):
    s, d = dxn.shape
    k_chunks, fs, _ = wd.shape
    tm = _row_tile(s)

    def body(dxn_ref, gate_ref, up_ref, gt_ref, wd_ref, dy_ref, a_ref, dgate_ref, dup_ref):
        dy = (0.5 * gt_ref[...] * dxn_ref[...]).astype(BF16)
        dy_ref[...] = dy
        for k in range(k_chunks):
            da = _dot_nt(dy, wd_ref[k])
            g = gate_ref[k].astype(F32)
            u = up_ref[k].astype(F32)
            sg = jax.nn.sigmoid(g)
            sl = g * sg
            a_ref[k] = (sl * u).astype(BF16)
            dgate_ref[k] = (da * u * (sg * (1.0 + g * (1.0 - sg)))).astype(BF16)
            dup_ref[k] = (da * sl).astype(BF16)

    row = pl.BlockSpec((tm, d), lambda i: (i, 0))
    act = pl.BlockSpec((k_chunks, tm, fs), lambda i: (0, i, 0))
    act_shape = jax.ShapeDtypeStruct((k_chunks, s, fs), BF16)
    return pl.pallas_call(
        body, name="ffn_bwd_act",
        grid=(s // tm,),
        in_specs=[row, act, act, pl.BlockSpec((1, d), lambda i: (0, 0)), _resident(wd.shape)],
        out_specs=[row, act, act, act],
        out_shape=[jax.ShapeDtypeStruct((s, d), BF16), act_shape, act_shape, act_shape],
        compiler_params=_params(("arbitrary",)),
    )(dxn, gate, up, gt, wd)


def ffn_bwd_in(dxn, x, y, dgate, dup, gn, sc, wg, wu):
    s, d = x.shape
    k_chunks, fs, _ = wg.shape
    tm = _row_tile(s)

    def body(dxn_ref, x_ref, y_ref, dgate_ref, dup_ref, gn_ref, sc_ref, wg_ref, wu_ref, dx_ref, dvec_ref):
        i = pl.program_id(0)

        @pl.when(i == 0)
        def _():
            dvec_ref[...] = jnp.zeros_like(dvec_ref)

        dh = jnp.zeros((tm, d), F32)
        for k in range(k_chunks):
            dh += _dot(dgate_ref[k], wg_ref[k]) + _dot(dup_ref[k], wu_ref[k])
        dxn_t = dxn_ref[...]
        dx, dsh, dsc, dgn = _normmod_bwd(dh, x_ref[...], gn_ref[...], sc_ref[...])
        dx_ref[...] = dx + dxn_t
        dvec_ref[0:1, :] += dsh
        dvec_ref[1:2, :] += dsc
        dvec_ref[2:3, :] += _sum0(0.5 * dxn_t * y_ref[...].astype(F32))
        dvec_ref[3:4, :] += dgn

    row = pl.BlockSpec((tm, d), lambda i: (i, 0))
    vec = pl.BlockSpec((1, d), lambda i: (0, 0))
    act = pl.BlockSpec((k_chunks, tm, fs), lambda i: (0, i, 0))
    return pl.pallas_call(
        body, name="ffn_bwd_in",
        grid=(s // tm,),
        in_specs=[row, row, row, act, act, vec, vec, _resident(wg.shape), _resident(wu.shape)],
        out_specs=[row, pl.BlockSpec((8, d), lambda i: (0, 0))],
        out_shape=[jax.ShapeDtypeStruct((s, d), F32), jax.ShapeDtypeStruct((8, d), F32)],
        compiler_params=_params(("arbitrary",)),
    )(dxn, x, y, dgate, dup, gn, sc, wg, wu)


def tn_mm(a, b):
    ga, s, m = a.shape
    gb, _, n = b.shape
    g = max(ga, gb)

    def body(a_ref, b_ref, o_ref):
        o_ref[...] = _dot_tn(a_ref[...], b_ref[...])

    a_spec = pl.BlockSpec((None, s, m), (lambda gi: (gi, 0, 0)) if ga > 1 else (lambda gi: (0, 0, 0)))
    b_spec = pl.BlockSpec((None, s, n), (lambda gi: (gi, 0, 0)) if gb > 1 else (lambda gi: (0, 0, 0)))
    return pl.pallas_call(
        body, name="tn_mm",
        grid=(g,), in_specs=[a_spec, b_spec], out_specs=pl.BlockSpec((None, m, n), lambda gi: (gi, 0, 0)),
        out_shape=jax.ShapeDtypeStruct((g, m, n), F32),
        compiler_params=_params(("arbitrary",)),
    )(a, b)


def mix_in_fwd(x, gn, sh, sc, w_in_t):
    s, d = x.shape
    tm = _row_tile(s)
    o1, o2, o3 = POOL_WIDTH, POOL_WIDTH + Q_LORA, POOL_WIDTH + Q_LORA + KV_LORA

    def body(x_ref, gn_ref, sh_ref, sc_ref, w_ref, h_ref, u_ref, cq_ref, ckv_ref, kr_ref):
        xhat, _ = _rms(x_ref[...])
        h = (xhat * gn_ref[...] * (1.0 + sc_ref[...]) + sh_ref[...]).astype(BF16)
        h_ref[...] = h
        z = _dot_nt(h, w_ref[0:o3, :])
        u_ref[...] = z[:, 0:o1]
        cq_ref[...] = z[:, o1:o2]
        ckv_ref[...] = z[:, o2:o3]
        kr_ref[...] = _dot_nt(h, w_ref[o3:, :])

    row = lambda w: pl.BlockSpec((tm, w), lambda i: (i, 0))
    vec = pl.BlockSpec((1, d), lambda i: (0, 0))
    return pl.pallas_call(
        body, name="mix_in_fwd",
        grid=(s // tm,),
        in_specs=[row(d), vec, vec, vec, _full(w_in_t.shape)],
        out_specs=[row(d), row(POOL_WIDTH), row(Q_LORA), row(KV_LORA), row(QK_ROPE)],
        out_shape=[jax.ShapeDtypeStruct((s, d), BF16), jax.ShapeDtypeStruct((s, POOL_WIDTH), F32),
                   jax.ShapeDtypeStruct((s, Q_LORA), F32), jax.ShapeDtypeStruct((s, KV_LORA), F32),
                   jax.ShapeDtypeStruct((s, QK_ROPE), F32)],
        compiler_params=_params(("arbitrary",)),
    )(x, gn, sh, sc, w_in_t)


def mix_in_bwd(dxn, du, dcq, dckv, dkr, x, gn, sc, w_in_t):
    s, d = x.shape
    tm = _row_tile(s)
    o1, o2, o3 = POOL_WIDTH, POOL_WIDTH + Q_LORA, POOL_WIDTH + Q_LORA + KV_LORA
    n_z = w_in_t.shape[0]

    def body(dxn_ref, du_ref, dcq_ref, dckv_ref, dkr_ref, x_ref, gn_ref, sc_ref, w_ref, dx_ref, dz_ref, dvec_ref):
        i = pl.program_id(0)

        @pl.when(i == 0)
        def _():
            dvec_ref[...] = jnp.zeros_like(dvec_ref)

        dub = du_ref[...].astype(BF16)
        dqb = dcq_ref[...].astype(BF16)
        dkb = dckv_ref[...].astype(BF16)
        drb = dkr_ref[...].astype(BF16)
        dz_ref[:, 0:o1] = dub
        dz_ref[:, o1:o2] = dqb
        dz_ref[:, o2:o3] = dkb
        dz_ref[:, o3:] = drb
        dh = (_dot(dub, w_ref[0:o1, :]) + _dot(dqb, w_ref[o1:o2, :]) + _dot(dkb, w_ref[o2:o3, :])
              + _dot(drb, w_ref[o3:, :]))
        dx, dsh, dsc, dgn = _normmod_bwd(dh, x_ref[...], gn_ref[...], sc_ref[...])
        dx_ref[...] = dx + dxn_ref[...]
        dvec_ref[0:1, :] += dsh
        dvec_ref[1:2, :] += dsc
        dvec_ref[3:4, :] += dgn

    row = lambda w: pl.BlockSpec((tm, w), lambda i: (i, 0))
    vec = pl.BlockSpec((1, d), lambda i: (0, 0))
    return pl.pallas_call(
        body, name="mix_in_bwd",
        grid=(s // tm,),
        in_specs=[row(d), row(POOL_WIDTH), row(Q_LORA), row(KV_LORA), row(QK_ROPE), row(d), vec, vec,
                  _full(w_in_t.shape)],
        out_specs=[row(d), row(n_z), pl.BlockSpec((8, d), lambda i: (0, 0))],
        out_shape=[jax.ShapeDtypeStruct((s, d), F32), jax.ShapeDtypeStruct((s, n_z), BF16),
                   jax.ShapeDtypeStruct((8, d), F32)],
        compiler_params=_params(("arbitrary",)),
    )(dxn, du, dcq, dckv, dkr, x, gn, sc, w_in_t)


def _window_sum(a, w, rows, forward):
    s = a.shape[0]
    step = 1
    while step < w:
        if forward:
            shifted = jnp.where(rows < s - step, pltpu.roll(a, s - step, 0), 0.0)
        else:
            shifted = jnp.where(rows >= step, pltpu.roll(a, step, 0), 0.0)
        a = a + shifted
        step *= 2
    return a


def pool_fwd(u, pool_w, pool_scale):
    s = u.shape[0]

    def body(u_ref, w_ref, sc_ref, y_ref, diff_ref):
        rows = lax.broadcasted_iota(jnp.int32, (s, POOL_GC), 0)
        for g, w in enumerate(POOL_WINDOWS):
            cols = slice(g * POOL_GC, (g + 1) * POOL_GC)
            ug = u_ref[:, cols]
            cnt = jnp.minimum(rows + 1, w).astype(F32)
            diff = (_window_sum(ug, w, rows, False) / cnt - ug).astype(BF16)
            diff_ref[:, cols] = diff
            y_ref[:, cols] = _dot(diff, w_ref[g].astype(BF16)) * sc_ref[:, cols]

    return pl.pallas_call(
        body, name="pool_fwd",
        out_shape=[jax.ShapeDtypeStruct(u.shape, F32), jax.ShapeDtypeStruct(u.shape, BF16)],
        compiler_params=_params(),
    )(u, pool_w, pool_scale)


def pool_bwd(dy, diff, pool_w, pool_scale):
    s = dy.shape[0]

    def body(dy_ref, diff_ref, w_ref, sc_ref, du_ref, dw_ref, dsc_ref):
        rows = lax.broadcasted_iota(jnp.int32, (s, POOL_GC), 0)
        for g, w in enumerate(POOL_WINDOWS):
            cols = slice(g * POOL_GC, (g + 1) * POOL_GC)
            dyg = dy_ref[:, cols]
            diff = diff_ref[:, cols]
            wb = w_ref[g].astype(BF16)
            dsc_ref[:, cols] = _sum0(dyg * _dot(diff, wb))
            dys = (dyg * sc_ref[:, cols]).astype(BF16)
            dw_ref[g] = _dot_tn(diff, dys)
            ddiff = _dot_nt(dys, wb)
            cnt = jnp.minimum(rows + 1, w).astype(F32)
            du_ref[:, cols] = _window_sum(ddiff / cnt, w, rows, True) - ddiff

    return pl.pallas_call(
        body, name="pool_bwd",
        out_shape=[jax.ShapeDtypeStruct(dy.shape, F32), jax.ShapeDtypeStruct(pool_w.shape, F32),
                   jax.ShapeDtypeStruct(pool_scale.shape, F32)],
        compiler_params=_params(),
    )(dy, diff, pool_w, pool_scale)


def mla_qkv_fwd(cq, ckv, kr, qan, kvan, wq, wkv, cos, sin, rot):
    s = cq.shape[0]
    tm = _row_tile(s)

    def body(cq_ref, ckv_ref, kr_ref, qan_ref, kvan_ref, wq_ref, wkv_ref, cos_ref, sin_ref, rot_ref,
             q_ref, k_ref, v_ref, ql_ref, kvl_ref):
        cos_t = cos_ref[...]
        sin_t = sin_ref[...]
        perm = rot_ref[...]

        def rope(t):
            return t * cos_t + _dot_exact(t, perm) * sin_t

        qhat, _ = _rms(cq_ref[...])
        ql = (qhat * qan_ref[...]).astype(BF16)
        ql_ref[...] = ql
        khat, _ = _rms(ckv_ref[...])
        kvl = (khat * kvan_ref[...]).astype(BF16)
        kvl_ref[...] = kvl
        krr = rope(kr_ref[...]).astype(BF16)
        for h in range(N_HEADS):
            q = _dot_nt(ql, wq_ref[h])
            q_ref[h, :, 0:QK_NOPE] = q[:, 0:QK_NOPE].astype(BF16)
            q_ref[h, :, QK_NOPE:] = rope(q[:, QK_NOPE:]).astype(BF16)
            kv = _dot(kvl, wkv_ref[h])
            k_ref[h, :, 0:QK_NOPE] = kv[:, 0:QK_NOPE].astype(BF16)
            k_ref[h, :, QK_NOPE:] = krr
            v_ref[h] = kv[:, QK_NOPE:].astype(BF16)

    row = lambda w: pl.BlockSpec((tm, w), lambda i: (i, 0))
    hrow = lambda w: pl.BlockSpec((N_HEADS, tm, w), lambda i: (0, i, 0))
    qk = QK_NOPE + QK_ROPE
    return pl.pallas_call(
        body, name="mla_qkv_fwd",
        grid=(s // tm,),
        in_specs=[row(Q_LORA), row(KV_LORA), row(QK_ROPE), _full(qan.shape), _full(kvan.shape),
                  _full(wq.shape), _full(wkv.shape), row(QK_ROPE), row(QK_ROPE), _full(rot.shape)],
        out_specs=[hrow(qk), hrow(qk), hrow(V_HEAD), row(Q_LORA), row(KV_LORA)],
        out_shape=[jax.ShapeDtypeStruct((N_HEADS, s, qk), BF16), jax.ShapeDtypeStruct((N_HEADS, s, qk), BF16),
                   jax.ShapeDtypeStruct((N_HEADS, s, V_HEAD), BF16), jax.ShapeDtypeStruct((s, Q_LORA), BF16),
                   jax.ShapeDtypeStruct((s, KV_LORA), BF16)],
        compiler_params=_params(("arbitrary",)),
    )(cq, ckv, kr, qan, kvan, wq, wkv, cos, sin, rot)


def _attn_probs(q_ref, k_ref, qi, tq):
    n = (qi + 1) * tq
    rows = slice(qi * tq, n)
    sc = _dot_nt(q_ref[rows, :], k_ref[0:n, :]) * SOFTMAX_SCALE
    qpos = qi * tq + lax.broadcasted_iota(jnp.int32, (tq, n), 0)
    kpos = lax.broadcasted_iota(jnp.int32, (tq, n), 1)
    sc = jnp.where(qpos >= kpos, sc, -1e30)
    e = jnp.exp(sc - jnp.max(sc, axis=-1, keepdims=True))
    return e / jnp.sum(e, axis=-1, keepdims=True)


def attn_fwd(q, k, v):
    nh, s, qk = q.shape
    tq = min(s, ATT_TILE)

    def body(q_ref, k_ref, v_ref, o_ref):
        for qi in range(s // tq):
            n = (qi + 1) * tq
            p = _attn_probs(q_ref, k_ref, qi, tq).astype(BF16)
            o_ref[qi * tq:n, :] = _dot(p, v_ref[0:n, :])

    head = lambda w: pl.BlockSpec((None, s, w), lambda h: (h, 0, 0))
    return pl.pallas_call(
        body, name="attn_fwd",
        grid=(nh,),
        in_specs=[head(qk), head(qk), head(V_HEAD)],
        out_specs=pl.BlockSpec((s, V_HEAD), lambda h: (0, h)),
        out_shape=jax.ShapeDtypeStruct((s, nh * V_HEAD), F32),
        compiler_params=_params(("arbitrary",)),
    )(q, k, v)


def attn_bwd(q, k, v, do):
    nh, s, qk = q.shape
    tq = min(s, ATT_TILE)

    def body(q_ref, k_ref, v_ref, do_ref, dq_ref, dk_ref, dv_ref):
        dk_ref[...] = jnp.zeros_like(dk_ref)
        dv_ref[...] = jnp.zeros_like(dv_ref)
        for qi in range(s // tq):
            n = (qi + 1) * tq
            rows = slice(qi * tq, n)
            p = _attn_probs(q_ref, k_ref, qi, tq)
            dob = do_ref[rows, :].astype(BF16)
            dp = _dot_nt(dob, v_ref[0:n, :])
            ds = (p * (dp - jnp.sum(p * dp, axis=-1, keepdims=True)) * SOFTMAX_SCALE).astype(BF16)
            dq_ref[rows, :] = _dot(ds, k_ref[0:n, :])
            dk_ref[0:n, :] += _dot_tn(ds, q_ref[rows, :])
            dv_ref[0:n, :] += _dot_tn(p.astype(BF16), dob)

    head = lambda w: pl.BlockSpec((None, s, w), lambda h: (h, 0, 0))
    return pl.pallas_call(
        body, name="attn_bwd",
        grid=(nh,),
        in_specs=[head(qk), head(qk), head(V_HEAD), pl.BlockSpec((s, V_HEAD), lambda h: (0, h))],
        out_specs=[head(qk), head(qk), head(V_HEAD)],
        out_shape=[jax.ShapeDtypeStruct((nh, s, qk), F32), jax.ShapeDtypeStruct((nh, s, qk), F32),
                   jax.ShapeDtypeStruct((nh, s, V_HEAD), F32)],
        compiler_params=_params(("arbitrary",)),
    )(q, k, v, do)


def mla_qkv_bwd(dq, dk, dv, cq, ckv, qan, kvan, wq, wkv, cos, sin, rot_t):
    s = cq.shape[0]
    tm = _row_tile(s)

    def body(dq_ref, dk_ref, dv_ref, cq_ref, ckv_ref, qan_ref, kvan_ref,
             wq_ref, wkv_ref, cos_ref, sin_ref, rot_ref,
             dcq_ref, dckv_ref, dkro_ref, gq_ref, gkv_ref, dqan_ref, dkvan_ref):
        i = pl.program_id(0)

        @pl.when(i == 0)
        def _():
            dqan_ref[...] = jnp.zeros_like(dqan_ref)
            dkvan_ref[...] = jnp.zeros_like(dkvan_ref)

        cos_t = cos_ref[...]
        sin_t = sin_ref[...]
        perm_t = rot_ref[...]

        def unrope(t):
            return t * cos_t + _dot_exact(t * sin_t, perm_t)

        acc_q = jnp.zeros((tm, Q_LORA), F32)
        acc_kv = jnp.zeros((tm, KV_LORA), F32)
        dkr_sum = jnp.zeros((tm, QK_ROPE), F32)
        for h in range(N_HEADS):
            dq_h = dq_ref[h]
            a = dq_h[:, 0:QK_NOPE].astype(BF16)
            b = unrope(dq_h[:, QK_NOPE:]).astype(BF16)
            gq_ref[h, :, 0:QK_NOPE] = a
            gq_ref[h, :, QK_NOPE:] = b
            wq_h = wq_ref[h]
            acc_q += _dot(a, wq_h[0:QK_NOPE, :]) + _dot(b, wq_h[QK_NOPE:, :])
            dk_h = dk_ref[h]
            dk = dk_h[:, 0:QK_NOPE].astype(BF16)
            dvv = dv_ref[h].astype(BF16)
            gkv_ref[h, :, 0:QK_NOPE] = dk
            gkv_ref[h, :, QK_NOPE:] = dvv
            wkv_h = wkv_ref[h]
            acc_kv += _dot_nt(dk, wkv_h[:, 0:QK_NOPE]) + _dot_nt(dvv, wkv_h[:, QK_NOPE:])
            dkr_sum += dk_h[:, QK_NOPE:]
        dkro_ref[...] = unrope(dkr_sum)
        dcq, dqan = _rms_bwd(acc_q, cq_ref[...], qan_ref[...])
        dcq_ref[...] = dcq
        dqan_ref[...] += dqan
        dckv, dkvan = _rms_bwd(acc_kv, ckv_ref[...], kvan_ref[...])
        dckv_ref[...] = dckv
        dkvan_ref[...] += dkvan

    row = lambda w: pl.BlockSpec((tm, w), lambda i: (i, 0))
    hrow = lambda w: pl.BlockSpec((N_HEADS, tm, w), lambda i: (0, i, 0))
    return pl.pallas_call(
        body, name="mla_qkv_bwd",
        grid=(s // tm,),
        in_specs=[hrow(QK_NOPE + QK_ROPE), hrow(QK_NOPE + QK_ROPE), hrow(V_HEAD),
                  row(Q_LORA), row(KV_LORA), _full(qan.shape), _full(kvan.shape),
                  _full(wq.shape), _full(wkv.shape), row(QK_ROPE), row(QK_ROPE), _full(rot_t.shape)],
        out_specs=[row(Q_LORA), row(KV_LORA), row(QK_ROPE), hrow(QK_NOPE + QK_ROPE), hrow(QK_NOPE + V_HEAD),
                   _full(qan.shape), _full(kvan.shape)],
        out_shape=[jax.ShapeDtypeStruct((s, Q_LORA), F32), jax.ShapeDtypeStruct((s, KV_LORA), F32),
                   jax.ShapeDtypeStruct((s, QK_ROPE), F32),
                   jax.ShapeDtypeStruct((N_HEADS, s, QK_NOPE + QK_ROPE), BF16),
                   jax.ShapeDtypeStruct((N_HEADS, s, QK_NOPE + V_HEAD), BF16),
                   jax.ShapeDtypeStruct(qan.shape, F32), jax.ShapeDtypeStruct(kvan.shape, F32)],
        compiler_params=_params(("arbitrary",)),
    )(dq, dk, dv, cq, ckv, qan, kvan, wq, wkv, cos, sin, rot_t)


def out_proj_fwd(yp, om, w_out, x, gt):
    s, d = x.shape
    n_sh, rs, _ = w_out.shape
    tm = _row_tile(s)
    per = POOL_WIDTH // rs

    def body(yp_ref, om_ref, w_ref, x_ref, gt_ref, xo_ref, ycat_ref, y_ref):
        y = jnp.zeros((tm, d), F32)
        for j in range(n_sh):
            src = yp_ref if j < per else om_ref
            part = src[:, (j % per) * rs:(j % per + 1) * rs].astype(BF16)
            ycat_ref[j] = part
            y += _dot(part, w_ref[j])
        y_ref[...] = y.astype(BF16)
        xo_ref[...] = x_ref[...] + gt_ref[...] * y

    row = lambda w: pl.BlockSpec((tm, w), lambda i: (i, 0))
    return pl.pallas_call(
        body, name="out_proj_fwd",
        grid=(s // tm,),
        in_specs=[row(POOL_WIDTH), row(POOL_WIDTH), _full(w_out.shape), row(d), pl.BlockSpec((1, d), lambda i: (0, 0))],
        out_specs=[row(d), pl.BlockSpec((n_sh, tm, rs), lambda i: (0, i, 0)), row(d)],
        out_shape=[jax.ShapeDtypeStruct((s, d), F32), jax.ShapeDtypeStruct((n_sh, s, rs), BF16),
                   jax.ShapeDtypeStruct((s, d), BF16)],
        compiler_params=_params(("arbitrary",)),
    )(yp, om, w_out, x, gt)


def out_proj_bwd(dxn, y, gt, w_out):
    s, d = dxn.shape
    n_sh, rs, _ = w_out.shape
    tm = _row_tile(s)
    per = POOL_WIDTH // rs

    def body(dxn_ref, y_ref, gt_ref, w_ref, dy_ref, dyp_ref, dom_ref, dgt_ref):
        i = pl.program_id(0)

        @pl.when(i == 0)
        def _():
            dgt_ref[...] = jnp.zeros_like(dgt_ref)

        dxn_t = dxn_ref[...]
        dy = (gt_ref[...] * dxn_t).astype(BF16)
        dy_ref[...] = dy
        dgt_ref[...] += _sum0(dxn_t * y_ref[...].astype(F32))
        for j in range(n_sh):
            dst = dyp_ref if j < per else dom_ref
            dst[:, (j % per) * rs:(j % per + 1) * rs] = _dot_nt(dy, w_ref[j])

    row = lambda w: pl.BlockSpec((tm, w), lambda i: (i, 0))
    vec = pl.BlockSpec((1, d), lambda i: (0, 0))
    return pl.pallas_call(
        body, name="out_proj_bwd",
        grid=(s // tm,),
        in_specs=[row(d), row(d), vec, _full(w_out.shape)],
        out_specs=[row(d), row(POOL_WIDTH), row(POOL_WIDTH), vec],
        out_shape=[jax.ShapeDtypeStruct((s, d), BF16), jax.ShapeDtypeStruct((s, POOL_WIDTH), F32),
                   jax.ShapeDtypeStruct((s, POOL_WIDTH), F32), jax.ShapeDtypeStruct((1, d), F32)],
        compiler_params=_params(("arbitrary",)),
    )(dxn, y, gt, w_out)


def final_loss(x, gn, tgt):
    s, d = x.shape
    tm = _row_tile(s)

    def body(x_ref, gn_ref, t_ref, loss_ref, dx_ref, dgn_ref):
        i = pl.program_id(0)

        @pl.when(i == 0)
        def _():
            loss_ref[...] = jnp.zeros_like(loss_ref)
            dgn_ref[...] = jnp.zeros_like(dgn_ref)

        xt = x_ref[...]
        g = gn_ref[...]
        xhat, _ = _rms(xt)
        err = xhat * g - t_ref[...]
        per_tok = jnp.mean(err * err, axis=-1, keepdims=True)
        loss_ref[...] += jnp.broadcast_to(0.5 * _sum0(per_tok), loss_ref.shape)
        dx, dgn = _rms_bwd(err * (1.0 / d), xt, g)
        dx_ref[...] = dx
        dgn_ref[...] += dgn

    row = pl.BlockSpec((tm, d), lambda i: (i, 0))
    vec = pl.BlockSpec((1, d), lambda i: (0, 0))
    return pl.pallas_call(
        body, name="final_loss",
        grid=(s // tm,),
        in_specs=[row, vec, row],
        out_specs=[pl.BlockSpec((1, LANES), lambda i: (0, 0)), row, vec],
        out_shape=[jax.ShapeDtypeStruct((1, LANES), F32), jax.ShapeDtypeStruct((s, d), F32),
                   jax.ShapeDtypeStruct((1, d), F32)],
        compiler_params=_params(("arbitrary",)),
    )(x, gn, tgt)


def _col_tile(cols):
    return 768 if cols % 768 == 0 else cols


def ada_fwd(c16, ada_w, ada_b_loc):
    n_layers, d, cols = ada_w.shape
    tn = _col_tile(cols)

    def body(c_ref, w_ref, b_ref, o_ref):
        cv = c_ref[...]
        ca = (cv * jax.nn.sigmoid(cv)).astype(BF16)
        o_ref[...] = _dot(ca, w_ref[...].astype(BF16)) + b_ref[...]

    return pl.pallas_call(
        body, name="ada_fwd",
        grid=(n_layers, cols // tn),
        in_specs=[pl.BlockSpec((16, d), lambda l, j: (0, 0)), pl.BlockSpec((None, d, tn), lambda l, j: (l, 0, j)),
                  pl.BlockSpec((None, 1, tn), lambda l, j: (l, 0, j))],
        out_specs=pl.BlockSpec((None, 16, tn), lambda l, j: (l, 0, j)),
        out_shape=jax.ShapeDtypeStruct((n_layers, 16, cols), F32),
        compiler_params=_params(("arbitrary", "arbitrary")),
    )(c16, ada_w, ada_b_loc)


def ada_bwd(c16, dmod16):
    n_layers, _, cols = dmod16.shape
    d = c16.shape[1]
    tn = _col_tile(cols)

    def body(c_ref, g_ref, o_ref):
        cv = c_ref[...]
        ca = (cv * jax.nn.sigmoid(cv)).astype(BF16)
        o_ref[...] = _dot_tn(ca, g_ref[...].astype(BF16))

    return pl.pallas_call(
        body, name="ada_bwd",
        grid=(n_layers, cols // tn),
        in_specs=[pl.BlockSpec((16, d), lambda l, j: (0, 0)), pl.BlockSpec((None, 16, tn), lambda l, j: (l, 0, j))],
        out_specs=pl.BlockSpec((None, d, tn), lambda l, j: (l, 0, j)),
        out_shape=jax.ShapeDtypeStruct((n_layers, d, cols), F32),
        compiler_params=_params(("arbitrary", "arbitrary")),
    )(c16, dmod16)


def _as_rows(a):
    if a.ndim == 1:
        return a.reshape(1, a.shape[0])
    return a.reshape(-1, a.shape[-1])


def _rows_tile(r, c, itemsize=4, budget=2 * 1024 * 1024):
    if r * c * itemsize <= budget:
        return r
    best = None
    t = BF16_ROWS
    while t < r:
        if r % t == 0 and t * c * itemsize <= budget:
            best = t
        t += BF16_ROWS
    return best if best is not None else r


def cast_place(w, chip):
    n_layers, r, c = w.shape
    tr = _rows_tile(r, c, budget=2 * 1024 * 1024 // n_layers)

    def body(chip_ref, w_ref, *o_refs):
        for l in range(n_layers):
            o_refs[l][...] = w_ref[l].astype(BF16)

    return list(pl.pallas_call(
        body, name="cast_place",
        grid_spec=pltpu.PrefetchScalarGridSpec(
            num_scalar_prefetch=1, grid=(r // tr,),
            in_specs=[pl.BlockSpec((n_layers, tr, c), lambda i, ch: (0, i, 0))],
            out_specs=[pl.BlockSpec((None, tr, c), lambda i, ch: (ch[0], i, 0))] * n_layers),
        out_shape=[jax.ShapeDtypeStruct((N_CHIPS, r, c), BF16)] * n_layers,
        compiler_params=_params(("arbitrary",)),
    )(chip, w))


def adamw(w, g, m, v):
    shape = w.shape
    w2, g2, m2, v2 = (_as_rows(t) for t in (w, g, m, v))
    r, c = w2.shape
    tr = _rows_tile(r, c, budget=1024 * 1024)
    c1 = 1.0 - ADAM_B1 ** ADAM_STEP
    c2 = 1.0 - ADAM_B2 ** ADAM_STEP

    def body(w_ref, g_ref, m_ref, v_ref, d_ref, mo_ref, vo_ref):
        gv = g_ref[...]
        mn = ADAM_B1 * m_ref[...] + (1.0 - ADAM_B1) * gv
        vn = ADAM_B2 * v_ref[...] + (1.0 - ADAM_B2) * (gv * gv)
        mo_ref[...] = mn
        vo_ref[...] = vn
        d_ref[...] = -ADAM_LR * ((mn / c1) / (jnp.sqrt(vn / c2) + ADAM_EPS) + ADAM_WD * w_ref[...])

    spec = pl.BlockSpec((tr, c), lambda i: (i, 0))
    outs = pl.pallas_call(
        body, name="adamw", grid=(r // tr,), in_specs=[spec] * 4, out_specs=[spec] * 3,
        out_shape=[jax.ShapeDtypeStruct((r, c), F32)] * 3, compiler_params=_params(("arbitrary",)),
    )(w2, g2, m2, v2)
    return tuple(o.reshape(shape) for o in outs)


def sum_devices(a):
    n, r, c = a.shape
    tr = _rows_tile(r, c, budget=512 * 1024)

    def body(a_ref, o_ref):
        acc = a_ref[0]
        for j in range(1, n):
            acc = acc + a_ref[j]
        o_ref[...] = acc

    return pl.pallas_call(
        body, name="sum_devices", grid=(r // tr,),
        in_specs=[pl.BlockSpec((n, tr, c), lambda i: (0, i, 0))], out_specs=pl.BlockSpec((tr, c), lambda i: (i, 0)),
        out_shape=jax.ShapeDtypeStruct((r, c), F32), compiler_params=_params(("arbitrary",)),
    )(a)


def _split_axis(r, c):
    if (r // 2) % BF16_ROWS == 0 and r % 2 == 0:
        return 0
    assert c % (2 * LANES) == 0, (r, c)
    return 1


def _half_shape(r, c):
    return (r // 2, c) if _split_axis(r, c) == 0 else (r, c // 2)


def _half_at(ref, lead, which):
    r, c = ref.shape[-2:]
    if _split_axis(r, c) == 0:
        return ref.at[(*lead, pl.ds(which * (r // 2), r // 2), slice(None))]
    return ref.at[(*lead, slice(None), pl.ds(which * (c // 2), c // 2))]


def _half_spec(r, c, lead_block, imap):
    hr, hc = _half_shape(r, c)
    if _split_axis(r, c) == 0:
        return pl.BlockSpec((*lead_block, hr, hc), lambda *a: (*imap(*a)[0], imap(*a)[1], 0))
    return pl.BlockSpec((*lead_block, hr, hc), lambda *a: (*imap(*a)[0], 0, imap(*a)[1]))


def pair_add(g, ra, half):
    n_sl, r, c = g.shape
    hr, hc = _half_shape(r, c)

    def body(h_ref, g_ref, ra_ref, p_ref, pb_ref):
        p = g_ref[...] + ra_ref[...]
        p_ref[...] = p
        pb_ref[...] = p.astype(BF16)

    mine = pl.BlockSpec((None, hr, hc), lambda k, h: (k, 0, 0))
    return pl.pallas_call(
        body, name="pair_add",
        grid_spec=pltpu.PrefetchScalarGridSpec(
            num_scalar_prefetch=1, grid=(n_sl,),
            in_specs=[_half_spec(r, c, (None,), lambda k, h: ((k,), h[0])), mine], out_specs=[mine, mine]),
        out_shape=[jax.ShapeDtypeStruct((n_sl, hr, hc), F32), jax.ShapeDtypeStruct((n_sl, hr, hc), BF16)],
        compiler_params=_params(("arbitrary",)),
    )(half, g, ra)


def chip_sum(p32, rb, sel, shape, acc):
    n_layers, r, c = shape
    hr, hc = _half_shape(r, c)

    def body(s_ref, p_ref, rb_ref, *rest):
        o_ref = rest[-1]
        acc_v = p_ref[...]
        for j in range(N_CHIPS - 1):
            acc_v = acc_v + rb_ref[j].astype(F32)
        o_ref[...] = acc_v

    in_specs = [pl.BlockSpec((None, hr, hc), lambda i, sr: (sr[1], 0, 0)),
                pl.BlockSpec((N_CHIPS - 1, hr, hc), lambda i, sr: (0, 0, 0))]
    args = [sel, p32, rb]
    aliases = {}
    if acc is not None:
        in_specs.append(pl.BlockSpec(memory_space=pl.ANY))
        args.append(acc)
        aliases = {3: 0}
    return pl.pallas_call(
        body, name="chip_sum",
        grid_spec=pltpu.PrefetchScalarGridSpec(
            num_scalar_prefetch=1, grid=(1,), in_specs=in_specs,
            out_specs=_half_spec(r, c, (None,), lambda i, sr: ((sr[2],), sr[0]))),
        out_shape=jax.ShapeDtypeStruct((n_layers, r, c), F32),
        input_output_aliases=aliases,
        compiler_params=_params(("arbitrary",)),
    )(*args)


def _me():
    return lax.axis_index("x"), lax.axis_index("y"), lax.axis_index("c")


def _flip(v, bit):
    return 1 - v if bit else v


def exchange8(xs, bcast):
    blk = xs.shape if bcast else xs.shape[1:]

    def body(x_ref, o_ref, send_sems, recv_sems, loc_sem):
        mx, my, mc = _me()
        me = 4 * mx + 2 * my + mc
        src = (lambda j: x_ref) if bcast else (lambda j: x_ref.at[j])
        loc = pltpu.make_async_copy(src(me), o_ref.at[me], loc_sem)
        loc.start()
        copies = []
        for o in range(1, N_DEV):
            px, py, pc = _flip(mx, o & 4), _flip(my, o & 2), _flip(mc, o & 1)
            cp = pltpu.make_async_remote_copy(
                src_ref=src(4 * px + 2 * py + pc), dst_ref=o_ref.at[me],
                send_sem=send_sems.at[o - 1], recv_sem=recv_sems.at[o - 1],
                device_id=(px, py, pc), device_id_type=MESH)
            cp.start()
            copies.append(cp)
        for cp in copies:
            cp.wait()
        loc.wait()

    return pl.pallas_call(
        body, name="exchange8_gather" if bcast else "exchange8_a2a",
        in_specs=[pl.BlockSpec(memory_space=pltpu.VMEM)], out_specs=pl.BlockSpec(memory_space=pltpu.VMEM),
        out_shape=jax.ShapeDtypeStruct((N_DEV,) + tuple(blk), xs.dtype),
        scratch_shapes=[pltpu.SemaphoreType.DMA((N_DEV - 1,)), pltpu.SemaphoreType.DMA((N_DEV - 1,)), pltpu.SemaphoreType.DMA],
        compiler_params=_params(),
    )(xs)


HBM = pl.BlockSpec(memory_space=pltpu.HBM)
SEM = pl.BlockSpec(memory_space=pltpu.SEMAPHORE)
EFFECT = pltpu.SideEffectType.DATAFLOW_SIDE_EFFECTING


def _hbm(a):
    return pltpu.with_memory_space_constraint(a, pltpu.HBM)


def _ici_copy(land, o, send_sem, recv_sem, sending):
    mx, my, mc = _me()
    px, py = _flip(mx, o & 2), _flip(my, o & 1)
    mine = _half_at(land, (2 * mx + my,), mc)
    return pltpu.make_async_remote_copy(
        src_ref=mine, dst_ref=mine if sending else _half_at(land, (2 * px + py,), mc),
        send_sem=send_sem, recv_sem=recv_sem, device_id=(px, py, mc), device_id_type=MESH)


N_PEERS = N_CHIPS - 1
DMA_SEM = pltpu.SemaphoreType.DMA(())


def gather_start(lands, groups, after):
    n_layers, n = len(lands), len(lands[0])
    flat = [a for layer in lands for a in layer]
    n_in = n * n_layers
    n_grp = len(groups)
    n_sem = 2 * n_layers * n_grp * N_PEERS
    first = lambda l, g, recv: ((l * n_grp + g) * 2 + recv) * N_PEERS

    def body(*refs):
        land = refs[:n_in]
        sems = refs[n_in + 1:n_in + 1 + n_sem]
        token = refs[-1]
        for l in range(n_layers):
            for g, members in enumerate(groups):
                for t in members:
                    for o in range(1, N_CHIPS):
                        _ici_copy(land[l * n + t], o, sems[first(l, g, 0) + o - 1], sems[first(l, g, 1) + o - 1],
                                  True).start()
        token[...] = jnp.zeros_like(token)

    outs = pl.pallas_call(
        body, name="gather_start",
        in_specs=[HBM] * n_in + [pl.BlockSpec(memory_space=pl.ANY)],
        out_specs=[SEM] * n_sem + [HBM] * n_in + [pl.BlockSpec(memory_space=pltpu.VMEM)],
        out_shape=[DMA_SEM] * n_sem + [pltpu.HBM(a.shape, a.dtype) for a in flat]
        + [jax.ShapeDtypeStruct((8, LANES), F32)],
        input_output_aliases={i: i + n_sem for i in range(n_in)},
        compiler_params=pltpu.CompilerParams(has_side_effects=EFFECT),
    )(*[_hbm(a) for a in flat], after)
    sems = [[(list(outs[first(l, g, 0):first(l, g, 0) + N_PEERS]), list(outs[first(l, g, 1):first(l, g, 1) + N_PEERS]))
             for g in range(n_grp)] for l in range(n_layers)]
    lands_thru = [list(outs[n_sem + l * n:n_sem + (l + 1) * n]) for l in range(n_layers)]
    return sems, lands_thru, outs[-1]


def gather_wait(tag, sems, lands, after):
    n = len(lands)
    send_sems, recv_sems = sems

    def body(*refs):
        land = refs[:n]
        send_r = refs[n:n + N_PEERS]
        recv_r = refs[n + N_PEERS:n + 2 * N_PEERS]
        for t in range(n):
            for o in range(1, N_CHIPS):
                _ici_copy(land[t], o, send_r[o - 1], recv_r[o - 1], True).wait_send()
                _ici_copy(land[t], o, send_r[o - 1], recv_r[o - 1], False).wait_recv()

    return list(pl.pallas_call(
        body, name=f"gather_wait_{tag}",
        in_specs=[HBM] * n + [SEM] * (2 * N_PEERS) + [pl.BlockSpec(memory_space=pl.ANY)],
        out_specs=[HBM] * n,
        out_shape=[pltpu.HBM(a.shape, a.dtype) for a in lands],
        input_output_aliases={i: i for i in range(n)},
        compiler_params=pltpu.CompilerParams(has_side_effects=EFFECT),
    )(*lands, *send_sems, *recv_sems, after))


def gather_forward(lands):
    n = len(lands)

    def body(*refs):
        dst = refs[n:2 * n]
        send_sems, recv_sems = refs[2 * n:]
        mx, my, mc = _me()
        fwds = []
        for t in range(n):
            for o in range(1, N_CHIPS):
                slot = 2 * _flip(mx, o & 2) + _flip(my, o & 1)
                mine = _half_at(dst[t], (slot,), mc)
                theirs = _half_at(dst[t], (slot,), 1 - mc)
                cp = pltpu.make_async_remote_copy(
                    src_ref=mine, dst_ref=mine, send_sem=send_sems.at[t, o - 1], recv_sem=recv_sems.at[t, o - 1],
                    device_id=(mx, my, 1 - mc), device_id_type=MESH)
                cp.start()
                fwds.append((cp, pltpu.make_async_remote_copy(
                    src_ref=theirs, dst_ref=theirs, send_sem=send_sems.at[t, o - 1], recv_sem=recv_sems.at[t, o - 1],
                    device_id=(mx, my, 1 - mc), device_id_type=MESH)))
        for cp, arrival in fwds:
            cp.wait_send()
            arrival.wait_recv()

    any_spec = pl.BlockSpec(memory_space=pl.ANY)
    return list(pl.pallas_call(
        body, name="gather_forward",
        in_specs=[any_spec] * n, out_specs=[any_spec] * n,
        out_shape=[jax.ShapeDtypeStruct(a.shape, a.dtype) for a in lands],
        input_output_aliases={t: t for t in range(n)},
        scratch_shapes=[pltpu.SemaphoreType.DMA((n, N_CHIPS - 1)), pltpu.SemaphoreType.DMA((n, N_CHIPS - 1))],
        compiler_params=_params(),
    )(*lands))


def _scatter_copy(src, land, o, send_sem, recv_sem):
    mx, my, mc = _me()
    px, py = _flip(mx, o & 2), _flip(my, o & 1)
    return pltpu.make_async_remote_copy(
        src_ref=src.at[2 * px + py], dst_ref=land.at[o - 1],
        send_sem=send_sem, recv_sem=recv_sem, device_id=(px, py, mc), device_id_type=MESH)


def scatter_start(pbs, tag, after):
    n = len(pbs)
    lands = [lax.empty((N_CHIPS - 1,) + p.shape[1:], p.dtype) for p in pbs]

    def body(*refs):
        src = refs[:n]
        land = refs[n:2 * n]
        send_sems = refs[2 * n + 1:2 * n + 1 + N_PEERS]
        recv_sems = refs[2 * n + 1 + N_PEERS:2 * n + 1 + 2 * N_PEERS]
        token = refs[-1]
        for t in range(n):
            for o in range(1, N_CHIPS):
                _scatter_copy(src[t], land[t], o, send_sems[o - 1], recv_sems[o - 1]).start()
        token[...] = jnp.zeros_like(token)

    n_sem = 2 * N_PEERS
    arrs = list(pbs) + lands
    outs = pl.pallas_call(
        body, name=f"scatter_start_{tag}",
        in_specs=[HBM] * (2 * n) + [pl.BlockSpec(memory_space=pl.ANY)],
        out_specs=[SEM] * n_sem + [HBM] * (2 * n) + [pl.BlockSpec(memory_space=pltpu.VMEM)],
        out_shape=[DMA_SEM] * n_sem + [pltpu.HBM(a.shape, a.dtype) for a in arrs]
        + [jax.ShapeDtypeStruct((8, LANES), F32)],
        input_output_aliases={i: i + n_sem for i in range(2 * n)},
        compiler_params=pltpu.CompilerParams(has_side_effects=EFFECT),
    )(*[_hbm(a) for a in arrs], after)
    return (list(outs[:N_PEERS]), list(outs[N_PEERS:n_sem]), list(outs[n_sem:n_sem + n]),
            list(outs[n_sem + n:n_sem + 2 * n]), outs[-1])


def scatter_wait(tag, send_sems, recv_sems, pbs, lands, after):
    n = len(pbs)

    def body(*refs):
        src = refs[:n]
        land = refs[n:2 * n]
        send_r = refs[2 * n:2 * n + N_PEERS]
        recv_r = refs[2 * n + N_PEERS:2 * n + 2 * N_PEERS]
        for t in range(n):
            for o in range(1, N_CHIPS):
                cp = _scatter_copy(src[t], land[t], o, send_r[o - 1], recv_r[o - 1])
                cp.wait_send()
                cp.wait_recv()

    arrs = list(pbs) + list(lands)
    outs = pl.pallas_call(
        body, name=f"scatter_wait_{tag}",
        in_specs=[HBM] * (2 * n) + [SEM] * (2 * N_PEERS) + [pl.BlockSpec(memory_space=pl.ANY)],
        out_specs=[HBM] * (2 * n),
        out_shape=[pltpu.HBM(a.shape, a.dtype) for a in arrs],
        input_output_aliases={i: i for i in range(2 * n)},
        compiler_params=pltpu.CompilerParams(has_side_effects=EFFECT),
    )(*arrs, *send_sems, *recv_sems, after)
    return list(outs[n:])


def _pair_copy(src, land, send_sem, recv_sem):
    mx, my, mc = _me()
    return pltpu.make_async_remote_copy(
        src_ref=_half_at(src, (slice(None),), 1 - mc), dst_ref=land, send_sem=send_sem, recv_sem=recv_sem,
        device_id=(mx, my, 1 - mc), device_id_type=MESH)


def pair_start(gs, tag, after):
    n = len(gs)
    lands = [lax.empty((g.shape[0],) + _half_shape(*g.shape[1:]), g.dtype) for g in gs]

    def body(*refs):
        src = refs[:n]
        land = refs[n:2 * n]
        send_sem, recv_sem = refs[2 * n + 1], refs[2 * n + 2]
        token = refs[-1]
        for t in range(n):
            _pair_copy(src[t], land[t], send_sem, recv_sem).start()
        token[...] = jnp.zeros_like(token)

    arrs = list(gs) + lands
    outs = pl.pallas_call(
        body, name=f"pair_start_{tag}",
        in_specs=[HBM] * (2 * n) + [pl.BlockSpec(memory_space=pl.ANY)],
        out_specs=[SEM, SEM] + [HBM] * (2 * n) + [pl.BlockSpec(memory_space=pltpu.VMEM)],
        out_shape=[DMA_SEM, DMA_SEM] + [pltpu.HBM(a.shape, a.dtype) for a in arrs] + [jax.ShapeDtypeStruct((8, LANES), F32)],
        input_output_aliases={i: i + 2 for i in range(2 * n)},
        compiler_params=pltpu.CompilerParams(has_side_effects=EFFECT),
    )(*[_hbm(a) for a in arrs], after)
    return outs[0], outs[1], list(outs[2:2 + n]), list(outs[2 + n:2 + 2 * n]), outs[-1]


def pair_wait(tag, send_sem, recv_sem, gs, lands, after):
    n = len(gs)

    def body(*refs):
        src = refs[:n]
        land = refs[n:2 * n]
        send_r, recv_r = refs[2 * n], refs[2 * n + 1]
        for t in range(n):
            cp = _pair_copy(src[t], land[t], send_r, recv_r)
            cp.wait_send()
            cp.wait_recv()

    arrs = list(gs) + list(lands)
    outs = pl.pallas_call(
        body, name=f"pair_wait_{tag}",
        in_specs=[HBM] * (2 * n) + [SEM, SEM, pl.BlockSpec(memory_space=pl.ANY)],
        out_specs=[HBM] * (2 * n),
        out_shape=[pltpu.HBM(a.shape, a.dtype) for a in arrs],
        input_output_aliases={i: i for i in range(2 * n)},
        compiler_params=pltpu.CompilerParams(has_side_effects=EFFECT),
    )(*arrs, send_sem, recv_sem, after)
    return list(outs[:n]), list(outs[n:])


def _gather8_copy(x, land, o, send_sem, recv_sem, sending):
    mx, my, mc = _me()
    px, py, pc = _flip(mx, o & 4), _flip(my, o & 2), _flip(mc, o & 1)
    slot = 4 * mx + 2 * my + mc if sending else 4 * px + 2 * py + pc
    return pltpu.make_async_remote_copy(
        src_ref=x, dst_ref=land.at[slot], send_sem=send_sem, recv_sem=recv_sem,
        device_id=(px, py, pc), device_id_type=MESH)


def gather8_start(x, land, after):
    n_peer = N_DEV - 1

    def body(x_ref, land_ref, after_ref, *rest):
        send_sems, recv_sems = rest[:n_peer], rest[n_peer:2 * n_peer]
        token = rest[-1]
        for o in range(1, N_DEV):
            _gather8_copy(x_ref, land_ref, o, send_sems[o - 1], recv_sems[o - 1], True).start()
        token[...] = jnp.zeros_like(token)

    outs = pl.pallas_call(
        body, name="gather8_start",
        in_specs=[HBM, HBM, pl.BlockSpec(memory_space=pl.ANY)],
        out_specs=[SEM] * (2 * n_peer) + [HBM, HBM, pl.BlockSpec(memory_space=pltpu.VMEM)],
        out_shape=[DMA_SEM] * (2 * n_peer) + [pltpu.HBM(x.shape, x.dtype), pltpu.HBM(land.shape, land.dtype),
                                              jax.ShapeDtypeStruct((8, LANES), F32)],
        input_output_aliases={0: 2 * n_peer, 1: 2 * n_peer + 1},
        compiler_params=pltpu.CompilerParams(has_side_effects=EFFECT),
    )(_hbm(x), _hbm(land), after)
    return list(outs[:n_peer]), list(outs[n_peer:2 * n_peer]), outs[2 * n_peer], outs[2 * n_peer + 1], outs[-1]


def gather8_wait(send_sems, recv_sems, x, land, after):
    n_peer = N_DEV - 1

    def body(x_ref, land_ref, *rest):
        send_r, recv_r = rest[:n_peer], rest[n_peer:2 * n_peer]
        for o in range(1, N_DEV):
            _gather8_copy(x_ref, land_ref, o, send_r[o - 1], recv_r[o - 1], True).wait_send()
            _gather8_copy(x_ref, land_ref, o, send_r[o - 1], recv_r[o - 1], False).wait_recv()

    return pl.pallas_call(
        body, name="gather8_wait",
        in_specs=[HBM, HBM] + [SEM] * (2 * n_peer) + [pl.BlockSpec(memory_space=pl.ANY)],
        out_specs=[HBM, HBM],
        out_shape=[pltpu.HBM(x.shape, x.dtype), pltpu.HBM(land.shape, land.dtype)],
        input_output_aliases={0: 0, 1: 1},
        compiler_params=pltpu.CompilerParams(has_side_effects=EFFECT),
    )(x, land, *send_sems, *recv_sems, after)[1]


def pair_fill_halves(fs):
    n = len(fs)

    def body(*refs):
        dst = refs[n:2 * n]
        send_sems, recv_sems = refs[2 * n:]
        mx, my, mc = _me()
        copies = []
        for t in range(n):
            mine = _half_at(dst[t], (slice(None),), mc)
            theirs = _half_at(dst[t], (slice(None),), 1 - mc)
            cp = pltpu.make_async_remote_copy(
                src_ref=mine, dst_ref=mine, send_sem=send_sems.at[t], recv_sem=recv_sems.at[t],
                device_id=(mx, my, 1 - mc), device_id_type=MESH)
            cp.start()
            copies.append((cp, pltpu.make_async_remote_copy(
                src_ref=theirs, dst_ref=theirs, send_sem=send_sems.at[t], recv_sem=recv_sems.at[t],
                device_id=(mx, my, 1 - mc), device_id_type=MESH)))
        for cp, arrival in copies:
            cp.wait_send()
            arrival.wait_recv()

    any_spec = pl.BlockSpec(memory_space=pl.ANY)
    return pl.pallas_call(
        body, name="pair_fill_halves",
        in_specs=[any_spec] * n, out_specs=[any_spec] * n,
        out_shape=[jax.ShapeDtypeStruct(f.shape, f.dtype) for f in fs],
        input_output_aliases={t: t for t in range(n)},
        scratch_shapes=[pltpu.SemaphoreType.DMA((n,)), pltpu.SemaphoreType.DMA((n,))],
        compiler_params=_params(),
    )(*fs)


def _pack_rows(parts, d):
    rows, spans = [], []
    at = 0
    for p in parts:
        flat = p.reshape(-1)
        n_rows = -(-flat.shape[0] // (8 * d)) * 8
        flat = jnp.pad(flat, (0, n_rows * d - flat.shape[0]))
        rows.append(flat.reshape(n_rows, d))
        spans.append((at, p.shape))
        at += n_rows
    return jnp.concatenate(rows, axis=0), spans


def _unpack_rows(packed, spans):
    out = []
    for at, shape in spans:
        n = math.prod(shape)
        d = packed.shape[1]
        n_rows = -(-n // d)
        out.append(packed[at:at + n_rows].reshape(-1)[:n].reshape(shape))
    return out


def _rotate_half_matrix():
    half = QK_ROPE // 2
    idx = jnp.arange(QK_ROPE)
    src = jnp.where(idx < half, idx + half, idx - half)
    sign = jnp.where(idx < half, -1.0, 1.0)
    return (jnp.zeros((QK_ROPE, QK_ROPE), F32).at[src, idx].set(sign)).astype(BF16)


def kernel(x, c, positions, ada_w, ada_b, ffn1_norm, ffn1_w_gate, ffn1_w_up, ffn1_w_down, mix_norm, w_in, pool_w, pool_scale, q_a_norm, w_q_b, kv_a_norm, w_kv_b, w_out, ffn2_norm, ffn2_w_gate, ffn2_w_up, ffn2_w_down, final_norm, loss_target, m_ada_w, m_ada_b, m_ffn1_norm, m_ffn1_w_gate, m_ffn1_w_up, m_ffn1_w_down, m_mix_norm, m_w_in, m_pool_w, m_pool_scale, m_q_a_norm, m_w_q_b, m_kv_a_norm, m_w_kv_b, m_w_out, m_ffn2_norm, m_ffn2_w_gate, m_ffn2_w_up, m_ffn2_w_down, m_final_norm, v_ada_w, v_ada_b, v_ffn1_norm, v_ffn1_w_gate, v_ffn1_w_up, v_ffn1_w_down, v_mix_norm, v_w_in, v_pool_w, v_pool_scale, v_q_a_norm, v_w_q_b, v_kv_a_norm, v_w_kv_b, v_w_out, v_ffn2_norm, v_ffn2_w_gate, v_ffn2_w_up, v_ffn2_w_down, v_final_norm):
    mx, my, mc = _me()
    chip = 2 * mx + my
    half = jnp.reshape(mc, (1,)).astype(jnp.int32)
    chip1 = jnp.reshape(chip, (1,)).astype(jnp.int32)
    n_layers, d, ada_cols = ada_w.shape
    xt = x[0]
    tgt = loss_target[0]

    inv_freq = 1.0 / (ROPE_THETA ** (jnp.arange(0, QK_ROPE, 2, dtype=F32) / QK_ROPE))
    ang = positions[0].astype(F32)[:, None] * inv_freq
    ang = jnp.concatenate([ang, ang], axis=-1)
    cos, sin = jnp.cos(ang), jnp.sin(ang)
    rot = _rotate_half_matrix()
    rot_t = rot.T

    c_all = exchange8(c, True).reshape(N_DEV, d)
    c16 = jnp.pad(c_all, ((0, 8), (0, 0)))
    ada_b_loc = lax.dynamic_slice_in_dim(ada_b, chip * ada_cols, ada_cols, axis=1).reshape(n_layers, 1, ada_cols)
    mod_part = ada_fwd(c16, ada_w, ada_b_loc)[:, :N_DEV]
    mod_got = exchange8(jnp.transpose(mod_part, (1, 0, 2)), False)
    mod = jnp.transpose(mod_got.reshape(N_CHIPS, 2, n_layers, ada_cols)[:, 0], (1, 0, 2))
    mod = mod.reshape(n_layers, 9, 1, d)

    tr = lambda a: jnp.transpose(a, (0, 2, 1))
    local = [tr(ffn1_w_gate), tr(ffn1_w_up), ffn1_w_down, tr(w_in), tr(w_q_b), w_kv_b, w_out,
             tr(ffn2_w_gate), tr(ffn2_w_up), ffn2_w_down]
    ffn1_pos, rest_pos = (0, 1, 2), tuple(range(3, len(local)))
    placed = [cast_place(w, chip1) for w in local]
    lands = [[placed[t][l] for t in range(len(local))] for l in range(n_layers)]
    g_sems, lands_fly, g_token = gather_start(lands, (ffn1_pos, rest_pos), mod)
    gathered = []

    row = lambda a, l: a[l].reshape(1, -1)
    saved = []
    for l in range(n_layers):
        g1, u1, d1 = gather_forward(gather_wait(
            f"{l}a", g_sems[l][0], [lands_fly[l][t] for t in ffn1_pos], xt if l else g_token))
        sv = dict(x0=xt)
        xt, sv["h1"], sv["gate1"], sv["up1"], sv["y1"] = ffn_fwd(
            xt, row(ffn1_norm, l), mod[l, 0], mod[l, 1], mod[l, 2], g1, u1, d1)
        sv["x1"] = xt
        win, wq, wkv, wout, g2, u2, d2 = gather_forward(gather_wait(
            f"{l}b", g_sems[l][1], [lands_fly[l][t] for t in rest_pos], xt))
        gathered.append([g1, u1, d1, win, wq, wkv, wout, g2, u2, d2])
        win = win.reshape(-1, d)
        sv["h2"], u, cq, ckv, kr = mix_in_fwd(xt, row(mix_norm, l), mod[l, 3], mod[l, 4], win)
        sv["cq"], sv["ckv"] = cq, ckv
        yp, sv["diff"] = pool_fwd(u, pool_w[l], row(pool_scale, l))
        qh, kh, vh, sv["ql"], sv["kvl"] = mla_qkv_fwd(
            cq, ckv, kr, row(q_a_norm, l), row(kv_a_norm, l), wq, wkv, cos, sin, rot)
        sv["qkv"] = (qh, kh, vh)
        om = attn_fwd(qh, kh, vh)
        xt, sv["ycat"], sv["y2"] = out_proj_fwd(yp, om, wout, xt, mod[l, 5])
        sv["x2"] = xt
        xt, sv["h3"], sv["gate3"], sv["up3"], sv["y3"] = ffn_fwd(
            xt, row(ffn2_norm, l), mod[l, 6], mod[l, 7], mod[l, 8], g2, u2, d2)
        saved.append(sv)

    loss_vec, dx, d_final_norm = final_loss(xt, final_norm.reshape(1, d), tgt)
    loss = lax.psum(loss_vec[0, 0], ("x", "y", "c"))

    none = [None] * n_layers
    dmods, dnorm1, dnorm2, dnorm3 = list(none), list(none), list(none), list(none)
    dpw, dps, dqan_l, dkvan_l = list(none), list(none), list(none), list(none)
    reduced = [None] * len(local)
    stages = []
    sel_of = lambda l: jnp.stack([mc, chip, jnp.asarray(l, mc.dtype)]).astype(jnp.int32)

    def to_chips(job, after_wait, after_start):
        send, recv, g_fly, lands_p = job.pop("pair")
        g_fly, got = pair_wait(job["tag"], send, recv, g_fly, lands_p, after_wait)
        job["sums"] = [pair_add(g, ra, half) for g, ra in zip(g_fly, got)]
        job["scatter"] = scatter_start([pb for _, pb in job["sums"]], job["tag"], after_start)
        return job["scatter"][4][0, 0]

    def finish(job, after):
        s_send, s_recv, pbs_fly, lands_j, _ = job.pop("scatter")
        parts = scatter_wait(job["tag"], s_send, s_recv, pbs_fly, lands_j, after)
        for t, (p32, _), rb, shp in zip(job["pos"], job["sums"], parts, job["shapes"]):
            reduced[t] = chip_sum(p32, rb, sel_of(job["l"]), (n_layers,) + shp, reduced[t])

    def checkpoint(tag, l, positions, grads_, done, before_scatter=None):
        send, recv, g_fly, lands_p, tok = pair_start(grads_, tag, done)
        order = tok[0, 0]
        if stages:
            order = order + to_chips(stages[-1], done, done if before_scatter is None else before_scatter)
        if len(stages) >= 3:
            finish(stages[-3], done)
        stages.append(dict(tag=tag, l=l, pos=positions, shapes=[g.shape[1:] for g in grads_],
                           pair=(send, recv, g_fly, lands_p)))
        return order

    order = None

    for l in reversed(range(n_layers)):
        sv = saved[l]
        g1, u1, d1, win, wq, wkv, wout, g2, u2, d2 = gathered[l]
        win = win.reshape(-1, d)
        gt3 = mod[l, 8] if order is None else mod[l, 8] + order
        dy, a, dgt, dup = ffn_bwd_act(dx, sv["gate3"], sv["up3"], gt3, d2)
        dx, dvec3 = ffn_bwd_in(dx, sv["x2"], sv["y3"], dgt, dup, row(ffn2_norm, l), mod[l, 7], g2, u2)
        g_g2, g_u2, g_d2 = tn_mm(dgt, sv["h3"][None]), tn_mm(dup, sv["h3"][None]), tn_mm(a, dy[None])
        dy2, dyp, dom, dg2 = out_proj_bwd(dx, sv["y2"], mod[l, 5], wout)
        g_wout = tn_mm(sv["ycat"], dy2[None])
        qh, kh, vh = sv["qkv"]
        dqh, dkh, dvh = attn_bwd(qh, kh, vh, dom)
        dcq, dckv, dkr_in, gq, gkv, dqan_l[l], dkvan_l[l] = mla_qkv_bwd(
            dqh, dkh, dvh, sv["cq"], sv["ckv"], row(q_a_norm, l), row(kv_a_norm, l), wq, wkv, cos, sin, rot_t)
        g_wq, g_wkv = tn_mm(gq, sv["ql"][None]), tn_mm(sv["kvl"][None], gkv)
        du, dpw[l], dps[l] = pool_bwd(dyp, sv["diff"], pool_w[l], row(pool_scale, l))
        dx, dz, dvec2 = mix_in_bwd(dx, du, dcq, dckv, dkr_in, sv["x1"], row(mix_norm, l), mod[l, 4], win)
        g_win = tn_mm(dz[None], sv["h2"][None]).reshape(N_CHIPS, -1, d)
        order = checkpoint(f"{l}a", l, rest_pos, [g_win, g_wq, g_wkv, g_wout, g_g2, g_u2, g_d2], dx)
        dy, a, dgt, dup = ffn_bwd_act(dx, sv["gate1"], sv["up1"], mod[l, 2] + order, d1)
        dx, dvec1 = ffn_bwd_in(dx, sv["x0"], sv["y1"], dgt, dup, row(ffn1_norm, l), mod[l, 1], g1, u1)
        g_g1, g_u1, g_d1 = tn_mm(dgt, sv["h1"][None]), tn_mm(dup, sv["h1"][None]), tn_mm(a, dy[None])
        dmods[l] = jnp.concatenate([dvec1[0:3], dvec2[0:2], dg2, dvec3[0:3]], axis=0)
        dnorm1[l], dnorm2[l], dnorm3[l] = dvec1[3], dvec2[3], dvec3[3]
        if l == 0:
            small_parts = [jnp.stack(dmods), jnp.stack(dnorm1), jnp.stack(dnorm2), jnp.stack(dnorm3), d_final_norm,
                           jnp.stack(dps), jnp.stack(dqan_l), jnp.stack(dkvan_l), jnp.stack(dpw)]
            packed, spans = _pack_rows(small_parts, d)
            me = 4 * mx + 2 * my + mc
            small_land = lax.dynamic_update_index_in_dim(lax.empty((N_DEV,) + packed.shape, F32), packed, me, 0)
            small_fly = gather8_start(packed, small_land, dx)

        order = checkpoint(f"{l}b", l, ffn1_pos, [g_g1, g_u1, g_d1], dx, small_fly[4] if l == 0 else None)

    to_chips(stages[-1], dx, dx)
    gathered_small = gather8_wait(small_fly[0], small_fly[1], small_fly[2], small_fly[3], stages[-1]["scatter"][4])
    total = sum_devices(gathered_small)
    (g_ada_b, g_n1, g_n2, g_n3, g_fn, g_ps, g_qan, g_kvan, g_pw) = _unpack_rows(total, spans)
    dmod_all = gathered_small[:, :9 * n_layers].reshape(N_DEV, n_layers, 9 * d)
    dmod_loc = lax.dynamic_slice_in_dim(dmod_all, chip * ada_cols, ada_cols, axis=2)
    dmod16 = jnp.pad(jnp.transpose(dmod_loc, (1, 0, 2)), ((0, 0), (0, 8), (0, 0)))
    g_ada_w = ada_bwd(c16, dmod16)

    grads = [g_ada_w, g_ada_b, g_n1, None, None, None, g_n2, None, g_pw, g_ps, g_qan, None, g_kvan, None, None, g_n3,
             None, None, None, g_fn]
    weights = [ada_w, ada_b, ffn1_norm, ffn1_w_gate, ffn1_w_up, ffn1_w_down, mix_norm, w_in, pool_w, pool_scale,
               q_a_norm, w_q_b, kv_a_norm, w_kv_b, w_out, ffn2_norm, ffn2_w_gate, ffn2_w_up, ffn2_w_down, final_norm]
    ms = [m_ada_w, m_ada_b, m_ffn1_norm, m_ffn1_w_gate, m_ffn1_w_up, m_ffn1_w_down, m_mix_norm, m_w_in, m_pool_w,
          m_pool_scale, m_q_a_norm, m_w_q_b, m_kv_a_norm, m_w_kv_b, m_w_out, m_ffn2_norm, m_ffn2_w_gate, m_ffn2_w_up,
          m_ffn2_w_down, m_final_norm]
    vs = [v_ada_w, v_ada_b, v_ffn1_norm, v_ffn1_w_gate, v_ffn1_w_up, v_ffn1_w_down, v_mix_norm, v_w_in, v_pool_w,
          v_pool_scale, v_q_a_norm, v_w_q_b, v_kv_a_norm, v_w_kv_b, v_w_out, v_ffn2_norm, v_ffn2_w_gate, v_ffn2_w_up,
          v_ffn2_w_down, v_final_norm]
    transposed = (3, 4, 7, 11, 16, 17)
    outs = [None] * len(weights)
    for i, (w, g, m, v) in enumerate(zip(weights, grads, ms, vs)):
        if g is not None:
            g = g.reshape(w.shape)
            outs[i] = (g,) + adamw(w, g, m, v)
    for job in stages[-3:]:
        finish(job, outs[0][1])
    g_local = iter(pair_fill_halves(reduced))
    for i, (w, g, m, v) in enumerate(zip(weights, grads, ms, vs)):
        if g is None:
            g = next(g_local)
            if i in transposed:
                outs[i] = tuple(tr(t) for t in (g,) + adamw(tr(w), g, tr(m), tr(v)))
            else:
                outs[i] = (g,) + adamw(w, g, m, v)
    return (loss, dx.reshape(x.shape), *[t[0] for t in outs], *[t[1] for t in outs], *[t[2] for t in outs],
            *[t[3] for t in outs])
```

```python
import math

import jax
import jax.numpy as jnp
from jax import lax
from jax.experimental import pallas as pl
from jax.experimental.pallas import tpu as pltpu

F32 = jnp.float32
BF16 = jnp.bfloat16
MESH = pl.DeviceIdType.MESH

EPS = 1e-6
ROPE_THETA = 10000.0
N_HEADS = 4
QK_NOPE = 128
QK_ROPE = 64
V_HEAD = 128
POOL_WINDOWS = (2, 4, 8, 16)
POOL_GC = 128
POOL_WIDTH = POOL_GC * len(POOL_WINDOWS)
Q_LORA = 384
KV_LORA = 256
SOFTMAX_SCALE = 1.0 / math.sqrt(QK_NOPE + QK_ROPE)
N_CHIPS = 4
N_DEV = 8

ADAM_LR = 0.001
ADAM_B1 = 0.9
ADAM_B2 = 0.999
ADAM_EPS = 1e-08
ADAM_WD = 0.01
ADAM_STEP = 10

ROW_TILE = 512
ATT_TILE = 256
VMEM_LIMIT = 56 * 1024 * 1024
BF16_ROWS = 16
LANES = 128


def _params(sem=None, vmem=VMEM_LIMIT):
    return pltpu.CompilerParams(dimension_semantics=sem, vmem_limit_bytes=vmem)


def _dot(a, b):
    return jnp.dot(a, b, preferred_element_type=F32)


def _dot_nt(a, b):
    return lax.dot_general(a, b, (((1,), (1,)), ((), ())), preferred_element_type=F32)


def _dot_tn(a, b):
    return lax.dot_general(a, b, (((0,), (0,)), ((), ())), preferred_element_type=F32)


def _dot_exact(t, perm):
    t1 = t.astype(BF16)
    r1 = t - t1.astype(F32)
    t2 = r1.astype(BF16)
    t3 = (r1 - t2.astype(F32)).astype(BF16)
    return _dot(t1, perm) + _dot(t2, perm) + _dot(t3, perm)


def _sum0(a):
    return jnp.sum(a, axis=0, keepdims=True)


def _rms(xt):
    r = lax.rsqrt(jnp.mean(xt * xt, axis=-1, keepdims=True) + EPS)
    return xt * r, r


def _rms_bwd(dy, xt, g):
    xhat, r = _rms(xt)
    dxhat = dy * g
    dx = r * (dxhat - xhat * jnp.mean(dxhat * xhat, axis=-1, keepdims=True))
    return dx, _sum0(dy * xhat)


def _normmod_bwd(dh, xt, gn, sc):
    xhat, _ = _rms(xt)
    dn = dh * (1.0 + sc)
    dx, dgn = _rms_bwd(dn, xt, gn)
    return dx, _sum0(dh), _sum0(dh * (xhat * gn)), dgn


def _row_tile(s):
    return min(s, ROW_TILE)


def _full(shape):
    n = len(shape)
    return pl.BlockSpec(shape, lambda *_: (0,) * n)


def _resident(shape):
    n = len(shape)
    return pl.BlockSpec(shape, lambda *_: (0,) * n, pipeline_mode=pl.Buffered(1))


def ffn_fwd(x, gn, sh, sc, gt, wg, wu, wd):
    s, d = x.shape
    k_chunks, fs, _ = wg.shape
    tm = _row_tile(s)

    def body(x_ref, gn_ref, sh_ref, sc_ref, gt_ref, wg_ref, wu_ref, wd_ref,
             xo_ref, h_ref, gate_ref, up_ref, y_ref):
        xt = x_ref[...]
        xhat, _ = _rms(xt)
        h = (xhat * gn_ref[...] * (1.0 + sc_ref[...]) + sh_ref[...]).astype(BF16)
        h_ref[...] = h
        y = jnp.zeros((tm, d), F32)
        for k in range(k_chunks):
            gate = _dot_nt(h, wg_ref[k])
            up = _dot_nt(h, wu_ref[k])
            gate_ref[k] = gate.astype(BF16)
            up_ref[k] = up.astype(BF16)
            y += _dot((gate * jax.nn.sigmoid(gate) * up).astype(BF16), wd_ref[k])
        y_ref[...] = y.astype(BF16)
        xo_ref[...] = xt + 0.5 * gt_ref[...] * y

    row = pl.BlockSpec((tm, d), lambda i: (i, 0))
    vec = pl.BlockSpec((1, d), lambda i: (0, 0))
    act = pl.BlockSpec((k_chunks, tm, fs), lambda i: (0, i, 0))
    return pl.pallas_call(
        body, name="ffn_fwd",
        grid=(s // tm,),
        in_specs=[row, vec, vec, vec, vec, _resident(wg.shape), _resident(wu.shape), _resident(wd.shape)],
        out_specs=[row, row, act, act, row],
        out_shape=[jax.ShapeDtypeStruct((s, d), F32), jax.ShapeDtypeStruct((s, d), BF16),
                   jax.ShapeDtypeStruct((k_chunks, s, fs), BF16), jax.ShapeDtypeStruct((k_chunks, s, fs), BF16),
                   jax.ShapeDtypeStruct((s, d), BF16)],
        compiler_params=_params(("arbitrary",)),
    )(x, gn, sh, sc, gt, wg, wu, wd)


def ffn_bwd_act(dxn, gate, up, gt, wd):
    s, d = dxn.shape
    k_chunks, fs, _ = wd.shape
    tm = _row_tile(s)

    def body(dxn_ref, gate_ref, up_ref, gt_ref, wd_ref, dy_ref, a_ref, dgate_ref, dup_ref):
        dy = (0.5 * gt_ref[...] * dxn_ref[...]).astype(BF16)
        dy_ref[...] = dy
        for k in range(k_chunks):
            da = _dot_nt(dy, wd_ref[k])
            g = gate_ref[k].astype(F32)
            u = up_ref[k].astype(F32)
            sg = jax.nn.sigmoid(g)
            sl = g * sg
            a_ref[k] = (sl * u).astype(BF16)
            dgate_ref[k] = (da * u * (sg * (1.0 + g * (1.0 - sg)))).astype(BF16)
            dup_ref[k] = (da * sl).astype(BF16)

    row = pl.BlockSpec((tm, d), lambda i: (i, 0))
    act = pl.BlockSpec((k_chunks, tm, fs), lambda i: (0, i, 0))
    act_shape = jax.ShapeDtypeStruct((k_chunks, s, fs), BF16)
    return pl.pallas_call(
        body, name="ffn_bwd_act",
        grid=(s // tm,),
        in_specs=[row, act, act, pl.BlockSpec((1, d), lambda i: (0, 0)), _resident(wd.shape)],
        out_specs=[row, act, act, act],
        out_shape=[jax.ShapeDtypeStruct((s, d), BF16), act_shape, act_shape, act_shape],
        compiler_params=_params(("arbitrary",)),
    )(dxn, gate, up, gt, wd)


def ffn_bwd_in(dxn, x, y, dgate, dup, gn, sc, wg, wu):
    s, d = x.shape
    k_chunks, fs, _ = wg.shape
    tm = _row_tile(s)

    def body(dxn_ref, x_ref, y_ref, dgate_ref, dup_ref, gn_ref, sc_ref, wg_ref, wu_ref, dx_ref, dvec_ref):
        i = pl.program_id(0)

        @pl.when(i == 0)
        def _():
            dvec_ref[...] = jnp.zeros_like(dvec_ref)

        dh = jnp.zeros((tm, d), F32)
        for k in range(k_chunks):
            dh += _dot(dgate_ref[k], wg_ref[k]) + _dot(dup_ref[k], wu_ref[k])
        dxn_t = dxn_ref[...]
        dx, dsh, dsc, dgn = _normmod_bwd(dh, x_ref[...], gn_ref[...], sc_ref[...])
        dx_ref[...] = dx + dxn_t
        dvec_ref[0:1, :] += dsh
        dvec_ref[1:2, :] += dsc
        dvec_ref[2:3, :] += _sum0(0.5 * dxn_t * y_ref[...].astype(F32))
        dvec_ref[3:4, :] += dgn

    row = pl.BlockSpec((tm, d), lambda i: (i, 0))
    vec = pl.BlockSpec((1, d), lambda i: (0, 0))
    act = pl.BlockSpec((k_chunks, tm, fs), lambda i: (0, i, 0))
    return pl.pallas_call(
        body, name="ffn_bwd_in",
        grid=(s // tm,),
        in_specs=[row, row, row, act, act, vec, vec, _resident(wg.shape), _resident(wu.shape)],
        out_specs=[row, pl.BlockSpec((8, d), lambda i: (0, 0))],
        out_shape=[jax.ShapeDtypeStruct((s, d), F32), jax.ShapeDtypeStruct((8, d), F32)],
        compiler_params=_params(("arbitrary",)),
    )(dxn, x, y, dgate, dup, gn, sc, wg, wu)


def tn_mm(a, b):
    ga, s, m = a.shape
    gb, _, n = b.shape
    g = max(ga, gb)

    def body(a_ref, b_ref, o_ref):
        o_ref[...] = _dot_tn(a_ref[...], b_ref[...])

    a_spec = pl.BlockSpec((None, s, m), (lambda gi: (gi, 0, 0)) if ga > 1 else (lambda gi: (0, 0, 0)))
    b_spec = pl.BlockSpec((None, s, n), (lambda gi: (gi, 0, 0)) if gb > 1 else (lambda gi: (0, 0, 0)))
    return pl.pallas_call(
        body, name="tn_mm",
        grid=(g,), in_specs=[a_spec, b_spec], out_specs=pl.BlockSpec((None, m, n), lambda gi: (gi, 0, 0)),
        out_shape=jax.ShapeDtypeStruct((g, m, n), F32),
        compiler_params=_params(("arbitrary",)),
    )(a, b)


def mix_in_fwd(x, gn, sh, sc, w_in_t):
    s, d = x.shape
    tm = _row_tile(s)
    o1, o2, o3 = POOL_WIDTH, POOL_WIDTH + Q_LORA, POOL_WIDTH + Q_LORA + KV_LORA

    def body(x_ref, gn_ref, sh_ref, sc_ref, w_ref, h_ref, u_ref, cq_ref, ckv_ref, kr_ref):
        xhat, _ = _rms(x_ref[...])
        h = (xhat * gn_ref[...] * (1.0 + sc_ref[...]) + sh_ref[...]).astype(BF16)
        h_ref[...] = h
        z = _dot_nt(h, w_ref[0:o3, :])
        u_ref[...] = z[:, 0:o1]
        cq_ref[...] = z[:, o1:o2]
        ckv_ref[...] = z[:, o2:o3]
        kr_ref[...] = _dot_nt(h, w_ref[o3:, :])

    row = lambda w: pl.BlockSpec((tm, w), lambda i: (i, 0))
    vec = pl.BlockSpec((1, d), lambda i: (0, 0))
    return pl.pallas_call(
        body, name="mix_in_fwd",
        grid=(s // tm,),
        in_specs=[row(d), vec, vec, vec, _full(w_in_t.shape)],
        out_specs=[row(d), row(POOL_WIDTH), row(Q_LORA), row(KV_LORA), row(QK_ROPE)],
        out_shape=[jax.ShapeDtypeStruct((s, d), BF16), jax.ShapeDtypeStruct((s, POOL_WIDTH), F32),
                   jax.ShapeDtypeStruct((s, Q_LORA), F32), jax.ShapeDtypeStruct((s, KV_LORA), F32),
                   jax.ShapeDtypeStruct((s, QK_ROPE), F32)],
        compiler_params=_params(("arbitrary",)),
    )(x, gn, sh, sc, w_in_t)


def mix_in_bwd(dxn, du, dcq, dckv, dkr, x, gn, sc, w_in_t):
    s, d = x.shape
    tm = _row_tile(s)
    o1, o2, o3 = POOL_WIDTH, POOL_WIDTH + Q_LORA, POOL_WIDTH + Q_LORA + KV_LORA
    n_z = w_in_t.shape[0]

    def body(dxn_ref, du_ref, dcq_ref, dckv_ref, dkr_ref, x_ref, gn_ref, sc_ref, w_ref, dx_ref, dz_ref, dvec_ref):
        i = pl.program_id(0)

        @pl.when(i == 0)
        def _():
            dvec_ref[...] = jnp.zeros_like(dvec_ref)

        dub = du_ref[...].astype(BF16)
        dqb = dcq_ref[...].astype(BF16)
        dkb = dckv_ref[...].astype(BF16)
        drb = dkr_ref[...].astype(BF16)
        dz_ref[:, 0:o1] = dub
        dz_ref[:, o1:o2] = dqb
        dz_ref[:, o2:o3] = dkb
        dz_ref[:, o3:] = drb
        dh = (_dot(dub, w_ref[0:o1, :]) + _dot(dqb, w_ref[o1:o2, :]) + _dot(dkb, w_ref[o2:o3, :])
              + _dot(drb, w_ref[o3:, :]))
        dx, dsh, dsc, dgn = _normmod_bwd(dh, x_ref[...], gn_ref[...], sc_ref[...])
        dx_ref[...] = dx + dxn_ref[...]
        dvec_ref[0:1, :] += dsh
        dvec_ref[1:2, :] += dsc
        dvec_ref[3:4, :] += dgn

    row = lambda w: pl.BlockSpec((tm, w), lambda i: (i, 0))
    vec = pl.BlockSpec((1, d), lambda i: (0, 0))
    return pl.pallas_call(
        body, name="mix_in_bwd",
        grid=(s // tm,),
        in_specs=[row(d), row(POOL_WIDTH), row(Q_LORA), row(KV_LORA), row(QK_ROPE), row(d), vec, vec,
                  _full(w_in_t.shape)],
        out_specs=[row(d), row(n_z), pl.BlockSpec((8, d), lambda i: (0, 0))],
        out_shape=[jax.ShapeDtypeStruct((s, d), F32), jax.ShapeDtypeStruct((s, n_z), BF16),
                   jax.ShapeDtypeStruct((8, d), F32)],
        compiler_params=_params(("arbitrary",)),
    )(dxn, du, dcq, dckv, dkr, x, gn, sc, w_in_t)


def _window_sum(a, w, rows, forward):
    s = a.shape[0]
    step = 1
    while step < w:
        if forward:
            shifted = jnp.where(rows < s - step, pltpu.roll(a, s - step, 0), 0.0)
        else:
            shifted = jnp.where(rows >= step, pltpu.roll(a, step, 0), 0.0)
        a = a + shifted
        step *= 2
    return a


def pool_fwd(u, pool_w, pool_scale):
    s = u.shape[0]

    def body(u_ref, w_ref, sc_ref, y_ref, diff_ref):
        rows = lax.broadcasted_iota(jnp.int32, (s, POOL_GC), 0)
        for g, w in enumerate(POOL_WINDOWS):
            cols = slice(g * POOL_GC, (g + 1) * POOL_GC)
            ug = u_ref[:, cols]
            cnt = jnp.minimum(rows + 1, w).astype(F32)
            diff = (_window_sum(ug, w, rows, False) / cnt - ug).astype(BF16)
            diff_ref[:, cols] = diff
            y_ref[:, cols] = _dot(diff, w_ref[g].astype(BF16)) * sc_ref[:, cols]

    return pl.pallas_call(
        body, name="pool_fwd",
        out_shape=[jax.ShapeDtypeStruct(u.shape, F32), jax.ShapeDtypeStruct(u.shape, BF16)],
        compiler_params=_params(),
    )(u, pool_w, pool_scale)


def pool_bwd(dy, diff, pool_w, pool_scale):
    s = dy.shape[0]

    def body(dy_ref, diff_ref, w_ref, sc_ref, du_ref, dw_ref, dsc_ref):
        rows = lax.broadcasted_iota(jnp.int32, (s, POOL_GC), 0)
        for g, w in enumerate(POOL_WINDOWS):
            cols = slice(g * POOL_GC, (g + 1) * POOL_GC)
            dyg = dy_ref[:, cols]
            diff = diff_ref[:, cols]
            wb = w_ref[g].astype(BF16)
            dsc_ref[:, cols] = _sum0(dyg * _dot(diff, wb))
            dys = (dyg * sc_ref[:, cols]).astype(BF16)
            dw_ref[g] = _dot_tn(diff, dys)
            ddiff = _dot_nt(dys, wb)
            cnt = jnp.minimum(rows + 1, w).astype(F32)
            du_ref[:, cols] = _window_sum(ddiff / cnt, w, rows, True) - ddiff

    return pl.pallas_call(
        body, name="pool_bwd",
        out_shape=[jax.ShapeDtypeStruct(dy.shape, F32), jax.ShapeDtypeStruct(pool_w.shape, F32),
                   jax.ShapeDtypeStruct(pool_scale.shape, F32)],
        compiler_params=_params(),
    )(dy, diff, pool_w, pool_scale)


def mla_qkv_fwd(cq, ckv, kr, qan, kvan, wq, wkv, cos, sin, rot):
    s = cq.shape[0]
    tm = _row_tile(s)

    def body(cq_ref, ckv_ref, kr_ref, qan_ref, kvan_ref, wq_ref, wkv_ref, cos_ref, sin_ref, rot_ref,
             q_ref, k_ref, v_ref, ql_ref, kvl_ref):
        cos_t = cos_ref[...]
        sin_t = sin_ref[...]
        perm = rot_ref[...]

        def rope(t):
            return t * cos_t + _dot_exact(t, perm) * sin_t

        qhat, _ = _rms(cq_ref[...])
        ql = (qhat * qan_ref[...]).astype(BF16)
        ql_ref[...] = ql
        khat, _ = _rms(ckv_ref[...])
        kvl = (khat * kvan_ref[...]).astype(BF16)
        kvl_ref[...] = kvl
        krr = rope(kr_ref[...]).astype(BF16)
        for h in range(N_HEADS):
            q = _dot_nt(ql, wq_ref[h])
            q_ref[h, :, 0:QK_NOPE] = q[:, 0:QK_NOPE].astype(BF16)
            q_ref[h, :, QK_NOPE:] = rope(q[:, QK_NOPE:]).astype(BF16)
            kv = _dot(kvl, wkv_ref[h])
            k_ref[h, :, 0:QK_NOPE] = kv[:, 0:QK_NOPE].astype(BF16)
            k_ref[h, :, QK_NOPE:] = krr
            v_ref[h] = kv[:, QK_NOPE:].astype(BF16)

    row = lambda w: pl.BlockSpec((tm, w), lambda i: (i, 0))
    hrow = lambda w: pl.BlockSpec((N_HEADS, tm, w), lambda i: (0, i, 0))
    qk = QK_NOPE + QK_ROPE
    return pl.pallas_call(
        body, name="mla_qkv_fwd",
        grid=(s // tm,),
        in_specs=[row(Q_LORA), row(KV_LORA), row(QK_ROPE), _full(qan.shape), _full(kvan.shape),
                  _full(wq.shape), _full(wkv.shape), row(QK_ROPE), row(QK_ROPE), _full(rot.shape)],
        out_specs=[hrow(qk), hrow(qk), hrow(V_HEAD), row(Q_LORA), row(KV_LORA)],
        out_shape=[jax.ShapeDtypeStruct((N_HEADS, s, qk), BF16), jax.ShapeDtypeStruct((N_HEADS, s, qk), BF16),
                   jax.ShapeDtypeStruct((N_HEADS, s, V_HEAD), BF16), jax.ShapeDtypeStruct((s, Q_LORA), BF16),
                   jax.ShapeDtypeStruct((s, KV_LORA), BF16)],
        compiler_params=_params(("arbitrary",)),
    )(cq, ckv, kr, qan, kvan, wq, wkv, cos, sin, rot)


def _attn_probs(q_ref, k_ref, qi, tq):
    n = (qi + 1) * tq
    rows = slice(qi * tq, n)
    sc = _dot_nt(q_ref[rows, :], k_ref[0:n, :]) * SOFTMAX_SCALE
    qpos = qi * tq + lax.broadcasted_iota(jnp.int32, (tq, n), 0)
    kpos = lax.broadcasted_iota(jnp.int32, (tq, n), 1)
    sc = jnp.where(qpos >= kpos, sc, -1e30)
    e = jnp.exp(sc - jnp.max(sc, axis=-1, keepdims=True))
    return e / jnp.sum(e, axis=-1, keepdims=True)


def attn_fwd(q, k, v):
    nh, s, qk = q.shape
    tq = min(s, ATT_TILE)

    def body(q_ref, k_ref, v_ref, o_ref):
        for qi in range(s // tq):
            n = (qi + 1) * tq
            p = _attn_probs(q_ref, k_ref, qi, tq).astype(BF16)
            o_ref[qi * tq:n, :] = _dot(p, v_ref[0:n, :])

    head = lambda w: pl.BlockSpec((None, s, w), lambda h: (h, 0, 0))
    return pl.pallas_call(
        body, name="attn_fwd",
        grid=(nh,),
        in_specs=[head(qk), head(qk), head(V_HEAD)],
        out_specs=pl.BlockSpec((s, V_HEAD), lambda h: (0, h)),
        out_shape=jax.ShapeDtypeStruct((s, nh * V_HEAD), F32),
        compiler_params=_params(("arbitrary",)),
    )(q, k, v)


def attn_bwd(q, k, v, do):
    nh, s, qk = q.shape
    tq = min(s, ATT_TILE)

    def body(q_ref, k_ref, v_ref, do_ref, dq_ref, dk_ref, dv_ref):
        dk_ref[...] = jnp.zeros_like(dk_ref)
        dv_ref[...] = jnp.zeros_like(dv_ref)
        for qi in range(s // tq):
            n = (qi + 1) * tq
            rows = slice(qi * tq, n)
            p = _attn_probs(q_ref, k_ref, qi, tq)
            dob = do_ref[rows, :].astype(BF16)
            dp = _dot_nt(dob, v_ref[0:n, :])
            ds = (p * (dp - jnp.sum(p * dp, axis=-1, keepdims=True)) * SOFTMAX_SCALE).astype(BF16)
            dq_ref[rows, :] = _dot(ds, k_ref[0:n, :])
            dk_ref[0:n, :] += _dot_tn(ds, q_ref[rows, :])
            dv_ref[0:n, :] += _dot_tn(p.astype(BF16), dob)

    head = lambda w: pl.BlockSpec((None, s, w), lambda h: (h, 0, 0))
    return pl.pallas_call(
        body, name="attn_bwd",
        grid=(nh,),
        in_specs=[head(qk), head(qk), head(V_HEAD), pl.BlockSpec((s, V_HEAD), lambda h: (0, h))],
        out_specs=[head(qk), head(qk), head(V_HEAD)],
        out_shape=[jax.ShapeDtypeStruct((nh, s, qk), F32), jax.ShapeDtypeStruct((nh, s, qk), F32),
                   jax.ShapeDtypeStruct((nh, s, V_HEAD), F32)],
        compiler_params=_params(("arbitrary",)),
    )(q, k, v, do)


def mla_qkv_bwd(dq, dk, dv, cq, ckv, qan, kvan, wq, wkv, cos, sin, rot_t):
    s = cq.shape[0]
    tm = _row_tile(s)

    def body(dq_ref, dk_ref, dv_ref, cq_ref, ckv_ref, qan_ref, kvan_ref,
             wq_ref, wkv_ref, cos_ref, sin_ref, rot_ref,
             dcq_ref, dckv_ref, dkro_ref, gq_ref, gkv_ref, dqan_ref, dkvan_ref):
        i = pl.program_id(0)

        @pl.when(i == 0)
        def _():
            dqan_ref[...] = jnp.zeros_like(dqan_ref)
            dkvan_ref[...] = jnp.zeros_like(dkvan_ref)

        cos_t = cos_ref[...]
        sin_t = sin_ref[...]
        perm_t = rot_ref[...]

        def unrope(t):
            return t * cos_t + _dot_exact(t * sin_t, perm_t)

        acc_q = jnp.zeros((tm, Q_LORA), F32)
        acc_kv = jnp.zeros((tm, KV_LORA), F32)
        dkr_sum = jnp.zeros((tm, QK_ROPE), F32)
        for h in range(N_HEADS):
            dq_h = dq_ref[h]
            a = dq_h[:, 0:QK_NOPE].astype(BF16)
            b = unrope(dq_h[:, QK_NOPE:]).astype(BF16)
            gq_ref[h, :, 0:QK_NOPE] = a
            gq_ref[h, :, QK_NOPE:] = b
            wq_h = wq_ref[h]
            acc_q += _dot(a, wq_h[0:QK_NOPE, :]) + _dot(b, wq_h[QK_NOPE:, :])
            dk_h = dk_ref[h]
            dk = dk_h[:, 0:QK_NOPE].astype(BF16)
            dvv = dv_ref[h].astype(BF16)
            gkv_ref[h, :, 0:QK_NOPE] = dk
            gkv_ref[h, :, QK_NOPE:] = dvv
            wkv_h = wkv_ref[h]
            acc_kv += _dot_nt(dk, wkv_h[:, 0:QK_NOPE]) + _dot_nt(dvv, wkv_h[:, QK_NOPE:])
            dkr_sum += dk_h[:, QK_NOPE:]
        dkro_ref[...] = unrope(dkr_sum)
        dcq, dqan = _rms_bwd(acc_q, cq_ref[...], qan_ref[...])
        dcq_ref[...] = dcq
        dqan_ref[...] += dqan
        dckv, dkvan = _rms_bwd(acc_kv, ckv_ref[...], kvan_ref[...])
        dckv_ref[...] = dckv
        dkvan_ref[...] += dkvan

    row = lambda w: pl.BlockSpec((tm, w), lambda i: (i, 0))
    hrow = lambda w: pl.BlockSpec((N_HEADS, tm, w), lambda i: (0, i, 0))
    return pl.pallas_call(
        body, name="mla_qkv_bwd",
        grid=(s // tm,),
        in_specs=[hrow(QK_NOPE + QK_ROPE), hrow(QK_NOPE + QK_ROPE), hrow(V_HEAD),
                  row(Q_LORA), row(KV_LORA), _full(qan.shape), _full(kvan.shape),
                  _full(wq.shape), _full(wkv.shape), row(QK_ROPE), row(QK_ROPE), _full(rot_t.shape)],
        out_specs=[row(Q_LORA), row(KV_LORA), row(QK_ROPE), hrow(QK_NOPE + QK_ROPE), hrow(QK_NOPE + V_HEAD),
                   _full(qan.shape), _full(kvan.shape)],
        out_shape=[jax.ShapeDtypeStruct((s, Q_LORA), F32), jax.ShapeDtypeStruct((s, KV_LORA), F32),
                   jax.ShapeDtypeStruct((s, QK_ROPE), F32),
                   jax.ShapeDtypeStruct((N_HEADS, s, QK_NOPE + QK_ROPE), BF16),
                   jax.ShapeDtypeStruct((N_HEADS, s, QK_NOPE + V_HEAD), BF16),
                   jax.ShapeDtypeStruct(qan.shape, F32), jax.ShapeDtypeStruct(kvan.shape, F32)],
        compiler_params=_params(("arbitrary",)),
    )(dq, dk, dv, cq, ckv, qan, kvan, wq, wkv, cos, sin, rot_t)


def out_proj_fwd(yp, om, w_out, x, gt):
    s, d = x.shape
    n_sh, rs, _ = w_out.shape
    tm = _row_tile(s)
    per = POOL_WIDTH // rs

    def body(yp_ref, om_ref, w_ref, x_ref, gt_ref, xo_ref, ycat_ref, y_ref):
        y = jnp.zeros((tm, d), F32)
        for j in range(n_sh):
            src = yp_ref if j < per else om_ref
            part = src[:, (j % per) * rs:(j % per + 1) * rs].astype(BF16)
            ycat_ref[j] = part
            y += _dot(part, w_ref[j])
        y_ref[...] = y.astype(BF16)
        xo_ref[...] = x_ref[...] + gt_ref[...] * y

    row = lambda w: pl.BlockSpec((tm, w), lambda i: (i, 0))
    return pl.pallas_call(
        body, name="out_proj_fwd",
        grid=(s // tm,),
        in_specs=[row(POOL_WIDTH), row(POOL_WIDTH), _full(w_out.shape), row(d), pl.BlockSpec((1, d), lambda i: (0, 0))],
        out_specs=[row(d), pl.BlockSpec((n_sh, tm, rs), lambda i: (0, i, 0)), row(d)],
        out_shape=[jax.ShapeDtypeStruct((s, d), F32), jax.ShapeDtypeStruct((n_sh, s, rs), BF16),
                   jax.ShapeDtypeStruct((s, d), BF16)],
        compiler_params=_params(("arbitrary",)),
    )(yp, om, w_out, x, gt)


def out_proj_bwd(dxn, y, gt, w_out):
    s, d = dxn.shape
    n_sh, rs, _ = w_out.shape
    tm = _row_tile(s)
    per = POOL_WIDTH // rs

    def body(dxn_ref, y_ref, gt_ref, w_ref, dy_ref, dyp_ref, dom_ref, dgt_ref):
        i = pl.program_id(0)

        @pl.when(i == 0)
        def _():
            dgt_ref[...] = jnp.zeros_like(dgt_ref)

        dxn_t = dxn_ref[...]
        dy = (gt_ref[...] * dxn_t).astype(BF16)
        dy_ref[...] = dy
        dgt_ref[...] += _sum0(dxn_t * y_ref[...].astype(F32))
        for j in range(n_sh):
            dst = dyp_ref if j < per else dom_ref
            dst[:, (j % per) * rs:(j % per + 1) * rs] = _dot_nt(dy, w_ref[j])

    row = lambda w: pl.BlockSpec((tm, w), lambda i: (i, 0))
    vec = pl.BlockSpec((1, d), lambda i: (0, 0))
    return pl.pallas_call(
        body, name="out_proj_bwd",
        grid=(s // tm,),
        in_specs=[row(d), row(d), vec, _full(w_out.shape)],
        out_specs=[row(d), row(POOL_WIDTH), row(POOL_WIDTH), vec],
        out_shape=[jax.ShapeDtypeStruct((s, d), BF16), jax.ShapeDtypeStruct((s, POOL_WIDTH), F32),
                   jax.ShapeDtypeStruct((s, POOL_WIDTH), F32), jax.ShapeDtypeStruct((1, d), F32)],
        compiler_params=_params(("arbitrary",)),
    )(dxn, y, gt, w_out)


def final_loss(x, gn, tgt):
    s, d = x.shape
    tm = _row_tile(s)

    def body(x_ref, gn_ref, t_ref, loss_ref, dx_ref, dgn_ref):
        i = pl.program_id(0)

        @pl.when(i == 0)
        def _():
            loss_ref[...] = jnp.zeros_like(loss_ref)
            dgn_ref[...] = jnp.zeros_like(dgn_ref)

        xt = x_ref[...]
        g = gn_ref[...]
        xhat, _ = _rms(xt)
        err = xhat * g - t_ref[...]
        per_tok = jnp.mean(err * err, axis=-1, keepdims=True)
        loss_ref[...] += jnp.broadcast_to(0.5 * _sum0(per_tok), loss_ref.shape)
        dx, dgn = _rms_bwd(err * (1.0 / d), xt, g)
        dx_ref[...] = dx
        dgn_ref[...] += dgn

    row = pl.BlockSpec((tm, d), lambda i: (i, 0))
    vec = pl.BlockSpec((1, d), lambda i: (0, 0))
    return pl.pallas_call(
        body, name="final_loss",
        grid=(s // tm,),
        in_specs=[row, vec, row],
        out_specs=[pl.BlockSpec((1, LANES), lambda i: (0, 0)), row, vec],
        out_shape=[jax.ShapeDtypeStruct((1, LANES), F32), jax.ShapeDtypeStruct((s, d), F32),
                   jax.ShapeDtypeStruct((1, d), F32)],
        compiler_params=_params(("arbitrary",)),
    )(x, gn, tgt)


def _col_tile(cols):
    return 768 if cols % 768 == 0 else cols


def ada_fwd(c16, ada_w, ada_b_loc):
    n_layers, d, cols = ada_w.shape
    tn = _col_tile(cols)

    def body(c_ref, w_ref, b_ref, o_ref):
        cv = c_ref[...]
        ca = (cv * jax.nn.sigmoid(cv)).astype(BF16)
        o_ref[...] = _dot(ca, w_ref[...].astype(BF16)) + b_ref[...]

    return pl.pallas_call(
        body, name="ada_fwd",
        grid=(n_layers, cols // tn),
        in_specs=[pl.BlockSpec((16, d), lambda l, j: (0, 0)), pl.BlockSpec((None, d, tn), lambda l, j: (l, 0, j)),
                  pl.BlockSpec((None, 1, tn), lambda l, j: (l, 0, j))],
        out_specs=pl.BlockSpec((None, 16, tn), lambda l, j: (l, 0, j)),
        out_shape=jax.ShapeDtypeStruct((n_layers, 16, cols), F32),
        compiler_params=_params(("arbitrary", "arbitrary")),
    )(c16, ada_w, ada_b_loc)


def ada_bwd(c16, dmod16):
    n_layers, _, cols = dmod16.shape
    d = c16.shape[1]
    tn = _col_tile(cols)

    def body(c_ref, g_ref, o_ref):
        cv = c_ref[...]
        ca = (cv * jax.nn.sigmoid(cv)).astype(BF16)
        o_ref[...] = _dot_tn(ca, g_ref[...].astype(BF16))

    return pl.pallas_call(
        body, name="ada_bwd",
        grid=(n_layers, cols // tn),
        in_specs=[pl.BlockSpec((16, d), lambda l, j: (0, 0)), pl.BlockSpec((None, 16, tn), lambda l, j: (l, 0, j))],
        out_specs=pl.BlockSpec((None, d, tn), lambda l, j: (l, 0, j)),
        out_shape=jax.ShapeDtypeStruct((n_layers, d, cols), F32),
        compiler_params=_params(("arbitrary", "arbitrary")),
    )(c16, dmod16)


def _as_rows(a):
    if a.ndim == 1:
        return a.reshape(1, a.shape[0])
    return a.reshape(-1, a.shape[-1])


def _rows_tile(r, c, itemsize=4, budget=2 * 1024 * 1024):
    if r * c * itemsize <= budget:
        return r
    best = None
    t = BF16_ROWS
    while t < r:
        if r % t == 0 and t * c * itemsize <= budget:
            best = t
        t += BF16_ROWS
    return best if best is not None else r


def cast_place(w, chip, layers, after):
    _, r, c = w.shape
    n_sel = len(layers)
    tr = _rows_tile(r, c, budget=2 * 1024 * 1024 // n_sel)

    def body(chip_ref, *refs):
        for j in range(n_sel):
            refs[n_sel + 1 + j][...] = refs[j][...].astype(BF16)

    layer_spec = lambda l: pl.BlockSpec((None, tr, c), lambda i, ch: (l, i, 0))
    return list(pl.pallas_call(
        body, name="cast_place",
        grid_spec=pltpu.PrefetchScalarGridSpec(
            num_scalar_prefetch=1, grid=(r // tr,),
            in_specs=[layer_spec(l) for l in layers] + [pl.BlockSpec(memory_space=pl.ANY)],
            out_specs=[pl.BlockSpec((None, tr, c), lambda i, ch: (ch[0], i, 0))] * n_sel),
        out_shape=[jax.ShapeDtypeStruct((N_CHIPS, r, c), BF16)] * n_sel,
        compiler_params=_params(("arbitrary",)),
    )(chip, *([w] * n_sel), after))


def adamw(w, g, m, v):
    shape = w.shape
    w2, g2, m2, v2 = (_as_rows(t) for t in (w, g, m, v))
    r, c = w2.shape
    tr = _rows_tile(r, c, budget=1024 * 1024)
    c1 = 1.0 - ADAM_B1 ** ADAM_STEP
    c2 = 1.0 - ADAM_B2 ** ADAM_STEP

    def body(w_ref, g_ref, m_ref, v_ref, d_ref, mo_ref, vo_ref):
        gv = g_ref[...]
        mn = ADAM_B1 * m_ref[...] + (1.0 - ADAM_B1) * gv
        vn = ADAM_B2 * v_ref[...] + (1.0 - ADAM_B2) * (gv * gv)
        mo_ref[...] = mn
        vo_ref[...] = vn
        d_ref[...] = -ADAM_LR * ((mn / c1) / (jnp.sqrt(vn / c2) + ADAM_EPS) + ADAM_WD * w_ref[...])

    spec = pl.BlockSpec((tr, c), lambda i: (i, 0))
    outs = pl.pallas_call(
        body, name="adamw", grid=(r // tr,), in_specs=[spec] * 4, out_specs=[spec] * 3,
        out_shape=[jax.ShapeDtypeStruct((r, c), F32)] * 3, compiler_params=_params(("arbitrary",)),
    )(w2, g2, m2, v2)
    return tuple(o.reshape(shape) for o in outs)


def sum_devices(a):
    n, r, c = a.shape
    tr = _rows_tile(r, c, budget=512 * 1024)

    def body(a_ref, o_ref):
        acc = a_ref[0]
        for j in range(1, n):
            acc = acc + a_ref[j]
        o_ref[...] = acc

    return pl.pallas_call(
        body, name="sum_devices", grid=(r // tr,),
        in_specs=[pl.BlockSpec((n, tr, c), lambda i: (0, i, 0))], out_specs=pl.BlockSpec((tr, c), lambda i: (i, 0)),
        out_shape=jax.ShapeDtypeStruct((r, c), F32), compiler_params=_params(("arbitrary",)),
    )(a)


def _split_axis(r, c):
    if (r // 2) % BF16_ROWS == 0 and r % 2 == 0:
        return 0
    assert c % (2 * LANES) == 0, (r, c)
    return 1


def _half_shape(r, c):
    return (r // 2, c) if _split_axis(r, c) == 0 else (r, c // 2)


def _half_at(ref, lead, which):
    r, c = ref.shape[-2:]
    if _split_axis(r, c) == 0:
        return ref.at[(*lead, pl.ds(which * (r // 2), r // 2), slice(None))]
    return ref.at[(*lead, slice(None), pl.ds(which * (c // 2), c // 2))]


def _half_spec(r, c, lead_block, imap):
    hr, hc = _half_shape(r, c)
    if _split_axis(r, c) == 0:
        return pl.BlockSpec((*lead_block, hr, hc), lambda *a: (*imap(*a)[0], imap(*a)[1], 0))
    return pl.BlockSpec((*lead_block, hr, hc), lambda *a: (*imap(*a)[0], 0, imap(*a)[1]))


def pair_add(gs, ras, sel):
    n = len(gs)
    n_sl = gs[0].shape[0]
    halves = [_half_shape(*g.shape[1:]) for g in gs]

    def body(s_ref, *refs):
        g_refs, ra_refs, pb_refs, own_refs = (refs[i * n:(i + 1) * n] for i in range(4))
        k = pl.program_id(0)
        for t in range(n):
            p = g_refs[t][...] + ra_refs[t][...]
            pb_refs[t][...] = p.astype(BF16)

            @pl.when(k == s_ref[1])
            def _(p=p, own=own_refs[t]):
                own[...] = p

    slot = lambda hs: pl.BlockSpec((None,) + hs, lambda k, sr: (k, 0, 0))
    outs = pl.pallas_call(
        body, name="pair_add",
        grid_spec=pltpu.PrefetchScalarGridSpec(
            num_scalar_prefetch=1, grid=(n_sl,),
            in_specs=[_half_spec(*g.shape[1:], (None,), lambda k, sr: ((k,), sr[0])) for g in gs]
            + [slot(hs) for hs in halves],
            out_specs=[slot(hs) for hs in halves] + [pl.BlockSpec(hs, lambda k, sr: (0, 0)) for hs in halves]),
        out_shape=[jax.ShapeDtypeStruct((n_sl,) + hs, BF16) for hs in halves]
        + [jax.ShapeDtypeStruct(hs, F32) for hs in halves],
        compiler_params=_params(("arbitrary",)),
    )(sel, *gs, *ras)
    return list(outs[:n]), list(outs[n:])


def chip_sum(owns, rbs, sel, shapes, accs):
    n = len(owns)
    fresh = accs[0] is None

    def body(s_ref, *refs):
        own_refs, rb_refs, o_refs = refs[:n], refs[n:2 * n], refs[-n:]
        for t in range(n):
            acc_v = own_refs[t][...]
            for j in range(N_CHIPS - 1):
                acc_v = acc_v + rb_refs[t][j].astype(F32)
            o_refs[t][...] = acc_v

    in_specs = ([pl.BlockSpec(o.shape, lambda i, sr: (0, 0)) for o in owns]
                + [pl.BlockSpec(rb.shape, lambda i, sr: (0, 0, 0)) for rb in rbs])
    args = [sel, *owns, *rbs]
    aliases = {}
    if not fresh:
        in_specs += [pl.BlockSpec(memory_space=pl.ANY)] * n
        args += list(accs)
        aliases = {1 + 2 * n + t: t for t in range(n)}
    return list(pl.pallas_call(
        body, name="chip_sum",
        grid_spec=pltpu.PrefetchScalarGridSpec(
            num_scalar_prefetch=1, grid=(1,), in_specs=in_specs,
            out_specs=[_half_spec(*shp[1:], (None,), lambda i, sr: ((sr[2],), sr[0])) for shp in shapes]),
        out_shape=[jax.ShapeDtypeStruct(shp, F32) for shp in shapes],
        input_output_aliases=aliases,
        compiler_params=_params(("arbitrary",)),
    )(*args))


def _me():
    return lax.axis_index("x"), lax.axis_index("y"), lax.axis_index("c")


def _flip(v, bit):
    return 1 - v if bit else v


def exchange8(xs, bcast):
    blk = xs.shape if bcast else xs.shape[1:]

    def body(x_ref, o_ref, send_sems, recv_sems, loc_sem):
        mx, my, mc = _me()
        me = 4 * mx + 2 * my + mc
        src = (lambda j: x_ref) if bcast else (lambda j: x_ref.at[j])
        loc = pltpu.make_async_copy(src(me), o_ref.at[me], loc_sem)
        loc.start()
        copies = []
        for o in range(1, N_DEV):
            px, py, pc = _flip(mx, o & 4), _flip(my, o & 2), _flip(mc, o & 1)
            cp = pltpu.make_async_remote_copy(
                src_ref=src(4 * px + 2 * py + pc), dst_ref=o_ref.at[me],
                send_sem=send_sems.at[o - 1], recv_sem=recv_sems.at[o - 1],
                device_id=(px, py, pc), device_id_type=MESH)
            cp.start()
            copies.append(cp)
        for cp in copies:
            cp.wait()
        loc.wait()

    return pl.pallas_call(
        body, name="exchange8_gather" if bcast else "exchange8_a2a",
        in_specs=[pl.BlockSpec(memory_space=pltpu.VMEM)], out_specs=pl.BlockSpec(memory_space=pltpu.VMEM),
        out_shape=jax.ShapeDtypeStruct((N_DEV,) + tuple(blk), xs.dtype),
        scratch_shapes=[pltpu.SemaphoreType.DMA((N_DEV - 1,)), pltpu.SemaphoreType.DMA((N_DEV - 1,)), pltpu.SemaphoreType.DMA],
        compiler_params=_params(),
    )(xs)


HBM = pl.BlockSpec(memory_space=pltpu.HBM)
SEM = pl.BlockSpec(memory_space=pltpu.SEMAPHORE)
EFFECT = pltpu.SideEffectType.DATAFLOW_SIDE_EFFECTING


def _hbm(a):
    return pltpu.with_memory_space_constraint(a, pltpu.HBM)


def _ici_copy(land, o, send_sem, recv_sem, sending):
    mx, my, mc = _me()
    px, py = _flip(mx, o & 2), _flip(my, o & 1)
    mine = _half_at(land, (2 * mx + my,), mc)
    return pltpu.make_async_remote_copy(
        src_ref=mine, dst_ref=mine if sending else _half_at(land, (2 * px + py,), mc),
        send_sem=send_sem, recv_sem=recv_sem, device_id=(px, py, mc), device_id_type=MESH)


N_PEERS = N_CHIPS - 1
DMA_SEM = pltpu.SemaphoreType.DMA(())


def gather_start(lands, groups, after, tag):
    n_layers, n = len(lands), len(lands[0])
    flat = [a for layer in lands for a in layer]
    n_in = n * n_layers
    n_grp = len(groups)
    n_sem = 2 * n_layers * n_grp * N_PEERS
    first = lambda l, g, recv: ((l * n_grp + g) * 2 + recv) * N_PEERS

    def body(*refs):
        land = refs[:n_in]
        sems = refs[n_in + 1:n_in + 1 + n_sem]
        token = refs[-1]
        for l in range(n_layers):
            for g, members in enumerate(groups):
                for t in members:
                    for o in range(1, N_CHIPS):
                        _ici_copy(land[l * n + t], o, sems[first(l, g, 0) + o - 1], sems[first(l, g, 1) + o - 1],
                                  True).start()
        token[...] = jnp.zeros_like(token)

    outs = pl.pallas_call(
        body, name=f"gather_start_{tag}",
        in_specs=[HBM] * n_in + [pl.BlockSpec(memory_space=pl.ANY)],
        out_specs=[SEM] * n_sem + [HBM] * n_in + [pl.BlockSpec(memory_space=pltpu.VMEM)],
        out_shape=[DMA_SEM] * n_sem + [pltpu.HBM(a.shape, a.dtype) for a in flat]
        + [jax.ShapeDtypeStruct((8, LANES), F32)],
        input_output_aliases={i: i + n_sem for i in range(n_in)},
        compiler_params=pltpu.CompilerParams(has_side_effects=EFFECT),
    )(*[_hbm(a) for a in flat], after)
    sems = [[(list(outs[first(l, g, 0):first(l, g, 0) + N_PEERS]), list(outs[first(l, g, 1):first(l, g, 1) + N_PEERS]))
             for g in range(n_grp)] for l in range(n_layers)]
    lands_thru = [list(outs[n_sem + l * n:n_sem + (l + 1) * n]) for l in range(n_layers)]
    return sems, lands_thru, outs[-1]


def gather_wait(tag, sems, lands, after):
    n = len(lands)
    send_sems, recv_sems = sems

    def body(*refs):
        land = refs[:n]
        send_r = refs[n:n + N_PEERS]
        recv_r = refs[n + N_PEERS:n + 2 * N_PEERS]
        for t in range(n):
            for o in range(1, N_CHIPS):
                _ici_copy(land[t], o, send_r[o - 1], recv_r[o - 1], True).wait_send()
                _ici_copy(land[t], o, send_r[o - 1], recv_r[o - 1], False).wait_recv()

    return list(pl.pallas_call(
        body, name=f"gather_wait_{tag}",
        in_specs=[HBM] * n + [SEM] * (2 * N_PEERS) + [pl.BlockSpec(memory_space=pl.ANY)],
        out_specs=[HBM] * n,
        out_shape=[pltpu.HBM(a.shape, a.dtype) for a in lands],
        input_output_aliases={i: i for i in range(n)},
        compiler_params=pltpu.CompilerParams(has_side_effects=EFFECT),
    )(*lands, *send_sems, *recv_sems, after))


def gather_forward(lands):
    n = len(lands)

    def body(*refs):
        dst = refs[n:2 * n]
        send_sems, recv_sems = refs[2 * n:]
        mx, my, mc = _me()
        fwds = []
        for t in range(n):
            for o in range(1, N_CHIPS):
                slot = 2 * _flip(mx, o & 2) + _flip(my, o & 1)
                mine = _half_at(dst[t], (slot,), mc)
                theirs = _half_at(dst[t], (slot,), 1 - mc)
                cp = pltpu.make_async_remote_copy(
                    src_ref=mine, dst_ref=mine, send_sem=send_sems.at[t, o - 1], recv_sem=recv_sems.at[t, o - 1],
                    device_id=(mx, my, 1 - mc), device_id_type=MESH)
                cp.start()
                fwds.append((cp, pltpu.make_async_remote_copy(
                    src_ref=theirs, dst_ref=theirs, send_sem=send_sems.at[t, o - 1], recv_sem=recv_sems.at[t, o - 1],
                    device_id=(mx, my, 1 - mc), device_id_type=MESH)))
        for cp, arrival in fwds:
            cp.wait_send()
            arrival.wait_recv()

    any_spec = pl.BlockSpec(memory_space=pl.ANY)
    return list(pl.pallas_call(
        body, name="gather_forward",
        in_specs=[any_spec] * n, out_specs=[any_spec] * n,
        out_shape=[jax.ShapeDtypeStruct(a.shape, a.dtype) for a in lands],
        input_output_aliases={t: t for t in range(n)},
        scratch_shapes=[pltpu.SemaphoreType.DMA((n, N_CHIPS - 1)), pltpu.SemaphoreType.DMA((n, N_CHIPS - 1))],
        compiler_params=_params(),
    )(*lands))


def _scatter_copy(src, land, o, send_sem, recv_sem):
    mx, my, mc = _me()
    px, py = _flip(mx, o & 2), _flip(my, o & 1)
    return pltpu.make_async_remote_copy(
        src_ref=src.at[2 * px + py], dst_ref=land.at[o - 1],
        send_sem=send_sem, recv_sem=recv_sem, device_id=(px, py, mc), device_id_type=MESH)


def scatter_start(pbs, tag, after):
    n = len(pbs)
    lands = [lax.empty((N_CHIPS - 1,) + p.shape[1:], p.dtype) for p in pbs]

    def body(*refs):
        src = refs[:n]
        land = refs[n:2 * n]
        send_sems = refs[2 * n + 1:2 * n + 1 + N_PEERS]
        recv_sems = refs[2 * n + 1 + N_PEERS:2 * n + 1 + 2 * N_PEERS]
        token = refs[-1]
        for t in range(n):
            for o in range(1, N_CHIPS):
                _scatter_copy(src[t], land[t], o, send_sems[o - 1], recv_sems[o - 1]).start()
        token[...] = jnp.zeros_like(token)

    n_sem = 2 * N_PEERS
    arrs = list(pbs) + lands
    outs = pl.pallas_call(
        body, name=f"scatter_start_{tag}",
        in_specs=[HBM] * (2 * n) + [pl.BlockSpec(memory_space=pl.ANY)],
        out_specs=[SEM] * n_sem + [HBM] * (2 * n) + [pl.BlockSpec(memory_space=pltpu.VMEM)],
        out_shape=[DMA_SEM] * n_sem + [pltpu.HBM(a.shape, a.dtype) for a in arrs]
        + [jax.ShapeDtypeStruct((8, LANES), F32)],
        input_output_aliases={i: i + n_sem for i in range(2 * n)},
        compiler_params=pltpu.CompilerParams(has_side_effects=EFFECT),
    )(*[_hbm(a) for a in arrs], after)
    return (list(outs[:N_PEERS]), list(outs[N_PEERS:n_sem]), list(outs[n_sem:n_sem + n]),
            list(outs[n_sem + n:n_sem + 2 * n]), outs[-1])


def scatter_wait(tag, send_sems, recv_sems, pbs, lands, after):
    n = len(pbs)

    def body(*refs):
        src = refs[:n]
        land = refs[n:2 * n]
        send_r = refs[2 * n:2 * n + N_PEERS]
        recv_r = refs[2 * n + N_PEERS:2 * n + 2 * N_PEERS]
        for t in range(n):
            for o in range(1, N_CHIPS):
                cp = _scatter_copy(src[t], land[t], o, send_r[o - 1], recv_r[o - 1])
                cp.wait_send()
                cp.wait_recv()

    arrs = list(pbs) + list(lands)
    outs = pl.pallas_call(
        body, name=f"scatter_wait_{tag}",
        in_specs=[HBM] * (2 * n) + [SEM] * (2 * N_PEERS) + [pl.BlockSpec(memory_space=pl.ANY)],
        out_specs=[HBM] * (2 * n),
        out_shape=[pltpu.HBM(a.shape, a.dtype) for a in arrs],
        input_output_aliases={i: i for i in range(2 * n)},
        compiler_params=pltpu.CompilerParams(has_side_effects=EFFECT),
    )(*arrs, *send_sems, *recv_sems, after)
    return list(outs[n:])


def _pair_copy(src, land, send_sem, recv_sem):
    mx, my, mc = _me()
    return pltpu.make_async_remote_copy(
        src_ref=_half_at(src, (slice(None),), 1 - mc), dst_ref=land, send_sem=send_sem, recv_sem=recv_sem,
        device_id=(mx, my, 1 - mc), device_id_type=MESH)


def pair_start(gs, tag, after):
    n = len(gs)
    lands = [lax.empty((g.shape[0],) + _half_shape(*g.shape[1:]), g.dtype) for g in gs]

    def body(*refs):
        src = refs[:n]
        land = refs[n:2 * n]
        send_sem, recv_sem = refs[2 * n + 1], refs[2 * n + 2]
        token = refs[-1]
        for t in range(n):
            _pair_copy(src[t], land[t], send_sem, recv_sem).start()
        token[...] = jnp.zeros_like(token)

    arrs = list(gs) + lands
    outs = pl.pallas_call(
        body, name=f"pair_start_{tag}",
        in_specs=[HBM] * (2 * n) + [pl.BlockSpec(memory_space=pl.ANY)],
        out_specs=[SEM, SEM] + [HBM] * (2 * n) + [pl.BlockSpec(memory_space=pltpu.VMEM)],
        out_shape=[DMA_SEM, DMA_SEM] + [pltpu.HBM(a.shape, a.dtype) for a in arrs] + [jax.ShapeDtypeStruct((8, LANES), F32)],
        input_output_aliases={i: i + 2 for i in range(2 * n)},
        compiler_params=pltpu.CompilerParams(has_side_effects=EFFECT),
    )(*[_hbm(a) for a in arrs], after)
    return outs[0], outs[1], list(outs[2:2 + n]), list(outs[2 + n:2 + 2 * n]), outs[-1]


def pair_wait(tag, send_sem, recv_sem, gs, lands, after):
    n = len(gs)

    def body(*refs):
        src = refs[:n]
        land = refs[n:2 * n]
        send_r, recv_r = refs[2 * n], refs[2 * n + 1]
        for t in range(n):
            cp = _pair_copy(src[t], land[t], send_r, recv_r)
            cp.wait_send()
            cp.wait_recv()

    arrs = list(gs) + list(lands)
    outs = pl.pallas_call(
        body, name=f"pair_wait_{tag}",
        in_specs=[HBM] * (2 * n) + [SEM, SEM, pl.BlockSpec(memory_space=pl.ANY)],
        out_specs=[HBM] * (2 * n),
        out_shape=[pltpu.HBM(a.shape, a.dtype) for a in arrs],
        input_output_aliases={i: i for i in range(2 * n)},
        compiler_params=pltpu.CompilerParams(has_side_effects=EFFECT),
    )(*arrs, send_sem, recv_sem, after)
    return list(outs[:n]), list(outs[n:])


def _gather8_copy(x, land, o, send_sem, recv_sem, sending):
    mx, my, mc = _me()
    px, py, pc = _flip(mx, o & 4), _flip(my, o & 2), _flip(mc, o & 1)
    slot = 4 * mx + 2 * my + mc if sending else 4 * px + 2 * py + pc
    return pltpu.make_async_remote_copy(
        src_ref=x, dst_ref=land.at[slot], send_sem=send_sem, recv_sem=recv_sem,
        device_id=(px, py, pc), device_id_type=MESH)


def gather8_start(x, land, after):
    n_peer = N_DEV - 1

    def body(x_ref, land_ref, after_ref, *rest):
        send_sems, recv_sems = rest[:n_peer], rest[n_peer:2 * n_peer]
        token = rest[-1]
        for o in range(1, N_DEV):
            _gather8_copy(x_ref, land_ref, o, send_sems[o - 1], recv_sems[o - 1], True).start()
        token[...] = jnp.zeros_like(token)

    outs = pl.pallas_call(
        body, name="gather8_start",
        in_specs=[HBM, HBM, pl.BlockSpec(memory_space=pl.ANY)],
        out_specs=[SEM] * (2 * n_peer) + [HBM, HBM, pl.BlockSpec(memory_space=pltpu.VMEM)],
        out_shape=[DMA_SEM] * (2 * n_peer) + [pltpu.HBM(x.shape, x.dtype), pltpu.HBM(land.shape, land.dtype),
                                              jax.ShapeDtypeStruct((8, LANES), F32)],
        input_output_aliases={0: 2 * n_peer, 1: 2 * n_peer + 1},
        compiler_params=pltpu.CompilerParams(has_side_effects=EFFECT),
    )(_hbm(x), _hbm(land), after)
    return list(outs[:n_peer]), list(outs[n_peer:2 * n_peer]), outs[2 * n_peer], outs[2 * n_peer + 1], outs[-1]


def gather8_wait(send_sems, recv_sems, x, land, after):
    n_peer = N_DEV - 1

    def body(x_ref, land_ref, *rest):
        send_r, recv_r = rest[:n_peer], rest[n_peer:2 * n_peer]
        for o in range(1, N_DEV):
            _gather8_copy(x_ref, land_ref, o, send_r[o - 1], recv_r[o - 1], True).wait_send()
            _gather8_copy(x_ref, land_ref, o, send_r[o - 1], recv_r[o - 1], False).wait_recv()

    return pl.pallas_call(
        body, name="gather8_wait",
        in_specs=[HBM, HBM] + [SEM] * (2 * n_peer) + [pl.BlockSpec(memory_space=pl.ANY)],
        out_specs=[HBM, HBM],
        out_shape=[pltpu.HBM(x.shape, x.dtype), pltpu.HBM(land.shape, land.dtype)],
        input_output_aliases={0: 0, 1: 1},
        compiler_params=pltpu.CompilerParams(has_side_effects=EFFECT),
    )(x, land, *send_sems, *recv_sems, after)[1]


def pair_fill_halves(fs):
    n = len(fs)

    def body(*refs):
        dst = refs[n:2 * n]
        send_sems, recv_sems = refs[2 * n:]
        mx, my, mc = _me()
        copies = []
        for t in range(n):
            mine = _half_at(dst[t], (slice(None),), mc)
            theirs = _half_at(dst[t], (slice(None),), 1 - mc)
            cp = pltpu.make_async_remote_copy(
                src_ref=mine, dst_ref=mine, send_sem=send_sems.at[t], recv_sem=recv_sems.at[t],
                device_id=(mx, my, 1 - mc), device_id_type=MESH)
            cp.start()
            copies.append((cp, pltpu.make_async_remote_copy(
                src_ref=theirs, dst_ref=theirs, send_sem=send_sems.at[t], recv_sem=recv_sems.at[t],
                device_id=(mx, my, 1 - mc), device_id_type=MESH)))
        for cp, arrival in copies:
            cp.wait_send()
            arrival.wait_recv()

    any_spec = pl.BlockSpec(memory_space=pl.ANY)
    return pl.pallas_call(
        body, name="pair_fill_halves",
        in_specs=[any_spec] * n, out_specs=[any_spec] * n,
        out_shape=[jax.ShapeDtypeStruct(f.shape, f.dtype) for f in fs],
        input_output_aliases={t: t for t in range(n)},
        scratch_shapes=[pltpu.SemaphoreType.DMA((n,)), pltpu.SemaphoreType.DMA((n,))],
        compiler_params=_params(),
    )(*fs)


def _pack_rows(parts, d):
    rows, spans = [], []
    at = 0
    for p in parts:
        flat = p.reshape(-1)
        n_rows = -(-flat.shape[0] // (8 * d)) * 8
        flat = jnp.pad(flat, (0, n_rows * d - flat.shape[0]))
        rows.append(flat.reshape(n_rows, d))
        spans.append((at, p.shape))
        at += n_rows
    return jnp.concatenate(rows, axis=0), spans


def _unpack_rows(packed, spans):
    out = []
    for at, shape in spans:
        n = math.prod(shape)
        d = packed.shape[1]
        n_rows = -(-n // d)
        out.append(packed[at:at + n_rows].reshape(-1)[:n].reshape(shape))
    return out


def _rotate_half_matrix():
    half = QK_ROPE // 2
    idx = jnp.arange(QK_ROPE)
    src = jnp.where(idx < half, idx + half, idx - half)
    sign = jnp.where(idx < half, -1.0, 1.0)
    return (jnp.zeros((QK_ROPE, QK_ROPE), F32).at[src, idx].set(sign)).astype(BF16)


def kernel(x, c, positions, ada_w, ada_b, ffn1_norm, ffn1_w_gate, ffn1_w_up, ffn1_w_down, mix_norm, w_in, pool_w, pool_scale, q_a_norm, w_q_b, kv_a_norm, w_kv_b, w_out, ffn2_norm, ffn2_w_gate, ffn2_w_up, ffn2_w_down, final_norm, loss_target, m_ada_w, m_ada_b, m_ffn1_norm, m_ffn1_w_gate, m_ffn1_w_up, m_ffn1_w_down, m_mix_norm, m_w_in, m_pool_w, m_pool_scale, m_q_a_norm, m_w_q_b, m_kv_a_norm, m_w_kv_b, m_w_out, m_ffn2_norm, m_ffn2_w_gate, m_ffn2_w_up, m_ffn2_w_down, m_final_norm, v_ada_w, v_ada_b, v_ffn1_norm, v_ffn1_w_gate, v_ffn1_w_up, v_ffn1_w_down, v_mix_norm, v_w_in, v_pool_w, v_pool_scale, v_q_a_norm, v_w_q_b, v_kv_a_norm, v_w_kv_b, v_w_out, v_ffn2_norm, v_ffn2_w_gate, v_ffn2_w_up, v_ffn2_w_down, v_final_norm):
    mx, my, mc = _me()
    chip = 2 * mx + my
    half = jnp.reshape(mc, (1,)).astype(jnp.int32)
    chip1 = jnp.reshape(chip, (1,)).astype(jnp.int32)
    n_layers, d, ada_cols = ada_w.shape
    xt = x[0]
    tgt = loss_target[0]

    inv_freq = 1.0 / (ROPE_THETA ** (jnp.arange(0, QK_ROPE, 2, dtype=F32) / QK_ROPE))
    ang = positions[0].astype(F32)[:, None] * inv_freq
    ang = jnp.concatenate([ang, ang], axis=-1)
    cos, sin = jnp.cos(ang), jnp.sin(ang)
    rot = _rotate_half_matrix()
    rot_t = rot.T

    c_all = exchange8(c, True).reshape(N_DEV, d)
    c16 = jnp.pad(c_all, ((0, 8), (0, 0)))
    ada_b_loc = lax.dynamic_slice_in_dim(ada_b, chip * ada_cols, ada_cols, axis=1).reshape(n_layers, 1, ada_cols)
    mod_part = ada_fwd(c16, ada_w, ada_b_loc)[:, :N_DEV]
    mod_got = exchange8(jnp.transpose(mod_part, (1, 0, 2)), False)
    mod = jnp.transpose(mod_got.reshape(N_CHIPS, 2, n_layers, ada_cols)[:, 0], (1, 0, 2))
    mod = mod.reshape(n_layers, 9, 1, d)

    tr = lambda a: jnp.transpose(a, (0, 2, 1))
    local = [tr(ffn1_w_gate), tr(ffn1_w_up), ffn1_w_down, tr(w_in), tr(w_q_b), w_kv_b, w_out,
             tr(ffn2_w_gate), tr(ffn2_w_up), ffn2_w_down]
    ffn1_pos, rest_pos = (0, 1, 2), tuple(range(3, len(local)))
    groups = (ffn1_pos, rest_pos)
    placed = [cast_place(w, chip1, (0,), mod) for w in local]
    g_sems, lands_fly, g_token = gather_start([[p[0] for p in placed]], groups, mod, "first")
    if n_layers > 1:
        later = tuple(range(1, n_layers))
        placed = [cast_place(w, chip1, later, g_token) for w in local]
        more_sems, more_fly, g_token = gather_start(
            [[p[j] for p in placed] for j in range(len(later))], groups, g_token, "rest")
        g_sems, lands_fly = g_sems + more_sems, lands_fly + more_fly
    gathered = []

    row = lambda a, l: a[l].reshape(1, -1)
    saved = []
    for l in range(n_layers):
        g1, u1, d1 = gather_forward(gather_wait(
            f"{l}a", g_sems[l][0], [lands_fly[l][t] for t in ffn1_pos], xt if l else g_token))
        sv = dict(x0=xt)
        xt, sv["h1"], sv["gate1"], sv["up1"], sv["y1"] = ffn_fwd(
            xt, row(ffn1_norm, l), mod[l, 0], mod[l, 1], mod[l, 2], g1, u1, d1)
        sv["x1"] = xt
        win, wq, wkv, wout, g2, u2, d2 = gather_forward(gather_wait(
            f"{l}b", g_sems[l][1], [lands_fly[l][t] for t in rest_pos], xt))
        gathered.append([g1, u1, d1, win, wq, wkv, wout, g2, u2, d2])
        win = win.reshape(-1, d)
        sv["h2"], u, cq, ckv, kr = mix_in_fwd(xt, row(mix_norm, l), mod[l, 3], mod[l, 4], win)
        sv["cq"], sv["ckv"] = cq, ckv
        yp, sv["diff"] = pool_fwd(u, pool_w[l], row(pool_scale, l))
        qh, kh, vh, sv["ql"], sv["kvl"] = mla_qkv_fwd(
            cq, ckv, kr, row(q_a_norm, l), row(kv_a_norm, l), wq, wkv, cos, sin, rot)
        sv["qkv"] = (qh, kh, vh)
        om = attn_fwd(qh, kh, vh)
        xt, sv["ycat"], sv["y2"] = out_proj_fwd(yp, om, wout, xt, mod[l, 5])
        sv["x2"] = xt
        xt, sv["h3"], sv["gate3"], sv["up3"], sv["y3"] = ffn_fwd(
            xt, row(ffn2_norm, l), mod[l, 6], mod[l, 7], mod[l, 8], g2, u2, d2)
        saved.append(sv)

    loss_vec, dx, d_final_norm = final_loss(xt, final_norm.reshape(1, d), tgt)
    loss = lax.psum(loss_vec[0, 0], ("x", "y", "c"))

    none = [None] * n_layers
    dmods, dnorm1, dnorm2, dnorm3 = list(none), list(none), list(none), list(none)
    dpw, dps, dqan_l, dkvan_l = list(none), list(none), list(none), list(none)
    reduced = [None] * len(local)
    stages = []
    sel_of = lambda l: jnp.stack([mc, chip, jnp.asarray(l, mc.dtype)]).astype(jnp.int32)

    def to_chips(job, after_wait, after_start):
        send, recv, g_fly, lands_p = job.pop("pair")
        g_fly, got = pair_wait(job["tag"], send, recv, g_fly, lands_p, after_wait)
        pbs, job["owns"] = pair_add(g_fly, got, sel_of(job["l"]))
        job["scatter"] = scatter_start(pbs, job["tag"], after_start)
        return job["scatter"][4][0, 0]

    def finish(job, after):
        s_send, s_recv, pbs_fly, lands_j, _ = job.pop("scatter")
        parts = scatter_wait(job["tag"], s_send, s_recv, pbs_fly, lands_j, after)
        sums = chip_sum(job["owns"], parts, sel_of(job["l"]), [(n_layers,) + shp for shp in job["shapes"]],
                        [reduced[t] for t in job["pos"]])
        for t, total_t in zip(job["pos"], sums):
            reduced[t] = total_t

    def checkpoint(tag, l, positions, grads_, done, before_scatter=None):
        send, recv, g_fly, lands_p, tok = pair_start(grads_, tag, done)
        order = tok[0, 0]
        if stages:
            order = order + to_chips(stages[-1], done, done if before_scatter is None else before_scatter)
        if len(stages) >= 3:
            finish(stages[-3], done)
        stages.append(dict(tag=tag, l=l, pos=positions, shapes=[g.shape[1:] for g in grads_],
                           pair=(send, recv, g_fly, lands_p)))
        return order

    order = None

    for l in reversed(range(n_layers)):
        sv = saved[l]
        g1, u1, d1, win, wq, wkv, wout, g2, u2, d2 = gathered[l]
        win = win.reshape(-1, d)
        gt3 = mod[l, 8] if order is None else mod[l, 8] + order
        dy, a, dgt, dup = ffn_bwd_act(dx, sv["gate3"], sv["up3"], gt3, d2)
        dx, dvec3 = ffn_bwd_in(dx, sv["x2"], sv["y3"], dgt, dup, row(ffn2_norm, l), mod[l, 7], g2, u2)
        g_g2, g_u2, g_d2 = tn_mm(dgt, sv["h3"][None]), tn_mm(dup, sv["h3"][None]), tn_mm(a, dy[None])
        dy2, dyp, dom, dg2 = out_proj_bwd(dx, sv["y2"], mod[l, 5], wout)
        g_wout = tn_mm(sv["ycat"], dy2[None])
        qh, kh, vh = sv["qkv"]
        dqh, dkh, dvh = attn_bwd(qh, kh, vh, dom)
        dcq, dckv, dkr_in, gq, gkv, dqan_l[l], dkvan_l[l] = mla_qkv_bwd(
            dqh, dkh, dvh, sv["cq"], sv["ckv"], row(q_a_norm, l), row(kv_a_norm, l), wq, wkv, cos, sin, rot_t)
        g_wq, g_wkv = tn_mm(gq, sv["ql"][None]), tn_mm(sv["kvl"][None], gkv)
        du, dpw[l], dps[l] = pool_bwd(dyp, sv["diff"], pool_w[l], row(pool_scale, l))
        dx, dz, dvec2 = mix_in_bwd(dx, du, dcq, dckv, dkr_in, sv["x1"], row(mix_norm, l), mod[l, 4], win)
        g_win = tn_mm(dz[None], sv["h2"][None]).reshape(N_CHIPS, -1, d)
        order = checkpoint(f"{l}a", l, rest_pos, [g_win, g_wq, g_wkv, g_wout, g_g2, g_u2, g_d2], dx)
        dy, a, dgt, dup = ffn_bwd_act(dx, sv["gate1"], sv["up1"], mod[l, 2] + order, d1)
        dx, dvec1 = ffn_bwd_in(dx, sv["x0"], sv["y1"], dgt, dup, row(ffn1_norm, l), mod[l, 1], g1, u1)
        g_g1, g_u1, g_d1 = tn_mm(dgt, sv["h1"][None]), tn_mm(dup, sv["h1"][None]), tn_mm(a, dy[None])
        dmods[l] = jnp.concatenate([dvec1[0:3], dvec2[0:2], dg2, dvec3[0:3]], axis=0)
        dnorm1[l], dnorm2[l], dnorm3[l] = dvec1[3], dvec2[3], dvec3[3]
        if l == 0:
            small_parts = [jnp.stack(dmods), jnp.stack(dnorm1), jnp.stack(dnorm2), jnp.stack(dnorm3), d_final_norm,
                           jnp.stack(dps), jnp.stack(dqan_l), jnp.stack(dkvan_l), jnp.stack(dpw)]
            packed, spans = _pack_rows(small_parts, d)
            me = 4 * mx + 2 * my + mc
            small_land = lax.dynamic_update_index_in_dim(lax.empty((N_DEV,) + packed.shape, F32), packed, me, 0)
            small_fly = gather8_start(packed, small_land, dx)

        order = checkpoint(f"{l}b", l, ffn1_pos, [g_g1, g_u1, g_d1], dx, small_fly[4] if l == 0 else None)

    to_chips(stages[-1], dx, dx)
    gathered_small = gather8_wait(small_fly[0], small_fly[1], small_fly[2], small_fly[3], stages[-1]["scatter"][4])
    total = sum_devices(gathered_small)
    (g_ada_b, g_n1, g_n2, g_n3, g_fn, g_ps, g_qan, g_kvan, g_pw) = _unpack_rows(total, spans)
    dmod_all = gathered_small[:, :9 * n_layers].reshape(N_DEV, n_layers, 9 * d)
    dmod_loc = lax.dynamic_slice_in_dim(dmod_all, chip * ada_cols, ada_cols, axis=2)
    dmod16 = jnp.pad(jnp.transpose(dmod_loc, (1, 0, 2)), ((0, 0), (0, 8), (0, 0)))
    g_ada_w = ada_bwd(c16, dmod16)

    grads = [g_ada_w, g_ada_b, g_n1, None, None, None, g_n2, None, g_pw, g_ps, g_qan, None, g_kvan, None, None, g_n3,
             None, None, None, g_fn]
    weights = [ada_w, ada_b, ffn1_norm, ffn1_w_gate, ffn1_w_up, ffn1_w_down, mix_norm, w_in, pool_w, pool_scale,
               q_a_norm, w_q_b, kv_a_norm, w_kv_b, w_out, ffn2_norm, ffn2_w_gate, ffn2_w_up, ffn2_w_down, final_norm]
    ms = [m_ada_w, m_ada_b, m_ffn1_norm, m_ffn1_w_gate, m_ffn1_w_up, m_ffn1_w_down, m_mix_norm, m_w_in, m_pool_w,
          m_pool_scale, m_q_a_norm, m_w_q_b, m_kv_a_norm, m_w_kv_b, m_w_out, m_ffn2_norm, m_ffn2_w_gate, m_ffn2_w_up,
          m_ffn2_w_down, m_final_norm]
    vs = [v_ada_w, v_ada_b, v_ffn1_norm, v_ffn1_w_gate, v_ffn1_w_up, v_ffn1_w_down, v_mix_norm, v_w_in, v_pool_w,
          v_pool_scale, v_q_a_norm, v_w_q_b, v_kv_a_norm, v_w_kv_b, v_w_out, v_ffn2_norm, v_ffn2_w_gate, v_ffn2_w_up,
          v_ffn2_w_down, v_final_norm]
    transposed = (3, 4, 7, 11, 16, 17)
    outs = [None] * len(weights)
    for i, (w, g, m, v) in enumerate(zip(weights, grads, ms, vs)):
        if g is not None:
            g = g.reshape(w.shape)
            outs[i] = (g,) + adamw(w, g, m, v)
    for job in stages[-3:]:
        finish(job, outs[0][1])
    g_local = iter(pair_fill_halves(reduced))
    for i, (w, g, m, v) in enumerate(zip(weights, grads, ms, vs)):
        if g is None:
            g = next(g_local)
            if i in transposed:
                outs[i] = tuple(tr(t) for t in (g,) + adamw(tr(w), g, tr(m), tr(v)))
            else:
                outs[i] = (g,) + adamw(w, g, m, v)
    return (loss, dx.reshape(x.shape), *[t[0] for t in outs], *[t[1] for t in outs], *[t[2] for t in outs],
            *[t[3] for t in outs])
```

```python
import math

import jax
import jax.numpy as jnp
from jax import lax
from jax.experimental import pallas as pl
from jax.experimental.pallas import tpu as pltpu

F32 = jnp.float32
BF16 = jnp.bfloat16
MESH = pl.DeviceIdType.MESH

EPS = 1e-6
ROPE_THETA = 10000.0
N_HEADS = 4
QK_NOPE = 128
QK_ROPE = 64
V_HEAD = 128
POOL_WINDOWS = (2, 4, 8, 16)
POOL_GC = 128
POOL_WIDTH = POOL_GC * len(POOL_WINDOWS)
Q_LORA = 384
KV_LORA = 256
SOFTMAX_SCALE = 1.0 / math.sqrt(QK_NOPE + QK_ROPE)
N_CHIPS = 4
N_DEV = 8

ADAM_LR = 0.001
ADAM_B1 = 0.9
ADAM_B2 = 0.999
ADAM_EPS = 1e-08
ADAM_WD = 0.01
ADAM_STEP = 10

ROW_TILE = 512
ATT_TILE = 256
VMEM_LIMIT = 56 * 1024 * 1024
BF16_ROWS = 16
LANES = 128


def _params(sem=None, vmem=VMEM_LIMIT):
    return pltpu.CompilerParams(dimension_semantics=sem, vmem_limit_bytes=vmem)


def _dot(a, b):
    return jnp.dot(a, b, preferred_element_type=F32)


def _dot_nt(a, b):
    return lax.dot_general(a, b, (((1,), (1,)), ((), ())), preferred_element_type=F32)


def _dot_tn(a, b):
    return lax.dot_general(a, b, (((0,), (0,)), ((), ())), preferred_element_type=F32)


def _dot_exact(t, perm):
    t1 = t.astype(BF16)
    r1 = t - t1.astype(F32)
    t2 = r1.astype(BF16)
    t3 = (r1 - t2.astype(F32)).astype(BF16)
    return _dot(t1, perm) + _dot(t2, perm) + _dot(t3, perm)


def _sum0(a):
    return jnp.sum(a, axis=0, keepdims=True)


def _rms(xt):
    r = lax.rsqrt(jnp.mean(xt * xt, axis=-1, keepdims=True) + EPS)
    return xt * r, r


def _rms_bwd(dy, xt, g):
    xhat, r = _rms(xt)
    dxhat = dy * g
    dx = r * (dxhat - xhat * jnp.mean(dxhat * xhat, axis=-1, keepdims=True))
    return dx, _sum0(dy * xhat)


def _normmod_bwd(dh, xt, gn, sc):
    xhat, _ = _rms(xt)
    dn = dh * (1.0 + sc)
    dx, dgn = _rms_bwd(dn, xt, gn)
    return dx, _sum0(dh), _sum0(dh * (xhat * gn)), dgn


def _row_tile(s):
    return min(s, ROW_TILE)


def _full(shape):
    n = len(shape)
    return pl.BlockSpec(shape, lambda *_: (0,) * n)


def _resident(shape):
    n = len(shape)
    return pl.BlockSpec(shape, lambda *_: (0,) * n, pipeline_mode=pl.Buffered(1))


def ffn_fwd(x, gn, sh, sc, gt, wg, wu, wd):
    s, d = x.shape
    k_chunks, fs, _ = wg.shape
    tm = _row_tile(s)

    def body(x_ref, gn_ref, sh_ref, sc_ref, gt_ref, wg_ref, wu_ref, wd_ref,
             xo_ref, h_ref, a_ref, sl_ref, dsu_ref, y_ref):
        xt = x_ref[...]
        xhat, _ = _rms(xt)
        h = (xhat * gn_ref[...] * (1.0 + sc_ref[...]) + sh_ref[...]).astype(BF16)
        h_ref[...] = h
        y = jnp.zeros((tm, d), F32)
        for k in range(k_chunks):
            gate = _dot_nt(h, wg_ref[k])
            up = _dot_nt(h, wu_ref[k])
            sg = jax.nn.sigmoid(gate)
            sl = gate * sg
            a = (sl * up).astype(BF16)
            a_ref[k] = a
            sl_ref[k] = sl.astype(BF16)
            dsu_ref[k] = (up * (sg * (1.0 + gate * (1.0 - sg)))).astype(BF16)
            y += _dot(a, wd_ref[k])
        y_ref[...] = y.astype(BF16)
        xo_ref[...] = xt + 0.5 * gt_ref[...] * y

    row = pl.BlockSpec((tm, d), lambda i: (i, 0))
    vec = pl.BlockSpec((1, d), lambda i: (0, 0))
    act = pl.BlockSpec((k_chunks, tm, fs), lambda i: (0, i, 0))
    act_shape = jax.ShapeDtypeStruct((k_chunks, s, fs), BF16)
    return pl.pallas_call(
        body, name="ffn_fwd",
        grid=(s // tm,),
        in_specs=[row, vec, vec, vec, vec, _resident(wg.shape), _resident(wu.shape), _resident(wd.shape)],
        out_specs=[row, row, act, act, act, row],
        out_shape=[jax.ShapeDtypeStruct((s, d), F32), jax.ShapeDtypeStruct((s, d), BF16),
                   act_shape, act_shape, act_shape, jax.ShapeDtypeStruct((s, d), BF16)],
        compiler_params=_params(("arbitrary",)),
    )(x, gn, sh, sc, gt, wg, wu, wd)


def ffn_bwd_act(dxn, sl, dsu, gt, wd):
    s, d = dxn.shape
    k_chunks, fs, _ = wd.shape
    tm = _row_tile(s)

    def body(dxn_ref, sl_ref, dsu_ref, gt_ref, wd_ref, dy_ref, dgate_ref, dup_ref):
        dy = (0.5 * gt_ref[...] * dxn_ref[...]).astype(BF16)
        dy_ref[...] = dy
        for k in range(k_chunks):
            da = _dot_nt(dy, wd_ref[k])
            dgate_ref[k] = (da * dsu_ref[k].astype(F32)).astype(BF16)
            dup_ref[k] = (da * sl_ref[k].astype(F32)).astype(BF16)

    row = pl.BlockSpec((tm, d), lambda i: (i, 0))
    act = pl.BlockSpec((k_chunks, tm, fs), lambda i: (0, i, 0))
    act_shape = jax.ShapeDtypeStruct((k_chunks, s, fs), BF16)
    return pl.pallas_call(
        body, name="ffn_bwd_act",
        grid=(s // tm,),
        in_specs=[row, act, act, pl.BlockSpec((1, d), lambda i: (0, 0)), _resident(wd.shape)],
        out_specs=[row, act, act],
        out_shape=[jax.ShapeDtypeStruct((s, d), BF16), act_shape, act_shape],
        compiler_params=_params(("arbitrary",)),
    )(dxn, sl, dsu, gt, wd)


def ffn_bwd_in(dxn, x, y, dgate, dup, gn, sc, wg, wu):
    s, d = x.shape
    k_chunks, fs, _ = wg.shape
    tm = _row_tile(s)

    def body(dxn_ref, x_ref, y_ref, dgate_ref, dup_ref, gn_ref, sc_ref, wg_ref, wu_ref, dx_ref, dvec_ref):
        i = pl.program_id(0)

        @pl.when(i == 0)
        def _():
            dvec_ref[...] = jnp.zeros_like(dvec_ref)

        dh = jnp.zeros((tm, d), F32)
        for k in range(k_chunks):
            dh += _dot(dgate_ref[k], wg_ref[k]) + _dot(dup_ref[k], wu_ref[k])
        dxn_t = dxn_ref[...]
        dx, dsh, dsc, dgn = _normmod_bwd(dh, x_ref[...], gn_ref[...], sc_ref[...])
        dx_ref[...] = dx + dxn_t
        dvec_ref[0:1, :] += dsh
        dvec_ref[1:2, :] += dsc
        dvec_ref[2:3, :] += _sum0(0.5 * dxn_t * y_ref[...].astype(F32))
        dvec_ref[3:4, :] += dgn

    row = pl.BlockSpec((tm, d), lambda i: (i, 0))
    vec = pl.BlockSpec((1, d), lambda i: (0, 0))
    act = pl.BlockSpec((k_chunks, tm, fs), lambda i: (0, i, 0))
    return pl.pallas_call(
        body, name="ffn_bwd_in",
        grid=(s // tm,),
        in_specs=[row, row, row, act, act, vec, vec, _resident(wg.shape), _resident(wu.shape)],
        out_specs=[row, pl.BlockSpec((8, d), lambda i: (0, 0))],
        out_shape=[jax.ShapeDtypeStruct((s, d), F32), jax.ShapeDtypeStruct((8, d), F32)],
        compiler_params=_params(("arbitrary",)),
    )(dxn, x, y, dgate, dup, gn, sc, wg, wu)


def tn_mm(a, b):
    ga, s, m = a.shape
    gb, _, n = b.shape
    g = max(ga, gb)

    def body(a_ref, b_ref, o_ref):
        o_ref[...] = _dot_tn(a_ref[...], b_ref[...])

    a_spec = pl.BlockSpec((None, s, m), (lambda gi: (gi, 0, 0)) if ga > 1 else (lambda gi: (0, 0, 0)))
    b_spec = pl.BlockSpec((None, s, n), (lambda gi: (gi, 0, 0)) if gb > 1 else (lambda gi: (0, 0, 0)))
    return pl.pallas_call(
        body, name="tn_mm",
        grid=(g,), in_specs=[a_spec, b_spec], out_specs=pl.BlockSpec((None, m, n), lambda gi: (gi, 0, 0)),
        out_shape=jax.ShapeDtypeStruct((g, m, n), F32),
        compiler_params=_params(("arbitrary",)),
    )(a, b)


def mix_in_fwd(x, gn, sh, sc, w_in_t):
    s, d = x.shape
    tm = _row_tile(s)
    o1, o2, o3 = POOL_WIDTH, POOL_WIDTH + Q_LORA, POOL_WIDTH + Q_LORA + KV_LORA

    def body(x_ref, gn_ref, sh_ref, sc_ref, w_ref, h_ref, u_ref, cq_ref, ckv_ref, kr_ref):
        xhat, _ = _rms(x_ref[...])
        h = (xhat * gn_ref[...] * (1.0 + sc_ref[...]) + sh_ref[...]).astype(BF16)
        h_ref[...] = h
        z = _dot_nt(h, w_ref[0:o3, :])
        u_ref[...] = z[:, 0:o1]
        cq_ref[...] = z[:, o1:o2]
        ckv_ref[...] = z[:, o2:o3]
        kr_ref[...] = _dot_nt(h, w_ref[o3:, :])

    row = lambda w: pl.BlockSpec((tm, w), lambda i: (i, 0))
    vec = pl.BlockSpec((1, d), lambda i: (0, 0))
    return pl.pallas_call(
        body, name="mix_in_fwd",
        grid=(s // tm,),
        in_specs=[row(d), vec, vec, vec, _full(w_in_t.shape)],
        out_specs=[row(d), row(POOL_WIDTH), row(Q_LORA), row(KV_LORA), row(QK_ROPE)],
        out_shape=[jax.ShapeDtypeStruct((s, d), BF16), jax.ShapeDtypeStruct((s, POOL_WIDTH), F32),
                   jax.ShapeDtypeStruct((s, Q_LORA), F32), jax.ShapeDtypeStruct((s, KV_LORA), F32),
                   jax.ShapeDtypeStruct((s, QK_ROPE), F32)],
        compiler_params=_params(("arbitrary",)),
    )(x, gn, sh, sc, w_in_t)


def mix_in_bwd(dxn, du, dcq, dckv, dkr, x, gn, sc, w_in_t):
    s, d = x.shape
    tm = _row_tile(s)
    o1, o2, o3 = POOL_WIDTH, POOL_WIDTH + Q_LORA, POOL_WIDTH + Q_LORA + KV_LORA
    n_z = w_in_t.shape[0]

    def body(dxn_ref, du_ref, dcq_ref, dckv_ref, dkr_ref, x_ref, gn_ref, sc_ref, w_ref, dx_ref, dz_ref, dvec_ref):
        i = pl.program_id(0)

        @pl.when(i == 0)
        def _():
            dvec_ref[...] = jnp.zeros_like(dvec_ref)

        dub = du_ref[...].astype(BF16)
        dqb = dcq_ref[...].astype(BF16)
        dkb = dckv_ref[...].astype(BF16)
        drb = dkr_ref[...].astype(BF16)
        dz_ref[:, 0:o1] = dub
        dz_ref[:, o1:o2] = dqb
        dz_ref[:, o2:o3] = dkb
        dz_ref[:, o3:] = drb
        dh = (_dot(dub, w_ref[0:o1, :]) + _dot(dqb, w_ref[o1:o2, :]) + _dot(dkb, w_ref[o2:o3, :])
              + _dot(drb, w_ref[o3:, :]))
        dx, dsh, dsc, dgn = _normmod_bwd(dh, x_ref[...], gn_ref[...], sc_ref[...])
        dx_ref[...] = dx + dxn_ref[...]
        dvec_ref[0:1, :] += dsh
        dvec_ref[1:2, :] += dsc
        dvec_ref[3:4, :] += dgn

    row = lambda w: pl.BlockSpec((tm, w), lambda i: (i, 0))
    vec = pl.BlockSpec((1, d), lambda i: (0, 0))
    return pl.pallas_call(
        body, name="mix_in_bwd",
        grid=(s // tm,),
        in_specs=[row(d), row(POOL_WIDTH), row(Q_LORA), row(KV_LORA), row(QK_ROPE), row(d), vec, vec,
                  _full(w_in_t.shape)],
        out_specs=[row(d), row(n_z), pl.BlockSpec((8, d), lambda i: (0, 0))],
        out_shape=[jax.ShapeDtypeStruct((s, d), F32), jax.ShapeDtypeStruct((s, n_z), BF16),
                   jax.ShapeDtypeStruct((8, d), F32)],
        compiler_params=_params(("arbitrary",)),
    )(dxn, du, dcq, dckv, dkr, x, gn, sc, w_in_t)


def _window_sum(a, w, rows, forward):
    s = a.shape[0]
    step = 1
    while step < w:
        if forward:
            shifted = jnp.where(rows < s - step, pltpu.roll(a, s - step, 0), 0.0)
        else:
            shifted = jnp.where(rows >= step, pltpu.roll(a, step, 0), 0.0)
        a = a + shifted
        step *= 2
    return a


def pool_fwd(u, pool_w, pool_scale):
    s = u.shape[0]

    def body(u_ref, w_ref, sc_ref, y_ref, diff_ref):
        rows = lax.broadcasted_iota(jnp.int32, (s, POOL_GC), 0)
        for g, w in enumerate(POOL_WINDOWS):
            cols = slice(g * POOL_GC, (g + 1) * POOL_GC)
            ug = u_ref[:, cols]
            cnt = jnp.minimum(rows + 1, w).astype(F32)
            diff = (_window_sum(ug, w, rows, False) / cnt - ug).astype(BF16)
            diff_ref[:, cols] = diff
            y_ref[:, cols] = _dot(diff, w_ref[g].astype(BF16)) * sc_ref[:, cols]

    return pl.pallas_call(
        body, name="pool_fwd",
        out_shape=[jax.ShapeDtypeStruct(u.shape, F32), jax.ShapeDtypeStruct(u.shape, BF16)],
        compiler_params=_params(),
    )(u, pool_w, pool_scale)


def pool_bwd(dy, diff, pool_w, pool_scale):
    s = dy.shape[0]

    def body(dy_ref, diff_ref, w_ref, sc_ref, du_ref, dw_ref, dsc_ref):
        rows = lax.broadcasted_iota(jnp.int32, (s, POOL_GC), 0)
        for g, w in enumerate(POOL_WINDOWS):
            cols = slice(g * POOL_GC, (g + 1) * POOL_GC)
            dyg = dy_ref[:, cols]
            diff = diff_ref[:, cols]
            wb = w_ref[g].astype(BF16)
            dsc_ref[:, cols] = _sum0(dyg * _dot(diff, wb))
            dys = (dyg * sc_ref[:, cols]).astype(BF16)
            dw_ref[g] = _dot_tn(diff, dys)
            ddiff = _dot_nt(dys, wb)
            cnt = jnp.minimum(rows + 1, w).astype(F32)
            du_ref[:, cols] = _window_sum(ddiff / cnt, w, rows, True) - ddiff

    return pl.pallas_call(
        body, name="pool_bwd",
        out_shape=[jax.ShapeDtypeStruct(dy.shape, F32), jax.ShapeDtypeStruct(pool_w.shape, F32),
                   jax.ShapeDtypeStruct(pool_scale.shape, F32)],
        compiler_params=_params(),
    )(dy, diff, pool_w, pool_scale)


def mla_qkv_fwd(cq, ckv, kr, qan, kvan, wq, wkv, cos, sin, rot):
    s = cq.shape[0]
    tm = _row_tile(s)

    def body(cq_ref, ckv_ref, kr_ref, qan_ref, kvan_ref, wq_ref, wkv_ref, cos_ref, sin_ref, rot_ref,
             q_ref, k_ref, v_ref, ql_ref, kvl_ref):
        cos_t = cos_ref[...]
        sin_t = sin_ref[...]
        perm = rot_ref[...]

        def rope(t):
            return t * cos_t + _dot_exact(t, perm) * sin_t

        qhat, _ = _rms(cq_ref[...])
        ql = (qhat * qan_ref[...]).astype(BF16)
        ql_ref[...] = ql
        khat, _ = _rms(ckv_ref[...])
        kvl = (khat * kvan_ref[...]).astype(BF16)
        kvl_ref[...] = kvl
        krr = rope(kr_ref[...]).astype(BF16)
        for h in range(N_HEADS):
            q = _dot_nt(ql, wq_ref[h])
            q_ref[h, :, 0:QK_NOPE] = q[:, 0:QK_NOPE].astype(BF16)
            q_ref[h, :, QK_NOPE:] = rope(q[:, QK_NOPE:]).astype(BF16)
            kv = _dot(kvl, wkv_ref[h])
            k_ref[h, :, 0:QK_NOPE] = kv[:, 0:QK_NOPE].astype(BF16)
            k_ref[h, :, QK_NOPE:] = krr
            v_ref[h] = kv[:, QK_NOPE:].astype(BF16)

    row = lambda w: pl.BlockSpec((tm, w), lambda i: (i, 0))
    hrow = lambda w: pl.BlockSpec((N_HEADS, tm, w), lambda i: (0, i, 0))
    qk = QK_NOPE + QK_ROPE
    return pl.pallas_call(
        body, name="mla_qkv_fwd",
        grid=(s // tm,),
        in_specs=[row(Q_LORA), row(KV_LORA), row(QK_ROPE), _full(qan.shape), _full(kvan.shape),
                  _full(wq.shape), _full(wkv.shape), row(QK_ROPE), row(QK_ROPE), _full(rot.shape)],
        out_specs=[hrow(qk), hrow(qk), hrow(V_HEAD), row(Q_LORA), row(KV_LORA)],
        out_shape=[jax.ShapeDtypeStruct((N_HEADS, s, qk), BF16), jax.ShapeDtypeStruct((N_HEADS, s, qk), BF16),
                   jax.ShapeDtypeStruct((N_HEADS, s, V_HEAD), BF16), jax.ShapeDtypeStruct((s, Q_LORA), BF16),
                   jax.ShapeDtypeStruct((s, KV_LORA), BF16)],
        compiler_params=_params(("arbitrary",)),
    )(cq, ckv, kr, qan, kvan, wq, wkv, cos, sin, rot)


def _attn_probs(q_ref, k_ref, qi, tq):
    n = (qi + 1) * tq
    rows = slice(qi * tq, n)
    sc = _dot_nt(q_ref[rows, :], k_ref[0:n, :]) * SOFTMAX_SCALE
    qpos = qi * tq + lax.broadcasted_iota(jnp.int32, (tq, n), 0)
    kpos = lax.broadcasted_iota(jnp.int32, (tq, n), 1)
    sc = jnp.where(qpos >= kpos, sc, -1e30)
    e = jnp.exp(sc - jnp.max(sc, axis=-1, keepdims=True))
    return e / jnp.sum(e, axis=-1, keepdims=True)


def attn_fwd(q, k, v):
    nh, s, qk = q.shape
    tq = min(s, ATT_TILE)

    def body(q_ref, k_ref, v_ref, o_ref):
        for qi in range(s // tq):
            n = (qi + 1) * tq
            p = _attn_probs(q_ref, k_ref, qi, tq).astype(BF16)
            o_ref[qi * tq:n, :] = _dot(p, v_ref[0:n, :])

    head = lambda w: pl.BlockSpec((None, s, w), lambda h: (h, 0, 0))
    return pl.pallas_call(
        body, name="attn_fwd",
        grid=(nh,),
        in_specs=[head(qk), head(qk), head(V_HEAD)],
        out_specs=pl.BlockSpec((s, V_HEAD), lambda h: (0, h)),
        out_shape=jax.ShapeDtypeStruct((s, nh * V_HEAD), F32),
        compiler_params=_params(("arbitrary",)),
    )(q, k, v)


def attn_bwd(q, k, v, do):
    nh, s, qk = q.shape
    tq = min(s, ATT_TILE)

    def body(q_ref, k_ref, v_ref, do_ref, dq_ref, dk_ref, dv_ref):
        dk_ref[...] = jnp.zeros_like(dk_ref)
        dv_ref[...] = jnp.zeros_like(dv_ref)
        for qi in range(s // tq):
            n = (qi + 1) * tq
            rows = slice(qi * tq, n)
            p = _attn_probs(q_ref, k_ref, qi, tq)
            dob = do_ref[rows, :].astype(BF16)
            dp = _dot_nt(dob, v_ref[0:n, :])
            ds = (p * (dp - jnp.sum(p * dp, axis=-1, keepdims=True)) * SOFTMAX_SCALE).astype(BF16)
            dq_ref[rows, :] = _dot(ds, k_ref[0:n, :])
            dk_ref[0:n, :] += _dot_tn(ds, q_ref[rows, :])
            dv_ref[0:n, :] += _dot_tn(p.astype(BF16), dob)

    head = lambda w: pl.BlockSpec((None, s, w), lambda h: (h, 0, 0))
    return pl.pallas_call(
        body, name="attn_bwd",
        grid=(nh,),
        in_specs=[head(qk), head(qk), head(V_HEAD), pl.BlockSpec((s, V_HEAD), lambda h: (0, h))],
        out_specs=[head(qk), head(qk), head(V_HEAD)],
        out_shape=[jax.ShapeDtypeStruct((nh, s, qk), F32), jax.ShapeDtypeStruct((nh, s, qk), F32),
                   jax.ShapeDtypeStruct((nh, s, V_HEAD), F32)],
        compiler_params=_params(("arbitrary",)),
    )(q, k, v, do)


def mla_qkv_bwd(dq, dk, dv, cq, ckv, qan, kvan, wq, wkv, cos, sin, rot_t):
    s = cq.shape[0]
    tm = _row_tile(s)

    def body(dq_ref, dk_ref, dv_ref, cq_ref, ckv_ref, qan_ref, kvan_ref,
             wq_ref, wkv_ref, cos_ref, sin_ref, rot_ref,
             dcq_ref, dckv_ref, dkro_ref, gq_ref, gkv_ref, dqan_ref, dkvan_ref):
        i = pl.program_id(0)

        @pl.when(i == 0)
        def _():
            dqan_ref[...] = jnp.zeros_like(dqan_ref)
            dkvan_ref[...] = jnp.zeros_like(dkvan_ref)

        cos_t = cos_ref[...]
        sin_t = sin_ref[...]
        perm_t = rot_ref[...]

        def unrope(t):
            return t * cos_t + _dot_exact(t * sin_t, perm_t)

        acc_q = jnp.zeros((tm, Q_LORA), F32)
        acc_kv = jnp.zeros((tm, KV_LORA), F32)
        dkr_sum = jnp.zeros((tm, QK_ROPE), F32)
        for h in range(N_HEADS):
            dq_h = dq_ref[h]
            a = dq_h[:, 0:QK_NOPE].astype(BF16)
            b = unrope(dq_h[:, QK_NOPE:]).astype(BF16)
            gq_ref[h, :, 0:QK_NOPE] = a
            gq_ref[h, :, QK_NOPE:] = b
            wq_h = wq_ref[h]
            acc_q += _dot(a, wq_h[0:QK_NOPE, :]) + _dot(b, wq_h[QK_NOPE:, :])
            dk_h = dk_ref[h]
            dk = dk_h[:, 0:QK_NOPE].astype(BF16)
            dvv = dv_ref[h].astype(BF16)
            gkv_ref[h, :, 0:QK_NOPE] = dk
            gkv_ref[h, :, QK_NOPE:] = dvv
            wkv_h = wkv_ref[h]
            acc_kv += _dot_nt(dk, wkv_h[:, 0:QK_NOPE]) + _dot_nt(dvv, wkv_h[:, QK_NOPE:])
            dkr_sum += dk_h[:, QK_NOPE:]
        dkro_ref[...] = unrope(dkr_sum)
        dcq, dqan = _rms_bwd(acc_q, cq_ref[...], qan_ref[...])
        dcq_ref[...] = dcq
        dqan_ref[...] += dqan
        dckv, dkvan = _rms_bwd(acc_kv, ckv_ref[...], kvan_ref[...])
        dckv_ref[...] = dckv
        dkvan_ref[...] += dkvan

    row = lambda w: pl.BlockSpec((tm, w), lambda i: (i, 0))
    hrow = lambda w: pl.BlockSpec((N_HEADS, tm, w), lambda i: (0, i, 0))
    return pl.pallas_call(
        body, name="mla_qkv_bwd",
        grid=(s // tm,),
        in_specs=[hrow(QK_NOPE + QK_ROPE), hrow(QK_NOPE + QK_ROPE), hrow(V_HEAD),
                  row(Q_LORA), row(KV_LORA), _full(qan.shape), _full(kvan.shape),
                  _full(wq.shape), _full(wkv.shape), row(QK_ROPE), row(QK_ROPE), _full(rot_t.shape)],
        out_specs=[row(Q_LORA), row(KV_LORA), row(QK_ROPE), hrow(QK_NOPE + QK_ROPE), hrow(QK_NOPE + V_HEAD),
                   _full(qan.shape), _full(kvan.shape)],
        out_shape=[jax.ShapeDtypeStruct((s, Q_LORA), F32), jax.ShapeDtypeStruct((s, KV_LORA), F32),
                   jax.ShapeDtypeStruct((s, QK_ROPE), F32),
                   jax.ShapeDtypeStruct((N_HEADS, s, QK_NOPE + QK_ROPE), BF16),
                   jax.ShapeDtypeStruct((N_HEADS, s, QK_NOPE + V_HEAD), BF16),
                   jax.ShapeDtypeStruct(qan.shape, F32), jax.ShapeDtypeStruct(kvan.shape, F32)],
        compiler_params=_params(("arbitrary",)),
    )(dq, dk, dv, cq, ckv, qan, kvan, wq, wkv, cos, sin, rot_t)


def out_proj_fwd(yp, om, w_out, x, gt):
    s, d = x.shape
    n_sh, rs, _ = w_out.shape
    tm = _row_tile(s)
    per = POOL_WIDTH // rs

    def body(yp_ref, om_ref, w_ref, x_ref, gt_ref, xo_ref, ycat_ref, y_ref):
        y = jnp.zeros((tm, d), F32)
        for j in range(n_sh):
            src = yp_ref if j < per else om_ref
            part = src[:, (j % per) * rs:(j % per + 1) * rs].astype(BF16)
            ycat_ref[j] = part
            y += _dot(part, w_ref[j])
        y_ref[...] = y.astype(BF16)
        xo_ref[...] = x_ref[...] + gt_ref[...] * y

    row = lambda w: pl.BlockSpec((tm, w), lambda i: (i, 0))
    return pl.pallas_call(
        body, name="out_proj_fwd",
        grid=(s // tm,),
        in_specs=[row(POOL_WIDTH), row(POOL_WIDTH), _full(w_out.shape), row(d), pl.BlockSpec((1, d), lambda i: (0, 0))],
        out_specs=[row(d), pl.BlockSpec((n_sh, tm, rs), lambda i: (0, i, 0)), row(d)],
        out_shape=[jax.ShapeDtypeStruct((s, d), F32), jax.ShapeDtypeStruct((n_sh, s, rs), BF16),
                   jax.ShapeDtypeStruct((s, d), BF16)],
        compiler_params=_params(("arbitrary",)),
    )(yp, om, w_out, x, gt)


def out_proj_bwd(dxn, y, gt, w_out):
    s, d = dxn.shape
    n_sh, rs, _ = w_out.shape
    tm = _row_tile(s)
    per = POOL_WIDTH // rs

    def body(dxn_ref, y_ref, gt_ref, w_ref, dy_ref, dyp_ref, dom_ref, dgt_ref):
        i = pl.program_id(0)

        @pl.when(i == 0)
        def _():
            dgt_ref[...] = jnp.zeros_like(dgt_ref)

        dxn_t = dxn_ref[...]
        dy = (gt_ref[...] * dxn_t).astype(BF16)
        dy_ref[...] = dy
        dgt_ref[...] += _sum0(dxn_t * y_ref[...].astype(F32))
        for j in range(n_sh):
            dst = dyp_ref if j < per else dom_ref
            dst[:, (j % per) * rs:(j % per + 1) * rs] = _dot_nt(dy, w_ref[j])

    row = lambda w: pl.BlockSpec((tm, w), lambda i: (i, 0))
    vec = pl.BlockSpec((1, d), lambda i: (0, 0))
    return pl.pallas_call(
        body, name="out_proj_bwd",
        grid=(s // tm,),
        in_specs=[row(d), row(d), vec, _full(w_out.shape)],
        out_specs=[row(d), row(POOL_WIDTH), row(POOL_WIDTH), vec],
        out_shape=[jax.ShapeDtypeStruct((s, d), BF16), jax.ShapeDtypeStruct((s, POOL_WIDTH), F32),
                   jax.ShapeDtypeStruct((s, POOL_WIDTH), F32), jax.ShapeDtypeStruct((1, d), F32)],
        compiler_params=_params(("arbitrary",)),
    )(dxn, y, gt, w_out)


def final_loss(x, gn, tgt):
    s, d = x.shape
    tm = _row_tile(s)

    def body(x_ref, gn_ref, t_ref, loss_ref, dx_ref, dgn_ref):
        i = pl.program_id(0)

        @pl.when(i == 0)
        def _():
            loss_ref[...] = jnp.zeros_like(loss_ref)
            dgn_ref[...] = jnp.zeros_like(dgn_ref)

        xt = x_ref[...]
        g = gn_ref[...]
        xhat, _ = _rms(xt)
        err = xhat * g - t_ref[...]
        per_tok = jnp.mean(err * err, axis=-1, keepdims=True)
        loss_ref[...] += jnp.broadcast_to(0.5 * _sum0(per_tok), loss_ref.shape)
        dx, dgn = _rms_bwd(err * (1.0 / d), xt, g)
        dx_ref[...] = dx
        dgn_ref[...] += dgn

    row = pl.BlockSpec((tm, d), lambda i: (i, 0))
    vec = pl.BlockSpec((1, d), lambda i: (0, 0))
    return pl.pallas_call(
        body, name="final_loss",
        grid=(s // tm,),
        in_specs=[row, vec, row],
        out_specs=[pl.BlockSpec((1, LANES), lambda i: (0, 0)), row, vec],
        out_shape=[jax.ShapeDtypeStruct((1, LANES), F32), jax.ShapeDtypeStruct((s, d), F32),
                   jax.ShapeDtypeStruct((1, d), F32)],
        compiler_params=_params(("arbitrary",)),
    )(x, gn, tgt)


def _col_tile(cols):
    return 768 if cols % 768 == 0 else cols


def ada_fwd(c16, ada_w, ada_b_loc):
    n_layers, d, cols = ada_w.shape
    tn = _col_tile(cols)

    def body(c_ref, w_ref, b_ref, o_ref):
        cv = c_ref[...]
        ca = (cv * jax.nn.sigmoid(cv)).astype(BF16)
        o_ref[...] = _dot(ca, w_ref[...].astype(BF16)) + b_ref[...]

    return pl.pallas_call(
        body, name="ada_fwd",
        grid=(n_layers, cols // tn),
        in_specs=[pl.BlockSpec((16, d), lambda l, j: (0, 0)), pl.BlockSpec((None, d, tn), lambda l, j: (l, 0, j)),
                  pl.BlockSpec((None, 1, tn), lambda l, j: (l, 0, j))],
        out_specs=pl.BlockSpec((None, 16, tn), lambda l, j: (l, 0, j)),
        out_shape=jax.ShapeDtypeStruct((n_layers, 16, cols), F32),
        compiler_params=_params(("arbitrary", "arbitrary")),
    )(c16, ada_w, ada_b_loc)


def ada_bwd(c16, dmod16):
    n_layers, _, cols = dmod16.shape
    d = c16.shape[1]
    tn = _col_tile(cols)

    def body(c_ref, g_ref, o_ref):
        cv = c_ref[...]
        ca = (cv * jax.nn.sigmoid(cv)).astype(BF16)
        o_ref[...] = _dot_tn(ca, g_ref[...].astype(BF16))

    return pl.pallas_call(
        body, name="ada_bwd",
        grid=(n_layers, cols // tn),
        in_specs=[pl.BlockSpec((16, d), lambda l, j: (0, 0)), pl.BlockSpec((None, 16, tn), lambda l, j: (l, 0, j))],
        out_specs=pl.BlockSpec((None, d, tn), lambda l, j: (l, 0, j)),
        out_shape=jax.ShapeDtypeStruct((n_layers, d, cols), F32),
        compiler_params=_params(("arbitrary", "arbitrary")),
    )(c16, dmod16)


def _as_rows(a):
    if a.ndim == 1:
        return a.reshape(1, a.shape[0])
    return a.reshape(-1, a.shape[-1])


def _rows_tile(r, c, itemsize=4, budget=2 * 1024 * 1024):
    if r * c * itemsize <= budget:
        return r
    best = None
    t = BF16_ROWS
    while t < r:
        if r % t == 0 and t * c * itemsize <= budget:
            best = t
        t += BF16_ROWS
    return best if best is not None else r


def cast_place(w, chip, layers, after):
    _, r, c = w.shape
    n_sel = len(layers)
    tr = _rows_tile(r, c, budget=2 * 1024 * 1024 // n_sel)

    def body(chip_ref, *refs):
        for j in range(n_sel):
            refs[n_sel + 1 + j][...] = refs[j][...].astype(BF16)

    layer_spec = lambda l: pl.BlockSpec((None, tr, c), lambda i, ch: (l, i, 0))
    return list(pl.pallas_call(
        body, name="cast_place",
        grid_spec=pltpu.PrefetchScalarGridSpec(
            num_scalar_prefetch=1, grid=(r // tr,),
            in_specs=[layer_spec(l) for l in layers] + [pl.BlockSpec(memory_space=pl.ANY)],
            out_specs=[pl.BlockSpec((None, tr, c), lambda i, ch: (ch[0], i, 0))] * n_sel),
        out_shape=[jax.ShapeDtypeStruct((N_CHIPS, r, c), BF16)] * n_sel,
        compiler_params=_params(("arbitrary",)),
    )(chip, *([w] * n_sel), after))


def adamw(w, g, m, v, copy_g=False):
    shape = w.shape
    w2, g2, m2, v2 = (_as_rows(t) for t in (w, g, m, v))
    r, c = w2.shape
    tr = _rows_tile(r, c, budget=1024 * 1024)
    c1 = 1.0 - ADAM_B1 ** ADAM_STEP
    c2 = 1.0 - ADAM_B2 ** ADAM_STEP

    def body(w_ref, g_ref, m_ref, v_ref, d_ref, mo_ref, vo_ref, *go_ref):
        gv = g_ref[...]
        if copy_g:
            go_ref[0][...] = gv
        mn = ADAM_B1 * m_ref[...] + (1.0 - ADAM_B1) * gv
        vn = ADAM_B2 * v_ref[...] + (1.0 - ADAM_B2) * (gv * gv)
        mo_ref[...] = mn
        vo_ref[...] = vn
        d_ref[...] = -ADAM_LR * ((mn / c1) / (jnp.sqrt(vn / c2) + ADAM_EPS) + ADAM_WD * w_ref[...])

    spec = pl.BlockSpec((tr, c), lambda i: (i, 0))
    n_out = 4 if copy_g else 3
    outs = pl.pallas_call(
        body, name="adamw", grid=(r // tr,), in_specs=[spec] * 4, out_specs=[spec] * n_out,
        out_shape=[jax.ShapeDtypeStruct((r, c), F32)] * n_out, compiler_params=_params(("arbitrary",)),
    )(w2, g2, m2, v2)
    g_out = outs[3] if copy_g else g2
    return tuple(o.reshape(shape) for o in (g_out,) + tuple(outs[:3]))


def sum_devices(a):
    n, r, c = a.shape
    tr = _rows_tile(r, c, budget=512 * 1024)

    def body(a_ref, o_ref):
        acc = a_ref[0]
        for j in range(1, n):
            acc = acc + a_ref[j]
        o_ref[...] = acc

    return pl.pallas_call(
        body, name="sum_devices", grid=(r // tr,),
        in_specs=[pl.BlockSpec((n, tr, c), lambda i: (0, i, 0))], out_specs=pl.BlockSpec((tr, c), lambda i: (i, 0)),
        out_shape=jax.ShapeDtypeStruct((r, c), F32), compiler_params=_params(("arbitrary",)),
    )(a)


def _split_axis(r, c):
    if (r // 2) % BF16_ROWS == 0 and r % 2 == 0:
        return 0
    assert c % (2 * LANES) == 0, (r, c)
    return 1


def _half_shape(r, c):
    return (r // 2, c) if _split_axis(r, c) == 0 else (r, c // 2)


def _half_at(ref, lead, which):
    r, c = ref.shape[-2:]
    if _split_axis(r, c) == 0:
        return ref.at[(*lead, pl.ds(which * (r // 2), r // 2), slice(None))]
    return ref.at[(*lead, slice(None), pl.ds(which * (c // 2), c // 2))]


def _half_spec(r, c, lead_block, imap):
    hr, hc = _half_shape(r, c)
    if _split_axis(r, c) == 0:
        return pl.BlockSpec((*lead_block, hr, hc), lambda *a: (*imap(*a)[0], imap(*a)[1], 0))
    return pl.BlockSpec((*lead_block, hr, hc), lambda *a: (*imap(*a)[0], 0, imap(*a)[1]))


def pair_add(gs, ras, sel):
    n = len(gs)
    n_sl = gs[0].shape[0]
    halves = [_half_shape(*g.shape[1:]) for g in gs]

    def body(s_ref, *refs):
        g_refs, ra_refs, pb_refs, own_refs = (refs[i * n:(i + 1) * n] for i in range(4))
        k = pl.program_id(0)
        for t in range(n):
            p = g_refs[t][...] + ra_refs[t][...]
            pb_refs[t][...] = p.astype(BF16)

            @pl.when(k == s_ref[1])
            def _(p=p, own=own_refs[t]):
                own[...] = p

    slot = lambda hs: pl.BlockSpec((None,) + hs, lambda k, sr: (k, 0, 0))
    outs = pl.pallas_call(
        body, name="pair_add",
        grid_spec=pltpu.PrefetchScalarGridSpec(
            num_scalar_prefetch=1, grid=(n_sl,),
            in_specs=[_half_spec(*g.shape[1:], (None,), lambda k, sr: ((k,), sr[0])) for g in gs]
            + [slot(hs) for hs in halves],
            out_specs=[slot(hs) for hs in halves] + [pl.BlockSpec(hs, lambda k, sr: (0, 0)) for hs in halves]),
        out_shape=[jax.ShapeDtypeStruct((n_sl,) + hs, BF16) for hs in halves]
        + [jax.ShapeDtypeStruct(hs, F32) for hs in halves],
        compiler_params=_params(("arbitrary",)),
    )(sel, *gs, *ras)
    return list(outs[:n]), list(outs[n:])


def chip_sum(owns, rbs, sel, shapes, accs):
    n = len(owns)
    fresh = accs[0] is None

    def body(s_ref, *refs):
        own_refs, rb_refs, o_refs = refs[:n], refs[n:2 * n], refs[-n:]
        for t in range(n):
            acc_v = own_refs[t][...]
            for j in range(N_CHIPS - 1):
                acc_v = acc_v + rb_refs[t][j].astype(F32)
            o_refs[t][...] = acc_v

    in_specs = ([pl.BlockSpec(o.shape, lambda i, sr: (0, 0)) for o in owns]
                + [pl.BlockSpec(rb.shape, lambda i, sr: (0, 0, 0)) for rb in rbs])
    args = [sel, *owns, *rbs]
    aliases = {}
    if not fresh:
        in_specs += [pl.BlockSpec(memory_space=pl.ANY)] * n
        args += list(accs)
        aliases = {1 + 2 * n + t: t for t in range(n)}
    return list(pl.pallas_call(
        body, name="chip_sum",
        grid_spec=pltpu.PrefetchScalarGridSpec(
            num_scalar_prefetch=1, grid=(1,), in_specs=in_specs,
            out_specs=[_half_spec(*shp[1:], (None,), lambda i, sr: ((sr[2],), sr[0])) for shp in shapes]),
        out_shape=[jax.ShapeDtypeStruct(shp, F32) for shp in shapes],
        input_output_aliases=aliases,
        compiler_params=_params(("arbitrary",)),
    )(*args))


def _me():
    return lax.axis_index("x"), lax.axis_index("y"), lax.axis_index("c")


def _flip(v, bit):
    return 1 - v if bit else v


def exchange8(xs, bcast):
    blk = xs.shape if bcast else xs.shape[1:]

    def body(x_ref, o_ref, send_sems, recv_sems, loc_sem):
        mx, my, mc = _me()
        me = 4 * mx + 2 * my + mc
        src = (lambda j: x_ref) if bcast else (lambda j: x_ref.at[j])
        loc = pltpu.make_async_copy(src(me), o_ref.at[me], loc_sem)
        loc.start()
        copies = []
        for o in range(1, N_DEV):
            px, py, pc = _flip(mx, o & 4), _flip(my, o & 2), _flip(mc, o & 1)
            cp = pltpu.make_async_remote_copy(
                src_ref=src(4 * px + 2 * py + pc), dst_ref=o_ref.at[me],
                send_sem=send_sems.at[o - 1], recv_sem=recv_sems.at[o - 1],
                device_id=(px, py, pc), device_id_type=MESH)
            cp.start()
            copies.append(cp)
        for cp in copies:
            cp.wait()
        loc.wait()

    return pl.pallas_call(
        body, name="exchange8_gather" if bcast else "exchange8_a2a",
        in_specs=[pl.BlockSpec(memory_space=pltpu.VMEM)], out_specs=pl.BlockSpec(memory_space=pltpu.VMEM),
        out_shape=jax.ShapeDtypeStruct((N_DEV,) + tuple(blk), xs.dtype),
        scratch_shapes=[pltpu.SemaphoreType.DMA((N_DEV - 1,)), pltpu.SemaphoreType.DMA((N_DEV - 1,)), pltpu.SemaphoreType.DMA],
        compiler_params=_params(),
    )(xs)


HBM = pl.BlockSpec(memory_space=pltpu.HBM)
SEM = pl.BlockSpec(memory_space=pltpu.SEMAPHORE)
EFFECT = pltpu.SideEffectType.DATAFLOW_SIDE_EFFECTING


def _hbm(a):
    return pltpu.with_memory_space_constraint(a, pltpu.HBM)


def _ici_copy(land, o, send_sem, recv_sem, sending):
    mx, my, mc = _me()
    px, py = _flip(mx, o & 2), _flip(my, o & 1)
    mine = _half_at(land, (2 * mx + my,), mc)
    return pltpu.make_async_remote_copy(
        src_ref=mine, dst_ref=mine if sending else _half_at(land, (2 * px + py,), mc),
        send_sem=send_sem, recv_sem=recv_sem, device_id=(px, py, mc), device_id_type=MESH)


N_PEERS = N_CHIPS - 1
DMA_SEM = pltpu.SemaphoreType.DMA(())


def gather_start(lands, groups, after, tag):
    n_layers, n = len(lands), len(lands[0])
    flat = [a for layer in lands for a in layer]
    n_in = n * n_layers
    n_grp = len(groups)
    n_sem = 2 * n_layers * n_grp * N_PEERS
    first = lambda l, g, recv: ((l * n_grp + g) * 2 + recv) * N_PEERS

    def body(*refs):
        land = refs[:n_in]
        sems = refs[n_in + 1:n_in + 1 + n_sem]
        token = refs[-1]
        for l in range(n_layers):
            for g, members in enumerate(groups):
                for t in members:
                    for o in range(1, N_CHIPS):
                        _ici_copy(land[l * n + t], o, sems[first(l, g, 0) + o - 1], sems[first(l, g, 1) + o - 1],
                                  True).start()
        token[...] = jnp.zeros_like(token)

    outs = pl.pallas_call(
        body, name=f"gather_start_{tag}",
        in_specs=[HBM] * n_in + [pl.BlockSpec(memory_space=pl.ANY)],
        out_specs=[SEM] * n_sem + [HBM] * n_in + [pl.BlockSpec(memory_space=pltpu.VMEM)],
        out_shape=[DMA_SEM] * n_sem + [pltpu.HBM(a.shape, a.dtype) for a in flat]
        + [jax.ShapeDtypeStruct((8, LANES), F32)],
        input_output_aliases={i: i + n_sem for i in range(n_in)},
        compiler_params=pltpu.CompilerParams(has_side_effects=EFFECT),
    )(*[_hbm(a) for a in flat], after)
    sems = [[(list(outs[first(l, g, 0):first(l, g, 0) + N_PEERS]), list(outs[first(l, g, 1):first(l, g, 1) + N_PEERS]))
             for g in range(n_grp)] for l in range(n_layers)]
    lands_thru = [list(outs[n_sem + l * n:n_sem + (l + 1) * n]) for l in range(n_layers)]
    return sems, lands_thru, outs[-1]


def gather_wait(tag, sems, lands, after):
    n = len(lands)
    send_sems, recv_sems = sems

    def body(*refs):
        land = refs[:n]
        send_r = refs[n:n + N_PEERS]
        recv_r = refs[n + N_PEERS:n + 2 * N_PEERS]
        for t in range(n):
            for o in range(1, N_CHIPS):
                _ici_copy(land[t], o, send_r[o - 1], recv_r[o - 1], True).wait_send()
                _ici_copy(land[t], o, send_r[o - 1], recv_r[o - 1], False).wait_recv()

    return list(pl.pallas_call(
        body, name=f"gather_wait_{tag}",
        in_specs=[HBM] * n + [SEM] * (2 * N_PEERS) + [pl.BlockSpec(memory_space=pl.ANY)],
        out_specs=[HBM] * n,
        out_shape=[pltpu.HBM(a.shape, a.dtype) for a in lands],
        input_output_aliases={i: i for i in range(n)},
        compiler_params=pltpu.CompilerParams(has_side_effects=EFFECT),
    )(*lands, *send_sems, *recv_sems, after))


def gather_forward(lands):
    n = len(lands)

    def body(*refs):
        dst = refs[n:2 * n]
        send_sems, recv_sems = refs[2 * n:]
        mx, my, mc = _me()
        fwds = []
        for t in range(n):
            for o in range(1, N_CHIPS):
                slot = 2 * _flip(mx, o & 2) + _flip(my, o & 1)
                mine = _half_at(dst[t], (slot,), mc)
                theirs = _half_at(dst[t], (slot,), 1 - mc)
                cp = pltpu.make_async_remote_copy(
                    src_ref=mine, dst_ref=mine, send_sem=send_sems.at[t, o - 1], recv_sem=recv_sems.at[t, o - 1],
                    device_id=(mx, my, 1 - mc), device_id_type=MESH)
                cp.start()
                fwds.append((cp, pltpu.make_async_remote_copy(
                    src_ref=theirs, dst_ref=theirs, send_sem=send_sems.at[t, o - 1], recv_sem=recv_sems.at[t, o - 1],
                    device_id=(mx, my, 1 - mc), device_id_type=MESH)))
        for cp, arrival in fwds:
            cp.wait_send()
            arrival.wait_recv()

    any_spec = pl.BlockSpec(memory_space=pl.ANY)
    return list(pl.pallas_call(
        body, name="gather_forward",
        in_specs=[any_spec] * n, out_specs=[any_spec] * n,
        out_shape=[jax.ShapeDtypeStruct(a.shape, a.dtype) for a in lands],
        input_output_aliases={t: t for t in range(n)},
        scratch_shapes=[pltpu.SemaphoreType.DMA((n, N_CHIPS - 1)), pltpu.SemaphoreType.DMA((n, N_CHIPS - 1))],
        compiler_params=_params(),
    )(*lands))


def _scatter_copy(src, land, o, send_sem, recv_sem):
    mx, my, mc = _me()
    px, py = _flip(mx, o & 2), _flip(my, o & 1)
    return pltpu.make_async_remote_copy(
        src_ref=src.at[2 * px + py], dst_ref=land.at[o - 1],
        send_sem=send_sem, recv_sem=recv_sem, device_id=(px, py, mc), device_id_type=MESH)


def scatter_start(pbs, tag, after):
    n = len(pbs)
    lands = [lax.empty((N_CHIPS - 1,) + p.shape[1:], p.dtype) for p in pbs]

    def body(*refs):
        src = refs[:n]
        land = refs[n:2 * n]
        send_sems = refs[2 * n + 1:2 * n + 1 + N_PEERS]
        recv_sems = refs[2 * n + 1 + N_PEERS:2 * n + 1 + 2 * N_PEERS]
        token = refs[-1]
        for t in range(n):
            for o in range(1, N_CHIPS):
                _scatter_copy(src[t], land[t], o, send_sems[o - 1], recv_sems[o - 1]).start()
        token[...] = jnp.zeros_like(token)

    n_sem = 2 * N_PEERS
    arrs = list(pbs) + lands
    outs = pl.pallas_call(
        body, name=f"scatter_start_{tag}",
        in_specs=[HBM] * (2 * n) + [pl.BlockSpec(memory_space=pl.ANY)],
        out_specs=[SEM] * n_sem + [HBM] * (2 * n) + [pl.BlockSpec(memory_space=pltpu.VMEM)],
        out_shape=[DMA_SEM] * n_sem + [pltpu.HBM(a.shape, a.dtype) for a in arrs]
        + [jax.ShapeDtypeStruct((8, LANES), F32)],
        input_output_aliases={i: i + n_sem for i in range(2 * n)},
        compiler_params=pltpu.CompilerParams(has_side_effects=EFFECT),
    )(*[_hbm(a) for a in arrs], after)
    return (list(outs[:N_PEERS]), list(outs[N_PEERS:n_sem]), list(outs[n_sem:n_sem + n]),
            list(outs[n_sem + n:n_sem + 2 * n]), outs[-1])


def scatter_wait(tag, send_sems, recv_sems, pbs, lands, after):
    n = len(pbs)

    def body(*refs):
        src = refs[:n]
        land = refs[n:2 * n]
        send_r = refs[2 * n:2 * n + N_PEERS]
        recv_r = refs[2 * n + N_PEERS:2 * n + 2 * N_PEERS]
        for t in range(n):
            for o in range(1, N_CHIPS):
                cp = _scatter_copy(src[t], land[t], o, send_r[o - 1], recv_r[o - 1])
                cp.wait_send()
                cp.wait_recv()

    arrs = list(pbs) + list(lands)
    outs = pl.pallas_call(
        body, name=f"scatter_wait_{tag}",
        in_specs=[HBM] * (2 * n) + [SEM] * (2 * N_PEERS) + [pl.BlockSpec(memory_space=pl.ANY)],
        out_specs=[HBM] * (2 * n),
        out_shape=[pltpu.HBM(a.shape, a.dtype) for a in arrs],
        input_output_aliases={i: i for i in range(2 * n)},
        compiler_params=pltpu.CompilerParams(has_side_effects=EFFECT),
    )(*arrs, *send_sems, *recv_sems, after)
    return list(outs[n:])


def _pair_copy(src, land, send_sem, recv_sem):
    mx, my, mc = _me()
    return pltpu.make_async_remote_copy(
        src_ref=_half_at(src, (slice(None),), 1 - mc), dst_ref=land, send_sem=send_sem, recv_sem=recv_sem,
        device_id=(mx, my, 1 - mc), device_id_type=MESH)


def pair_start(gs, tag, after):
    n = len(gs)
    lands = [lax.empty((g.shape[0],) + _half_shape(*g.shape[1:]), g.dtype) for g in gs]

    def body(*refs):
        src = refs[:n]
        land = refs[n:2 * n]
        send_sem, recv_sem = refs[2 * n + 1], refs[2 * n + 2]
        token = refs[-1]
        for t in range(n):
            _pair_copy(src[t], land[t], send_sem, recv_sem).start()
        token[...] = jnp.zeros_like(token)

    arrs = list(gs) + lands
    outs = pl.pallas_call(
        body, name=f"pair_start_{tag}",
        in_specs=[HBM] * (2 * n) + [pl.BlockSpec(memory_space=pl.ANY)],
        out_specs=[SEM, SEM] + [HBM] * (2 * n) + [pl.BlockSpec(memory_space=pltpu.VMEM)],
        out_shape=[DMA_SEM, DMA_SEM] + [pltpu.HBM(a.shape, a.dtype) for a in arrs] + [jax.ShapeDtypeStruct((8, LANES), F32)],
        input_output_aliases={i: i + 2 for i in range(2 * n)},
        compiler_params=pltpu.CompilerParams(has_side_effects=EFFECT),
    )(*[_hbm(a) for a in arrs], after)
    return outs[0], outs[1], list(outs[2:2 + n]), list(outs[2 + n:2 + 2 * n]), outs[-1]


def pair_wait(tag, send_sem, recv_sem, gs, lands, after):
    n = len(gs)

    def body(*refs):
        src = refs[:n]
        land = refs[n:2 * n]
        send_r, recv_r = refs[2 * n], refs[2 * n + 1]
        for t in range(n):
            cp = _pair_copy(src[t], land[t], send_r, recv_r)
            cp.wait_send()
            cp.wait_recv()

    arrs = list(gs) + list(lands)
    outs = pl.pallas_call(
        body, name=f"pair_wait_{tag}",
        in_specs=[HBM] * (2 * n) + [SEM, SEM, pl.BlockSpec(memory_space=pl.ANY)],
        out_specs=[HBM] * (2 * n),
        out_shape=[pltpu.HBM(a.shape, a.dtype) for a in arrs],
        input_output_aliases={i: i for i in range(2 * n)},
        compiler_params=pltpu.CompilerParams(has_side_effects=EFFECT),
    )(*arrs, send_sem, recv_sem, after)
    return list(outs[:n]), list(outs[n:])


def _gather8_copy(x, land, o, send_sem, recv_sem, sending):
    mx, my, mc = _me()
    px, py, pc = _flip(mx, o & 4), _flip(my, o & 2), _flip(mc, o & 1)
    slot = 4 * mx + 2 * my + mc if sending else 4 * px + 2 * py + pc
    return pltpu.make_async_remote_copy(
        src_ref=x, dst_ref=land.at[slot], send_sem=send_sem, recv_sem=recv_sem,
        device_id=(px, py, pc), device_id_type=MESH)


def gather8_start(x, land, after):
    n_peer = N_DEV - 1

    def body(x_ref, land_ref, after_ref, *rest):
        send_sems, recv_sems = rest[:n_peer], rest[n_peer:2 * n_peer]
        token = rest[-1]
        for o in range(1, N_DEV):
            _gather8_copy(x_ref, land_ref, o, send_sems[o - 1], recv_sems[o - 1], True).start()
        token[...] = jnp.zeros_like(token)

    outs = pl.pallas_call(
        body, name="gather8_start",
        in_specs=[HBM, HBM, pl.BlockSpec(memory_space=pl.ANY)],
        out_specs=[SEM] * (2 * n_peer) + [HBM, HBM, pl.BlockSpec(memory_space=pltpu.VMEM)],
        out_shape=[DMA_SEM] * (2 * n_peer) + [pltpu.HBM(x.shape, x.dtype), pltpu.HBM(land.shape, land.dtype),
                                              jax.ShapeDtypeStruct((8, LANES), F32)],
        input_output_aliases={0: 2 * n_peer, 1: 2 * n_peer + 1},
        compiler_params=pltpu.CompilerParams(has_side_effects=EFFECT),
    )(_hbm(x), _hbm(land), after)
    return list(outs[:n_peer]), list(outs[n_peer:2 * n_peer]), outs[2 * n_peer], outs[2 * n_peer + 1], outs[-1]


def gather8_wait(send_sems, recv_sems, x, land, after):
    n_peer = N_DEV - 1

    def body(x_ref, land_ref, *rest):
        send_r, recv_r = rest[:n_peer], rest[n_peer:2 * n_peer]
        for o in range(1, N_DEV):
            _gather8_copy(x_ref, land_ref, o, send_r[o - 1], recv_r[o - 1], True).wait_send()
            _gather8_copy(x_ref, land_ref, o, send_r[o - 1], recv_r[o - 1], False).wait_recv()

    return pl.pallas_call(
        body, name="gather8_wait",
        in_specs=[HBM, HBM] + [SEM] * (2 * n_peer) + [pl.BlockSpec(memory_space=pl.ANY)],
        out_specs=[HBM, HBM],
        out_shape=[pltpu.HBM(x.shape, x.dtype), pltpu.HBM(land.shape, land.dtype)],
        input_output_aliases={0: 0, 1: 1},
        compiler_params=pltpu.CompilerParams(has_side_effects=EFFECT),
    )(x, land, *send_sems, *recv_sems, after)[1]


def pair_fill_halves(fs):
    n = len(fs)

    def body(*refs):
        dst = refs[n:2 * n]
        send_sems, recv_sems = refs[2 * n:]
        mx, my, mc = _me()
        copies = []
        for t in range(n):
            mine = _half_at(dst[t], (slice(None),), mc)
            theirs = _half_at(dst[t], (slice(None),), 1 - mc)
            cp = pltpu.make_async_remote_copy(
                src_ref=mine, dst_ref=mine, send_sem=send_sems.at[t], recv_sem=recv_sems.at[t],
                device_id=(mx, my, 1 - mc), device_id_type=MESH)
            cp.start()
            copies.append((cp, pltpu.make_async_remote_copy(
                src_ref=theirs, dst_ref=theirs, send_sem=send_sems.at[t], recv_sem=recv_sems.at[t],
                device_id=(mx, my, 1 - mc), device_id_type=MESH)))
        for cp, arrival in copies:
            cp.wait_send()
            arrival.wait_recv()

    any_spec = pl.BlockSpec(memory_space=pl.ANY)
    return pl.pallas_call(
        body, name="pair_fill_halves",
        in_specs=[any_spec] * n, out_specs=[any_spec] * n,
        out_shape=[jax.ShapeDtypeStruct(f.shape, f.dtype) for f in fs],
        input_output_aliases={t: t for t in range(n)},
        scratch_shapes=[pltpu.SemaphoreType.DMA((n,)), pltpu.SemaphoreType.DMA((n,))],
        compiler_params=_params(),
    )(*fs)


def _pack_rows(parts, d):
    rows, spans = [], []
    at = 0
    for p in parts:
        flat = p.reshape(-1)
        n_rows = -(-flat.shape[0] // (8 * d)) * 8
        flat = jnp.pad(flat, (0, n_rows * d - flat.shape[0]))
        rows.append(flat.reshape(n_rows, d))
        spans.append((at, p.shape))
        at += n_rows
    return jnp.concatenate(rows, axis=0), spans


def _unpack_rows(packed, spans):
    out = []
    for at, shape in spans:
        n = math.prod(shape)
        d = packed.shape[1]
        n_rows = -(-n // d)
        out.append(packed[at:at + n_rows].reshape(-1)[:n].reshape(shape))
    return out


def _rotate_half_matrix():
    half = QK_ROPE // 2
    idx = jnp.arange(QK_ROPE)
    src = jnp.where(idx < half, idx + half, idx - half)
    sign = jnp.where(idx < half, -1.0, 1.0)
    return (jnp.zeros((QK_ROPE, QK_ROPE), F32).at[src, idx].set(sign)).astype(BF16)


def kernel(x, c, positions, ada_w, ada_b, ffn1_norm, ffn1_w_gate, ffn1_w_up, ffn1_w_down, mix_norm, w_in, pool_w, pool_scale, q_a_norm, w_q_b, kv_a_norm, w_kv_b, w_out, ffn2_norm, ffn2_w_gate, ffn2_w_up, ffn2_w_down, final_norm, loss_target, m_ada_w, m_ada_b, m_ffn1_norm, m_ffn1_w_gate, m_ffn1_w_up, m_ffn1_w_down, m_mix_norm, m_w_in, m_pool_w, m_pool_scale, m_q_a_norm, m_w_q_b, m_kv_a_norm, m_w_kv_b, m_w_out, m_ffn2_norm, m_ffn2_w_gate, m_ffn2_w_up, m_ffn2_w_down, m_final_norm, v_ada_w, v_ada_b, v_ffn1_norm, v_ffn1_w_gate, v_ffn1_w_up, v_ffn1_w_down, v_mix_norm, v_w_in, v_pool_w, v_pool_scale, v_q_a_norm, v_w_q_b, v_kv_a_norm, v_w_kv_b, v_w_out, v_ffn2_norm, v_ffn2_w_gate, v_ffn2_w_up, v_ffn2_w_down, v_final_norm):
    mx, my, mc = _me()
    chip = 2 * mx + my
    half = jnp.reshape(mc, (1,)).astype(jnp.int32)
    chip1 = jnp.reshape(chip, (1,)).astype(jnp.int32)
    n_layers, d, ada_cols = ada_w.shape
    xt = x[0]
    tgt = loss_target[0]

    inv_freq = 1.0 / (ROPE_THETA ** (jnp.arange(0, QK_ROPE, 2, dtype=F32) / QK_ROPE))
    ang = positions[0].astype(F32)[:, None] * inv_freq
    ang = jnp.concatenate([ang, ang], axis=-1)
    cos, sin = jnp.cos(ang), jnp.sin(ang)
    rot = _rotate_half_matrix()
    rot_t = rot.T

    c_all = exchange8(c, True).reshape(N_DEV, d)
    c16 = jnp.pad(c_all, ((0, 8), (0, 0)))
    ada_b_loc = lax.dynamic_slice_in_dim(ada_b, chip * ada_cols, ada_cols, axis=1).reshape(n_layers, 1, ada_cols)
    mod_part = ada_fwd(c16, ada_w, ada_b_loc)[:, :N_DEV]
    mod_got = exchange8(jnp.transpose(mod_part, (1, 0, 2)), False)
    mod = jnp.transpose(mod_got.reshape(N_CHIPS, 2, n_layers, ada_cols)[:, 0], (1, 0, 2))
    mod = mod.reshape(n_layers, 9, 1, d)

    tr = lambda a: jnp.transpose(a, (0, 2, 1))
    local = [tr(ffn1_w_gate), tr(ffn1_w_up), ffn1_w_down, tr(w_in), tr(w_q_b), w_kv_b, w_out,
             tr(ffn2_w_gate), tr(ffn2_w_up), ffn2_w_down]
    ffn1_pos, rest_pos = (0, 1, 2), tuple(range(3, len(local)))
    groups = (ffn1_pos, rest_pos)
    placed = [cast_place(w, chip1, (0,), mod) for w in local]
    g_sems, lands_fly, g_token = gather_start([[p[0] for p in placed]], groups, mod, "first")
    if n_layers > 1:
        later = tuple(range(1, n_layers))
        placed = [cast_place(w, chip1, later, g_token) for w in local]
        more_sems, more_fly, g_token = gather_start(
            [[p[j] for p in placed] for j in range(len(later))], groups, g_token, "rest")
        g_sems, lands_fly = g_sems + more_sems, lands_fly + more_fly
    gathered = []

    row = lambda a, l: a[l].reshape(1, -1)
    saved = []
    for l in range(n_layers):
        g1, u1, d1 = gather_forward(gather_wait(
            f"{l}a", g_sems[l][0], [lands_fly[l][t] for t in ffn1_pos], xt if l else g_token))
        sv = dict(x0=xt)
        xt, sv["h1"], sv["a1"], sv["sl1"], sv["dsu1"], sv["y1"] = ffn_fwd(
            xt, row(ffn1_norm, l), mod[l, 0], mod[l, 1], mod[l, 2], g1, u1, d1)
        sv["x1"] = xt
        win, wq, wkv, wout, g2, u2, d2 = gather_forward(gather_wait(
            f"{l}b", g_sems[l][1], [lands_fly[l][t] for t in rest_pos], xt))
        gathered.append([g1, u1, d1, win, wq, wkv, wout, g2, u2, d2])
        win = win.reshape(-1, d)
        sv["h2"], u, cq, ckv, kr = mix_in_fwd(xt, row(mix_norm, l), mod[l, 3], mod[l, 4], win)
        sv["cq"], sv["ckv"] = cq, ckv
        yp, sv["diff"] = pool_fwd(u, pool_w[l], row(pool_scale, l))
        qh, kh, vh, sv["ql"], sv["kvl"] = mla_qkv_fwd(
            cq, ckv, kr, row(q_a_norm, l), row(kv_a_norm, l), wq, wkv, cos, sin, rot)
        sv["qkv"] = (qh, kh, vh)
        om = attn_fwd(qh, kh, vh)
        xt, sv["ycat"], sv["y2"] = out_proj_fwd(yp, om, wout, xt, mod[l, 5])
        sv["x2"] = xt
        xt, sv["h3"], sv["a3"], sv["sl3"], sv["dsu3"], sv["y3"] = ffn_fwd(
            xt, row(ffn2_norm, l), mod[l, 6], mod[l, 7], mod[l, 8], g2, u2, d2)
        saved.append(sv)

    loss_vec, dx, d_final_norm = final_loss(xt, final_norm.reshape(1, d), tgt)
    loss = lax.psum(loss_vec[0, 0], ("x", "y", "c"))

    none = [None] * n_layers
    dmods, dnorm1, dnorm2, dnorm3 = list(none), list(none), list(none), list(none)
    dpw, dps, dqan_l, dkvan_l = list(none), list(none), list(none), list(none)
    reduced = [None] * len(local)
    stages = []
    sel_of = lambda l: jnp.stack([mc, chip, jnp.asarray(l, mc.dtype)]).astype(jnp.int32)

    def to_chips(job, after_wait, after_start):
        send, recv, g_fly, lands_p = job.pop("pair")
        g_fly, got = pair_wait(job["tag"], send, recv, g_fly, lands_p, after_wait)
        pbs, job["owns"] = pair_add(g_fly, got, sel_of(job["l"]))
        job["scatter"] = scatter_start(pbs, job["tag"], after_start)
        return job["scatter"][4][0, 0]

    def finish(job, after):
        s_send, s_recv, pbs_fly, lands_j, _ = job.pop("scatter")
        parts = scatter_wait(job["tag"], s_send, s_recv, pbs_fly, lands_j, after)
        sums = chip_sum(job["owns"], parts, sel_of(job["l"]), [(n_layers,) + shp for shp in job["shapes"]],
                        [reduced[t] for t in job["pos"]])
        for t, total_t in zip(job["pos"], sums):
            reduced[t] = total_t

    def checkpoint(tag, l, positions, grads_, done, before_scatter=None):
        send, recv, g_fly, lands_p, tok = pair_start(grads_, tag, done)
        order = tok[0, 0]
        if stages:
            order = order + to_chips(stages[-1], done, done if before_scatter is None else before_scatter)
        if len(stages) >= 3:
            finish(stages[-3], done)
        stages.append(dict(tag=tag, l=l, pos=positions, shapes=[g.shape[1:] for g in grads_],
                           pair=(send, recv, g_fly, lands_p)))
        return order

    order = None

    for l in reversed(range(n_layers)):
        sv = saved[l]
        g1, u1, d1, win, wq, wkv, wout, g2, u2, d2 = gathered[l]
        win = win.reshape(-1, d)
        gt3 = mod[l, 8] if order is None else mod[l, 8] + order
        dy, dgt, dup = ffn_bwd_act(dx, sv["sl3"], sv["dsu3"], gt3, d2)
        dx, dvec3 = ffn_bwd_in(dx, sv["x2"], sv["y3"], dgt, dup, row(ffn2_norm, l), mod[l, 7], g2, u2)
        g_g2, g_u2, g_d2 = tn_mm(dgt, sv["h3"][None]), tn_mm(dup, sv["h3"][None]), tn_mm(sv["a3"], dy[None])
        dy2, dyp, dom, dg2 = out_proj_bwd(dx, sv["y2"], mod[l, 5], wout)
        g_wout = tn_mm(sv["ycat"], dy2[None])
        qh, kh, vh = sv["qkv"]
        dqh, dkh, dvh = attn_bwd(qh, kh, vh, dom)
        dcq, dckv, dkr_in, gq, gkv, dqan_l[l], dkvan_l[l] = mla_qkv_bwd(
            dqh, dkh, dvh, sv["cq"], sv["ckv"], row(q_a_norm, l), row(kv_a_norm, l), wq, wkv, cos, sin, rot_t)
        g_wq, g_wkv = tn_mm(gq, sv["ql"][None]), tn_mm(sv["kvl"][None], gkv)
        du, dpw[l], dps[l] = pool_bwd(dyp, sv["diff"], pool_w[l], row(pool_scale, l))
        dx, dz, dvec2 = mix_in_bwd(dx, du, dcq, dckv, dkr_in, sv["x1"], row(mix_norm, l), mod[l, 4], win)
        g_win = tn_mm(dz[None], sv["h2"][None]).reshape(N_CHIPS, -1, d)
        order = checkpoint(f"{l}a", l, rest_pos, [g_win, g_wq, g_wkv, g_wout, g_g2, g_u2, g_d2], dx)
        dy, dgt, dup = ffn_bwd_act(dx, sv["sl1"], sv["dsu1"], mod[l, 2] + order, d1)
        dx, dvec1 = ffn_bwd_in(dx, sv["x0"], sv["y1"], dgt, dup, row(ffn1_norm, l), mod[l, 1], g1, u1)
        g_g1, g_u1, g_d1 = tn_mm(dgt, sv["h1"][None]), tn_mm(dup, sv["h1"][None]), tn_mm(sv["a1"], dy[None])
        dmods[l] = jnp.concatenate([dvec1[0:3], dvec2[0:2], dg2, dvec3[0:3]], axis=0)
        dnorm1[l], dnorm2[l], dnorm3[l] = dvec1[3], dvec2[3], dvec3[3]
        if l == 0:
            small_parts = [jnp.stack(dmods), jnp.stack(dnorm1), jnp.stack(dnorm2), jnp.stack(dnorm3), d_final_norm,
                           jnp.stack(dps), jnp.stack(dqan_l), jnp.stack(dkvan_l), jnp.stack(dpw)]
            packed, spans = _pack_rows(small_parts, d)
            me = 4 * mx + 2 * my + mc
            small_land = lax.dynamic_update_index_in_dim(lax.empty((N_DEV,) + packed.shape, F32), packed, me, 0)
            small_fly = gather8_start(packed, small_land, dx)

        order = checkpoint(f"{l}b", l, ffn1_pos, [g_g1, g_u1, g_d1], dx, small_fly[4] if l == 0 else None)

    to_chips(stages[-1], dx, dx)
    gathered_small = gather8_wait(small_fly[0], small_fly[1], small_fly[2], small_fly[3], stages[-1]["scatter"][4])
    total = sum_devices(gathered_small)
    (g_ada_b, g_n1, g_n2, g_n3, g_fn, g_ps, g_qan, g_kvan, g_pw) = _unpack_rows(total, spans)
    dmod_all = gathered_small[:, :9 * n_layers].reshape(N_DEV, n_layers, 9 * d)
    dmod_loc = lax.dynamic_slice_in_dim(dmod_all, chip * ada_cols, ada_cols, axis=2)
    dmod16 = jnp.pad(jnp.transpose(dmod_loc, (1, 0, 2)), ((0, 0), (0, 8), (0, 0)))
    g_ada_w = ada_bwd(c16, dmod16)

    grads = [g_ada_w, g_ada_b, g_n1, None, None, None, g_n2, None, g_pw, g_ps, g_qan, None, g_kvan, None, None, g_n3,
             None, None, None, g_fn]
    weights = [ada_w, ada_b, ffn1_norm, ffn1_w_gate, ffn1_w_up, ffn1_w_down, mix_norm, w_in, pool_w, pool_scale,
               q_a_norm, w_q_b, kv_a_norm, w_kv_b, w_out, ffn2_norm, ffn2_w_gate, ffn2_w_up, ffn2_w_down, final_norm]
    ms = [m_ada_w, m_ada_b, m_ffn1_norm, m_ffn1_w_gate, m_ffn1_w_up, m_ffn1_w_down, m_mix_norm, m_w_in, m_pool_w,
          m_pool_scale, m_q_a_norm, m_w_q_b, m_kv_a_norm, m_w_kv_b, m_w_out, m_ffn2_norm, m_ffn2_w_gate, m_ffn2_w_up,
          m_ffn2_w_down, m_final_norm]
    vs = [v_ada_w, v_ada_b, v_ffn1_norm, v_ffn1_w_gate, v_ffn1_w_up, v_ffn1_w_down, v_mix_norm, v_w_in, v_pool_w,
          v_pool_scale, v_q_a_norm, v_w_q_b, v_kv_a_norm, v_w_kv_b, v_w_out, v_ffn2_norm, v_ffn2_w_gate, v_ffn2_w_up,
          v_ffn2_w_down, v_final_norm]
    transposed = (3, 4, 7, 11, 16, 17)
    outs = [None] * len(weights)
    for i, (w, g, m, v) in enumerate(zip(weights, grads, ms, vs)):
        if g is not None:
            outs[i] = adamw(w, g.reshape(w.shape), m, v)
    big = [i for i, g in enumerate(grads) if g is None]

    def update(positions):
        filled = pair_fill_halves([reduced[t] for t in positions])
        for t, g in zip(positions, filled):
            i = big[t]
            if i in transposed:
                outs[i] = tuple(tr(o) for o in adamw(tr(weights[i]), g, tr(ms[i]), tr(vs[i]), copy_g=True))
            else:
                outs[i] = adamw(weights[i], g, ms[i], vs[i], copy_g=True)

    finish(stages[-3], outs[0][1])
    finish(stages[-2], outs[0][1])
    update(rest_pos)
    finish(stages[-1], outs[big[rest_pos[-1]]][1])
    update(ffn1_pos)
    return (loss, dx.reshape(x.shape), *[t[0] for t in outs], *[t[1] for t in outs], *[t[2] for t in outs],
            *[t[3] for t in outs])
```

```python
import math

import jax
import jax.numpy as jnp
from jax import lax
from jax.experimental import pallas as pl
from jax.experimental.pallas import tpu as pltpu

F32 = jnp.float32
BF16 = jnp.bfloat16
MESH = pl.DeviceIdType.MESH

EPS = 1e-6
ROPE_THETA = 10000.0
N_HEADS = 4
QK_NOPE = 128
QK_ROPE = 64
V_HEAD = 128
POOL_WINDOWS = (2, 4, 8, 16)
POOL_GC = 128
POOL_WIDTH = POOL_GC * len(POOL_WINDOWS)
Q_LORA = 384
KV_LORA = 256
SOFTMAX_SCALE = 1.0 / math.sqrt(QK_NOPE + QK_ROPE)
N_CHIPS = 4
N_DEV = 8

ADAM_LR = 0.001
ADAM_B1 = 0.9
ADAM_B2 = 0.999
ADAM_EPS = 1e-08
ADAM_WD = 0.01
ADAM_STEP = 10

ROW_TILE = 512
ATT_TILE = 256
VMEM_LIMIT = 56 * 1024 * 1024
BF16_ROWS = 16
LANES = 128


def _params(sem=None, vmem=VMEM_LIMIT):
    return pltpu.CompilerParams(dimension_semantics=sem, vmem_limit_bytes=vmem)


def _dot(a, b):
    return jnp.dot(a, b, preferred_element_type=F32)


def _dot_nt(a, b):
    return lax.dot_general(a, b, (((1,), (1,)), ((), ())), preferred_element_type=F32)


def _dot_tn(a, b):
    return lax.dot_general(a, b, (((0,), (0,)), ((), ())), preferred_element_type=F32)


def _dot_exact(t, perm):
    t1 = t.astype(BF16)
    r1 = t - t1.astype(F32)
    t2 = r1.astype(BF16)
    t3 = (r1 - t2.astype(F32)).astype(BF16)
    return _dot(t1, perm) + _dot(t2, perm) + _dot(t3, perm)


def _sum0(a):
    return jnp.sum(a, axis=0, keepdims=True)


def _rms(xt):
    r = lax.rsqrt(jnp.mean(xt * xt, axis=-1, keepdims=True) + EPS)
    return xt * r, r


def _rms_bwd(dy, xt, g):
    xhat, r = _rms(xt)
    dxhat = dy * g
    dx = r * (dxhat - xhat * jnp.mean(dxhat * xhat, axis=-1, keepdims=True))
    return dx, _sum0(dy * xhat)


def _normmod_bwd(dh, xt, gn, sc):
    xhat, _ = _rms(xt)
    dn = dh * (1.0 + sc)
    dx, dgn = _rms_bwd(dn, xt, gn)
    return dx, _sum0(dh), _sum0(dh * (xhat * gn)), dgn


def _row_tile(s):
    return min(s, ROW_TILE)


def _full(shape):
    n = len(shape)
    return pl.BlockSpec(shape, lambda *_: (0,) * n)


def _resident(shape):
    n = len(shape)
    return pl.BlockSpec(shape, lambda *_: (0,) * n, pipeline_mode=pl.Buffered(1))


def ffn_fwd(x, gn, sh, sc, gt, wg, wu, wd):
    s, d = x.shape
    k_chunks, fs, _ = wg.shape
    tm = _row_tile(s)

    def body(x_ref, gn_ref, sh_ref, sc_ref, gt_ref, wg_ref, wu_ref, wd_ref,
             xo_ref, h_ref, a_ref, sl_ref, dsu_ref, y_ref):
        xt = x_ref[...]
        xhat, _ = _rms(xt)
        h = (xhat * gn_ref[...] * (1.0 + sc_ref[...]) + sh_ref[...]).astype(BF16)
        h_ref[...] = h
        y = jnp.zeros((tm, d), F32)
        for k in range(k_chunks):
            gate = _dot_nt(h, wg_ref[k])
            up = _dot_nt(h, wu_ref[k])
            sg = jax.nn.sigmoid(gate)
            sl = gate * sg
            a = (sl * up).astype(BF16)
            a_ref[k] = a
            sl_ref[k] = sl.astype(BF16)
            dsu_ref[k] = (up * (sg * (1.0 + gate * (1.0 - sg)))).astype(BF16)
            y += _dot(a, wd_ref[k])
        y_ref[...] = y.astype(BF16)
        xo_ref[...] = xt + 0.5 * gt_ref[...] * y

    row = pl.BlockSpec((tm, d), lambda i: (i, 0))
    vec = pl.BlockSpec((1, d), lambda i: (0, 0))
    act = pl.BlockSpec((k_chunks, tm, fs), lambda i: (0, i, 0))
    act_shape = jax.ShapeDtypeStruct((k_chunks, s, fs), BF16)
    return pl.pallas_call(
        body, name="ffn_fwd",
        grid=(s // tm,),
        in_specs=[row, vec, vec, vec, vec, _resident(wg.shape), _resident(wu.shape), _resident(wd.shape)],
        out_specs=[row, row, act, act, act, row],
        out_shape=[jax.ShapeDtypeStruct((s, d), F32), jax.ShapeDtypeStruct((s, d), BF16),
                   act_shape, act_shape, act_shape, jax.ShapeDtypeStruct((s, d), BF16)],
        compiler_params=_params(("arbitrary",)),
    )(x, gn, sh, sc, gt, wg, wu, wd)


def ffn_bwd_act(dxn, sl, dsu, gt, wd):
    s, d = dxn.shape
    k_chunks, fs, _ = wd.shape
    tm = _row_tile(s)

    def body(dxn_ref, sl_ref, dsu_ref, gt_ref, wd_ref, dy_ref, dgate_ref, dup_ref):
        dy = (0.5 * gt_ref[...] * dxn_ref[...]).astype(BF16)
        dy_ref[...] = dy
        for k in range(k_chunks):
            da = _dot_nt(dy, wd_ref[k])
            dgate_ref[k] = (da * dsu_ref[k].astype(F32)).astype(BF16)
            dup_ref[k] = (da * sl_ref[k].astype(F32)).astype(BF16)

    row = pl.BlockSpec((tm, d), lambda i: (i, 0))
    act = pl.BlockSpec((k_chunks, tm, fs), lambda i: (0, i, 0))
    act_shape = jax.ShapeDtypeStruct((k_chunks, s, fs), BF16)
    return pl.pallas_call(
        body, name="ffn_bwd_act",
        grid=(s // tm,),
        in_specs=[row, act, act, pl.BlockSpec((1, d), lambda i: (0, 0)), _resident(wd.shape)],
        out_specs=[row, act, act],
        out_shape=[jax.ShapeDtypeStruct((s, d), BF16), act_shape, act_shape],
        compiler_params=_params(("arbitrary",)),
    )(dxn, sl, dsu, gt, wd)


def ffn_bwd_in(dxn, x, y, dgate, dup, gn, sc, wg, wu):
    s, d = x.shape
    k_chunks, fs, _ = wg.shape
    tm = _row_tile(s)

    def body(dxn_ref, x_ref, y_ref, dgate_ref, dup_ref, gn_ref, sc_ref, wg_ref, wu_ref, dx_ref, dvec_ref):
        i = pl.program_id(0)

        @pl.when(i == 0)
        def _():
            dvec_ref[...] = jnp.zeros_like(dvec_ref)

        dh = jnp.zeros((tm, d), F32)
        for k in range(k_chunks):
            dh += _dot(dgate_ref[k], wg_ref[k]) + _dot(dup_ref[k], wu_ref[k])
        dxn_t = dxn_ref[...]
        dx, dsh, dsc, dgn = _normmod_bwd(dh, x_ref[...], gn_ref[...], sc_ref[...])
        dx_ref[...] = dx + dxn_t
        dvec_ref[0:1, :] += dsh
        dvec_ref[1:2, :] += dsc
        dvec_ref[2:3, :] += _sum0(0.5 * dxn_t * y_ref[...].astype(F32))
        dvec_ref[3:4, :] += dgn

    row = pl.BlockSpec((tm, d), lambda i: (i, 0))
    vec = pl.BlockSpec((1, d), lambda i: (0, 0))
    act = pl.BlockSpec((k_chunks, tm, fs), lambda i: (0, i, 0))
    return pl.pallas_call(
        body, name="ffn_bwd_in",
        grid=(s // tm,),
        in_specs=[row, row, row, act, act, vec, vec, _resident(wg.shape), _resident(wu.shape)],
        out_specs=[row, pl.BlockSpec((8, d), lambda i: (0, 0))],
        out_shape=[jax.ShapeDtypeStruct((s, d), F32), jax.ShapeDtypeStruct((8, d), F32)],
        compiler_params=_params(("arbitrary",)),
    )(dxn, x, y, dgate, dup, gn, sc, wg, wu)


def tn_mm(a, b):
    ga, s, m = a.shape
    gb, _, n = b.shape
    g = max(ga, gb)

    def body(a_ref, b_ref, o_ref):
        o_ref[...] = _dot_tn(a_ref[...], b_ref[...])

    a_spec = pl.BlockSpec((None, s, m), (lambda gi: (gi, 0, 0)) if ga > 1 else (lambda gi: (0, 0, 0)))
    b_spec = pl.BlockSpec((None, s, n), (lambda gi: (gi, 0, 0)) if gb > 1 else (lambda gi: (0, 0, 0)))
    return pl.pallas_call(
        body, name="tn_mm",
        grid=(g,), in_specs=[a_spec, b_spec], out_specs=pl.BlockSpec((None, m, n), lambda gi: (gi, 0, 0)),
        out_shape=jax.ShapeDtypeStruct((g, m, n), F32),
        compiler_params=_params(("arbitrary",)),
    )(a, b)


def mix_in_fwd(x, gn, sh, sc, w_in_t):
    s, d = x.shape
    tm = _row_tile(s)
    o1, o2, o3 = POOL_WIDTH, POOL_WIDTH + Q_LORA, POOL_WIDTH + Q_LORA + KV_LORA

    def body(x_ref, gn_ref, sh_ref, sc_ref, w_ref, h_ref, u_ref, cq_ref, ckv_ref, kr_ref):
        xhat, _ = _rms(x_ref[...])
        h = (xhat * gn_ref[...] * (1.0 + sc_ref[...]) + sh_ref[...]).astype(BF16)
        h_ref[...] = h
        z = _dot_nt(h, w_ref[0:o3, :])
        u_ref[...] = z[:, 0:o1]
        cq_ref[...] = z[:, o1:o2]
        ckv_ref[...] = z[:, o2:o3]
        kr_ref[...] = _dot_nt(h, w_ref[o3:, :])

    row = lambda w: pl.BlockSpec((tm, w), lambda i: (i, 0))
    vec = pl.BlockSpec((1, d), lambda i: (0, 0))
    return pl.pallas_call(
        body, name="mix_in_fwd",
        grid=(s // tm,),
        in_specs=[row(d), vec, vec, vec, _full(w_in_t.shape)],
        out_specs=[row(d), row(POOL_WIDTH), row(Q_LORA), row(KV_LORA), row(QK_ROPE)],
        out_shape=[jax.ShapeDtypeStruct((s, d), BF16), jax.ShapeDtypeStruct((s, POOL_WIDTH), F32),
                   jax.ShapeDtypeStruct((s, Q_LORA), F32), jax.ShapeDtypeStruct((s, KV_LORA), F32),
                   jax.ShapeDtypeStruct((s, QK_ROPE), F32)],
        compiler_params=_params(("arbitrary",)),
    )(x, gn, sh, sc, w_in_t)


def mix_in_bwd(dxn, du, dcq, dckv, dkr, x, gn, sc, w_in_t):
    s, d = x.shape
    tm = _row_tile(s)
    o1, o2, o3 = POOL_WIDTH, POOL_WIDTH + Q_LORA, POOL_WIDTH + Q_LORA + KV_LORA
    n_z = w_in_t.shape[0]

    def body(dxn_ref, du_ref, dcq_ref, dckv_ref, dkr_ref, x_ref, gn_ref, sc_ref, w_ref, dx_ref, dz_ref, dvec_ref):
        i = pl.program_id(0)

        @pl.when(i == 0)
        def _():
            dvec_ref[...] = jnp.zeros_like(dvec_ref)

        dub = du_ref[...].astype(BF16)
        dqb = dcq_ref[...].astype(BF16)
        dkb = dckv_ref[...].astype(BF16)
        drb = dkr_ref[...].astype(BF16)
        dz_ref[:, 0:o1] = dub
        dz_ref[:, o1:o2] = dqb
        dz_ref[:, o2:o3] = dkb
        dz_ref[:, o3:] = drb
        dh = (_dot(dub, w_ref[0:o1, :]) + _dot(dqb, w_ref[o1:o2, :]) + _dot(dkb, w_ref[o2:o3, :])
              + _dot(drb, w_ref[o3:, :]))
        dx, dsh, dsc, dgn = _normmod_bwd(dh, x_ref[...], gn_ref[...], sc_ref[...])
        dx_ref[...] = dx + dxn_ref[...]
        dvec_ref[0:1, :] += dsh
        dvec_ref[1:2, :] += dsc
        dvec_ref[3:4, :] += dgn

    row = lambda w: pl.BlockSpec((tm, w), lambda i: (i, 0))
    vec = pl.BlockSpec((1, d), lambda i: (0, 0))
    return pl.pallas_call(
        body, name="mix_in_bwd",
        grid=(s // tm,),
        in_specs=[row(d), row(POOL_WIDTH), row(Q_LORA), row(KV_LORA), row(QK_ROPE), row(d), vec, vec,
                  _full(w_in_t.shape)],
        out_specs=[row(d), row(n_z), pl.BlockSpec((8, d), lambda i: (0, 0))],
        out_shape=[jax.ShapeDtypeStruct((s, d), F32), jax.ShapeDtypeStruct((s, n_z), BF16),
                   jax.ShapeDtypeStruct((8, d), F32)],
        compiler_params=_params(("arbitrary",)),
    )(dxn, du, dcq, dckv, dkr, x, gn, sc, w_in_t)


def _window_sum(a, w, rows, forward):
    s = a.shape[0]
    step = 1
    while step < w:
        if forward:
            shifted = jnp.where(rows < s - step, pltpu.roll(a, s - step, 0), 0.0)
        else:
            shifted = jnp.where(rows >= step, pltpu.roll(a, step, 0), 0.0)
        a = a + shifted
        step *= 2
    return a


def pool_fwd(u, pool_w, pool_scale):
    s = u.shape[0]

    def body(u_ref, w_ref, sc_ref, y_ref, diff_ref):
        rows = lax.broadcasted_iota(jnp.int32, (s, POOL_GC), 0)
        for g, w in enumerate(POOL_WINDOWS):
            cols = slice(g * POOL_GC, (g + 1) * POOL_GC)
            ug = u_ref[:, cols]
            cnt = jnp.minimum(rows + 1, w).astype(F32)
            diff = (_window_sum(ug, w, rows, False) / cnt - ug).astype(BF16)
            diff_ref[:, cols] = diff
            y_ref[:, cols] = _dot(diff, w_ref[g].astype(BF16)) * sc_ref[:, cols]

    return pl.pallas_call(
        body, name="pool_fwd",
        out_shape=[jax.ShapeDtypeStruct(u.shape, F32), jax.ShapeDtypeStruct(u.shape, BF16)],
        compiler_params=_params(),
    )(u, pool_w, pool_scale)


def pool_bwd(dy, diff, pool_w, pool_scale):
    s = dy.shape[0]

    def body(dy_ref, diff_ref, w_ref, sc_ref, du_ref, dw_ref, dsc_ref):
        rows = lax.broadcasted_iota(jnp.int32, (s, POOL_GC), 0)
        for g, w in enumerate(POOL_WINDOWS):
            cols = slice(g * POOL_GC, (g + 1) * POOL_GC)
            dyg = dy_ref[:, cols]
            diff = diff_ref[:, cols]
            wb = w_ref[g].astype(BF16)
            dsc_ref[:, cols] = _sum0(dyg * _dot(diff, wb))
            dys = (dyg * sc_ref[:, cols]).astype(BF16)
            dw_ref[g] = _dot_tn(diff, dys)
            ddiff = _dot_nt(dys, wb)
            cnt = jnp.minimum(rows + 1, w).astype(F32)
            du_ref[:, cols] = _window_sum(ddiff / cnt, w, rows, True) - ddiff

    return pl.pallas_call(
        body, name="pool_bwd",
        out_shape=[jax.ShapeDtypeStruct(dy.shape, F32), jax.ShapeDtypeStruct(pool_w.shape, F32),
                   jax.ShapeDtypeStruct(pool_scale.shape, F32)],
        compiler_params=_params(),
    )(dy, diff, pool_w, pool_scale)


def mla_qkv_fwd(cq, ckv, kr, qan, kvan, wq, wkv, cos, sin, rot):
    s = cq.shape[0]
    tm = _row_tile(s)

    def body(cq_ref, ckv_ref, kr_ref, qan_ref, kvan_ref, wq_ref, wkv_ref, cos_ref, sin_ref, rot_ref,
             q_ref, k_ref, v_ref, ql_ref, kvl_ref):
        cos_t = cos_ref[...]
        sin_t = sin_ref[...]
        perm = rot_ref[...]

        def rope(t):
            return t * cos_t + _dot_exact(t, perm) * sin_t

        qhat, _ = _rms(cq_ref[...])
        ql = (qhat * qan_ref[...]).astype(BF16)
        ql_ref[...] = ql
        khat, _ = _rms(ckv_ref[...])
        kvl = (khat * kvan_ref[...]).astype(BF16)
        kvl_ref[...] = kvl
        krr = rope(kr_ref[...]).astype(BF16)
        for h in range(N_HEADS):
            q = _dot_nt(ql, wq_ref[h])
            q_ref[h, :, 0:QK_NOPE] = q[:, 0:QK_NOPE].astype(BF16)
            q_ref[h, :, QK_NOPE:] = rope(q[:, QK_NOPE:]).astype(BF16)
            kv = _dot(kvl, wkv_ref[h])
            k_ref[h, :, 0:QK_NOPE] = kv[:, 0:QK_NOPE].astype(BF16)
            k_ref[h, :, QK_NOPE:] = krr
            v_ref[h] = kv[:, QK_NOPE:].astype(BF16)

    row = lambda w: pl.BlockSpec((tm, w), lambda i: (i, 0))
    hrow = lambda w: pl.BlockSpec((N_HEADS, tm, w), lambda i: (0, i, 0))
    qk = QK_NOPE + QK_ROPE
    return pl.pallas_call(
        body, name="mla_qkv_fwd",
        grid=(s // tm,),
        in_specs=[row(Q_LORA), row(KV_LORA), row(QK_ROPE), _full(qan.shape), _full(kvan.shape),
                  _full(wq.shape), _full(wkv.shape), row(QK_ROPE), row(QK_ROPE), _full(rot.shape)],
        out_specs=[hrow(qk), hrow(qk), hrow(V_HEAD), row(Q_LORA), row(KV_LORA)],
        out_shape=[jax.ShapeDtypeStruct((N_HEADS, s, qk), BF16), jax.ShapeDtypeStruct((N_HEADS, s, qk), BF16),
                   jax.ShapeDtypeStruct((N_HEADS, s, V_HEAD), BF16), jax.ShapeDtypeStruct((s, Q_LORA), BF16),
                   jax.ShapeDtypeStruct((s, KV_LORA), BF16)],
        compiler_params=_params(("arbitrary",)),
    )(cq, ckv, kr, qan, kvan, wq, wkv, cos, sin, rot)


def _attn_probs(q_ref, k_ref, qi, tq):
    n = (qi + 1) * tq
    rows = slice(qi * tq, n)
    sc = _dot_nt(q_ref[rows, :], k_ref[0:n, :]) * SOFTMAX_SCALE
    qpos = qi * tq + lax.broadcasted_iota(jnp.int32, (tq, n), 0)
    kpos = lax.broadcasted_iota(jnp.int32, (tq, n), 1)
    sc = jnp.where(qpos >= kpos, sc, -1e30)
    e = jnp.exp(sc - jnp.max(sc, axis=-1, keepdims=True))
    return e / jnp.sum(e, axis=-1, keepdims=True)


def attn_fwd(q, k, v):
    nh, s, qk = q.shape
    tq = min(s, ATT_TILE)

    def body(q_ref, k_ref, v_ref, o_ref):
        for qi in range(s // tq):
            n = (qi + 1) * tq
            p = _attn_probs(q_ref, k_ref, qi, tq).astype(BF16)
            o_ref[qi * tq:n, :] = _dot(p, v_ref[0:n, :])

    head = lambda w: pl.BlockSpec((None, s, w), lambda h: (h, 0, 0))
    return pl.pallas_call(
        body, name="attn_fwd",
        grid=(nh,),
        in_specs=[head(qk), head(qk), head(V_HEAD)],
        out_specs=pl.BlockSpec((s, V_HEAD), lambda h: (0, h)),
        out_shape=jax.ShapeDtypeStruct((s, nh * V_HEAD), F32),
        compiler_params=_params(("arbitrary",)),
    )(q, k, v)


def attn_bwd(q, k, v, do):
    nh, s, qk = q.shape
    tq = min(s, ATT_TILE)

    def body(q_ref, k_ref, v_ref, do_ref, dq_ref, dk_ref, dv_ref):
        dk_ref[...] = jnp.zeros_like(dk_ref)
        dv_ref[...] = jnp.zeros_like(dv_ref)
        for qi in range(s // tq):
            n = (qi + 1) * tq
            rows = slice(qi * tq, n)
            p = _attn_probs(q_ref, k_ref, qi, tq)
            dob = do_ref[rows, :].astype(BF16)
            dp = _dot_nt(dob, v_ref[0:n, :])
            ds = (p * (dp - jnp.sum(p * dp, axis=-1, keepdims=True)) * SOFTMAX_SCALE).astype(BF16)
            dq_ref[rows, :] = _dot(ds, k_ref[0:n, :])
            dk_ref[0:n, :] += _dot_tn(ds, q_ref[rows, :])
            dv_ref[0:n, :] += _dot_tn(p.astype(BF16), dob)

    head = lambda w: pl.BlockSpec((None, s, w), lambda h: (h, 0, 0))
    return pl.pallas_call(
        body, name="attn_bwd",
        grid=(nh,),
        in_specs=[head(qk), head(qk), head(V_HEAD), pl.BlockSpec((s, V_HEAD), lambda h: (0, h))],
        out_specs=[head(qk), head(qk), head(V_HEAD)],
        out_shape=[jax.ShapeDtypeStruct((nh, s, qk), F32), jax.ShapeDtypeStruct((nh, s, qk), F32),
                   jax.ShapeDtypeStruct((nh, s, V_HEAD), F32)],
        compiler_params=_params(("arbitrary",)),
    )(q, k, v, do)


def mla_qkv_bwd(dq, dk, dv, cq, ckv, qan, kvan, wq, wkv, cos, sin, rot_t):
    s = cq.shape[0]
    tm = _row_tile(s)

    def body(dq_ref, dk_ref, dv_ref, cq_ref, ckv_ref, qan_ref, kvan_ref,
             wq_ref, wkv_ref, cos_ref, sin_ref, rot_ref,
             dcq_ref, dckv_ref, dkro_ref, gq_ref, gkv_ref, dqan_ref, dkvan_ref):
        i = pl.program_id(0)

        @pl.when(i == 0)
        def _():
            dqan_ref[...] = jnp.zeros_like(dqan_ref)
            dkvan_ref[...] = jnp.zeros_like(dkvan_ref)

        cos_t = cos_ref[...]
        sin_t = sin_ref[...]
        perm_t = rot_ref[...]

        def unrope(t):
            return t * cos_t + _dot_exact(t * sin_t, perm_t)

        acc_q = jnp.zeros((tm, Q_LORA), F32)
        acc_kv = jnp.zeros((tm, KV_LORA), F32)
        dkr_sum = jnp.zeros((tm, QK_ROPE), F32)
        for h in range(N_HEADS):
            dq_h = dq_ref[h]
            a = dq_h[:, 0:QK_NOPE].astype(BF16)
            b = unrope(dq_h[:, QK_NOPE:]).astype(BF16)
            gq_ref[h, :, 0:QK_NOPE] = a
            gq_ref[h, :, QK_NOPE:] = b
            wq_h = wq_ref[h]
            acc_q += _dot(a, wq_h[0:QK_NOPE, :]) + _dot(b, wq_h[QK_NOPE:, :])
            dk_h = dk_ref[h]
            dk = dk_h[:, 0:QK_NOPE].astype(BF16)
            dvv = dv_ref[h].astype(BF16)
            gkv_ref[h, :, 0:QK_NOPE] = dk
            gkv_ref[h, :, QK_NOPE:] = dvv
            wkv_h = wkv_ref[h]
            acc_kv += _dot_nt(dk, wkv_h[:, 0:QK_NOPE]) + _dot_nt(dvv, wkv_h[:, QK_NOPE:])
            dkr_sum += dk_h[:, QK_NOPE:]
        dkro_ref[...] = unrope(dkr_sum)
        dcq, dqan = _rms_bwd(acc_q, cq_ref[...], qan_ref[...])
        dcq_ref[...] = dcq
        dqan_ref[...] += dqan
        dckv, dkvan = _rms_bwd(acc_kv, ckv_ref[...], kvan_ref[...])
        dckv_ref[...] = dckv
        dkvan_ref[...] += dkvan

    row = lambda w: pl.BlockSpec((tm, w), lambda i: (i, 0))
    hrow = lambda w: pl.BlockSpec((N_HEADS, tm, w), lambda i: (0, i, 0))
    return pl.pallas_call(
        body, name="mla_qkv_bwd",
        grid=(s // tm,),
        in_specs=[hrow(QK_NOPE + QK_ROPE), hrow(QK_NOPE + QK_ROPE), hrow(V_HEAD),
                  row(Q_LORA), row(KV_LORA), _full(qan.shape), _full(kvan.shape),
                  _full(wq.shape), _full(wkv.shape), row(QK_ROPE), row(QK_ROPE), _full(rot_t.shape)],
        out_specs=[row(Q_LORA), row(KV_LORA), row(QK_ROPE), hrow(QK_NOPE + QK_ROPE), hrow(QK_NOPE + V_HEAD),
                   _full(qan.shape), _full(kvan.shape)],
        out_shape=[jax.ShapeDtypeStruct((s, Q_LORA), F32), jax.ShapeDtypeStruct((s, KV_LORA), F32),
                   jax.ShapeDtypeStruct((s, QK_ROPE), F32),
                   jax.ShapeDtypeStruct((N_HEADS, s, QK_NOPE + QK_ROPE), BF16),
                   jax.ShapeDtypeStruct((N_HEADS, s, QK_NOPE + V_HEAD), BF16),
                   jax.ShapeDtypeStruct(qan.shape, F32), jax.ShapeDtypeStruct(kvan.shape, F32)],
        compiler_params=_params(("arbitrary",)),
    )(dq, dk, dv, cq, ckv, qan, kvan, wq, wkv, cos, sin, rot_t)


def out_proj_fwd(yp, om, w_out, x, gt):
    s, d = x.shape
    n_sh, rs, _ = w_out.shape
    tm = _row_tile(s)
    per = POOL_WIDTH // rs

    def body(yp_ref, om_ref, w_ref, x_ref, gt_ref, xo_ref, ycat_ref, y_ref):
        y = jnp.zeros((tm, d), F32)
        for j in range(n_sh):
            src = yp_ref if j < per else om_ref
            part = src[:, (j % per) * rs:(j % per + 1) * rs].astype(BF16)
            ycat_ref[j] = part
            y += _dot(part, w_ref[j])
        y_ref[...] = y.astype(BF16)
        xo_ref[...] = x_ref[...] + gt_ref[...] * y

    row = lambda w: pl.BlockSpec((tm, w), lambda i: (i, 0))
    return pl.pallas_call(
        body, name="out_proj_fwd",
        grid=(s // tm,),
        in_specs=[row(POOL_WIDTH), row(POOL_WIDTH), _full(w_out.shape), row(d), pl.BlockSpec((1, d), lambda i: (0, 0))],
        out_specs=[row(d), pl.BlockSpec((n_sh, tm, rs), lambda i: (0, i, 0)), row(d)],
        out_shape=[jax.ShapeDtypeStruct((s, d), F32), jax.ShapeDtypeStruct((n_sh, s, rs), BF16),
                   jax.ShapeDtypeStruct((s, d), BF16)],
        compiler_params=_params(("arbitrary",)),
    )(yp, om, w_out, x, gt)


def out_proj_bwd(dxn, y, gt, w_out):
    s, d = dxn.shape
    n_sh, rs, _ = w_out.shape
    tm = _row_tile(s)
    per = POOL_WIDTH // rs

    def body(dxn_ref, y_ref, gt_ref, w_ref, dy_ref, dyp_ref, dom_ref, dgt_ref):
        i = pl.program_id(0)

        @pl.when(i == 0)
        def _():
            dgt_ref[...] = jnp.zeros_like(dgt_ref)

        dxn_t = dxn_ref[...]
        dy = (gt_ref[...] * dxn_t).astype(BF16)
        dy_ref[...] = dy
        dgt_ref[...] += _sum0(dxn_t * y_ref[...].astype(F32))
        for j in range(n_sh):
            dst = dyp_ref if j < per else dom_ref
            dst[:, (j % per) * rs:(j % per + 1) * rs] = _dot_nt(dy, w_ref[j])

    row = lambda w: pl.BlockSpec((tm, w), lambda i: (i, 0))
    vec = pl.BlockSpec((1, d), lambda i: (0, 0))
    return pl.pallas_call(
        body, name="out_proj_bwd",
        grid=(s // tm,),
        in_specs=[row(d), row(d), vec, _full(w_out.shape)],
        out_specs=[row(d), row(POOL_WIDTH), row(POOL_WIDTH), vec],
        out_shape=[jax.ShapeDtypeStruct((s, d), BF16), jax.ShapeDtypeStruct((s, POOL_WIDTH), F32),
                   jax.ShapeDtypeStruct((s, POOL_WIDTH), F32), jax.ShapeDtypeStruct((1, d), F32)],
        compiler_params=_params(("arbitrary",)),
    )(dxn, y, gt, w_out)


def final_loss(x, gn, tgt):
    s, d = x.shape
    tm = _row_tile(s)

    def body(x_ref, gn_ref, t_ref, loss_ref, dx_ref, dgn_ref):
        i = pl.program_id(0)

        @pl.when(i == 0)
        def _():
            loss_ref[...] = jnp.zeros_like(loss_ref)
            dgn_ref[...] = jnp.zeros_like(dgn_ref)

        xt = x_ref[...]
        g = gn_ref[...]
        xhat, _ = _rms(xt)
        err = xhat * g - t_ref[...]
        per_tok = jnp.mean(err * err, axis=-1, keepdims=True)
        loss_ref[...] += jnp.broadcast_to(0.5 * _sum0(per_tok), loss_ref.shape)
        dx, dgn = _rms_bwd(err * (1.0 / d), xt, g)
        dx_ref[...] = dx
        dgn_ref[...] += dgn

    row = pl.BlockSpec((tm, d), lambda i: (i, 0))
    vec = pl.BlockSpec((1, d), lambda i: (0, 0))
    return pl.pallas_call(
        body, name="final_loss",
        grid=(s // tm,),
        in_specs=[row, vec, row],
        out_specs=[pl.BlockSpec((1, LANES), lambda i: (0, 0)), row, vec],
        out_shape=[jax.ShapeDtypeStruct((1, LANES), F32), jax.ShapeDtypeStruct((s, d), F32),
                   jax.ShapeDtypeStruct((1, d), F32)],
        compiler_params=_params(("arbitrary",)),
    )(x, gn, tgt)


def _col_tile(cols):
    return 768 if cols % 768 == 0 else cols


def ada_fwd(c16, ada_w, ada_b_loc):
    n_layers, d, cols = ada_w.shape
    tn = _col_tile(cols)

    def body(c_ref, w_ref, b_ref, o_ref):
        cv = c_ref[...]
        ca = (cv * jax.nn.sigmoid(cv)).astype(BF16)
        o_ref[...] = _dot(ca, w_ref[...].astype(BF16)) + b_ref[...]

    return pl.pallas_call(
        body, name="ada_fwd",
        grid=(n_layers, cols // tn),
        in_specs=[pl.BlockSpec((16, d), lambda l, j: (0, 0)), pl.BlockSpec((None, d, tn), lambda l, j: (l, 0, j)),
                  pl.BlockSpec((None, 1, tn), lambda l, j: (l, 0, j))],
        out_specs=pl.BlockSpec((None, 16, tn), lambda l, j: (l, 0, j)),
        out_shape=jax.ShapeDtypeStruct((n_layers, 16, cols), F32),
        compiler_params=_params(("arbitrary", "arbitrary")),
    )(c16, ada_w, ada_b_loc)


def ada_bwd(c16, dmod16):
    n_layers, _, cols = dmod16.shape
    d = c16.shape[1]
    tn = _col_tile(cols)

    def body(c_ref, g_ref, o_ref):
        cv = c_ref[...]
        ca = (cv * jax.nn.sigmoid(cv)).astype(BF16)
        o_ref[...] = _dot_tn(ca, g_ref[...].astype(BF16))

    return pl.pallas_call(
        body, name="ada_bwd",
        grid=(n_layers, cols // tn),
        in_specs=[pl.BlockSpec((16, d), lambda l, j: (0, 0)), pl.BlockSpec((None, 16, tn), lambda l, j: (l, 0, j))],
        out_specs=pl.BlockSpec((None, d, tn), lambda l, j: (l, 0, j)),
        out_shape=jax.ShapeDtypeStruct((n_layers, d, cols), F32),
        compiler_params=_params(("arbitrary", "arbitrary")),
    )(c16, dmod16)


def _as_rows(a):
    if a.ndim == 1:
        return a.reshape(1, a.shape[0])
    return a.reshape(-1, a.shape[-1])


def _rows_tile(r, c, itemsize=4, budget=2 * 1024 * 1024):
    if r * c * itemsize <= budget:
        return r
    best = None
    t = BF16_ROWS
    while t < r:
        if r % t == 0 and t * c * itemsize <= budget:
            best = t
        t += BF16_ROWS
    return best if best is not None else r


def cast_place(w, chip, layers, after):
    _, r, c = w.shape
    n_sel = len(layers)
    tr = _rows_tile(r, c, budget=2 * 1024 * 1024 // n_sel)

    def body(chip_ref, *refs):
        for j in range(n_sel):
            refs[n_sel + 1 + j][...] = refs[j][...].astype(BF16)

    layer_spec = lambda l: pl.BlockSpec((None, tr, c), lambda i, ch: (l, i, 0))
    return list(pl.pallas_call(
        body, name="cast_place",
        grid_spec=pltpu.PrefetchScalarGridSpec(
            num_scalar_prefetch=1, grid=(r // tr,),
            in_specs=[layer_spec(l) for l in layers] + [pl.BlockSpec(memory_space=pl.ANY)],
            out_specs=[pl.BlockSpec((None, tr, c), lambda i, ch: (ch[0], i, 0))] * n_sel),
        out_shape=[jax.ShapeDtypeStruct((N_CHIPS, r, c), BF16)] * n_sel,
        compiler_params=_params(("arbitrary",)),
    )(chip, *([w] * n_sel), after))


def adamw(w, g, m, v, copy_g=False):
    shape = w.shape
    w2, g2, m2, v2 = (_as_rows(t) for t in (w, g, m, v))
    r, c = w2.shape
    tr = _rows_tile(r, c, budget=1024 * 1024)
    c1 = 1.0 - ADAM_B1 ** ADAM_STEP
    c2 = 1.0 - ADAM_B2 ** ADAM_STEP

    def body(w_ref, g_ref, m_ref, v_ref, d_ref, mo_ref, vo_ref, *go_ref):
        gv = g_ref[...]
        if copy_g:
            go_ref[0][...] = gv
        mn = ADAM_B1 * m_ref[...] + (1.0 - ADAM_B1) * gv
        vn = ADAM_B2 * v_ref[...] + (1.0 - ADAM_B2) * (gv * gv)
        mo_ref[...] = mn
        vo_ref[...] = vn
        d_ref[...] = -ADAM_LR * ((mn / c1) / (jnp.sqrt(vn / c2) + ADAM_EPS) + ADAM_WD * w_ref[...])

    spec = pl.BlockSpec((tr, c), lambda i: (i, 0))
    n_out = 4 if copy_g else 3
    outs = pl.pallas_call(
        body, name="adamw", grid=(r // tr,), in_specs=[spec] * 4, out_specs=[spec] * n_out,
        out_shape=[jax.ShapeDtypeStruct((r, c), F32)] * n_out, compiler_params=_params(("arbitrary",)),
    )(w2, g2, m2, v2)
    g_out = outs[3] if copy_g else g2
    return tuple(o.reshape(shape) for o in (g_out,) + tuple(outs[:3]))


def sum_devices(a):
    n, r, c = a.shape
    tr = _rows_tile(r, c, budget=512 * 1024)

    def body(a_ref, o_ref):
        acc = a_ref[0]
        for j in range(1, n):
            acc = acc + a_ref[j]
        o_ref[...] = acc

    return pl.pallas_call(
        body, name="sum_devices", grid=(r // tr,),
        in_specs=[pl.BlockSpec((n, tr, c), lambda i: (0, i, 0))], out_specs=pl.BlockSpec((tr, c), lambda i: (i, 0)),
        out_shape=jax.ShapeDtypeStruct((r, c), F32), compiler_params=_params(("arbitrary",)),
    )(a)


def _split_axis(r, c):
    if (r // 2) % BF16_ROWS == 0 and r % 2 == 0:
        return 0
    assert c % (2 * LANES) == 0, (r, c)
    return 1


def _half_shape(r, c):
    return (r // 2, c) if _split_axis(r, c) == 0 else (r, c // 2)


def _half_at(ref, lead, which):
    r, c = ref.shape[-2:]
    if _split_axis(r, c) == 0:
        return ref.at[(*lead, pl.ds(which * (r // 2), r // 2), slice(None))]
    return ref.at[(*lead, slice(None), pl.ds(which * (c // 2), c // 2))]


def _half_spec(r, c, lead_block, imap):
    hr, hc = _half_shape(r, c)
    if _split_axis(r, c) == 0:
        return pl.BlockSpec((*lead_block, hr, hc), lambda *a: (*imap(*a)[0], imap(*a)[1], 0))
    return pl.BlockSpec((*lead_block, hr, hc), lambda *a: (*imap(*a)[0], 0, imap(*a)[1]))


def pair_add(gs, ras, sel):
    n = len(gs)
    n_sl = gs[0].shape[0]
    halves = [_half_shape(*g.shape[1:]) for g in gs]

    def body(s_ref, *refs):
        g_refs, ra_refs, pb_refs, own_refs = (refs[i * n:(i + 1) * n] for i in range(4))
        k = pl.program_id(0)
        for t in range(n):
            p = g_refs[t][...] + ra_refs[t][...]
            pb_refs[t][...] = p.astype(BF16)

            @pl.when(k == s_ref[1])
            def _(p=p, own=own_refs[t]):
                own[...] = p

    slot = lambda hs: pl.BlockSpec((None,) + hs, lambda k, sr: (k, 0, 0))
    outs = pl.pallas_call(
        body, name="pair_add",
        grid_spec=pltpu.PrefetchScalarGridSpec(
            num_scalar_prefetch=1, grid=(n_sl,),
            in_specs=[_half_spec(*g.shape[1:], (None,), lambda k, sr: ((k,), sr[0])) for g in gs]
            + [slot(hs) for hs in halves],
            out_specs=[slot(hs) for hs in halves] + [pl.BlockSpec(hs, lambda k, sr: (0, 0)) for hs in halves]),
        out_shape=[jax.ShapeDtypeStruct((n_sl,) + hs, BF16) for hs in halves]
        + [jax.ShapeDtypeStruct(hs, F32) for hs in halves],
        compiler_params=_params(("arbitrary",)),
    )(sel, *gs, *ras)
    return list(outs[:n]), list(outs[n:])


def chip_sum(owns, rbs, sel, shapes, accs):
    n = len(owns)
    fresh = accs[0] is None

    def body(s_ref, *refs):
        own_refs, rb_refs, o_refs = refs[:n], refs[n:2 * n], refs[-n:]
        for t in range(n):
            acc_v = own_refs[t][...]
            for j in range(N_CHIPS - 1):
                acc_v = acc_v + rb_refs[t][j].astype(F32)
            o_refs[t][...] = acc_v

    in_specs = ([pl.BlockSpec(o.shape, lambda i, sr: (0, 0)) for o in owns]
                + [pl.BlockSpec(rb.shape, lambda i, sr: (0, 0, 0)) for rb in rbs])
    args = [sel, *owns, *rbs]
    aliases = {}
    if not fresh:
        in_specs += [pl.BlockSpec(memory_space=pl.ANY)] * n
        args += list(accs)
        aliases = {1 + 2 * n + t: t for t in range(n)}
    return list(pl.pallas_call(
        body, name="chip_sum",
        grid_spec=pltpu.PrefetchScalarGridSpec(
            num_scalar_prefetch=1, grid=(1,), in_specs=in_specs,
            out_specs=[_half_spec(*shp[1:], (None,), lambda i, sr: ((sr[2],), sr[0])) for shp in shapes]),
        out_shape=[jax.ShapeDtypeStruct(shp, F32) for shp in shapes],
        input_output_aliases=aliases,
        compiler_params=_params(("arbitrary",)),
    )(*args))


def _me():
    return lax.axis_index("x"), lax.axis_index("y"), lax.axis_index("c")


def _flip(v, bit):
    return 1 - v if bit else v


def exchange8(xs, bcast):
    blk = xs.shape if bcast else xs.shape[1:]

    def body(x_ref, o_ref, send_sems, recv_sems, loc_sem):
        mx, my, mc = _me()
        me = 4 * mx + 2 * my + mc
        src = (lambda j: x_ref) if bcast else (lambda j: x_ref.at[j])
        loc = pltpu.make_async_copy(src(me), o_ref.at[me], loc_sem)
        loc.start()
        copies = []
        for o in range(1, N_DEV):
            px, py, pc = _flip(mx, o & 4), _flip(my, o & 2), _flip(mc, o & 1)
            cp = pltpu.make_async_remote_copy(
                src_ref=src(4 * px + 2 * py + pc), dst_ref=o_ref.at[me],
                send_sem=send_sems.at[o - 1], recv_sem=recv_sems.at[o - 1],
                device_id=(px, py, pc), device_id_type=MESH)
            cp.start()
            copies.append(cp)
        for cp in copies:
            cp.wait()
        loc.wait()

    return pl.pallas_call(
        body, name="exchange8_gather" if bcast else "exchange8_a2a",
        in_specs=[pl.BlockSpec(memory_space=pltpu.VMEM)], out_specs=pl.BlockSpec(memory_space=pltpu.VMEM),
        out_shape=jax.ShapeDtypeStruct((N_DEV,) + tuple(blk), xs.dtype),
        scratch_shapes=[pltpu.SemaphoreType.DMA((N_DEV - 1,)), pltpu.SemaphoreType.DMA((N_DEV - 1,)), pltpu.SemaphoreType.DMA],
        compiler_params=_params(),
    )(xs)


HBM = pl.BlockSpec(memory_space=pltpu.HBM)
SEM = pl.BlockSpec(memory_space=pltpu.SEMAPHORE)
EFFECT = pltpu.SideEffectType.DATAFLOW_SIDE_EFFECTING


def _hbm(a):
    return pltpu.with_memory_space_constraint(a, pltpu.HBM)


def _ici_copy(land, o, send_sem, recv_sem, sending):
    mx, my, mc = _me()
    px, py = _flip(mx, o & 2), _flip(my, o & 1)
    mine = _half_at(land, (2 * mx + my,), mc)
    return pltpu.make_async_remote_copy(
        src_ref=mine, dst_ref=mine if sending else _half_at(land, (2 * px + py,), mc),
        send_sem=send_sem, recv_sem=recv_sem, device_id=(px, py, mc), device_id_type=MESH)


N_PEERS = N_CHIPS - 1
DMA_SEM = pltpu.SemaphoreType.DMA(())


def gather_start(lands, groups, after, tag):
    n_layers, n = len(lands), len(lands[0])
    flat = [a for layer in lands for a in layer]
    n_in = n * n_layers
    n_grp = len(groups)
    n_sem = 2 * n_layers * n_grp * N_PEERS
    first = lambda l, g, recv: ((l * n_grp + g) * 2 + recv) * N_PEERS

    def body(*refs):
        land = refs[:n_in]
        sems = refs[n_in + 1:n_in + 1 + n_sem]
        token = refs[-1]
        for l in range(n_layers):
            for g, members in enumerate(groups):
                for t in members:
                    for o in range(1, N_CHIPS):
                        _ici_copy(land[l * n + t], o, sems[first(l, g, 0) + o - 1], sems[first(l, g, 1) + o - 1],
                                  True).start()
        token[...] = jnp.zeros_like(token)

    outs = pl.pallas_call(
        body, name=f"gather_start_{tag}",
        in_specs=[HBM] * n_in + [pl.BlockSpec(memory_space=pl.ANY)],
        out_specs=[SEM] * n_sem + [HBM] * n_in + [pl.BlockSpec(memory_space=pltpu.VMEM)],
        out_shape=[DMA_SEM] * n_sem + [pltpu.HBM(a.shape, a.dtype) for a in flat]
        + [jax.ShapeDtypeStruct((8, LANES), F32)],
        input_output_aliases={i: i + n_sem for i in range(n_in)},
        compiler_params=pltpu.CompilerParams(has_side_effects=EFFECT),
    )(*[_hbm(a) for a in flat], after)
    sems = [[(list(outs[first(l, g, 0):first(l, g, 0) + N_PEERS]), list(outs[first(l, g, 1):first(l, g, 1) + N_PEERS]))
             for g in range(n_grp)] for l in range(n_layers)]
    lands_thru = [list(outs[n_sem + l * n:n_sem + (l + 1) * n]) for l in range(n_layers)]
    return sems, lands_thru, outs[-1]


def gather_wait(tag, sems, lands, after):
    n = len(lands)
    send_sems, recv_sems = sems

    def body(*refs):
        land = refs[:n]
        send_r = refs[n:n + N_PEERS]
        recv_r = refs[n + N_PEERS:n + 2 * N_PEERS]
        for t in range(n):
            for o in range(1, N_CHIPS):
                _ici_copy(land[t], o, send_r[o - 1], recv_r[o - 1], True).wait_send()
                _ici_copy(land[t], o, send_r[o - 1], recv_r[o - 1], False).wait_recv()

    return list(pl.pallas_call(
        body, name=f"gather_wait_{tag}",
        in_specs=[HBM] * n + [SEM] * (2 * N_PEERS) + [pl.BlockSpec(memory_space=pl.ANY)],
        out_specs=[HBM] * n,
        out_shape=[pltpu.HBM(a.shape, a.dtype) for a in lands],
        input_output_aliases={i: i for i in range(n)},
        compiler_params=pltpu.CompilerParams(has_side_effects=EFFECT),
    )(*lands, *send_sems, *recv_sems, after))


def gather_forward(lands):
    n = len(lands)

    def body(*refs):
        dst = refs[n:2 * n]
        send_sems, recv_sems = refs[2 * n:]
        mx, my, mc = _me()
        fwds = []
        for t in range(n):
            for o in range(1, N_CHIPS):
                slot = 2 * _flip(mx, o & 2) + _flip(my, o & 1)
                mine = _half_at(dst[t], (slot,), mc)
                theirs = _half_at(dst[t], (slot,), 1 - mc)
                cp = pltpu.make_async_remote_copy(
                    src_ref=mine, dst_ref=mine, send_sem=send_sems.at[t, o - 1], recv_sem=recv_sems.at[t, o - 1],
                    device_id=(mx, my, 1 - mc), device_id_type=MESH)
                cp.start()
                fwds.append((cp, pltpu.make_async_remote_copy(
                    src_ref=theirs, dst_ref=theirs, send_sem=send_sems.at[t, o - 1], recv_sem=recv_sems.at[t, o - 1],
                    device_id=(mx, my, 1 - mc), device_id_type=MESH)))
        for cp, arrival in fwds:
            cp.wait_send()
            arrival.wait_recv()

    any_spec = pl.BlockSpec(memory_space=pl.ANY)
    return list(pl.pallas_call(
        body, name="gather_forward",
        in_specs=[any_spec] * n, out_specs=[any_spec] * n,
        out_shape=[jax.ShapeDtypeStruct(a.shape, a.dtype) for a in lands],
        input_output_aliases={t: t for t in range(n)},
        scratch_shapes=[pltpu.SemaphoreType.DMA((n, N_CHIPS - 1)), pltpu.SemaphoreType.DMA((n, N_CHIPS - 1))],
        compiler_params=_params(),
    )(*lands))


def _scatter_copy(src, land, o, send_sem, recv_sem):
    mx, my, mc = _me()
    px, py = _flip(mx, o & 2), _flip(my, o & 1)
    return pltpu.make_async_remote_copy(
        src_ref=src.at[2 * px + py], dst_ref=land.at[o - 1],
        send_sem=send_sem, recv_sem=recv_sem, device_id=(px, py, mc), device_id_type=MESH)


def scatter_start(pbs, tag, after):
    n = len(pbs)
    lands = [lax.empty((N_CHIPS - 1,) + p.shape[1:], p.dtype) for p in pbs]

    def body(*refs):
        src = refs[:n]
        land = refs[n:2 * n]
        send_sems = refs[2 * n + 1:2 * n + 1 + N_PEERS]
        recv_sems = refs[2 * n + 1 + N_PEERS:2 * n + 1 + 2 * N_PEERS]
        token = refs[-1]
        for t in range(n):
            for o in range(1, N_CHIPS):
                _scatter_copy(src[t], land[t], o, send_sems[o - 1], recv_sems[o - 1]).start()
        token[...] = jnp.zeros_like(token)

    n_sem = 2 * N_PEERS
    arrs = list(pbs) + lands
    outs = pl.pallas_call(
        body, name=f"scatter_start_{tag}",
        in_specs=[HBM] * (2 * n) + [pl.BlockSpec(memory_space=pl.ANY)],
        out_specs=[SEM] * n_sem + [HBM] * (2 * n) + [pl.BlockSpec(memory_space=pltpu.VMEM)],
        out_shape=[DMA_SEM] * n_sem + [pltpu.HBM(a.shape, a.dtype) for a in arrs]
        + [jax.ShapeDtypeStruct((8, LANES), F32)],
        input_output_aliases={i: i + n_sem for i in range(2 * n)},
        compiler_params=pltpu.CompilerParams(has_side_effects=EFFECT),
    )(*[_hbm(a) for a in arrs], after)
    return (list(outs[:N_PEERS]), list(outs[N_PEERS:n_sem]), list(outs[n_sem:n_sem + n]),
            list(outs[n_sem + n:n_sem + 2 * n]), outs[-1])


def scatter_wait(tag, send_sems, recv_sems, pbs, lands, after):
    n = len(pbs)

    def body(*refs):
        src = refs[:n]
        land = refs[n:2 * n]
        send_r = refs[2 * n:2 * n + N_PEERS]
        recv_r = refs[2 * n + N_PEERS:2 * n + 2 * N_PEERS]
        for t in range(n):
            for o in range(1, N_CHIPS):
                cp = _scatter_copy(src[t], land[t], o, send_r[o - 1], recv_r[o - 1])
                cp.wait_send()
                cp.wait_recv()

    arrs = list(pbs) + list(lands)
    outs = pl.pallas_call(
        body, name=f"scatter_wait_{tag}",
        in_specs=[HBM] * (2 * n) + [SEM] * (2 * N_PEERS) + [pl.BlockSpec(memory_space=pl.ANY)],
        out_specs=[HBM] * (2 * n),
        out_shape=[pltpu.HBM(a.shape, a.dtype) for a in arrs],
        input_output_aliases={i: i for i in range(2 * n)},
        compiler_params=pltpu.CompilerParams(has_side_effects=EFFECT),
    )(*arrs, *send_sems, *recv_sems, after)
    return list(outs[n:])


def _pair_copy(src, land, send_sem, recv_sem):
    mx, my, mc = _me()
    return pltpu.make_async_remote_copy(
        src_ref=_half_at(src, (slice(None),), 1 - mc), dst_ref=land, send_sem=send_sem, recv_sem=recv_sem,
        device_id=(mx, my, 1 - mc), device_id_type=MESH)


def pair_start(gs, tag, after):
    n = len(gs)
    lands = [lax.empty((g.shape[0],) + _half_shape(*g.shape[1:]), g.dtype) for g in gs]

    def body(*refs):
        src = refs[:n]
        land = refs[n:2 * n]
        send_sem, recv_sem = refs[2 * n + 1], refs[2 * n + 2]
        token = refs[-1]
        for t in range(n):
            _pair_copy(src[t], land[t], send_sem, recv_sem).start()
        token[...] = jnp.zeros_like(token)

    arrs = list(gs) + lands
    outs = pl.pallas_call(
        body, name=f"pair_start_{tag}",
        in_specs=[HBM] * (2 * n) + [pl.BlockSpec(memory_space=pl.ANY)],
        out_specs=[SEM, SEM] + [HBM] * (2 * n) + [pl.BlockSpec(memory_space=pltpu.VMEM)],
        out_shape=[DMA_SEM, DMA_SEM] + [pltpu.HBM(a.shape, a.dtype) for a in arrs] + [jax.ShapeDtypeStruct((8, LANES), F32)],
        input_output_aliases={i: i + 2 for i in range(2 * n)},
        compiler_params=pltpu.CompilerParams(has_side_effects=EFFECT),
    )(*[_hbm(a) for a in arrs], after)
    return outs[0], outs[1], list(outs[2:2 + n]), list(outs[2 + n:2 + 2 * n]), outs[-1]


def pair_wait(tag, send_sem, recv_sem, gs, lands, after):
    n = len(gs)

    def body(*refs):
        src = refs[:n]
        land = refs[n:2 * n]
        send_r, recv_r = refs[2 * n], refs[2 * n + 1]
        for t in range(n):
            cp = _pair_copy(src[t], land[t], send_r, recv_r)
            cp.wait_send()
            cp.wait_recv()

    arrs = list(gs) + list(lands)
    outs = pl.pallas_call(
        body, name=f"pair_wait_{tag}",
        in_specs=[HBM] * (2 * n) + [SEM, SEM, pl.BlockSpec(memory_space=pl.ANY)],
        out_specs=[HBM] * (2 * n),
        out_shape=[pltpu.HBM(a.shape, a.dtype) for a in arrs],
        input_output_aliases={i: i for i in range(2 * n)},
        compiler_params=pltpu.CompilerParams(has_side_effects=EFFECT),
    )(*arrs, send_sem, recv_sem, after)
    return list(outs[:n]), list(outs[n:])


def _gather8_copy(x, land, o, send_sem, recv_sem, sending):
    mx, my, mc = _me()
    px, py, pc = _flip(mx, o & 4), _flip(my, o & 2), _flip(mc, o & 1)
    slot = 4 * mx + 2 * my + mc if sending else 4 * px + 2 * py + pc
    return pltpu.make_async_remote_copy(
        src_ref=x, dst_ref=land.at[slot], send_sem=send_sem, recv_sem=recv_sem,
        device_id=(px, py, pc), device_id_type=MESH)


def gather8_start(x, land, after, tag):
    n_peer = N_DEV - 1

    def body(x_ref, land_ref, after_ref, *rest):
        send_sems, recv_sems = rest[:n_peer], rest[n_peer:2 * n_peer]
        token = rest[-1]
        for o in range(1, N_DEV):
            _gather8_copy(x_ref, land_ref, o, send_sems[o - 1], recv_sems[o - 1], True).start()
        token[...] = jnp.zeros_like(token)

    outs = pl.pallas_call(
        body, name=f"gather8_start_{tag}",
        in_specs=[HBM, HBM, pl.BlockSpec(memory_space=pl.ANY)],
        out_specs=[SEM] * (2 * n_peer) + [HBM, HBM, pl.BlockSpec(memory_space=pltpu.VMEM)],
        out_shape=[DMA_SEM] * (2 * n_peer) + [pltpu.HBM(x.shape, x.dtype), pltpu.HBM(land.shape, land.dtype),
                                              jax.ShapeDtypeStruct((8, LANES), F32)],
        input_output_aliases={0: 2 * n_peer, 1: 2 * n_peer + 1},
        compiler_params=pltpu.CompilerParams(has_side_effects=EFFECT),
    )(_hbm(x), _hbm(land), after)
    return list(outs[:n_peer]), list(outs[n_peer:2 * n_peer]), outs[2 * n_peer], outs[2 * n_peer + 1], outs[-1]


def gather8_wait(tag, send_sems, recv_sems, x, land, after):
    n_peer = N_DEV - 1

    def body(x_ref, land_ref, *rest):
        send_r, recv_r = rest[:n_peer], rest[n_peer:2 * n_peer]
        for o in range(1, N_DEV):
            _gather8_copy(x_ref, land_ref, o, send_r[o - 1], recv_r[o - 1], True).wait_send()
            _gather8_copy(x_ref, land_ref, o, send_r[o - 1], recv_r[o - 1], False).wait_recv()

    return pl.pallas_call(
        body, name=f"gather8_wait_{tag}",
        in_specs=[HBM, HBM] + [SEM] * (2 * n_peer) + [pl.BlockSpec(memory_space=pl.ANY)],
        out_specs=[HBM, HBM],
        out_shape=[pltpu.HBM(x.shape, x.dtype), pltpu.HBM(land.shape, land.dtype)],
        input_output_aliases={0: 0, 1: 1},
        compiler_params=pltpu.CompilerParams(has_side_effects=EFFECT),
    )(x, land, *send_sems, *recv_sems, after)[1]


def pair_fill_halves(fs):
    n = len(fs)

    def body(*refs):
        dst = refs[n:2 * n]
        send_sems, recv_sems = refs[2 * n:]
        mx, my, mc = _me()
        copies = []
        for t in range(n):
            mine = _half_at(dst[t], (slice(None),), mc)
            theirs = _half_at(dst[t], (slice(None),), 1 - mc)
            cp = pltpu.make_async_remote_copy(
                src_ref=mine, dst_ref=mine, send_sem=send_sems.at[t], recv_sem=recv_sems.at[t],
                device_id=(mx, my, 1 - mc), device_id_type=MESH)
            cp.start()
            copies.append((cp, pltpu.make_async_remote_copy(
                src_ref=theirs, dst_ref=theirs, send_sem=send_sems.at[t], recv_sem=recv_sems.at[t],
                device_id=(mx, my, 1 - mc), device_id_type=MESH)))
        for cp, arrival in copies:
            cp.wait_send()
            arrival.wait_recv()

    any_spec = pl.BlockSpec(memory_space=pl.ANY)
    return pl.pallas_call(
        body, name="pair_fill_halves",
        in_specs=[any_spec] * n, out_specs=[any_spec] * n,
        out_shape=[jax.ShapeDtypeStruct(f.shape, f.dtype) for f in fs],
        input_output_aliases={t: t for t in range(n)},
        scratch_shapes=[pltpu.SemaphoreType.DMA((n,)), pltpu.SemaphoreType.DMA((n,))],
        compiler_params=_params(),
    )(*fs)


def _pack_rows(parts, d):
    rows, spans = [], []
    at = 0
    for p in parts:
        flat = p.reshape(-1)
        n_rows = -(-flat.shape[0] // (8 * d)) * 8
        flat = jnp.pad(flat, (0, n_rows * d - flat.shape[0]))
        rows.append(flat.reshape(n_rows, d))
        spans.append((at, p.shape))
        at += n_rows
    return jnp.concatenate(rows, axis=0), spans


def _unpack_rows(packed, spans):
    lead, d = packed.shape[:-2], packed.shape[-1]
    out = []
    for at, shape in spans:
        n = math.prod(shape)
        n_rows = -(-n // d)
        out.append(packed[..., at:at + n_rows, :].reshape(lead + (-1,))[..., :n].reshape(lead + tuple(shape)))
    return out


def _rotate_half_matrix():
    half = QK_ROPE // 2
    idx = jnp.arange(QK_ROPE)
    src = jnp.where(idx < half, idx + half, idx - half)
    sign = jnp.where(idx < half, -1.0, 1.0)
    return (jnp.zeros((QK_ROPE, QK_ROPE), F32).at[src, idx].set(sign)).astype(BF16)


def kernel(x, c, positions, ada_w, ada_b, ffn1_norm, ffn1_w_gate, ffn1_w_up, ffn1_w_down, mix_norm, w_in, pool_w, pool_scale, q_a_norm, w_q_b, kv_a_norm, w_kv_b, w_out, ffn2_norm, ffn2_w_gate, ffn2_w_up, ffn2_w_down, final_norm, loss_target, m_ada_w, m_ada_b, m_ffn1_norm, m_ffn1_w_gate, m_ffn1_w_up, m_ffn1_w_down, m_mix_norm, m_w_in, m_pool_w, m_pool_scale, m_q_a_norm, m_w_q_b, m_kv_a_norm, m_w_kv_b, m_w_out, m_ffn2_norm, m_ffn2_w_gate, m_ffn2_w_up, m_ffn2_w_down, m_final_norm, v_ada_w, v_ada_b, v_ffn1_norm, v_ffn1_w_gate, v_ffn1_w_up, v_ffn1_w_down, v_mix_norm, v_w_in, v_pool_w, v_pool_scale, v_q_a_norm, v_w_q_b, v_kv_a_norm, v_w_kv_b, v_w_out, v_ffn2_norm, v_ffn2_w_gate, v_ffn2_w_up, v_ffn2_w_down, v_final_norm):
    mx, my, mc = _me()
    chip = 2 * mx + my
    half = jnp.reshape(mc, (1,)).astype(jnp.int32)
    chip1 = jnp.reshape(chip, (1,)).astype(jnp.int32)
    n_layers, d, ada_cols = ada_w.shape
    xt = x[0]
    tgt = loss_target[0]

    inv_freq = 1.0 / (ROPE_THETA ** (jnp.arange(0, QK_ROPE, 2, dtype=F32) / QK_ROPE))
    ang = positions[0].astype(F32)[:, None] * inv_freq
    ang = jnp.concatenate([ang, ang], axis=-1)
    cos, sin = jnp.cos(ang), jnp.sin(ang)
    rot = _rotate_half_matrix()
    rot_t = rot.T

    c_all = exchange8(c, True).reshape(N_DEV, d)
    c16 = jnp.pad(c_all, ((0, 8), (0, 0)))
    ada_b_loc = lax.dynamic_slice_in_dim(ada_b, chip * ada_cols, ada_cols, axis=1).reshape(n_layers, 1, ada_cols)
    mod_part = ada_fwd(c16, ada_w, ada_b_loc)[:, :N_DEV]
    mod_got = exchange8(jnp.transpose(mod_part, (1, 0, 2)), False)
    mod = jnp.transpose(mod_got.reshape(N_CHIPS, 2, n_layers, ada_cols)[:, 0], (1, 0, 2))
    mod = mod.reshape(n_layers, 9, 1, d)

    tr = lambda a: jnp.transpose(a, (0, 2, 1))
    local = [tr(ffn1_w_gate), tr(ffn1_w_up), ffn1_w_down, tr(w_in), tr(w_q_b), w_kv_b, w_out,
             tr(ffn2_w_gate), tr(ffn2_w_up), ffn2_w_down]
    ffn1_pos, rest_pos = (0, 1, 2), tuple(range(3, len(local)))
    groups = (ffn1_pos, rest_pos)
    placed = [cast_place(w, chip1, (0,), mod) for w in local]
    g_sems, lands_fly, g_token = gather_start([[p[0] for p in placed]], groups, mod, "first")
    if n_layers > 1:
        later = tuple(range(1, n_layers))
        placed = [cast_place(w, chip1, later, g_token) for w in local]
        more_sems, more_fly, g_token = gather_start(
            [[p[j] for p in placed] for j in range(len(later))], groups, g_token, "rest")
        g_sems, lands_fly = g_sems + more_sems, lands_fly + more_fly
    gathered = []

    row = lambda a, l: a[l].reshape(1, -1)
    saved = []
    for l in range(n_layers):
        g1, u1, d1 = gather_forward(gather_wait(
            f"{l}a", g_sems[l][0], [lands_fly[l][t] for t in ffn1_pos], xt if l else g_token))
        sv = dict(x0=xt)
        xt, sv["h1"], sv["a1"], sv["sl1"], sv["dsu1"], sv["y1"] = ffn_fwd(
            xt, row(ffn1_norm, l), mod[l, 0], mod[l, 1], mod[l, 2], g1, u1, d1)
        sv["x1"] = xt
        win, wq, wkv, wout, g2, u2, d2 = gather_forward(gather_wait(
            f"{l}b", g_sems[l][1], [lands_fly[l][t] for t in rest_pos], xt))
        gathered.append([g1, u1, d1, win, wq, wkv, wout, g2, u2, d2])
        win = win.reshape(-1, d)
        sv["h2"], u, cq, ckv, kr = mix_in_fwd(xt, row(mix_norm, l), mod[l, 3], mod[l, 4], win)
        sv["cq"], sv["ckv"] = cq, ckv
        yp, sv["diff"] = pool_fwd(u, pool_w[l], row(pool_scale, l))
        qh, kh, vh, sv["ql"], sv["kvl"] = mla_qkv_fwd(
            cq, ckv, kr, row(q_a_norm, l), row(kv_a_norm, l), wq, wkv, cos, sin, rot)
        sv["qkv"] = (qh, kh, vh)
        om = attn_fwd(qh, kh, vh)
        xt, sv["ycat"], sv["y2"] = out_proj_fwd(yp, om, wout, xt, mod[l, 5])
        sv["x2"] = xt
        xt, sv["h3"], sv["a3"], sv["sl3"], sv["dsu3"], sv["y3"] = ffn_fwd(
            xt, row(ffn2_norm, l), mod[l, 6], mod[l, 7], mod[l, 8], g2, u2, d2)
        saved.append(sv)

    loss_vec, dx, d_final_norm = final_loss(xt, final_norm.reshape(1, d), tgt)
    loss = lax.psum(loss_vec[0, 0], ("x", "y", "c"))

    none = [None] * n_layers
    dmods, dnorm1, dnorm2, dnorm3 = list(none), list(none), list(none), list(none)
    dpw, dps, dqan_l, dkvan_l = list(none), list(none), list(none), list(none)
    reduced = [None] * len(local)
    stages = []
    sel_of = lambda l: jnp.stack([mc, chip, jnp.asarray(l, mc.dtype)]).astype(jnp.int32)

    def to_chips(job, after_wait, after_start):
        send, recv, g_fly, lands_p = job.pop("pair")
        g_fly, got = pair_wait(job["tag"], send, recv, g_fly, lands_p, after_wait)
        pbs, job["owns"] = pair_add(g_fly, got, sel_of(job["l"]))
        job["scatter"] = scatter_start(pbs, job["tag"], after_start)
        return job["scatter"][4][0, 0]

    def finish(job, after):
        s_send, s_recv, pbs_fly, lands_j, _ = job.pop("scatter")
        parts = scatter_wait(job["tag"], s_send, s_recv, pbs_fly, lands_j, after)
        sums = chip_sum(job["owns"], parts, sel_of(job["l"]), [(n_layers,) + shp for shp in job["shapes"]],
                        [reduced[t] for t in job["pos"]])
        for t, total_t in zip(job["pos"], sums):
            reduced[t] = total_t

    def checkpoint(tag, l, positions, grads_, done, before_scatter=None):
        send, recv, g_fly, lands_p, tok = pair_start(grads_, tag, done)
        order = tok[0, 0]
        if stages:
            order = order + to_chips(stages[-1], done, done if before_scatter is None else before_scatter)
        if len(stages) >= 3:
            finish(stages[-3], done)
        stages.append(dict(tag=tag, l=l, pos=positions, shapes=[g.shape[1:] for g in grads_],
                           pair=(send, recv, g_fly, lands_p)))
        return order

    def small_gather(tag, parts, after):
        packed, spans = _pack_rows(parts, d)
        land = lax.dynamic_update_index_in_dim(lax.empty((N_DEV,) + packed.shape, F32), packed, 4 * mx + 2 * my + mc, 0)
        return gather8_start(packed, land, after, tag), spans

    order = None

    for l in reversed(range(n_layers)):
        sv = saved[l]
        g1, u1, d1, win, wq, wkv, wout, g2, u2, d2 = gathered[l]
        win = win.reshape(-1, d)
        gt3 = mod[l, 8] if order is None else mod[l, 8] + order
        dy, dgt, dup = ffn_bwd_act(dx, sv["sl3"], sv["dsu3"], gt3, d2)
        dx, dvec3 = ffn_bwd_in(dx, sv["x2"], sv["y3"], dgt, dup, row(ffn2_norm, l), mod[l, 7], g2, u2)
        g_g2, g_u2, g_d2 = tn_mm(dgt, sv["h3"][None]), tn_mm(dup, sv["h3"][None]), tn_mm(sv["a3"], dy[None])
        dy2, dyp, dom, dg2 = out_proj_bwd(dx, sv["y2"], mod[l, 5], wout)
        g_wout = tn_mm(sv["ycat"], dy2[None])
        qh, kh, vh = sv["qkv"]
        dqh, dkh, dvh = attn_bwd(qh, kh, vh, dom)
        dcq, dckv, dkr_in, gq, gkv, dqan_l[l], dkvan_l[l] = mla_qkv_bwd(
            dqh, dkh, dvh, sv["cq"], sv["ckv"], row(q_a_norm, l), row(kv_a_norm, l), wq, wkv, cos, sin, rot_t)
        g_wq, g_wkv = tn_mm(gq, sv["ql"][None]), tn_mm(sv["kvl"][None], gkv)
        du, dpw[l], dps[l] = pool_bwd(dyp, sv["diff"], pool_w[l], row(pool_scale, l))
        dx, dz, dvec2 = mix_in_bwd(dx, du, dcq, dckv, dkr_in, sv["x1"], row(mix_norm, l), mod[l, 4], win)
        g_win = tn_mm(dz[None], sv["h2"][None]).reshape(N_CHIPS, -1, d)
        dnorm2[l], dnorm3[l] = dvec2[3], dvec3[3]
        dmod_rest = jnp.concatenate([dvec2[0:2], dg2, dvec3[0:3]], axis=0)
        if l == 0:
            early = small_gather("early", [jnp.stack(dmods[1:]), dmod_rest, jnp.stack(dnorm1[1:]), jnp.stack(dnorm2),
                                           jnp.stack(dnorm3), d_final_norm, jnp.stack(dps), jnp.stack(dqan_l),
                                           jnp.stack(dkvan_l), jnp.stack(dpw)], dx)
        order = checkpoint(f"{l}a", l, rest_pos, [g_win, g_wq, g_wkv, g_wout, g_g2, g_u2, g_d2], dx,
                           early[0][4] if l == 0 else None)
        dy, dgt, dup = ffn_bwd_act(dx, sv["sl1"], sv["dsu1"], mod[l, 2] + order, d1)
        dx, dvec1 = ffn_bwd_in(dx, sv["x0"], sv["y1"], dgt, dup, row(ffn1_norm, l), mod[l, 1], g1, u1)
        g_g1, g_u1, g_d1 = tn_mm(dgt, sv["h1"][None]), tn_mm(dup, sv["h1"][None]), tn_mm(sv["a1"], dy[None])
        dmods[l] = jnp.concatenate([dvec1[0:3], dmod_rest], axis=0)
        dnorm1[l] = dvec1[3]
        if l == 0:
            late = small_gather("late", [dvec1[0:3], dvec1[3]], dx)
        order = checkpoint(f"{l}b", l, ffn1_pos, [g_g1, g_u1, g_d1], dx, late[0][4] if l == 0 else None)

    to_chips(stages[-1], stages[-2]["scatter"][4], stages[-2]["scatter"][4])
    sent = stages[-1]["scatter"][4]
    got_early = gather8_wait("early", *early[0][:4], sent)
    got_late = gather8_wait("late", *late[0][:4], sent)
    (g_dmod_rest, g_dmod0_rest, g_n1_rest, g_n2, g_n3, g_fn, g_ps, g_qan, g_kvan, g_pw) = _unpack_rows(
        sum_devices(got_early), early[1])
    g_dmod0_first, g_n1_first = _unpack_rows(sum_devices(got_late), late[1])
    g_ada_b = jnp.concatenate([jnp.concatenate([g_dmod0_first, g_dmod0_rest], axis=0)[None], g_dmod_rest], axis=0)
    g_n1 = jnp.concatenate([g_n1_first[None], g_n1_rest], axis=0)
    each_rest, each0_rest = _unpack_rows(got_early, early[1])[:2]
    each0_first = _unpack_rows(got_late, late[1])[0]
    dmod_all = jnp.concatenate([jnp.concatenate([each0_first, each0_rest], axis=1)[:, None], each_rest], axis=1)
    dmod_all = dmod_all.reshape(N_DEV, n_layers, 9 * d)
    dmod_loc = lax.dynamic_slice_in_dim(dmod_all, chip * ada_cols, ada_cols, axis=2)
    dmod16 = jnp.pad(jnp.transpose(dmod_loc, (1, 0, 2)), ((0, 0), (0, 8), (0, 0)))
    g_ada_w = ada_bwd(c16, dmod16)

    grads = [g_ada_w, g_ada_b, g_n1, None, None, None, g_n2, None, g_pw, g_ps, g_qan, None, g_kvan, None, None, g_n3,
             None, None, None, g_fn]
    weights = [ada_w, ada_b, ffn1_norm, ffn1_w_gate, ffn1_w_up, ffn1_w_down, mix_norm, w_in, pool_w, pool_scale,
               q_a_norm, w_q_b, kv_a_norm, w_kv_b, w_out, ffn2_norm, ffn2_w_gate, ffn2_w_up, ffn2_w_down, final_norm]
    ms = [m_ada_w, m_ada_b, m_ffn1_norm, m_ffn1_w_gate, m_ffn1_w_up, m_ffn1_w_down, m_mix_norm, m_w_in, m_pool_w,
          m_pool_scale, m_q_a_norm, m_w_q_b, m_kv_a_norm, m_w_kv_b, m_w_out, m_ffn2_norm, m_ffn2_w_gate, m_ffn2_w_up,
          m_ffn2_w_down, m_final_norm]
    vs = [v_ada_w, v_ada_b, v_ffn1_norm, v_ffn1_w_gate, v_ffn1_w_up, v_ffn1_w_down, v_mix_norm, v_w_in, v_pool_w,
          v_pool_scale, v_q_a_norm, v_w_q_b, v_kv_a_norm, v_w_kv_b, v_w_out, v_ffn2_norm, v_ffn2_w_gate, v_ffn2_w_up,
          v_ffn2_w_down, v_final_norm]
    transposed = (3, 4, 7, 11, 16, 17)
    outs = [None] * len(weights)
    for i, (w, g, m, v) in enumerate(zip(weights, grads, ms, vs)):
        if g is not None:
            outs[i] = adamw(w, g.reshape(w.shape), m, v)
    big = [i for i, g in enumerate(grads) if g is None]

    def update(positions):
        filled = pair_fill_halves([reduced[t] for t in positions])
        for t, g in zip(positions, filled):
            i = big[t]
            if i in transposed:
                outs[i] = tuple(tr(o) for o in adamw(tr(weights[i]), g, tr(ms[i]), tr(vs[i]), copy_g=True))
            else:
                outs[i] = adamw(weights[i], g, ms[i], vs[i], copy_g=True)

    finish(stages[-3], outs[0][1])
    finish(stages[-2], outs[0][1])
    update(rest_pos)
    finish(stages[-1], outs[big[rest_pos[-1]]][1])
    update(ffn1_pos)
    return (loss, dx.reshape(x.shape), *[t[0] for t in outs], *[t[1] for t in outs], *[t[2] for t in outs],
            *[t[3] for t in outs])
```

```python
import math

import jax
import jax.numpy as jnp
from jax import lax
from jax.experimental import pallas as pl
from jax.experimental.pallas import tpu as pltpu

F32 = jnp.float32
BF16 = jnp.bfloat16
MESH = pl.DeviceIdType.MESH

EPS = 1e-6
ROPE_THETA = 10000.0
N_HEADS = 4
QK_NOPE = 128
QK_ROPE = 64
V_HEAD = 128
POOL_WINDOWS = (2, 4, 8, 16)
POOL_GC = 128
POOL_WIDTH = POOL_GC * len(POOL_WINDOWS)
Q_LORA = 384
KV_LORA = 256
SOFTMAX_SCALE = 1.0 / math.sqrt(QK_NOPE + QK_ROPE)
N_CHIPS = 4
N_DEV = 8

ADAM_LR = 0.001
ADAM_B1 = 0.9
ADAM_B2 = 0.999
ADAM_EPS = 1e-08
ADAM_WD = 0.01
ADAM_STEP = 10

ROW_TILE = 512
ATT_TILE = 512
VMEM_LIMIT = 56 * 1024 * 1024
BF16_ROWS = 16
LANES = 128


def _params(sem=None, vmem=VMEM_LIMIT):
    return pltpu.CompilerParams(dimension_semantics=sem, vmem_limit_bytes=vmem)


def _dot(a, b):
    return jnp.dot(a, b, preferred_element_type=F32)


def _dot_nt(a, b):
    return lax.dot_general(a, b, (((1,), (1,)), ((), ())), preferred_element_type=F32)


def _dot_tn(a, b):
    return lax.dot_general(a, b, (((0,), (0,)), ((), ())), preferred_element_type=F32)


def _dot_exact(t, perm):
    t1 = t.astype(BF16)
    r1 = t - t1.astype(F32)
    t2 = r1.astype(BF16)
    t3 = (r1 - t2.astype(F32)).astype(BF16)
    return _dot(t1, perm) + _dot(t2, perm) + _dot(t3, perm)


def _sum0(a):
    return jnp.sum(a, axis=0, keepdims=True)


def _rms(xt):
    r = lax.rsqrt(jnp.mean(xt * xt, axis=-1, keepdims=True) + EPS)
    return xt * r, r


def _rms_bwd(dy, xt, g):
    xhat, r = _rms(xt)
    dxhat = dy * g
    dx = r * (dxhat - xhat * jnp.mean(dxhat * xhat, axis=-1, keepdims=True))
    return dx, _sum0(dy * xhat)


def _normmod_bwd(dh, xt, gn, sc):
    xhat, _ = _rms(xt)
    dn = dh * (1.0 + sc)
    dx, dgn = _rms_bwd(dn, xt, gn)
    return dx, _sum0(dh), _sum0(dh * (xhat * gn)), dgn


def _row_tile(s):
    return min(s, ROW_TILE)


def _full(shape):
    n = len(shape)
    return pl.BlockSpec(shape, lambda *_: (0,) * n)


def _resident(shape):
    n = len(shape)
    return pl.BlockSpec(shape, lambda *_: (0,) * n, pipeline_mode=pl.Buffered(1))


def ffn_fwd(x, gn, sh, sc, gt, wg, wu, wd):
    s, d = x.shape
    k_chunks, fs, _ = wg.shape
    tm = _row_tile(s)

    def body(x_ref, gn_ref, sh_ref, sc_ref, gt_ref, wg_ref, wu_ref, wd_ref,
             xo_ref, h_ref, a_ref, sl_ref, dsu_ref, y_ref):
        xt = x_ref[...]
        xhat, _ = _rms(xt)
        h = (xhat * gn_ref[...] * (1.0 + sc_ref[...]) + sh_ref[...]).astype(BF16)
        h_ref[...] = h
        y = jnp.zeros((tm, d), F32)
        for k in range(k_chunks):
            gate = _dot_nt(h, wg_ref[k])
            up = _dot_nt(h, wu_ref[k])
            sg = jax.nn.sigmoid(gate)
            sl = gate * sg
            a = (sl * up).astype(BF16)
            a_ref[k] = a
            sl_ref[k] = sl.astype(BF16)
            dsu_ref[k] = (up * (sg * (1.0 + gate * (1.0 - sg)))).astype(BF16)
            y += _dot(a, wd_ref[k])
        y_ref[...] = y.astype(BF16)
        xo_ref[...] = xt + 0.5 * gt_ref[...] * y

    row = pl.BlockSpec((tm, d), lambda i: (i, 0))
    vec = pl.BlockSpec((1, d), lambda i: (0, 0))
    act = pl.BlockSpec((k_chunks, tm, fs), lambda i: (0, i, 0))
    act_shape = jax.ShapeDtypeStruct((k_chunks, s, fs), BF16)
    return pl.pallas_call(
        body, name="ffn_fwd",
        grid=(s // tm,),
        in_specs=[row, vec, vec, vec, vec, _resident(wg.shape), _resident(wu.shape), _resident(wd.shape)],
        out_specs=[row, row, act, act, act, row],
        out_shape=[jax.ShapeDtypeStruct((s, d), F32), jax.ShapeDtypeStruct((s, d), BF16),
                   act_shape, act_shape, act_shape, jax.ShapeDtypeStruct((s, d), BF16)],
        compiler_params=_params(("arbitrary",)),
    )(x, gn, sh, sc, gt, wg, wu, wd)


def ffn_bwd_act(dxn, sl, dsu, gt, wd):
    s, d = dxn.shape
    k_chunks, fs, _ = wd.shape
    tm = _row_tile(s)

    def body(dxn_ref, sl_ref, dsu_ref, gt_ref, wd_ref, dy_ref, dgate_ref, dup_ref):
        dy = (0.5 * gt_ref[...] * dxn_ref[...]).astype(BF16)
        dy_ref[...] = dy
        for k in range(k_chunks):
            da = _dot_nt(dy, wd_ref[k])
            dgate_ref[k] = (da * dsu_ref[k].astype(F32)).astype(BF16)
            dup_ref[k] = (da * sl_ref[k].astype(F32)).astype(BF16)

    row = pl.BlockSpec((tm, d), lambda i: (i, 0))
    act = pl.BlockSpec((k_chunks, tm, fs), lambda i: (0, i, 0))
    act_shape = jax.ShapeDtypeStruct((k_chunks, s, fs), BF16)
    return pl.pallas_call(
        body, name="ffn_bwd_act",
        grid=(s // tm,),
        in_specs=[row, act, act, pl.BlockSpec((1, d), lambda i: (0, 0)), _resident(wd.shape)],
        out_specs=[row, act, act],
        out_shape=[jax.ShapeDtypeStruct((s, d), BF16), act_shape, act_shape],
        compiler_params=_params(("arbitrary",)),
    )(dxn, sl, dsu, gt, wd)


def ffn_bwd_in(dxn, x, y, dgate, dup, gn, sc, wg, wu):
    s, d = x.shape
    k_chunks, fs, _ = wg.shape
    tm = _row_tile(s)

    def body(dxn_ref, x_ref, y_ref, dgate_ref, dup_ref, gn_ref, sc_ref, wg_ref, wu_ref, dx_ref, dvec_ref):
        i = pl.program_id(0)

        @pl.when(i == 0)
        def _():
            dvec_ref[...] = jnp.zeros_like(dvec_ref)

        dh = jnp.zeros((tm, d), F32)
        for k in range(k_chunks):
            dh += _dot(dgate_ref[k], wg_ref[k]) + _dot(dup_ref[k], wu_ref[k])
        dxn_t = dxn_ref[...]
        dx, dsh, dsc, dgn = _normmod_bwd(dh, x_ref[...], gn_ref[...], sc_ref[...])
        dx_ref[...] = dx + dxn_t
        dvec_ref[0:1, :] += dsh
        dvec_ref[1:2, :] += dsc
        dvec_ref[2:3, :] += _sum0(0.5 * dxn_t * y_ref[...].astype(F32))
        dvec_ref[3:4, :] += dgn

    row = pl.BlockSpec((tm, d), lambda i: (i, 0))
    vec = pl.BlockSpec((1, d), lambda i: (0, 0))
    act = pl.BlockSpec((k_chunks, tm, fs), lambda i: (0, i, 0))
    return pl.pallas_call(
        body, name="ffn_bwd_in",
        grid=(s // tm,),
        in_specs=[row, row, row, act, act, vec, vec, _resident(wg.shape), _resident(wu.shape)],
        out_specs=[row, pl.BlockSpec((8, d), lambda i: (0, 0))],
        out_shape=[jax.ShapeDtypeStruct((s, d), F32), jax.ShapeDtypeStruct((8, d), F32)],
        compiler_params=_params(("arbitrary",)),
    )(dxn, x, y, dgate, dup, gn, sc, wg, wu)


def tn_mm(a, b):
    ga, s, m = a.shape
    gb, _, n = b.shape
    g = max(ga, gb)

    def body(a_ref, b_ref, o_ref):
        o_ref[...] = _dot_tn(a_ref[...], b_ref[...])

    a_spec = pl.BlockSpec((None, s, m), (lambda gi: (gi, 0, 0)) if ga > 1 else (lambda gi: (0, 0, 0)))
    b_spec = pl.BlockSpec((None, s, n), (lambda gi: (gi, 0, 0)) if gb > 1 else (lambda gi: (0, 0, 0)))
    return pl.pallas_call(
        body, name="tn_mm",
        grid=(g,), in_specs=[a_spec, b_spec], out_specs=pl.BlockSpec((None, m, n), lambda gi: (gi, 0, 0)),
        out_shape=jax.ShapeDtypeStruct((g, m, n), F32),
        compiler_params=_params(("arbitrary",)),
    )(a, b)


def mix_in_fwd(x, gn, sh, sc, w_in_t):
    s, d = x.shape
    tm = _row_tile(s)
    o1, o2, o3 = POOL_WIDTH, POOL_WIDTH + Q_LORA, POOL_WIDTH + Q_LORA + KV_LORA

    def body(x_ref, gn_ref, sh_ref, sc_ref, w_ref, h_ref, u_ref, cq_ref, ckv_ref, kr_ref):
        xhat, _ = _rms(x_ref[...])
        h = (xhat * gn_ref[...] * (1.0 + sc_ref[...]) + sh_ref[...]).astype(BF16)
        h_ref[...] = h
        z = _dot_nt(h, w_ref[0:o3, :])
        u_ref[...] = z[:, 0:o1]
        cq_ref[...] = z[:, o1:o2]
        ckv_ref[...] = z[:, o2:o3]
        kr_ref[...] = _dot_nt(h, w_ref[o3:, :])

    row = lambda w: pl.BlockSpec((tm, w), lambda i: (i, 0))
    vec = pl.BlockSpec((1, d), lambda i: (0, 0))
    return pl.pallas_call(
        body, name="mix_in_fwd",
        grid=(s // tm,),
        in_specs=[row(d), vec, vec, vec, _full(w_in_t.shape)],
        out_specs=[row(d), row(POOL_WIDTH), row(Q_LORA), row(KV_LORA), row(QK_ROPE)],
        out_shape=[jax.ShapeDtypeStruct((s, d), BF16), jax.ShapeDtypeStruct((s, POOL_WIDTH), F32),
                   jax.ShapeDtypeStruct((s, Q_LORA), F32), jax.ShapeDtypeStruct((s, KV_LORA), F32),
                   jax.ShapeDtypeStruct((s, QK_ROPE), F32)],
        compiler_params=_params(("arbitrary",)),
    )(x, gn, sh, sc, w_in_t)


def mix_in_bwd(dxn, du, dcq, dckv, dkr, x, gn, sc, w_in_t):
    s, d = x.shape
    tm = _row_tile(s)
    o1, o2, o3 = POOL_WIDTH, POOL_WIDTH + Q_LORA, POOL_WIDTH + Q_LORA + KV_LORA
    n_z = w_in_t.shape[0]

    def body(dxn_ref, du_ref, dcq_ref, dckv_ref, dkr_ref, x_ref, gn_ref, sc_ref, w_ref, dx_ref, dz_ref, dvec_ref):
        i = pl.program_id(0)

        @pl.when(i == 0)
        def _():
            dvec_ref[...] = jnp.zeros_like(dvec_ref)

        dub = du_ref[...].astype(BF16)
        dqb = dcq_ref[...].astype(BF16)
        dkb = dckv_ref[...].astype(BF16)
        drb = dkr_ref[...].astype(BF16)
        dz_ref[:, 0:o1] = dub
        dz_ref[:, o1:o2] = dqb
        dz_ref[:, o2:o3] = dkb
        dz_ref[:, o3:] = drb
        dh = (_dot(dub, w_ref[0:o1, :]) + _dot(dqb, w_ref[o1:o2, :]) + _dot(dkb, w_ref[o2:o3, :])
              + _dot(drb, w_ref[o3:, :]))
        dx, dsh, dsc, dgn = _normmod_bwd(dh, x_ref[...], gn_ref[...], sc_ref[...])
        dx_ref[...] = dx + dxn_ref[...]
        dvec_ref[0:1, :] += dsh
        dvec_ref[1:2, :] += dsc
        dvec_ref[3:4, :] += dgn

    row = lambda w: pl.BlockSpec((tm, w), lambda i: (i, 0))
    vec = pl.BlockSpec((1, d), lambda i: (0, 0))
    return pl.pallas_call(
        body, name="mix_in_bwd",
        grid=(s // tm,),
        in_specs=[row(d), row(POOL_WIDTH), row(Q_LORA), row(KV_LORA), row(QK_ROPE), row(d), vec, vec,
                  _full(w_in_t.shape)],
        out_specs=[row(d), row(n_z), pl.BlockSpec((8, d), lambda i: (0, 0))],
        out_shape=[jax.ShapeDtypeStruct((s, d), F32), jax.ShapeDtypeStruct((s, n_z), BF16),
                   jax.ShapeDtypeStruct((8, d), F32)],
        compiler_params=_params(("arbitrary",)),
    )(dxn, du, dcq, dckv, dkr, x, gn, sc, w_in_t)


def _window_sum(a, w, rows, forward):
    s = a.shape[0]
    step = 1
    while step < w:
        if forward:
            shifted = jnp.where(rows < s - step, pltpu.roll(a, s - step, 0), 0.0)
        else:
            shifted = jnp.where(rows >= step, pltpu.roll(a, step, 0), 0.0)
        a = a + shifted
        step *= 2
    return a


def pool_fwd(u, pool_w, pool_scale):
    s = u.shape[0]

    def body(u_ref, w_ref, sc_ref, y_ref, diff_ref):
        rows = lax.broadcasted_iota(jnp.int32, (s, POOL_GC), 0)
        for g, w in enumerate(POOL_WINDOWS):
            cols = slice(g * POOL_GC, (g + 1) * POOL_GC)
            ug = u_ref[:, cols]
            cnt = jnp.minimum(rows + 1, w).astype(F32)
            diff = (_window_sum(ug, w, rows, False) / cnt - ug).astype(BF16)
            diff_ref[:, cols] = diff
            y_ref[:, cols] = _dot(diff, w_ref[g].astype(BF16)) * sc_ref[:, cols]

    return pl.pallas_call(
        body, name="pool_fwd",
        out_shape=[jax.ShapeDtypeStruct(u.shape, F32), jax.ShapeDtypeStruct(u.shape, BF16)],
        compiler_params=_params(),
    )(u, pool_w, pool_scale)


def pool_bwd(dy, diff, pool_w, pool_scale):
    s = dy.shape[0]

    def body(dy_ref, diff_ref, w_ref, sc_ref, du_ref, dw_ref, dsc_ref):
        rows = lax.broadcasted_iota(jnp.int32, (s, POOL_GC), 0)
        for g, w in enumerate(POOL_WINDOWS):
            cols = slice(g * POOL_GC, (g + 1) * POOL_GC)
            dyg = dy_ref[:, cols]
            diff = diff_ref[:, cols]
            wb = w_ref[g].astype(BF16)
            dsc_ref[:, cols] = _sum0(dyg * _dot(diff, wb))
            dys = (dyg * sc_ref[:, cols]).astype(BF16)
            dw_ref[g] = _dot_tn(diff, dys)
            ddiff = _dot_nt(dys, wb)
            cnt = jnp.minimum(rows + 1, w).astype(F32)
            du_ref[:, cols] = _window_sum(ddiff / cnt, w, rows, True) - ddiff

    return pl.pallas_call(
        body, name="pool_bwd",
        out_shape=[jax.ShapeDtypeStruct(dy.shape, F32), jax.ShapeDtypeStruct(pool_w.shape, F32),
                   jax.ShapeDtypeStruct(pool_scale.shape, F32)],
        compiler_params=_params(),
    )(dy, diff, pool_w, pool_scale)


def mla_qkv_fwd(cq, ckv, kr, qan, kvan, wq, wkv, cos, sin, rot):
    s = cq.shape[0]
    tm = _row_tile(s)

    def body(cq_ref, ckv_ref, kr_ref, qan_ref, kvan_ref, wq_ref, wkv_ref, cos_ref, sin_ref, rot_ref,
             q_ref, k_ref, v_ref, ql_ref, kvl_ref):
        cos_t = cos_ref[...]
        sin_t = sin_ref[...]
        perm = rot_ref[...]

        def rope(t):
            return t * cos_t + _dot_exact(t, perm) * sin_t

        qhat, _ = _rms(cq_ref[...])
        ql = (qhat * qan_ref[...]).astype(BF16)
        ql_ref[...] = ql
        khat, _ = _rms(ckv_ref[...])
        kvl = (khat * kvan_ref[...]).astype(BF16)
        kvl_ref[...] = kvl
        krr = rope(kr_ref[...]).astype(BF16)
        for h in range(N_HEADS):
            q = _dot_nt(ql, wq_ref[h])
            q_ref[h, :, 0:QK_NOPE] = q[:, 0:QK_NOPE].astype(BF16)
            q_ref[h, :, QK_NOPE:] = rope(q[:, QK_NOPE:]).astype(BF16)
            kv = _dot(kvl, wkv_ref[h])
            k_ref[h, :, 0:QK_NOPE] = kv[:, 0:QK_NOPE].astype(BF16)
            k_ref[h, :, QK_NOPE:] = krr
            v_ref[h] = kv[:, QK_NOPE:].astype(BF16)

    row = lambda w: pl.BlockSpec((tm, w), lambda i: (i, 0))
    hrow = lambda w: pl.BlockSpec((N_HEADS, tm, w), lambda i: (0, i, 0))
    qk = QK_NOPE + QK_ROPE
    return pl.pallas_call(
        body, name="mla_qkv_fwd",
        grid=(s // tm,),
        in_specs=[row(Q_LORA), row(KV_LORA), row(QK_ROPE), _full(qan.shape), _full(kvan.shape),
                  _full(wq.shape), _full(wkv.shape), row(QK_ROPE), row(QK_ROPE), _full(rot.shape)],
        out_specs=[hrow(qk), hrow(qk), hrow(V_HEAD), row(Q_LORA), row(KV_LORA)],
        out_shape=[jax.ShapeDtypeStruct((N_HEADS, s, qk), BF16), jax.ShapeDtypeStruct((N_HEADS, s, qk), BF16),
                   jax.ShapeDtypeStruct((N_HEADS, s, V_HEAD), BF16), jax.ShapeDtypeStruct((s, Q_LORA), BF16),
                   jax.ShapeDtypeStruct((s, KV_LORA), BF16)],
        compiler_params=_params(("arbitrary",)),
    )(cq, ckv, kr, qan, kvan, wq, wkv, cos, sin, rot)


def _attn_probs(q_ref, k_ref, qi, tq):
    n = (qi + 1) * tq
    rows = slice(qi * tq, n)
    sc = _dot_nt(q_ref[rows, :], k_ref[0:n, :]) * SOFTMAX_SCALE
    qpos = qi * tq + lax.broadcasted_iota(jnp.int32, (tq, n), 0)
    kpos = lax.broadcasted_iota(jnp.int32, (tq, n), 1)
    sc = jnp.where(qpos >= kpos, sc, -1e30)
    e = jnp.exp(sc - jnp.max(sc, axis=-1, keepdims=True))
    return e / jnp.sum(e, axis=-1, keepdims=True)


def attn_fwd(q, k, v):
    nh, s, qk = q.shape
    tq = min(s, ATT_TILE)

    def body(q_ref, k_ref, v_ref, o_ref):
        for qi in range(s // tq):
            n = (qi + 1) * tq
            p = _attn_probs(q_ref, k_ref, qi, tq).astype(BF16)
            o_ref[qi * tq:n, :] = _dot(p, v_ref[0:n, :])

    head = lambda w: pl.BlockSpec((None, s, w), lambda h: (h, 0, 0))
    return pl.pallas_call(
        body, name="attn_fwd",
        grid=(nh,),
        in_specs=[head(qk), head(qk), head(V_HEAD)],
        out_specs=pl.BlockSpec((s, V_HEAD), lambda h: (0, h)),
        out_shape=jax.ShapeDtypeStruct((s, nh * V_HEAD), F32),
        compiler_params=_params(("arbitrary",)),
    )(q, k, v)


def attn_bwd(q, k, v, do):
    nh, s, qk = q.shape
    tq = min(s, ATT_TILE)

    def body(q_ref, k_ref, v_ref, do_ref, dq_ref, dk_ref, dv_ref):
        dk_ref[...] = jnp.zeros_like(dk_ref)
        dv_ref[...] = jnp.zeros_like(dv_ref)
        for qi in range(s // tq):
            n = (qi + 1) * tq
            rows = slice(qi * tq, n)
            p = _attn_probs(q_ref, k_ref, qi, tq)
            dob = do_ref[rows, :].astype(BF16)
            dp = _dot_nt(dob, v_ref[0:n, :])
            ds = (p * (dp - jnp.sum(p * dp, axis=-1, keepdims=True)) * SOFTMAX_SCALE).astype(BF16)
            dq_ref[rows, :] = _dot(ds, k_ref[0:n, :])
            dk_ref[0:n, :] += _dot_tn(ds, q_ref[rows, :])
            dv_ref[0:n, :] += _dot_tn(p.astype(BF16), dob)

    head = lambda w: pl.BlockSpec((None, s, w), lambda h: (h, 0, 0))
    return pl.pallas_call(
        body, name="attn_bwd",
        grid=(nh,),
        in_specs=[head(qk), head(qk), head(V_HEAD), pl.BlockSpec((s, V_HEAD), lambda h: (0, h))],
        out_specs=[head(qk), head(qk), head(V_HEAD)],
        out_shape=[jax.ShapeDtypeStruct((nh, s, qk), F32), jax.ShapeDtypeStruct((nh, s, qk), F32),
                   jax.ShapeDtypeStruct((nh, s, V_HEAD), F32)],
        compiler_params=_params(("arbitrary",)),
    )(q, k, v, do)


def mla_qkv_bwd(dq, dk, dv, cq, ckv, qan, kvan, wq, wkv, cos, sin, rot_t):
    s = cq.shape[0]
    tm = _row_tile(s)

    def body(dq_ref, dk_ref, dv_ref, cq_ref, ckv_ref, qan_ref, kvan_ref,
             wq_ref, wkv_ref, cos_ref, sin_ref, rot_ref,
             dcq_ref, dckv_ref, dkro_ref, gq_ref, gkv_ref, dqan_ref, dkvan_ref):
        i = pl.program_id(0)

        @pl.when(i == 0)
        def _():
            dqan_ref[...] = jnp.zeros_like(dqan_ref)
            dkvan_ref[...] = jnp.zeros_like(dkvan_ref)

        cos_t = cos_ref[...]
        sin_t = sin_ref[...]
        perm_t = rot_ref[...]

        def unrope(t):
            return t * cos_t + _dot_exact(t * sin_t, perm_t)

        acc_q = jnp.zeros((tm, Q_LORA), F32)
        acc_kv = jnp.zeros((tm, KV_LORA), F32)
        dkr_sum = jnp.zeros((tm, QK_ROPE), F32)
        for h in range(N_HEADS):
            dq_h = dq_ref[h]
            a = dq_h[:, 0:QK_NOPE].astype(BF16)
            b = unrope(dq_h[:, QK_NOPE:]).astype(BF16)
            gq_ref[h, :, 0:QK_NOPE] = a
            gq_ref[h, :, QK_NOPE:] = b
            wq_h = wq_ref[h]
            acc_q += _dot(a, wq_h[0:QK_NOPE, :]) + _dot(b, wq_h[QK_NOPE:, :])
            dk_h = dk_ref[h]
            dk = dk_h[:, 0:QK_NOPE].astype(BF16)
            dvv = dv_ref[h].astype(BF16)
            gkv_ref[h, :, 0:QK_NOPE] = dk
            gkv_ref[h, :, QK_NOPE:] = dvv
            wkv_h = wkv_ref[h]
            acc_kv += _dot_nt(dk, wkv_h[:, 0:QK_NOPE]) + _dot_nt(dvv, wkv_h[:, QK_NOPE:])
            dkr_sum += dk_h[:, QK_NOPE:]
        dkro_ref[...] = unrope(dkr_sum)
        dcq, dqan = _rms_bwd(acc_q, cq_ref[...], qan_ref[...])
        dcq_ref[...] = dcq
        dqan_ref[...] += dqan
        dckv, dkvan = _rms_bwd(acc_kv, ckv_ref[...], kvan_ref[...])
        dckv_ref[...] = dckv
        dkvan_ref[...] += dkvan

    row = lambda w: pl.BlockSpec((tm, w), lambda i: (i, 0))
    hrow = lambda w: pl.BlockSpec((N_HEADS, tm, w), lambda i: (0, i, 0))
    return pl.pallas_call(
        body, name="mla_qkv_bwd",
        grid=(s // tm,),
        in_specs=[hrow(QK_NOPE + QK_ROPE), hrow(QK_NOPE + QK_ROPE), hrow(V_HEAD),
                  row(Q_LORA), row(KV_LORA), _full(qan.shape), _full(kvan.shape),
                  _full(wq.shape), _full(wkv.shape), row(QK_ROPE), row(QK_ROPE), _full(rot_t.shape)],
        out_specs=[row(Q_LORA), row(KV_LORA), row(QK_ROPE), hrow(QK_NOPE + QK_ROPE), hrow(QK_NOPE + V_HEAD),
                   _full(qan.shape), _full(kvan.shape)],
        out_shape=[jax.ShapeDtypeStruct((s, Q_LORA), F32), jax.ShapeDtypeStruct((s, KV_LORA), F32),
                   jax.ShapeDtypeStruct((s, QK_ROPE), F32),
                   jax.ShapeDtypeStruct((N_HEADS, s, QK_NOPE + QK_ROPE), BF16),
                   jax.ShapeDtypeStruct((N_HEADS, s, QK_NOPE + V_HEAD), BF16),
                   jax.ShapeDtypeStruct(qan.shape, F32), jax.ShapeDtypeStruct(kvan.shape, F32)],
        compiler_params=_params(("arbitrary",)),
    )(dq, dk, dv, cq, ckv, qan, kvan, wq, wkv, cos, sin, rot_t)


def out_proj_fwd(yp, om, w_out, x, gt):
    s, d = x.shape
    n_sh, rs, _ = w_out.shape
    tm = _row_tile(s)
    per = POOL_WIDTH // rs

    def body(yp_ref, om_ref, w_ref, x_ref, gt_ref, xo_ref, ycat_ref, y_ref):
        y = jnp.zeros((tm, d), F32)
        for j in range(n_sh):
            src = yp_ref if j < per else om_ref
            part = src[:, (j % per) * rs:(j % per + 1) * rs].astype(BF16)
            ycat_ref[j] = part
            y += _dot(part, w_ref[j])
        y_ref[...] = y.astype(BF16)
        xo_ref[...] = x_ref[...] + gt_ref[...] * y

    row = lambda w: pl.BlockSpec((tm, w), lambda i: (i, 0))
    return pl.pallas_call(
        body, name="out_proj_fwd",
        grid=(s // tm,),
        in_specs=[row(POOL_WIDTH), row(POOL_WIDTH), _full(w_out.shape), row(d), pl.BlockSpec((1, d), lambda i: (0, 0))],
        out_specs=[row(d), pl.BlockSpec((n_sh, tm, rs), lambda i: (0, i, 0)), row(d)],
        out_shape=[jax.ShapeDtypeStruct((s, d), F32), jax.ShapeDtypeStruct((n_sh, s, rs), BF16),
                   jax.ShapeDtypeStruct((s, d), BF16)],
        compiler_params=_params(("arbitrary",)),
    )(yp, om, w_out, x, gt)


def out_proj_bwd(dxn, y, gt, w_out):
    s, d = dxn.shape
    n_sh, rs, _ = w_out.shape
    tm = _row_tile(s)
    per = POOL_WIDTH // rs

    def body(dxn_ref, y_ref, gt_ref, w_ref, dy_ref, dyp_ref, dom_ref, dgt_ref):
        i = pl.program_id(0)

        @pl.when(i == 0)
        def _():
            dgt_ref[...] = jnp.zeros_like(dgt_ref)

        dxn_t = dxn_ref[...]
        dy = (gt_ref[...] * dxn_t).astype(BF16)
        dy_ref[...] = dy
        dgt_ref[...] += _sum0(dxn_t * y_ref[...].astype(F32))
        for j in range(n_sh):
            dst = dyp_ref if j < per else dom_ref
            dst[:, (j % per) * rs:(j % per + 1) * rs] = _dot_nt(dy, w_ref[j])

    row = lambda w: pl.BlockSpec((tm, w), lambda i: (i, 0))
    vec = pl.BlockSpec((1, d), lambda i: (0, 0))
    return pl.pallas_call(
        body, name="out_proj_bwd",
        grid=(s // tm,),
        in_specs=[row(d), row(d), vec, _full(w_out.shape)],
        out_specs=[row(d), row(POOL_WIDTH), row(POOL_WIDTH), vec],
        out_shape=[jax.ShapeDtypeStruct((s, d), BF16), jax.ShapeDtypeStruct((s, POOL_WIDTH), F32),
                   jax.ShapeDtypeStruct((s, POOL_WIDTH), F32), jax.ShapeDtypeStruct((1, d), F32)],
        compiler_params=_params(("arbitrary",)),
    )(dxn, y, gt, w_out)


def final_loss(x, gn, tgt):
    s, d = x.shape
    tm = _row_tile(s)

    def body(x_ref, gn_ref, t_ref, loss_ref, dx_ref, dgn_ref):
        i = pl.program_id(0)

        @pl.when(i == 0)
        def _():
            loss_ref[...] = jnp.zeros_like(loss_ref)
            dgn_ref[...] = jnp.zeros_like(dgn_ref)

        xt = x_ref[...]
        g = gn_ref[...]
        xhat, _ = _rms(xt)
        err = xhat * g - t_ref[...]
        per_tok = jnp.mean(err * err, axis=-1, keepdims=True)
        loss_ref[...] += jnp.broadcast_to(0.5 * _sum0(per_tok), loss_ref.shape)
        dx, dgn = _rms_bwd(err * (1.0 / d), xt, g)
        dx_ref[...] = dx
        dgn_ref[...] += dgn

    row = pl.BlockSpec((tm, d), lambda i: (i, 0))
    vec = pl.BlockSpec((1, d), lambda i: (0, 0))
    return pl.pallas_call(
        body, name="final_loss",
        grid=(s // tm,),
        in_specs=[row, vec, row],
        out_specs=[pl.BlockSpec((1, LANES), lambda i: (0, 0)), row, vec],
        out_shape=[jax.ShapeDtypeStruct((1, LANES), F32), jax.ShapeDtypeStruct((s, d), F32),
                   jax.ShapeDtypeStruct((1, d), F32)],
        compiler_params=_params(("arbitrary",)),
    )(x, gn, tgt)


def _col_tile(cols):
    return 768 if cols % 768 == 0 else cols


def ada_fwd(c16, ada_w, ada_b_loc):
    n_layers, d, cols = ada_w.shape
    tn = _col_tile(cols)

    def body(c_ref, w_ref, b_ref, o_ref):
        cv = c_ref[...]
        ca = (cv * jax.nn.sigmoid(cv)).astype(BF16)
        o_ref[...] = _dot(ca, w_ref[...].astype(BF16)) + b_ref[...]

    return pl.pallas_call(
        body, name="ada_fwd",
        grid=(n_layers, cols // tn),
        in_specs=[pl.BlockSpec((16, d), lambda l, j: (0, 0)), pl.BlockSpec((None, d, tn), lambda l, j: (l, 0, j)),
                  pl.BlockSpec((None, 1, tn), lambda l, j: (l, 0, j))],
        out_specs=pl.BlockSpec((None, 16, tn), lambda l, j: (l, 0, j)),
        out_shape=jax.ShapeDtypeStruct((n_layers, 16, cols), F32),
        compiler_params=_params(("arbitrary", "arbitrary")),
    )(c16, ada_w, ada_b_loc)


def ada_bwd(c16, dmod16):
    n_layers, _, cols = dmod16.shape
    d = c16.shape[1]
    tn = _col_tile(cols)

    def body(c_ref, g_ref, o_ref):
        cv = c_ref[...]
        ca = (cv * jax.nn.sigmoid(cv)).astype(BF16)
        o_ref[...] = _dot_tn(ca, g_ref[...].astype(BF16))

    return pl.pallas_call(
        body, name="ada_bwd",
        grid=(n_layers, cols // tn),
        in_specs=[pl.BlockSpec((16, d), lambda l, j: (0, 0)), pl.BlockSpec((None, 16, tn), lambda l, j: (l, 0, j))],
        out_specs=pl.BlockSpec((None, d, tn), lambda l, j: (l, 0, j)),
        out_shape=jax.ShapeDtypeStruct((n_layers, d, cols), F32),
        compiler_params=_params(("arbitrary", "arbitrary")),
    )(c16, dmod16)


def _as_rows(a):
    if a.ndim == 1:
        return a.reshape(1, a.shape[0])
    return a.reshape(-1, a.shape[-1])


def _rows_tile(r, c, itemsize=4, budget=2 * 1024 * 1024):
    if r * c * itemsize <= budget:
        return r
    best = None
    t = BF16_ROWS
    while t < r:
        if r % t == 0 and t * c * itemsize <= budget:
            best = t
        t += BF16_ROWS
    return best if best is not None else r


def cast_place(w, chip, layers, after):
    _, r, c = w.shape
    n_sel = len(layers)
    tr = _rows_tile(r, c, budget=2 * 1024 * 1024 // n_sel)

    def body(chip_ref, *refs):
        for j in range(n_sel):
            refs[n_sel + 1 + j][...] = refs[j][...].astype(BF16)

    layer_spec = lambda l: pl.BlockSpec((None, tr, c), lambda i, ch: (l, i, 0))
    return list(pl.pallas_call(
        body, name="cast_place",
        grid_spec=pltpu.PrefetchScalarGridSpec(
            num_scalar_prefetch=1, grid=(r // tr,),
            in_specs=[layer_spec(l) for l in layers] + [pl.BlockSpec(memory_space=pl.ANY)],
            out_specs=[pl.BlockSpec((None, tr, c), lambda i, ch: (ch[0], i, 0))] * n_sel),
        out_shape=[jax.ShapeDtypeStruct((N_CHIPS, r, c), BF16)] * n_sel,
        compiler_params=_params(("arbitrary",)),
    )(chip, *([w] * n_sel), after))


def adamw(w, g, m, v, copy_g=False):
    shape = w.shape
    w2, g2, m2, v2 = (_as_rows(t) for t in (w, g, m, v))
    r, c = w2.shape
    tr = _rows_tile(r, c, budget=2 * 1024 * 1024)
    c1 = 1.0 - ADAM_B1 ** ADAM_STEP
    c2 = 1.0 - ADAM_B2 ** ADAM_STEP

    def body(w_ref, g_ref, m_ref, v_ref, d_ref, mo_ref, vo_ref, *go_ref):
        gv = g_ref[...]
        if copy_g:
            go_ref[0][...] = gv
        mn = ADAM_B1 * m_ref[...] + (1.0 - ADAM_B1) * gv
        vn = ADAM_B2 * v_ref[...] + (1.0 - ADAM_B2) * (gv * gv)
        mo_ref[...] = mn
        vo_ref[...] = vn
        d_ref[...] = -ADAM_LR * ((mn / c1) / (jnp.sqrt(vn / c2) + ADAM_EPS) + ADAM_WD * w_ref[...])

    spec = pl.BlockSpec((tr, c), lambda i: (i, 0))
    n_out = 4 if copy_g else 3
    outs = pl.pallas_call(
        body, name="adamw", grid=(r // tr,), in_specs=[spec] * 4, out_specs=[spec] * n_out,
        out_shape=[jax.ShapeDtypeStruct((r, c), F32)] * n_out, compiler_params=_params(("arbitrary",)),
    )(w2, g2, m2, v2)
    g_out = outs[3] if copy_g else g2
    return tuple(o.reshape(shape) for o in (g_out,) + tuple(outs[:3]))


def sum_devices(a):
    n, r, c = a.shape
    tr = _rows_tile(r, c, budget=512 * 1024)

    def body(a_ref, o_ref):
        acc = a_ref[0]
        for j in range(1, n):
            acc = acc + a_ref[j]
        o_ref[...] = acc

    return pl.pallas_call(
        body, name="sum_devices", grid=(r // tr,),
        in_specs=[pl.BlockSpec((n, tr, c), lambda i: (0, i, 0))], out_specs=pl.BlockSpec((tr, c), lambda i: (i, 0)),
        out_shape=jax.ShapeDtypeStruct((r, c), F32), compiler_params=_params(("arbitrary",)),
    )(a)


def _split_axis(r, c):
    if (r // 2) % BF16_ROWS == 0 and r % 2 == 0:
        return 0
    assert c % (2 * LANES) == 0, (r, c)
    return 1


def _half_shape(r, c):
    return (r // 2, c) if _split_axis(r, c) == 0 else (r, c // 2)


def _half_at(ref, lead, which):
    r, c = ref.shape[-2:]
    if _split_axis(r, c) == 0:
        return ref.at[(*lead, pl.ds(which * (r // 2), r // 2), slice(None))]
    return ref.at[(*lead, slice(None), pl.ds(which * (c // 2), c // 2))]


def _half_spec(r, c, lead_block, imap):
    hr, hc = _half_shape(r, c)
    if _split_axis(r, c) == 0:
        return pl.BlockSpec((*lead_block, hr, hc), lambda *a: (*imap(*a)[0], imap(*a)[1], 0))
    return pl.BlockSpec((*lead_block, hr, hc), lambda *a: (*imap(*a)[0], 0, imap(*a)[1]))


def pair_add(gs, ras, sel):
    n = len(gs)
    n_sl = gs[0].shape[0]
    halves = [_half_shape(*g.shape[1:]) for g in gs]

    def body(s_ref, *refs):
        g_refs, ra_refs, pb_refs, own_refs = (refs[i * n:(i + 1) * n] for i in range(4))
        k = pl.program_id(0)
        for t in range(n):
            p = g_refs[t][...] + ra_refs[t][...]
            pb_refs[t][...] = p.astype(BF16)

            @pl.when(k == s_ref[1])
            def _(p=p, own=own_refs[t]):
                own[...] = p

    slot = lambda hs: pl.BlockSpec((None,) + hs, lambda k, sr: (k, 0, 0))
    outs = pl.pallas_call(
        body, name="pair_add",
        grid_spec=pltpu.PrefetchScalarGridSpec(
            num_scalar_prefetch=1, grid=(n_sl,),
            in_specs=[_half_spec(*g.shape[1:], (None,), lambda k, sr: ((k,), sr[0])) for g in gs]
            + [slot(hs) for hs in halves],
            out_specs=[slot(hs) for hs in halves] + [pl.BlockSpec(hs, lambda k, sr: (0, 0)) for hs in halves]),
        out_shape=[jax.ShapeDtypeStruct((n_sl,) + hs, BF16) for hs in halves]
        + [jax.ShapeDtypeStruct(hs, F32) for hs in halves],
        compiler_params=_params(("arbitrary",)),
    )(sel, *gs, *ras)
    return list(outs[:n]), list(outs[n:])


def chip_sum(owns, rbs, sel, shapes, accs):
    n = len(owns)
    fresh = accs[0] is None

    def body(s_ref, *refs):
        own_refs, rb_refs, o_refs = refs[:n], refs[n:2 * n], refs[-n:]
        for t in range(n):
            acc_v = own_refs[t][...]
            for j in range(N_CHIPS - 1):
                acc_v = acc_v + rb_refs[t][j].astype(F32)
            o_refs[t][...] = acc_v

    in_specs = ([pl.BlockSpec(o.shape, lambda i, sr: (0, 0)) for o in owns]
                + [pl.BlockSpec(rb.shape, lambda i, sr: (0, 0, 0)) for rb in rbs])
    args = [sel, *owns, *rbs]
    aliases = {}
    if not fresh:
        in_specs += [pl.BlockSpec(memory_space=pl.ANY)] * n
        args += list(accs)
        aliases = {1 + 2 * n + t: t for t in range(n)}
    return list(pl.pallas_call(
        body, name="chip_sum",
        grid_spec=pltpu.PrefetchScalarGridSpec(
            num_scalar_prefetch=1, grid=(1,), in_specs=in_specs,
            out_specs=[_half_spec(*shp[1:], (None,), lambda i, sr: ((sr[2],), sr[0])) for shp in shapes]),
        out_shape=[jax.ShapeDtypeStruct(shp, F32) for shp in shapes],
        input_output_aliases=aliases,
        compiler_params=_params(("arbitrary",)),
    )(*args))


def _me():
    return lax.axis_index("x"), lax.axis_index("y"), lax.axis_index("c")


def _flip(v, bit):
    return 1 - v if bit else v


def exchange8(xs, bcast):
    blk = xs.shape if bcast else xs.shape[1:]

    def body(x_ref, o_ref, send_sems, recv_sems, loc_sem):
        mx, my, mc = _me()
        me = 4 * mx + 2 * my + mc
        src = (lambda j: x_ref) if bcast else (lambda j: x_ref.at[j])
        loc = pltpu.make_async_copy(src(me), o_ref.at[me], loc_sem)
        loc.start()
        copies = []
        for o in range(1, N_DEV):
            px, py, pc = _flip(mx, o & 4), _flip(my, o & 2), _flip(mc, o & 1)
            cp = pltpu.make_async_remote_copy(
                src_ref=src(4 * px + 2 * py + pc), dst_ref=o_ref.at[me],
                send_sem=send_sems.at[o - 1], recv_sem=recv_sems.at[o - 1],
                device_id=(px, py, pc), device_id_type=MESH)
            cp.start()
            copies.append(cp)
        for cp in copies:
            cp.wait()
        loc.wait()

    return pl.pallas_call(
        body, name="exchange8_gather" if bcast else "exchange8_a2a",
        in_specs=[pl.BlockSpec(memory_space=pltpu.VMEM)], out_specs=pl.BlockSpec(memory_space=pltpu.VMEM),
        out_shape=jax.ShapeDtypeStruct((N_DEV,) + tuple(blk), xs.dtype),
        scratch_shapes=[pltpu.SemaphoreType.DMA((N_DEV - 1,)), pltpu.SemaphoreType.DMA((N_DEV - 1,)), pltpu.SemaphoreType.DMA],
        compiler_params=_params(),
    )(xs)


HBM = pl.BlockSpec(memory_space=pltpu.HBM)
SEM = pl.BlockSpec(memory_space=pltpu.SEMAPHORE)
EFFECT = pltpu.SideEffectType.DATAFLOW_SIDE_EFFECTING


def _hbm(a):
    return pltpu.with_memory_space_constraint(a, pltpu.HBM)


def _ici_copy(land, o, send_sem, recv_sem, sending):
    mx, my, mc = _me()
    px, py = _flip(mx, o & 2), _flip(my, o & 1)
    mine = _half_at(land, (2 * mx + my,), mc)
    return pltpu.make_async_remote_copy(
        src_ref=mine, dst_ref=mine if sending else _half_at(land, (2 * px + py,), mc),
        send_sem=send_sem, recv_sem=recv_sem, device_id=(px, py, mc), device_id_type=MESH)


N_PEERS = N_CHIPS - 1
DMA_SEM = pltpu.SemaphoreType.DMA(())


def gather_start(lands, groups, after, tag):
    n_layers, n = len(lands), len(lands[0])
    flat = [a for layer in lands for a in layer]
    n_in = n * n_layers
    n_grp = len(groups)
    n_sem = 2 * n_layers * n_grp * N_PEERS
    first = lambda l, g, recv: ((l * n_grp + g) * 2 + recv) * N_PEERS

    def body(*refs):
        land = refs[:n_in]
        sems = refs[n_in + 1:n_in + 1 + n_sem]
        token = refs[-1]
        for l in range(n_layers):
            for g, members in enumerate(groups):
                for t in members:
                    for o in range(1, N_CHIPS):
                        _ici_copy(land[l * n + t], o, sems[first(l, g, 0) + o - 1], sems[first(l, g, 1) + o - 1],
                                  True).start()
        token[...] = jnp.zeros_like(token)

    outs = pl.pallas_call(
        body, name=f"gather_start_{tag}",
        in_specs=[HBM] * n_in + [pl.BlockSpec(memory_space=pl.ANY)],
        out_specs=[SEM] * n_sem + [HBM] * n_in + [pl.BlockSpec(memory_space=pltpu.VMEM)],
        out_shape=[DMA_SEM] * n_sem + [pltpu.HBM(a.shape, a.dtype) for a in flat]
        + [jax.ShapeDtypeStruct((8, LANES), F32)],
        input_output_aliases={i: i + n_sem for i in range(n_in)},
        compiler_params=pltpu.CompilerParams(has_side_effects=EFFECT),
    )(*[_hbm(a) for a in flat], after)
    sems = [[(list(outs[first(l, g, 0):first(l, g, 0) + N_PEERS]), list(outs[first(l, g, 1):first(l, g, 1) + N_PEERS]))
             for g in range(n_grp)] for l in range(n_layers)]
    lands_thru = [list(outs[n_sem + l * n:n_sem + (l + 1) * n]) for l in range(n_layers)]
    return sems, lands_thru, outs[-1]


def gather_wait(tag, sems, lands, after):
    n = len(lands)
    send_sems, recv_sems = sems

    def body(*refs):
        land = refs[:n]
        send_r = refs[n:n + N_PEERS]
        recv_r = refs[n + N_PEERS:n + 2 * N_PEERS]
        for t in range(n):
            for o in range(1, N_CHIPS):
                _ici_copy(land[t], o, send_r[o - 1], recv_r[o - 1], True).wait_send()
                _ici_copy(land[t], o, send_r[o - 1], recv_r[o - 1], False).wait_recv()

    return list(pl.pallas_call(
        body, name=f"gather_wait_{tag}",
        in_specs=[HBM] * n + [SEM] * (2 * N_PEERS) + [pl.BlockSpec(memory_space=pl.ANY)],
        out_specs=[HBM] * n,
        out_shape=[pltpu.HBM(a.shape, a.dtype) for a in lands],
        input_output_aliases={i: i for i in range(n)},
        compiler_params=pltpu.CompilerParams(has_side_effects=EFFECT),
    )(*lands, *send_sems, *recv_sems, after))


def gather_forward(lands):
    n = len(lands)

    def body(*refs):
        dst = refs[n:2 * n]
        send_sems, recv_sems = refs[2 * n:]
        mx, my, mc = _me()
        fwds = []
        for t in range(n):
            for o in range(1, N_CHIPS):
                slot = 2 * _flip(mx, o & 2) + _flip(my, o & 1)
                mine = _half_at(dst[t], (slot,), mc)
                theirs = _half_at(dst[t], (slot,), 1 - mc)
                cp = pltpu.make_async_remote_copy(
                    src_ref=mine, dst_ref=mine, send_sem=send_sems.at[t, o - 1], recv_sem=recv_sems.at[t, o - 1],
                    device_id=(mx, my, 1 - mc), device_id_type=MESH)
                cp.start()
                fwds.append((cp, pltpu.make_async_remote_copy(
                    src_ref=theirs, dst_ref=theirs, send_sem=send_sems.at[t, o - 1], recv_sem=recv_sems.at[t, o - 1],
                    device_id=(mx, my, 1 - mc), device_id_type=MESH)))
        for cp, arrival in fwds:
            cp.wait_send()
            arrival.wait_recv()

    any_spec = pl.BlockSpec(memory_space=pl.ANY)
    return list(pl.pallas_call(
        body, name="gather_forward",
        in_specs=[any_spec] * n, out_specs=[any_spec] * n,
        out_shape=[jax.ShapeDtypeStruct(a.shape, a.dtype) for a in lands],
        input_output_aliases={t: t for t in range(n)},
        scratch_shapes=[pltpu.SemaphoreType.DMA((n, N_CHIPS - 1)), pltpu.SemaphoreType.DMA((n, N_CHIPS - 1))],
        compiler_params=_params(),
    )(*lands))


def _scatter_copy(src, land, o, send_sem, recv_sem):
    mx, my, mc = _me()
    px, py = _flip(mx, o & 2), _flip(my, o & 1)
    return pltpu.make_async_remote_copy(
        src_ref=src.at[2 * px + py], dst_ref=land.at[o - 1],
        send_sem=send_sem, recv_sem=recv_sem, device_id=(px, py, mc), device_id_type=MESH)


def scatter_start(pbs, tag, after):
    n = len(pbs)
    lands = [lax.empty((N_CHIPS - 1,) + p.shape[1:], p.dtype) for p in pbs]

    def body(*refs):
        src = refs[:n]
        land = refs[n:2 * n]
        send_sems = refs[2 * n + 1:2 * n + 1 + N_PEERS]
        recv_sems = refs[2 * n + 1 + N_PEERS:2 * n + 1 + 2 * N_PEERS]
        token = refs[-1]
        for t in range(n):
            for o in range(1, N_CHIPS):
                _scatter_copy(src[t], land[t], o, send_sems[o - 1], recv_sems[o - 1]).start()
        token[...] = jnp.zeros_like(token)

    n_sem = 2 * N_PEERS
    arrs = list(pbs) + lands
    outs = pl.pallas_call(
        body, name=f"scatter_start_{tag}",
        in_specs=[HBM] * (2 * n) + [pl.BlockSpec(memory_space=pl.ANY)],
        out_specs=[SEM] * n_sem + [HBM] * (2 * n) + [pl.BlockSpec(memory_space=pltpu.VMEM)],
        out_shape=[DMA_SEM] * n_sem + [pltpu.HBM(a.shape, a.dtype) for a in arrs]
        + [jax.ShapeDtypeStruct((8, LANES), F32)],
        input_output_aliases={i: i + n_sem for i in range(2 * n)},
        compiler_params=pltpu.CompilerParams(has_side_effects=EFFECT),
    )(*[_hbm(a) for a in arrs], after)
    return (list(outs[:N_PEERS]), list(outs[N_PEERS:n_sem]), list(outs[n_sem:n_sem + n]),
            list(outs[n_sem + n:n_sem + 2 * n]), outs[-1])


def scatter_wait(tag, send_sems, recv_sems, pbs, lands, after):
    n = len(pbs)

    def body(*refs):
        src = refs[:n]
        land = refs[n:2 * n]
        send_r = refs[2 * n:2 * n + N_PEERS]
        recv_r = refs[2 * n + N_PEERS:2 * n + 2 * N_PEERS]
        for t in range(n):
            for o in range(1, N_CHIPS):
                cp = _scatter_copy(src[t], land[t], o, send_r[o - 1], recv_r[o - 1])
                cp.wait_send()
                cp.wait_recv()

    arrs = list(pbs) + list(lands)
    outs = pl.pallas_call(
        body, name=f"scatter_wait_{tag}",
        in_specs=[HBM] * (2 * n) + [SEM] * (2 * N_PEERS) + [pl.BlockSpec(memory_space=pl.ANY)],
        out_specs=[HBM] * (2 * n),
        out_shape=[pltpu.HBM(a.shape, a.dtype) for a in arrs],
        input_output_aliases={i: i for i in range(2 * n)},
        compiler_params=pltpu.CompilerParams(has_side_effects=EFFECT),
    )(*arrs, *send_sems, *recv_sems, after)
    return list(outs[n:])


def _pair_copy(src, land, send_sem, recv_sem):
    mx, my, mc = _me()
    return pltpu.make_async_remote_copy(
        src_ref=_half_at(src, (slice(None),), 1 - mc), dst_ref=land, send_sem=send_sem, recv_sem=recv_sem,
        device_id=(mx, my, 1 - mc), device_id_type=MESH)


def pair_start(gs, tag, after):
    n = len(gs)
    lands = [lax.empty((g.shape[0],) + _half_shape(*g.shape[1:]), g.dtype) for g in gs]

    def body(*refs):
        src = refs[:n]
        land = refs[n:2 * n]
        send_sem, recv_sem = refs[2 * n + 1], refs[2 * n + 2]
        token = refs[-1]
        for t in range(n):
            _pair_copy(src[t], land[t], send_sem, recv_sem).start()
        token[...] = jnp.zeros_like(token)

    arrs = list(gs) + lands
    outs = pl.pallas_call(
        body, name=f"pair_start_{tag}",
        in_specs=[HBM] * (2 * n) + [pl.BlockSpec(memory_space=pl.ANY)],
        out_specs=[SEM, SEM] + [HBM] * (2 * n) + [pl.BlockSpec(memory_space=pltpu.VMEM)],
        out_shape=[DMA_SEM, DMA_SEM] + [pltpu.HBM(a.shape, a.dtype) for a in arrs] + [jax.ShapeDtypeStruct((8, LANES), F32)],
        input_output_aliases={i: i + 2 for i in range(2 * n)},
        compiler_params=pltpu.CompilerParams(has_side_effects=EFFECT),
    )(*[_hbm(a) for a in arrs], after)
    return outs[0], outs[1], list(outs[2:2 + n]), list(outs[2 + n:2 + 2 * n]), outs[-1]


def pair_wait(tag, send_sem, recv_sem, gs, lands, after):
    n = len(gs)

    def body(*refs):
        src = refs[:n]
        land = refs[n:2 * n]
        send_r, recv_r = refs[2 * n], refs[2 * n + 1]
        for t in range(n):
            cp = _pair_copy(src[t], land[t], send_r, recv_r)
            cp.wait_send()
            cp.wait_recv()

    arrs = list(gs) + list(lands)
    outs = pl.pallas_call(
        body, name=f"pair_wait_{tag}",
        in_specs=[HBM] * (2 * n) + [SEM, SEM, pl.BlockSpec(memory_space=pl.ANY)],
        out_specs=[HBM] * (2 * n),
        out_shape=[pltpu.HBM(a.shape, a.dtype) for a in arrs],
        input_output_aliases={i: i for i in range(2 * n)},
        compiler_params=pltpu.CompilerParams(has_side_effects=EFFECT),
    )(*arrs, send_sem, recv_sem, after)
    return list(outs[:n]), list(outs[n:])


def _gather8_copy(x, land, o, send_sem, recv_sem, sending):
    mx, my, mc = _me()
    px, py, pc = _flip(mx, o & 4), _flip(my, o & 2), _flip(mc, o & 1)
    slot = 4 * mx + 2 * my + mc if sending else 4 * px + 2 * py + pc
    return pltpu.make_async_remote_copy(
        src_ref=x, dst_ref=land.at[slot], send_sem=send_sem, recv_sem=recv_sem,
        device_id=(px, py, pc), device_id_type=MESH)


def gather8_start(x, land, after, tag):
    n_peer = N_DEV - 1

    def body(x_ref, land_ref, after_ref, *rest):
        send_sems, recv_sems = rest[:n_peer], rest[n_peer:2 * n_peer]
        token = rest[-1]
        for o in range(1, N_DEV):
            _gather8_copy(x_ref, land_ref, o, send_sems[o - 1], recv_sems[o - 1], True).start()
        token[...] = jnp.zeros_like(token)

    outs = pl.pallas_call(
        body, name=f"gather8_start_{tag}",
        in_specs=[HBM, HBM, pl.BlockSpec(memory_space=pl.ANY)],
        out_specs=[SEM] * (2 * n_peer) + [HBM, HBM, pl.BlockSpec(memory_space=pltpu.VMEM)],
        out_shape=[DMA_SEM] * (2 * n_peer) + [pltpu.HBM(x.shape, x.dtype), pltpu.HBM(land.shape, land.dtype),
                                              jax.ShapeDtypeStruct((8, LANES), F32)],
        input_output_aliases={0: 2 * n_peer, 1: 2 * n_peer + 1},
        compiler_params=pltpu.CompilerParams(has_side_effects=EFFECT),
    )(_hbm(x), _hbm(land), after)
    return list(outs[:n_peer]), list(outs[n_peer:2 * n_peer]), outs[2 * n_peer], outs[2 * n_peer + 1], outs[-1]


def gather8_wait(tag, send_sems, recv_sems, x, land, after):
    n_peer = N_DEV - 1

    def body(x_ref, land_ref, *rest):
        send_r, recv_r = rest[:n_peer], rest[n_peer:2 * n_peer]
        for o in range(1, N_DEV):
            _gather8_copy(x_ref, land_ref, o, send_r[o - 1], recv_r[o - 1], True).wait_send()
            _gather8_copy(x_ref, land_ref, o, send_r[o - 1], recv_r[o - 1], False).wait_recv()

    return pl.pallas_call(
        body, name=f"gather8_wait_{tag}",
        in_specs=[HBM, HBM] + [SEM] * (2 * n_peer) + [pl.BlockSpec(memory_space=pl.ANY)],
        out_specs=[HBM, HBM],
        out_shape=[pltpu.HBM(x.shape, x.dtype), pltpu.HBM(land.shape, land.dtype)],
        input_output_aliases={0: 0, 1: 1},
        compiler_params=pltpu.CompilerParams(has_side_effects=EFFECT),
    )(x, land, *send_sems, *recv_sems, after)[1]


def pair_fill_halves(fs):
    n = len(fs)

    def body(*refs):
        dst = refs[n:2 * n]
        send_sems, recv_sems = refs[2 * n:]
        mx, my, mc = _me()
        copies = []
        for t in range(n):
            mine = _half_at(dst[t], (slice(None),), mc)
            theirs = _half_at(dst[t], (slice(None),), 1 - mc)
            cp = pltpu.make_async_remote_copy(
                src_ref=mine, dst_ref=mine, send_sem=send_sems.at[t], recv_sem=recv_sems.at[t],
                device_id=(mx, my, 1 - mc), device_id_type=MESH)
            cp.start()
            copies.append((cp, pltpu.make_async_remote_copy(
                src_ref=theirs, dst_ref=theirs, send_sem=send_sems.at[t], recv_sem=recv_sems.at[t],
                device_id=(mx, my, 1 - mc), device_id_type=MESH)))
        for cp, arrival in copies:
            cp.wait_send()
            arrival.wait_recv()

    any_spec = pl.BlockSpec(memory_space=pl.ANY)
    return pl.pallas_call(
        body, name="pair_fill_halves",
        in_specs=[any_spec] * n, out_specs=[any_spec] * n,
        out_shape=[jax.ShapeDtypeStruct(f.shape, f.dtype) for f in fs],
        input_output_aliases={t: t for t in range(n)},
        scratch_shapes=[pltpu.SemaphoreType.DMA((n,)), pltpu.SemaphoreType.DMA((n,))],
        compiler_params=_params(),
    )(*fs)


def _pack_rows(parts, d):
    rows, spans = [], []
    at = 0
    for p in parts:
        flat = p.reshape(-1)
        n_rows = -(-flat.shape[0] // (8 * d)) * 8
        flat = jnp.pad(flat, (0, n_rows * d - flat.shape[0]))
        rows.append(flat.reshape(n_rows, d))
        spans.append((at, p.shape))
        at += n_rows
    return jnp.concatenate(rows, axis=0), spans


def _unpack_rows(packed, spans):
    lead, d = packed.shape[:-2], packed.shape[-1]
    out = []
    for at, shape in spans:
        n = math.prod(shape)
        n_rows = -(-n // d)
        out.append(packed[..., at:at + n_rows, :].reshape(lead + (-1,))[..., :n].reshape(lead + tuple(shape)))
    return out


def _rotate_half_matrix():
    half = QK_ROPE // 2
    idx = jnp.arange(QK_ROPE)
    src = jnp.where(idx < half, idx + half, idx - half)
    sign = jnp.where(idx < half, -1.0, 1.0)
    return (jnp.zeros((QK_ROPE, QK_ROPE), F32).at[src, idx].set(sign)).astype(BF16)


def kernel(x, c, positions, ada_w, ada_b, ffn1_norm, ffn1_w_gate, ffn1_w_up, ffn1_w_down, mix_norm, w_in, pool_w, pool_scale, q_a_norm, w_q_b, kv_a_norm, w_kv_b, w_out, ffn2_norm, ffn2_w_gate, ffn2_w_up, ffn2_w_down, final_norm, loss_target, m_ada_w, m_ada_b, m_ffn1_norm, m_ffn1_w_gate, m_ffn1_w_up, m_ffn1_w_down, m_mix_norm, m_w_in, m_pool_w, m_pool_scale, m_q_a_norm, m_w_q_b, m_kv_a_norm, m_w_kv_b, m_w_out, m_ffn2_norm, m_ffn2_w_gate, m_ffn2_w_up, m_ffn2_w_down, m_final_norm, v_ada_w, v_ada_b, v_ffn1_norm, v_ffn1_w_gate, v_ffn1_w_up, v_ffn1_w_down, v_mix_norm, v_w_in, v_pool_w, v_pool_scale, v_q_a_norm, v_w_q_b, v_kv_a_norm, v_w_kv_b, v_w_out, v_ffn2_norm, v_ffn2_w_gate, v_ffn2_w_up, v_ffn2_w_down, v_final_norm):
    mx, my, mc = _me()
    chip = 2 * mx + my
    half = jnp.reshape(mc, (1,)).astype(jnp.int32)
    chip1 = jnp.reshape(chip, (1,)).astype(jnp.int32)
    n_layers, d, ada_cols = ada_w.shape
    xt = x[0]
    tgt = loss_target[0]

    inv_freq = 1.0 / (ROPE_THETA ** (jnp.arange(0, QK_ROPE, 2, dtype=F32) / QK_ROPE))
    ang = positions[0].astype(F32)[:, None] * inv_freq
    ang = jnp.concatenate([ang, ang], axis=-1)
    cos, sin = jnp.cos(ang), jnp.sin(ang)
    rot = _rotate_half_matrix()
    rot_t = rot.T

    c_all = exchange8(c, True).reshape(N_DEV, d)
    c16 = jnp.pad(c_all, ((0, 8), (0, 0)))
    ada_b_loc = lax.dynamic_slice_in_dim(ada_b, chip * ada_cols, ada_cols, axis=1).reshape(n_layers, 1, ada_cols)
    mod_part = ada_fwd(c16, ada_w, ada_b_loc)[:, :N_DEV]
    mod_got = exchange8(jnp.transpose(mod_part, (1, 0, 2)), False)
    mod = jnp.transpose(mod_got.reshape(N_CHIPS, 2, n_layers, ada_cols)[:, 0], (1, 0, 2))
    mod = mod.reshape(n_layers, 9, 1, d)

    tr = lambda a: jnp.transpose(a, (0, 2, 1))
    local = [tr(ffn1_w_gate), tr(ffn1_w_up), ffn1_w_down, tr(w_in), tr(w_q_b), w_kv_b, w_out,
             tr(ffn2_w_gate), tr(ffn2_w_up), ffn2_w_down]
    ffn1_pos, rest_pos = (0, 1, 2), tuple(range(3, len(local)))
    groups = (ffn1_pos, rest_pos)
    placed = [cast_place(w, chip1, (0,), mod) for w in local]
    g_sems, lands_fly, g_token = gather_start([[p[0] for p in placed]], groups, mod, "first")
    if n_layers > 1:
        later = tuple(range(1, n_layers))
        placed = [cast_place(w, chip1, later, g_token) for w in local]
        more_sems, more_fly, g_token = gather_start(
            [[p[j] for p in placed] for j in range(len(later))], groups, g_token, "rest")
        g_sems, lands_fly = g_sems + more_sems, lands_fly + more_fly
    gathered = []

    row = lambda a, l: a[l].reshape(1, -1)
    saved = []
    for l in range(n_layers):
        g1, u1, d1 = gather_forward(gather_wait(
            f"{l}a", g_sems[l][0], [lands_fly[l][t] for t in ffn1_pos], xt if l else g_token))
        sv = dict(x0=xt)
        xt, sv["h1"], sv["a1"], sv["sl1"], sv["dsu1"], sv["y1"] = ffn_fwd(
            xt, row(ffn1_norm, l), mod[l, 0], mod[l, 1], mod[l, 2], g1, u1, d1)
        sv["x1"] = xt
        win, wq, wkv, wout, g2, u2, d2 = gather_forward(gather_wait(
            f"{l}b", g_sems[l][1], [lands_fly[l][t] for t in rest_pos], xt))
        gathered.append([g1, u1, d1, win, wq, wkv, wout, g2, u2, d2])
        win = win.reshape(-1, d)
        sv["h2"], u, cq, ckv, kr = mix_in_fwd(xt, row(mix_norm, l), mod[l, 3], mod[l, 4], win)
        sv["cq"], sv["ckv"] = cq, ckv
        yp, sv["diff"] = pool_fwd(u, pool_w[l], row(pool_scale, l))
        qh, kh, vh, sv["ql"], sv["kvl"] = mla_qkv_fwd(
            cq, ckv, kr, row(q_a_norm, l), row(kv_a_norm, l), wq, wkv, cos, sin, rot)
        sv["qkv"] = (qh, kh, vh)
        om = attn_fwd(qh, kh, vh)
        xt, sv["ycat"], sv["y2"] = out_proj_fwd(yp, om, wout, xt, mod[l, 5])
        sv["x2"] = xt
        xt, sv["h3"], sv["a3"], sv["sl3"], sv["dsu3"], sv["y3"] = ffn_fwd(
            xt, row(ffn2_norm, l), mod[l, 6], mod[l, 7], mod[l, 8], g2, u2, d2)
        saved.append(sv)

    loss_vec, dx, d_final_norm = final_loss(xt, final_norm.reshape(1, d), tgt)
    loss = lax.psum(loss_vec[0, 0], ("x", "y", "c"))

    none = [None] * n_layers
    dmods, dnorm1, dnorm2, dnorm3 = list(none), list(none), list(none), list(none)
    dpw, dps, dqan_l, dkvan_l = list(none), list(none), list(none), list(none)
    reduced = [None] * len(local)
    stages = []
    sel_of = lambda l: jnp.stack([mc, chip, jnp.asarray(l, mc.dtype)]).astype(jnp.int32)

    def to_chips(job, after_wait, after_start):
        send, recv, g_fly, lands_p = job.pop("pair")
        g_fly, got = pair_wait(job["tag"], send, recv, g_fly, lands_p, after_wait)
        pbs, job["owns"] = pair_add(g_fly, got, sel_of(job["l"]))
        job["scatter"] = scatter_start(pbs, job["tag"], after_start)
        return job["scatter"][4][0, 0]

    def finish(job, after):
        s_send, s_recv, pbs_fly, lands_j, _ = job.pop("scatter")
        parts = scatter_wait(job["tag"], s_send, s_recv, pbs_fly, lands_j, after)
        sums = chip_sum(job["owns"], parts, sel_of(job["l"]), [(n_layers,) + shp for shp in job["shapes"]],
                        [reduced[t] for t in job["pos"]])
        for t, total_t in zip(job["pos"], sums):
            reduced[t] = total_t

    def checkpoint(tag, l, positions, grads_, done, before_scatter=None):
        send, recv, g_fly, lands_p, tok = pair_start(grads_, tag, done)
        order = tok[0, 0]
        if stages:
            order = order + to_chips(stages[-1], done, done if before_scatter is None else before_scatter)
        if len(stages) >= 3:
            finish(stages[-3], done)
        stages.append(dict(tag=tag, l=l, pos=positions, shapes=[g.shape[1:] for g in grads_],
                           pair=(send, recv, g_fly, lands_p)))
        return order

    def small_gather(tag, parts, after):
        packed, spans = _pack_rows(parts, d)
        land = lax.dynamic_update_index_in_dim(lax.empty((N_DEV,) + packed.shape, F32), packed, 4 * mx + 2 * my + mc, 0)
        return gather8_start(packed, land, after, tag), spans

    order = None

    for l in reversed(range(n_layers)):
        sv = saved[l]
        g1, u1, d1, win, wq, wkv, wout, g2, u2, d2 = gathered[l]
        win = win.reshape(-1, d)
        gt3 = mod[l, 8] if order is None else mod[l, 8] + order
        dy, dgt, dup = ffn_bwd_act(dx, sv["sl3"], sv["dsu3"], gt3, d2)
        dx, dvec3 = ffn_bwd_in(dx, sv["x2"], sv["y3"], dgt, dup, row(ffn2_norm, l), mod[l, 7], g2, u2)
        g_g2, g_u2, g_d2 = tn_mm(dgt, sv["h3"][None]), tn_mm(dup, sv["h3"][None]), tn_mm(sv["a3"], dy[None])
        dy2, dyp, dom, dg2 = out_proj_bwd(dx, sv["y2"], mod[l, 5], wout)
        g_wout = tn_mm(sv["ycat"], dy2[None])
        qh, kh, vh = sv["qkv"]
        dqh, dkh, dvh = attn_bwd(qh, kh, vh, dom)
        dcq, dckv, dkr_in, gq, gkv, dqan_l[l], dkvan_l[l] = mla_qkv_bwd(
            dqh, dkh, dvh, sv["cq"], sv["ckv"], row(q_a_norm, l), row(kv_a_norm, l), wq, wkv, cos, sin, rot_t)
        g_wq, g_wkv = tn_mm(gq, sv["ql"][None]), tn_mm(sv["kvl"][None], gkv)
        du, dpw[l], dps[l] = pool_bwd(dyp, sv["diff"], pool_w[l], row(pool_scale, l))
        dx, dz, dvec2 = mix_in_bwd(dx, du, dcq, dckv, dkr_in, sv["x1"], row(mix_norm, l), mod[l, 4], win)
        g_win = tn_mm(dz[None], sv["h2"][None]).reshape(N_CHIPS, -1, d)
        dnorm2[l], dnorm3[l] = dvec2[3], dvec3[3]
        dmod_rest = jnp.concatenate([dvec2[0:2], dg2, dvec3[0:3]], axis=0)
        if l == 0:
            early = small_gather("early", [jnp.stack(dmods[1:]), dmod_rest, jnp.stack(dnorm1[1:]), jnp.stack(dnorm2),
                                           jnp.stack(dnorm3), d_final_norm, jnp.stack(dps), jnp.stack(dqan_l),
                                           jnp.stack(dkvan_l), jnp.stack(dpw)], dx)
        order = checkpoint(f"{l}a", l, rest_pos, [g_win, g_wq, g_wkv, g_wout, g_g2, g_u2, g_d2], dx,
                           early[0][4] if l == 0 else None)
        dy, dgt, dup = ffn_bwd_act(dx, sv["sl1"], sv["dsu1"], mod[l, 2] + order, d1)
        dx, dvec1 = ffn_bwd_in(dx, sv["x0"], sv["y1"], dgt, dup, row(ffn1_norm, l), mod[l, 1], g1, u1)
        g_g1, g_u1, g_d1 = tn_mm(dgt, sv["h1"][None]), tn_mm(dup, sv["h1"][None]), tn_mm(sv["a1"], dy[None])
        dmods[l] = jnp.concatenate([dvec1[0:3], dmod_rest], axis=0)
        dnorm1[l] = dvec1[3]
        if l == 0:
            late = small_gather("late", [dvec1[0:3], dvec1[3]], dx)
        order = checkpoint(f"{l}b", l, ffn1_pos, [g_g1, g_u1, g_d1], dx, late[0][4] if l == 0 else None)

    to_chips(stages[-1], stages[-2]["scatter"][4], stages[-2]["scatter"][4])
    sent = stages[-1]["scatter"][4]
    got_early = gather8_wait("early", *early[0][:4], sent)
    got_late = gather8_wait("late", *late[0][:4], sent)
    (g_dmod_rest, g_dmod0_rest, g_n1_rest, g_n2, g_n3, g_fn, g_ps, g_qan, g_kvan, g_pw) = _unpack_rows(
        sum_devices(got_early), early[1])
    g_dmod0_first, g_n1_first = _unpack_rows(sum_devices(got_late), late[1])
    g_ada_b = jnp.concatenate([jnp.concatenate([g_dmod0_first, g_dmod0_rest], axis=0)[None], g_dmod_rest], axis=0)
    g_n1 = jnp.concatenate([g_n1_first[None], g_n1_rest], axis=0)
    each_rest, each0_rest = _unpack_rows(got_early, early[1])[:2]
    each0_first = _unpack_rows(got_late, late[1])[0]
    dmod_all = jnp.concatenate([jnp.concatenate([each0_first, each0_rest], axis=1)[:, None], each_rest], axis=1)
    dmod_all = dmod_all.reshape(N_DEV, n_layers, 9 * d)
    dmod_loc = lax.dynamic_slice_in_dim(dmod_all, chip * ada_cols, ada_cols, axis=2)
    dmod16 = jnp.pad(jnp.transpose(dmod_loc, (1, 0, 2)), ((0, 0), (0, 8), (0, 0)))
    g_ada_w = ada_bwd(c16, dmod16)

    grads = [g_ada_w, g_ada_b, g_n1, None, None, None, g_n2, None, g_pw, g_ps, g_qan, None, g_kvan, None, None, g_n3,
             None, None, None, g_fn]
    weights = [ada_w, ada_b, ffn1_norm, ffn1_w_gate, ffn1_w_up, ffn1_w_down, mix_norm, w_in, pool_w, pool_scale,
               q_a_norm, w_q_b, kv_a_norm, w_kv_b, w_out, ffn2_norm, ffn2_w_gate, ffn2_w_up, ffn2_w_down, final_norm]
    ms = [m_ada_w, m_ada_b, m_ffn1_norm, m_ffn1_w_gate, m_ffn1_w_up, m_ffn1_w_down, m_mix_norm, m_w_in, m_pool_w,
          m_pool_scale, m_q_a_norm, m_w_q_b, m_kv_a_norm, m_w_kv_b, m_w_out, m_ffn2_norm, m_ffn2_w_gate, m_ffn2_w_up,
          m_ffn2_w_down, m_final_norm]
    vs = [v_ada_w, v_ada_b, v_ffn1_norm, v_ffn1_w_gate, v_ffn1_w_up, v_ffn1_w_down, v_mix_norm, v_w_in, v_pool_w,
          v_pool_scale, v_q_a_norm, v_w_q_b, v_kv_a_norm, v_w_kv_b, v_w_out, v_ffn2_norm, v_ffn2_w_gate, v_ffn2_w_up,
          v_ffn2_w_down, v_final_norm]
    transposed = (3, 4, 7, 11, 16, 17)
    outs = [None] * len(weights)
    for i, (w, g, m, v) in enumerate(zip(weights, grads, ms, vs)):
        if g is not None:
            outs[i] = adamw(w, g.reshape(w.shape), m, v)
    big = [i for i, g in enumerate(grads) if g is None]

    def update(positions):
        filled = pair_fill_halves([reduced[t] for t in positions])
        for t, g in zip(positions, filled):
            i = big[t]
            if i in transposed:
                outs[i] = tuple(tr(o) for o in adamw(tr(weights[i]), g, tr(ms[i]), tr(vs[i]), copy_g=True))
            else:
                outs[i] = adamw(weights[i], g, ms[i], vs[i], copy_g=True)

    finish(stages[-3], outs[0][1])
    finish(stages[-2], outs[0][1])
    update(rest_pos)
    finish(stages[-1], outs[big[rest_pos[-1]]][1])
    update(ffn1_pos)
    return (loss, dx.reshape(x.shape), *[t[0] for t in outs], *[t[1] for t in outs], *[t[2] for t in outs],
            *[t[3] for t in outs])
```

```python
import math

import jax
import jax.numpy as jnp
from jax import lax
from jax.experimental import pallas as pl
from jax.experimental.pallas import tpu as pltpu

F32 = jnp.float32
BF16 = jnp.bfloat16
MESH = pl.DeviceIdType.MESH

EPS = 1e-6
ROPE_THETA = 10000.0
N_HEADS = 4
QK_NOPE = 128
QK_ROPE = 64
V_HEAD = 128
POOL_WINDOWS = (2, 4, 8, 16)
POOL_GC = 128
POOL_WIDTH = POOL_GC * len(POOL_WINDOWS)
Q_LORA = 384
KV_LORA = 256
SOFTMAX_SCALE = 1.0 / math.sqrt(QK_NOPE + QK_ROPE)
N_CHIPS = 4
N_DEV = 8

ADAM_LR = 0.001
ADAM_B1 = 0.9
ADAM_B2 = 0.999
ADAM_EPS = 1e-08
ADAM_WD = 0.01
ADAM_STEP = 10

ROW_TILE = 512
ATT_TILE = 512
VMEM_LIMIT = 56 * 1024 * 1024
BF16_ROWS = 16
LANES = 128


def _params(sem=None, vmem=VMEM_LIMIT):
    return pltpu.CompilerParams(dimension_semantics=sem, vmem_limit_bytes=vmem)


def _dot(a, b):
    return jnp.dot(a, b, preferred_element_type=F32)


def _dot_nt(a, b):
    return lax.dot_general(a, b, (((1,), (1,)), ((), ())), preferred_element_type=F32)


def _dot_tn(a, b):
    return lax.dot_general(a, b, (((0,), (0,)), ((), ())), preferred_element_type=F32)


def _dot_exact(t, perm):
    t1 = t.astype(BF16)
    r1 = t - t1.astype(F32)
    t2 = r1.astype(BF16)
    t3 = (r1 - t2.astype(F32)).astype(BF16)
    return _dot(t1, perm) + _dot(t2, perm) + _dot(t3, perm)


def _sum0(a):
    return jnp.sum(a, axis=0, keepdims=True)


def _rms(xt):
    r = lax.rsqrt(jnp.mean(xt * xt, axis=-1, keepdims=True) + EPS)
    return xt * r, r


def _rms_bwd(dy, xt, g):
    xhat, r = _rms(xt)
    dxhat = dy * g
    dx = r * (dxhat - xhat * jnp.mean(dxhat * xhat, axis=-1, keepdims=True))
    return dx, _sum0(dy * xhat)


def _normmod_bwd(dh, xt, gn, sc):
    xhat, _ = _rms(xt)
    dn = dh * (1.0 + sc)
    dx, dgn = _rms_bwd(dn, xt, gn)
    return dx, _sum0(dh), _sum0(dh * (xhat * gn)), dgn


def _row_tile(s):
    return min(s, ROW_TILE)


def _full(shape):
    n = len(shape)
    return pl.BlockSpec(shape, lambda *_: (0,) * n)


def _resident(shape):
    n = len(shape)
    return pl.BlockSpec(shape, lambda *_: (0,) * n, pipeline_mode=pl.Buffered(1))


def ffn_fwd(x, gn, sh, sc, gt, wg, wu, wd):
    s, d = x.shape
    k_chunks, fs, _ = wg.shape
    tm = _row_tile(s)

    def body(x_ref, gn_ref, sh_ref, sc_ref, gt_ref, wg_ref, wu_ref, wd_ref,
             xo_ref, h_ref, a_ref, sl_ref, dsu_ref, y_ref):
        xt = x_ref[...]
        xhat, _ = _rms(xt)
        h = (xhat * gn_ref[...] * (1.0 + sc_ref[...]) + sh_ref[...]).astype(BF16)
        h_ref[...] = h
        y = jnp.zeros((tm, d), F32)
        for k in range(k_chunks):
            gate = _dot_nt(h, wg_ref[k])
            up = _dot_nt(h, wu_ref[k])
            sg = jax.nn.sigmoid(gate)
            sl = gate * sg
            a = (sl * up).astype(BF16)
            a_ref[k] = a.T
            sl_ref[k] = sl.astype(BF16)
            dsu_ref[k] = (up * (sg * (1.0 + gate * (1.0 - sg)))).astype(BF16)
            y += _dot(a, wd_ref[k])
        y_ref[...] = y.astype(BF16)
        xo_ref[...] = xt + 0.5 * gt_ref[...] * y

    row = pl.BlockSpec((tm, d), lambda i: (i, 0))
    vec = pl.BlockSpec((1, d), lambda i: (0, 0))
    act = pl.BlockSpec((k_chunks, tm, fs), lambda i: (0, i, 0))
    act_shape = jax.ShapeDtypeStruct((k_chunks, s, fs), BF16)
    return pl.pallas_call(
        body, name="ffn_fwd",
        grid=(s // tm,),
        in_specs=[row, vec, vec, vec, vec, _resident(wg.shape), _resident(wu.shape), _resident(wd.shape)],
        out_specs=[row, row, pl.BlockSpec((k_chunks, fs, tm), lambda i: (0, 0, i)), act, act, row],
        out_shape=[jax.ShapeDtypeStruct((s, d), F32), jax.ShapeDtypeStruct((s, d), BF16),
                   jax.ShapeDtypeStruct((k_chunks, fs, s), BF16), act_shape, act_shape,
                   jax.ShapeDtypeStruct((s, d), BF16)],
        compiler_params=_params(("arbitrary",)),
    )(x, gn, sh, sc, gt, wg, wu, wd)


def ffn_bwd_act(dxn, sl, dsu, gt, wd):
    s, d = dxn.shape
    k_chunks, fs, _ = wd.shape
    tm = _row_tile(s)

    def body(dxn_ref, sl_ref, dsu_ref, gt_ref, wd_ref, dy_ref, dgate_ref, dup_ref):
        dy = (0.5 * gt_ref[...] * dxn_ref[...]).astype(BF16)
        dy_ref[...] = dy
        for k in range(k_chunks):
            da = _dot_nt(dy, wd_ref[k])
            dgate_ref[k] = (da * dsu_ref[k].astype(F32)).astype(BF16)
            dup_ref[k] = (da * sl_ref[k].astype(F32)).astype(BF16)

    row = pl.BlockSpec((tm, d), lambda i: (i, 0))
    act = pl.BlockSpec((k_chunks, tm, fs), lambda i: (0, i, 0))
    act_shape = jax.ShapeDtypeStruct((k_chunks, s, fs), BF16)
    return pl.pallas_call(
        body, name="ffn_bwd_act",
        grid=(s // tm,),
        in_specs=[row, act, act, pl.BlockSpec((1, d), lambda i: (0, 0)), _resident(wd.shape)],
        out_specs=[row, act, act],
        out_shape=[jax.ShapeDtypeStruct((s, d), BF16), act_shape, act_shape],
        compiler_params=_params(("arbitrary",)),
    )(dxn, sl, dsu, gt, wd)


def ffn_bwd_in(dxn, x, y, dgate, dup, gn, sc, wg, wu):
    s, d = x.shape
    k_chunks, fs, _ = wg.shape
    tm = _row_tile(s)

    def body(dxn_ref, x_ref, y_ref, dgate_ref, dup_ref, gn_ref, sc_ref, wg_ref, wu_ref, dx_ref, dvec_ref):
        i = pl.program_id(0)

        @pl.when(i == 0)
        def _():
            dvec_ref[...] = jnp.zeros_like(dvec_ref)

        dh = jnp.zeros((tm, d), F32)
        for k in range(k_chunks):
            dh += _dot(dgate_ref[k], wg_ref[k]) + _dot(dup_ref[k], wu_ref[k])
        dxn_t = dxn_ref[...]
        dx, dsh, dsc, dgn = _normmod_bwd(dh, x_ref[...], gn_ref[...], sc_ref[...])
        dx_ref[...] = dx + dxn_t
        dvec_ref[0:1, :] += dsh
        dvec_ref[1:2, :] += dsc
        dvec_ref[2:3, :] += _sum0(0.5 * dxn_t * y_ref[...].astype(F32))
        dvec_ref[3:4, :] += dgn

    row = pl.BlockSpec((tm, d), lambda i: (i, 0))
    vec = pl.BlockSpec((1, d), lambda i: (0, 0))
    act = pl.BlockSpec((k_chunks, tm, fs), lambda i: (0, i, 0))
    return pl.pallas_call(
        body, name="ffn_bwd_in",
        grid=(s // tm,),
        in_specs=[row, row, row, act, act, vec, vec, _resident(wg.shape), _resident(wu.shape)],
        out_specs=[row, pl.BlockSpec((8, d), lambda i: (0, 0))],
        out_shape=[jax.ShapeDtypeStruct((s, d), F32), jax.ShapeDtypeStruct((8, d), F32)],
        compiler_params=_params(("arbitrary",)),
    )(dxn, x, y, dgate, dup, gn, sc, wg, wu)


def nn_mm(a_t, b):
    g, m, s = a_t.shape
    n = b.shape[1]

    def body(a_ref, b_ref, o_ref):
        o_ref[...] = _dot(a_ref[...], b_ref[...])

    return pl.pallas_call(
        body, name="nn_mm",
        grid=(g,), in_specs=[pl.BlockSpec((None, m, s), lambda gi: (gi, 0, 0)), pl.BlockSpec((s, n), lambda gi: (0, 0))],
        out_specs=pl.BlockSpec((None, m, n), lambda gi: (gi, 0, 0)),
        out_shape=jax.ShapeDtypeStruct((g, m, n), F32),
        compiler_params=_params(("arbitrary",)),
    )(a_t, b)


def tn_mm(a, b):
    ga, s, m = a.shape
    gb, _, n = b.shape
    g = max(ga, gb)

    def body(a_ref, b_ref, o_ref):
        o_ref[...] = _dot_tn(a_ref[...], b_ref[...])

    a_spec = pl.BlockSpec((None, s, m), (lambda gi: (gi, 0, 0)) if ga > 1 else (lambda gi: (0, 0, 0)))
    b_spec = pl.BlockSpec((None, s, n), (lambda gi: (gi, 0, 0)) if gb > 1 else (lambda gi: (0, 0, 0)))
    return pl.pallas_call(
        body, name="tn_mm",
        grid=(g,), in_specs=[a_spec, b_spec], out_specs=pl.BlockSpec((None, m, n), lambda gi: (gi, 0, 0)),
        out_shape=jax.ShapeDtypeStruct((g, m, n), F32),
        compiler_params=_params(("arbitrary",)),
    )(a, b)


def mix_in_fwd(x, gn, sh, sc, w_in_t):
    s, d = x.shape
    tm = _row_tile(s)
    o1, o2, o3 = POOL_WIDTH, POOL_WIDTH + Q_LORA, POOL_WIDTH + Q_LORA + KV_LORA

    def body(x_ref, gn_ref, sh_ref, sc_ref, w_ref, h_ref, u_ref, cq_ref, ckv_ref, kr_ref):
        xhat, _ = _rms(x_ref[...])
        h = (xhat * gn_ref[...] * (1.0 + sc_ref[...]) + sh_ref[...]).astype(BF16)
        h_ref[...] = h
        z = _dot_nt(h, w_ref[0:o3, :])
        u_ref[...] = z[:, 0:o1]
        cq_ref[...] = z[:, o1:o2]
        ckv_ref[...] = z[:, o2:o3]
        kr_ref[...] = _dot_nt(h, w_ref[o3:, :])

    row = lambda w: pl.BlockSpec((tm, w), lambda i: (i, 0))
    vec = pl.BlockSpec((1, d), lambda i: (0, 0))
    return pl.pallas_call(
        body, name="mix_in_fwd",
        grid=(s // tm,),
        in_specs=[row(d), vec, vec, vec, _full(w_in_t.shape)],
        out_specs=[row(d), row(POOL_WIDTH), row(Q_LORA), row(KV_LORA), row(QK_ROPE)],
        out_shape=[jax.ShapeDtypeStruct((s, d), BF16), jax.ShapeDtypeStruct((s, POOL_WIDTH), F32),
                   jax.ShapeDtypeStruct((s, Q_LORA), F32), jax.ShapeDtypeStruct((s, KV_LORA), F32),
                   jax.ShapeDtypeStruct((s, QK_ROPE), F32)],
        compiler_params=_params(("arbitrary",)),
    )(x, gn, sh, sc, w_in_t)


def mix_in_bwd(dxn, du, dcq, dckv, dkr, x, gn, sc, w_in_t):
    s, d = x.shape
    tm = _row_tile(s)
    o1, o2, o3 = POOL_WIDTH, POOL_WIDTH + Q_LORA, POOL_WIDTH + Q_LORA + KV_LORA
    n_z = w_in_t.shape[0]

    def body(dxn_ref, du_ref, dcq_ref, dckv_ref, dkr_ref, x_ref, gn_ref, sc_ref, w_ref, dx_ref, dz_ref, dvec_ref):
        i = pl.program_id(0)

        @pl.when(i == 0)
        def _():
            dvec_ref[...] = jnp.zeros_like(dvec_ref)

        dub = du_ref[...].astype(BF16)
        dqb = dcq_ref[...].astype(BF16)
        dkb = dckv_ref[...].astype(BF16)
        drb = dkr_ref[...].astype(BF16)
        dz_ref[0:o1, :] = dub.T
        dz_ref[o1:o2, :] = dqb.T
        dz_ref[o2:o3, :] = dkb.T
        dz_ref[o3:, :] = drb.T
        dh = (_dot(dub, w_ref[0:o1, :]) + _dot(dqb, w_ref[o1:o2, :]) + _dot(dkb, w_ref[o2:o3, :])
              + _dot(drb, w_ref[o3:, :]))
        dx, dsh, dsc, dgn = _normmod_bwd(dh, x_ref[...], gn_ref[...], sc_ref[...])
        dx_ref[...] = dx + dxn_ref[...]
        dvec_ref[0:1, :] += dsh
        dvec_ref[1:2, :] += dsc
        dvec_ref[3:4, :] += dgn

    row = lambda w: pl.BlockSpec((tm, w), lambda i: (i, 0))
    vec = pl.BlockSpec((1, d), lambda i: (0, 0))
    return pl.pallas_call(
        body, name="mix_in_bwd",
        grid=(s // tm,),
        in_specs=[row(d), row(POOL_WIDTH), row(Q_LORA), row(KV_LORA), row(QK_ROPE), row(d), vec, vec,
                  _full(w_in_t.shape)],
        out_specs=[row(d), pl.BlockSpec((n_z, tm), lambda i: (0, i)), pl.BlockSpec((8, d), lambda i: (0, 0))],
        out_shape=[jax.ShapeDtypeStruct((s, d), F32), jax.ShapeDtypeStruct((n_z, s), BF16),
                   jax.ShapeDtypeStruct((8, d), F32)],
        compiler_params=_params(("arbitrary",)),
    )(dxn, du, dcq, dckv, dkr, x, gn, sc, w_in_t)


def _window_sum(a, w, rows, forward):
    s = a.shape[0]
    step = 1
    while step < w:
        if forward:
            shifted = jnp.where(rows < s - step, pltpu.roll(a, s - step, 0), 0.0)
        else:
            shifted = jnp.where(rows >= step, pltpu.roll(a, step, 0), 0.0)
        a = a + shifted
        step *= 2
    return a


def pool_fwd(u, pool_w, pool_scale):
    s = u.shape[0]

    def body(u_ref, w_ref, sc_ref, y_ref, diff_ref):
        rows = lax.broadcasted_iota(jnp.int32, (s, POOL_GC), 0)
        for g, w in enumerate(POOL_WINDOWS):
            cols = slice(g * POOL_GC, (g + 1) * POOL_GC)
            ug = u_ref[:, cols]
            cnt = jnp.minimum(rows + 1, w).astype(F32)
            diff = (_window_sum(ug, w, rows, False) / cnt - ug).astype(BF16)
            diff_ref[:, cols] = diff
            y_ref[:, cols] = _dot(diff, w_ref[g].astype(BF16)) * sc_ref[:, cols]

    return pl.pallas_call(
        body, name="pool_fwd",
        out_shape=[jax.ShapeDtypeStruct(u.shape, F32), jax.ShapeDtypeStruct(u.shape, BF16)],
        compiler_params=_params(),
    )(u, pool_w, pool_scale)


def pool_bwd(dy, diff, pool_w, pool_scale):
    s = dy.shape[0]

    def body(dy_ref, diff_ref, w_ref, sc_ref, du_ref, dw_ref, dsc_ref):
        rows = lax.broadcasted_iota(jnp.int32, (s, POOL_GC), 0)
        for g, w in enumerate(POOL_WINDOWS):
            cols = slice(g * POOL_GC, (g + 1) * POOL_GC)
            dyg = dy_ref[:, cols]
            diff = diff_ref[:, cols]
            wb = w_ref[g].astype(BF16)
            dsc_ref[:, cols] = _sum0(dyg * _dot(diff, wb))
            dys = (dyg * sc_ref[:, cols]).astype(BF16)
            dw_ref[g] = _dot_tn(diff, dys)
            ddiff = _dot_nt(dys, wb)
            cnt = jnp.minimum(rows + 1, w).astype(F32)
            du_ref[:, cols] = _window_sum(ddiff / cnt, w, rows, True) - ddiff

    return pl.pallas_call(
        body, name="pool_bwd",
        out_shape=[jax.ShapeDtypeStruct(dy.shape, F32), jax.ShapeDtypeStruct(pool_w.shape, F32),
                   jax.ShapeDtypeStruct(pool_scale.shape, F32)],
        compiler_params=_params(),
    )(dy, diff, pool_w, pool_scale)


def mla_qkv_fwd(cq, ckv, kr, qan, kvan, wq, wkv, cos, sin, rot):
    s = cq.shape[0]
    tm = _row_tile(s)

    def body(cq_ref, ckv_ref, kr_ref, qan_ref, kvan_ref, wq_ref, wkv_ref, cos_ref, sin_ref, rot_ref,
             q_ref, k_ref, v_ref, ql_ref, kvl_ref):
        cos_t = cos_ref[...]
        sin_t = sin_ref[...]
        perm = rot_ref[...]

        def rope(t):
            return t * cos_t + _dot_exact(t, perm) * sin_t

        qhat, _ = _rms(cq_ref[...])
        ql = (qhat * qan_ref[...]).astype(BF16)
        ql_ref[...] = ql
        khat, _ = _rms(ckv_ref[...])
        kvl = (khat * kvan_ref[...]).astype(BF16)
        kvl_ref[...] = kvl
        krr = rope(kr_ref[...]).astype(BF16)
        for h in range(N_HEADS):
            q = _dot_nt(ql, wq_ref[h])
            q_ref[h, :, 0:QK_NOPE] = q[:, 0:QK_NOPE].astype(BF16)
            q_ref[h, :, QK_NOPE:] = rope(q[:, QK_NOPE:]).astype(BF16)
            kv = _dot(kvl, wkv_ref[h])
            k_ref[h, :, 0:QK_NOPE] = kv[:, 0:QK_NOPE].astype(BF16)
            k_ref[h, :, QK_NOPE:] = krr
            v_ref[h] = kv[:, QK_NOPE:].astype(BF16)

    row = lambda w: pl.BlockSpec((tm, w), lambda i: (i, 0))
    hrow = lambda w: pl.BlockSpec((N_HEADS, tm, w), lambda i: (0, i, 0))
    qk = QK_NOPE + QK_ROPE
    return pl.pallas_call(
        body, name="mla_qkv_fwd",
        grid=(s // tm,),
        in_specs=[row(Q_LORA), row(KV_LORA), row(QK_ROPE), _full(qan.shape), _full(kvan.shape),
                  _full(wq.shape), _full(wkv.shape), row(QK_ROPE), row(QK_ROPE), _full(rot.shape)],
        out_specs=[hrow(qk), hrow(qk), hrow(V_HEAD), row(Q_LORA), row(KV_LORA)],
        out_shape=[jax.ShapeDtypeStruct((N_HEADS, s, qk), BF16), jax.ShapeDtypeStruct((N_HEADS, s, qk), BF16),
                   jax.ShapeDtypeStruct((N_HEADS, s, V_HEAD), BF16), jax.ShapeDtypeStruct((s, Q_LORA), BF16),
                   jax.ShapeDtypeStruct((s, KV_LORA), BF16)],
        compiler_params=_params(("arbitrary",)),
    )(cq, ckv, kr, qan, kvan, wq, wkv, cos, sin, rot)


def _attn_probs(q_ref, k_ref, qi, tq):
    n = (qi + 1) * tq
    rows = slice(qi * tq, n)
    sc = _dot_nt(q_ref[rows, :], k_ref[0:n, :]) * SOFTMAX_SCALE
    qpos = qi * tq + lax.broadcasted_iota(jnp.int32, (tq, n), 0)
    kpos = lax.broadcasted_iota(jnp.int32, (tq, n), 1)
    sc = jnp.where(qpos >= kpos, sc, -1e30)
    e = jnp.exp(sc - jnp.max(sc, axis=-1, keepdims=True))
    return e / jnp.sum(e, axis=-1, keepdims=True)


def attn_fwd(q, k, v):
    nh, s, qk = q.shape
    tq = min(s, ATT_TILE)

    def body(q_ref, k_ref, v_ref, o_ref):
        for qi in range(s // tq):
            n = (qi + 1) * tq
            p = _attn_probs(q_ref, k_ref, qi, tq).astype(BF16)
            o_ref[qi * tq:n, :] = _dot(p, v_ref[0:n, :])

    head = lambda w: pl.BlockSpec((None, s, w), lambda h: (h, 0, 0))
    return pl.pallas_call(
        body, name="attn_fwd",
        grid=(nh,),
        in_specs=[head(qk), head(qk), head(V_HEAD)],
        out_specs=pl.BlockSpec((s, V_HEAD), lambda h: (0, h)),
        out_shape=jax.ShapeDtypeStruct((s, nh * V_HEAD), F32),
        compiler_params=_params(("arbitrary",)),
    )(q, k, v)


def attn_bwd(q, k, v, do):
    nh, s, qk = q.shape
    tq = min(s, ATT_TILE)

    def body(q_ref, k_ref, v_ref, do_ref, dq_ref, dk_ref, dv_ref):
        dk_ref[...] = jnp.zeros_like(dk_ref)
        dv_ref[...] = jnp.zeros_like(dv_ref)
        for qi in range(s // tq):
            n = (qi + 1) * tq
            rows = slice(qi * tq, n)
            p = _attn_probs(q_ref, k_ref, qi, tq)
            dob = do_ref[rows, :].astype(BF16)
            dp = _dot_nt(dob, v_ref[0:n, :])
            ds = (p * (dp - jnp.sum(p * dp, axis=-1, keepdims=True)) * SOFTMAX_SCALE).astype(BF16)
            dq_ref[rows, :] = _dot(ds, k_ref[0:n, :])
            dk_ref[0:n, :] += _dot_tn(ds, q_ref[rows, :])
            dv_ref[0:n, :] += _dot_tn(p.astype(BF16), dob)

    head = lambda w: pl.BlockSpec((None, s, w), lambda h: (h, 0, 0))
    return pl.pallas_call(
        body, name="attn_bwd",
        grid=(nh,),
        in_specs=[head(qk), head(qk), head(V_HEAD), pl.BlockSpec((s, V_HEAD), lambda h: (0, h))],
        out_specs=[head(qk), head(qk), head(V_HEAD)],
        out_shape=[jax.ShapeDtypeStruct((nh, s, qk), F32), jax.ShapeDtypeStruct((nh, s, qk), F32),
                   jax.ShapeDtypeStruct((nh, s, V_HEAD), F32)],
        compiler_params=_params(("arbitrary",)),
    )(q, k, v, do)


def mla_qkv_bwd(dq, dk, dv, cq, ckv, qan, kvan, wq, wkv, cos, sin, rot_t):
    s = cq.shape[0]
    tm = _row_tile(s)

    def body(dq_ref, dk_ref, dv_ref, cq_ref, ckv_ref, qan_ref, kvan_ref,
             wq_ref, wkv_ref, cos_ref, sin_ref, rot_ref,
             dcq_ref, dckv_ref, dkro_ref, gq_ref, gkv_ref, dqan_ref, dkvan_ref):
        i = pl.program_id(0)

        @pl.when(i == 0)
        def _():
            dqan_ref[...] = jnp.zeros_like(dqan_ref)
            dkvan_ref[...] = jnp.zeros_like(dkvan_ref)

        cos_t = cos_ref[...]
        sin_t = sin_ref[...]
        perm_t = rot_ref[...]

        def unrope(t):
            return t * cos_t + _dot_exact(t * sin_t, perm_t)

        acc_q = jnp.zeros((tm, Q_LORA), F32)
        acc_kv = jnp.zeros((tm, KV_LORA), F32)
        dkr_sum = jnp.zeros((tm, QK_ROPE), F32)
        for h in range(N_HEADS):
            dq_h = dq_ref[h]
            a = dq_h[:, 0:QK_NOPE].astype(BF16)
            b = unrope(dq_h[:, QK_NOPE:]).astype(BF16)
            gq_ref[h, :, 0:QK_NOPE] = a
            gq_ref[h, :, QK_NOPE:] = b
            wq_h = wq_ref[h]
            acc_q += _dot(a, wq_h[0:QK_NOPE, :]) + _dot(b, wq_h[QK_NOPE:, :])
            dk_h = dk_ref[h]
            dk = dk_h[:, 0:QK_NOPE].astype(BF16)
            dvv = dv_ref[h].astype(BF16)
            gkv_ref[h, :, 0:QK_NOPE] = dk
            gkv_ref[h, :, QK_NOPE:] = dvv
            wkv_h = wkv_ref[h]
            acc_kv += _dot_nt(dk, wkv_h[:, 0:QK_NOPE]) + _dot_nt(dvv, wkv_h[:, QK_NOPE:])
            dkr_sum += dk_h[:, QK_NOPE:]
        dkro_ref[...] = unrope(dkr_sum)
        dcq, dqan = _rms_bwd(acc_q, cq_ref[...], qan_ref[...])
        dcq_ref[...] = dcq
        dqan_ref[...] += dqan
        dckv, dkvan = _rms_bwd(acc_kv, ckv_ref[...], kvan_ref[...])
        dckv_ref[...] = dckv
        dkvan_ref[...] += dkvan

    row = lambda w: pl.BlockSpec((tm, w), lambda i: (i, 0))
    hrow = lambda w: pl.BlockSpec((N_HEADS, tm, w), lambda i: (0, i, 0))
    return pl.pallas_call(
        body, name="mla_qkv_bwd",
        grid=(s // tm,),
        in_specs=[hrow(QK_NOPE + QK_ROPE), hrow(QK_NOPE + QK_ROPE), hrow(V_HEAD),
                  row(Q_LORA), row(KV_LORA), _full(qan.shape), _full(kvan.shape),
                  _full(wq.shape), _full(wkv.shape), row(QK_ROPE), row(QK_ROPE), _full(rot_t.shape)],
        out_specs=[row(Q_LORA), row(KV_LORA), row(QK_ROPE), hrow(QK_NOPE + QK_ROPE), hrow(QK_NOPE + V_HEAD),
                   _full(qan.shape), _full(kvan.shape)],
        out_shape=[jax.ShapeDtypeStruct((s, Q_LORA), F32), jax.ShapeDtypeStruct((s, KV_LORA), F32),
                   jax.ShapeDtypeStruct((s, QK_ROPE), F32),
                   jax.ShapeDtypeStruct((N_HEADS, s, QK_NOPE + QK_ROPE), BF16),
                   jax.ShapeDtypeStruct((N_HEADS, s, QK_NOPE + V_HEAD), BF16),
                   jax.ShapeDtypeStruct(qan.shape, F32), jax.ShapeDtypeStruct(kvan.shape, F32)],
        compiler_params=_params(("arbitrary",)),
    )(dq, dk, dv, cq, ckv, qan, kvan, wq, wkv, cos, sin, rot_t)


def out_proj_fwd(yp, om, w_out, x, gt):
    s, d = x.shape
    n_sh, rs, _ = w_out.shape
    tm = _row_tile(s)
    per = POOL_WIDTH // rs

    def body(yp_ref, om_ref, w_ref, x_ref, gt_ref, xo_ref, ycat_ref, y_ref):
        y = jnp.zeros((tm, d), F32)
        for j in range(n_sh):
            src = yp_ref if j < per else om_ref
            part = src[:, (j % per) * rs:(j % per + 1) * rs].astype(BF16)
            ycat_ref[j] = part.T
            y += _dot(part, w_ref[j])
        y_ref[...] = y.astype(BF16)
        xo_ref[...] = x_ref[...] + gt_ref[...] * y

    row = lambda w: pl.BlockSpec((tm, w), lambda i: (i, 0))
    return pl.pallas_call(
        body, name="out_proj_fwd",
        grid=(s // tm,),
        in_specs=[row(POOL_WIDTH), row(POOL_WIDTH), _full(w_out.shape), row(d), pl.BlockSpec((1, d), lambda i: (0, 0))],
        out_specs=[row(d), pl.BlockSpec((n_sh, rs, tm), lambda i: (0, 0, i)), row(d)],
        out_shape=[jax.ShapeDtypeStruct((s, d), F32), jax.ShapeDtypeStruct((n_sh, rs, s), BF16),
                   jax.ShapeDtypeStruct((s, d), BF16)],
        compiler_params=_params(("arbitrary",)),
    )(yp, om, w_out, x, gt)


def out_proj_bwd(dxn, y, gt, w_out):
    s, d = dxn.shape
    n_sh, rs, _ = w_out.shape
    tm = _row_tile(s)
    per = POOL_WIDTH // rs

    def body(dxn_ref, y_ref, gt_ref, w_ref, dy_ref, dyp_ref, dom_ref, dgt_ref):
        i = pl.program_id(0)

        @pl.when(i == 0)
        def _():
            dgt_ref[...] = jnp.zeros_like(dgt_ref)

        dxn_t = dxn_ref[...]
        dy = (gt_ref[...] * dxn_t).astype(BF16)
        dy_ref[...] = dy
        dgt_ref[...] += _sum0(dxn_t * y_ref[...].astype(F32))
        for j in range(n_sh):
            dst = dyp_ref if j < per else dom_ref
            dst[:, (j % per) * rs:(j % per + 1) * rs] = _dot_nt(dy, w_ref[j])

    row = lambda w: pl.BlockSpec((tm, w), lambda i: (i, 0))
    vec = pl.BlockSpec((1, d), lambda i: (0, 0))
    return pl.pallas_call(
        body, name="out_proj_bwd",
        grid=(s // tm,),
        in_specs=[row(d), row(d), vec, _full(w_out.shape)],
        out_specs=[row(d), row(POOL_WIDTH), row(POOL_WIDTH), vec],
        out_shape=[jax.ShapeDtypeStruct((s, d), BF16), jax.ShapeDtypeStruct((s, POOL_WIDTH), F32),
                   jax.ShapeDtypeStruct((s, POOL_WIDTH), F32), jax.ShapeDtypeStruct((1, d), F32)],
        compiler_params=_params(("arbitrary",)),
    )(dxn, y, gt, w_out)


def final_loss(x, gn, tgt):
    s, d = x.shape
    tm = _row_tile(s)

    def body(x_ref, gn_ref, t_ref, loss_ref, dx_ref, dgn_ref):
        i = pl.program_id(0)

        @pl.when(i == 0)
        def _():
            loss_ref[...] = jnp.zeros_like(loss_ref)
            dgn_ref[...] = jnp.zeros_like(dgn_ref)

        xt = x_ref[...]
        g = gn_ref[...]
        xhat, _ = _rms(xt)
        err = xhat * g - t_ref[...]
        per_tok = jnp.mean(err * err, axis=-1, keepdims=True)
        loss_ref[...] += jnp.broadcast_to(0.5 * _sum0(per_tok), loss_ref.shape)
        dx, dgn = _rms_bwd(err * (1.0 / d), xt, g)
        dx_ref[...] = dx
        dgn_ref[...] += dgn

    row = pl.BlockSpec((tm, d), lambda i: (i, 0))
    vec = pl.BlockSpec((1, d), lambda i: (0, 0))
    return pl.pallas_call(
        body, name="final_loss",
        grid=(s // tm,),
        in_specs=[row, vec, row],
        out_specs=[pl.BlockSpec((1, LANES), lambda i: (0, 0)), row, vec],
        out_shape=[jax.ShapeDtypeStruct((1, LANES), F32), jax.ShapeDtypeStruct((s, d), F32),
                   jax.ShapeDtypeStruct((1, d), F32)],
        compiler_params=_params(("arbitrary",)),
    )(x, gn, tgt)


def _col_tile(cols):
    return 768 if cols % 768 == 0 else cols


def ada_fwd(c16, ada_w, ada_b_loc):
    n_layers, d, cols = ada_w.shape
    tn = _col_tile(cols)

    def body(c_ref, w_ref, b_ref, o_ref):
        cv = c_ref[...]
        ca = (cv * jax.nn.sigmoid(cv)).astype(BF16)
        o_ref[...] = _dot(ca, w_ref[...].astype(BF16)) + b_ref[...]

    return pl.pallas_call(
        body, name="ada_fwd",
        grid=(n_layers, cols // tn),
        in_specs=[pl.BlockSpec((16, d), lambda l, j: (0, 0)), pl.BlockSpec((None, d, tn), lambda l, j: (l, 0, j)),
                  pl.BlockSpec((None, 1, tn), lambda l, j: (l, 0, j))],
        out_specs=pl.BlockSpec((None, 16, tn), lambda l, j: (l, 0, j)),
        out_shape=jax.ShapeDtypeStruct((n_layers, 16, cols), F32),
        compiler_params=_params(("arbitrary", "arbitrary")),
    )(c16, ada_w, ada_b_loc)


def ada_bwd(c16, dmod16):
    n_layers, _, cols = dmod16.shape
    d = c16.shape[1]
    tn = _col_tile(cols)

    def body(c_ref, g_ref, o_ref):
        cv = c_ref[...]
        ca = (cv * jax.nn.sigmoid(cv)).astype(BF16)
        o_ref[...] = _dot_tn(ca, g_ref[...].astype(BF16))

    return pl.pallas_call(
        body, name="ada_bwd",
        grid=(n_layers, cols // tn),
        in_specs=[pl.BlockSpec((16, d), lambda l, j: (0, 0)), pl.BlockSpec((None, 16, tn), lambda l, j: (l, 0, j))],
        out_specs=pl.BlockSpec((None, d, tn), lambda l, j: (l, 0, j)),
        out_shape=jax.ShapeDtypeStruct((n_layers, d, cols), F32),
        compiler_params=_params(("arbitrary", "arbitrary")),
    )(c16, dmod16)


def _as_rows(a):
    if a.ndim == 1:
        return a.reshape(1, a.shape[0])
    return a.reshape(-1, a.shape[-1])


def _rows_tile(r, c, itemsize=4, budget=2 * 1024 * 1024):
    if r * c * itemsize <= budget:
        return r
    best = None
    t = BF16_ROWS
    while t < r:
        if r % t == 0 and t * c * itemsize <= budget:
            best = t
        t += BF16_ROWS
    return best if best is not None else r


def cast_place(w, chip, layers, after):
    _, r, c = w.shape
    n_sel = len(layers)
    tr = _rows_tile(r, c, budget=2 * 1024 * 1024 // n_sel)

    def body(chip_ref, *refs):
        for j in range(n_sel):
            refs[n_sel + 1 + j][...] = refs[j][...].astype(BF16)

    layer_spec = lambda l: pl.BlockSpec((None, tr, c), lambda i, ch: (l, i, 0))
    return list(pl.pallas_call(
        body, name="cast_place",
        grid_spec=pltpu.PrefetchScalarGridSpec(
            num_scalar_prefetch=1, grid=(r // tr,),
            in_specs=[layer_spec(l) for l in layers] + [pl.BlockSpec(memory_space=pl.ANY)],
            out_specs=[pl.BlockSpec((None, tr, c), lambda i, ch: (ch[0], i, 0))] * n_sel),
        out_shape=[jax.ShapeDtypeStruct((N_CHIPS, r, c), BF16)] * n_sel,
        compiler_params=_params(("arbitrary",)),
    )(chip, *([w] * n_sel), after))


def adamw(w, g, m, v, copy_g=False):
    shape = w.shape
    w2, g2, m2, v2 = (_as_rows(t) for t in (w, g, m, v))
    r, c = w2.shape
    tr = _rows_tile(r, c, budget=2 * 1024 * 1024)
    c1 = 1.0 - ADAM_B1 ** ADAM_STEP
    c2 = 1.0 - ADAM_B2 ** ADAM_STEP

    def body(w_ref, g_ref, m_ref, v_ref, d_ref, mo_ref, vo_ref, *go_ref):
        gv = g_ref[...]
        if copy_g:
            go_ref[0][...] = gv
        mn = ADAM_B1 * m_ref[...] + (1.0 - ADAM_B1) * gv
        vn = ADAM_B2 * v_ref[...] + (1.0 - ADAM_B2) * (gv * gv)
        mo_ref[...] = mn
        vo_ref[...] = vn
        d_ref[...] = -ADAM_LR * ((mn / c1) / (jnp.sqrt(vn / c2) + ADAM_EPS) + ADAM_WD * w_ref[...])

    spec = pl.BlockSpec((tr, c), lambda i: (i, 0))
    n_out = 4 if copy_g else 3
    outs = pl.pallas_call(
        body, name="adamw", grid=(r // tr,), in_specs=[spec] * 4, out_specs=[spec] * n_out,
        out_shape=[jax.ShapeDtypeStruct((r, c), F32)] * n_out, compiler_params=_params(("arbitrary",)),
    )(w2, g2, m2, v2)
    g_out = outs[3] if copy_g else g2
    return tuple(o.reshape(shape) for o in (g_out,) + tuple(outs[:3]))


def sum_devices(a):
    n, r, c = a.shape
    tr = _rows_tile(r, c, budget=512 * 1024)

    def body(a_ref, o_ref):
        acc = a_ref[0]
        for j in range(1, n):
            acc = acc + a_ref[j]
        o_ref[...] = acc

    return pl.pallas_call(
        body, name="sum_devices", grid=(r // tr,),
        in_specs=[pl.BlockSpec((n, tr, c), lambda i: (0, i, 0))], out_specs=pl.BlockSpec((tr, c), lambda i: (i, 0)),
        out_shape=jax.ShapeDtypeStruct((r, c), F32), compiler_params=_params(("arbitrary",)),
    )(a)


def _split_axis(r, c):
    if (r // 2) % BF16_ROWS == 0 and r % 2 == 0:
        return 0
    assert c % (2 * LANES) == 0, (r, c)
    return 1


def _half_shape(r, c):
    return (r // 2, c) if _split_axis(r, c) == 0 else (r, c // 2)


def _half_at(ref, lead, which):
    r, c = ref.shape[-2:]
    if _split_axis(r, c) == 0:
        return ref.at[(*lead, pl.ds(which * (r // 2), r // 2), slice(None))]
    return ref.at[(*lead, slice(None), pl.ds(which * (c // 2), c // 2))]


def _half_spec(r, c, lead_block, imap):
    hr, hc = _half_shape(r, c)
    if _split_axis(r, c) == 0:
        return pl.BlockSpec((*lead_block, hr, hc), lambda *a: (*imap(*a)[0], imap(*a)[1], 0))
    return pl.BlockSpec((*lead_block, hr, hc), lambda *a: (*imap(*a)[0], 0, imap(*a)[1]))


def pair_add(gs, ras, sel):
    n = len(gs)
    n_sl = gs[0].shape[0]
    halves = [_half_shape(*g.shape[1:]) for g in gs]

    def body(s_ref, *refs):
        g_refs, ra_refs, pb_refs, own_refs = (refs[i * n:(i + 1) * n] for i in range(4))
        k = pl.program_id(0)
        for t in range(n):
            p = g_refs[t][...] + ra_refs[t][...]
            pb_refs[t][...] = p.astype(BF16)

            @pl.when(k == s_ref[1])
            def _(p=p, own=own_refs[t]):
                own[...] = p

    slot = lambda hs: pl.BlockSpec((None,) + hs, lambda k, sr: (k, 0, 0))
    outs = pl.pallas_call(
        body, name="pair_add",
        grid_spec=pltpu.PrefetchScalarGridSpec(
            num_scalar_prefetch=1, grid=(n_sl,),
            in_specs=[_half_spec(*g.shape[1:], (None,), lambda k, sr: ((k,), sr[0])) for g in gs]
            + [slot(hs) for hs in halves],
            out_specs=[slot(hs) for hs in halves] + [pl.BlockSpec(hs, lambda k, sr: (0, 0)) for hs in halves]),
        out_shape=[jax.ShapeDtypeStruct((n_sl,) + hs, BF16) for hs in halves]
        + [jax.ShapeDtypeStruct(hs, F32) for hs in halves],
        compiler_params=_params(("arbitrary",)),
    )(sel, *gs, *ras)
    return list(outs[:n]), list(outs[n:])


def chip_sum(owns, rbs, sel, shapes, accs):
    n = len(owns)
    fresh = accs[0] is None

    def body(s_ref, *refs):
        own_refs, rb_refs, o_refs = refs[:n], refs[n:2 * n], refs[-n:]
        for t in range(n):
            acc_v = own_refs[t][...]
            for j in range(N_CHIPS - 1):
                acc_v = acc_v + rb_refs[t][j].astype(F32)
            o_refs[t][...] = acc_v

    in_specs = ([pl.BlockSpec(o.shape, lambda i, sr: (0, 0)) for o in owns]
                + [pl.BlockSpec(rb.shape, lambda i, sr: (0, 0, 0)) for rb in rbs])
    args = [sel, *owns, *rbs]
    aliases = {}
    if not fresh:
        in_specs += [pl.BlockSpec(memory_space=pl.ANY)] * n
        args += list(accs)
        aliases = {1 + 2 * n + t: t for t in range(n)}
    return list(pl.pallas_call(
        body, name="chip_sum",
        grid_spec=pltpu.PrefetchScalarGridSpec(
            num_scalar_prefetch=1, grid=(1,), in_specs=in_specs,
            out_specs=[_half_spec(*shp[1:], (None,), lambda i, sr: ((sr[2],), sr[0])) for shp in shapes]),
        out_shape=[jax.ShapeDtypeStruct(shp, F32) for shp in shapes],
        input_output_aliases=aliases,
        compiler_params=_params(("arbitrary",)),
    )(*args))


def _me():
    return lax.axis_index("x"), lax.axis_index("y"), lax.axis_index("c")


def _flip(v, bit):
    return 1 - v if bit else v


def exchange8(xs, bcast):
    blk = xs.shape if bcast else xs.shape[1:]

    def body(x_ref, o_ref, send_sems, recv_sems, loc_sem):
        mx, my, mc = _me()
        me = 4 * mx + 2 * my + mc
        src = (lambda j: x_ref) if bcast else (lambda j: x_ref.at[j])
        loc = pltpu.make_async_copy(src(me), o_ref.at[me], loc_sem)
        loc.start()
        copies = []
        for o in range(1, N_DEV):
            px, py, pc = _flip(mx, o & 4), _flip(my, o & 2), _flip(mc, o & 1)
            cp = pltpu.make_async_remote_copy(
                src_ref=src(4 * px + 2 * py + pc), dst_ref=o_ref.at[me],
                send_sem=send_sems.at[o - 1], recv_sem=recv_sems.at[o - 1],
                device_id=(px, py, pc), device_id_type=MESH)
            cp.start()
            copies.append(cp)
        for cp in copies:
            cp.wait()
        loc.wait()

    return pl.pallas_call(
        body, name="exchange8_gather" if bcast else "exchange8_a2a",
        in_specs=[pl.BlockSpec(memory_space=pltpu.VMEM)], out_specs=pl.BlockSpec(memory_space=pltpu.VMEM),
        out_shape=jax.ShapeDtypeStruct((N_DEV,) + tuple(blk), xs.dtype),
        scratch_shapes=[pltpu.SemaphoreType.DMA((N_DEV - 1,)), pltpu.SemaphoreType.DMA((N_DEV - 1,)), pltpu.SemaphoreType.DMA],
        compiler_params=_params(),
    )(xs)


HBM = pl.BlockSpec(memory_space=pltpu.HBM)
SEM = pl.BlockSpec(memory_space=pltpu.SEMAPHORE)
EFFECT = pltpu.SideEffectType.DATAFLOW_SIDE_EFFECTING


def _hbm(a):
    return pltpu.with_memory_space_constraint(a, pltpu.HBM)


def _ici_copy(land, o, send_sem, recv_sem, sending):
    mx, my, mc = _me()
    px, py = _flip(mx, o & 2), _flip(my, o & 1)
    mine = _half_at(land, (2 * mx + my,), mc)
    return pltpu.make_async_remote_copy(
        src_ref=mine, dst_ref=mine if sending else _half_at(land, (2 * px + py,), mc),
        send_sem=send_sem, recv_sem=recv_sem, device_id=(px, py, mc), device_id_type=MESH)


N_PEERS = N_CHIPS - 1
DMA_SEM = pltpu.SemaphoreType.DMA(())


def gather_start(lands, groups, after, tag):
    n_layers, n = len(lands), len(lands[0])
    flat = [a for layer in lands for a in layer]
    n_in = n * n_layers
    n_grp = len(groups)
    n_sem = 2 * n_layers * n_grp * N_PEERS
    first = lambda l, g, recv: ((l * n_grp + g) * 2 + recv) * N_PEERS

    def body(*refs):
        land = refs[:n_in]
        sems = refs[n_in + 1:n_in + 1 + n_sem]
        token = refs[-1]
        for l in range(n_layers):
            for g, members in enumerate(groups):
                for t in members:
                    for o in range(1, N_CHIPS):
                        _ici_copy(land[l * n + t], o, sems[first(l, g, 0) + o - 1], sems[first(l, g, 1) + o - 1],
                                  True).start()
        token[...] = jnp.zeros_like(token)

    outs = pl.pallas_call(
        body, name=f"gather_start_{tag}",
        in_specs=[HBM] * n_in + [pl.BlockSpec(memory_space=pl.ANY)],
        out_specs=[SEM] * n_sem + [HBM] * n_in + [pl.BlockSpec(memory_space=pltpu.VMEM)],
        out_shape=[DMA_SEM] * n_sem + [pltpu.HBM(a.shape, a.dtype) for a in flat]
        + [jax.ShapeDtypeStruct((8, LANES), F32)],
        input_output_aliases={i: i + n_sem for i in range(n_in)},
        compiler_params=pltpu.CompilerParams(has_side_effects=EFFECT),
    )(*[_hbm(a) for a in flat], after)
    sems = [[(list(outs[first(l, g, 0):first(l, g, 0) + N_PEERS]), list(outs[first(l, g, 1):first(l, g, 1) + N_PEERS]))
             for g in range(n_grp)] for l in range(n_layers)]
    lands_thru = [list(outs[n_sem + l * n:n_sem + (l + 1) * n]) for l in range(n_layers)]
    return sems, lands_thru, outs[-1]


def gather_wait(tag, sems, lands, after):
    n = len(lands)
    send_sems, recv_sems = sems

    def body(*refs):
        land = refs[:n]
        send_r = refs[n:n + N_PEERS]
        recv_r = refs[n + N_PEERS:n + 2 * N_PEERS]
        for t in range(n):
            for o in range(1, N_CHIPS):
                _ici_copy(land[t], o, send_r[o - 1], recv_r[o - 1], True).wait_send()
                _ici_copy(land[t], o, send_r[o - 1], recv_r[o - 1], False).wait_recv()

    return list(pl.pallas_call(
        body, name=f"gather_wait_{tag}",
        in_specs=[HBM] * n + [SEM] * (2 * N_PEERS) + [pl.BlockSpec(memory_space=pl.ANY)],
        out_specs=[HBM] * n,
        out_shape=[pltpu.HBM(a.shape, a.dtype) for a in lands],
        input_output_aliases={i: i for i in range(n)},
        compiler_params=pltpu.CompilerParams(has_side_effects=EFFECT),
    )(*lands, *send_sems, *recv_sems, after))


def gather_forward(lands):
    n = len(lands)

    def body(*refs):
        dst = refs[n:2 * n]
        send_sems, recv_sems = refs[2 * n:]
        mx, my, mc = _me()
        fwds = []
        for t in range(n):
            for o in range(1, N_CHIPS):
                slot = 2 * _flip(mx, o & 2) + _flip(my, o & 1)
                mine = _half_at(dst[t], (slot,), mc)
                theirs = _half_at(dst[t], (slot,), 1 - mc)
                cp = pltpu.make_async_remote_copy(
                    src_ref=mine, dst_ref=mine, send_sem=send_sems.at[t, o - 1], recv_sem=recv_sems.at[t, o - 1],
                    device_id=(mx, my, 1 - mc), device_id_type=MESH)
                cp.start()
                fwds.append((cp, pltpu.make_async_remote_copy(
                    src_ref=theirs, dst_ref=theirs, send_sem=send_sems.at[t, o - 1], recv_sem=recv_sems.at[t, o - 1],
                    device_id=(mx, my, 1 - mc), device_id_type=MESH)))
        for cp, arrival in fwds:
            cp.wait_send()
            arrival.wait_recv()

    any_spec = pl.BlockSpec(memory_space=pl.ANY)
    return list(pl.pallas_call(
        body, name="gather_forward",
        in_specs=[any_spec] * n, out_specs=[any_spec] * n,
        out_shape=[jax.ShapeDtypeStruct(a.shape, a.dtype) for a in lands],
        input_output_aliases={t: t for t in range(n)},
        scratch_shapes=[pltpu.SemaphoreType.DMA((n, N_CHIPS - 1)), pltpu.SemaphoreType.DMA((n, N_CHIPS - 1))],
        compiler_params=_params(),
    )(*lands))


def _scatter_copy(src, land, o, send_sem, recv_sem):
    mx, my, mc = _me()
    px, py = _flip(mx, o & 2), _flip(my, o & 1)
    return pltpu.make_async_remote_copy(
        src_ref=src.at[2 * px + py], dst_ref=land.at[o - 1],
        send_sem=send_sem, recv_sem=recv_sem, device_id=(px, py, mc), device_id_type=MESH)


def scatter_start(pbs, tag, after):
    n = len(pbs)
    lands = [lax.empty((N_CHIPS - 1,) + p.shape[1:], p.dtype) for p in pbs]

    def body(*refs):
        src = refs[:n]
        land = refs[n:2 * n]
        send_sems = refs[2 * n + 1:2 * n + 1 + N_PEERS]
        recv_sems = refs[2 * n + 1 + N_PEERS:2 * n + 1 + 2 * N_PEERS]
        token = refs[-1]
        for t in range(n):
            for o in range(1, N_CHIPS):
                _scatter_copy(src[t], land[t], o, send_sems[o - 1], recv_sems[o - 1]).start()
        token[...] = jnp.zeros_like(token)

    n_sem = 2 * N_PEERS
    arrs = list(pbs) + lands
    outs = pl.pallas_call(
        body, name=f"scatter_start_{tag}",
        in_specs=[HBM] * (2 * n) + [pl.BlockSpec(memory_space=pl.ANY)],
        out_specs=[SEM] * n_sem + [HBM] * (2 * n) + [pl.BlockSpec(memory_space=pltpu.VMEM)],
        out_shape=[DMA_SEM] * n_sem + [pltpu.HBM(a.shape, a.dtype) for a in arrs]
        + [jax.ShapeDtypeStruct((8, LANES), F32)],
        input_output_aliases={i: i + n_sem for i in range(2 * n)},
        compiler_params=pltpu.CompilerParams(has_side_effects=EFFECT),
    )(*[_hbm(a) for a in arrs], after)
    return (list(outs[:N_PEERS]), list(outs[N_PEERS:n_sem]), list(outs[n_sem:n_sem + n]),
            list(outs[n_sem + n:n_sem + 2 * n]), outs[-1])


def scatter_wait(tag, send_sems, recv_sems, pbs, lands, after):
    n = len(pbs)

    def body(*refs):
        src = refs[:n]
        land = refs[n:2 * n]
        send_r = refs[2 * n:2 * n + N_PEERS]
        recv_r = refs[2 * n + N_PEERS:2 * n + 2 * N_PEERS]
        for t in range(n):
            for o in range(1, N_CHIPS):
                cp = _scatter_copy(src[t], land[t], o, send_r[o - 1], recv_r[o - 1])
                cp.wait_send()
                cp.wait_recv()

    arrs = list(pbs) + list(lands)
    outs = pl.pallas_call(
        body, name=f"scatter_wait_{tag}",
        in_specs=[HBM] * (2 * n) + [SEM] * (2 * N_PEERS) + [pl.BlockSpec(memory_space=pl.ANY)],
        out_specs=[HBM] * (2 * n),
        out_shape=[pltpu.HBM(a.shape, a.dtype) for a in arrs],
        input_output_aliases={i: i for i in range(2 * n)},
        compiler_params=pltpu.CompilerParams(has_side_effects=EFFECT),
    )(*arrs, *send_sems, *recv_sems, after)
    return list(outs[n:])


def _pair_copy(src, land, send_sem, recv_sem):
    mx, my, mc = _me()
    return pltpu.make_async_remote_copy(
        src_ref=_half_at(src, (slice(None),), 1 - mc), dst_ref=land, send_sem=send_sem, recv_sem=recv_sem,
        device_id=(mx, my, 1 - mc), device_id_type=MESH)


def pair_start(gs, tag, after):
    n = len(gs)
    lands = [lax.empty((g.shape[0],) + _half_shape(*g.shape[1:]), g.dtype) for g in gs]

    def body(*refs):
        src = refs[:n]
        land = refs[n:2 * n]
        send_sem, recv_sem = refs[2 * n + 1], refs[2 * n + 2]
        token = refs[-1]
        for t in range(n):
            _pair_copy(src[t], land[t], send_sem, recv_sem).start()
        token[...] = jnp.zeros_like(token)

    arrs = list(gs) + lands
    outs = pl.pallas_call(
        body, name=f"pair_start_{tag}",
        in_specs=[HBM] * (2 * n) + [pl.BlockSpec(memory_space=pl.ANY)],
        out_specs=[SEM, SEM] + [HBM] * (2 * n) + [pl.BlockSpec(memory_space=pltpu.VMEM)],
        out_shape=[DMA_SEM, DMA_SEM] + [pltpu.HBM(a.shape, a.dtype) for a in arrs] + [jax.ShapeDtypeStruct((8, LANES), F32)],
        input_output_aliases={i: i + 2 for i in range(2 * n)},
        compiler_params=pltpu.CompilerParams(has_side_effects=EFFECT),
    )(*[_hbm(a) for a in arrs], after)
    return outs[0], outs[1], list(outs[2:2 + n]), list(outs[2 + n:2 + 2 * n]), outs[-1]


def pair_wait(tag, send_sem, recv_sem, gs, lands, after):
    n = len(gs)

    def body(*refs):
        src = refs[:n]
        land = refs[n:2 * n]
        send_r, recv_r = refs[2 * n], refs[2 * n + 1]
        for t in range(n):
            cp = _pair_copy(src[t], land[t], send_r, recv_r)
            cp.wait_send()
            cp.wait_recv()

    arrs = list(gs) + list(lands)
    outs = pl.pallas_call(
        body, name=f"pair_wait_{tag}",
        in_specs=[HBM] * (2 * n) + [SEM, SEM, pl.BlockSpec(memory_space=pl.ANY)],
        out_specs=[HBM] * (2 * n),
        out_shape=[pltpu.HBM(a.shape, a.dtype) for a in arrs],
        input_output_aliases={i: i for i in range(2 * n)},
        compiler_params=pltpu.CompilerParams(has_side_effects=EFFECT),
    )(*arrs, send_sem, recv_sem, after)
    return list(outs[:n]), list(outs[n:])


def _gather8_copy(x, land, o, send_sem, recv_sem, sending):
    mx, my, mc = _me()
    px, py, pc = _flip(mx, o & 4), _flip(my, o & 2), _flip(mc, o & 1)
    slot = 4 * mx + 2 * my + mc if sending else 4 * px + 2 * py + pc
    return pltpu.make_async_remote_copy(
        src_ref=x, dst_ref=land.at[slot], send_sem=send_sem, recv_sem=recv_sem,
        device_id=(px, py, pc), device_id_type=MESH)


def gather8_start(x, land, after, tag):
    n_peer = N_DEV - 1

    def body(x_ref, land_ref, after_ref, *rest):
        send_sems, recv_sems = rest[:n_peer], rest[n_peer:2 * n_peer]
        token = rest[-1]
        for o in range(1, N_DEV):
            _gather8_copy(x_ref, land_ref, o, send_sems[o - 1], recv_sems[o - 1], True).start()
        token[...] = jnp.zeros_like(token)

    outs = pl.pallas_call(
        body, name=f"gather8_start_{tag}",
        in_specs=[HBM, HBM, pl.BlockSpec(memory_space=pl.ANY)],
        out_specs=[SEM] * (2 * n_peer) + [HBM, HBM, pl.BlockSpec(memory_space=pltpu.VMEM)],
        out_shape=[DMA_SEM] * (2 * n_peer) + [pltpu.HBM(x.shape, x.dtype), pltpu.HBM(land.shape, land.dtype),
                                              jax.ShapeDtypeStruct((8, LANES), F32)],
        input_output_aliases={0: 2 * n_peer, 1: 2 * n_peer + 1},
        compiler_params=pltpu.CompilerParams(has_side_effects=EFFECT),
    )(_hbm(x), _hbm(land), after)
    return list(outs[:n_peer]), list(outs[n_peer:2 * n_peer]), outs[2 * n_peer], outs[2 * n_peer + 1], outs[-1]


def gather8_wait(tag, send_sems, recv_sems, x, land, after):
    n_peer = N_DEV - 1

    def body(x_ref, land_ref, *rest):
        send_r, recv_r = rest[:n_peer], rest[n_peer:2 * n_peer]
        for o in range(1, N_DEV):
            _gather8_copy(x_ref, land_ref, o, send_r[o - 1], recv_r[o - 1], True).wait_send()
            _gather8_copy(x_ref, land_ref, o, send_r[o - 1], recv_r[o - 1], False).wait_recv()

    return pl.pallas_call(
        body, name=f"gather8_wait_{tag}",
        in_specs=[HBM, HBM] + [SEM] * (2 * n_peer) + [pl.BlockSpec(memory_space=pl.ANY)],
        out_specs=[HBM, HBM],
        out_shape=[pltpu.HBM(x.shape, x.dtype), pltpu.HBM(land.shape, land.dtype)],
        input_output_aliases={0: 0, 1: 1},
        compiler_params=pltpu.CompilerParams(has_side_effects=EFFECT),
    )(x, land, *send_sems, *recv_sems, after)[1]


def pair_fill_halves(fs):
    n = len(fs)

    def body(*refs):
        dst = refs[n:2 * n]
        send_sems, recv_sems = refs[2 * n:]
        mx, my, mc = _me()
        copies = []
        for t in range(n):
            mine = _half_at(dst[t], (slice(None),), mc)
            theirs = _half_at(dst[t], (slice(None),), 1 - mc)
            cp = pltpu.make_async_remote_copy(
                src_ref=mine, dst_ref=mine, send_sem=send_sems.at[t], recv_sem=recv_sems.at[t],
                device_id=(mx, my, 1 - mc), device_id_type=MESH)
            cp.start()
            copies.append((cp, pltpu.make_async_remote_copy(
                src_ref=theirs, dst_ref=theirs, send_sem=send_sems.at[t], recv_sem=recv_sems.at[t],
                device_id=(mx, my, 1 - mc), device_id_type=MESH)))
        for cp, arrival in copies:
            cp.wait_send()
            arrival.wait_recv()

    any_spec = pl.BlockSpec(memory_space=pl.ANY)
    return pl.pallas_call(
        body, name="pair_fill_halves",
        in_specs=[any_spec] * n, out_specs=[any_spec] * n,
        out_shape=[jax.ShapeDtypeStruct(f.shape, f.dtype) for f in fs],
        input_output_aliases={t: t for t in range(n)},
        scratch_shapes=[pltpu.SemaphoreType.DMA((n,)), pltpu.SemaphoreType.DMA((n,))],
        compiler_params=_params(),
    )(*fs)


def _pack_rows(parts, d):
    rows, spans = [], []
    at = 0
    for p in parts:
        flat = p.reshape(-1)
        n_rows = -(-flat.shape[0] // (8 * d)) * 8
        flat = jnp.pad(flat, (0, n_rows * d - flat.shape[0]))
        rows.append(flat.reshape(n_rows, d))
        spans.append((at, p.shape))
        at += n_rows
    return jnp.concatenate(rows, axis=0), spans


def _unpack_rows(packed, spans):
    lead, d = packed.shape[:-2], packed.shape[-1]
    out = []
    for at, shape in spans:
        n = math.prod(shape)
        n_rows = -(-n // d)
        out.append(packed[..., at:at + n_rows, :].reshape(lead + (-1,))[..., :n].reshape(lead + tuple(shape)))
    return out


def _rotate_half_matrix():
    half = QK_ROPE // 2
    idx = jnp.arange(QK_ROPE)
    src = jnp.where(idx < half, idx + half, idx - half)
    sign = jnp.where(idx < half, -1.0, 1.0)
    return (jnp.zeros((QK_ROPE, QK_ROPE), F32).at[src, idx].set(sign)).astype(BF16)


def kernel(x, c, positions, ada_w, ada_b, ffn1_norm, ffn1_w_gate, ffn1_w_up, ffn1_w_down, mix_norm, w_in, pool_w, pool_scale, q_a_norm, w_q_b, kv_a_norm, w_kv_b, w_out, ffn2_norm, ffn2_w_gate, ffn2_w_up, ffn2_w_down, final_norm, loss_target, m_ada_w, m_ada_b, m_ffn1_norm, m_ffn1_w_gate, m_ffn1_w_up, m_ffn1_w_down, m_mix_norm, m_w_in, m_pool_w, m_pool_scale, m_q_a_norm, m_w_q_b, m_kv_a_norm, m_w_kv_b, m_w_out, m_ffn2_norm, m_ffn2_w_gate, m_ffn2_w_up, m_ffn2_w_down, m_final_norm, v_ada_w, v_ada_b, v_ffn1_norm, v_ffn1_w_gate, v_ffn1_w_up, v_ffn1_w_down, v_mix_norm, v_w_in, v_pool_w, v_pool_scale, v_q_a_norm, v_w_q_b, v_kv_a_norm, v_w_kv_b, v_w_out, v_ffn2_norm, v_ffn2_w_gate, v_ffn2_w_up, v_ffn2_w_down, v_final_norm):
    mx, my, mc = _me()
    chip = 2 * mx + my
    half = jnp.reshape(mc, (1,)).astype(jnp.int32)
    chip1 = jnp.reshape(chip, (1,)).astype(jnp.int32)
    n_layers, d, ada_cols = ada_w.shape
    xt = x[0]
    tgt = loss_target[0]

    inv_freq = 1.0 / (ROPE_THETA ** (jnp.arange(0, QK_ROPE, 2, dtype=F32) / QK_ROPE))
    ang = positions[0].astype(F32)[:, None] * inv_freq
    ang = jnp.concatenate([ang, ang], axis=-1)
    cos, sin = jnp.cos(ang), jnp.sin(ang)
    rot = _rotate_half_matrix()
    rot_t = rot.T

    c_all = exchange8(c, True).reshape(N_DEV, d)
    c16 = jnp.pad(c_all, ((0, 8), (0, 0)))
    ada_b_loc = lax.dynamic_slice_in_dim(ada_b, chip * ada_cols, ada_cols, axis=1).reshape(n_layers, 1, ada_cols)
    mod_part = ada_fwd(c16, ada_w, ada_b_loc)[:, :N_DEV]
    mod_got = exchange8(jnp.transpose(mod_part, (1, 0, 2)), False)
    mod = jnp.transpose(mod_got.reshape(N_CHIPS, 2, n_layers, ada_cols)[:, 0], (1, 0, 2))
    mod = mod.reshape(n_layers, 9, 1, d)

    tr = lambda a: jnp.transpose(a, (0, 2, 1))
    local = [tr(ffn1_w_gate), tr(ffn1_w_up), ffn1_w_down, tr(w_in), tr(w_q_b), w_kv_b, w_out,
             tr(ffn2_w_gate), tr(ffn2_w_up), ffn2_w_down]
    ffn1_pos, rest_pos = (0, 1, 2), tuple(range(3, len(local)))
    groups = (ffn1_pos, rest_pos)
    placed = [cast_place(w, chip1, (0,), mod) for w in local]
    g_sems, lands_fly, g_token = gather_start([[p[0] for p in placed]], groups, mod, "first")
    if n_layers > 1:
        later = tuple(range(1, n_layers))
        placed = [cast_place(w, chip1, later, g_token) for w in local]
        more_sems, more_fly, g_token = gather_start(
            [[p[j] for p in placed] for j in range(len(later))], groups, g_token, "rest")
        g_sems, lands_fly = g_sems + more_sems, lands_fly + more_fly
    gathered = []

    row = lambda a, l: a[l].reshape(1, -1)
    saved = []
    for l in range(n_layers):
        g1, u1, d1 = gather_forward(gather_wait(
            f"{l}a", g_sems[l][0], [lands_fly[l][t] for t in ffn1_pos], xt if l else g_token))
        sv = dict(x0=xt)
        xt, sv["h1"], sv["a1"], sv["sl1"], sv["dsu1"], sv["y1"] = ffn_fwd(
            xt, row(ffn1_norm, l), mod[l, 0], mod[l, 1], mod[l, 2], g1, u1, d1)
        sv["x1"] = xt
        win, wq, wkv, wout, g2, u2, d2 = gather_forward(gather_wait(
            f"{l}b", g_sems[l][1], [lands_fly[l][t] for t in rest_pos], xt))
        gathered.append([g1, u1, d1, win, wq, wkv, wout, g2, u2, d2])
        win = win.reshape(-1, d)
        sv["h2"], u, cq, ckv, kr = mix_in_fwd(xt, row(mix_norm, l), mod[l, 3], mod[l, 4], win)
        sv["cq"], sv["ckv"] = cq, ckv
        yp, sv["diff"] = pool_fwd(u, pool_w[l], row(pool_scale, l))
        qh, kh, vh, sv["ql"], sv["kvl"] = mla_qkv_fwd(
            cq, ckv, kr, row(q_a_norm, l), row(kv_a_norm, l), wq, wkv, cos, sin, rot)
        sv["qkv"] = (qh, kh, vh)
        om = attn_fwd(qh, kh, vh)
        xt, sv["ycat"], sv["y2"] = out_proj_fwd(yp, om, wout, xt, mod[l, 5])
        sv["x2"] = xt
        xt, sv["h3"], sv["a3"], sv["sl3"], sv["dsu3"], sv["y3"] = ffn_fwd(
            xt, row(ffn2_norm, l), mod[l, 6], mod[l, 7], mod[l, 8], g2, u2, d2)
        saved.append(sv)

    loss_vec, dx, d_final_norm = final_loss(xt, final_norm.reshape(1, d), tgt)
    loss = lax.psum(loss_vec[0, 0], ("x", "y", "c"))

    none = [None] * n_layers
    dmods, dnorm1, dnorm2, dnorm3 = list(none), list(none), list(none), list(none)
    dpw, dps, dqan_l, dkvan_l = list(none), list(none), list(none), list(none)
    reduced = [None] * len(local)
    stages = []
    sel_of = lambda l: jnp.stack([mc, chip, jnp.asarray(l, mc.dtype)]).astype(jnp.int32)

    def to_chips(job, after_wait, after_start):
        send, recv, g_fly, lands_p = job.pop("pair")
        g_fly, got = pair_wait(job["tag"], send, recv, g_fly, lands_p, after_wait)
        pbs, job["owns"] = pair_add(g_fly, got, sel_of(job["l"]))
        job["scatter"] = scatter_start(pbs, job["tag"], after_start)
        return job["scatter"][4][0, 0]

    def finish(job, after):
        s_send, s_recv, pbs_fly, lands_j, _ = job.pop("scatter")
        parts = scatter_wait(job["tag"], s_send, s_recv, pbs_fly, lands_j, after)
        sums = chip_sum(job["owns"], parts, sel_of(job["l"]), [(n_layers,) + shp for shp in job["shapes"]],
                        [reduced[t] for t in job["pos"]])
        for t, total_t in zip(job["pos"], sums):
            reduced[t] = total_t

    def checkpoint(tag, l, positions, grads_, done, before_scatter=None):
        send, recv, g_fly, lands_p, tok = pair_start(grads_, tag, done)
        order = tok[0, 0]
        if stages:
            order = order + to_chips(stages[-1], done, done if before_scatter is None else before_scatter)
        if len(stages) >= 3:
            finish(stages[-3], done)
        stages.append(dict(tag=tag, l=l, pos=positions, shapes=[g.shape[1:] for g in grads_],
                           pair=(send, recv, g_fly, lands_p)))
        return order

    def small_gather(tag, parts, after):
        packed, spans = _pack_rows(parts, d)
        land = lax.dynamic_update_index_in_dim(lax.empty((N_DEV,) + packed.shape, F32), packed, 4 * mx + 2 * my + mc, 0)
        return gather8_start(packed, land, after, tag), spans

    order = None

    for l in reversed(range(n_layers)):
        sv = saved[l]
        g1, u1, d1, win, wq, wkv, wout, g2, u2, d2 = gathered[l]
        win = win.reshape(-1, d)
        gt3 = mod[l, 8] if order is None else mod[l, 8] + order
        dy, dgt, dup = ffn_bwd_act(dx, sv["sl3"], sv["dsu3"], gt3, d2)
        dx, dvec3 = ffn_bwd_in(dx, sv["x2"], sv["y3"], dgt, dup, row(ffn2_norm, l), mod[l, 7], g2, u2)
        g_g2, g_u2, g_d2 = tn_mm(dgt, sv["h3"][None]), tn_mm(dup, sv["h3"][None]), nn_mm(sv["a3"], dy)
        dy2, dyp, dom, dg2 = out_proj_bwd(dx, sv["y2"], mod[l, 5], wout)
        g_wout = nn_mm(sv["ycat"], dy2)
        qh, kh, vh = sv["qkv"]
        dqh, dkh, dvh = attn_bwd(qh, kh, vh, dom)
        dcq, dckv, dkr_in, gq, gkv, dqan_l[l], dkvan_l[l] = mla_qkv_bwd(
            dqh, dkh, dvh, sv["cq"], sv["ckv"], row(q_a_norm, l), row(kv_a_norm, l), wq, wkv, cos, sin, rot_t)
        g_wq, g_wkv = tn_mm(gq, sv["ql"][None]), tn_mm(sv["kvl"][None], gkv)
        du, dpw[l], dps[l] = pool_bwd(dyp, sv["diff"], pool_w[l], row(pool_scale, l))
        dx, dz, dvec2 = mix_in_bwd(dx, du, dcq, dckv, dkr_in, sv["x1"], row(mix_norm, l), mod[l, 4], win)
        g_win = nn_mm(dz[None], sv["h2"]).reshape(N_CHIPS, -1, d)
        dnorm2[l], dnorm3[l] = dvec2[3], dvec3[3]
        dmod_rest = jnp.concatenate([dvec2[0:2], dg2, dvec3[0:3]], axis=0)
        if l == 0:
            early = small_gather("early", [jnp.stack(dmods[1:]), dmod_rest, jnp.stack(dnorm1[1:]), jnp.stack(dnorm2),
                                           jnp.stack(dnorm3), d_final_norm, jnp.stack(dps), jnp.stack(dqan_l),
                                           jnp.stack(dkvan_l), jnp.stack(dpw)], dx)
        order = checkpoint(f"{l}a", l, rest_pos, [g_win, g_wq, g_wkv, g_wout, g_g2, g_u2, g_d2], dx,
                           early[0][4] if l == 0 else None)
        dy, dgt, dup = ffn_bwd_act(dx, sv["sl1"], sv["dsu1"], mod[l, 2] + order, d1)
        dx, dvec1 = ffn_bwd_in(dx, sv["x0"], sv["y1"], dgt, dup, row(ffn1_norm, l), mod[l, 1], g1, u1)
        g_g1, g_u1, g_d1 = tn_mm(dgt, sv["h1"][None]), tn_mm(dup, sv["h1"][None]), nn_mm(sv["a1"], dy)
        dmods[l] = jnp.concatenate([dvec1[0:3], dmod_rest], axis=0)
        dnorm1[l] = dvec1[3]
        if l == 0:
            late = small_gather("late", [dvec1[0:3], dvec1[3]], dx)
        order = checkpoint(f"{l}b", l, ffn1_pos, [g_g1, g_u1, g_d1], dx, late[0][4] if l == 0 else None)

    to_chips(stages[-1], stages[-2]["scatter"][4], stages[-2]["scatter"][4])
    sent = stages[-1]["scatter"][4]
    got_early = gather8_wait("early", *early[0][:4], sent)
    got_late = gather8_wait("late", *late[0][:4], sent)
    (g_dmod_rest, g_dmod0_rest, g_n1_rest, g_n2, g_n3, g_fn, g_ps, g_qan, g_kvan, g_pw) = _unpack_rows(
        sum_devices(got_early), early[1])
    g_dmod0_first, g_n1_first = _unpack_rows(sum_devices(got_late), late[1])
    g_ada_b = jnp.concatenate([jnp.concatenate([g_dmod0_first, g_dmod0_rest], axis=0)[None], g_dmod_rest], axis=0)
    g_n1 = jnp.concatenate([g_n1_first[None], g_n1_rest], axis=0)
    each_rest, each0_rest = _unpack_rows(got_early, early[1])[:2]
    each0_first = _unpack_rows(got_late, late[1])[0]
    dmod_all = jnp.concatenate([jnp.concatenate([each0_first, each0_rest], axis=1)[:, None], each_rest], axis=1)
    dmod_all = dmod_all.reshape(N_DEV, n_layers, 9 * d)
    dmod_loc = lax.dynamic_slice_in_dim(dmod_all, chip * ada_cols, ada_cols, axis=2)
    dmod16 = jnp.pad(jnp.transpose(dmod_loc, (1, 0, 2)), ((0, 0), (0, 8), (0, 0)))
    g_ada_w = ada_bwd(c16, dmod16)

    grads = [g_ada_w, g_ada_b, g_n1, None, None, None, g_n2, None, g_pw, g_ps, g_qan, None, g_kvan, None, None, g_n3,
             None, None, None, g_fn]
    weights = [ada_w, ada_b, ffn1_norm, ffn1_w_gate, ffn1_w_up, ffn1_w_down, mix_norm, w_in, pool_w, pool_scale,
               q_a_norm, w_q_b, kv_a_norm, w_kv_b, w_out, ffn2_norm, ffn2_w_gate, ffn2_w_up, ffn2_w_down, final_norm]
    ms = [m_ada_w, m_ada_b, m_ffn1_norm, m_ffn1_w_gate, m_ffn1_w_up, m_ffn1_w_down, m_mix_norm, m_w_in, m_pool_w,
          m_pool_scale, m_q_a_norm, m_w_q_b, m_kv_a_norm, m_w_kv_b, m_w_out, m_ffn2_norm, m_ffn2_w_gate, m_ffn2_w_up,
          m_ffn2_w_down, m_final_norm]
    vs = [v_ada_w, v_ada_b, v_ffn1_norm, v_ffn1_w_gate, v_ffn1_w_up, v_ffn1_w_down, v_mix_norm, v_w_in, v_pool_w,
          v_pool_scale, v_q_a_norm, v_w_q_b, v_kv_a_norm, v_w_kv_b, v_w_out, v_ffn2_norm, v_ffn2_w_gate, v_ffn2_w_up,
          v_ffn2_w_down, v_final_norm]
    transposed = (3, 4, 7, 11, 16, 17)
    outs = [None] * len(weights)
    for i, (w, g, m, v) in enumerate(zip(weights, grads, ms, vs)):
        if g is not None:
            outs[i] = adamw(w, g.reshape(w.shape), m, v)
    big = [i for i, g in enumerate(grads) if g is None]

    def update(positions):
        filled = pair_fill_halves([reduced[t] for t in positions])
        for t, g in zip(positions, filled):
            i = big[t]
            if i in transposed:
                outs[i] = tuple(tr(o) for o in adamw(tr(weights[i]), g, tr(ms[i]), tr(vs[i]), copy_g=True))
            else:
                outs[i] = adamw(weights[i], g, ms[i], vs[i], copy_g=True)

    finish(stages[-3], outs[0][1])
    finish(stages[-2], outs[0][1])
    update(rest_pos)
    finish(stages[-1], outs[big[rest_pos[-1]]][1])
    update(ffn1_pos)
    return (loss, dx.reshape(x.shape), *[t[0] for t in outs], *[t[1] for t in outs], *[t[2] for t in outs],
            *[t[3] for t in outs])
```

```python
import math

import jax
import jax.numpy as jnp
from jax import lax
from jax.experimental import pallas as pl
from jax.experimental.pallas import tpu as pltpu

F32 = jnp.float32
BF16 = jnp.bfloat16
MESH = pl.DeviceIdType.MESH

EPS = 1e-6
ROPE_THETA = 10000.0
N_HEADS = 4
QK_NOPE = 128
QK_ROPE = 64
V_HEAD = 128
POOL_WINDOWS = (2, 4, 8, 16)
POOL_GC = 128
POOL_WIDTH = POOL_GC * len(POOL_WINDOWS)
Q_LORA = 384
KV_LORA = 256
SOFTMAX_SCALE = 1.0 / math.sqrt(QK_NOPE + QK_ROPE)
N_CHIPS = 4
N_DEV = 8

ADAM_LR = 0.001
ADAM_B1 = 0.9
ADAM_B2 = 0.999
ADAM_EPS = 1e-08
ADAM_WD = 0.01
ADAM_STEP = 10

ROW_TILE = 512
ATT_TILE = 512
VMEM_LIMIT = 56 * 1024 * 1024
BF16_ROWS = 16
LANES = 128


def _params(sem=None, vmem=VMEM_LIMIT):
    return pltpu.CompilerParams(dimension_semantics=sem, vmem_limit_bytes=vmem)


def _dot(a, b):
    return jnp.dot(a, b, preferred_element_type=F32)


def _dot_nt(a, b):
    return lax.dot_general(a, b, (((1,), (1,)), ((), ())), preferred_element_type=F32)


def _dot_tn(a, b):
    return lax.dot_general(a, b, (((0,), (0,)), ((), ())), preferred_element_type=F32)


def _dot_exact(t, perm):
    t1 = t.astype(BF16)
    r1 = t - t1.astype(F32)
    t2 = r1.astype(BF16)
    t3 = (r1 - t2.astype(F32)).astype(BF16)
    return _dot(t1, perm) + _dot(t2, perm) + _dot(t3, perm)


def _sum0(a):
    return jnp.sum(a, axis=0, keepdims=True)


def _rms(xt):
    r = lax.rsqrt(jnp.mean(xt * xt, axis=-1, keepdims=True) + EPS)
    return xt * r, r


def _rms_bwd(dy, xt, g):
    xhat, r = _rms(xt)
    dxhat = dy * g
    dx = r * (dxhat - xhat * jnp.mean(dxhat * xhat, axis=-1, keepdims=True))
    return dx, _sum0(dy * xhat)


def _normmod_bwd(dh, xt, gn, sc):
    xhat, _ = _rms(xt)
    dn = dh * (1.0 + sc)
    dx, dgn = _rms_bwd(dn, xt, gn)
    return dx, _sum0(dh), _sum0(dh * (xhat * gn)), dgn


def _row_tile(s):
    return min(s, ROW_TILE)


def _full(shape):
    n = len(shape)
    return pl.BlockSpec(shape, lambda *_: (0,) * n)


def _resident(shape):
    n = len(shape)
    return pl.BlockSpec(shape, lambda *_: (0,) * n, pipeline_mode=pl.Buffered(1))


def ffn_fwd(x, gn, sh, sc, gt, wg, wu, wd):
    s, d = x.shape
    k_chunks, fs, _ = wg.shape
    tm = _row_tile(s)

    def body(x_ref, gn_ref, sh_ref, sc_ref, gt_ref, wg_ref, wu_ref, wd_ref,
             xo_ref, h_ref, a_ref, sl_ref, dsu_ref, y_ref):
        xt = x_ref[...]
        xhat, _ = _rms(xt)
        h = (xhat * gn_ref[...] * (1.0 + sc_ref[...]) + sh_ref[...]).astype(BF16)
        h_ref[...] = h
        y = jnp.zeros((tm, d), F32)
        for k in range(k_chunks):
            gate = _dot_nt(h, wg_ref[k])
            up = _dot_nt(h, wu_ref[k])
            sg = jax.nn.sigmoid(gate)
            sl = gate * sg
            a = (sl * up).astype(BF16)
            a_ref[k] = a.T
            sl_ref[k] = sl.astype(BF16)
            dsu_ref[k] = (up * (sg * (1.0 + gate * (1.0 - sg)))).astype(BF16)
            y += _dot(a, wd_ref[k])
        y_ref[...] = y.astype(BF16)
        xo_ref[...] = xt + 0.5 * gt_ref[...] * y

    row = pl.BlockSpec((tm, d), lambda i: (i, 0))
    vec = pl.BlockSpec((1, d), lambda i: (0, 0))
    act = pl.BlockSpec((k_chunks, tm, fs), lambda i: (0, i, 0))
    act_shape = jax.ShapeDtypeStruct((k_chunks, s, fs), BF16)
    return pl.pallas_call(
        body, name="ffn_fwd",
        grid=(s // tm,),
        in_specs=[row, vec, vec, vec, vec, _resident(wg.shape), _resident(wu.shape), _resident(wd.shape)],
        out_specs=[row, row, pl.BlockSpec((k_chunks, fs, tm), lambda i: (0, 0, i)), act, act, row],
        out_shape=[jax.ShapeDtypeStruct((s, d), F32), jax.ShapeDtypeStruct((s, d), BF16),
                   jax.ShapeDtypeStruct((k_chunks, fs, s), BF16), act_shape, act_shape,
                   jax.ShapeDtypeStruct((s, d), BF16)],
        compiler_params=_params(("arbitrary",)),
    )(x, gn, sh, sc, gt, wg, wu, wd)


def ffn_bwd_act(dxn, sl, dsu, gt, wd):
    s, d = dxn.shape
    k_chunks, fs, _ = wd.shape
    tm = _row_tile(s)

    def body(dxn_ref, sl_ref, dsu_ref, gt_ref, wd_ref, dy_ref, dgate_ref, dup_ref):
        dy = (0.5 * gt_ref[...] * dxn_ref[...]).astype(BF16)
        dy_ref[...] = dy
        for k in range(k_chunks):
            da = _dot_nt(dy, wd_ref[k])
            dgate_ref[k] = (da * dsu_ref[k].astype(F32)).astype(BF16)
            dup_ref[k] = (da * sl_ref[k].astype(F32)).astype(BF16)

    row = pl.BlockSpec((tm, d), lambda i: (i, 0))
    act = pl.BlockSpec((k_chunks, tm, fs), lambda i: (0, i, 0))
    act_shape = jax.ShapeDtypeStruct((k_chunks, s, fs), BF16)
    return pl.pallas_call(
        body, name="ffn_bwd_act",
        grid=(s // tm,),
        in_specs=[row, act, act, pl.BlockSpec((1, d), lambda i: (0, 0)), _resident(wd.shape)],
        out_specs=[row, act, act],
        out_shape=[jax.ShapeDtypeStruct((s, d), BF16), act_shape, act_shape],
        compiler_params=_params(("arbitrary",)),
    )(dxn, sl, dsu, gt, wd)


def ffn_bwd_in(dxn, x, y, dgate, dup, gn, sc, wg, wu):
    s, d = x.shape
    k_chunks, fs, _ = wg.shape
    tm = _row_tile(s)

    def body(dxn_ref, x_ref, y_ref, dgate_ref, dup_ref, gn_ref, sc_ref, wg_ref, wu_ref, dx_ref, dvec_ref):
        i = pl.program_id(0)

        @pl.when(i == 0)
        def _():
            dvec_ref[...] = jnp.zeros_like(dvec_ref)

        dh = jnp.zeros((tm, d), F32)
        for k in range(k_chunks):
            dh += _dot(dgate_ref[k], wg_ref[k]) + _dot(dup_ref[k], wu_ref[k])
        dxn_t = dxn_ref[...]
        dx, dsh, dsc, dgn = _normmod_bwd(dh, x_ref[...], gn_ref[...], sc_ref[...])
        dx_ref[...] = dx + dxn_t
        dvec_ref[0:1, :] += dsh
        dvec_ref[1:2, :] += dsc
        dvec_ref[2:3, :] += _sum0(0.5 * dxn_t * y_ref[...].astype(F32))
        dvec_ref[3:4, :] += dgn

    row = pl.BlockSpec((tm, d), lambda i: (i, 0))
    vec = pl.BlockSpec((1, d), lambda i: (0, 0))
    act = pl.BlockSpec((k_chunks, tm, fs), lambda i: (0, i, 0))
    return pl.pallas_call(
        body, name="ffn_bwd_in",
        grid=(s // tm,),
        in_specs=[row, row, row, act, act, vec, vec, _resident(wg.shape), _resident(wu.shape)],
        out_specs=[row, pl.BlockSpec((8, d), lambda i: (0, 0))],
        out_shape=[jax.ShapeDtypeStruct((s, d), F32), jax.ShapeDtypeStruct((8, d), F32)],
        compiler_params=_params(("arbitrary",)),
    )(dxn, x, y, dgate, dup, gn, sc, wg, wu)


def nn_mm(a_t, b):
    g, m, s = a_t.shape
    n = b.shape[1]

    def body(a_ref, b_ref, o_ref):
        o_ref[...] = _dot(a_ref[...], b_ref[...])

    return pl.pallas_call(
        body, name="nn_mm",
        grid=(g,), in_specs=[pl.BlockSpec((None, m, s), lambda gi: (gi, 0, 0)), pl.BlockSpec((s, n), lambda gi: (0, 0))],
        out_specs=pl.BlockSpec((None, m, n), lambda gi: (gi, 0, 0)),
        out_shape=jax.ShapeDtypeStruct((g, m, n), F32),
        compiler_params=_params(("arbitrary",)),
    )(a_t, b)


def tn_mm(a, b):
    ga, s, m = a.shape
    gb, _, n = b.shape
    g = max(ga, gb)

    def body(a_ref, b_ref, o_ref):
        o_ref[...] = _dot_tn(a_ref[...], b_ref[...])

    a_spec = pl.BlockSpec((None, s, m), (lambda gi: (gi, 0, 0)) if ga > 1 else (lambda gi: (0, 0, 0)))
    b_spec = pl.BlockSpec((None, s, n), (lambda gi: (gi, 0, 0)) if gb > 1 else (lambda gi: (0, 0, 0)))
    return pl.pallas_call(
        body, name="tn_mm",
        grid=(g,), in_specs=[a_spec, b_spec], out_specs=pl.BlockSpec((None, m, n), lambda gi: (gi, 0, 0)),
        out_shape=jax.ShapeDtypeStruct((g, m, n), F32),
        compiler_params=_params(("arbitrary",)),
    )(a, b)


def mix_in_fwd(x, gn, sh, sc, w_in_t):
    s, d = x.shape
    tm = _row_tile(s)
    o1, o2, o3 = POOL_WIDTH, POOL_WIDTH + Q_LORA, POOL_WIDTH + Q_LORA + KV_LORA

    def body(x_ref, gn_ref, sh_ref, sc_ref, w_ref, h_ref, u_ref, cq_ref, ckv_ref, kr_ref):
        xhat, _ = _rms(x_ref[...])
        h = (xhat * gn_ref[...] * (1.0 + sc_ref[...]) + sh_ref[...]).astype(BF16)
        h_ref[...] = h
        z = _dot_nt(h, w_ref[0:o3, :])
        u_ref[...] = z[:, 0:o1]
        cq_ref[...] = z[:, o1:o2]
        ckv_ref[...] = z[:, o2:o3]
        kr_ref[...] = _dot_nt(h, w_ref[o3:, :])

    row = lambda w: pl.BlockSpec((tm, w), lambda i: (i, 0))
    vec = pl.BlockSpec((1, d), lambda i: (0, 0))
    return pl.pallas_call(
        body, name="mix_in_fwd",
        grid=(s // tm,),
        in_specs=[row(d), vec, vec, vec, _full(w_in_t.shape)],
        out_specs=[row(d), row(POOL_WIDTH), row(Q_LORA), row(KV_LORA), row(QK_ROPE)],
        out_shape=[jax.ShapeDtypeStruct((s, d), BF16), jax.ShapeDtypeStruct((s, POOL_WIDTH), F32),
                   jax.ShapeDtypeStruct((s, Q_LORA), F32), jax.ShapeDtypeStruct((s, KV_LORA), F32),
                   jax.ShapeDtypeStruct((s, QK_ROPE), F32)],
        compiler_params=_params(("arbitrary",)),
    )(x, gn, sh, sc, w_in_t)


def mix_in_bwd(dxn, du, dcq, dckv, dkr, x, gn, sc, w_in_t):
    s, d = x.shape
    tm = _row_tile(s)
    o1, o2, o3 = POOL_WIDTH, POOL_WIDTH + Q_LORA, POOL_WIDTH + Q_LORA + KV_LORA
    n_z = w_in_t.shape[0]

    def body(dxn_ref, du_ref, dcq_ref, dckv_ref, dkr_ref, x_ref, gn_ref, sc_ref, w_ref, dx_ref, dz_ref, dvec_ref):
        i = pl.program_id(0)

        @pl.when(i == 0)
        def _():
            dvec_ref[...] = jnp.zeros_like(dvec_ref)

        dub = du_ref[...].astype(BF16)
        dqb = dcq_ref[...].astype(BF16)
        dkb = dckv_ref[...].astype(BF16)
        drb = dkr_ref[...].astype(BF16)
        dz_ref[0:o1, :] = dub.T
        dz_ref[o1:o2, :] = dqb.T
        dz_ref[o2:o3, :] = dkb.T
        dz_ref[o3:, :] = drb.T
        dh = (_dot(dub, w_ref[0:o1, :]) + _dot(dqb, w_ref[o1:o2, :]) + _dot(dkb, w_ref[o2:o3, :])
              + _dot(drb, w_ref[o3:, :]))
        dx, dsh, dsc, dgn = _normmod_bwd(dh, x_ref[...], gn_ref[...], sc_ref[...])
        dx_ref[...] = dx + dxn_ref[...]
        dvec_ref[0:1, :] += dsh
        dvec_ref[1:2, :] += dsc
        dvec_ref[3:4, :] += dgn

    row = lambda w: pl.BlockSpec((tm, w), lambda i: (i, 0))
    vec = pl.BlockSpec((1, d), lambda i: (0, 0))
    return pl.pallas_call(
        body, name="mix_in_bwd",
        grid=(s // tm,),
        in_specs=[row(d), row(POOL_WIDTH), row(Q_LORA), row(KV_LORA), row(QK_ROPE), row(d), vec, vec,
                  _full(w_in_t.shape)],
        out_specs=[row(d), pl.BlockSpec((n_z, tm), lambda i: (0, i)), pl.BlockSpec((8, d), lambda i: (0, 0))],
        out_shape=[jax.ShapeDtypeStruct((s, d), F32), jax.ShapeDtypeStruct((n_z, s), BF16),
                   jax.ShapeDtypeStruct((8, d), F32)],
        compiler_params=_params(("arbitrary",)),
    )(dxn, du, dcq, dckv, dkr, x, gn, sc, w_in_t)


def _window_sum(a, w, rows, forward):
    s = a.shape[0]
    step = 1
    while step < w:
        if forward:
            shifted = jnp.where(rows < s - step, pltpu.roll(a, s - step, 0), 0.0)
        else:
            shifted = jnp.where(rows >= step, pltpu.roll(a, step, 0), 0.0)
        a = a + shifted
        step *= 2
    return a


def pool_fwd(u, pool_w, pool_scale):
    s = u.shape[0]

    def body(u_ref, w_ref, sc_ref, y_ref, diff_ref):
        rows = lax.broadcasted_iota(jnp.int32, (s, POOL_GC), 0)
        for g, w in enumerate(POOL_WINDOWS):
            cols = slice(g * POOL_GC, (g + 1) * POOL_GC)
            ug = u_ref[:, cols]
            cnt = jnp.minimum(rows + 1, w).astype(F32)
            diff = (_window_sum(ug, w, rows, False) / cnt - ug).astype(BF16)
            diff_ref[:, cols] = diff
            y_ref[:, cols] = _dot(diff, w_ref[g].astype(BF16)) * sc_ref[:, cols]

    return pl.pallas_call(
        body, name="pool_fwd",
        out_shape=[jax.ShapeDtypeStruct(u.shape, F32), jax.ShapeDtypeStruct(u.shape, BF16)],
        compiler_params=_params(),
    )(u, pool_w, pool_scale)


def pool_bwd(dy, diff, pool_w, pool_scale):
    s = dy.shape[0]

    def body(dy_ref, diff_ref, w_ref, sc_ref, du_ref, dw_ref, dsc_ref):
        rows = lax.broadcasted_iota(jnp.int32, (s, POOL_GC), 0)
        for g, w in enumerate(POOL_WINDOWS):
            cols = slice(g * POOL_GC, (g + 1) * POOL_GC)
            dyg = dy_ref[:, cols]
            diff = diff_ref[:, cols]
            wb = w_ref[g].astype(BF16)
            dsc_ref[:, cols] = _sum0(dyg * _dot(diff, wb))
            dys = (dyg * sc_ref[:, cols]).astype(BF16)
            dw_ref[g] = _dot_tn(diff, dys)
            ddiff = _dot_nt(dys, wb)
            cnt = jnp.minimum(rows + 1, w).astype(F32)
            du_ref[:, cols] = _window_sum(ddiff / cnt, w, rows, True) - ddiff

    return pl.pallas_call(
        body, name="pool_bwd",
        out_shape=[jax.ShapeDtypeStruct(dy.shape, F32), jax.ShapeDtypeStruct(pool_w.shape, F32),
                   jax.ShapeDtypeStruct(pool_scale.shape, F32)],
        compiler_params=_params(),
    )(dy, diff, pool_w, pool_scale)


def mla_qkv_fwd(cq, ckv, kr, qan, kvan, wq, wkv, cos, sin, rot):
    s = cq.shape[0]
    tm = _row_tile(s)

    def body(cq_ref, ckv_ref, kr_ref, qan_ref, kvan_ref, wq_ref, wkv_ref, cos_ref, sin_ref, rot_ref,
             q_ref, k_ref, v_ref, ql_ref, kvl_ref):
        cos_t = cos_ref[...]
        sin_t = sin_ref[...]
        perm = rot_ref[...]

        def rope(t):
            return t * cos_t + _dot_exact(t, perm) * sin_t

        qhat, _ = _rms(cq_ref[...])
        ql = (qhat * qan_ref[...]).astype(BF16)
        ql_ref[...] = ql
        khat, _ = _rms(ckv_ref[...])
        kvl = (khat * kvan_ref[...]).astype(BF16)
        kvl_ref[...] = kvl
        krr = rope(kr_ref[...]).astype(BF16)
        for h in range(N_HEADS):
            q = _dot_nt(ql, wq_ref[h])
            q_ref[h, :, 0:QK_NOPE] = q[:, 0:QK_NOPE].astype(BF16)
            q_ref[h, :, QK_NOPE:] = rope(q[:, QK_NOPE:]).astype(BF16)
            kv = _dot(kvl, wkv_ref[h])
            k_ref[h, :, 0:QK_NOPE] = kv[:, 0:QK_NOPE].astype(BF16)
            k_ref[h, :, QK_NOPE:] = krr
            v_ref[h] = kv[:, QK_NOPE:].astype(BF16)

    row = lambda w: pl.BlockSpec((tm, w), lambda i: (i, 0))
    hrow = lambda w: pl.BlockSpec((N_HEADS, tm, w), lambda i: (0, i, 0))
    qk = QK_NOPE + QK_ROPE
    return pl.pallas_call(
        body, name="mla_qkv_fwd",
        grid=(s // tm,),
        in_specs=[row(Q_LORA), row(KV_LORA), row(QK_ROPE), _full(qan.shape), _full(kvan.shape),
                  _full(wq.shape), _full(wkv.shape), row(QK_ROPE), row(QK_ROPE), _full(rot.shape)],
        out_specs=[hrow(qk), hrow(qk), hrow(V_HEAD), row(Q_LORA), row(KV_LORA)],
        out_shape=[jax.ShapeDtypeStruct((N_HEADS, s, qk), BF16), jax.ShapeDtypeStruct((N_HEADS, s, qk), BF16),
                   jax.ShapeDtypeStruct((N_HEADS, s, V_HEAD), BF16), jax.ShapeDtypeStruct((s, Q_LORA), BF16),
                   jax.ShapeDtypeStruct((s, KV_LORA), BF16)],
        compiler_params=_params(("arbitrary",)),
    )(cq, ckv, kr, qan, kvan, wq, wkv, cos, sin, rot)


def _attn_probs(q_ref, k_ref, qi, tq):
    n = (qi + 1) * tq
    rows = slice(qi * tq, n)
    sc = _dot_nt(q_ref[rows, :], k_ref[0:n, :]) * SOFTMAX_SCALE
    qpos = qi * tq + lax.broadcasted_iota(jnp.int32, (tq, n), 0)
    kpos = lax.broadcasted_iota(jnp.int32, (tq, n), 1)
    sc = jnp.where(qpos >= kpos, sc, -1e30)
    e = jnp.exp(sc - jnp.max(sc, axis=-1, keepdims=True))
    return e * (1.0 / jnp.sum(e, axis=-1, keepdims=True))


def attn_fwd(q, k, v):
    nh, s, qk = q.shape
    tq = min(s, ATT_TILE)

    def body(q_ref, k_ref, v_ref, o_ref):
        for qi in range(s // tq):
            n = (qi + 1) * tq
            p = _attn_probs(q_ref, k_ref, qi, tq).astype(BF16)
            o_ref[qi * tq:n, :] = _dot(p, v_ref[0:n, :])

    head = lambda w: pl.BlockSpec((None, s, w), lambda h: (h, 0, 0))
    return pl.pallas_call(
        body, name="attn_fwd",
        grid=(nh,),
        in_specs=[head(qk), head(qk), head(V_HEAD)],
        out_specs=pl.BlockSpec((s, V_HEAD), lambda h: (0, h)),
        out_shape=jax.ShapeDtypeStruct((s, nh * V_HEAD), F32),
        compiler_params=_params(("arbitrary",)),
    )(q, k, v)


def attn_bwd(q, k, v, do):
    nh, s, qk = q.shape
    tq = min(s, ATT_TILE)

    def body(q_ref, k_ref, v_ref, do_ref, dq_ref, dk_ref, dv_ref):
        dk_ref[...] = jnp.zeros_like(dk_ref)
        dv_ref[...] = jnp.zeros_like(dv_ref)
        for qi in range(s // tq):
            n = (qi + 1) * tq
            rows = slice(qi * tq, n)
            p = _attn_probs(q_ref, k_ref, qi, tq)
            dob = do_ref[rows, :].astype(BF16)
            dp = _dot_nt(dob, v_ref[0:n, :])
            ds = (p * (dp - jnp.sum(p * dp, axis=-1, keepdims=True)) * SOFTMAX_SCALE).astype(BF16)
            dq_ref[rows, :] = _dot(ds, k_ref[0:n, :])
            dk_ref[0:n, :] += _dot_tn(ds, q_ref[rows, :])
            dv_ref[0:n, :] += _dot_tn(p.astype(BF16), dob)

    head = lambda w: pl.BlockSpec((None, s, w), lambda h: (h, 0, 0))
    return pl.pallas_call(
        body, name="attn_bwd",
        grid=(nh,),
        in_specs=[head(qk), head(qk), head(V_HEAD), pl.BlockSpec((s, V_HEAD), lambda h: (0, h))],
        out_specs=[head(qk), head(qk), head(V_HEAD)],
        out_shape=[jax.ShapeDtypeStruct((nh, s, qk), F32), jax.ShapeDtypeStruct((nh, s, qk), F32),
                   jax.ShapeDtypeStruct((nh, s, V_HEAD), F32)],
        compiler_params=_params(("arbitrary",)),
    )(q, k, v, do)


def mla_qkv_bwd(dq, dk, dv, cq, ckv, qan, kvan, wq, wkv, cos, sin, rot_t):
    s = cq.shape[0]
    tm = _row_tile(s)

    def body(dq_ref, dk_ref, dv_ref, cq_ref, ckv_ref, qan_ref, kvan_ref,
             wq_ref, wkv_ref, cos_ref, sin_ref, rot_ref,
             dcq_ref, dckv_ref, dkro_ref, gq_ref, gkv_ref, dqan_ref, dkvan_ref):
        i = pl.program_id(0)

        @pl.when(i == 0)
        def _():
            dqan_ref[...] = jnp.zeros_like(dqan_ref)
            dkvan_ref[...] = jnp.zeros_like(dkvan_ref)

        cos_t = cos_ref[...]
        sin_t = sin_ref[...]
        perm_t = rot_ref[...]

        def unrope(t):
            return t * cos_t + _dot_exact(t * sin_t, perm_t)

        acc_q = jnp.zeros((tm, Q_LORA), F32)
        acc_kv = jnp.zeros((tm, KV_LORA), F32)
        dkr_sum = jnp.zeros((tm, QK_ROPE), F32)
        for h in range(N_HEADS):
            dq_h = dq_ref[h]
            a = dq_h[:, 0:QK_NOPE].astype(BF16)
            b = unrope(dq_h[:, QK_NOPE:]).astype(BF16)
            gq_ref[h, :, 0:QK_NOPE] = a
            gq_ref[h, :, QK_NOPE:] = b
            wq_h = wq_ref[h]
            acc_q += _dot(a, wq_h[0:QK_NOPE, :]) + _dot(b, wq_h[QK_NOPE:, :])
            dk_h = dk_ref[h]
            dk = dk_h[:, 0:QK_NOPE].astype(BF16)
            dvv = dv_ref[h].astype(BF16)
            gkv_ref[h, :, 0:QK_NOPE] = dk
            gkv_ref[h, :, QK_NOPE:] = dvv
            wkv_h = wkv_ref[h]
            acc_kv += _dot_nt(dk, wkv_h[:, 0:QK_NOPE]) + _dot_nt(dvv, wkv_h[:, QK_NOPE:])
            dkr_sum += dk_h[:, QK_NOPE:]
        dkro_ref[...] = unrope(dkr_sum)
        dcq, dqan = _rms_bwd(acc_q, cq_ref[...], qan_ref[...])
        dcq_ref[...] = dcq
        dqan_ref[...] += dqan
        dckv, dkvan = _rms_bwd(acc_kv, ckv_ref[...], kvan_ref[...])
        dckv_ref[...] = dckv
        dkvan_ref[...] += dkvan

    row = lambda w: pl.BlockSpec((tm, w), lambda i: (i, 0))
    hrow = lambda w: pl.BlockSpec((N_HEADS, tm, w), lambda i: (0, i, 0))
    return pl.pallas_call(
        body, name="mla_qkv_bwd",
        grid=(s // tm,),
        in_specs=[hrow(QK_NOPE + QK_ROPE), hrow(QK_NOPE + QK_ROPE), hrow(V_HEAD),
                  row(Q_LORA), row(KV_LORA), _full(qan.shape), _full(kvan.shape),
                  _full(wq.shape), _full(wkv.shape), row(QK_ROPE), row(QK_ROPE), _full(rot_t.shape)],
        out_specs=[row(Q_LORA), row(KV_LORA), row(QK_ROPE), hrow(QK_NOPE + QK_ROPE), hrow(QK_NOPE + V_HEAD),
                   _full(qan.shape), _full(kvan.shape)],
        out_shape=[jax.ShapeDtypeStruct((s, Q_LORA), F32), jax.ShapeDtypeStruct((s, KV_LORA), F32),
                   jax.ShapeDtypeStruct((s, QK_ROPE), F32),
                   jax.ShapeDtypeStruct((N_HEADS, s, QK_NOPE + QK_ROPE), BF16),
                   jax.ShapeDtypeStruct((N_HEADS, s, QK_NOPE + V_HEAD), BF16),
                   jax.ShapeDtypeStruct(qan.shape, F32), jax.ShapeDtypeStruct(kvan.shape, F32)],
        compiler_params=_params(("arbitrary",)),
    )(dq, dk, dv, cq, ckv, qan, kvan, wq, wkv, cos, sin, rot_t)


def out_proj_fwd(yp, om, w_out, x, gt):
    s, d = x.shape
    n_sh, rs, _ = w_out.shape
    tm = _row_tile(s)
    per = POOL_WIDTH // rs

    def body(yp_ref, om_ref, w_ref, x_ref, gt_ref, xo_ref, ycat_ref, y_ref):
        y = jnp.zeros((tm, d), F32)
        for j in range(n_sh):
            src = yp_ref if j < per else om_ref
            part = src[:, (j % per) * rs:(j % per + 1) * rs].astype(BF16)
            ycat_ref[j] = part.T
            y += _dot(part, w_ref[j])
        y_ref[...] = y.astype(BF16)
        xo_ref[...] = x_ref[...] + gt_ref[...] * y

    row = lambda w: pl.BlockSpec((tm, w), lambda i: (i, 0))
    return pl.pallas_call(
        body, name="out_proj_fwd",
        grid=(s // tm,),
        in_specs=[row(POOL_WIDTH), row(POOL_WIDTH), _full(w_out.shape), row(d), pl.BlockSpec((1, d), lambda i: (0, 0))],
        out_specs=[row(d), pl.BlockSpec((n_sh, rs, tm), lambda i: (0, 0, i)), row(d)],
        out_shape=[jax.ShapeDtypeStruct((s, d), F32), jax.ShapeDtypeStruct((n_sh, rs, s), BF16),
                   jax.ShapeDtypeStruct((s, d), BF16)],
        compiler_params=_params(("arbitrary",)),
    )(yp, om, w_out, x, gt)


def out_proj_bwd(dxn, y, gt, w_out):
    s, d = dxn.shape
    n_sh, rs, _ = w_out.shape
    tm = _row_tile(s)
    per = POOL_WIDTH // rs

    def body(dxn_ref, y_ref, gt_ref, w_ref, dy_ref, dyp_ref, dom_ref, dgt_ref):
        i = pl.program_id(0)

        @pl.when(i == 0)
        def _():
            dgt_ref[...] = jnp.zeros_like(dgt_ref)

        dxn_t = dxn_ref[...]
        dy = (gt_ref[...] * dxn_t).astype(BF16)
        dy_ref[...] = dy
        dgt_ref[...] += _sum0(dxn_t * y_ref[...].astype(F32))
        for j in range(n_sh):
            dst = dyp_ref if j < per else dom_ref
            dst[:, (j % per) * rs:(j % per + 1) * rs] = _dot_nt(dy, w_ref[j])

    row = lambda w: pl.BlockSpec((tm, w), lambda i: (i, 0))
    vec = pl.BlockSpec((1, d), lambda i: (0, 0))
    return pl.pallas_call(
        body, name="out_proj_bwd",
        grid=(s // tm,),
        in_specs=[row(d), row(d), vec, _full(w_out.shape)],
        out_specs=[row(d), row(POOL_WIDTH), row(POOL_WIDTH), vec],
        out_shape=[jax.ShapeDtypeStruct((s, d), BF16), jax.ShapeDtypeStruct((s, POOL_WIDTH), F32),
                   jax.ShapeDtypeStruct((s, POOL_WIDTH), F32), jax.ShapeDtypeStruct((1, d), F32)],
        compiler_params=_params(("arbitrary",)),
    )(dxn, y, gt, w_out)


def final_loss(x, gn, tgt):
    s, d = x.shape
    tm = _row_tile(s)

    def body(x_ref, gn_ref, t_ref, loss_ref, dx_ref, dgn_ref):
        i = pl.program_id(0)

        @pl.when(i == 0)
        def _():
            loss_ref[...] = jnp.zeros_like(loss_ref)
            dgn_ref[...] = jnp.zeros_like(dgn_ref)

        xt = x_ref[...]
        g = gn_ref[...]
        xhat, _ = _rms(xt)
        err = xhat * g - t_ref[...]
        per_tok = jnp.mean(err * err, axis=-1, keepdims=True)
        loss_ref[...] += jnp.broadcast_to(0.5 * _sum0(per_tok), loss_ref.shape)
        dx, dgn = _rms_bwd(err * (1.0 / d), xt, g)
        dx_ref[...] = dx
        dgn_ref[...] += dgn

    row = pl.BlockSpec((tm, d), lambda i: (i, 0))
    vec = pl.BlockSpec((1, d), lambda i: (0, 0))
    return pl.pallas_call(
        body, name="final_loss",
        grid=(s // tm,),
        in_specs=[row, vec, row],
        out_specs=[pl.BlockSpec((1, LANES), lambda i: (0, 0)), row, vec],
        out_shape=[jax.ShapeDtypeStruct((1, LANES), F32), jax.ShapeDtypeStruct((s, d), F32),
                   jax.ShapeDtypeStruct((1, d), F32)],
        compiler_params=_params(("arbitrary",)),
    )(x, gn, tgt)


def _col_tile(cols):
    return 768 if cols % 768 == 0 else cols


def ada_fwd(c16, ada_w, ada_b_loc):
    n_layers, d, cols = ada_w.shape
    tn = _col_tile(cols)

    def body(c_ref, w_ref, b_ref, o_ref):
        cv = c_ref[...]
        ca = (cv * jax.nn.sigmoid(cv)).astype(BF16)
        o_ref[...] = _dot(ca, w_ref[...].astype(BF16)) + b_ref[...]

    return pl.pallas_call(
        body, name="ada_fwd",
        grid=(n_layers, cols // tn),
        in_specs=[pl.BlockSpec((16, d), lambda l, j: (0, 0)), pl.BlockSpec((None, d, tn), lambda l, j: (l, 0, j)),
                  pl.BlockSpec((None, 1, tn), lambda l, j: (l, 0, j))],
        out_specs=pl.BlockSpec((None, 16, tn), lambda l, j: (l, 0, j)),
        out_shape=jax.ShapeDtypeStruct((n_layers, 16, cols), F32),
        compiler_params=_params(("arbitrary", "arbitrary")),
    )(c16, ada_w, ada_b_loc)


def ada_bwd(c16, dmod16):
    n_layers, _, cols = dmod16.shape
    d = c16.shape[1]
    tn = _col_tile(cols)

    def body(c_ref, g_ref, o_ref):
        cv = c_ref[...]
        ca = (cv * jax.nn.sigmoid(cv)).astype(BF16)
        o_ref[...] = _dot_tn(ca, g_ref[...].astype(BF16))

    return pl.pallas_call(
        body, name="ada_bwd",
        grid=(n_layers, cols // tn),
        in_specs=[pl.BlockSpec((16, d), lambda l, j: (0, 0)), pl.BlockSpec((None, 16, tn), lambda l, j: (l, 0, j))],
        out_specs=pl.BlockSpec((None, d, tn), lambda l, j: (l, 0, j)),
        out_shape=jax.ShapeDtypeStruct((n_layers, d, cols), F32),
        compiler_params=_params(("arbitrary", "arbitrary")),
    )(c16, dmod16)


def _as_rows(a):
    if a.ndim == 1:
        return a.reshape(1, a.shape[0])
    return a.reshape(-1, a.shape[-1])


def _rows_tile(r, c, itemsize=4, budget=2 * 1024 * 1024):
    if r * c * itemsize <= budget:
        return r
    best = None
    t = BF16_ROWS
    while t < r:
        if r % t == 0 and t * c * itemsize <= budget:
            best = t
        t += BF16_ROWS
    return best if best is not None else r


def cast_place(w, chip, layers, after):
    _, r, c = w.shape
    n_sel = len(layers)
    tr = _rows_tile(r, c, budget=2 * 1024 * 1024 // n_sel)

    def body(chip_ref, *refs):
        for j in range(n_sel):
            refs[n_sel + 1 + j][...] = refs[j][...].astype(BF16)

    layer_spec = lambda l: pl.BlockSpec((None, tr, c), lambda i, ch: (l, i, 0))
    return list(pl.pallas_call(
        body, name="cast_place",
        grid_spec=pltpu.PrefetchScalarGridSpec(
            num_scalar_prefetch=1, grid=(r // tr,),
            in_specs=[layer_spec(l) for l in layers] + [pl.BlockSpec(memory_space=pl.ANY)],
            out_specs=[pl.BlockSpec((None, tr, c), lambda i, ch: (ch[0], i, 0))] * n_sel),
        out_shape=[jax.ShapeDtypeStruct((N_CHIPS, r, c), BF16)] * n_sel,
        compiler_params=_params(("arbitrary",)),
    )(chip, *([w] * n_sel), after))


def adamw(w, g, m, v, copy_g=False):
    shape = w.shape
    w2, g2, m2, v2 = (_as_rows(t) for t in (w, g, m, v))
    r, c = w2.shape
    tr = _rows_tile(r, c, budget=3 * 1024 * 1024)
    c1 = 1.0 - ADAM_B1 ** ADAM_STEP
    c2 = 1.0 - ADAM_B2 ** ADAM_STEP

    def body(w_ref, g_ref, m_ref, v_ref, d_ref, mo_ref, vo_ref, *go_ref):
        gv = g_ref[...]
        if copy_g:
            go_ref[0][...] = gv
        mn = ADAM_B1 * m_ref[...] + (1.0 - ADAM_B1) * gv
        vn = ADAM_B2 * v_ref[...] + (1.0 - ADAM_B2) * (gv * gv)
        mo_ref[...] = mn
        vo_ref[...] = vn
        d_ref[...] = -ADAM_LR * ((mn / c1) / (jnp.sqrt(vn / c2) + ADAM_EPS) + ADAM_WD * w_ref[...])

    spec = pl.BlockSpec((tr, c), lambda i: (i, 0))
    n_out = 4 if copy_g else 3
    outs = pl.pallas_call(
        body, name="adamw", grid=(r // tr,), in_specs=[spec] * 4, out_specs=[spec] * n_out,
        out_shape=[jax.ShapeDtypeStruct((r, c), F32)] * n_out, compiler_params=_params(("arbitrary",)),
    )(w2, g2, m2, v2)
    g_out = outs[3] if copy_g else g2
    return tuple(o.reshape(shape) for o in (g_out,) + tuple(outs[:3]))


def sum_devices(a):
    n, r, c = a.shape
    tr = _rows_tile(r, c, budget=512 * 1024)

    def body(a_ref, o_ref):
        acc = a_ref[0]
        for j in range(1, n):
            acc = acc + a_ref[j]
        o_ref[...] = acc

    return pl.pallas_call(
        body, name="sum_devices", grid=(r // tr,),
        in_specs=[pl.BlockSpec((n, tr, c), lambda i: (0, i, 0))], out_specs=pl.BlockSpec((tr, c), lambda i: (i, 0)),
        out_shape=jax.ShapeDtypeStruct((r, c), F32), compiler_params=_params(("arbitrary",)),
    )(a)


def _split_axis(r, c):
    if (r // 2) % BF16_ROWS == 0 and r % 2 == 0:
        return 0
    assert c % (2 * LANES) == 0, (r, c)
    return 1


def _half_shape(r, c):
    return (r // 2, c) if _split_axis(r, c) == 0 else (r, c // 2)


def _half_at(ref, lead, which):
    r, c = ref.shape[-2:]
    if _split_axis(r, c) == 0:
        return ref.at[(*lead, pl.ds(which * (r // 2), r // 2), slice(None))]
    return ref.at[(*lead, slice(None), pl.ds(which * (c // 2), c // 2))]


def _half_spec(r, c, lead_block, imap):
    hr, hc = _half_shape(r, c)
    if _split_axis(r, c) == 0:
        return pl.BlockSpec((*lead_block, hr, hc), lambda *a: (*imap(*a)[0], imap(*a)[1], 0))
    return pl.BlockSpec((*lead_block, hr, hc), lambda *a: (*imap(*a)[0], 0, imap(*a)[1]))


def pair_add(gs, ras, sel):
    n = len(gs)
    n_sl = gs[0].shape[0]
    halves = [_half_shape(*g.shape[1:]) for g in gs]

    def body(s_ref, *refs):
        g_refs, ra_refs, pb_refs, own_refs = (refs[i * n:(i + 1) * n] for i in range(4))
        k = pl.program_id(0)
        for t in range(n):
            p = g_refs[t][...] + ra_refs[t][...]
            pb_refs[t][...] = p.astype(BF16)

            @pl.when(k == s_ref[1])
            def _(p=p, own=own_refs[t]):
                own[...] = p

    slot = lambda hs: pl.BlockSpec((None,) + hs, lambda k, sr: (k, 0, 0))
    outs = pl.pallas_call(
        body, name="pair_add",
        grid_spec=pltpu.PrefetchScalarGridSpec(
            num_scalar_prefetch=1, grid=(n_sl,),
            in_specs=[_half_spec(*g.shape[1:], (None,), lambda k, sr: ((k,), sr[0])) for g in gs]
            + [slot(hs) for hs in halves],
            out_specs=[slot(hs) for hs in halves] + [pl.BlockSpec(hs, lambda k, sr: (0, 0)) for hs in halves]),
        out_shape=[jax.ShapeDtypeStruct((n_sl,) + hs, BF16) for hs in halves]
        + [jax.ShapeDtypeStruct(hs, F32) for hs in halves],
        compiler_params=_params(("arbitrary",)),
    )(sel, *gs, *ras)
    return list(outs[:n]), list(outs[n:])


def chip_sum(owns, rbs, sel, shapes, accs):
    n = len(owns)
    fresh = accs[0] is None

    def body(s_ref, *refs):
        own_refs, rb_refs, o_refs = refs[:n], refs[n:2 * n], refs[-n:]
        for t in range(n):
            acc_v = own_refs[t][...]
            for j in range(N_CHIPS - 1):
                acc_v = acc_v + rb_refs[t][j].astype(F32)
            o_refs[t][...] = acc_v

    in_specs = ([pl.BlockSpec(o.shape, lambda i, sr: (0, 0)) for o in owns]
                + [pl.BlockSpec(rb.shape, lambda i, sr: (0, 0, 0)) for rb in rbs])
    args = [sel, *owns, *rbs]
    aliases = {}
    if not fresh:
        in_specs += [pl.BlockSpec(memory_space=pl.ANY)] * n
        args += list(accs)
        aliases = {1 + 2 * n + t: t for t in range(n)}
    return list(pl.pallas_call(
        body, name="chip_sum",
        grid_spec=pltpu.PrefetchScalarGridSpec(
            num_scalar_prefetch=1, grid=(1,), in_specs=in_specs,
            out_specs=[_half_spec(*shp[1:], (None,), lambda i, sr: ((sr[2],), sr[0])) for shp in shapes]),
        out_shape=[jax.ShapeDtypeStruct(shp, F32) for shp in shapes],
        input_output_aliases=aliases,
        compiler_params=_params(("arbitrary",)),
    )(*args))


def _me():
    return lax.axis_index("x"), lax.axis_index("y"), lax.axis_index("c")


def _flip(v, bit):
    return 1 - v if bit else v


def exchange8(xs, bcast):
    blk = xs.shape if bcast else xs.shape[1:]

    def body(x_ref, o_ref, send_sems, recv_sems, loc_sem):
        mx, my, mc = _me()
        me = 4 * mx + 2 * my + mc
        src = (lambda j: x_ref) if bcast else (lambda j: x_ref.at[j])
        loc = pltpu.make_async_copy(src(me), o_ref.at[me], loc_sem)
        loc.start()
        copies = []
        for o in range(1, N_DEV):
            px, py, pc = _flip(mx, o & 4), _flip(my, o & 2), _flip(mc, o & 1)
            cp = pltpu.make_async_remote_copy(
                src_ref=src(4 * px + 2 * py + pc), dst_ref=o_ref.at[me],
                send_sem=send_sems.at[o - 1], recv_sem=recv_sems.at[o - 1],
                device_id=(px, py, pc), device_id_type=MESH)
            cp.start()
            copies.append(cp)
        for cp in copies:
            cp.wait()
        loc.wait()

    return pl.pallas_call(
        body, name="exchange8_gather" if bcast else "exchange8_a2a",
        in_specs=[pl.BlockSpec(memory_space=pltpu.VMEM)], out_specs=pl.BlockSpec(memory_space=pltpu.VMEM),
        out_shape=jax.ShapeDtypeStruct((N_DEV,) + tuple(blk), xs.dtype),
        scratch_shapes=[pltpu.SemaphoreType.DMA((N_DEV - 1,)), pltpu.SemaphoreType.DMA((N_DEV - 1,)), pltpu.SemaphoreType.DMA],
        compiler_params=_params(),
    )(xs)


HBM = pl.BlockSpec(memory_space=pltpu.HBM)
SEM = pl.BlockSpec(memory_space=pltpu.SEMAPHORE)
EFFECT = pltpu.SideEffectType.DATAFLOW_SIDE_EFFECTING


def _hbm(a):
    return pltpu.with_memory_space_constraint(a, pltpu.HBM)


def _ici_copy(land, o, send_sem, recv_sem, sending):
    mx, my, mc = _me()
    px, py = _flip(mx, o & 2), _flip(my, o & 1)
    mine = _half_at(land, (2 * mx + my,), mc)
    return pltpu.make_async_remote_copy(
        src_ref=mine, dst_ref=mine if sending else _half_at(land, (2 * px + py,), mc),
        send_sem=send_sem, recv_sem=recv_sem, device_id=(px, py, mc), device_id_type=MESH)


N_PEERS = N_CHIPS - 1
DMA_SEM = pltpu.SemaphoreType.DMA(())


def gather_start(lands, groups, after, tag):
    n_layers, n = len(lands), len(lands[0])
    flat = [a for layer in lands for a in layer]
    n_in = n * n_layers
    n_grp = len(groups)
    n_sem = 2 * n_layers * n_grp * N_PEERS
    first = lambda l, g, recv: ((l * n_grp + g) * 2 + recv) * N_PEERS

    def body(*refs):
        land = refs[:n_in]
        sems = refs[n_in + 1:n_in + 1 + n_sem]
        token = refs[-1]
        for l in range(n_layers):
            for g, members in enumerate(groups):
                for t in members:
                    for o in range(1, N_CHIPS):
                        _ici_copy(land[l * n + t], o, sems[first(l, g, 0) + o - 1], sems[first(l, g, 1) + o - 1],
                                  True).start()
        token[...] = jnp.zeros_like(token)

    outs = pl.pallas_call(
        body, name=f"gather_start_{tag}",
        in_specs=[HBM] * n_in + [pl.BlockSpec(memory_space=pl.ANY)],
        out_specs=[SEM] * n_sem + [HBM] * n_in + [pl.BlockSpec(memory_space=pltpu.VMEM)],
        out_shape=[DMA_SEM] * n_sem + [pltpu.HBM(a.shape, a.dtype) for a in flat]
        + [jax.ShapeDtypeStruct((8, LANES), F32)],
        input_output_aliases={i: i + n_sem for i in range(n_in)},
        compiler_params=pltpu.CompilerParams(has_side_effects=EFFECT),
    )(*[_hbm(a) for a in flat], after)
    sems = [[(list(outs[first(l, g, 0):first(l, g, 0) + N_PEERS]), list(outs[first(l, g, 1):first(l, g, 1) + N_PEERS]))
             for g in range(n_grp)] for l in range(n_layers)]
    lands_thru = [list(outs[n_sem + l * n:n_sem + (l + 1) * n]) for l in range(n_layers)]
    return sems, lands_thru, outs[-1]


def gather_wait(tag, sems, lands, after):
    n = len(lands)
    send_sems, recv_sems = sems

    def body(*refs):
        land = refs[:n]
        send_r = refs[n:n + N_PEERS]
        recv_r = refs[n + N_PEERS:n + 2 * N_PEERS]
        for t in range(n):
            for o in range(1, N_CHIPS):
                _ici_copy(land[t], o, send_r[o - 1], recv_r[o - 1], True).wait_send()
                _ici_copy(land[t], o, send_r[o - 1], recv_r[o - 1], False).wait_recv()

    return list(pl.pallas_call(
        body, name=f"gather_wait_{tag}",
        in_specs=[HBM] * n + [SEM] * (2 * N_PEERS) + [pl.BlockSpec(memory_space=pl.ANY)],
        out_specs=[HBM] * n,
        out_shape=[pltpu.HBM(a.shape, a.dtype) for a in lands],
        input_output_aliases={i: i for i in range(n)},
        compiler_params=pltpu.CompilerParams(has_side_effects=EFFECT),
    )(*lands, *send_sems, *recv_sems, after))


def gather_forward(lands):
    n = len(lands)

    def body(*refs):
        dst = refs[n:2 * n]
        send_sems, recv_sems = refs[2 * n:]
        mx, my, mc = _me()
        fwds = []
        for t in range(n):
            for o in range(1, N_CHIPS):
                slot = 2 * _flip(mx, o & 2) + _flip(my, o & 1)
                mine = _half_at(dst[t], (slot,), mc)
                theirs = _half_at(dst[t], (slot,), 1 - mc)
                cp = pltpu.make_async_remote_copy(
                    src_ref=mine, dst_ref=mine, send_sem=send_sems.at[t, o - 1], recv_sem=recv_sems.at[t, o - 1],
                    device_id=(mx, my, 1 - mc), device_id_type=MESH)
                cp.start()
                fwds.append((cp, pltpu.make_async_remote_copy(
                    src_ref=theirs, dst_ref=theirs, send_sem=send_sems.at[t, o - 1], recv_sem=recv_sems.at[t, o - 1],
                    device_id=(mx, my, 1 - mc), device_id_type=MESH)))
        for cp, arrival in fwds:
            cp.wait_send()
            arrival.wait_recv()

    any_spec = pl.BlockSpec(memory_space=pl.ANY)
    return list(pl.pallas_call(
        body, name="gather_forward",
        in_specs=[any_spec] * n, out_specs=[any_spec] * n,
        out_shape=[jax.ShapeDtypeStruct(a.shape, a.dtype) for a in lands],
        input_output_aliases={t: t for t in range(n)},
        scratch_shapes=[pltpu.SemaphoreType.DMA((n, N_CHIPS - 1)), pltpu.SemaphoreType.DMA((n, N_CHIPS - 1))],
        compiler_params=_params(),
    )(*lands))


def _scatter_copy(src, land, o, send_sem, recv_sem):
    mx, my, mc = _me()
    px, py = _flip(mx, o & 2), _flip(my, o & 1)
    return pltpu.make_async_remote_copy(
        src_ref=src.at[2 * px + py], dst_ref=land.at[o - 1],
        send_sem=send_sem, recv_sem=recv_sem, device_id=(px, py, mc), device_id_type=MESH)


def scatter_start(pbs, tag, after):
    n = len(pbs)
    lands = [lax.empty((N_CHIPS - 1,) + p.shape[1:], p.dtype) for p in pbs]

    def body(*refs):
        src = refs[:n]
        land = refs[n:2 * n]
        send_sems = refs[2 * n + 1:2 * n + 1 + N_PEERS]
        recv_sems = refs[2 * n + 1 + N_PEERS:2 * n + 1 + 2 * N_PEERS]
        token = refs[-1]
        for t in range(n):
            for o in range(1, N_CHIPS):
                _scatter_copy(src[t], land[t], o, send_sems[o - 1], recv_sems[o - 1]).start()
        token[...] = jnp.zeros_like(token)

    n_sem = 2 * N_PEERS
    arrs = list(pbs) + lands
    outs = pl.pallas_call(
        body, name=f"scatter_start_{tag}",
        in_specs=[HBM] * (2 * n) + [pl.BlockSpec(memory_space=pl.ANY)],
        out_specs=[SEM] * n_sem + [HBM] * (2 * n) + [pl.BlockSpec(memory_space=pltpu.VMEM)],
        out_shape=[DMA_SEM] * n_sem + [pltpu.HBM(a.shape, a.dtype) for a in arrs]
        + [jax.ShapeDtypeStruct((8, LANES), F32)],
        input_output_aliases={i: i + n_sem for i in range(2 * n)},
        compiler_params=pltpu.CompilerParams(has_side_effects=EFFECT),
    )(*[_hbm(a) for a in arrs], after)
    return (list(outs[:N_PEERS]), list(outs[N_PEERS:n_sem]), list(outs[n_sem:n_sem + n]),
            list(outs[n_sem + n:n_sem + 2 * n]), outs[-1])


def scatter_wait(tag, send_sems, recv_sems, pbs, lands, after):
    n = len(pbs)

    def body(*refs):
        src = refs[:n]
        land = refs[n:2 * n]
        send_r = refs[2 * n:2 * n + N_PEERS]
        recv_r = refs[2 * n + N_PEERS:2 * n + 2 * N_PEERS]
        for t in range(n):
            for o in range(1, N_CHIPS):
                cp = _scatter_copy(src[t], land[t], o, send_r[o - 1], recv_r[o - 1])
                cp.wait_send()
                cp.wait_recv()

    arrs = list(pbs) + list(lands)
    outs = pl.pallas_call(
        body, name=f"scatter_wait_{tag}",
        in_specs=[HBM] * (2 * n) + [SEM] * (2 * N_PEERS) + [pl.BlockSpec(memory_space=pl.ANY)],
        out_specs=[HBM] * (2 * n),
        out_shape=[pltpu.HBM(a.shape, a.dtype) for a in arrs],
        input_output_aliases={i: i for i in range(2 * n)},
        compiler_params=pltpu.CompilerParams(has_side_effects=EFFECT),
    )(*arrs, *send_sems, *recv_sems, after)
    return list(outs[n:])


def _pair_copy(src, land, send_sem, recv_sem):
    mx, my, mc = _me()
    return pltpu.make_async_remote_copy(
        src_ref=_half_at(src, (slice(None),), 1 - mc), dst_ref=land, send_sem=send_sem, recv_sem=recv_sem,
        device_id=(mx, my, 1 - mc), device_id_type=MESH)


def pair_start(gs, tag, after):
    n = len(gs)
    lands = [lax.empty((g.shape[0],) + _half_shape(*g.shape[1:]), g.dtype) for g in gs]

    def body(*refs):
        src = refs[:n]
        land = refs[n:2 * n]
        send_sem, recv_sem = refs[2 * n + 1], refs[2 * n + 2]
        token = refs[-1]
        for t in range(n):
            _pair_copy(src[t], land[t], send_sem, recv_sem).start()
        token[...] = jnp.zeros_like(token)

    arrs = list(gs) + lands
    outs = pl.pallas_call(
        body, name=f"pair_start_{tag}",
        in_specs=[HBM] * (2 * n) + [pl.BlockSpec(memory_space=pl.ANY)],
        out_specs=[SEM, SEM] + [HBM] * (2 * n) + [pl.BlockSpec(memory_space=pltpu.VMEM)],
        out_shape=[DMA_SEM, DMA_SEM] + [pltpu.HBM(a.shape, a.dtype) for a in arrs] + [jax.ShapeDtypeStruct((8, LANES), F32)],
        input_output_aliases={i: i + 2 for i in range(2 * n)},
        compiler_params=pltpu.CompilerParams(has_side_effects=EFFECT),
    )(*[_hbm(a) for a in arrs], after)
    return outs[0], outs[1], list(outs[2:2 + n]), list(outs[2 + n:2 + 2 * n]), outs[-1]


def pair_wait(tag, send_sem, recv_sem, gs, lands, after):
    n = len(gs)

    def body(*refs):
        src = refs[:n]
        land = refs[n:2 * n]
        send_r, recv_r = refs[2 * n], refs[2 * n + 1]
        for t in range(n):
            cp = _pair_copy(src[t], land[t], send_r, recv_r)
            cp.wait_send()
            cp.wait_recv()

    arrs = list(gs) + list(lands)
    outs = pl.pallas_call(
        body, name=f"pair_wait_{tag}",
        in_specs=[HBM] * (2 * n) + [SEM, SEM, pl.BlockSpec(memory_space=pl.ANY)],
        out_specs=[HBM] * (2 * n),
        out_shape=[pltpu.HBM(a.shape, a.dtype) for a in arrs],
        input_output_aliases={i: i for i in range(2 * n)},
        compiler_params=pltpu.CompilerParams(has_side_effects=EFFECT),
    )(*arrs, send_sem, recv_sem, after)
    return list(outs[:n]), list(outs[n:])


def _gather8_copy(x, land, o, send_sem, recv_sem, sending):
    mx, my, mc = _me()
    px, py, pc = _flip(mx, o & 4), _flip(my, o & 2), _flip(mc, o & 1)
    slot = 4 * mx + 2 * my + mc if sending else 4 * px + 2 * py + pc
    return pltpu.make_async_remote_copy(
        src_ref=x, dst_ref=land.at[slot], send_sem=send_sem, recv_sem=recv_sem,
        device_id=(px, py, pc), device_id_type=MESH)


def gather8_start(x, land, after, tag):
    n_peer = N_DEV - 1

    def body(x_ref, land_ref, after_ref, *rest):
        send_sems, recv_sems = rest[:n_peer], rest[n_peer:2 * n_peer]
        token = rest[-1]
        for o in range(1, N_DEV):
            _gather8_copy(x_ref, land_ref, o, send_sems[o - 1], recv_sems[o - 1], True).start()
        token[...] = jnp.zeros_like(token)

    outs = pl.pallas_call(
        body, name=f"gather8_start_{tag}",
        in_specs=[HBM, HBM, pl.BlockSpec(memory_space=pl.ANY)],
        out_specs=[SEM] * (2 * n_peer) + [HBM, HBM, pl.BlockSpec(memory_space=pltpu.VMEM)],
        out_shape=[DMA_SEM] * (2 * n_peer) + [pltpu.HBM(x.shape, x.dtype), pltpu.HBM(land.shape, land.dtype),
                                              jax.ShapeDtypeStruct((8, LANES), F32)],
        input_output_aliases={0: 2 * n_peer, 1: 2 * n_peer + 1},
        compiler_params=pltpu.CompilerParams(has_side_effects=EFFECT),
    )(_hbm(x), _hbm(land), after)
    return list(outs[:n_peer]), list(outs[n_peer:2 * n_peer]), outs[2 * n_peer], outs[2 * n_peer + 1], outs[-1]


def gather8_wait(tag, send_sems, recv_sems, x, land, after):
    n_peer = N_DEV - 1

    def body(x_ref, land_ref, *rest):
        send_r, recv_r = rest[:n_peer], rest[n_peer:2 * n_peer]
        for o in range(1, N_DEV):
            _gather8_copy(x_ref, land_ref, o, send_r[o - 1], recv_r[o - 1], True).wait_send()
            _gather8_copy(x_ref, land_ref, o, send_r[o - 1], recv_r[o - 1], False).wait_recv()

    return pl.pallas_call(
        body, name=f"gather8_wait_{tag}",
        in_specs=[HBM, HBM] + [SEM] * (2 * n_peer) + [pl.BlockSpec(memory_space=pl.ANY)],
        out_specs=[HBM, HBM],
        out_shape=[pltpu.HBM(x.shape, x.dtype), pltpu.HBM(land.shape, land.dtype)],
        input_output_aliases={0: 0, 1: 1},
        compiler_params=pltpu.CompilerParams(has_side_effects=EFFECT),
    )(x, land, *send_sems, *recv_sems, after)[1]


def pair_fill_halves(fs):
    n = len(fs)

    def body(*refs):
        dst = refs[n:2 * n]
        send_sems, recv_sems = refs[2 * n:]
        mx, my, mc = _me()
        copies = []
        for t in range(n):
            mine = _half_at(dst[t], (slice(None),), mc)
            theirs = _half_at(dst[t], (slice(None),), 1 - mc)
            cp = pltpu.make_async_remote_copy(
                src_ref=mine, dst_ref=mine, send_sem=send_sems.at[t], recv_sem=recv_sems.at[t],
                device_id=(mx, my, 1 - mc), device_id_type=MESH)
            cp.start()
            copies.append((cp, pltpu.make_async_remote_copy(
                src_ref=theirs, dst_ref=theirs, send_sem=send_sems.at[t], recv_sem=recv_sems.at[t],
                device_id=(mx, my, 1 - mc), device_id_type=MESH)))
        for cp, arrival in copies:
            cp.wait_send()
            arrival.wait_recv()

    any_spec = pl.BlockSpec(memory_space=pl.ANY)
    return pl.pallas_call(
        body, name="pair_fill_halves",
        in_specs=[any_spec] * n, out_specs=[any_spec] * n,
        out_shape=[jax.ShapeDtypeStruct(f.shape, f.dtype) for f in fs],
        input_output_aliases={t: t for t in range(n)},
        scratch_shapes=[pltpu.SemaphoreType.DMA((n,)), pltpu.SemaphoreType.DMA((n,))],
        compiler_params=_params(),
    )(*fs)


def _pack_rows(parts, d):
    rows, spans = [], []
    at = 0
    for p in parts:
        flat = p.reshape(-1)
        n_rows = -(-flat.shape[0] // (8 * d)) * 8
        flat = jnp.pad(flat, (0, n_rows * d - flat.shape[0]))
        rows.append(flat.reshape(n_rows, d))
        spans.append((at, p.shape))
        at += n_rows
    return jnp.concatenate(rows, axis=0), spans


def _unpack_rows(packed, spans):
    lead, d = packed.shape[:-2], packed.shape[-1]
    out = []
    for at, shape in spans:
        n = math.prod(shape)
        n_rows = -(-n // d)
        out.append(packed[..., at:at + n_rows, :].reshape(lead + (-1,))[..., :n].reshape(lead + tuple(shape)))
    return out


def _rotate_half_matrix():
    half = QK_ROPE // 2
    idx = jnp.arange(QK_ROPE)
    src = jnp.where(idx < half, idx + half, idx - half)
    sign = jnp.where(idx < half, -1.0, 1.0)
    return (jnp.zeros((QK_ROPE, QK_ROPE), F32).at[src, idx].set(sign)).astype(BF16)


def kernel(x, c, positions, ada_w, ada_b, ffn1_norm, ffn1_w_gate, ffn1_w_up, ffn1_w_down, mix_norm, w_in, pool_w, pool_scale, q_a_norm, w_q_b, kv_a_norm, w_kv_b, w_out, ffn2_norm, ffn2_w_gate, ffn2_w_up, ffn2_w_down, final_norm, loss_target, m_ada_w, m_ada_b, m_ffn1_norm, m_ffn1_w_gate, m_ffn1_w_up, m_ffn1_w_down, m_mix_norm, m_w_in, m_pool_w, m_pool_scale, m_q_a_norm, m_w_q_b, m_kv_a_norm, m_w_kv_b, m_w_out, m_ffn2_norm, m_ffn2_w_gate, m_ffn2_w_up, m_ffn2_w_down, m_final_norm, v_ada_w, v_ada_b, v_ffn1_norm, v_ffn1_w_gate, v_ffn1_w_up, v_ffn1_w_down, v_mix_norm, v_w_in, v_pool_w, v_pool_scale, v_q_a_norm, v_w_q_b, v_kv_a_norm, v_w_kv_b, v_w_out, v_ffn2_norm, v_ffn2_w_gate, v_ffn2_w_up, v_ffn2_w_down, v_final_norm):
    mx, my, mc = _me()
    chip = 2 * mx + my
    half = jnp.reshape(mc, (1,)).astype(jnp.int32)
    chip1 = jnp.reshape(chip, (1,)).astype(jnp.int32)
    n_layers, d, ada_cols = ada_w.shape
    xt = x[0]
    tgt = loss_target[0]

    inv_freq = 1.0 / (ROPE_THETA ** (jnp.arange(0, QK_ROPE, 2, dtype=F32) / QK_ROPE))
    ang = positions[0].astype(F32)[:, None] * inv_freq
    ang = jnp.concatenate([ang, ang], axis=-1)
    cos, sin = jnp.cos(ang), jnp.sin(ang)
    rot = _rotate_half_matrix()
    rot_t = rot.T

    c_all = exchange8(c, True).reshape(N_DEV, d)
    c16 = jnp.pad(c_all, ((0, 8), (0, 0)))
    ada_b_loc = lax.dynamic_slice_in_dim(ada_b, chip * ada_cols, ada_cols, axis=1).reshape(n_layers, 1, ada_cols)
    mod_part = ada_fwd(c16, ada_w, ada_b_loc)[:, :N_DEV]
    mod_got = exchange8(jnp.transpose(mod_part, (1, 0, 2)), False)
    mod = jnp.transpose(mod_got.reshape(N_CHIPS, 2, n_layers, ada_cols)[:, 0], (1, 0, 2))
    mod = mod.reshape(n_layers, 9, 1, d)

    tr = lambda a: jnp.transpose(a, (0, 2, 1))
    local = [tr(ffn1_w_gate), tr(ffn1_w_up), ffn1_w_down, tr(w_in), tr(w_q_b), w_kv_b, w_out,
             tr(ffn2_w_gate), tr(ffn2_w_up), ffn2_w_down]
    ffn1_pos, rest_pos = (0, 1, 2), tuple(range(3, len(local)))
    groups = (ffn1_pos, rest_pos)
    placed = [cast_place(w, chip1, (0,), mod) for w in local]
    g_sems, lands_fly, g_token = gather_start([[p[0] for p in placed]], groups, mod, "first")
    if n_layers > 1:
        later = tuple(range(1, n_layers))
        placed = [cast_place(w, chip1, later, g_token) for w in local]
        more_sems, more_fly, g_token = gather_start(
            [[p[j] for p in placed] for j in range(len(later))], groups, g_token, "rest")
        g_sems, lands_fly = g_sems + more_sems, lands_fly + more_fly
    gathered = []

    row = lambda a, l: a[l].reshape(1, -1)
    saved = []
    for l in range(n_layers):
        g1, u1, d1 = gather_forward(gather_wait(
            f"{l}a", g_sems[l][0], [lands_fly[l][t] for t in ffn1_pos], xt if l else g_token))
        sv = dict(x0=xt)
        xt, sv["h1"], sv["a1"], sv["sl1"], sv["dsu1"], sv["y1"] = ffn_fwd(
            xt, row(ffn1_norm, l), mod[l, 0], mod[l, 1], mod[l, 2], g1, u1, d1)
        sv["x1"] = xt
        win, wq, wkv, wout, g2, u2, d2 = gather_forward(gather_wait(
            f"{l}b", g_sems[l][1], [lands_fly[l][t] for t in rest_pos], xt))
        gathered.append([g1, u1, d1, win, wq, wkv, wout, g2, u2, d2])
        win = win.reshape(-1, d)
        sv["h2"], u, cq, ckv, kr = mix_in_fwd(xt, row(mix_norm, l), mod[l, 3], mod[l, 4], win)
        sv["cq"], sv["ckv"] = cq, ckv
        yp, sv["diff"] = pool_fwd(u, pool_w[l], row(pool_scale, l))
        qh, kh, vh, sv["ql"], sv["kvl"] = mla_qkv_fwd(
            cq, ckv, kr, row(q_a_norm, l), row(kv_a_norm, l), wq, wkv, cos, sin, rot)
        sv["qkv"] = (qh, kh, vh)
        om = attn_fwd(qh, kh, vh)
        xt, sv["ycat"], sv["y2"] = out_proj_fwd(yp, om, wout, xt, mod[l, 5])
        sv["x2"] = xt
        xt, sv["h3"], sv["a3"], sv["sl3"], sv["dsu3"], sv["y3"] = ffn_fwd(
            xt, row(ffn2_norm, l), mod[l, 6], mod[l, 7], mod[l, 8], g2, u2, d2)
        saved.append(sv)

    loss_vec, dx, d_final_norm = final_loss(xt, final_norm.reshape(1, d), tgt)
    loss = lax.psum(loss_vec[0, 0], ("x", "y", "c"))

    none = [None] * n_layers
    dmods, dnorm1, dnorm2, dnorm3 = list(none), list(none), list(none), list(none)
    dpw, dps, dqan_l, dkvan_l = list(none), list(none), list(none), list(none)
    reduced = [None] * len(local)
    stages = []
    sel_of = lambda l: jnp.stack([mc, chip, jnp.asarray(l, mc.dtype)]).astype(jnp.int32)

    def to_chips(job, after_wait, after_start):
        send, recv, g_fly, lands_p = job.pop("pair")
        g_fly, got = pair_wait(job["tag"], send, recv, g_fly, lands_p, after_wait)
        pbs, job["owns"] = pair_add(g_fly, got, sel_of(job["l"]))
        job["scatter"] = scatter_start(pbs, job["tag"], after_start)
        return job["scatter"][4][0, 0]

    def finish(job, after):
        s_send, s_recv, pbs_fly, lands_j, _ = job.pop("scatter")
        parts = scatter_wait(job["tag"], s_send, s_recv, pbs_fly, lands_j, after)
        sums = chip_sum(job["owns"], parts, sel_of(job["l"]), [(n_layers,) + shp for shp in job["shapes"]],
                        [reduced[t] for t in job["pos"]])
        for t, total_t in zip(job["pos"], sums):
            reduced[t] = total_t

    def checkpoint(tag, l, positions, grads_, done, before_scatter=None):
        send, recv, g_fly, lands_p, tok = pair_start(grads_, tag, done)
        order = tok[0, 0]
        if stages:
            order = order + to_chips(stages[-1], done, done if before_scatter is None else before_scatter)
        if len(stages) >= 3:
            finish(stages[-3], done)
        stages.append(dict(tag=tag, l=l, pos=positions, shapes=[g.shape[1:] for g in grads_],
                           pair=(send, recv, g_fly, lands_p)))
        return order

    def small_gather(tag, parts, after):
        packed, spans = _pack_rows(parts, d)
        land = lax.dynamic_update_index_in_dim(lax.empty((N_DEV,) + packed.shape, F32), packed, 4 * mx + 2 * my + mc, 0)
        return gather8_start(packed, land, after, tag), spans

    order = None

    for l in reversed(range(n_layers)):
        sv = saved[l]
        g1, u1, d1, win, wq, wkv, wout, g2, u2, d2 = gathered[l]
        win = win.reshape(-1, d)
        gt3 = mod[l, 8] if order is None else mod[l, 8] + order
        dy, dgt, dup = ffn_bwd_act(dx, sv["sl3"], sv["dsu3"], gt3, d2)
        dx, dvec3 = ffn_bwd_in(dx, sv["x2"], sv["y3"], dgt, dup, row(ffn2_norm, l), mod[l, 7], g2, u2)
        g_g2, g_u2, g_d2 = tn_mm(dgt, sv["h3"][None]), tn_mm(dup, sv["h3"][None]), nn_mm(sv["a3"], dy)
        dy2, dyp, dom, dg2 = out_proj_bwd(dx, sv["y2"], mod[l, 5], wout)
        g_wout = nn_mm(sv["ycat"], dy2)
        qh, kh, vh = sv["qkv"]
        dqh, dkh, dvh = attn_bwd(qh, kh, vh, dom)
        dcq, dckv, dkr_in, gq, gkv, dqan_l[l], dkvan_l[l] = mla_qkv_bwd(
            dqh, dkh, dvh, sv["cq"], sv["ckv"], row(q_a_norm, l), row(kv_a_norm, l), wq, wkv, cos, sin, rot_t)
        g_wq, g_wkv = tn_mm(gq, sv["ql"][None]), tn_mm(sv["kvl"][None], gkv)
        du, dpw[l], dps[l] = pool_bwd(dyp, sv["diff"], pool_w[l], row(pool_scale, l))
        dx, dz, dvec2 = mix_in_bwd(dx, du, dcq, dckv, dkr_in, sv["x1"], row(mix_norm, l), mod[l, 4], win)
        g_win = nn_mm(dz[None], sv["h2"]).reshape(N_CHIPS, -1, d)
        dnorm2[l], dnorm3[l] = dvec2[3], dvec3[3]
        dmod_rest = jnp.concatenate([dvec2[0:2], dg2, dvec3[0:3]], axis=0)
        if l == 0:
            early = small_gather("early", [jnp.stack(dmods[1:]), dmod_rest, jnp.stack(dnorm1[1:]), jnp.stack(dnorm2),
                                           jnp.stack(dnorm3), d_final_norm, jnp.stack(dps), jnp.stack(dqan_l),
                                           jnp.stack(dkvan_l), jnp.stack(dpw)], dx)
        order = checkpoint(f"{l}a", l, rest_pos, [g_win, g_wq, g_wkv, g_wout, g_g2, g_u2, g_d2], dx,
                           early[0][4] if l == 0 else None)
        dy, dgt, dup = ffn_bwd_act(dx, sv["sl1"], sv["dsu1"], mod[l, 2] + order, d1)
        dx, dvec1 = ffn_bwd_in(dx, sv["x0"], sv["y1"], dgt, dup, row(ffn1_norm, l), mod[l, 1], g1, u1)
        g_g1, g_u1, g_d1 = tn_mm(dgt, sv["h1"][None]), tn_mm(dup, sv["h1"][None]), nn_mm(sv["a1"], dy)
        dmods[l] = jnp.concatenate([dvec1[0:3], dmod_rest], axis=0)
        dnorm1[l] = dvec1[3]
        if l == 0:
            late = small_gather("late", [dvec1[0:3], dvec1[3]], dx)
        order = checkpoint(f"{l}b", l, ffn1_pos, [g_g1, g_u1, g_d1], dx, late[0][4] if l == 0 else None)

    to_chips(stages[-1], stages[-2]["scatter"][4], stages[-2]["scatter"][4])
    sent = stages[-1]["scatter"][4]
    got_early = gather8_wait("early", *early[0][:4], sent)
    got_late = gather8_wait("late", *late[0][:4], sent)
    (g_dmod_rest, g_dmod0_rest, g_n1_rest, g_n2, g_n3, g_fn, g_ps, g_qan, g_kvan, g_pw) = _unpack_rows(
        sum_devices(got_early), early[1])
    g_dmod0_first, g_n1_first = _unpack_rows(sum_devices(got_late), late[1])
    g_ada_b = jnp.concatenate([jnp.concatenate([g_dmod0_first, g_dmod0_rest], axis=0)[None], g_dmod_rest], axis=0)
    g_n1 = jnp.concatenate([g_n1_first[None], g_n1_rest], axis=0)
    each_rest, each0_rest = _unpack_rows(got_early, early[1])[:2]
    each0_first = _unpack_rows(got_late, late[1])[0]
    dmod_all = jnp.concatenate([jnp.concatenate([each0_first, each0_rest], axis=1)[:, None], each_rest], axis=1)
    dmod_all = dmod_all.reshape(N_DEV, n_layers, 9 * d)
    dmod_loc = lax.dynamic_slice_in_dim(dmod_all, chip * ada_cols, ada_cols, axis=2)
    dmod16 = jnp.pad(jnp.transpose(dmod_loc, (1, 0, 2)), ((0, 0), (0, 8), (0, 0)))
    g_ada_w = ada_bwd(c16, dmod16)

    grads = [g_ada_w, g_ada_b, g_n1, None, None, None, g_n2, None, g_pw, g_ps, g_qan, None, g_kvan, None, None, g_n3,
             None, None, None, g_fn]
    weights = [ada_w, ada_b, ffn1_norm, ffn1_w_gate, ffn1_w_up, ffn1_w_down, mix_norm, w_in, pool_w, pool_scale,
               q_a_norm, w_q_b, kv_a_norm, w_kv_b, w_out, ffn2_norm, ffn2_w_gate, ffn2_w_up, ffn2_w_down, final_norm]
    ms = [m_ada_w, m_ada_b, m_ffn1_norm, m_ffn1_w_gate, m_ffn1_w_up, m_ffn1_w_down, m_mix_norm, m_w_in, m_pool_w,
          m_pool_scale, m_q_a_norm, m_w_q_b, m_kv_a_norm, m_w_kv_b, m_w_out, m_ffn2_norm, m_ffn2_w_gate, m_ffn2_w_up,
          m_ffn2_w_down, m_final_norm]
    vs = [v_ada_w, v_ada_b, v_ffn1_norm, v_ffn1_w_gate, v_ffn1_w_up, v_ffn1_w_down, v_mix_norm, v_w_in, v_pool_w,
          v_pool_scale, v_q_a_norm, v_w_q_b, v_kv_a_norm, v_w_kv_b, v_w_out, v_ffn2_norm, v_ffn2_w_gate, v_ffn2_w_up,
          v_ffn2_w_down, v_final_norm]
    transposed = (3, 4, 7, 11, 16, 17)
    outs = [None] * len(weights)
    for i, (w, g, m, v) in enumerate(zip(weights, grads, ms, vs)):
        if g is not None:
            outs[i] = adamw(w, g.reshape(w.shape), m, v)
    big = [i for i, g in enumerate(grads) if g is None]

    def update(positions):
        filled = pair_fill_halves([reduced[t] for t in positions])
        for t, g in zip(positions, filled):
            i = big[t]
            if i in transposed:
                outs[i] = tuple(tr(o) for o in adamw(tr(weights[i]), g, tr(ms[i]), tr(vs[i]), copy_g=True))
            else:
                outs[i] = adamw(weights[i], g, ms[i], vs[i], copy_g=True)

    finish(stages[-3], outs[0][1])
    finish(stages[-2], outs[0][1])
    update(rest_pos)
    finish(stages[-1], outs[big[rest_pos[-1]]][1])
    update(ffn1_pos)
    return (loss, dx.reshape(x.shape), *[t[0] for t in outs], *[t[1] for t in outs], *[t[2] for t in outs],
            *[t[3] for t in outs])
```

```python
import math

import jax
import jax.numpy as jnp
from jax import lax
from jax.experimental import pallas as pl
from jax.experimental.pallas import tpu as pltpu

F32 = jnp.float32
BF16 = jnp.bfloat16
MESH = pl.DeviceIdType.MESH

EPS = 1e-6
ROPE_THETA = 10000.0
N_HEADS = 4
QK_NOPE = 128
QK_ROPE = 64
V_HEAD = 128
POOL_WINDOWS = (2, 4, 8, 16)
POOL_GC = 128
POOL_WIDTH = POOL_GC * len(POOL_WINDOWS)
Q_LORA = 384
KV_LORA = 256
SOFTMAX_SCALE = 1.0 / math.sqrt(QK_NOPE + QK_ROPE)
N_CHIPS = 4
N_DEV = 8

ADAM_LR = 0.001
ADAM_B1 = 0.9
ADAM_B2 = 0.999
ADAM_EPS = 1e-08
ADAM_WD = 0.01
ADAM_STEP = 10

ROW_TILE = 512
ATT_TILE = 512
VMEM_LIMIT = 56 * 1024 * 1024
BF16_ROWS = 16
LANES = 128


def _params(sem=None, vmem=VMEM_LIMIT):
    return pltpu.CompilerParams(dimension_semantics=sem, vmem_limit_bytes=vmem)


def _dot(a, b):
    return jnp.dot(a, b, preferred_element_type=F32)


def _dot_nt(a, b):
    return lax.dot_general(a, b, (((1,), (1,)), ((), ())), preferred_element_type=F32)


def _dot_tn(a, b):
    return lax.dot_general(a, b, (((0,), (0,)), ((), ())), preferred_element_type=F32)


def _dot_exact(t, perm):
    t1 = t.astype(BF16)
    r1 = t - t1.astype(F32)
    t2 = r1.astype(BF16)
    t3 = (r1 - t2.astype(F32)).astype(BF16)
    return _dot(t1, perm) + _dot(t2, perm) + _dot(t3, perm)


def _sum0(a):
    return jnp.sum(a, axis=0, keepdims=True)


def _rms(xt):
    r = lax.rsqrt(jnp.mean(xt * xt, axis=-1, keepdims=True) + EPS)
    return xt * r, r


def _rms_bwd(dy, xt, g):
    xhat, r = _rms(xt)
    dxhat = dy * g
    dx = r * (dxhat - xhat * jnp.mean(dxhat * xhat, axis=-1, keepdims=True))
    return dx, _sum0(dy * xhat)


def _normmod_bwd(dh, xt, gn, sc):
    xhat, _ = _rms(xt)
    dn = dh * (1.0 + sc)
    dx, dgn = _rms_bwd(dn, xt, gn)
    return dx, _sum0(dh), _sum0(dh * (xhat * gn)), dgn


def _row_tile(s):
    return min(s, ROW_TILE)


def _full(shape):
    n = len(shape)
    return pl.BlockSpec(shape, lambda *_: (0,) * n)


def _resident(shape):
    n = len(shape)
    return pl.BlockSpec(shape, lambda *_: (0,) * n, pipeline_mode=pl.Buffered(1))


def ffn_fwd(x, gn, sh, sc, gt, wg, wu, wd):
    s, d = x.shape
    k_chunks, fs, _ = wg.shape
    tm = _row_tile(s)

    def body(x_ref, gn_ref, sh_ref, sc_ref, gt_ref, wg_ref, wu_ref, wd_ref,
             xo_ref, h_ref, a_ref, sl_ref, dsu_ref, y_ref):
        xt = x_ref[...]
        xhat, _ = _rms(xt)
        h = (xhat * gn_ref[...] * (1.0 + sc_ref[...]) + sh_ref[...]).astype(BF16)
        h_ref[...] = h
        y = jnp.zeros((tm, d), F32)
        for k in range(k_chunks):
            gate = _dot_nt(h, wg_ref[k])
            up = _dot_nt(h, wu_ref[k])
            sg = jax.nn.sigmoid(gate)
            sl = gate * sg
            a = (sl * up).astype(BF16)
            a_ref[k] = a.T
            sl_ref[k] = sl.astype(BF16)
            dsu_ref[k] = (up * (sg * (1.0 + gate * (1.0 - sg)))).astype(BF16)
            y += _dot(a, wd_ref[k])
        y_ref[...] = y.astype(BF16)
        xo_ref[...] = xt + 0.5 * gt_ref[...] * y

    row = pl.BlockSpec((tm, d), lambda i: (i, 0))
    vec = pl.BlockSpec((1, d), lambda i: (0, 0))
    act = pl.BlockSpec((k_chunks, tm, fs), lambda i: (0, i, 0))
    act_shape = jax.ShapeDtypeStruct((k_chunks, s, fs), BF16)
    return pl.pallas_call(
        body, name="ffn_fwd",
        grid=(s // tm,),
        in_specs=[row, vec, vec, vec, vec, _resident(wg.shape), _resident(wu.shape), _resident(wd.shape)],
        out_specs=[row, row, pl.BlockSpec((k_chunks, fs, tm), lambda i: (0, 0, i)), act, act, row],
        out_shape=[jax.ShapeDtypeStruct((s, d), F32), jax.ShapeDtypeStruct((s, d), BF16),
                   jax.ShapeDtypeStruct((k_chunks, fs, s), BF16), act_shape, act_shape,
                   jax.ShapeDtypeStruct((s, d), BF16)],
        compiler_params=_params(("arbitrary",)),
    )(x, gn, sh, sc, gt, wg, wu, wd)


def ffn_bwd_act(dxn, sl, dsu, gt, wd):
    s, d = dxn.shape
    k_chunks, fs, _ = wd.shape
    tm = _row_tile(s)

    def body(dxn_ref, sl_ref, dsu_ref, gt_ref, wd_ref, dy_ref, dgate_ref, dup_ref):
        dy = (0.5 * gt_ref[...] * dxn_ref[...]).astype(BF16)
        dy_ref[...] = dy
        for k in range(k_chunks):
            da = _dot_nt(dy, wd_ref[k])
            dgate_ref[k] = (da * dsu_ref[k].astype(F32)).astype(BF16)
            dup_ref[k] = (da * sl_ref[k].astype(F32)).astype(BF16)

    row = pl.BlockSpec((tm, d), lambda i: (i, 0))
    act = pl.BlockSpec((k_chunks, tm, fs), lambda i: (0, i, 0))
    act_shape = jax.ShapeDtypeStruct((k_chunks, s, fs), BF16)
    return pl.pallas_call(
        body, name="ffn_bwd_act",
        grid=(s // tm,),
        in_specs=[row, act, act, pl.BlockSpec((1, d), lambda i: (0, 0)), _resident(wd.shape)],
        out_specs=[row, act, act],
        out_shape=[jax.ShapeDtypeStruct((s, d), BF16), act_shape, act_shape],
        compiler_params=_params(("arbitrary",)),
    )(dxn, sl, dsu, gt, wd)


def ffn_bwd_in(dxn, x, y, dgate, dup, gn, sc, wg, wu):
    s, d = x.shape
    k_chunks, fs, _ = wg.shape
    tm = _row_tile(s)

    def body(dxn_ref, x_ref, y_ref, dgate_ref, dup_ref, gn_ref, sc_ref, wg_ref, wu_ref, dx_ref, dvec_ref):
        i = pl.program_id(0)

        @pl.when(i == 0)
        def _():
            dvec_ref[...] = jnp.zeros_like(dvec_ref)

        dh = jnp.zeros((tm, d), F32)
        for k in range(k_chunks):
            dh += _dot(dgate_ref[k], wg_ref[k]) + _dot(dup_ref[k], wu_ref[k])
        dxn_t = dxn_ref[...]
        dx, dsh, dsc, dgn = _normmod_bwd(dh, x_ref[...], gn_ref[...], sc_ref[...])
        dx_ref[...] = dx + dxn_t
        dvec_ref[0:1, :] += dsh
        dvec_ref[1:2, :] += dsc
        dvec_ref[2:3, :] += _sum0(0.5 * dxn_t * y_ref[...].astype(F32))
        dvec_ref[3:4, :] += dgn

    row = pl.BlockSpec((tm, d), lambda i: (i, 0))
    vec = pl.BlockSpec((1, d), lambda i: (0, 0))
    act = pl.BlockSpec((k_chunks, tm, fs), lambda i: (0, i, 0))
    return pl.pallas_call(
        body, name="ffn_bwd_in",
        grid=(s // tm,),
        in_specs=[row, row, row, act, act, vec, vec, _resident(wg.shape), _resident(wu.shape)],
        out_specs=[row, pl.BlockSpec((8, d), lambda i: (0, 0))],
        out_shape=[jax.ShapeDtypeStruct((s, d), F32), jax.ShapeDtypeStruct((8, d), F32)],
        compiler_params=_params(("arbitrary",)),
    )(dxn, x, y, dgate, dup, gn, sc, wg, wu)


def nn_mm(a_t, b):
    g, m, s = a_t.shape
    n = b.shape[1]

    def body(a_ref, b_ref, o_ref):
        o_ref[...] = _dot(a_ref[...], b_ref[...])

    return pl.pallas_call(
        body, name="nn_mm",
        grid=(g,), in_specs=[pl.BlockSpec((None, m, s), lambda gi: (gi, 0, 0)), pl.BlockSpec((s, n), lambda gi: (0, 0))],
        out_specs=pl.BlockSpec((None, m, n), lambda gi: (gi, 0, 0)),
        out_shape=jax.ShapeDtypeStruct((g, m, n), F32),
        compiler_params=_params(("arbitrary",)),
    )(a_t, b)


def tn_mm(a, b):
    ga, s, m = a.shape
    gb, _, n = b.shape
    g = max(ga, gb)

    def body(a_ref, b_ref, o_ref):
        o_ref[...] = _dot_tn(a_ref[...], b_ref[...])

    a_spec = pl.BlockSpec((None, s, m), (lambda gi: (gi, 0, 0)) if ga > 1 else (lambda gi: (0, 0, 0)))
    b_spec = pl.BlockSpec((None, s, n), (lambda gi: (gi, 0, 0)) if gb > 1 else (lambda gi: (0, 0, 0)))
    return pl.pallas_call(
        body, name="tn_mm",
        grid=(g,), in_specs=[a_spec, b_spec], out_specs=pl.BlockSpec((None, m, n), lambda gi: (gi, 0, 0)),
        out_shape=jax.ShapeDtypeStruct((g, m, n), F32),
        compiler_params=_params(("arbitrary",)),
    )(a, b)


def mix_in_fwd(x, gn, sh, sc, w_in_t):
    s, d = x.shape
    tm = _row_tile(s)
    o1, o2, o3 = POOL_WIDTH, POOL_WIDTH + Q_LORA, POOL_WIDTH + Q_LORA + KV_LORA

    def body(x_ref, gn_ref, sh_ref, sc_ref, w_ref, h_ref, u_ref, cq_ref, ckv_ref, kr_ref):
        xhat, _ = _rms(x_ref[...])
        h = (xhat * gn_ref[...] * (1.0 + sc_ref[...]) + sh_ref[...]).astype(BF16)
        h_ref[...] = h
        z = _dot_nt(h, w_ref[0:o3, :])
        u_ref[...] = z[:, 0:o1]
        cq_ref[...] = z[:, o1:o2]
        ckv_ref[...] = z[:, o2:o3]
        kr_ref[...] = _dot_nt(h, w_ref[o3:, :])

    row = lambda w: pl.BlockSpec((tm, w), lambda i: (i, 0))
    vec = pl.BlockSpec((1, d), lambda i: (0, 0))
    return pl.pallas_call(
        body, name="mix_in_fwd",
        grid=(s // tm,),
        in_specs=[row(d), vec, vec, vec, _full(w_in_t.shape)],
        out_specs=[row(d), row(POOL_WIDTH), row(Q_LORA), row(KV_LORA), row(QK_ROPE)],
        out_shape=[jax.ShapeDtypeStruct((s, d), BF16), jax.ShapeDtypeStruct((s, POOL_WIDTH), F32),
                   jax.ShapeDtypeStruct((s, Q_LORA), F32), jax.ShapeDtypeStruct((s, KV_LORA), F32),
                   jax.ShapeDtypeStruct((s, QK_ROPE), F32)],
        compiler_params=_params(("arbitrary",)),
    )(x, gn, sh, sc, w_in_t)


def mix_in_bwd(dxn, du, dcq, dckv, dkr, x, gn, sc, w_in_t):
    s, d = x.shape
    tm = _row_tile(s)
    o1, o2, o3 = POOL_WIDTH, POOL_WIDTH + Q_LORA, POOL_WIDTH + Q_LORA + KV_LORA
    n_z = w_in_t.shape[0]

    def body(dxn_ref, du_ref, dcq_ref, dckv_ref, dkr_ref, x_ref, gn_ref, sc_ref, w_ref, dx_ref, dz_ref, dvec_ref):
        i = pl.program_id(0)

        @pl.when(i == 0)
        def _():
            dvec_ref[...] = jnp.zeros_like(dvec_ref)

        dub = du_ref[...].astype(BF16)
        dqb = dcq_ref[...].astype(BF16)
        dkb = dckv_ref[...].astype(BF16)
        drb = dkr_ref[...].astype(BF16)
        dz_ref[0:o1, :] = dub.T
        dz_ref[o1:o2, :] = dqb.T
        dz_ref[o2:o3, :] = dkb.T
        dz_ref[o3:, :] = drb.T
        dh = (_dot(dub, w_ref[0:o1, :]) + _dot(dqb, w_ref[o1:o2, :]) + _dot(dkb, w_ref[o2:o3, :])
              + _dot(drb, w_ref[o3:, :]))
        dx, dsh, dsc, dgn = _normmod_bwd(dh, x_ref[...], gn_ref[...], sc_ref[...])
        dx_ref[...] = dx + dxn_ref[...]
        dvec_ref[0:1, :] += dsh
        dvec_ref[1:2, :] += dsc
        dvec_ref[3:4, :] += dgn

    row = lambda w: pl.BlockSpec((tm, w), lambda i: (i, 0))
    vec = pl.BlockSpec((1, d), lambda i: (0, 0))
    return pl.pallas_call(
        body, name="mix_in_bwd",
        grid=(s // tm,),
        in_specs=[row(d), row(POOL_WIDTH), row(Q_LORA), row(KV_LORA), row(QK_ROPE), row(d), vec, vec,
                  _full(w_in_t.shape)],
        out_specs=[row(d), pl.BlockSpec((n_z, tm), lambda i: (0, i)), pl.BlockSpec((8, d), lambda i: (0, 0))],
        out_shape=[jax.ShapeDtypeStruct((s, d), F32), jax.ShapeDtypeStruct((n_z, s), BF16),
                   jax.ShapeDtypeStruct((8, d), F32)],
        compiler_params=_params(("arbitrary",)),
    )(dxn, du, dcq, dckv, dkr, x, gn, sc, w_in_t)


def _window_sum(a, w, rows, forward):
    s = a.shape[0]
    step = 1
    while step < w:
        if forward:
            shifted = jnp.where(rows < s - step, pltpu.roll(a, s - step, 0), 0.0)
        else:
            shifted = jnp.where(rows >= step, pltpu.roll(a, step, 0), 0.0)
        a = a + shifted
        step *= 2
    return a


def pool_fwd(u, pool_w, pool_scale):
    s = u.shape[0]

    def body(u_ref, w_ref, sc_ref, y_ref, diff_ref):
        rows = lax.broadcasted_iota(jnp.int32, (s, POOL_GC), 0)
        for g, w in enumerate(POOL_WINDOWS):
            cols = slice(g * POOL_GC, (g + 1) * POOL_GC)
            ug = u_ref[:, cols]
            cnt = jnp.minimum(rows + 1, w).astype(F32)
            diff = (_window_sum(ug, w, rows, False) / cnt - ug).astype(BF16)
            diff_ref[:, cols] = diff
            y_ref[:, cols] = _dot(diff, w_ref[g].astype(BF16)) * sc_ref[:, cols]

    return pl.pallas_call(
        body, name="pool_fwd",
        out_shape=[jax.ShapeDtypeStruct(u.shape, F32), jax.ShapeDtypeStruct(u.shape, BF16)],
        compiler_params=_params(),
    )(u, pool_w, pool_scale)


def pool_bwd(dy, diff, pool_w, pool_scale):
    s = dy.shape[0]

    def body(dy_ref, diff_ref, w_ref, sc_ref, du_ref, dw_ref, dsc_ref):
        rows = lax.broadcasted_iota(jnp.int32, (s, POOL_GC), 0)
        for g, w in enumerate(POOL_WINDOWS):
            cols = slice(g * POOL_GC, (g + 1) * POOL_GC)
            dyg = dy_ref[:, cols]
            diff = diff_ref[:, cols]
            wb = w_ref[g].astype(BF16)
            dsc_ref[:, cols] = _sum0(dyg * _dot(diff, wb))
            dys = (dyg * sc_ref[:, cols]).astype(BF16)
            dw_ref[g] = _dot_tn(diff, dys)
            ddiff = _dot_nt(dys, wb)
            cnt = jnp.minimum(rows + 1, w).astype(F32)
            du_ref[:, cols] = _window_sum(ddiff / cnt, w, rows, True) - ddiff

    return pl.pallas_call(
        body, name="pool_bwd",
        out_shape=[jax.ShapeDtypeStruct(dy.shape, F32), jax.ShapeDtypeStruct(pool_w.shape, F32),
                   jax.ShapeDtypeStruct(pool_scale.shape, F32)],
        compiler_params=_params(),
    )(dy, diff, pool_w, pool_scale)


def mla_qkv_fwd(cq, ckv, kr, qan, kvan, wq, wkv, cos, sin, rot):
    s = cq.shape[0]
    tm = _row_tile(s)

    def body(cq_ref, ckv_ref, kr_ref, qan_ref, kvan_ref, wq_ref, wkv_ref, cos_ref, sin_ref, rot_ref,
             q_ref, k_ref, v_ref, ql_ref, kvl_ref):
        cos_t = cos_ref[...]
        sin_t = sin_ref[...]
        perm = rot_ref[...]

        def rope(t):
            return t * cos_t + _dot_exact(t, perm) * sin_t

        qhat, _ = _rms(cq_ref[...])
        ql = (qhat * qan_ref[...]).astype(BF16)
        ql_ref[...] = ql
        khat, _ = _rms(ckv_ref[...])
        kvl = (khat * kvan_ref[...]).astype(BF16)
        kvl_ref[...] = kvl
        krr = rope(kr_ref[...]).astype(BF16)
        for h in range(N_HEADS):
            q = _dot_nt(ql, wq_ref[h])
            q_ref[h, :, 0:QK_NOPE] = q[:, 0:QK_NOPE].astype(BF16)
            q_ref[h, :, QK_NOPE:] = rope(q[:, QK_NOPE:]).astype(BF16)
            kv = _dot(kvl, wkv_ref[h])
            k_ref[h, :, 0:QK_NOPE] = kv[:, 0:QK_NOPE].astype(BF16)
            k_ref[h, :, QK_NOPE:] = krr
            v_ref[h] = kv[:, QK_NOPE:].astype(BF16)

    row = lambda w: pl.BlockSpec((tm, w), lambda i: (i, 0))
    hrow = lambda w: pl.BlockSpec((N_HEADS, tm, w), lambda i: (0, i, 0))
    qk = QK_NOPE + QK_ROPE
    return pl.pallas_call(
        body, name="mla_qkv_fwd",
        grid=(s // tm,),
        in_specs=[row(Q_LORA), row(KV_LORA), row(QK_ROPE), _full(qan.shape), _full(kvan.shape),
                  _full(wq.shape), _full(wkv.shape), row(QK_ROPE), row(QK_ROPE), _full(rot.shape)],
        out_specs=[hrow(qk), hrow(qk), hrow(V_HEAD), row(Q_LORA), row(KV_LORA)],
        out_shape=[jax.ShapeDtypeStruct((N_HEADS, s, qk), BF16), jax.ShapeDtypeStruct((N_HEADS, s, qk), BF16),
                   jax.ShapeDtypeStruct((N_HEADS, s, V_HEAD), BF16), jax.ShapeDtypeStruct((s, Q_LORA), BF16),
                   jax.ShapeDtypeStruct((s, KV_LORA), BF16)],
        compiler_params=_params(("arbitrary",)),
    )(cq, ckv, kr, qan, kvan, wq, wkv, cos, sin, rot)


def _attn_probs(q_ref, k_ref, qi, tq):
    n = (qi + 1) * tq
    rows = slice(qi * tq, n)
    sc = _dot_nt(q_ref[rows, :], k_ref[0:n, :]) * SOFTMAX_SCALE
    qpos = qi * tq + lax.broadcasted_iota(jnp.int32, (tq, n), 0)
    kpos = lax.broadcasted_iota(jnp.int32, (tq, n), 1)
    sc = jnp.where(qpos >= kpos, sc, -1e30)
    e = jnp.exp(sc - jnp.max(sc, axis=-1, keepdims=True))
    return e * (1.0 / jnp.sum(e, axis=-1, keepdims=True))


def attn_fwd(q, k, v):
    nh, s, qk = q.shape
    tq = min(s, ATT_TILE)

    def body(q_ref, k_ref, v_ref, o_ref):
        for qi in range(s // tq):
            n = (qi + 1) * tq
            p = _attn_probs(q_ref, k_ref, qi, tq).astype(BF16)
            o_ref[qi * tq:n, :] = _dot(p, v_ref[0:n, :])

    head = lambda w: pl.BlockSpec((None, s, w), lambda h: (h, 0, 0))
    return pl.pallas_call(
        body, name="attn_fwd",
        grid=(nh,),
        in_specs=[head(qk), head(qk), head(V_HEAD)],
        out_specs=pl.BlockSpec((s, V_HEAD), lambda h: (0, h)),
        out_shape=jax.ShapeDtypeStruct((s, nh * V_HEAD), F32),
        compiler_params=_params(("arbitrary",)),
    )(q, k, v)


def attn_bwd(q, k, v, do):
    nh, s, qk = q.shape
    tq = min(s, ATT_TILE)

    def body(q_ref, k_ref, v_ref, do_ref, dq_ref, dk_ref, dv_ref):
        dk_ref[...] = jnp.zeros_like(dk_ref)
        dv_ref[...] = jnp.zeros_like(dv_ref)
        for qi in range(s // tq):
            n = (qi + 1) * tq
            rows = slice(qi * tq, n)
            p = _attn_probs(q_ref, k_ref, qi, tq)
            dob = do_ref[rows, :].astype(BF16)
            dp = _dot_nt(dob, v_ref[0:n, :])
            ds = (p * (dp - jnp.sum(p * dp, axis=-1, keepdims=True)) * SOFTMAX_SCALE).astype(BF16)
            dq_ref[rows, :] = _dot(ds, k_ref[0:n, :])
            dk_ref[0:n, :] += _dot_tn(ds, q_ref[rows, :])
            dv_ref[0:n, :] += _dot_tn(p.astype(BF16), dob)

    head = lambda w: pl.BlockSpec((None, s, w), lambda h: (h, 0, 0))
    return pl.pallas_call(
        body, name="attn_bwd",
        grid=(nh,),
        in_specs=[head(qk), head(qk), head(V_HEAD), pl.BlockSpec((s, V_HEAD), lambda h: (0, h))],
        out_specs=[head(qk), head(qk), head(V_HEAD)],
        out_shape=[jax.ShapeDtypeStruct((nh, s, qk), F32), jax.ShapeDtypeStruct((nh, s, qk), F32),
                   jax.ShapeDtypeStruct((nh, s, V_HEAD), F32)],
        compiler_params=_params(("arbitrary",)),
    )(q, k, v, do)


def mla_qkv_bwd(dq, dk, dv, cq, ckv, qan, kvan, wq, wkv, cos, sin, rot_t):
    s = cq.shape[0]
    tm = _row_tile(s)

    def body(dq_ref, dk_ref, dv_ref, cq_ref, ckv_ref, qan_ref, kvan_ref,
             wq_ref, wkv_ref, cos_ref, sin_ref, rot_ref,
             dcq_ref, dckv_ref, dkro_ref, gq_ref, gkv_ref, dqan_ref, dkvan_ref):
        i = pl.program_id(0)

        @pl.when(i == 0)
        def _():
            dqan_ref[...] = jnp.zeros_like(dqan_ref)
            dkvan_ref[...] = jnp.zeros_like(dkvan_ref)

        cos_t = cos_ref[...]
        sin_t = sin_ref[...]
        perm_t = rot_ref[...]

        def unrope(t):
            return t * cos_t + _dot_exact(t * sin_t, perm_t)

        acc_q = jnp.zeros((tm, Q_LORA), F32)
        acc_kv = jnp.zeros((tm, KV_LORA), F32)
        dkr_sum = jnp.zeros((tm, QK_ROPE), F32)
        for h in range(N_HEADS):
            dq_h = dq_ref[h]
            a = dq_h[:, 0:QK_NOPE].astype(BF16)
            b = unrope(dq_h[:, QK_NOPE:]).astype(BF16)
            gq_ref[h, :, 0:QK_NOPE] = a
            gq_ref[h, :, QK_NOPE:] = b
            wq_h = wq_ref[h]
            acc_q += _dot(a, wq_h[0:QK_NOPE, :]) + _dot(b, wq_h[QK_NOPE:, :])
            dk_h = dk_ref[h]
            dk = dk_h[:, 0:QK_NOPE].astype(BF16)
            dvv = dv_ref[h].astype(BF16)
            gkv_ref[h, :, 0:QK_NOPE] = dk
            gkv_ref[h, :, QK_NOPE:] = dvv
            wkv_h = wkv_ref[h]
            acc_kv += _dot_nt(dk, wkv_h[:, 0:QK_NOPE]) + _dot_nt(dvv, wkv_h[:, QK_NOPE:])
            dkr_sum += dk_h[:, QK_NOPE:]
        dkro_ref[...] = unrope(dkr_sum)
        dcq, dqan = _rms_bwd(acc_q, cq_ref[...], qan_ref[...])
        dcq_ref[...] = dcq
        dqan_ref[...] += dqan
        dckv, dkvan = _rms_bwd(acc_kv, ckv_ref[...], kvan_ref[...])
        dckv_ref[...] = dckv
        dkvan_ref[...] += dkvan

    row = lambda w: pl.BlockSpec((tm, w), lambda i: (i, 0))
    hrow = lambda w: pl.BlockSpec((N_HEADS, tm, w), lambda i: (0, i, 0))
    return pl.pallas_call(
        body, name="mla_qkv_bwd",
        grid=(s // tm,),
        in_specs=[hrow(QK_NOPE + QK_ROPE), hrow(QK_NOPE + QK_ROPE), hrow(V_HEAD),
                  row(Q_LORA), row(KV_LORA), _full(qan.shape), _full(kvan.shape),
                  _full(wq.shape), _full(wkv.shape), row(QK_ROPE), row(QK_ROPE), _full(rot_t.shape)],
        out_specs=[row(Q_LORA), row(KV_LORA), row(QK_ROPE), hrow(QK_NOPE + QK_ROPE), hrow(QK_NOPE + V_HEAD),
                   _full(qan.shape), _full(kvan.shape)],
        out_shape=[jax.ShapeDtypeStruct((s, Q_LORA), F32), jax.ShapeDtypeStruct((s, KV_LORA), F32),
                   jax.ShapeDtypeStruct((s, QK_ROPE), F32),
                   jax.ShapeDtypeStruct((N_HEADS, s, QK_NOPE + QK_ROPE), BF16),
                   jax.ShapeDtypeStruct((N_HEADS, s, QK_NOPE + V_HEAD), BF16),
                   jax.ShapeDtypeStruct(qan.shape, F32), jax.ShapeDtypeStruct(kvan.shape, F32)],
        compiler_params=_params(("arbitrary",)),
    )(dq, dk, dv, cq, ckv, qan, kvan, wq, wkv, cos, sin, rot_t)


def out_proj_fwd(yp, om, w_out, x, gt):
    s, d = x.shape
    n_sh, rs, _ = w_out.shape
    tm = _row_tile(s)
    per = POOL_WIDTH // rs

    def body(yp_ref, om_ref, w_ref, x_ref, gt_ref, xo_ref, ycat_ref, y_ref):
        y = jnp.zeros((tm, d), F32)
        for j in range(n_sh):
            src = yp_ref if j < per else om_ref
            part = src[:, (j % per) * rs:(j % per + 1) * rs].astype(BF16)
            ycat_ref[j] = part.T
            y += _dot(part, w_ref[j])
        y_ref[...] = y.astype(BF16)
        xo_ref[...] = x_ref[...] + gt_ref[...] * y

    row = lambda w: pl.BlockSpec((tm, w), lambda i: (i, 0))
    return pl.pallas_call(
        body, name="out_proj_fwd",
        grid=(s // tm,),
        in_specs=[row(POOL_WIDTH), row(POOL_WIDTH), _full(w_out.shape), row(d), pl.BlockSpec((1, d), lambda i: (0, 0))],
        out_specs=[row(d), pl.BlockSpec((n_sh, rs, tm), lambda i: (0, 0, i)), row(d)],
        out_shape=[jax.ShapeDtypeStruct((s, d), F32), jax.ShapeDtypeStruct((n_sh, rs, s), BF16),
                   jax.ShapeDtypeStruct((s, d), BF16)],
        compiler_params=_params(("arbitrary",)),
    )(yp, om, w_out, x, gt)


def out_proj_bwd(dxn, y, gt, w_out):
    s, d = dxn.shape
    n_sh, rs, _ = w_out.shape
    tm = _row_tile(s)
    per = POOL_WIDTH // rs

    def body(dxn_ref, y_ref, gt_ref, w_ref, dy_ref, dyp_ref, dom_ref, dgt_ref):
        i = pl.program_id(0)

        @pl.when(i == 0)
        def _():
            dgt_ref[...] = jnp.zeros_like(dgt_ref)

        dxn_t = dxn_ref[...]
        dy = (gt_ref[...] * dxn_t).astype(BF16)
        dy_ref[...] = dy
        dgt_ref[...] += _sum0(dxn_t * y_ref[...].astype(F32))
        for j in range(n_sh):
            dst = dyp_ref if j < per else dom_ref
            dst[:, (j % per) * rs:(j % per + 1) * rs] = _dot_nt(dy, w_ref[j])

    row = lambda w: pl.BlockSpec((tm, w), lambda i: (i, 0))
    vec = pl.BlockSpec((1, d), lambda i: (0, 0))
    return pl.pallas_call(
        body, name="out_proj_bwd",
        grid=(s // tm,),
        in_specs=[row(d), row(d), vec, _full(w_out.shape)],
        out_specs=[row(d), row(POOL_WIDTH), row(POOL_WIDTH), vec],
        out_shape=[jax.ShapeDtypeStruct((s, d), BF16), jax.ShapeDtypeStruct((s, POOL_WIDTH), F32),
                   jax.ShapeDtypeStruct((s, POOL_WIDTH), F32), jax.ShapeDtypeStruct((1, d), F32)],
        compiler_params=_params(("arbitrary",)),
    )(dxn, y, gt, w_out)


def final_loss(x, gn, tgt):
    s, d = x.shape
    tm = _row_tile(s)

    def body(x_ref, gn_ref, t_ref, loss_ref, dx_ref, dgn_ref):
        i = pl.program_id(0)

        @pl.when(i == 0)
        def _():
            loss_ref[...] = jnp.zeros_like(loss_ref)
            dgn_ref[...] = jnp.zeros_like(dgn_ref)

        xt = x_ref[...]
        g = gn_ref[...]
        xhat, _ = _rms(xt)
        err = xhat * g - t_ref[...]
        per_tok = jnp.mean(err * err, axis=-1, keepdims=True)
        loss_ref[...] += jnp.broadcast_to(0.5 * _sum0(per_tok), loss_ref.shape)
        dx, dgn = _rms_bwd(err * (1.0 / d), xt, g)
        dx_ref[...] = dx
        dgn_ref[...] += dgn

    row = pl.BlockSpec((tm, d), lambda i: (i, 0))
    vec = pl.BlockSpec((1, d), lambda i: (0, 0))
    return pl.pallas_call(
        body, name="final_loss",
        grid=(s // tm,),
        in_specs=[row, vec, row],
        out_specs=[pl.BlockSpec((1, LANES), lambda i: (0, 0)), row, vec],
        out_shape=[jax.ShapeDtypeStruct((1, LANES), F32), jax.ShapeDtypeStruct((s, d), F32),
                   jax.ShapeDtypeStruct((1, d), F32)],
        compiler_params=_params(("arbitrary",)),
    )(x, gn, tgt)


def _col_tile(cols):
    return 768 if cols % 768 == 0 else cols


def ada_fwd(c16, ada_w, ada_b_loc):
    n_layers, d, cols = ada_w.shape
    tn = _col_tile(cols)

    def body(c_ref, w_ref, b_ref, o_ref):
        cv = c_ref[...]
        ca = (cv * jax.nn.sigmoid(cv)).astype(BF16)
        o_ref[...] = _dot(ca, w_ref[...].astype(BF16)) + b_ref[...]

    return pl.pallas_call(
        body, name="ada_fwd",
        grid=(n_layers, cols // tn),
        in_specs=[pl.BlockSpec((16, d), lambda l, j: (0, 0)), pl.BlockSpec((None, d, tn), lambda l, j: (l, 0, j)),
                  pl.BlockSpec((None, 1, tn), lambda l, j: (l, 0, j))],
        out_specs=pl.BlockSpec((None, 16, tn), lambda l, j: (l, 0, j)),
        out_shape=jax.ShapeDtypeStruct((n_layers, 16, cols), F32),
        compiler_params=_params(("arbitrary", "arbitrary")),
    )(c16, ada_w, ada_b_loc)


def ada_bwd(c16, dmod16):
    n_layers, _, cols = dmod16.shape
    d = c16.shape[1]
    tn = _col_tile(cols)

    def body(c_ref, g_ref, o_ref):
        cv = c_ref[...]
        ca = (cv * jax.nn.sigmoid(cv)).astype(BF16)
        o_ref[...] = _dot_tn(ca, g_ref[...].astype(BF16))

    return pl.pallas_call(
        body, name="ada_bwd",
        grid=(n_layers, cols // tn),
        in_specs=[pl.BlockSpec((16, d), lambda l, j: (0, 0)), pl.BlockSpec((None, 16, tn), lambda l, j: (l, 0, j))],
        out_specs=pl.BlockSpec((None, d, tn), lambda l, j: (l, 0, j)),
        out_shape=jax.ShapeDtypeStruct((n_layers, d, cols), F32),
        compiler_params=_params(("arbitrary", "arbitrary")),
    )(c16, dmod16)


def _as_rows(a):
    if a.ndim == 1:
        return a.reshape(1, a.shape[0])
    return a.reshape(-1, a.shape[-1])


def _rows_tile(r, c, itemsize=4, budget=2 * 1024 * 1024):
    if r * c * itemsize <= budget:
        return r
    best = None
    t = BF16_ROWS
    while t < r:
        if r % t == 0 and t * c * itemsize <= budget:
            best = t
        t += BF16_ROWS
    return best if best is not None else r


CAST_VMEM = 16 * 1024 * 1024


def cast_place(ws, chip, layers, after):
    _, r, c = ws[0].shape
    n_sel = len(layers)
    n_blk = len(ws) * n_sel
    tr = _rows_tile(r, c, budget=CAST_VMEM // (3 * n_blk))

    def body(chip_ref, *refs):
        for j in range(n_blk):
            refs[n_blk + 1 + j][...] = refs[j][...].astype(BF16)

    layer_spec = lambda l: pl.BlockSpec((None, tr, c), lambda i, ch: (l, i, 0))
    outs = pl.pallas_call(
        body, name="cast_place",
        grid_spec=pltpu.PrefetchScalarGridSpec(
            num_scalar_prefetch=1, grid=(r // tr,),
            in_specs=[layer_spec(l) for _ in ws for l in layers] + [pl.BlockSpec(memory_space=pl.ANY)],
            out_specs=[pl.BlockSpec((None, tr, c), lambda i, ch: (ch[0], i, 0))] * n_blk),
        out_shape=[jax.ShapeDtypeStruct((N_CHIPS, r, c), BF16)] * n_blk,
        compiler_params=_params(("arbitrary",)),
    )(chip, *[w for w in ws for _ in layers], after)
    return [list(outs[i * n_sel:(i + 1) * n_sel]) for i in range(len(ws))]


def adamw(w, g, m, v, copy_g=False):
    shape = w.shape
    w2, g2, m2, v2 = (_as_rows(t) for t in (w, g, m, v))
    r, c = w2.shape
    tr = _rows_tile(r, c, budget=3 * 1024 * 1024)
    c1 = 1.0 - ADAM_B1 ** ADAM_STEP
    c2 = 1.0 - ADAM_B2 ** ADAM_STEP

    def body(w_ref, g_ref, m_ref, v_ref, d_ref, mo_ref, vo_ref, *go_ref):
        gv = g_ref[...]
        if copy_g:
            go_ref[0][...] = gv
        mn = ADAM_B1 * m_ref[...] + (1.0 - ADAM_B1) * gv
        vn = ADAM_B2 * v_ref[...] + (1.0 - ADAM_B2) * (gv * gv)
        mo_ref[...] = mn
        vo_ref[...] = vn
        d_ref[...] = -ADAM_LR * ((mn / c1) / (jnp.sqrt(vn / c2) + ADAM_EPS) + ADAM_WD * w_ref[...])

    spec = pl.BlockSpec((tr, c), lambda i: (i, 0))
    n_out = 4 if copy_g else 3
    outs = pl.pallas_call(
        body, name="adamw", grid=(r // tr,), in_specs=[spec] * 4, out_specs=[spec] * n_out,
        out_shape=[jax.ShapeDtypeStruct((r, c), F32)] * n_out, compiler_params=_params(("arbitrary",)),
    )(w2, g2, m2, v2)
    g_out = outs[3] if copy_g else g2
    return tuple(o.reshape(shape) for o in (g_out,) + tuple(outs[:3]))


def sum_devices(a):
    n, r, c = a.shape
    tr = _rows_tile(r, c, budget=512 * 1024)

    def body(a_ref, o_ref):
        acc = a_ref[0]
        for j in range(1, n):
            acc = acc + a_ref[j]
        o_ref[...] = acc

    return pl.pallas_call(
        body, name="sum_devices", grid=(r // tr,),
        in_specs=[pl.BlockSpec((n, tr, c), lambda i: (0, i, 0))], out_specs=pl.BlockSpec((tr, c), lambda i: (i, 0)),
        out_shape=jax.ShapeDtypeStruct((r, c), F32), compiler_params=_params(("arbitrary",)),
    )(a)


def _split_axis(r, c):
    if (r // 2) % BF16_ROWS == 0 and r % 2 == 0:
        return 0
    assert c % (2 * LANES) == 0, (r, c)
    return 1


def _half_shape(r, c):
    return (r // 2, c) if _split_axis(r, c) == 0 else (r, c // 2)


def _half_at(ref, lead, which):
    r, c = ref.shape[-2:]
    if _split_axis(r, c) == 0:
        return ref.at[(*lead, pl.ds(which * (r // 2), r // 2), slice(None))]
    return ref.at[(*lead, slice(None), pl.ds(which * (c // 2), c // 2))]


def _half_spec(r, c, lead_block, imap):
    hr, hc = _half_shape(r, c)
    if _split_axis(r, c) == 0:
        return pl.BlockSpec((*lead_block, hr, hc), lambda *a: (*imap(*a)[0], imap(*a)[1], 0))
    return pl.BlockSpec((*lead_block, hr, hc), lambda *a: (*imap(*a)[0], 0, imap(*a)[1]))


def pair_add(gs, ras, sel):
    n = len(gs)
    n_sl = gs[0].shape[0]
    halves = [_half_shape(*g.shape[1:]) for g in gs]

    def body(s_ref, *refs):
        g_refs, ra_refs, pb_refs, own_refs = (refs[i * n:(i + 1) * n] for i in range(4))
        k = pl.program_id(0)
        for t in range(n):
            p = g_refs[t][...] + ra_refs[t][...]
            pb_refs[t][...] = p.astype(BF16)

            @pl.when(k == s_ref[1])
            def _(p=p, own=own_refs[t]):
                own[...] = p

    slot = lambda hs: pl.BlockSpec((None,) + hs, lambda k, sr: (k, 0, 0))
    outs = pl.pallas_call(
        body, name="pair_add",
        grid_spec=pltpu.PrefetchScalarGridSpec(
            num_scalar_prefetch=1, grid=(n_sl,),
            in_specs=[_half_spec(*g.shape[1:], (None,), lambda k, sr: ((k,), sr[0])) for g in gs]
            + [slot(hs) for hs in halves],
            out_specs=[slot(hs) for hs in halves] + [pl.BlockSpec(hs, lambda k, sr: (0, 0)) for hs in halves]),
        out_shape=[jax.ShapeDtypeStruct((n_sl,) + hs, BF16) for hs in halves]
        + [jax.ShapeDtypeStruct(hs, F32) for hs in halves],
        compiler_params=_params(("arbitrary",)),
    )(sel, *gs, *ras)
    return list(outs[:n]), list(outs[n:])


def chip_sum(owns, rbs, sel, shapes, accs):
    n = len(owns)
    fresh = accs[0] is None

    def body(s_ref, *refs):
        own_refs, rb_refs, o_refs = refs[:n], refs[n:2 * n], refs[-n:]
        for t in range(n):
            acc_v = own_refs[t][...]
            for j in range(N_CHIPS - 1):
                acc_v = acc_v + rb_refs[t][j].astype(F32)
            o_refs[t][...] = acc_v

    in_specs = ([pl.BlockSpec(o.shape, lambda i, sr: (0, 0)) for o in owns]
                + [pl.BlockSpec(rb.shape, lambda i, sr: (0, 0, 0)) for rb in rbs])
    args = [sel, *owns, *rbs]
    aliases = {}
    if not fresh:
        in_specs += [pl.BlockSpec(memory_space=pl.ANY)] * n
        args += list(accs)
        aliases = {1 + 2 * n + t: t for t in range(n)}
    return list(pl.pallas_call(
        body, name="chip_sum",
        grid_spec=pltpu.PrefetchScalarGridSpec(
            num_scalar_prefetch=1, grid=(1,), in_specs=in_specs,
            out_specs=[_half_spec(*shp[1:], (None,), lambda i, sr: ((sr[2],), sr[0])) for shp in shapes]),
        out_shape=[jax.ShapeDtypeStruct(shp, F32) for shp in shapes],
        input_output_aliases=aliases,
        compiler_params=_params(("arbitrary",)),
    )(*args))


def _me():
    return lax.axis_index("x"), lax.axis_index("y"), lax.axis_index("c")


def _flip(v, bit):
    return 1 - v if bit else v


def exchange8(xs, bcast):
    blk = xs.shape if bcast else xs.shape[1:]

    def body(x_ref, o_ref, send_sems, recv_sems, loc_sem):
        mx, my, mc = _me()
        me = 4 * mx + 2 * my + mc
        src = (lambda j: x_ref) if bcast else (lambda j: x_ref.at[j])
        loc = pltpu.make_async_copy(src(me), o_ref.at[me], loc_sem)
        loc.start()
        copies = []
        for o in range(1, N_DEV):
            px, py, pc = _flip(mx, o & 4), _flip(my, o & 2), _flip(mc, o & 1)
            cp = pltpu.make_async_remote_copy(
                src_ref=src(4 * px + 2 * py + pc), dst_ref=o_ref.at[me],
                send_sem=send_sems.at[o - 1], recv_sem=recv_sems.at[o - 1],
                device_id=(px, py, pc), device_id_type=MESH)
            cp.start()
            copies.append(cp)
        for cp in copies:
            cp.wait()
        loc.wait()

    return pl.pallas_call(
        body, name="exchange8_gather" if bcast else "exchange8_a2a",
        in_specs=[pl.BlockSpec(memory_space=pltpu.VMEM)], out_specs=pl.BlockSpec(memory_space=pltpu.VMEM),
        out_shape=jax.ShapeDtypeStruct((N_DEV,) + tuple(blk), xs.dtype),
        scratch_shapes=[pltpu.SemaphoreType.DMA((N_DEV - 1,)), pltpu.SemaphoreType.DMA((N_DEV - 1,)), pltpu.SemaphoreType.DMA],
        compiler_params=_params(),
    )(xs)


HBM = pl.BlockSpec(memory_space=pltpu.HBM)
SEM = pl.BlockSpec(memory_space=pltpu.SEMAPHORE)
EFFECT = pltpu.SideEffectType.DATAFLOW_SIDE_EFFECTING


def _hbm(a):
    return pltpu.with_memory_space_constraint(a, pltpu.HBM)


def _ici_copy(land, o, send_sem, recv_sem, sending):
    mx, my, mc = _me()
    px, py = _flip(mx, o & 2), _flip(my, o & 1)
    mine = _half_at(land, (2 * mx + my,), mc)
    return pltpu.make_async_remote_copy(
        src_ref=mine, dst_ref=mine if sending else _half_at(land, (2 * px + py,), mc),
        send_sem=send_sem, recv_sem=recv_sem, device_id=(px, py, mc), device_id_type=MESH)


N_PEERS = N_CHIPS - 1
DMA_SEM = pltpu.SemaphoreType.DMA(())


def gather_start(lands, groups, after, tag):
    n_layers, n = len(lands), len(lands[0])
    flat = [a for layer in lands for a in layer]
    n_in = n * n_layers
    n_grp = len(groups)
    n_sem = 2 * n_layers * n_grp * N_PEERS
    first = lambda l, g, recv: ((l * n_grp + g) * 2 + recv) * N_PEERS

    def body(*refs):
        land = refs[:n_in]
        sems = refs[n_in + 1:n_in + 1 + n_sem]
        token = refs[-1]
        for l in range(n_layers):
            for g, members in enumerate(groups):
                for t in members:
                    for o in range(1, N_CHIPS):
                        _ici_copy(land[l * n + t], o, sems[first(l, g, 0) + o - 1], sems[first(l, g, 1) + o - 1],
                                  True).start()
        token[...] = jnp.zeros_like(token)

    outs = pl.pallas_call(
        body, name=f"gather_start_{tag}",
        in_specs=[HBM] * n_in + [pl.BlockSpec(memory_space=pl.ANY)],
        out_specs=[SEM] * n_sem + [HBM] * n_in + [pl.BlockSpec(memory_space=pltpu.VMEM)],
        out_shape=[DMA_SEM] * n_sem + [pltpu.HBM(a.shape, a.dtype) for a in flat]
        + [jax.ShapeDtypeStruct((8, LANES), F32)],
        input_output_aliases={i: i + n_sem for i in range(n_in)},
        compiler_params=pltpu.CompilerParams(has_side_effects=EFFECT),
    )(*[_hbm(a) for a in flat], after)
    sems = [[(list(outs[first(l, g, 0):first(l, g, 0) + N_PEERS]), list(outs[first(l, g, 1):first(l, g, 1) + N_PEERS]))
             for g in range(n_grp)] for l in range(n_layers)]
    lands_thru = [list(outs[n_sem + l * n:n_sem + (l + 1) * n]) for l in range(n_layers)]
    return sems, lands_thru, outs[-1]


def gather_wait(tag, sems, lands, after):
    n = len(lands)
    send_sems, recv_sems = sems

    def body(*refs):
        land = refs[:n]
        send_r = refs[n:n + N_PEERS]
        recv_r = refs[n + N_PEERS:n + 2 * N_PEERS]
        for t in range(n):
            for o in range(1, N_CHIPS):
                _ici_copy(land[t], o, send_r[o - 1], recv_r[o - 1], True).wait_send()
                _ici_copy(land[t], o, send_r[o - 1], recv_r[o - 1], False).wait_recv()

    return list(pl.pallas_call(
        body, name=f"gather_wait_{tag}",
        in_specs=[HBM] * n + [SEM] * (2 * N_PEERS) + [pl.BlockSpec(memory_space=pl.ANY)],
        out_specs=[HBM] * n,
        out_shape=[pltpu.HBM(a.shape, a.dtype) for a in lands],
        input_output_aliases={i: i for i in range(n)},
        compiler_params=pltpu.CompilerParams(has_side_effects=EFFECT),
    )(*lands, *send_sems, *recv_sems, after))


def gather_forward(lands):
    n = len(lands)

    def body(*refs):
        dst = refs[n:2 * n]
        send_sems, recv_sems = refs[2 * n:]
        mx, my, mc = _me()
        fwds = []
        for t in range(n):
            for o in range(1, N_CHIPS):
                slot = 2 * _flip(mx, o & 2) + _flip(my, o & 1)
                mine = _half_at(dst[t], (slot,), mc)
                theirs = _half_at(dst[t], (slot,), 1 - mc)
                cp = pltpu.make_async_remote_copy(
                    src_ref=mine, dst_ref=mine, send_sem=send_sems.at[t, o - 1], recv_sem=recv_sems.at[t, o - 1],
                    device_id=(mx, my, 1 - mc), device_id_type=MESH)
                cp.start()
                fwds.append((cp, pltpu.make_async_remote_copy(
                    src_ref=theirs, dst_ref=theirs, send_sem=send_sems.at[t, o - 1], recv_sem=recv_sems.at[t, o - 1],
                    device_id=(mx, my, 1 - mc), device_id_type=MESH)))
        for cp, arrival in fwds:
            cp.wait_send()
            arrival.wait_recv()

    any_spec = pl.BlockSpec(memory_space=pl.ANY)
    return list(pl.pallas_call(
        body, name="gather_forward",
        in_specs=[any_spec] * n, out_specs=[any_spec] * n,
        out_shape=[jax.ShapeDtypeStruct(a.shape, a.dtype) for a in lands],
        input_output_aliases={t: t for t in range(n)},
        scratch_shapes=[pltpu.SemaphoreType.DMA((n, N_CHIPS - 1)), pltpu.SemaphoreType.DMA((n, N_CHIPS - 1))],
        compiler_params=_params(),
    )(*lands))


def _scatter_copy(src, land, o, send_sem, recv_sem):
    mx, my, mc = _me()
    px, py = _flip(mx, o & 2), _flip(my, o & 1)
    return pltpu.make_async_remote_copy(
        src_ref=src.at[2 * px + py], dst_ref=land.at[o - 1],
        send_sem=send_sem, recv_sem=recv_sem, device_id=(px, py, mc), device_id_type=MESH)


def scatter_start(pbs, tag, after):
    n = len(pbs)
    lands = [lax.empty((N_CHIPS - 1,) + p.shape[1:], p.dtype) for p in pbs]

    def body(*refs):
        src = refs[:n]
        land = refs[n:2 * n]
        send_sems = refs[2 * n + 1:2 * n + 1 + N_PEERS]
        recv_sems = refs[2 * n + 1 + N_PEERS:2 * n + 1 + 2 * N_PEERS]
        token = refs[-1]
        for t in range(n):
            for o in range(1, N_CHIPS):
                _scatter_copy(src[t], land[t], o, send_sems[o - 1], recv_sems[o - 1]).start()
        token[...] = jnp.zeros_like(token)

    n_sem = 2 * N_PEERS
    arrs = list(pbs) + lands
    outs = pl.pallas_call(
        body, name=f"scatter_start_{tag}",
        in_specs=[HBM] * (2 * n) + [pl.BlockSpec(memory_space=pl.ANY)],
        out_specs=[SEM] * n_sem + [HBM] * (2 * n) + [pl.BlockSpec(memory_space=pltpu.VMEM)],
        out_shape=[DMA_SEM] * n_sem + [pltpu.HBM(a.shape, a.dtype) for a in arrs]
        + [jax.ShapeDtypeStruct((8, LANES), F32)],
        input_output_aliases={i: i + n_sem for i in range(2 * n)},
        compiler_params=pltpu.CompilerParams(has_side_effects=EFFECT),
    )(*[_hbm(a) for a in arrs], after)
    return (list(outs[:N_PEERS]), list(outs[N_PEERS:n_sem]), list(outs[n_sem:n_sem + n]),
            list(outs[n_sem + n:n_sem + 2 * n]), outs[-1])


def scatter_wait(tag, send_sems, recv_sems, pbs, lands, after):
    n = len(pbs)

    def body(*refs):
        src = refs[:n]
        land = refs[n:2 * n]
        send_r = refs[2 * n:2 * n + N_PEERS]
        recv_r = refs[2 * n + N_PEERS:2 * n + 2 * N_PEERS]
        for t in range(n):
            for o in range(1, N_CHIPS):
                cp = _scatter_copy(src[t], land[t], o, send_r[o - 1], recv_r[o - 1])
                cp.wait_send()
                cp.wait_recv()

    arrs = list(pbs) + list(lands)
    outs = pl.pallas_call(
        body, name=f"scatter_wait_{tag}",
        in_specs=[HBM] * (2 * n) + [SEM] * (2 * N_PEERS) + [pl.BlockSpec(memory_space=pl.ANY)],
        out_specs=[HBM] * (2 * n),
        out_shape=[pltpu.HBM(a.shape, a.dtype) for a in arrs],
        input_output_aliases={i: i for i in range(2 * n)},
        compiler_params=pltpu.CompilerParams(has_side_effects=EFFECT),
    )(*arrs, *send_sems, *recv_sems, after)
    return list(outs[n:])


def _pair_copy(src, land, send_sem, recv_sem):
    mx, my, mc = _me()
    return pltpu.make_async_remote_copy(
        src_ref=_half_at(src, (slice(None),), 1 - mc), dst_ref=land, send_sem=send_sem, recv_sem=recv_sem,
        device_id=(mx, my, 1 - mc), device_id_type=MESH)


def pair_start(gs, tag, after):
    n = len(gs)
    lands = [lax.empty((g.shape[0],) + _half_shape(*g.shape[1:]), g.dtype) for g in gs]

    def body(*refs):
        src = refs[:n]
        land = refs[n:2 * n]
        send_sem, recv_sem = refs[2 * n + 1], refs[2 * n + 2]
        token = refs[-1]
        for t in range(n):
            _pair_copy(src[t], land[t], send_sem, recv_sem).start()
        token[...] = jnp.zeros_like(token)

    arrs = list(gs) + lands
    outs = pl.pallas_call(
        body, name=f"pair_start_{tag}",
        in_specs=[HBM] * (2 * n) + [pl.BlockSpec(memory_space=pl.ANY)],
        out_specs=[SEM, SEM] + [HBM] * (2 * n) + [pl.BlockSpec(memory_space=pltpu.VMEM)],
        out_shape=[DMA_SEM, DMA_SEM] + [pltpu.HBM(a.shape, a.dtype) for a in arrs] + [jax.ShapeDtypeStruct((8, LANES), F32)],
        input_output_aliases={i: i + 2 for i in range(2 * n)},
        compiler_params=pltpu.CompilerParams(has_side_effects=EFFECT),
    )(*[_hbm(a) for a in arrs], after)
    return outs[0], outs[1], list(outs[2:2 + n]), list(outs[2 + n:2 + 2 * n]), outs[-1]


def pair_wait(tag, send_sem, recv_sem, gs, lands, after):
    n = len(gs)

    def body(*refs):
        src = refs[:n]
        land = refs[n:2 * n]
        send_r, recv_r = refs[2 * n], refs[2 * n + 1]
        for t in range(n):
            cp = _pair_copy(src[t], land[t], send_r, recv_r)
            cp.wait_send()
            cp.wait_recv()

    arrs = list(gs) + list(lands)
    outs = pl.pallas_call(
        body, name=f"pair_wait_{tag}",
        in_specs=[HBM] * (2 * n) + [SEM, SEM, pl.BlockSpec(memory_space=pl.ANY)],
        out_specs=[HBM] * (2 * n),
        out_shape=[pltpu.HBM(a.shape, a.dtype) for a in arrs],
        input_output_aliases={i: i for i in range(2 * n)},
        compiler_params=pltpu.CompilerParams(has_side_effects=EFFECT),
    )(*arrs, send_sem, recv_sem, after)
    return list(outs[:n]), list(outs[n:])


def _gather8_copy(x, land, o, send_sem, recv_sem, sending):
    mx, my, mc = _me()
    px, py, pc = _flip(mx, o & 4), _flip(my, o & 2), _flip(mc, o & 1)
    slot = 4 * mx + 2 * my + mc if sending else 4 * px + 2 * py + pc
    return pltpu.make_async_remote_copy(
        src_ref=x, dst_ref=land.at[slot], send_sem=send_sem, recv_sem=recv_sem,
        device_id=(px, py, pc), device_id_type=MESH)


def gather8_start(x, land, after, tag):
    n_peer = N_DEV - 1

    def body(x_ref, land_ref, after_ref, *rest):
        send_sems, recv_sems = rest[:n_peer], rest[n_peer:2 * n_peer]
        token = rest[-1]
        for o in range(1, N_DEV):
            _gather8_copy(x_ref, land_ref, o, send_sems[o - 1], recv_sems[o - 1], True).start()
        token[...] = jnp.zeros_like(token)

    outs = pl.pallas_call(
        body, name=f"gather8_start_{tag}",
        in_specs=[HBM, HBM, pl.BlockSpec(memory_space=pl.ANY)],
        out_specs=[SEM] * (2 * n_peer) + [HBM, HBM, pl.BlockSpec(memory_space=pltpu.VMEM)],
        out_shape=[DMA_SEM] * (2 * n_peer) + [pltpu.HBM(x.shape, x.dtype), pltpu.HBM(land.shape, land.dtype),
                                              jax.ShapeDtypeStruct((8, LANES), F32)],
        input_output_aliases={0: 2 * n_peer, 1: 2 * n_peer + 1},
        compiler_params=pltpu.CompilerParams(has_side_effects=EFFECT),
    )(_hbm(x), _hbm(land), after)
    return list(outs[:n_peer]), list(outs[n_peer:2 * n_peer]), outs[2 * n_peer], outs[2 * n_peer + 1], outs[-1]


def gather8_wait(tag, send_sems, recv_sems, x, land, after):
    n_peer = N_DEV - 1

    def body(x_ref, land_ref, *rest):
        send_r, recv_r = rest[:n_peer], rest[n_peer:2 * n_peer]
        for o in range(1, N_DEV):
            _gather8_copy(x_ref, land_ref, o, send_r[o - 1], recv_r[o - 1], True).wait_send()
            _gather8_copy(x_ref, land_ref, o, send_r[o - 1], recv_r[o - 1], False).wait_recv()

    return pl.pallas_call(
        body, name=f"gather8_wait_{tag}",
        in_specs=[HBM, HBM] + [SEM] * (2 * n_peer) + [pl.BlockSpec(memory_space=pl.ANY)],
        out_specs=[HBM, HBM],
        out_shape=[pltpu.HBM(x.shape, x.dtype), pltpu.HBM(land.shape, land.dtype)],
        input_output_aliases={0: 0, 1: 1},
        compiler_params=pltpu.CompilerParams(has_side_effects=EFFECT),
    )(x, land, *send_sems, *recv_sems, after)[1]


def pair_fill_halves(fs):
    n = len(fs)

    def body(*refs):
        dst = refs[n:2 * n]
        send_sems, recv_sems = refs[2 * n:]
        mx, my, mc = _me()
        copies = []
        for t in range(n):
            mine = _half_at(dst[t], (slice(None),), mc)
            theirs = _half_at(dst[t], (slice(None),), 1 - mc)
            cp = pltpu.make_async_remote_copy(
                src_ref=mine, dst_ref=mine, send_sem=send_sems.at[t], recv_sem=recv_sems.at[t],
                device_id=(mx, my, 1 - mc), device_id_type=MESH)
            cp.start()
            copies.append((cp, pltpu.make_async_remote_copy(
                src_ref=theirs, dst_ref=theirs, send_sem=send_sems.at[t], recv_sem=recv_sems.at[t],
                device_id=(mx, my, 1 - mc), device_id_type=MESH)))
        for cp, arrival in copies:
            cp.wait_send()
            arrival.wait_recv()

    any_spec = pl.BlockSpec(memory_space=pl.ANY)
    return pl.pallas_call(
        body, name="pair_fill_halves",
        in_specs=[any_spec] * n, out_specs=[any_spec] * n,
        out_shape=[jax.ShapeDtypeStruct(f.shape, f.dtype) for f in fs],
        input_output_aliases={t: t for t in range(n)},
        scratch_shapes=[pltpu.SemaphoreType.DMA((n,)), pltpu.SemaphoreType.DMA((n,))],
        compiler_params=_params(),
    )(*fs)


def _pack_rows(parts, d):
    rows, spans = [], []
    at = 0
    for p in parts:
        flat = p.reshape(-1)
        n_rows = -(-flat.shape[0] // (8 * d)) * 8
        flat = jnp.pad(flat, (0, n_rows * d - flat.shape[0]))
        rows.append(flat.reshape(n_rows, d))
        spans.append((at, p.shape))
        at += n_rows
    return jnp.concatenate(rows, axis=0), spans


def _unpack_rows(packed, spans):
    lead, d = packed.shape[:-2], packed.shape[-1]
    out = []
    for at, shape in spans:
        n = math.prod(shape)
        n_rows = -(-n // d)
        out.append(packed[..., at:at + n_rows, :].reshape(lead + (-1,))[..., :n].reshape(lead + tuple(shape)))
    return out


def _rotate_half_matrix():
    half = QK_ROPE // 2
    idx = jnp.arange(QK_ROPE)
    src = jnp.where(idx < half, idx + half, idx - half)
    sign = jnp.where(idx < half, -1.0, 1.0)
    return (jnp.zeros((QK_ROPE, QK_ROPE), F32).at[src, idx].set(sign)).astype(BF16)


def kernel(x, c, positions, ada_w, ada_b, ffn1_norm, ffn1_w_gate, ffn1_w_up, ffn1_w_down, mix_norm, w_in, pool_w, pool_scale, q_a_norm, w_q_b, kv_a_norm, w_kv_b, w_out, ffn2_norm, ffn2_w_gate, ffn2_w_up, ffn2_w_down, final_norm, loss_target, m_ada_w, m_ada_b, m_ffn1_norm, m_ffn1_w_gate, m_ffn1_w_up, m_ffn1_w_down, m_mix_norm, m_w_in, m_pool_w, m_pool_scale, m_q_a_norm, m_w_q_b, m_kv_a_norm, m_w_kv_b, m_w_out, m_ffn2_norm, m_ffn2_w_gate, m_ffn2_w_up, m_ffn2_w_down, m_final_norm, v_ada_w, v_ada_b, v_ffn1_norm, v_ffn1_w_gate, v_ffn1_w_up, v_ffn1_w_down, v_mix_norm, v_w_in, v_pool_w, v_pool_scale, v_q_a_norm, v_w_q_b, v_kv_a_norm, v_w_kv_b, v_w_out, v_ffn2_norm, v_ffn2_w_gate, v_ffn2_w_up, v_ffn2_w_down, v_final_norm):
    mx, my, mc = _me()
    chip = 2 * mx + my
    half = jnp.reshape(mc, (1,)).astype(jnp.int32)
    chip1 = jnp.reshape(chip, (1,)).astype(jnp.int32)
    n_layers, d, ada_cols = ada_w.shape
    xt = x[0]
    tgt = loss_target[0]

    inv_freq = 1.0 / (ROPE_THETA ** (jnp.arange(0, QK_ROPE, 2, dtype=F32) / QK_ROPE))
    ang = positions[0].astype(F32)[:, None] * inv_freq
    ang = jnp.concatenate([ang, ang], axis=-1)
    cos, sin = jnp.cos(ang), jnp.sin(ang)
    rot = _rotate_half_matrix()
    rot_t = rot.T

    c_all = exchange8(c, True).reshape(N_DEV, d)
    c16 = jnp.pad(c_all, ((0, 8), (0, 0)))
    ada_b_loc = lax.dynamic_slice_in_dim(ada_b, chip * ada_cols, ada_cols, axis=1).reshape(n_layers, 1, ada_cols)
    mod_part = ada_fwd(c16, ada_w, ada_b_loc)[:, :N_DEV]
    mod_got = exchange8(jnp.transpose(mod_part, (1, 0, 2)), False)
    mod = jnp.transpose(mod_got.reshape(N_CHIPS, 2, n_layers, ada_cols)[:, 0], (1, 0, 2))
    mod = mod.reshape(n_layers, 9, 1, d)

    tr = lambda a: jnp.transpose(a, (0, 2, 1))
    local = [tr(ffn1_w_gate), tr(ffn1_w_up), ffn1_w_down, tr(w_in), tr(w_q_b), w_kv_b, w_out,
             tr(ffn2_w_gate), tr(ffn2_w_up), ffn2_w_down]
    ffn1_pos, mixer_pos, ffn2_pos = (0, 1, 2), (3, 4, 5, 6), (7, 8, 9)
    rest_pos = mixer_pos + ffn2_pos

    def cast_all(layers, after):
        by_shape = {}
        for t, w in enumerate(local):
            by_shape.setdefault(w.shape, []).append(t)
        out = [None] * len(local)
        for ts in by_shape.values():
            for t, per_layer in zip(ts, cast_place([local[t] for t in ts], chip1, layers, after)):
                out[t] = per_layer
        return out

    placed = cast_all((0,), mod)
    g_sems, lands_fly, g_token = gather_start([[p[0] for p in placed]], (ffn1_pos, mixer_pos, ffn2_pos), mod, "first")
    if n_layers > 1:
        later = tuple(range(1, n_layers))
        placed = cast_all(later, g_token)
        more_sems, more_fly, g_token = gather_start(
            [[p[j] for p in placed] for j in range(len(later))], (ffn1_pos, rest_pos), g_token, "rest")
        g_sems, lands_fly = g_sems + more_sems, lands_fly + more_fly
    gathered = []

    row = lambda a, l: a[l].reshape(1, -1)
    saved = []
    for l in range(n_layers):
        def fetch(tag, group, members, after, l=l):
            return gather_forward(gather_wait(tag, g_sems[l][group], [lands_fly[l][t] for t in members], after))

        g1, u1, d1 = fetch(f"{l}a", 0, ffn1_pos, xt if l else g_token)
        sv = dict(x0=xt)
        xt, sv["h1"], sv["a1"], sv["sl1"], sv["dsu1"], sv["y1"] = ffn_fwd(
            xt, row(ffn1_norm, l), mod[l, 0], mod[l, 1], mod[l, 2], g1, u1, d1)
        sv["x1"] = xt
        if l == 0:
            win, wq, wkv, wout = fetch("0b", 1, mixer_pos, xt)
        else:
            win, wq, wkv, wout, g2, u2, d2 = fetch(f"{l}b", 1, rest_pos, xt)
        win = win.reshape(-1, d)
        sv["h2"], u, cq, ckv, kr = mix_in_fwd(xt, row(mix_norm, l), mod[l, 3], mod[l, 4], win)
        sv["cq"], sv["ckv"] = cq, ckv
        yp, sv["diff"] = pool_fwd(u, pool_w[l], row(pool_scale, l))
        qh, kh, vh, sv["ql"], sv["kvl"] = mla_qkv_fwd(
            cq, ckv, kr, row(q_a_norm, l), row(kv_a_norm, l), wq, wkv, cos, sin, rot)
        sv["qkv"] = (qh, kh, vh)
        om = attn_fwd(qh, kh, vh)
        xt, sv["ycat"], sv["y2"] = out_proj_fwd(yp, om, wout, xt, mod[l, 5])
        sv["x2"] = xt
        if l == 0:
            g2, u2, d2 = fetch("0c", 2, ffn2_pos, xt)
        gathered.append([g1, u1, d1, win, wq, wkv, wout, g2, u2, d2])
        xt, sv["h3"], sv["a3"], sv["sl3"], sv["dsu3"], sv["y3"] = ffn_fwd(
            xt, row(ffn2_norm, l), mod[l, 6], mod[l, 7], mod[l, 8], g2, u2, d2)
        saved.append(sv)

    loss_vec, dx, d_final_norm = final_loss(xt, final_norm.reshape(1, d), tgt)
    loss = lax.psum(loss_vec[0, 0], ("x", "y", "c"))

    none = [None] * n_layers
    dmods, dnorm1, dnorm2, dnorm3 = list(none), list(none), list(none), list(none)
    dpw, dps, dqan_l, dkvan_l = list(none), list(none), list(none), list(none)
    reduced = [None] * len(local)
    stages = []
    sel_of = lambda l: jnp.stack([mc, chip, jnp.asarray(l, mc.dtype)]).astype(jnp.int32)

    def to_chips(job, after_wait, after_start):
        send, recv, g_fly, lands_p = job.pop("pair")
        g_fly, got = pair_wait(job["tag"], send, recv, g_fly, lands_p, after_wait)
        pbs, job["owns"] = pair_add(g_fly, got, sel_of(job["l"]))
        job["scatter"] = scatter_start(pbs, job["tag"], after_start)
        return job["scatter"][4][0, 0]

    def finish(job, after):
        s_send, s_recv, pbs_fly, lands_j, _ = job.pop("scatter")
        parts = scatter_wait(job["tag"], s_send, s_recv, pbs_fly, lands_j, after)
        sums = chip_sum(job["owns"], parts, sel_of(job["l"]), [(n_layers,) + shp for shp in job["shapes"]],
                        [reduced[t] for t in job["pos"]])
        for t, total_t in zip(job["pos"], sums):
            reduced[t] = total_t

    def checkpoint(tag, l, positions, grads_, done, before_scatter=None):
        send, recv, g_fly, lands_p, tok = pair_start(grads_, tag, done)
        order = tok[0, 0]
        if stages:
            order = order + to_chips(stages[-1], done, done if before_scatter is None else before_scatter)
        if len(stages) >= 3:
            finish(stages[-3], done)
        stages.append(dict(tag=tag, l=l, pos=positions, shapes=[g.shape[1:] for g in grads_],
                           pair=(send, recv, g_fly, lands_p)))
        return order

    def small_gather(tag, parts, after):
        packed, spans = _pack_rows(parts, d)
        land = lax.dynamic_update_index_in_dim(lax.empty((N_DEV,) + packed.shape, F32), packed, 4 * mx + 2 * my + mc, 0)
        return gather8_start(packed, land, after, tag), spans

    order = None

    for l in reversed(range(n_layers)):
        sv = saved[l]
        g1, u1, d1, win, wq, wkv, wout, g2, u2, d2 = gathered[l]
        win = win.reshape(-1, d)
        gt3 = mod[l, 8] if order is None else mod[l, 8] + order
        dy, dgt, dup = ffn_bwd_act(dx, sv["sl3"], sv["dsu3"], gt3, d2)
        dx, dvec3 = ffn_bwd_in(dx, sv["x2"], sv["y3"], dgt, dup, row(ffn2_norm, l), mod[l, 7], g2, u2)
        g_g2, g_u2, g_d2 = tn_mm(dgt, sv["h3"][None]), tn_mm(dup, sv["h3"][None]), nn_mm(sv["a3"], dy)
        dy2, dyp, dom, dg2 = out_proj_bwd(dx, sv["y2"], mod[l, 5], wout)
        g_wout = nn_mm(sv["ycat"], dy2)
        qh, kh, vh = sv["qkv"]
        dqh, dkh, dvh = attn_bwd(qh, kh, vh, dom)
        dcq, dckv, dkr_in, gq, gkv, dqan_l[l], dkvan_l[l] = mla_qkv_bwd(
            dqh, dkh, dvh, sv["cq"], sv["ckv"], row(q_a_norm, l), row(kv_a_norm, l), wq, wkv, cos, sin, rot_t)
        g_wq, g_wkv = tn_mm(gq, sv["ql"][None]), tn_mm(sv["kvl"][None], gkv)
        du, dpw[l], dps[l] = pool_bwd(dyp, sv["diff"], pool_w[l], row(pool_scale, l))
        dx, dz, dvec2 = mix_in_bwd(dx, du, dcq, dckv, dkr_in, sv["x1"], row(mix_norm, l), mod[l, 4], win)
        g_win = nn_mm(dz[None], sv["h2"]).reshape(N_CHIPS, -1, d)
        dnorm2[l], dnorm3[l] = dvec2[3], dvec3[3]
        dmod_rest = jnp.concatenate([dvec2[0:2], dg2, dvec3[0:3]], axis=0)
        if l == 0:
            early = small_gather("early", [jnp.stack(dmods[1:]), dmod_rest, jnp.stack(dnorm1[1:]), jnp.stack(dnorm2),
                                           jnp.stack(dnorm3), d_final_norm, jnp.stack(dps), jnp.stack(dqan_l),
                                           jnp.stack(dkvan_l), jnp.stack(dpw)], dx)
        order = checkpoint(f"{l}a", l, rest_pos, [g_win, g_wq, g_wkv, g_wout, g_g2, g_u2, g_d2], dx,
                           early[0][4] if l == 0 else None)
        dy, dgt, dup = ffn_bwd_act(dx, sv["sl1"], sv["dsu1"], mod[l, 2] + order, d1)
        dx, dvec1 = ffn_bwd_in(dx, sv["x0"], sv["y1"], dgt, dup, row(ffn1_norm, l), mod[l, 1], g1, u1)
        g_g1, g_u1, g_d1 = tn_mm(dgt, sv["h1"][None]), tn_mm(dup, sv["h1"][None]), nn_mm(sv["a1"], dy)
        dmods[l] = jnp.concatenate([dvec1[0:3], dmod_rest], axis=0)
        dnorm1[l] = dvec1[3]
        if l == 0:
            late = small_gather("late", [dvec1[0:3], dvec1[3]], dx)
        order = checkpoint(f"{l}b", l, ffn1_pos, [g_g1, g_u1, g_d1], dx, late[0][4] if l == 0 else None)

    to_chips(stages[-1], stages[-2]["scatter"][4], stages[-2]["scatter"][4])
    sent = stages[-1]["scatter"][4]
    got_early = gather8_wait("early", *early[0][:4], sent)
    got_late = gather8_wait("late", *late[0][:4], sent)
    (g_dmod_rest, g_dmod0_rest, g_n1_rest, g_n2, g_n3, g_fn, g_ps, g_qan, g_kvan, g_pw) = _unpack_rows(
        sum_devices(got_early), early[1])
    g_dmod0_first, g_n1_first = _unpack_rows(sum_devices(got_late), late[1])
    g_ada_b = jnp.concatenate([jnp.concatenate([g_dmod0_first, g_dmod0_rest], axis=0)[None], g_dmod_rest], axis=0)
    g_n1 = jnp.concatenate([g_n1_first[None], g_n1_rest], axis=0)
    each_rest, each0_rest = _unpack_rows(got_early, early[1])[:2]
    each0_first = _unpack_rows(got_late, late[1])[0]
    dmod_all = jnp.concatenate([jnp.concatenate([each0_first, each0_rest], axis=1)[:, None], each_rest], axis=1)
    dmod_all = dmod_all.reshape(N_DEV, n_layers, 9 * d)
    dmod_loc = lax.dynamic_slice_in_dim(dmod_all, chip * ada_cols, ada_cols, axis=2)
    dmod16 = jnp.pad(jnp.transpose(dmod_loc, (1, 0, 2)), ((0, 0), (0, 8), (0, 0)))
    g_ada_w = ada_bwd(c16, dmod16)

    grads = [g_ada_w, g_ada_b, g_n1, None, None, None, g_n2, None, g_pw, g_ps, g_qan, None, g_kvan, None, None, g_n3,
             None, None, None, g_fn]
    weights = [ada_w, ada_b, ffn1_norm, ffn1_w_gate, ffn1_w_up, ffn1_w_down, mix_norm, w_in, pool_w, pool_scale,
               q_a_norm, w_q_b, kv_a_norm, w_kv_b, w_out, ffn2_norm, ffn2_w_gate, ffn2_w_up, ffn2_w_down, final_norm]
    ms = [m_ada_w, m_ada_b, m_ffn1_norm, m_ffn1_w_gate, m_ffn1_w_up, m_ffn1_w_down, m_mix_norm, m_w_in, m_pool_w,
          m_pool_scale, m_q_a_norm, m_w_q_b, m_kv_a_norm, m_w_kv_b, m_w_out, m_ffn2_norm, m_ffn2_w_gate, m_ffn2_w_up,
          m_ffn2_w_down, m_final_norm]
    vs = [v_ada_w, v_ada_b, v_ffn1_norm, v_ffn1_w_gate, v_ffn1_w_up, v_ffn1_w_down, v_mix_norm, v_w_in, v_pool_w,
          v_pool_scale, v_q_a_norm, v_w_q_b, v_kv_a_norm, v_w_kv_b, v_w_out, v_ffn2_norm, v_ffn2_w_gate, v_ffn2_w_up,
          v_ffn2_w_down, v_final_norm]
    transposed = (3, 4, 7, 11, 16, 17)
    outs = [None] * len(weights)
    for i, (w, g, m, v) in enumerate(zip(weights, grads, ms, vs)):
        if g is not None:
            outs[i] = adamw(w, g.reshape(w.shape), m, v)
    big = [i for i, g in enumerate(grads) if g is None]

    def update(positions):
        filled = pair_fill_halves([reduced[t] for t in positions])
        for t, g in zip(positions, filled):
            i = big[t]
            if i in transposed:
                outs[i] = tuple(tr(o) for o in adamw(tr(weights[i]), g, tr(ms[i]), tr(vs[i]), copy_g=True))
            else:
                outs[i] = adamw(weights[i], g, ms[i], vs[i], copy_g=True)

    finish(stages[-3], outs[0][1])
    finish(stages[-2], outs[0][1])
    update(rest_pos)
    finish(stages[-1], outs[big[rest_pos[-1]]][1])
    update(ffn1_pos)
    return (loss, dx.reshape(x.shape), *[t[0] for t in outs], *[t[1] for t in outs], *[t[2] for t in outs],
            *[t[3] for t in outs])
```

```python
import math

import jax
import jax.numpy as jnp
from jax import lax
from jax.experimental import pallas as pl
from jax.experimental.pallas import tpu as pltpu

F32 = jnp.float32
BF16 = jnp.bfloat16
MESH = pl.DeviceIdType.MESH

EPS = 1e-6
ROPE_THETA = 10000.0
N_HEADS = 4
QK_NOPE = 128
QK_ROPE = 64
V_HEAD = 128
POOL_WINDOWS = (2, 4, 8, 16)
POOL_GC = 128
POOL_WIDTH = POOL_GC * len(POOL_WINDOWS)
Q_LORA = 384
KV_LORA = 256
SOFTMAX_SCALE = 1.0 / math.sqrt(QK_NOPE + QK_ROPE)
N_CHIPS = 4
N_DEV = 8

ADAM_LR = 0.001
ADAM_B1 = 0.9
ADAM_B2 = 0.999
ADAM_EPS = 1e-08
ADAM_WD = 0.01
ADAM_STEP = 10

ROW_TILE = 512
ATT_TILE = 512
VMEM_LIMIT = 56 * 1024 * 1024
BF16_ROWS = 16
LANES = 128


def _params(sem=None, vmem=VMEM_LIMIT):
    return pltpu.CompilerParams(dimension_semantics=sem, vmem_limit_bytes=vmem)


def _dot(a, b):
    return jnp.dot(a, b, preferred_element_type=F32)


def _dot_nt(a, b):
    return lax.dot_general(a, b, (((1,), (1,)), ((), ())), preferred_element_type=F32)


def _dot_tn(a, b):
    return lax.dot_general(a, b, (((0,), (0,)), ((), ())), preferred_element_type=F32)


def _dot_exact(t, perm):
    t1 = t.astype(BF16)
    r1 = t - t1.astype(F32)
    t2 = r1.astype(BF16)
    t3 = (r1 - t2.astype(F32)).astype(BF16)
    return _dot(t1, perm) + _dot(t2, perm) + _dot(t3, perm)


def _sum0(a):
    return jnp.sum(a, axis=0, keepdims=True)


def _rms(xt):
    r = lax.rsqrt(jnp.mean(xt * xt, axis=-1, keepdims=True) + EPS)
    return xt * r, r


def _rms_bwd(dy, xt, g):
    xhat, r = _rms(xt)
    dxhat = dy * g
    dx = r * (dxhat - xhat * jnp.mean(dxhat * xhat, axis=-1, keepdims=True))
    return dx, _sum0(dy * xhat)


def _normmod_bwd(dh, xt, gn, sc):
    xhat, _ = _rms(xt)
    dn = dh * (1.0 + sc)
    dx, dgn = _rms_bwd(dn, xt, gn)
    return dx, _sum0(dh), _sum0(dh * (xhat * gn)), dgn


def _row_tile(s):
    return min(s, ROW_TILE)


def _full(shape):
    n = len(shape)
    return pl.BlockSpec(shape, lambda *_: (0,) * n)


def _resident(shape):
    n = len(shape)
    return pl.BlockSpec(shape, lambda *_: (0,) * n, pipeline_mode=pl.Buffered(1))


def ffn_fwd(x, gn, sh, sc, gt, wg, wu, wd):
    s, d = x.shape
    k_chunks, fs, _ = wg.shape
    tm = _row_tile(s)

    def body(x_ref, gn_ref, sh_ref, sc_ref, gt_ref, wg_ref, wu_ref, wd_ref,
             xo_ref, h_ref, a_ref, sl_ref, dsu_ref, y_ref):
        xt = x_ref[...]
        xhat, _ = _rms(xt)
        h = (xhat * gn_ref[...] * (1.0 + sc_ref[...]) + sh_ref[...]).astype(BF16)
        h_ref[...] = h
        y = jnp.zeros((tm, d), F32)
        for k in range(k_chunks):
            gate = _dot_nt(h, wg_ref[k])
            up = _dot_nt(h, wu_ref[k])
            sg = jax.nn.sigmoid(gate)
            sl = gate * sg
            a = (sl * up).astype(BF16)
            a_ref[k] = a.T
            sl_ref[k] = sl.astype(BF16)
            dsu_ref[k] = (up * (sg * (1.0 + gate * (1.0 - sg)))).astype(BF16)
            y += _dot(a, wd_ref[k])
        y_ref[...] = y.astype(BF16)
        xo_ref[...] = xt + 0.5 * gt_ref[...] * y

    row = pl.BlockSpec((tm, d), lambda i: (i, 0))
    vec = pl.BlockSpec((1, d), lambda i: (0, 0))
    act = pl.BlockSpec((k_chunks, tm, fs), lambda i: (0, i, 0))
    act_shape = jax.ShapeDtypeStruct((k_chunks, s, fs), BF16)
    return pl.pallas_call(
        body, name="ffn_fwd",
        grid=(s // tm,),
        in_specs=[row, vec, vec, vec, vec, _resident(wg.shape), _resident(wu.shape), _resident(wd.shape)],
        out_specs=[row, row, pl.BlockSpec((k_chunks, fs, tm), lambda i: (0, 0, i)), act, act, row],
        out_shape=[jax.ShapeDtypeStruct((s, d), F32), jax.ShapeDtypeStruct((s, d), BF16),
                   jax.ShapeDtypeStruct((k_chunks, fs, s), BF16), act_shape, act_shape,
                   jax.ShapeDtypeStruct((s, d), BF16)],
        compiler_params=_params(("arbitrary",)),
    )(x, gn, sh, sc, gt, wg, wu, wd)


def ffn_bwd_act(dxn, sl, dsu, gt, wd):
    s, d = dxn.shape
    k_chunks, fs, _ = wd.shape
    tm = _row_tile(s)

    def body(dxn_ref, sl_ref, dsu_ref, gt_ref, wd_ref, dy_ref, dgate_ref, dup_ref):
        dy = (0.5 * gt_ref[...] * dxn_ref[...]).astype(BF16)
        dy_ref[...] = dy
        for k in range(k_chunks):
            da = _dot_nt(dy, wd_ref[k])
            dgate_ref[k] = (da * dsu_ref[k].astype(F32)).astype(BF16)
            dup_ref[k] = (da * sl_ref[k].astype(F32)).astype(BF16)

    row = pl.BlockSpec((tm, d), lambda i: (i, 0))
    act = pl.BlockSpec((k_chunks, tm, fs), lambda i: (0, i, 0))
    act_shape = jax.ShapeDtypeStruct((k_chunks, s, fs), BF16)
    return pl.pallas_call(
        body, name="ffn_bwd_act",
        grid=(s // tm,),
        in_specs=[row, act, act, pl.BlockSpec((1, d), lambda i: (0, 0)), _resident(wd.shape)],
        out_specs=[row, act, act],
        out_shape=[jax.ShapeDtypeStruct((s, d), BF16), act_shape, act_shape],
        compiler_params=_params(("arbitrary",)),
    )(dxn, sl, dsu, gt, wd)


def ffn_bwd_in(dxn, x, y, dgate, dup, gn, sc, wg, wu):
    s, d = x.shape
    k_chunks, fs, _ = wg.shape
    tm = _row_tile(s)

    def body(dxn_ref, x_ref, y_ref, dgate_ref, dup_ref, gn_ref, sc_ref, wg_ref, wu_ref, dx_ref, dvec_ref):
        i = pl.program_id(0)

        @pl.when(i == 0)
        def _():
            dvec_ref[...] = jnp.zeros_like(dvec_ref)

        dh = jnp.zeros((tm, d), F32)
        for k in range(k_chunks):
            dh += _dot(dgate_ref[k], wg_ref[k]) + _dot(dup_ref[k], wu_ref[k])
        dxn_t = dxn_ref[...]
        dx, dsh, dsc, dgn = _normmod_bwd(dh, x_ref[...], gn_ref[...], sc_ref[...])
        dx_ref[...] = dx + dxn_t
        dvec_ref[0:1, :] += dsh
        dvec_ref[1:2, :] += dsc
        dvec_ref[2:3, :] += _sum0(0.5 * dxn_t * y_ref[...].astype(F32))
        dvec_ref[3:4, :] += dgn

    row = pl.BlockSpec((tm, d), lambda i: (i, 0))
    vec = pl.BlockSpec((1, d), lambda i: (0, 0))
    act = pl.BlockSpec((k_chunks, tm, fs), lambda i: (0, i, 0))
    return pl.pallas_call(
        body, name="ffn_bwd_in",
        grid=(s // tm,),
        in_specs=[row, row, row, act, act, vec, vec, _resident(wg.shape), _resident(wu.shape)],
        out_specs=[row, pl.BlockSpec((8, d), lambda i: (0, 0))],
        out_shape=[jax.ShapeDtypeStruct((s, d), F32), jax.ShapeDtypeStruct((8, d), F32)],
        compiler_params=_params(("arbitrary",)),
    )(dxn, x, y, dgate, dup, gn, sc, wg, wu)


def _grad_mm(dot, a, b, a_spec, b_spec, g, m, n, chip, name):
    def body(c_ref, a_ref, b_ref, own_ref, all_ref):
        res = dot(a_ref[...], b_ref[...])
        all_ref[...] = res.astype(BF16)

        @pl.when(pl.program_id(0) == c_ref[0])
        def _():
            own_ref[...] = res

    return pl.pallas_call(
        body, name=name,
        grid_spec=pltpu.PrefetchScalarGridSpec(
            num_scalar_prefetch=1, grid=(g,), in_specs=[a_spec, b_spec],
            out_specs=[pl.BlockSpec((m, n), lambda gi, c: (0, 0)), pl.BlockSpec((None, m, n), lambda gi, c: (gi, 0, 0))]),
        out_shape=[jax.ShapeDtypeStruct((m, n), F32), jax.ShapeDtypeStruct((g, m, n), BF16)],
        compiler_params=_params(("arbitrary",)),
    )(chip, a, b)


def nn_mm(a_t, b, chip):
    g, m, s = a_t.shape
    n = b.shape[1]
    return _grad_mm(_dot, a_t, b, pl.BlockSpec((None, m, s), lambda gi, c: (gi, 0, 0)),
                    pl.BlockSpec((s, n), lambda gi, c: (0, 0)), g, m, n, chip, "nn_mm")


def tn_mm(a, b, chip):
    ga, s, m = a.shape
    gb, _, n = b.shape
    a_spec = pl.BlockSpec((None, s, m), (lambda gi, c: (gi, 0, 0)) if ga > 1 else (lambda gi, c: (0, 0, 0)))
    b_spec = pl.BlockSpec((None, s, n), (lambda gi, c: (gi, 0, 0)) if gb > 1 else (lambda gi, c: (0, 0, 0)))
    return _grad_mm(_dot_tn, a, b, a_spec, b_spec, max(ga, gb), m, n, chip, "tn_mm")


def mix_in_fwd(x, gn, sh, sc, w_in_t):
    s, d = x.shape
    tm = _row_tile(s)
    o1, o2, o3 = POOL_WIDTH, POOL_WIDTH + Q_LORA, POOL_WIDTH + Q_LORA + KV_LORA

    def body(x_ref, gn_ref, sh_ref, sc_ref, w_ref, h_ref, u_ref, cq_ref, ckv_ref, kr_ref):
        xhat, _ = _rms(x_ref[...])
        h = (xhat * gn_ref[...] * (1.0 + sc_ref[...]) + sh_ref[...]).astype(BF16)
        h_ref[...] = h
        z = _dot_nt(h, w_ref[0:o3, :])
        u_ref[...] = z[:, 0:o1]
        cq_ref[...] = z[:, o1:o2]
        ckv_ref[...] = z[:, o2:o3]
        kr_ref[...] = _dot_nt(h, w_ref[o3:, :])

    row = lambda w: pl.BlockSpec((tm, w), lambda i: (i, 0))
    vec = pl.BlockSpec((1, d), lambda i: (0, 0))
    return pl.pallas_call(
        body, name="mix_in_fwd",
        grid=(s // tm,),
        in_specs=[row(d), vec, vec, vec, _full(w_in_t.shape)],
        out_specs=[row(d), row(POOL_WIDTH), row(Q_LORA), row(KV_LORA), row(QK_ROPE)],
        out_shape=[jax.ShapeDtypeStruct((s, d), BF16), jax.ShapeDtypeStruct((s, POOL_WIDTH), F32),
                   jax.ShapeDtypeStruct((s, Q_LORA), F32), jax.ShapeDtypeStruct((s, KV_LORA), F32),
                   jax.ShapeDtypeStruct((s, QK_ROPE), F32)],
        compiler_params=_params(("arbitrary",)),
    )(x, gn, sh, sc, w_in_t)


def mix_in_bwd(dxn, du, dcq, dckv, dkr, x, gn, sc, w_in_t):
    s, d = x.shape
    tm = _row_tile(s)
    o1, o2, o3 = POOL_WIDTH, POOL_WIDTH + Q_LORA, POOL_WIDTH + Q_LORA + KV_LORA
    n_z = w_in_t.shape[0]

    def body(dxn_ref, du_ref, dcq_ref, dckv_ref, dkr_ref, x_ref, gn_ref, sc_ref, w_ref, dx_ref, dz_ref, dvec_ref):
        i = pl.program_id(0)

        @pl.when(i == 0)
        def _():
            dvec_ref[...] = jnp.zeros_like(dvec_ref)

        dub = du_ref[...].astype(BF16)
        dqb = dcq_ref[...].astype(BF16)
        dkb = dckv_ref[...].astype(BF16)
        drb = dkr_ref[...].astype(BF16)
        dz_ref[0:o1, :] = dub.T
        dz_ref[o1:o2, :] = dqb.T
        dz_ref[o2:o3, :] = dkb.T
        dz_ref[o3:, :] = drb.T
        dh = (_dot(dub, w_ref[0:o1, :]) + _dot(dqb, w_ref[o1:o2, :]) + _dot(dkb, w_ref[o2:o3, :])
              + _dot(drb, w_ref[o3:, :]))
        dx, dsh, dsc, dgn = _normmod_bwd(dh, x_ref[...], gn_ref[...], sc_ref[...])
        dx_ref[...] = dx + dxn_ref[...]
        dvec_ref[0:1, :] += dsh
        dvec_ref[1:2, :] += dsc
        dvec_ref[3:4, :] += dgn

    row = lambda w: pl.BlockSpec((tm, w), lambda i: (i, 0))
    vec = pl.BlockSpec((1, d), lambda i: (0, 0))
    return pl.pallas_call(
        body, name="mix_in_bwd",
        grid=(s // tm,),
        in_specs=[row(d), row(POOL_WIDTH), row(Q_LORA), row(KV_LORA), row(QK_ROPE), row(d), vec, vec,
                  _full(w_in_t.shape)],
        out_specs=[row(d), pl.BlockSpec((n_z, tm), lambda i: (0, i)), pl.BlockSpec((8, d), lambda i: (0, 0))],
        out_shape=[jax.ShapeDtypeStruct((s, d), F32), jax.ShapeDtypeStruct((n_z, s), BF16),
                   jax.ShapeDtypeStruct((8, d), F32)],
        compiler_params=_params(("arbitrary",)),
    )(dxn, du, dcq, dckv, dkr, x, gn, sc, w_in_t)


def _window_sum(a, w, rows, forward):
    s = a.shape[0]
    step = 1
    while step < w:
        if forward:
            shifted = jnp.where(rows < s - step, pltpu.roll(a, s - step, 0), 0.0)
        else:
            shifted = jnp.where(rows >= step, pltpu.roll(a, step, 0), 0.0)
        a = a + shifted
        step *= 2
    return a


def pool_fwd(u, pool_w, pool_scale):
    s = u.shape[0]

    def body(u_ref, w_ref, sc_ref, y_ref, diff_ref):
        rows = lax.broadcasted_iota(jnp.int32, (s, POOL_GC), 0)
        for g, w in enumerate(POOL_WINDOWS):
            cols = slice(g * POOL_GC, (g + 1) * POOL_GC)
            ug = u_ref[:, cols]
            cnt = jnp.minimum(rows + 1, w).astype(F32)
            diff = (_window_sum(ug, w, rows, False) / cnt - ug).astype(BF16)
            diff_ref[:, cols] = diff
            y_ref[:, cols] = _dot(diff, w_ref[g].astype(BF16)) * sc_ref[:, cols]

    return pl.pallas_call(
        body, name="pool_fwd",
        out_shape=[jax.ShapeDtypeStruct(u.shape, F32), jax.ShapeDtypeStruct(u.shape, BF16)],
        compiler_params=_params(),
    )(u, pool_w, pool_scale)


def pool_bwd(dy, diff, pool_w, pool_scale):
    s = dy.shape[0]

    def body(dy_ref, diff_ref, w_ref, sc_ref, du_ref, dw_ref, dsc_ref):
        rows = lax.broadcasted_iota(jnp.int32, (s, POOL_GC), 0)
        for g, w in enumerate(POOL_WINDOWS):
            cols = slice(g * POOL_GC, (g + 1) * POOL_GC)
            dyg = dy_ref[:, cols]
            diff = diff_ref[:, cols]
            wb = w_ref[g].astype(BF16)
            dsc_ref[:, cols] = _sum0(dyg * _dot(diff, wb))
            dys = (dyg * sc_ref[:, cols]).astype(BF16)
            dw_ref[g] = _dot_tn(diff, dys)
            ddiff = _dot_nt(dys, wb)
            cnt = jnp.minimum(rows + 1, w).astype(F32)
            du_ref[:, cols] = _window_sum(ddiff / cnt, w, rows, True) - ddiff

    return pl.pallas_call(
        body, name="pool_bwd",
        out_shape=[jax.ShapeDtypeStruct(dy.shape, F32), jax.ShapeDtypeStruct(pool_w.shape, F32),
                   jax.ShapeDtypeStruct(pool_scale.shape, F32)],
        compiler_params=_params(),
    )(dy, diff, pool_w, pool_scale)


def mla_qkv_fwd(cq, ckv, kr, qan, kvan, wq, wkv, cos, sin, rot):
    s = cq.shape[0]
    tm = _row_tile(s)

    def body(cq_ref, ckv_ref, kr_ref, qan_ref, kvan_ref, wq_ref, wkv_ref, cos_ref, sin_ref, rot_ref,
             q_ref, k_ref, v_ref, ql_ref, kvl_ref):
        cos_t = cos_ref[...]
        sin_t = sin_ref[...]
        perm = rot_ref[...]

        def rope(t):
            return t * cos_t + _dot_exact(t, perm) * sin_t

        qhat, _ = _rms(cq_ref[...])
        ql = (qhat * qan_ref[...]).astype(BF16)
        ql_ref[...] = ql
        khat, _ = _rms(ckv_ref[...])
        kvl = (khat * kvan_ref[...]).astype(BF16)
        kvl_ref[...] = kvl
        krr = rope(kr_ref[...]).astype(BF16)
        for h in range(N_HEADS):
            q = _dot_nt(ql, wq_ref[h])
            q_ref[h, :, 0:QK_NOPE] = q[:, 0:QK_NOPE].astype(BF16)
            q_ref[h, :, QK_NOPE:] = rope(q[:, QK_NOPE:]).astype(BF16)
            kv = _dot(kvl, wkv_ref[h])
            k_ref[h, :, 0:QK_NOPE] = kv[:, 0:QK_NOPE].astype(BF16)
            k_ref[h, :, QK_NOPE:] = krr
            v_ref[h] = kv[:, QK_NOPE:].astype(BF16)

    row = lambda w: pl.BlockSpec((tm, w), lambda i: (i, 0))
    hrow = lambda w: pl.BlockSpec((N_HEADS, tm, w), lambda i: (0, i, 0))
    qk = QK_NOPE + QK_ROPE
    return pl.pallas_call(
        body, name="mla_qkv_fwd",
        grid=(s // tm,),
        in_specs=[row(Q_LORA), row(KV_LORA), row(QK_ROPE), _full(qan.shape), _full(kvan.shape),
                  _full(wq.shape), _full(wkv.shape), row(QK_ROPE), row(QK_ROPE), _full(rot.shape)],
        out_specs=[hrow(qk), hrow(qk), hrow(V_HEAD), row(Q_LORA), row(KV_LORA)],
        out_shape=[jax.ShapeDtypeStruct((N_HEADS, s, qk), BF16), jax.ShapeDtypeStruct((N_HEADS, s, qk), BF16),
                   jax.ShapeDtypeStruct((N_HEADS, s, V_HEAD), BF16), jax.ShapeDtypeStruct((s, Q_LORA), BF16),
                   jax.ShapeDtypeStruct((s, KV_LORA), BF16)],
        compiler_params=_params(("arbitrary",)),
    )(cq, ckv, kr, qan, kvan, wq, wkv, cos, sin, rot)


def _attn_probs(q_ref, k_ref, qi, tq):
    n = (qi + 1) * tq
    rows = slice(qi * tq, n)
    sc = _dot_nt(q_ref[rows, :], k_ref[0:n, :]) * SOFTMAX_SCALE
    qpos = qi * tq + lax.broadcasted_iota(jnp.int32, (tq, n), 0)
    kpos = lax.broadcasted_iota(jnp.int32, (tq, n), 1)
    sc = jnp.where(qpos >= kpos, sc, -1e30)
    e = jnp.exp(sc - jnp.max(sc, axis=-1, keepdims=True))
    return e * (1.0 / jnp.sum(e, axis=-1, keepdims=True))


def attn_fwd(q, k, v):
    nh, s, qk = q.shape
    tq = min(s, ATT_TILE)

    def body(q_ref, k_ref, v_ref, o_ref):
        for qi in range(s // tq):
            n = (qi + 1) * tq
            p = _attn_probs(q_ref, k_ref, qi, tq).astype(BF16)
            o_ref[qi * tq:n, :] = _dot(p, v_ref[0:n, :])

    head = lambda w: pl.BlockSpec((None, s, w), lambda h: (h, 0, 0))
    return pl.pallas_call(
        body, name="attn_fwd",
        grid=(nh,),
        in_specs=[head(qk), head(qk), head(V_HEAD)],
        out_specs=pl.BlockSpec((s, V_HEAD), lambda h: (0, h)),
        out_shape=jax.ShapeDtypeStruct((s, nh * V_HEAD), F32),
        compiler_params=_params(("arbitrary",)),
    )(q, k, v)


def attn_bwd(q, k, v, do):
    nh, s, qk = q.shape
    tq = min(s, ATT_TILE)

    def body(q_ref, k_ref, v_ref, do_ref, dq_ref, dk_ref, dv_ref):
        dk_ref[...] = jnp.zeros_like(dk_ref)
        dv_ref[...] = jnp.zeros_like(dv_ref)
        for qi in range(s // tq):
            n = (qi + 1) * tq
            rows = slice(qi * tq, n)
            p = _attn_probs(q_ref, k_ref, qi, tq)
            dob = do_ref[rows, :].astype(BF16)
            dp = _dot_nt(dob, v_ref[0:n, :])
            ds = (p * (dp - jnp.sum(p * dp, axis=-1, keepdims=True)) * SOFTMAX_SCALE).astype(BF16)
            dq_ref[rows, :] = _dot(ds, k_ref[0:n, :])
            dk_ref[0:n, :] += _dot_tn(ds, q_ref[rows, :])
            dv_ref[0:n, :] += _dot_tn(p.astype(BF16), dob)

    head = lambda w: pl.BlockSpec((None, s, w), lambda h: (h, 0, 0))
    return pl.pallas_call(
        body, name="attn_bwd",
        grid=(nh,),
        in_specs=[head(qk), head(qk), head(V_HEAD), pl.BlockSpec((s, V_HEAD), lambda h: (0, h))],
        out_specs=[head(qk), head(qk), head(V_HEAD)],
        out_shape=[jax.ShapeDtypeStruct((nh, s, qk), F32), jax.ShapeDtypeStruct((nh, s, qk), F32),
                   jax.ShapeDtypeStruct((nh, s, V_HEAD), F32)],
        compiler_params=_params(("arbitrary",)),
    )(q, k, v, do)


def mla_qkv_bwd(dq, dk, dv, cq, ckv, qan, kvan, wq, wkv, cos, sin, rot_t):
    s = cq.shape[0]
    tm = _row_tile(s)

    def body(dq_ref, dk_ref, dv_ref, cq_ref, ckv_ref, qan_ref, kvan_ref,
             wq_ref, wkv_ref, cos_ref, sin_ref, rot_ref,
             dcq_ref, dckv_ref, dkro_ref, gq_ref, gkv_ref, dqan_ref, dkvan_ref):
        i = pl.program_id(0)

        @pl.when(i == 0)
        def _():
            dqan_ref[...] = jnp.zeros_like(dqan_ref)
            dkvan_ref[...] = jnp.zeros_like(dkvan_ref)

        cos_t = cos_ref[...]
        sin_t = sin_ref[...]
        perm_t = rot_ref[...]

        def unrope(t):
            return t * cos_t + _dot_exact(t * sin_t, perm_t)

        acc_q = jnp.zeros((tm, Q_LORA), F32)
        acc_kv = jnp.zeros((tm, KV_LORA), F32)
        dkr_sum = jnp.zeros((tm, QK_ROPE), F32)
        for h in range(N_HEADS):
            dq_h = dq_ref[h]
            a = dq_h[:, 0:QK_NOPE].astype(BF16)
            b = unrope(dq_h[:, QK_NOPE:]).astype(BF16)
            gq_ref[h, :, 0:QK_NOPE] = a
            gq_ref[h, :, QK_NOPE:] = b
            wq_h = wq_ref[h]
            acc_q += _dot(a, wq_h[0:QK_NOPE, :]) + _dot(b, wq_h[QK_NOPE:, :])
            dk_h = dk_ref[h]
            dk = dk_h[:, 0:QK_NOPE].astype(BF16)
            dvv = dv_ref[h].astype(BF16)
            gkv_ref[h, :, 0:QK_NOPE] = dk
            gkv_ref[h, :, QK_NOPE:] = dvv
            wkv_h = wkv_ref[h]
            acc_kv += _dot_nt(dk, wkv_h[:, 0:QK_NOPE]) + _dot_nt(dvv, wkv_h[:, QK_NOPE:])
            dkr_sum += dk_h[:, QK_NOPE:]
        dkro_ref[...] = unrope(dkr_sum)
        dcq, dqan = _rms_bwd(acc_q, cq_ref[...], qan_ref[...])
        dcq_ref[...] = dcq
        dqan_ref[...] += dqan
        dckv, dkvan = _rms_bwd(acc_kv, ckv_ref[...], kvan_ref[...])
        dckv_ref[...] = dckv
        dkvan_ref[...] += dkvan

    row = lambda w: pl.BlockSpec((tm, w), lambda i: (i, 0))
    hrow = lambda w: pl.BlockSpec((N_HEADS, tm, w), lambda i: (0, i, 0))
    return pl.pallas_call(
        body, name="mla_qkv_bwd",
        grid=(s // tm,),
        in_specs=[hrow(QK_NOPE + QK_ROPE), hrow(QK_NOPE + QK_ROPE), hrow(V_HEAD),
                  row(Q_LORA), row(KV_LORA), _full(qan.shape), _full(kvan.shape),
                  _full(wq.shape), _full(wkv.shape), row(QK_ROPE), row(QK_ROPE), _full(rot_t.shape)],
        out_specs=[row(Q_LORA), row(KV_LORA), row(QK_ROPE), hrow(QK_NOPE + QK_ROPE), hrow(QK_NOPE + V_HEAD),
                   _full(qan.shape), _full(kvan.shape)],
        out_shape=[jax.ShapeDtypeStruct((s, Q_LORA), F32), jax.ShapeDtypeStruct((s, KV_LORA), F32),
                   jax.ShapeDtypeStruct((s, QK_ROPE), F32),
                   jax.ShapeDtypeStruct((N_HEADS, s, QK_NOPE + QK_ROPE), BF16),
                   jax.ShapeDtypeStruct((N_HEADS, s, QK_NOPE + V_HEAD), BF16),
                   jax.ShapeDtypeStruct(qan.shape, F32), jax.ShapeDtypeStruct(kvan.shape, F32)],
        compiler_params=_params(("arbitrary",)),
    )(dq, dk, dv, cq, ckv, qan, kvan, wq, wkv, cos, sin, rot_t)


def out_proj_fwd(yp, om, w_out, x, gt):
    s, d = x.shape
    n_sh, rs, _ = w_out.shape
    tm = _row_tile(s)
    per = POOL_WIDTH // rs

    def body(yp_ref, om_ref, w_ref, x_ref, gt_ref, xo_ref, ycat_ref, y_ref):
        y = jnp.zeros((tm, d), F32)
        for j in range(n_sh):
            src = yp_ref if j < per else om_ref
            part = src[:, (j % per) * rs:(j % per + 1) * rs].astype(BF16)
            ycat_ref[j] = part.T
            y += _dot(part, w_ref[j])
        y_ref[...] = y.astype(BF16)
        xo_ref[...] = x_ref[...] + gt_ref[...] * y

    row = lambda w: pl.BlockSpec((tm, w), lambda i: (i, 0))
    return pl.pallas_call(
        body, name="out_proj_fwd",
        grid=(s // tm,),
        in_specs=[row(POOL_WIDTH), row(POOL_WIDTH), _full(w_out.shape), row(d), pl.BlockSpec((1, d), lambda i: (0, 0))],
        out_specs=[row(d), pl.BlockSpec((n_sh, rs, tm), lambda i: (0, 0, i)), row(d)],
        out_shape=[jax.ShapeDtypeStruct((s, d), F32), jax.ShapeDtypeStruct((n_sh, rs, s), BF16),
                   jax.ShapeDtypeStruct((s, d), BF16)],
        compiler_params=_params(("arbitrary",)),
    )(yp, om, w_out, x, gt)


def out_proj_bwd(dxn, y, gt, w_out):
    s, d = dxn.shape
    n_sh, rs, _ = w_out.shape
    tm = _row_tile(s)
    per = POOL_WIDTH // rs

    def body(dxn_ref, y_ref, gt_ref, w_ref, dy_ref, dyp_ref, dom_ref, dgt_ref):
        i = pl.program_id(0)

        @pl.when(i == 0)
        def _():
            dgt_ref[...] = jnp.zeros_like(dgt_ref)

        dxn_t = dxn_ref[...]
        dy = (gt_ref[...] * dxn_t).astype(BF16)
        dy_ref[...] = dy
        dgt_ref[...] += _sum0(dxn_t * y_ref[...].astype(F32))
        for j in range(n_sh):
            dst = dyp_ref if j < per else dom_ref
            dst[:, (j % per) * rs:(j % per + 1) * rs] = _dot_nt(dy, w_ref[j])

    row = lambda w: pl.BlockSpec((tm, w), lambda i: (i, 0))
    vec = pl.BlockSpec((1, d), lambda i: (0, 0))
    return pl.pallas_call(
        body, name="out_proj_bwd",
        grid=(s // tm,),
        in_specs=[row(d), row(d), vec, _full(w_out.shape)],
        out_specs=[row(d), row(POOL_WIDTH), row(POOL_WIDTH), vec],
        out_shape=[jax.ShapeDtypeStruct((s, d), BF16), jax.ShapeDtypeStruct((s, POOL_WIDTH), F32),
                   jax.ShapeDtypeStruct((s, POOL_WIDTH), F32), jax.ShapeDtypeStruct((1, d), F32)],
        compiler_params=_params(("arbitrary",)),
    )(dxn, y, gt, w_out)


def final_loss(x, gn, tgt):
    s, d = x.shape
    tm = _row_tile(s)

    def body(x_ref, gn_ref, t_ref, loss_ref, dx_ref, dgn_ref):
        i = pl.program_id(0)

        @pl.when(i == 0)
        def _():
            loss_ref[...] = jnp.zeros_like(loss_ref)
            dgn_ref[...] = jnp.zeros_like(dgn_ref)

        xt = x_ref[...]
        g = gn_ref[...]
        xhat, _ = _rms(xt)
        err = xhat * g - t_ref[...]
        per_tok = jnp.mean(err * err, axis=-1, keepdims=True)
        loss_ref[...] += jnp.broadcast_to(0.5 * _sum0(per_tok), loss_ref.shape)
        dx, dgn = _rms_bwd(err * (1.0 / d), xt, g)
        dx_ref[...] = dx
        dgn_ref[...] += dgn

    row = pl.BlockSpec((tm, d), lambda i: (i, 0))
    vec = pl.BlockSpec((1, d), lambda i: (0, 0))
    return pl.pallas_call(
        body, name="final_loss",
        grid=(s // tm,),
        in_specs=[row, vec, row],
        out_specs=[pl.BlockSpec((1, LANES), lambda i: (0, 0)), row, vec],
        out_shape=[jax.ShapeDtypeStruct((1, LANES), F32), jax.ShapeDtypeStruct((s, d), F32),
                   jax.ShapeDtypeStruct((1, d), F32)],
        compiler_params=_params(("arbitrary",)),
    )(x, gn, tgt)


def _col_tile(cols):
    return 768 if cols % 768 == 0 else cols


def ada_fwd(c16, ada_w, ada_b_loc):
    n_layers, d, cols = ada_w.shape
    tn = _col_tile(cols)

    def body(c_ref, w_ref, b_ref, o_ref):
        cv = c_ref[...]
        ca = (cv * jax.nn.sigmoid(cv)).astype(BF16)
        o_ref[...] = _dot(ca, w_ref[...].astype(BF16)) + b_ref[...]

    return pl.pallas_call(
        body, name="ada_fwd",
        grid=(n_layers, cols // tn),
        in_specs=[pl.BlockSpec((16, d), lambda l, j: (0, 0)), pl.BlockSpec((None, d, tn), lambda l, j: (l, 0, j)),
                  pl.BlockSpec((None, 1, tn), lambda l, j: (l, 0, j))],
        out_specs=pl.BlockSpec((None, 16, tn), lambda l, j: (l, 0, j)),
        out_shape=jax.ShapeDtypeStruct((n_layers, 16, cols), F32),
        compiler_params=_params(("arbitrary", "arbitrary")),
    )(c16, ada_w, ada_b_loc)


def ada_bwd(c16, dmod16):
    n_layers, _, cols = dmod16.shape
    d = c16.shape[1]
    tn = _col_tile(cols)

    def body(c_ref, g_ref, o_ref):
        cv = c_ref[...]
        ca = (cv * jax.nn.sigmoid(cv)).astype(BF16)
        o_ref[...] = _dot_tn(ca, g_ref[...].astype(BF16))

    return pl.pallas_call(
        body, name="ada_bwd",
        grid=(n_layers, cols // tn),
        in_specs=[pl.BlockSpec((16, d), lambda l, j: (0, 0)), pl.BlockSpec((None, 16, tn), lambda l, j: (l, 0, j))],
        out_specs=pl.BlockSpec((None, d, tn), lambda l, j: (l, 0, j)),
        out_shape=jax.ShapeDtypeStruct((n_layers, d, cols), F32),
        compiler_params=_params(("arbitrary", "arbitrary")),
    )(c16, dmod16)


def _as_rows(a):
    if a.ndim == 1:
        return a.reshape(1, a.shape[0])
    return a.reshape(-1, a.shape[-1])


def _rows_tile(r, c, itemsize=4, budget=2 * 1024 * 1024):
    if r * c * itemsize <= budget:
        return r
    best = None
    t = BF16_ROWS
    while t < r:
        if r % t == 0 and t * c * itemsize <= budget:
            best = t
        t += BF16_ROWS
    return best if best is not None else r


CAST_VMEM = 16 * 1024 * 1024


def cast_place(ws, chip, layers, after):
    _, r, c = ws[0].shape
    n_sel = len(layers)
    n_blk = len(ws) * n_sel
    tr = _rows_tile(r, c, budget=CAST_VMEM // (3 * n_blk))

    def body(chip_ref, *refs):
        for j in range(n_blk):
            refs[n_blk + 1 + j][...] = refs[j][...].astype(BF16)

    layer_spec = lambda l: pl.BlockSpec((None, tr, c), lambda i, ch: (l, i, 0))
    outs = pl.pallas_call(
        body, name="cast_place",
        grid_spec=pltpu.PrefetchScalarGridSpec(
            num_scalar_prefetch=1, grid=(r // tr,),
            in_specs=[layer_spec(l) for _ in ws for l in layers] + [pl.BlockSpec(memory_space=pl.ANY)],
            out_specs=[pl.BlockSpec((None, tr, c), lambda i, ch: (ch[0], i, 0))] * n_blk),
        out_shape=[jax.ShapeDtypeStruct((N_CHIPS, r, c), BF16)] * n_blk,
        compiler_params=_params(("arbitrary",)),
    )(chip, *[w for w in ws for _ in layers], after)
    return [list(outs[i * n_sel:(i + 1) * n_sel]) for i in range(len(ws))]


def adamw(w, g, m, v, copy_g=False):
    shape = w.shape
    w2, g2, m2, v2 = (_as_rows(t) for t in (w, g, m, v))
    r, c = w2.shape
    tr = _rows_tile(r, c, budget=3 * 1024 * 1024)
    c1 = 1.0 - ADAM_B1 ** ADAM_STEP
    c2 = 1.0 - ADAM_B2 ** ADAM_STEP

    def body(w_ref, g_ref, m_ref, v_ref, d_ref, mo_ref, vo_ref, *go_ref):
        gv = g_ref[...]
        if copy_g:
            go_ref[0][...] = gv
        mn = ADAM_B1 * m_ref[...] + (1.0 - ADAM_B1) * gv
        vn = ADAM_B2 * v_ref[...] + (1.0 - ADAM_B2) * (gv * gv)
        mo_ref[...] = mn
        vo_ref[...] = vn
        d_ref[...] = -ADAM_LR * ((mn / c1) / (jnp.sqrt(vn / c2) + ADAM_EPS) + ADAM_WD * w_ref[...])

    spec = pl.BlockSpec((tr, c), lambda i: (i, 0))
    n_out = 4 if copy_g else 3
    outs = pl.pallas_call(
        body, name="adamw", grid=(r // tr,), in_specs=[spec] * 4, out_specs=[spec] * n_out,
        out_shape=[jax.ShapeDtypeStruct((r, c), F32)] * n_out, compiler_params=_params(("arbitrary",)),
    )(w2, g2, m2, v2)
    g_out = outs[3] if copy_g else g2
    return tuple(o.reshape(shape) for o in (g_out,) + tuple(outs[:3]))


def sum_devices(a):
    n, r, c = a.shape
    tr = _rows_tile(r, c, budget=512 * 1024)

    def body(a_ref, o_ref):
        acc = a_ref[0]
        for j in range(1, n):
            acc = acc + a_ref[j]
        o_ref[...] = acc

    return pl.pallas_call(
        body, name="sum_devices", grid=(r // tr,),
        in_specs=[pl.BlockSpec((n, tr, c), lambda i: (0, i, 0))], out_specs=pl.BlockSpec((tr, c), lambda i: (i, 0)),
        out_shape=jax.ShapeDtypeStruct((r, c), F32), compiler_params=_params(("arbitrary",)),
    )(a)


def _split_axis(r, c):
    if (r // 2) % BF16_ROWS == 0 and r % 2 == 0:
        return 0
    assert c % (2 * LANES) == 0, (r, c)
    return 1


def _half_shape(r, c):
    return (r // 2, c) if _split_axis(r, c) == 0 else (r, c // 2)


def _half_at(ref, lead, which):
    r, c = ref.shape[-2:]
    if _split_axis(r, c) == 0:
        return ref.at[(*lead, pl.ds(which * (r // 2), r // 2), slice(None))]
    return ref.at[(*lead, slice(None), pl.ds(which * (c // 2), c // 2))]


def _half_spec(r, c, lead_block, imap):
    hr, hc = _half_shape(r, c)
    if _split_axis(r, c) == 0:
        return pl.BlockSpec((*lead_block, hr, hc), lambda *a: (*imap(*a)[0], imap(*a)[1], 0))
    return pl.BlockSpec((*lead_block, hr, hc), lambda *a: (*imap(*a)[0], 0, imap(*a)[1]))


def pair_add(owns, alls, ra_owns, ra_alls, sel):
    n = len(owns)
    n_sl = alls[0].shape[0]
    halves = [_half_shape(*g.shape) for g in owns]

    def body(s_ref, *refs):
        own_refs, all_refs, ra_own_refs, ra_all_refs, pb_refs, sum_refs = (refs[i * n:(i + 1) * n] for i in range(6))
        k = pl.program_id(0)
        for t in range(n):
            pb_refs[t][...] = (all_refs[t][...].astype(F32) + ra_all_refs[t][...].astype(F32)).astype(BF16)

            @pl.when(k == s_ref[1])
            def _(t=t):
                sum_refs[t][...] = own_refs[t][...] + ra_own_refs[t][...]

    slot = lambda hs: pl.BlockSpec((None,) + hs, lambda k, sr: (k, 0, 0))
    whole = lambda hs: pl.BlockSpec(hs, lambda k, sr: (0, 0))
    outs = pl.pallas_call(
        body, name="pair_add",
        grid_spec=pltpu.PrefetchScalarGridSpec(
            num_scalar_prefetch=1, grid=(n_sl,),
            in_specs=[_half_spec(*g.shape, (), lambda k, sr: ((), sr[0])) for g in owns]
            + [_half_spec(*g.shape[1:], (None,), lambda k, sr: ((k,), sr[0])) for g in alls]
            + [whole(hs) for hs in halves] + [slot(hs) for hs in halves],
            out_specs=[slot(hs) for hs in halves] + [whole(hs) for hs in halves]),
        out_shape=[jax.ShapeDtypeStruct((n_sl,) + hs, BF16) for hs in halves]
        + [jax.ShapeDtypeStruct(hs, F32) for hs in halves],
        compiler_params=_params(("arbitrary",)),
    )(sel, *owns, *alls, *ra_owns, *ra_alls)
    return list(outs[:n]), list(outs[n:])


def chip_sum(owns, rbs, sel, shapes, accs):
    n = len(owns)
    fresh = accs[0] is None

    def body(s_ref, *refs):
        own_refs, rb_refs, o_refs = refs[:n], refs[n:2 * n], refs[-n:]
        for t in range(n):
            acc_v = own_refs[t][...]
            for j in range(N_CHIPS - 1):
                acc_v = acc_v + rb_refs[t][j].astype(F32)
            o_refs[t][...] = acc_v

    in_specs = ([pl.BlockSpec(o.shape, lambda i, sr: (0, 0)) for o in owns]
                + [pl.BlockSpec(rb.shape, lambda i, sr: (0, 0, 0)) for rb in rbs])
    args = [sel, *owns, *rbs]
    aliases = {}
    if not fresh:
        in_specs += [pl.BlockSpec(memory_space=pl.ANY)] * n
        args += list(accs)
        aliases = {1 + 2 * n + t: t for t in range(n)}
    return list(pl.pallas_call(
        body, name="chip_sum",
        grid_spec=pltpu.PrefetchScalarGridSpec(
            num_scalar_prefetch=1, grid=(1,), in_specs=in_specs,
            out_specs=[_half_spec(*shp[1:], (None,), lambda i, sr: ((sr[2],), sr[0])) for shp in shapes]),
        out_shape=[jax.ShapeDtypeStruct(shp, F32) for shp in shapes],
        input_output_aliases=aliases,
        compiler_params=_params(("arbitrary",)),
    )(*args))


def _me():
    return lax.axis_index("x"), lax.axis_index("y"), lax.axis_index("c")


def _flip(v, bit):
    return 1 - v if bit else v


def exchange8(xs, bcast):
    blk = xs.shape if bcast else xs.shape[1:]

    def body(x_ref, o_ref, send_sems, recv_sems, loc_sem):
        mx, my, mc = _me()
        me = 4 * mx + 2 * my + mc
        src = (lambda j: x_ref) if bcast else (lambda j: x_ref.at[j])
        loc = pltpu.make_async_copy(src(me), o_ref.at[me], loc_sem)
        loc.start()
        copies = []
        for o in range(1, N_DEV):
            px, py, pc = _flip(mx, o & 4), _flip(my, o & 2), _flip(mc, o & 1)
            cp = pltpu.make_async_remote_copy(
                src_ref=src(4 * px + 2 * py + pc), dst_ref=o_ref.at[me],
                send_sem=send_sems.at[o - 1], recv_sem=recv_sems.at[o - 1],
                device_id=(px, py, pc), device_id_type=MESH)
            cp.start()
            copies.append(cp)
        for cp in copies:
            cp.wait()
        loc.wait()

    return pl.pallas_call(
        body, name="exchange8_gather" if bcast else "exchange8_a2a",
        in_specs=[pl.BlockSpec(memory_space=pltpu.VMEM)], out_specs=pl.BlockSpec(memory_space=pltpu.VMEM),
        out_shape=jax.ShapeDtypeStruct((N_DEV,) + tuple(blk), xs.dtype),
        scratch_shapes=[pltpu.SemaphoreType.DMA((N_DEV - 1,)), pltpu.SemaphoreType.DMA((N_DEV - 1,)), pltpu.SemaphoreType.DMA],
        compiler_params=_params(),
    )(xs)


HBM = pl.BlockSpec(memory_space=pltpu.HBM)
SEM = pl.BlockSpec(memory_space=pltpu.SEMAPHORE)
EFFECT = pltpu.SideEffectType.DATAFLOW_SIDE_EFFECTING


def _hbm(a):
    return pltpu.with_memory_space_constraint(a, pltpu.HBM)


def _ici_copy(land, o, send_sem, recv_sem, sending):
    mx, my, mc = _me()
    px, py = _flip(mx, o & 2), _flip(my, o & 1)
    mine = _half_at(land, (2 * mx + my,), mc)
    return pltpu.make_async_remote_copy(
        src_ref=mine, dst_ref=mine if sending else _half_at(land, (2 * px + py,), mc),
        send_sem=send_sem, recv_sem=recv_sem, device_id=(px, py, mc), device_id_type=MESH)


N_PEERS = N_CHIPS - 1
DMA_SEM = pltpu.SemaphoreType.DMA(())


def gather_start(lands, groups, after, tag):
    n_layers, n = len(lands), len(lands[0])
    flat = [a for layer in lands for a in layer]
    n_in = n * n_layers
    n_grp = len(groups)
    n_sem = 2 * n_layers * n_grp * N_PEERS
    first = lambda l, g, recv: ((l * n_grp + g) * 2 + recv) * N_PEERS

    def body(*refs):
        land = refs[:n_in]
        sems = refs[n_in + 1:n_in + 1 + n_sem]
        token = refs[-1]
        for l in range(n_layers):
            for g, members in enumerate(groups):
                for t in members:
                    for o in range(1, N_CHIPS):
                        _ici_copy(land[l * n + t], o, sems[first(l, g, 0) + o - 1], sems[first(l, g, 1) + o - 1],
                                  True).start()
        token[...] = jnp.zeros_like(token)

    outs = pl.pallas_call(
        body, name=f"gather_start_{tag}",
        in_specs=[HBM] * n_in + [pl.BlockSpec(memory_space=pl.ANY)],
        out_specs=[SEM] * n_sem + [HBM] * n_in + [pl.BlockSpec(memory_space=pltpu.VMEM)],
        out_shape=[DMA_SEM] * n_sem + [pltpu.HBM(a.shape, a.dtype) for a in flat]
        + [jax.ShapeDtypeStruct((8, LANES), F32)],
        input_output_aliases={i: i + n_sem for i in range(n_in)},
        compiler_params=pltpu.CompilerParams(has_side_effects=EFFECT),
    )(*[_hbm(a) for a in flat], after)
    sems = [[(list(outs[first(l, g, 0):first(l, g, 0) + N_PEERS]), list(outs[first(l, g, 1):first(l, g, 1) + N_PEERS]))
             for g in range(n_grp)] for l in range(n_layers)]
    lands_thru = [list(outs[n_sem + l * n:n_sem + (l + 1) * n]) for l in range(n_layers)]
    return sems, lands_thru, outs[-1]


def gather_wait(tag, sems, lands, after):
    n = len(lands)
    send_sems, recv_sems = sems

    def body(*refs):
        land = refs[:n]
        send_r = refs[n:n + N_PEERS]
        recv_r = refs[n + N_PEERS:n + 2 * N_PEERS]
        for t in range(n):
            for o in range(1, N_CHIPS):
                _ici_copy(land[t], o, send_r[o - 1], recv_r[o - 1], True).wait_send()
                _ici_copy(land[t], o, send_r[o - 1], recv_r[o - 1], False).wait_recv()

    return list(pl.pallas_call(
        body, name=f"gather_wait_{tag}",
        in_specs=[HBM] * n + [SEM] * (2 * N_PEERS) + [pl.BlockSpec(memory_space=pl.ANY)],
        out_specs=[HBM] * n,
        out_shape=[pltpu.HBM(a.shape, a.dtype) for a in lands],
        input_output_aliases={i: i for i in range(n)},
        compiler_params=pltpu.CompilerParams(has_side_effects=EFFECT),
    )(*lands, *send_sems, *recv_sems, after))


def gather_forward(lands):
    n = len(lands)

    def body(*refs):
        dst = refs[n:2 * n]
        send_sems, recv_sems = refs[2 * n:]
        mx, my, mc = _me()
        fwds = []
        for t in range(n):
            for o in range(1, N_CHIPS):
                slot = 2 * _flip(mx, o & 2) + _flip(my, o & 1)
                mine = _half_at(dst[t], (slot,), mc)
                theirs = _half_at(dst[t], (slot,), 1 - mc)
                cp = pltpu.make_async_remote_copy(
                    src_ref=mine, dst_ref=mine, send_sem=send_sems.at[t, o - 1], recv_sem=recv_sems.at[t, o - 1],
                    device_id=(mx, my, 1 - mc), device_id_type=MESH)
                cp.start()
                fwds.append((cp, pltpu.make_async_remote_copy(
                    src_ref=theirs, dst_ref=theirs, send_sem=send_sems.at[t, o - 1], recv_sem=recv_sems.at[t, o - 1],
                    device_id=(mx, my, 1 - mc), device_id_type=MESH)))
        for cp, arrival in fwds:
            cp.wait_send()
            arrival.wait_recv()

    any_spec = pl.BlockSpec(memory_space=pl.ANY)
    return list(pl.pallas_call(
        body, name="gather_forward",
        in_specs=[any_spec] * n, out_specs=[any_spec] * n,
        out_shape=[jax.ShapeDtypeStruct(a.shape, a.dtype) for a in lands],
        input_output_aliases={t: t for t in range(n)},
        scratch_shapes=[pltpu.SemaphoreType.DMA((n, N_CHIPS - 1)), pltpu.SemaphoreType.DMA((n, N_CHIPS - 1))],
        compiler_params=_params(),
    )(*lands))


def _scatter_copy(src, land, o, send_sem, recv_sem):
    mx, my, mc = _me()
    px, py = _flip(mx, o & 2), _flip(my, o & 1)
    return pltpu.make_async_remote_copy(
        src_ref=src.at[2 * px + py], dst_ref=land.at[o - 1],
        send_sem=send_sem, recv_sem=recv_sem, device_id=(px, py, mc), device_id_type=MESH)


def scatter_start(pbs, tag, after):
    n = len(pbs)
    lands = [lax.empty((N_CHIPS - 1,) + p.shape[1:], p.dtype) for p in pbs]

    def body(*refs):
        src = refs[:n]
        land = refs[n:2 * n]
        send_sems = refs[2 * n + 1:2 * n + 1 + N_PEERS]
        recv_sems = refs[2 * n + 1 + N_PEERS:2 * n + 1 + 2 * N_PEERS]
        token = refs[-1]
        for t in range(n):
            for o in range(1, N_CHIPS):
                _scatter_copy(src[t], land[t], o, send_sems[o - 1], recv_sems[o - 1]).start()
        token[...] = jnp.zeros_like(token)

    n_sem = 2 * N_PEERS
    arrs = list(pbs) + lands
    outs = pl.pallas_call(
        body, name=f"scatter_start_{tag}",
        in_specs=[HBM] * (2 * n) + [pl.BlockSpec(memory_space=pl.ANY)],
        out_specs=[SEM] * n_sem + [HBM] * (2 * n) + [pl.BlockSpec(memory_space=pltpu.VMEM)],
        out_shape=[DMA_SEM] * n_sem + [pltpu.HBM(a.shape, a.dtype) for a in arrs]
        + [jax.ShapeDtypeStruct((8, LANES), F32)],
        input_output_aliases={i: i + n_sem for i in range(2 * n)},
        compiler_params=pltpu.CompilerParams(has_side_effects=EFFECT),
    )(*[_hbm(a) for a in arrs], after)
    return (list(outs[:N_PEERS]), list(outs[N_PEERS:n_sem]), list(outs[n_sem:n_sem + n]),
            list(outs[n_sem + n:n_sem + 2 * n]), outs[-1])


def scatter_wait(tag, send_sems, recv_sems, pbs, lands, after):
    n = len(pbs)

    def body(*refs):
        src = refs[:n]
        land = refs[n:2 * n]
        send_r = refs[2 * n:2 * n + N_PEERS]
        recv_r = refs[2 * n + N_PEERS:2 * n + 2 * N_PEERS]
        for t in range(n):
            for o in range(1, N_CHIPS):
                cp = _scatter_copy(src[t], land[t], o, send_r[o - 1], recv_r[o - 1])
                cp.wait_send()
                cp.wait_recv()

    arrs = list(pbs) + list(lands)
    outs = pl.pallas_call(
        body, name=f"scatter_wait_{tag}",
        in_specs=[HBM] * (2 * n) + [SEM] * (2 * N_PEERS) + [pl.BlockSpec(memory_space=pl.ANY)],
        out_specs=[HBM] * (2 * n),
        out_shape=[pltpu.HBM(a.shape, a.dtype) for a in arrs],
        input_output_aliases={i: i for i in range(2 * n)},
        compiler_params=pltpu.CompilerParams(has_side_effects=EFFECT),
    )(*arrs, *send_sems, *recv_sems, after)
    return list(outs[n:])


def _pair_copies(srcs, lands, send_sem, recv_sem):
    mx, my, mc = _me()
    return [pltpu.make_async_remote_copy(
        src_ref=_half_at(src, (slice(None),) * (len(src.shape) - 2), 1 - mc), dst_ref=land,
        send_sem=send_sem, recv_sem=recv_sem, device_id=(mx, my, 1 - mc), device_id_type=MESH)
        for src, land in zip(srcs, lands)]


def pair_start(gs, tag, after):
    n = len(gs)
    lands = [lax.empty(g.shape[:-2] + _half_shape(*g.shape[-2:]), g.dtype) for g in gs]

    def body(*refs):
        send_sem, recv_sem = refs[2 * n + 1], refs[2 * n + 2]
        token = refs[-1]
        for cp in _pair_copies(refs[:n], refs[n:2 * n], send_sem, recv_sem):
            cp.start()
        token[...] = jnp.zeros_like(token)

    arrs = list(gs) + lands
    outs = pl.pallas_call(
        body, name=f"pair_start_{tag}",
        in_specs=[HBM] * (2 * n) + [pl.BlockSpec(memory_space=pl.ANY)],
        out_specs=[SEM, SEM] + [HBM] * (2 * n) + [pl.BlockSpec(memory_space=pltpu.VMEM)],
        out_shape=[DMA_SEM, DMA_SEM] + [pltpu.HBM(a.shape, a.dtype) for a in arrs] + [jax.ShapeDtypeStruct((8, LANES), F32)],
        input_output_aliases={i: i + 2 for i in range(2 * n)},
        compiler_params=pltpu.CompilerParams(has_side_effects=EFFECT),
    )(*[_hbm(a) for a in arrs], after)
    return outs[0], outs[1], list(outs[2:2 + n]), list(outs[2 + n:2 + 2 * n]), outs[-1]


def pair_wait(tag, send_sem, recv_sem, gs, lands, after):
    n = len(gs)

    def body(*refs):
        for cp in _pair_copies(refs[:n], refs[n:2 * n], refs[2 * n], refs[2 * n + 1]):
            cp.wait_send()
            cp.wait_recv()

    arrs = list(gs) + list(lands)
    outs = pl.pallas_call(
        body, name=f"pair_wait_{tag}",
        in_specs=[HBM] * (2 * n) + [SEM, SEM, pl.BlockSpec(memory_space=pl.ANY)],
        out_specs=[HBM] * (2 * n),
        out_shape=[pltpu.HBM(a.shape, a.dtype) for a in arrs],
        input_output_aliases={i: i for i in range(2 * n)},
        compiler_params=pltpu.CompilerParams(has_side_effects=EFFECT),
    )(*arrs, send_sem, recv_sem, after)
    return list(outs[:n]), list(outs[n:])


def _gather8_copy(x, land, o, send_sem, recv_sem, sending):
    mx, my, mc = _me()
    px, py, pc = _flip(mx, o & 4), _flip(my, o & 2), _flip(mc, o & 1)
    slot = 4 * mx + 2 * my + mc if sending else 4 * px + 2 * py + pc
    return pltpu.make_async_remote_copy(
        src_ref=x, dst_ref=land.at[slot], send_sem=send_sem, recv_sem=recv_sem,
        device_id=(px, py, pc), device_id_type=MESH)


def gather8_start(x, land, after, tag):
    n_peer = N_DEV - 1

    def body(x_ref, land_ref, after_ref, *rest):
        send_sems, recv_sems = rest[:n_peer], rest[n_peer:2 * n_peer]
        token = rest[-1]
        for o in range(1, N_DEV):
            _gather8_copy(x_ref, land_ref, o, send_sems[o - 1], recv_sems[o - 1], True).start()
        token[...] = jnp.zeros_like(token)

    outs = pl.pallas_call(
        body, name=f"gather8_start_{tag}",
        in_specs=[HBM, HBM, pl.BlockSpec(memory_space=pl.ANY)],
        out_specs=[SEM] * (2 * n_peer) + [HBM, HBM, pl.BlockSpec(memory_space=pltpu.VMEM)],
        out_shape=[DMA_SEM] * (2 * n_peer) + [pltpu.HBM(x.shape, x.dtype), pltpu.HBM(land.shape, land.dtype),
                                              jax.ShapeDtypeStruct((8, LANES), F32)],
        input_output_aliases={0: 2 * n_peer, 1: 2 * n_peer + 1},
        compiler_params=pltpu.CompilerParams(has_side_effects=EFFECT),
    )(_hbm(x), _hbm(land), after)
    return list(outs[:n_peer]), list(outs[n_peer:2 * n_peer]), outs[2 * n_peer], outs[2 * n_peer + 1], outs[-1]


def gather8_wait(tag, send_sems, recv_sems, x, land, after):
    n_peer = N_DEV - 1

    def body(x_ref, land_ref, *rest):
        send_r, recv_r = rest[:n_peer], rest[n_peer:2 * n_peer]
        for o in range(1, N_DEV):
            _gather8_copy(x_ref, land_ref, o, send_r[o - 1], recv_r[o - 1], True).wait_send()
            _gather8_copy(x_ref, land_ref, o, send_r[o - 1], recv_r[o - 1], False).wait_recv()

    return pl.pallas_call(
        body, name=f"gather8_wait_{tag}",
        in_specs=[HBM, HBM] + [SEM] * (2 * n_peer) + [pl.BlockSpec(memory_space=pl.ANY)],
        out_specs=[HBM, HBM],
        out_shape=[pltpu.HBM(x.shape, x.dtype), pltpu.HBM(land.shape, land.dtype)],
        input_output_aliases={0: 0, 1: 1},
        compiler_params=pltpu.CompilerParams(has_side_effects=EFFECT),
    )(x, land, *send_sems, *recv_sems, after)[1]


def pair_fill_halves(fs):
    n = len(fs)

    def body(*refs):
        dst = refs[n:2 * n]
        send_sems, recv_sems = refs[2 * n:]
        mx, my, mc = _me()
        copies = []
        for t in range(n):
            mine = _half_at(dst[t], (slice(None),), mc)
            theirs = _half_at(dst[t], (slice(None),), 1 - mc)
            cp = pltpu.make_async_remote_copy(
                src_ref=mine, dst_ref=mine, send_sem=send_sems.at[t], recv_sem=recv_sems.at[t],
                device_id=(mx, my, 1 - mc), device_id_type=MESH)
            cp.start()
            copies.append((cp, pltpu.make_async_remote_copy(
                src_ref=theirs, dst_ref=theirs, send_sem=send_sems.at[t], recv_sem=recv_sems.at[t],
                device_id=(mx, my, 1 - mc), device_id_type=MESH)))
        for cp, arrival in copies:
            cp.wait_send()
            arrival.wait_recv()

    any_spec = pl.BlockSpec(memory_space=pl.ANY)
    return pl.pallas_call(
        body, name="pair_fill_halves",
        in_specs=[any_spec] * n, out_specs=[any_spec] * n,
        out_shape=[jax.ShapeDtypeStruct(f.shape, f.dtype) for f in fs],
        input_output_aliases={t: t for t in range(n)},
        scratch_shapes=[pltpu.SemaphoreType.DMA((n,)), pltpu.SemaphoreType.DMA((n,))],
        compiler_params=_params(),
    )(*fs)


def _pack_rows(parts, d):
    rows, spans = [], []
    at = 0
    for p in parts:
        flat = p.reshape(-1)
        n_rows = -(-flat.shape[0] // (8 * d)) * 8
        flat = jnp.pad(flat, (0, n_rows * d - flat.shape[0]))
        rows.append(flat.reshape(n_rows, d))
        spans.append((at, p.shape))
        at += n_rows
    return jnp.concatenate(rows, axis=0), spans


def _unpack_rows(packed, spans):
    lead, d = packed.shape[:-2], packed.shape[-1]
    out = []
    for at, shape in spans:
        n = math.prod(shape)
        n_rows = -(-n // d)
        out.append(packed[..., at:at + n_rows, :].reshape(lead + (-1,))[..., :n].reshape(lead + tuple(shape)))
    return out


def _rotate_half_matrix():
    half = QK_ROPE // 2
    idx = jnp.arange(QK_ROPE)
    src = jnp.where(idx < half, idx + half, idx - half)
    sign = jnp.where(idx < half, -1.0, 1.0)
    return (jnp.zeros((QK_ROPE, QK_ROPE), F32).at[src, idx].set(sign)).astype(BF16)


def kernel(x, c, positions, ada_w, ada_b, ffn1_norm, ffn1_w_gate, ffn1_w_up, ffn1_w_down, mix_norm, w_in, pool_w, pool_scale, q_a_norm, w_q_b, kv_a_norm, w_kv_b, w_out, ffn2_norm, ffn2_w_gate, ffn2_w_up, ffn2_w_down, final_norm, loss_target, m_ada_w, m_ada_b, m_ffn1_norm, m_ffn1_w_gate, m_ffn1_w_up, m_ffn1_w_down, m_mix_norm, m_w_in, m_pool_w, m_pool_scale, m_q_a_norm, m_w_q_b, m_kv_a_norm, m_w_kv_b, m_w_out, m_ffn2_norm, m_ffn2_w_gate, m_ffn2_w_up, m_ffn2_w_down, m_final_norm, v_ada_w, v_ada_b, v_ffn1_norm, v_ffn1_w_gate, v_ffn1_w_up, v_ffn1_w_down, v_mix_norm, v_w_in, v_pool_w, v_pool_scale, v_q_a_norm, v_w_q_b, v_kv_a_norm, v_w_kv_b, v_w_out, v_ffn2_norm, v_ffn2_w_gate, v_ffn2_w_up, v_ffn2_w_down, v_final_norm):
    mx, my, mc = _me()
    chip = 2 * mx + my
    half = jnp.reshape(mc, (1,)).astype(jnp.int32)
    chip1 = jnp.reshape(chip, (1,)).astype(jnp.int32)
    n_layers, d, ada_cols = ada_w.shape
    xt = x[0]
    tgt = loss_target[0]

    inv_freq = 1.0 / (ROPE_THETA ** (jnp.arange(0, QK_ROPE, 2, dtype=F32) / QK_ROPE))
    ang = positions[0].astype(F32)[:, None] * inv_freq
    ang = jnp.concatenate([ang, ang], axis=-1)
    cos, sin = jnp.cos(ang), jnp.sin(ang)
    rot = _rotate_half_matrix()
    rot_t = rot.T

    c_all = exchange8(c, True).reshape(N_DEV, d)
    c16 = jnp.pad(c_all, ((0, 8), (0, 0)))
    ada_b_loc = lax.dynamic_slice_in_dim(ada_b, chip * ada_cols, ada_cols, axis=1).reshape(n_layers, 1, ada_cols)
    mod_part = ada_fwd(c16, ada_w, ada_b_loc)[:, :N_DEV]
    mod_got = exchange8(jnp.transpose(mod_part, (1, 0, 2)), False)
    mod = jnp.transpose(mod_got.reshape(N_CHIPS, 2, n_layers, ada_cols)[:, 0], (1, 0, 2))
    mod = mod.reshape(n_layers, 9, 1, d)

    tr = lambda a: jnp.transpose(a, (0, 2, 1))
    local = [tr(ffn1_w_gate), tr(ffn1_w_up), ffn1_w_down, tr(w_in), tr(w_q_b), w_kv_b, w_out,
             tr(ffn2_w_gate), tr(ffn2_w_up), ffn2_w_down]
    ffn1_pos, mixer_pos, ffn2_pos = (0, 1, 2), (3, 4, 5, 6), (7, 8, 9)
    rest_pos = mixer_pos + ffn2_pos

    def cast_all(layers, after):
        by_shape = {}
        for t, w in enumerate(local):
            by_shape.setdefault(w.shape, []).append(t)
        out = [None] * len(local)
        for ts in by_shape.values():
            for t, per_layer in zip(ts, cast_place([local[t] for t in ts], chip1, layers, after)):
                out[t] = per_layer
        return out

    placed = cast_all((0,), mod)
    g_sems, lands_fly, g_token = gather_start([[p[0] for p in placed]], (ffn1_pos, mixer_pos, ffn2_pos), mod, "first")
    if n_layers > 1:
        later = tuple(range(1, n_layers))
        placed = cast_all(later, g_token)
        more_sems, more_fly, g_token = gather_start(
            [[p[j] for p in placed] for j in range(len(later))], (ffn1_pos, rest_pos), g_token, "rest")
        g_sems, lands_fly = g_sems + more_sems, lands_fly + more_fly
    gathered = []

    row = lambda a, l: a[l].reshape(1, -1)
    saved = []
    for l in range(n_layers):
        def fetch(tag, group, members, after, l=l):
            return gather_forward(gather_wait(tag, g_sems[l][group], [lands_fly[l][t] for t in members], after))

        g1, u1, d1 = fetch(f"{l}a", 0, ffn1_pos, xt if l else g_token)
        sv = dict(x0=xt)
        xt, sv["h1"], sv["a1"], sv["sl1"], sv["dsu1"], sv["y1"] = ffn_fwd(
            xt, row(ffn1_norm, l), mod[l, 0], mod[l, 1], mod[l, 2], g1, u1, d1)
        sv["x1"] = xt
        if l == 0:
            win, wq, wkv, wout = fetch("0b", 1, mixer_pos, xt)
        else:
            win, wq, wkv, wout, g2, u2, d2 = fetch(f"{l}b", 1, rest_pos, xt)
        win = win.reshape(-1, d)
        sv["h2"], u, cq, ckv, kr = mix_in_fwd(xt, row(mix_norm, l), mod[l, 3], mod[l, 4], win)
        sv["cq"], sv["ckv"] = cq, ckv
        yp, sv["diff"] = pool_fwd(u, pool_w[l], row(pool_scale, l))
        qh, kh, vh, sv["ql"], sv["kvl"] = mla_qkv_fwd(
            cq, ckv, kr, row(q_a_norm, l), row(kv_a_norm, l), wq, wkv, cos, sin, rot)
        sv["qkv"] = (qh, kh, vh)
        om = attn_fwd(qh, kh, vh)
        xt, sv["ycat"], sv["y2"] = out_proj_fwd(yp, om, wout, xt, mod[l, 5])
        sv["x2"] = xt
        if l == 0:
            g2, u2, d2 = fetch("0c", 2, ffn2_pos, xt)
        gathered.append([g1, u1, d1, win, wq, wkv, wout, g2, u2, d2])
        xt, sv["h3"], sv["a3"], sv["sl3"], sv["dsu3"], sv["y3"] = ffn_fwd(
            xt, row(ffn2_norm, l), mod[l, 6], mod[l, 7], mod[l, 8], g2, u2, d2)
        saved.append(sv)

    loss_vec, dx, d_final_norm = final_loss(xt, final_norm.reshape(1, d), tgt)
    loss = lax.psum(loss_vec[0, 0], ("x", "y", "c"))

    none = [None] * n_layers
    dmods, dnorm1, dnorm2, dnorm3 = list(none), list(none), list(none), list(none)
    dpw, dps, dqan_l, dkvan_l = list(none), list(none), list(none), list(none)
    reduced = [None] * len(local)
    stages = []
    sel_of = lambda l: jnp.stack([mc, chip, jnp.asarray(l, mc.dtype)]).astype(jnp.int32)

    def to_chips(job, after_wait, after_start):
        send, recv, g_fly, lands_p = job.pop("pair")
        g_fly, got = pair_wait(job["tag"], send, recv, g_fly, lands_p, after_wait)
        n_w = len(job["pos"])
        pbs, job["owns"] = pair_add(g_fly[:n_w], g_fly[n_w:], got[:n_w], got[n_w:], sel_of(job["l"]))
        job["scatter"] = scatter_start(pbs, job["tag"], after_start)
        return job["scatter"][4][0, 0]

    def finish(job, after):
        s_send, s_recv, pbs_fly, lands_j, _ = job.pop("scatter")
        parts = scatter_wait(job["tag"], s_send, s_recv, pbs_fly, lands_j, after)
        sums = chip_sum(job["owns"], parts, sel_of(job["l"]), [(n_layers,) + shp for shp in job["shapes"]],
                        [reduced[t] for t in job["pos"]])
        for t, total_t in zip(job["pos"], sums):
            reduced[t] = total_t

    def checkpoint(tag, l, positions, grads_, done, before_scatter=None):
        send, recv, g_fly, lands_p, tok = pair_start([g[0] for g in grads_] + [g[1] for g in grads_], tag, done)
        order = tok[0, 0]
        if stages:
            order = order + to_chips(stages[-1], done, done if before_scatter is None else before_scatter)
        if len(stages) >= 3:
            finish(stages[-3], done)
        stages.append(dict(tag=tag, l=l, pos=positions, shapes=[g[0].shape for g in grads_],
                           pair=(send, recv, g_fly, lands_p)))
        return order

    def small_gather(tag, parts, after):
        packed, spans = _pack_rows(parts, d)
        land = lax.dynamic_update_index_in_dim(lax.empty((N_DEV,) + packed.shape, F32), packed, 4 * mx + 2 * my + mc, 0)
        return gather8_start(packed, land, after, tag), spans

    order = None

    for l in reversed(range(n_layers)):
        sv = saved[l]
        g1, u1, d1, win, wq, wkv, wout, g2, u2, d2 = gathered[l]
        win = win.reshape(-1, d)
        gt3 = mod[l, 8] if order is None else mod[l, 8] + order
        dy, dgt, dup = ffn_bwd_act(dx, sv["sl3"], sv["dsu3"], gt3, d2)
        dx, dvec3 = ffn_bwd_in(dx, sv["x2"], sv["y3"], dgt, dup, row(ffn2_norm, l), mod[l, 7], g2, u2)
        g_g2, g_u2, g_d2 = tn_mm(dgt, sv["h3"][None], chip1), tn_mm(dup, sv["h3"][None], chip1), nn_mm(sv["a3"], dy, chip1)
        dy2, dyp, dom, dg2 = out_proj_bwd(dx, sv["y2"], mod[l, 5], wout)
        g_wout = nn_mm(sv["ycat"], dy2, chip1)
        qh, kh, vh = sv["qkv"]
        dqh, dkh, dvh = attn_bwd(qh, kh, vh, dom)
        dcq, dckv, dkr_in, gq, gkv, dqan_l[l], dkvan_l[l] = mla_qkv_bwd(
            dqh, dkh, dvh, sv["cq"], sv["ckv"], row(q_a_norm, l), row(kv_a_norm, l), wq, wkv, cos, sin, rot_t)
        g_wq, g_wkv = tn_mm(gq, sv["ql"][None], chip1), tn_mm(sv["kvl"][None], gkv, chip1)
        du, dpw[l], dps[l] = pool_bwd(dyp, sv["diff"], pool_w[l], row(pool_scale, l))
        dx, dz, dvec2 = mix_in_bwd(dx, du, dcq, dckv, dkr_in, sv["x1"], row(mix_norm, l), mod[l, 4], win)
        g_win = nn_mm(dz.reshape(N_CHIPS, -1, dz.shape[1]), sv["h2"], chip1)
        dnorm2[l], dnorm3[l] = dvec2[3], dvec3[3]
        dmod_rest = jnp.concatenate([dvec2[0:2], dg2, dvec3[0:3]], axis=0)
        if l == 0:
            early = small_gather("early", [jnp.stack(dmods[1:]), dmod_rest, jnp.stack(dnorm1[1:]), jnp.stack(dnorm2),
                                           jnp.stack(dnorm3), d_final_norm, jnp.stack(dps), jnp.stack(dqan_l),
                                           jnp.stack(dkvan_l), jnp.stack(dpw)], dx)
        order = checkpoint(f"{l}a", l, rest_pos, [g_win, g_wq, g_wkv, g_wout, g_g2, g_u2, g_d2], dx,
                           early[0][4] if l == 0 else None)
        dy, dgt, dup = ffn_bwd_act(dx, sv["sl1"], sv["dsu1"], mod[l, 2] + order, d1)
        dx, dvec1 = ffn_bwd_in(dx, sv["x0"], sv["y1"], dgt, dup, row(ffn1_norm, l), mod[l, 1], g1, u1)
        g_g1, g_u1, g_d1 = tn_mm(dgt, sv["h1"][None], chip1), tn_mm(dup, sv["h1"][None], chip1), nn_mm(sv["a1"], dy, chip1)
        dmods[l] = jnp.concatenate([dvec1[0:3], dmod_rest], axis=0)
        dnorm1[l] = dvec1[3]
        if l == 0:
            late = small_gather("late", [dvec1[0:3], dvec1[3]], dx)
        order = checkpoint(f"{l}b", l, ffn1_pos, [g_g1, g_u1, g_d1], dx, late[0][4] if l == 0 else None)

    to_chips(stages[-1], stages[-2]["scatter"][4], stages[-2]["scatter"][4])
    sent = stages[-1]["scatter"][4]
    got_early = gather8_wait("early", *early[0][:4], sent)
    got_late = gather8_wait("late", *late[0][:4], sent)
    (g_dmod_rest, g_dmod0_rest, g_n1_rest, g_n2, g_n3, g_fn, g_ps, g_qan, g_kvan, g_pw) = _unpack_rows(
        sum_devices(got_early), early[1])
    g_dmod0_first, g_n1_first = _unpack_rows(sum_devices(got_late), late[1])
    g_ada_b = jnp.concatenate([jnp.concatenate([g_dmod0_first, g_dmod0_rest], axis=0)[None], g_dmod_rest], axis=0)
    g_n1 = jnp.concatenate([g_n1_first[None], g_n1_rest], axis=0)
    each_rest, each0_rest = _unpack_rows(got_early, early[1])[:2]
    each0_first = _unpack_rows(got_late, late[1])[0]
    dmod_all = jnp.concatenate([jnp.concatenate([each0_first, each0_rest], axis=1)[:, None], each_rest], axis=1)
    dmod_all = dmod_all.reshape(N_DEV, n_layers, 9 * d)
    dmod_loc = lax.dynamic_slice_in_dim(dmod_all, chip * ada_cols, ada_cols, axis=2)
    dmod16 = jnp.pad(jnp.transpose(dmod_loc, (1, 0, 2)), ((0, 0), (0, 8), (0, 0)))
    g_ada_w = ada_bwd(c16, dmod16)

    grads = [g_ada_w, g_ada_b, g_n1, None, None, None, g_n2, None, g_pw, g_ps, g_qan, None, g_kvan, None, None, g_n3,
             None, None, None, g_fn]
    weights = [ada_w, ada_b, ffn1_norm, ffn1_w_gate, ffn1_w_up, ffn1_w_down, mix_norm, w_in, pool_w, pool_scale,
               q_a_norm, w_q_b, kv_a_norm, w_kv_b, w_out, ffn2_norm, ffn2_w_gate, ffn2_w_up, ffn2_w_down, final_norm]
    ms = [m_ada_w, m_ada_b, m_ffn1_norm, m_ffn1_w_gate, m_ffn1_w_up, m_ffn1_w_down, m_mix_norm, m_w_in, m_pool_w,
          m_pool_scale, m_q_a_norm, m_w_q_b, m_kv_a_norm, m_w_kv_b, m_w_out, m_ffn2_norm, m_ffn2_w_gate, m_ffn2_w_up,
          m_ffn2_w_down, m_final_norm]
    vs = [v_ada_w, v_ada_b, v_ffn1_norm, v_ffn1_w_gate, v_ffn1_w_up, v_ffn1_w_down, v_mix_norm, v_w_in, v_pool_w,
          v_pool_scale, v_q_a_norm, v_w_q_b, v_kv_a_norm, v_w_kv_b, v_w_out, v_ffn2_norm, v_ffn2_w_gate, v_ffn2_w_up,
          v_ffn2_w_down, v_final_norm]
    transposed = (3, 4, 7, 11, 16, 17)
    outs = [None] * len(weights)
    for i, (w, g, m, v) in enumerate(zip(weights, grads, ms, vs)):
        if g is not None:
            outs[i] = adamw(w, g.reshape(w.shape), m, v)
    big = [i for i, g in enumerate(grads) if g is None]

    def update(positions):
        filled = pair_fill_halves([reduced[t] for t in positions])
        for t, g in zip(positions, filled):
            i = big[t]
            if i in transposed:
                outs[i] = tuple(tr(o) for o in adamw(tr(weights[i]), g, tr(ms[i]), tr(vs[i]), copy_g=True))
            else:
                outs[i] = adamw(weights[i], g, ms[i], vs[i], copy_g=True)

    finish(stages[-3], outs[0][1])
    finish(stages[-2], outs[0][1])
    update(rest_pos)
    finish(stages[-1], outs[big[rest_pos[-1]]][1])
    update(ffn1_pos)
    return (loss, dx.reshape(x.shape), *[t[0] for t in outs], *[t[1] for t in outs], *[t[2] for t in outs],
            *[t[3] for t in outs])
```

```python
import math

import jax
import jax.numpy as jnp
from jax import lax
from jax.experimental import pallas as pl
from jax.experimental.pallas import tpu as pltpu

F32 = jnp.float32
BF16 = jnp.bfloat16
MESH = pl.DeviceIdType.MESH

EPS = 1e-6
ROPE_THETA = 10000.0
N_HEADS = 4
QK_NOPE = 128
QK_ROPE = 64
V_HEAD = 128
POOL_WINDOWS = (2, 4, 8, 16)
POOL_GC = 128
POOL_WIDTH = POOL_GC * len(POOL_WINDOWS)
Q_LORA = 384
KV_LORA = 256
SOFTMAX_SCALE = 1.0 / math.sqrt(QK_NOPE + QK_ROPE)
N_CHIPS = 4
N_DEV = 8

ADAM_LR = 0.001
ADAM_B1 = 0.9
ADAM_B2 = 0.999
ADAM_EPS = 1e-08
ADAM_WD = 0.01
ADAM_STEP = 10

ROW_TILE = 512
ATT_TILE = 512
VMEM_LIMIT = 56 * 1024 * 1024
BF16_ROWS = 16
LANES = 128


def _params(sem=None, vmem=VMEM_LIMIT):
    return pltpu.CompilerParams(dimension_semantics=sem, vmem_limit_bytes=vmem)


def _dot(a, b):
    return jnp.dot(a, b, preferred_element_type=F32)


def _dot_nt(a, b):
    return lax.dot_general(a, b, (((1,), (1,)), ((), ())), preferred_element_type=F32)


def _dot_tn(a, b):
    return lax.dot_general(a, b, (((0,), (0,)), ((), ())), preferred_element_type=F32)


def _dot_exact(t, perm):
    t1 = t.astype(BF16)
    r1 = t - t1.astype(F32)
    t2 = r1.astype(BF16)
    t3 = (r1 - t2.astype(F32)).astype(BF16)
    return _dot(t1, perm) + _dot(t2, perm) + _dot(t3, perm)


def _sum0(a):
    return jnp.sum(a, axis=0, keepdims=True)


def _rms(xt):
    r = lax.rsqrt(jnp.mean(xt * xt, axis=-1, keepdims=True) + EPS)
    return xt * r, r


def _rms_bwd(dy, xt, g):
    xhat, r = _rms(xt)
    dxhat = dy * g
    dx = r * (dxhat - xhat * jnp.mean(dxhat * xhat, axis=-1, keepdims=True))
    return dx, _sum0(dy * xhat)


def _normmod_bwd(dh, xt, gn, sc):
    xhat, _ = _rms(xt)
    dn = dh * (1.0 + sc)
    dx, dgn = _rms_bwd(dn, xt, gn)
    return dx, _sum0(dh), _sum0(dh * (xhat * gn)), dgn


def _row_tile(s):
    return min(s, ROW_TILE)


def _full(shape):
    n = len(shape)
    return pl.BlockSpec(shape, lambda *_: (0,) * n)


def _resident(shape):
    n = len(shape)
    return pl.BlockSpec(shape, lambda *_: (0,) * n, pipeline_mode=pl.Buffered(1))


def ffn_fwd(x, gn, sh, sc, gt, wg, wu, wd):
    s, d = x.shape
    k_chunks, fs, _ = wg.shape
    tm = _row_tile(s)

    def body(x_ref, gn_ref, sh_ref, sc_ref, gt_ref, wg_ref, wu_ref, wd_ref,
             xo_ref, h_ref, a_ref, sl_ref, dsu_ref, y_ref):
        xt = x_ref[...]
        xhat, _ = _rms(xt)
        h = (xhat * gn_ref[...] * (1.0 + sc_ref[...]) + sh_ref[...]).astype(BF16)
        h_ref[...] = h
        y = jnp.zeros((tm, d), F32)
        for k in range(k_chunks):
            gate = _dot_nt(h, wg_ref[k])
            up = _dot_nt(h, wu_ref[k])
            sg = jax.nn.sigmoid(gate)
            sl = gate * sg
            a = (sl * up).astype(BF16)
            a_ref[k] = a.T
            sl_ref[k] = sl.astype(BF16)
            dsu_ref[k] = (up * (sg * (1.0 + gate * (1.0 - sg)))).astype(BF16)
            y += _dot(a, wd_ref[k])
        y_ref[...] = y.astype(BF16)
        xo_ref[...] = xt + 0.5 * gt_ref[...] * y

    row = pl.BlockSpec((tm, d), lambda i: (i, 0))
    vec = pl.BlockSpec((1, d), lambda i: (0, 0))
    act = pl.BlockSpec((k_chunks, tm, fs), lambda i: (0, i, 0))
    act_shape = jax.ShapeDtypeStruct((k_chunks, s, fs), BF16)
    return pl.pallas_call(
        body, name="ffn_fwd",
        grid=(s // tm,),
        in_specs=[row, vec, vec, vec, vec, _resident(wg.shape), _resident(wu.shape), _resident(wd.shape)],
        out_specs=[row, row, pl.BlockSpec((k_chunks, fs, tm), lambda i: (0, 0, i)), act, act, row],
        out_shape=[jax.ShapeDtypeStruct((s, d), F32), jax.ShapeDtypeStruct((s, d), BF16),
                   jax.ShapeDtypeStruct((k_chunks, fs, s), BF16), act_shape, act_shape,
                   jax.ShapeDtypeStruct((s, d), BF16)],
        compiler_params=_params(("arbitrary",)),
    )(x, gn, sh, sc, gt, wg, wu, wd)


def ffn_bwd_act(dxn, sl, dsu, gt, wd):
    s, d = dxn.shape
    k_chunks, fs, _ = wd.shape
    tm = _row_tile(s)

    def body(dxn_ref, sl_ref, dsu_ref, gt_ref, wd_ref, dy_ref, dgate_ref, dup_ref):
        dy = (0.5 * gt_ref[...] * dxn_ref[...]).astype(BF16)
        dy_ref[...] = dy
        for k in range(k_chunks):
            da = _dot_nt(dy, wd_ref[k])
            dgate_ref[k] = (da * dsu_ref[k].astype(F32)).astype(BF16)
            dup_ref[k] = (da * sl_ref[k].astype(F32)).astype(BF16)

    row = pl.BlockSpec((tm, d), lambda i: (i, 0))
    act = pl.BlockSpec((k_chunks, tm, fs), lambda i: (0, i, 0))
    act_shape = jax.ShapeDtypeStruct((k_chunks, s, fs), BF16)
    return pl.pallas_call(
        body, name="ffn_bwd_act",
        grid=(s // tm,),
        in_specs=[row, act, act, pl.BlockSpec((1, d), lambda i: (0, 0)), _resident(wd.shape)],
        out_specs=[row, act, act],
        out_shape=[jax.ShapeDtypeStruct((s, d), BF16), act_shape, act_shape],
        compiler_params=_params(("arbitrary",)),
    )(dxn, sl, dsu, gt, wd)


def ffn_bwd_in(dxn, x, y, dgate, dup, gn, sc, wg, wu):
    s, d = x.shape
    k_chunks, fs, _ = wg.shape
    tm = _row_tile(s)

    def body(dxn_ref, x_ref, y_ref, dgate_ref, dup_ref, gn_ref, sc_ref, wg_ref, wu_ref, dx_ref, dvec_ref):
        i = pl.program_id(0)

        @pl.when(i == 0)
        def _():
            dvec_ref[...] = jnp.zeros_like(dvec_ref)

        dh = jnp.zeros((tm, d), F32)
        for k in range(k_chunks):
            dh += _dot(dgate_ref[k], wg_ref[k]) + _dot(dup_ref[k], wu_ref[k])
        dxn_t = dxn_ref[...]
        dx, dsh, dsc, dgn = _normmod_bwd(dh, x_ref[...], gn_ref[...], sc_ref[...])
        dx_ref[...] = dx + dxn_t
        dvec_ref[0:1, :] += dsh
        dvec_ref[1:2, :] += dsc
        dvec_ref[2:3, :] += _sum0(0.5 * dxn_t * y_ref[...].astype(F32))
        dvec_ref[3:4, :] += dgn

    row = pl.BlockSpec((tm, d), lambda i: (i, 0))
    vec = pl.BlockSpec((1, d), lambda i: (0, 0))
    act = pl.BlockSpec((k_chunks, tm, fs), lambda i: (0, i, 0))
    return pl.pallas_call(
        body, name="ffn_bwd_in",
        grid=(s // tm,),
        in_specs=[row, row, row, act, act, vec, vec, _resident(wg.shape), _resident(wu.shape)],
        out_specs=[row, pl.BlockSpec((8, d), lambda i: (0, 0))],
        out_shape=[jax.ShapeDtypeStruct((s, d), F32), jax.ShapeDtypeStruct((8, d), F32)],
        compiler_params=_params(("arbitrary",)),
    )(dxn, x, y, dgate, dup, gn, sc, wg, wu)


def _grad_mm(dot, a, b, a_spec, b_spec, g, m, n, chip, name):
    def body(c_ref, a_ref, b_ref, own_ref, all_ref):
        res = dot(a_ref[...], b_ref[...])
        all_ref[...] = res.astype(BF16)

        @pl.when(pl.program_id(0) == c_ref[0])
        def _():
            own_ref[...] = res

    return pl.pallas_call(
        body, name=name,
        grid_spec=pltpu.PrefetchScalarGridSpec(
            num_scalar_prefetch=1, grid=(g,), in_specs=[a_spec, b_spec],
            out_specs=[pl.BlockSpec((m, n), lambda gi, c: (0, 0)), pl.BlockSpec((None, m, n), lambda gi, c: (gi, 0, 0))]),
        out_shape=[jax.ShapeDtypeStruct((m, n), F32), jax.ShapeDtypeStruct((g, m, n), BF16)],
        compiler_params=_params(("arbitrary",)),
    )(chip, a, b)


def nn_mm(a_t, b, chip):
    g, m, s = a_t.shape
    n = b.shape[1]
    return _grad_mm(_dot, a_t, b, pl.BlockSpec((None, m, s), lambda gi, c: (gi, 0, 0)),
                    pl.BlockSpec((s, n), lambda gi, c: (0, 0)), g, m, n, chip, "nn_mm")


def tn_mm(a, b, chip):
    ga, s, m = a.shape
    gb, _, n = b.shape
    a_spec = pl.BlockSpec((None, s, m), (lambda gi, c: (gi, 0, 0)) if ga > 1 else (lambda gi, c: (0, 0, 0)))
    b_spec = pl.BlockSpec((None, s, n), (lambda gi, c: (gi, 0, 0)) if gb > 1 else (lambda gi, c: (0, 0, 0)))
    return _grad_mm(_dot_tn, a, b, a_spec, b_spec, max(ga, gb), m, n, chip, "tn_mm")


def mix_in_fwd(x, gn, sh, sc, w_in_t):
    s, d = x.shape
    tm = _row_tile(s)
    o1, o2, o3 = POOL_WIDTH, POOL_WIDTH + Q_LORA, POOL_WIDTH + Q_LORA + KV_LORA

    def body(x_ref, gn_ref, sh_ref, sc_ref, w_ref, h_ref, u_ref, cq_ref, ckv_ref, kr_ref):
        xhat, _ = _rms(x_ref[...])
        h = (xhat * gn_ref[...] * (1.0 + sc_ref[...]) + sh_ref[...]).astype(BF16)
        h_ref[...] = h
        z = _dot_nt(h, w_ref[0:o3, :])
        u_ref[...] = z[:, 0:o1]
        cq_ref[...] = z[:, o1:o2]
        ckv_ref[...] = z[:, o2:o3]
        kr_ref[...] = _dot_nt(h, w_ref[o3:, :])

    row = lambda w: pl.BlockSpec((tm, w), lambda i: (i, 0))
    vec = pl.BlockSpec((1, d), lambda i: (0, 0))
    return pl.pallas_call(
        body, name="mix_in_fwd",
        grid=(s // tm,),
        in_specs=[row(d), vec, vec, vec, _full(w_in_t.shape)],
        out_specs=[row(d), row(POOL_WIDTH), row(Q_LORA), row(KV_LORA), row(QK_ROPE)],
        out_shape=[jax.ShapeDtypeStruct((s, d), BF16), jax.ShapeDtypeStruct((s, POOL_WIDTH), F32),
                   jax.ShapeDtypeStruct((s, Q_LORA), F32), jax.ShapeDtypeStruct((s, KV_LORA), F32),
                   jax.ShapeDtypeStruct((s, QK_ROPE), F32)],
        compiler_params=_params(("arbitrary",)),
    )(x, gn, sh, sc, w_in_t)


def mix_in_bwd(dxn, du, dcq, dckv, dkr, x, gn, sc, w_in_t):
    s, d = x.shape
    tm = _row_tile(s)
    o1, o2, o3 = POOL_WIDTH, POOL_WIDTH + Q_LORA, POOL_WIDTH + Q_LORA + KV_LORA
    n_z = w_in_t.shape[0]

    def body(dxn_ref, du_ref, dcq_ref, dckv_ref, dkr_ref, x_ref, gn_ref, sc_ref, w_ref, dx_ref, dz_ref, dvec_ref):
        i = pl.program_id(0)

        @pl.when(i == 0)
        def _():
            dvec_ref[...] = jnp.zeros_like(dvec_ref)

        dub = du_ref[...].astype(BF16)
        dqb = dcq_ref[...].astype(BF16)
        dkb = dckv_ref[...].astype(BF16)
        drb = dkr_ref[...].astype(BF16)
        dz_ref[0:o1, :] = dub.T
        dz_ref[o1:o2, :] = dqb.T
        dz_ref[o2:o3, :] = dkb.T
        dz_ref[o3:, :] = drb.T
        dh = (_dot(dub, w_ref[0:o1, :]) + _dot(dqb, w_ref[o1:o2, :]) + _dot(dkb, w_ref[o2:o3, :])
              + _dot(drb, w_ref[o3:, :]))
        dx, dsh, dsc, dgn = _normmod_bwd(dh, x_ref[...], gn_ref[...], sc_ref[...])
        dx_ref[...] = dx + dxn_ref[...]
        dvec_ref[0:1, :] += dsh
        dvec_ref[1:2, :] += dsc
        dvec_ref[3:4, :] += dgn

    row = lambda w: pl.BlockSpec((tm, w), lambda i: (i, 0))
    vec = pl.BlockSpec((1, d), lambda i: (0, 0))
    return pl.pallas_call(
        body, name="mix_in_bwd",
        grid=(s // tm,),
        in_specs=[row(d), row(POOL_WIDTH), row(Q_LORA), row(KV_LORA), row(QK_ROPE), row(d), vec, vec,
                  _full(w_in_t.shape)],
        out_specs=[row(d), pl.BlockSpec((n_z, tm), lambda i: (0, i)), pl.BlockSpec((8, d), lambda i: (0, 0))],
        out_shape=[jax.ShapeDtypeStruct((s, d), F32), jax.ShapeDtypeStruct((n_z, s), BF16),
                   jax.ShapeDtypeStruct((8, d), F32)],
        compiler_params=_params(("arbitrary",)),
    )(dxn, du, dcq, dckv, dkr, x, gn, sc, w_in_t)


def _window_sum(a, w, rows, forward):
    s = a.shape[0]
    step = 1
    while step < w:
        if forward:
            shifted = jnp.where(rows < s - step, pltpu.roll(a, s - step, 0), 0.0)
        else:
            shifted = jnp.where(rows >= step, pltpu.roll(a, step, 0), 0.0)
        a = a + shifted
        step *= 2
    return a


def pool_fwd(u, pool_w, pool_scale):
    s = u.shape[0]

    def body(u_ref, w_ref, sc_ref, y_ref, diff_ref):
        rows = lax.broadcasted_iota(jnp.int32, (s, POOL_GC), 0)
        for g, w in enumerate(POOL_WINDOWS):
            cols = slice(g * POOL_GC, (g + 1) * POOL_GC)
            ug = u_ref[:, cols]
            cnt = jnp.minimum(rows + 1, w).astype(F32)
            diff = (_window_sum(ug, w, rows, False) / cnt - ug).astype(BF16)
            diff_ref[:, cols] = diff
            y_ref[:, cols] = _dot(diff, w_ref[g].astype(BF16)) * sc_ref[:, cols]

    return pl.pallas_call(
        body, name="pool_fwd",
        out_shape=[jax.ShapeDtypeStruct(u.shape, F32), jax.ShapeDtypeStruct(u.shape, BF16)],
        compiler_params=_params(),
    )(u, pool_w, pool_scale)


def pool_bwd(dy, diff, pool_w, pool_scale):
    s = dy.shape[0]

    def body(dy_ref, diff_ref, w_ref, sc_ref, du_ref, dw_ref, dsc_ref):
        rows = lax.broadcasted_iota(jnp.int32, (s, POOL_GC), 0)
        for g, w in enumerate(POOL_WINDOWS):
            cols = slice(g * POOL_GC, (g + 1) * POOL_GC)
            dyg = dy_ref[:, cols]
            diff = diff_ref[:, cols]
            wb = w_ref[g].astype(BF16)
            dsc_ref[:, cols] = _sum0(dyg * _dot(diff, wb))
            dys = (dyg * sc_ref[:, cols]).astype(BF16)
            dw_ref[g] = _dot_tn(diff, dys)
            ddiff = _dot_nt(dys, wb)
            cnt = jnp.minimum(rows + 1, w).astype(F32)
            du_ref[:, cols] = _window_sum(ddiff / cnt, w, rows, True) - ddiff

    return pl.pallas_call(
        body, name="pool_bwd",
        out_shape=[jax.ShapeDtypeStruct(dy.shape, F32), jax.ShapeDtypeStruct(pool_w.shape, F32),
                   jax.ShapeDtypeStruct(pool_scale.shape, F32)],
        compiler_params=_params(),
    )(dy, diff, pool_w, pool_scale)


def mla_qkv_fwd(cq, ckv, kr, qan, kvan, wq, wkv, cos, sin, rot):
    s = cq.shape[0]
    tm = _row_tile(s)

    def body(cq_ref, ckv_ref, kr_ref, qan_ref, kvan_ref, wq_ref, wkv_ref, cos_ref, sin_ref, rot_ref,
             q_ref, k_ref, v_ref, ql_ref, kvl_ref):
        cos_t = cos_ref[...]
        sin_t = sin_ref[...]
        perm = rot_ref[...]

        def rope(t):
            return t * cos_t + _dot_exact(t, perm) * sin_t

        qhat, _ = _rms(cq_ref[...])
        ql = (qhat * qan_ref[...]).astype(BF16)
        ql_ref[...] = ql
        khat, _ = _rms(ckv_ref[...])
        kvl = (khat * kvan_ref[...]).astype(BF16)
        kvl_ref[...] = kvl
        krr = rope(kr_ref[...]).astype(BF16)
        for h in range(N_HEADS):
            q = _dot_nt(ql, wq_ref[h])
            q_ref[h, :, 0:QK_NOPE] = q[:, 0:QK_NOPE].astype(BF16)
            q_ref[h, :, QK_NOPE:] = rope(q[:, QK_NOPE:]).astype(BF16)
            kv = _dot(kvl, wkv_ref[h])
            k_ref[h, :, 0:QK_NOPE] = kv[:, 0:QK_NOPE].astype(BF16)
            k_ref[h, :, QK_NOPE:] = krr
            v_ref[h] = kv[:, QK_NOPE:].astype(BF16)

    row = lambda w: pl.BlockSpec((tm, w), lambda i: (i, 0))
    hrow = lambda w: pl.BlockSpec((N_HEADS, tm, w), lambda i: (0, i, 0))
    qk = QK_NOPE + QK_ROPE
    return pl.pallas_call(
        body, name="mla_qkv_fwd",
        grid=(s // tm,),
        in_specs=[row(Q_LORA), row(KV_LORA), row(QK_ROPE), _full(qan.shape), _full(kvan.shape),
                  _full(wq.shape), _full(wkv.shape), row(QK_ROPE), row(QK_ROPE), _full(rot.shape)],
        out_specs=[hrow(qk), hrow(qk), hrow(V_HEAD), row(Q_LORA), row(KV_LORA)],
        out_shape=[jax.ShapeDtypeStruct((N_HEADS, s, qk), BF16), jax.ShapeDtypeStruct((N_HEADS, s, qk), BF16),
                   jax.ShapeDtypeStruct((N_HEADS, s, V_HEAD), BF16), jax.ShapeDtypeStruct((s, Q_LORA), BF16),
                   jax.ShapeDtypeStruct((s, KV_LORA), BF16)],
        compiler_params=_params(("arbitrary",)),
    )(cq, ckv, kr, qan, kvan, wq, wkv, cos, sin, rot)


def _attn_probs(q_ref, k_ref, qi, tq):
    n = (qi + 1) * tq
    rows = slice(qi * tq, n)
    sc = _dot_nt(q_ref[rows, :], k_ref[0:n, :]) * SOFTMAX_SCALE
    qpos = qi * tq + lax.broadcasted_iota(jnp.int32, (tq, n), 0)
    kpos = lax.broadcasted_iota(jnp.int32, (tq, n), 1)
    sc = jnp.where(qpos >= kpos, sc, -1e30)
    e = jnp.exp(sc - jnp.max(sc, axis=-1, keepdims=True))
    return e * (1.0 / jnp.sum(e, axis=-1, keepdims=True))


def attn_fwd(q, k, v):
    nh, s, qk = q.shape
    tq = min(s, ATT_TILE)

    def body(q_ref, k_ref, v_ref, o_ref):
        for qi in range(s // tq):
            n = (qi + 1) * tq
            p = _attn_probs(q_ref, k_ref, qi, tq).astype(BF16)
            o_ref[qi * tq:n, :] = _dot(p, v_ref[0:n, :])

    head = lambda w: pl.BlockSpec((None, s, w), lambda h: (h, 0, 0))
    return pl.pallas_call(
        body, name="attn_fwd",
        grid=(nh,),
        in_specs=[head(qk), head(qk), head(V_HEAD)],
        out_specs=pl.BlockSpec((s, V_HEAD), lambda h: (0, h)),
        out_shape=jax.ShapeDtypeStruct((s, nh * V_HEAD), F32),
        compiler_params=_params(("arbitrary",)),
    )(q, k, v)


def attn_bwd(q, k, v, do):
    nh, s, qk = q.shape
    tq = min(s, ATT_TILE)

    def body(q_ref, k_ref, v_ref, do_ref, dq_ref, dk_ref, dv_ref):
        dk_ref[...] = jnp.zeros_like(dk_ref)
        dv_ref[...] = jnp.zeros_like(dv_ref)
        for qi in range(s // tq):
            n = (qi + 1) * tq
            rows = slice(qi * tq, n)
            p = _attn_probs(q_ref, k_ref, qi, tq)
            dob = do_ref[rows, :].astype(BF16)
            dp = _dot_nt(dob, v_ref[0:n, :])
            ds = (p * (dp - jnp.sum(p * dp, axis=-1, keepdims=True)) * SOFTMAX_SCALE).astype(BF16)
            dq_ref[rows, :] = _dot(ds, k_ref[0:n, :])
            dk_ref[0:n, :] += _dot_tn(ds, q_ref[rows, :])
            dv_ref[0:n, :] += _dot_tn(p.astype(BF16), dob)

    head = lambda w: pl.BlockSpec((None, s, w), lambda h: (h, 0, 0))
    return pl.pallas_call(
        body, name="attn_bwd",
        grid=(nh,),
        in_specs=[head(qk), head(qk), head(V_HEAD), pl.BlockSpec((s, V_HEAD), lambda h: (0, h))],
        out_specs=[head(qk), head(qk), head(V_HEAD)],
        out_shape=[jax.ShapeDtypeStruct((nh, s, qk), F32), jax.ShapeDtypeStruct((nh, s, qk), F32),
                   jax.ShapeDtypeStruct((nh, s, V_HEAD), F32)],
        compiler_params=_params(("arbitrary",)),
    )(q, k, v, do)


def mla_qkv_bwd(dq, dk, dv, cq, ckv, qan, kvan, wq, wkv, cos, sin, rot_t):
    s = cq.shape[0]
    tm = _row_tile(s)

    def body(dq_ref, dk_ref, dv_ref, cq_ref, ckv_ref, qan_ref, kvan_ref,
             wq_ref, wkv_ref, cos_ref, sin_ref, rot_ref,
             dcq_ref, dckv_ref, dkro_ref, gq_ref, gkv_ref, dqan_ref, dkvan_ref):
        i = pl.program_id(0)

        @pl.when(i == 0)
        def _():
            dqan_ref[...] = jnp.zeros_like(dqan_ref)
            dkvan_ref[...] = jnp.zeros_like(dkvan_ref)

        cos_t = cos_ref[...]
        sin_t = sin_ref[...]
        perm_t = rot_ref[...]

        def unrope(t):
            return t * cos_t + _dot_exact(t * sin_t, perm_t)

        acc_q = jnp.zeros((tm, Q_LORA), F32)
        acc_kv = jnp.zeros((tm, KV_LORA), F32)
        dkr_sum = jnp.zeros((tm, QK_ROPE), F32)
        for h in range(N_HEADS):
            dq_h = dq_ref[h]
            a = dq_h[:, 0:QK_NOPE].astype(BF16)
            b = unrope(dq_h[:, QK_NOPE:]).astype(BF16)
            gq_ref[h, :, 0:QK_NOPE] = a
            gq_ref[h, :, QK_NOPE:] = b
            wq_h = wq_ref[h]
            acc_q += _dot(a, wq_h[0:QK_NOPE, :]) + _dot(b, wq_h[QK_NOPE:, :])
            dk_h = dk_ref[h]
            dk = dk_h[:, 0:QK_NOPE].astype(BF16)
            dvv = dv_ref[h].astype(BF16)
            gkv_ref[h, :, 0:QK_NOPE] = dk
            gkv_ref[h, :, QK_NOPE:] = dvv
            wkv_h = wkv_ref[h]
            acc_kv += _dot_nt(dk, wkv_h[:, 0:QK_NOPE]) + _dot_nt(dvv, wkv_h[:, QK_NOPE:])
            dkr_sum += dk_h[:, QK_NOPE:]
        dkro_ref[...] = unrope(dkr_sum)
        dcq, dqan = _rms_bwd(acc_q, cq_ref[...], qan_ref[...])
        dcq_ref[...] = dcq
        dqan_ref[...] += dqan
        dckv, dkvan = _rms_bwd(acc_kv, ckv_ref[...], kvan_ref[...])
        dckv_ref[...] = dckv
        dkvan_ref[...] += dkvan

    row = lambda w: pl.BlockSpec((tm, w), lambda i: (i, 0))
    hrow = lambda w: pl.BlockSpec((N_HEADS, tm, w), lambda i: (0, i, 0))
    return pl.pallas_call(
        body, name="mla_qkv_bwd",
        grid=(s // tm,),
        in_specs=[hrow(QK_NOPE + QK_ROPE), hrow(QK_NOPE + QK_ROPE), hrow(V_HEAD),
                  row(Q_LORA), row(KV_LORA), _full(qan.shape), _full(kvan.shape),
                  _full(wq.shape), _full(wkv.shape), row(QK_ROPE), row(QK_ROPE), _full(rot_t.shape)],
        out_specs=[row(Q_LORA), row(KV_LORA), row(QK_ROPE), hrow(QK_NOPE + QK_ROPE), hrow(QK_NOPE + V_HEAD),
                   _full(qan.shape), _full(kvan.shape)],
        out_shape=[jax.ShapeDtypeStruct((s, Q_LORA), F32), jax.ShapeDtypeStruct((s, KV_LORA), F32),
                   jax.ShapeDtypeStruct((s, QK_ROPE), F32),
                   jax.ShapeDtypeStruct((N_HEADS, s, QK_NOPE + QK_ROPE), BF16),
                   jax.ShapeDtypeStruct((N_HEADS, s, QK_NOPE + V_HEAD), BF16),
                   jax.ShapeDtypeStruct(qan.shape, F32), jax.ShapeDtypeStruct(kvan.shape, F32)],
        compiler_params=_params(("arbitrary",)),
    )(dq, dk, dv, cq, ckv, qan, kvan, wq, wkv, cos, sin, rot_t)


def out_proj_fwd(yp, om, w_out, x, gt):
    s, d = x.shape
    n_sh, rs, _ = w_out.shape
    tm = _row_tile(s)
    per = POOL_WIDTH // rs

    def body(yp_ref, om_ref, w_ref, x_ref, gt_ref, xo_ref, ycat_ref, y_ref):
        y = jnp.zeros((tm, d), F32)
        for j in range(n_sh):
            src = yp_ref if j < per else om_ref
            part = src[:, (j % per) * rs:(j % per + 1) * rs].astype(BF16)
            ycat_ref[j] = part.T
            y += _dot(part, w_ref[j])
        y_ref[...] = y.astype(BF16)
        xo_ref[...] = x_ref[...] + gt_ref[...] * y

    row = lambda w: pl.BlockSpec((tm, w), lambda i: (i, 0))
    return pl.pallas_call(
        body, name="out_proj_fwd",
        grid=(s // tm,),
        in_specs=[row(POOL_WIDTH), row(POOL_WIDTH), _full(w_out.shape), row(d), pl.BlockSpec((1, d), lambda i: (0, 0))],
        out_specs=[row(d), pl.BlockSpec((n_sh, rs, tm), lambda i: (0, 0, i)), row(d)],
        out_shape=[jax.ShapeDtypeStruct((s, d), F32), jax.ShapeDtypeStruct((n_sh, rs, s), BF16),
                   jax.ShapeDtypeStruct((s, d), BF16)],
        compiler_params=_params(("arbitrary",)),
    )(yp, om, w_out, x, gt)


def out_proj_bwd(dxn, y, gt, w_out):
    s, d = dxn.shape
    n_sh, rs, _ = w_out.shape
    tm = _row_tile(s)
    per = POOL_WIDTH // rs

    def body(dxn_ref, y_ref, gt_ref, w_ref, dy_ref, dyp_ref, dom_ref, dgt_ref):
        i = pl.program_id(0)

        @pl.when(i == 0)
        def _():
            dgt_ref[...] = jnp.zeros_like(dgt_ref)

        dxn_t = dxn_ref[...]
        dy = (gt_ref[...] * dxn_t).astype(BF16)
        dy_ref[...] = dy
        dgt_ref[...] += _sum0(dxn_t * y_ref[...].astype(F32))
        for j in range(n_sh):
            dst = dyp_ref if j < per else dom_ref
            dst[:, (j % per) * rs:(j % per + 1) * rs] = _dot_nt(dy, w_ref[j])

    row = lambda w: pl.BlockSpec((tm, w), lambda i: (i, 0))
    vec = pl.BlockSpec((1, d), lambda i: (0, 0))
    return pl.pallas_call(
        body, name="out_proj_bwd",
        grid=(s // tm,),
        in_specs=[row(d), row(d), vec, _full(w_out.shape)],
        out_specs=[row(d), row(POOL_WIDTH), row(POOL_WIDTH), vec],
        out_shape=[jax.ShapeDtypeStruct((s, d), BF16), jax.ShapeDtypeStruct((s, POOL_WIDTH), F32),
                   jax.ShapeDtypeStruct((s, POOL_WIDTH), F32), jax.ShapeDtypeStruct((1, d), F32)],
        compiler_params=_params(("arbitrary",)),
    )(dxn, y, gt, w_out)


def final_loss(x, gn, tgt):
    s, d = x.shape
    tm = _row_tile(s)

    def body(x_ref, gn_ref, t_ref, loss_ref, dx_ref, dgn_ref):
        i = pl.program_id(0)

        @pl.when(i == 0)
        def _():
            loss_ref[...] = jnp.zeros_like(loss_ref)
            dgn_ref[...] = jnp.zeros_like(dgn_ref)

        xt = x_ref[...]
        g = gn_ref[...]
        xhat, _ = _rms(xt)
        err = xhat * g - t_ref[...]
        per_tok = jnp.mean(err * err, axis=-1, keepdims=True)
        loss_ref[...] += jnp.broadcast_to(0.5 * _sum0(per_tok), loss_ref.shape)
        dx, dgn = _rms_bwd(err * (1.0 / d), xt, g)
        dx_ref[...] = dx
        dgn_ref[...] += dgn

    row = pl.BlockSpec((tm, d), lambda i: (i, 0))
    vec = pl.BlockSpec((1, d), lambda i: (0, 0))
    return pl.pallas_call(
        body, name="final_loss",
        grid=(s // tm,),
        in_specs=[row, vec, row],
        out_specs=[pl.BlockSpec((1, LANES), lambda i: (0, 0)), row, vec],
        out_shape=[jax.ShapeDtypeStruct((1, LANES), F32), jax.ShapeDtypeStruct((s, d), F32),
                   jax.ShapeDtypeStruct((1, d), F32)],
        compiler_params=_params(("arbitrary",)),
    )(x, gn, tgt)


def _col_tile(cols):
    return 768 if cols % 768 == 0 else cols


def ada_fwd(c16, ada_w, ada_b_loc):
    n_layers, d, cols = ada_w.shape
    tn = _col_tile(cols)

    def body(c_ref, w_ref, b_ref, o_ref):
        cv = c_ref[...]
        ca = (cv * jax.nn.sigmoid(cv)).astype(BF16)
        o_ref[...] = _dot(ca, w_ref[...].astype(BF16)) + b_ref[...]

    return pl.pallas_call(
        body, name="ada_fwd",
        grid=(n_layers, cols // tn),
        in_specs=[pl.BlockSpec((16, d), lambda l, j: (0, 0)), pl.BlockSpec((None, d, tn), lambda l, j: (l, 0, j)),
                  pl.BlockSpec((None, 1, tn), lambda l, j: (l, 0, j))],
        out_specs=pl.BlockSpec((None, 16, tn), lambda l, j: (l, 0, j)),
        out_shape=jax.ShapeDtypeStruct((n_layers, 16, cols), F32),
        compiler_params=_params(("arbitrary", "arbitrary")),
    )(c16, ada_w, ada_b_loc)


def ada_bwd(c16, dmod16):
    n_layers, _, cols = dmod16.shape
    d = c16.shape[1]
    tn = _col_tile(cols)

    def body(c_ref, g_ref, o_ref):
        cv = c_ref[...]
        ca = (cv * jax.nn.sigmoid(cv)).astype(BF16)
        o_ref[...] = _dot_tn(ca, g_ref[...].astype(BF16))

    return pl.pallas_call(
        body, name="ada_bwd",
        grid=(n_layers, cols // tn),
        in_specs=[pl.BlockSpec((16, d), lambda l, j: (0, 0)), pl.BlockSpec((None, 16, tn), lambda l, j: (l, 0, j))],
        out_specs=pl.BlockSpec((None, d, tn), lambda l, j: (l, 0, j)),
        out_shape=jax.ShapeDtypeStruct((n_layers, d, cols), F32),
        compiler_params=_params(("arbitrary", "arbitrary")),
    )(c16, dmod16)


def _as_rows(a):
    if a.ndim == 1:
        return a.reshape(1, a.shape[0])
    return a.reshape(-1, a.shape[-1])


def _rows_tile(r, c, itemsize=4, budget=2 * 1024 * 1024):
    if r * c * itemsize <= budget:
        return r
    best = None
    t = BF16_ROWS
    while t < r:
        if r % t == 0 and t * c * itemsize <= budget:
            best = t
        t += BF16_ROWS
    return best if best is not None else r


CAST_VMEM = 16 * 1024 * 1024


def cast_place(ws, chip, layers, after):
    _, r, c = ws[0].shape
    n_sel = len(layers)
    n_blk = len(ws) * n_sel
    tr = _rows_tile(r, c, budget=CAST_VMEM // (3 * n_blk))

    def body(chip_ref, *refs):
        for j in range(n_blk):
            refs[n_blk + 1 + j][...] = refs[j][...].astype(BF16)

    layer_spec = lambda l: pl.BlockSpec((None, tr, c), lambda i, ch: (l, i, 0))
    outs = pl.pallas_call(
        body, name="cast_place",
        grid_spec=pltpu.PrefetchScalarGridSpec(
            num_scalar_prefetch=1, grid=(r // tr,),
            in_specs=[layer_spec(l) for _ in ws for l in layers] + [pl.BlockSpec(memory_space=pl.ANY)],
            out_specs=[pl.BlockSpec((None, tr, c), lambda i, ch: (ch[0], i, 0))] * n_blk),
        out_shape=[jax.ShapeDtypeStruct((N_CHIPS, r, c), BF16)] * n_blk,
        compiler_params=_params(("arbitrary",)),
    )(chip, *[w for w in ws for _ in layers], after)
    return [list(outs[i * n_sel:(i + 1) * n_sel]) for i in range(len(ws))]


def adamw(w, g, m, v, copy_g=False):
    shape = w.shape
    w2, g2, m2, v2 = (_as_rows(t) for t in (w, g, m, v))
    r, c = w2.shape
    tr = _rows_tile(r, c, budget=3 * 1024 * 1024)
    c1 = 1.0 - ADAM_B1 ** ADAM_STEP
    c2 = 1.0 - ADAM_B2 ** ADAM_STEP

    def body(w_ref, g_ref, m_ref, v_ref, d_ref, mo_ref, vo_ref, *go_ref):
        gv = g_ref[...]
        if copy_g:
            go_ref[0][...] = gv
        mn = ADAM_B1 * m_ref[...] + (1.0 - ADAM_B1) * gv
        vn = ADAM_B2 * v_ref[...] + (1.0 - ADAM_B2) * (gv * gv)
        mo_ref[...] = mn
        vo_ref[...] = vn
        d_ref[...] = -ADAM_LR * ((mn / c1) / (jnp.sqrt(vn / c2) + ADAM_EPS) + ADAM_WD * w_ref[...])

    spec = pl.BlockSpec((tr, c), lambda i: (i, 0))
    n_out = 4 if copy_g else 3
    outs = pl.pallas_call(
        body, name="adamw", grid=(r // tr,), in_specs=[spec] * 4, out_specs=[spec] * n_out,
        out_shape=[jax.ShapeDtypeStruct((r, c), F32)] * n_out, compiler_params=_params(("arbitrary",)),
    )(w2, g2, m2, v2)
    g_out = outs[3] if copy_g else g2
    return tuple(o.reshape(shape) for o in (g_out,) + tuple(outs[:3]))


def sum_devices(a, after):
    n, r, c = a.shape
    tr = _rows_tile(r, c, budget=512 * 1024)

    def body(a_ref, after_ref, o_ref):
        acc = a_ref[0]
        for j in range(1, n):
            acc = acc + a_ref[j]
        o_ref[...] = acc

    return pl.pallas_call(
        body, name="sum_devices", grid=(r // tr,),
        in_specs=[pl.BlockSpec((n, tr, c), lambda i: (0, i, 0)), pl.BlockSpec(memory_space=pl.ANY)],
        out_specs=pl.BlockSpec((tr, c), lambda i: (i, 0)),
        out_shape=jax.ShapeDtypeStruct((r, c), F32), compiler_params=_params(("arbitrary",)),
    )(a, after)


def _split_axis(r, c):
    if (r // 2) % BF16_ROWS == 0 and r % 2 == 0:
        return 0
    assert c % (2 * LANES) == 0, (r, c)
    return 1


def _half_shape(r, c):
    return (r // 2, c) if _split_axis(r, c) == 0 else (r, c // 2)


def _half_at(ref, lead, which):
    r, c = ref.shape[-2:]
    if _split_axis(r, c) == 0:
        return ref.at[(*lead, pl.ds(which * (r // 2), r // 2), slice(None))]
    return ref.at[(*lead, slice(None), pl.ds(which * (c // 2), c // 2))]


def _half_spec(r, c, lead_block, imap):
    hr, hc = _half_shape(r, c)
    if _split_axis(r, c) == 0:
        return pl.BlockSpec((*lead_block, hr, hc), lambda *a: (*imap(*a)[0], imap(*a)[1], 0))
    return pl.BlockSpec((*lead_block, hr, hc), lambda *a: (*imap(*a)[0], 0, imap(*a)[1]))


def pair_add(owns, alls, ra_owns, ra_alls, sel):
    n = len(owns)
    n_sl = alls[0].shape[0]
    halves = [_half_shape(*g.shape) for g in owns]

    def body(s_ref, *refs):
        own_refs, all_refs, ra_own_refs, ra_all_refs, pb_refs, sum_refs = (refs[i * n:(i + 1) * n] for i in range(6))
        k = pl.program_id(0)
        for t in range(n):
            pb_refs[t][...] = (all_refs[t][...].astype(F32) + ra_all_refs[t][...].astype(F32)).astype(BF16)

            @pl.when(k == s_ref[1])
            def _(t=t):
                sum_refs[t][...] = own_refs[t][...] + ra_own_refs[t][...]

    slot = lambda hs: pl.BlockSpec((None,) + hs, lambda k, sr: (k, 0, 0))
    whole = lambda hs: pl.BlockSpec(hs, lambda k, sr: (0, 0))
    outs = pl.pallas_call(
        body, name="pair_add",
        grid_spec=pltpu.PrefetchScalarGridSpec(
            num_scalar_prefetch=1, grid=(n_sl,),
            in_specs=[_half_spec(*g.shape, (), lambda k, sr: ((), sr[0])) for g in owns]
            + [_half_spec(*g.shape[1:], (None,), lambda k, sr: ((k,), sr[0])) for g in alls]
            + [whole(hs) for hs in halves] + [slot(hs) for hs in halves],
            out_specs=[slot(hs) for hs in halves] + [whole(hs) for hs in halves]),
        out_shape=[jax.ShapeDtypeStruct((n_sl,) + hs, BF16) for hs in halves]
        + [jax.ShapeDtypeStruct(hs, F32) for hs in halves],
        compiler_params=_params(("arbitrary",)),
    )(sel, *owns, *alls, *ra_owns, *ra_alls)
    return list(outs[:n]), list(outs[n:])


def chip_sum(owns, rbs, sel, shapes, accs):
    n = len(owns)
    fresh = accs[0] is None

    def body(s_ref, *refs):
        own_refs, rb_refs, o_refs = refs[:n], refs[n:2 * n], refs[-n:]
        for t in range(n):
            acc_v = own_refs[t][...]
            for j in range(N_CHIPS - 1):
                acc_v = acc_v + rb_refs[t][j].astype(F32)
            o_refs[t][...] = acc_v

    in_specs = ([pl.BlockSpec(o.shape, lambda i, sr: (0, 0)) for o in owns]
                + [pl.BlockSpec(rb.shape, lambda i, sr: (0, 0, 0)) for rb in rbs])
    args = [sel, *owns, *rbs]
    aliases = {}
    if not fresh:
        in_specs += [pl.BlockSpec(memory_space=pl.ANY)] * n
        args += list(accs)
        aliases = {1 + 2 * n + t: t for t in range(n)}
    return list(pl.pallas_call(
        body, name="chip_sum",
        grid_spec=pltpu.PrefetchScalarGridSpec(
            num_scalar_prefetch=1, grid=(1,), in_specs=in_specs,
            out_specs=[_half_spec(*shp[1:], (None,), lambda i, sr: ((sr[2],), sr[0])) for shp in shapes]),
        out_shape=[jax.ShapeDtypeStruct(shp, F32) for shp in shapes],
        input_output_aliases=aliases,
        compiler_params=_params(("arbitrary",)),
    )(*args))


def _me():
    return lax.axis_index("x"), lax.axis_index("y"), lax.axis_index("c")


def _flip(v, bit):
    return 1 - v if bit else v


def exchange8(xs, bcast):
    blk = xs.shape if bcast else xs.shape[1:]

    def body(x_ref, o_ref, send_sems, recv_sems, loc_sem):
        mx, my, mc = _me()
        me = 4 * mx + 2 * my + mc
        src = (lambda j: x_ref) if bcast else (lambda j: x_ref.at[j])
        loc = pltpu.make_async_copy(src(me), o_ref.at[me], loc_sem)
        loc.start()
        copies = []
        for o in range(1, N_DEV):
            px, py, pc = _flip(mx, o & 4), _flip(my, o & 2), _flip(mc, o & 1)
            cp = pltpu.make_async_remote_copy(
                src_ref=src(4 * px + 2 * py + pc), dst_ref=o_ref.at[me],
                send_sem=send_sems.at[o - 1], recv_sem=recv_sems.at[o - 1],
                device_id=(px, py, pc), device_id_type=MESH)
            cp.start()
            copies.append(cp)
        for cp in copies:
            cp.wait()
        loc.wait()

    return pl.pallas_call(
        body, name="exchange8_gather" if bcast else "exchange8_a2a",
        in_specs=[pl.BlockSpec(memory_space=pltpu.VMEM)], out_specs=pl.BlockSpec(memory_space=pltpu.VMEM),
        out_shape=jax.ShapeDtypeStruct((N_DEV,) + tuple(blk), xs.dtype),
        scratch_shapes=[pltpu.SemaphoreType.DMA((N_DEV - 1,)), pltpu.SemaphoreType.DMA((N_DEV - 1,)), pltpu.SemaphoreType.DMA],
        compiler_params=_params(),
    )(xs)


HBM = pl.BlockSpec(memory_space=pltpu.HBM)
SEM = pl.BlockSpec(memory_space=pltpu.SEMAPHORE)
EFFECT = pltpu.SideEffectType.DATAFLOW_SIDE_EFFECTING


def _hbm(a):
    return pltpu.with_memory_space_constraint(a, pltpu.HBM)


def _ici_copy(land, o, send_sem, recv_sem, sending):
    mx, my, mc = _me()
    px, py = _flip(mx, o & 2), _flip(my, o & 1)
    mine = _half_at(land, (2 * mx + my,), mc)
    return pltpu.make_async_remote_copy(
        src_ref=mine, dst_ref=mine if sending else _half_at(land, (2 * px + py,), mc),
        send_sem=send_sem, recv_sem=recv_sem, device_id=(px, py, mc), device_id_type=MESH)


N_PEERS = N_CHIPS - 1
DMA_SEM = pltpu.SemaphoreType.DMA(())


def gather_start(lands, groups, after, tag):
    n_layers, n = len(lands), len(lands[0])
    flat = [a for layer in lands for a in layer]
    n_in = n * n_layers
    n_grp = len(groups)
    n_sem = 2 * n_layers * n_grp * N_PEERS
    first = lambda l, g, recv: ((l * n_grp + g) * 2 + recv) * N_PEERS

    def body(*refs):
        land = refs[:n_in]
        sems = refs[n_in + 1:n_in + 1 + n_sem]
        token = refs[-1]
        for l in range(n_layers):
            for g, members in enumerate(groups):
                for t in members:
                    for o in range(1, N_CHIPS):
                        _ici_copy(land[l * n + t], o, sems[first(l, g, 0) + o - 1], sems[first(l, g, 1) + o - 1],
                                  True).start()
        token[...] = jnp.zeros_like(token)

    outs = pl.pallas_call(
        body, name=f"gather_start_{tag}",
        in_specs=[HBM] * n_in + [pl.BlockSpec(memory_space=pl.ANY)],
        out_specs=[SEM] * n_sem + [HBM] * n_in + [pl.BlockSpec(memory_space=pltpu.VMEM)],
        out_shape=[DMA_SEM] * n_sem + [pltpu.HBM(a.shape, a.dtype) for a in flat]
        + [jax.ShapeDtypeStruct((8, LANES), F32)],
        input_output_aliases={i: i + n_sem for i in range(n_in)},
        compiler_params=pltpu.CompilerParams(has_side_effects=EFFECT),
    )(*[_hbm(a) for a in flat], after)
    sems = [[(list(outs[first(l, g, 0):first(l, g, 0) + N_PEERS]), list(outs[first(l, g, 1):first(l, g, 1) + N_PEERS]))
             for g in range(n_grp)] for l in range(n_layers)]
    lands_thru = [list(outs[n_sem + l * n:n_sem + (l + 1) * n]) for l in range(n_layers)]
    return sems, lands_thru, outs[-1]


def gather_wait(tag, sems, lands, after):
    n = len(lands)
    send_sems, recv_sems = sems

    def body(*refs):
        land = refs[:n]
        send_r = refs[n:n + N_PEERS]
        recv_r = refs[n + N_PEERS:n + 2 * N_PEERS]
        for t in range(n):
            for o in range(1, N_CHIPS):
                _ici_copy(land[t], o, send_r[o - 1], recv_r[o - 1], True).wait_send()
                _ici_copy(land[t], o, send_r[o - 1], recv_r[o - 1], False).wait_recv()

    return list(pl.pallas_call(
        body, name=f"gather_wait_{tag}",
        in_specs=[HBM] * n + [SEM] * (2 * N_PEERS) + [pl.BlockSpec(memory_space=pl.ANY)],
        out_specs=[HBM] * n,
        out_shape=[pltpu.HBM(a.shape, a.dtype) for a in lands],
        input_output_aliases={i: i for i in range(n)},
        compiler_params=pltpu.CompilerParams(has_side_effects=EFFECT),
    )(*lands, *send_sems, *recv_sems, after))


def gather_forward(lands):
    n = len(lands)

    def body(*refs):
        dst = refs[n:2 * n]
        send_sems, recv_sems = refs[2 * n:]
        mx, my, mc = _me()
        fwds = []
        for t in range(n):
            for o in range(1, N_CHIPS):
                slot = 2 * _flip(mx, o & 2) + _flip(my, o & 1)
                mine = _half_at(dst[t], (slot,), mc)
                theirs = _half_at(dst[t], (slot,), 1 - mc)
                cp = pltpu.make_async_remote_copy(
                    src_ref=mine, dst_ref=mine, send_sem=send_sems.at[t, o - 1], recv_sem=recv_sems.at[t, o - 1],
                    device_id=(mx, my, 1 - mc), device_id_type=MESH)
                cp.start()
                fwds.append((cp, pltpu.make_async_remote_copy(
                    src_ref=theirs, dst_ref=theirs, send_sem=send_sems.at[t, o - 1], recv_sem=recv_sems.at[t, o - 1],
                    device_id=(mx, my, 1 - mc), device_id_type=MESH)))
        for cp, arrival in fwds:
            cp.wait_send()
            arrival.wait_recv()

    any_spec = pl.BlockSpec(memory_space=pl.ANY)
    return list(pl.pallas_call(
        body, name="gather_forward",
        in_specs=[any_spec] * n, out_specs=[any_spec] * n,
        out_shape=[jax.ShapeDtypeStruct(a.shape, a.dtype) for a in lands],
        input_output_aliases={t: t for t in range(n)},
        scratch_shapes=[pltpu.SemaphoreType.DMA((n, N_CHIPS - 1)), pltpu.SemaphoreType.DMA((n, N_CHIPS - 1))],
        compiler_params=_params(),
    )(*lands))


def _scatter_copy(src, land, o, send_sem, recv_sem):
    mx, my, mc = _me()
    px, py = _flip(mx, o & 2), _flip(my, o & 1)
    return pltpu.make_async_remote_copy(
        src_ref=src.at[2 * px + py], dst_ref=land.at[o - 1],
        send_sem=send_sem, recv_sem=recv_sem, device_id=(px, py, mc), device_id_type=MESH)


def scatter_start(pbs, tag, after):
    n = len(pbs)
    lands = [lax.empty((N_CHIPS - 1,) + p.shape[1:], p.dtype) for p in pbs]

    def body(*refs):
        src = refs[:n]
        land = refs[n:2 * n]
        send_sems = refs[2 * n + 1:2 * n + 1 + N_PEERS]
        recv_sems = refs[2 * n + 1 + N_PEERS:2 * n + 1 + 2 * N_PEERS]
        token = refs[-1]
        for t in range(n):
            for o in range(1, N_CHIPS):
                _scatter_copy(src[t], land[t], o, send_sems[o - 1], recv_sems[o - 1]).start()
        token[...] = jnp.zeros_like(token)

    n_sem = 2 * N_PEERS
    arrs = list(pbs) + lands
    outs = pl.pallas_call(
        body, name=f"scatter_start_{tag}",
        in_specs=[HBM] * (2 * n) + [pl.BlockSpec(memory_space=pl.ANY)],
        out_specs=[SEM] * n_sem + [HBM] * (2 * n) + [pl.BlockSpec(memory_space=pltpu.VMEM)],
        out_shape=[DMA_SEM] * n_sem + [pltpu.HBM(a.shape, a.dtype) for a in arrs]
        + [jax.ShapeDtypeStruct((8, LANES), F32)],
        input_output_aliases={i: i + n_sem for i in range(2 * n)},
        compiler_params=pltpu.CompilerParams(has_side_effects=EFFECT),
    )(*[_hbm(a) for a in arrs], after)
    return (list(outs[:N_PEERS]), list(outs[N_PEERS:n_sem]), list(outs[n_sem:n_sem + n]),
            list(outs[n_sem + n:n_sem + 2 * n]), outs[-1])


def scatter_wait(tag, send_sems, recv_sems, pbs, lands, after):
    n = len(pbs)

    def body(*refs):
        src = refs[:n]
        land = refs[n:2 * n]
        send_r = refs[2 * n:2 * n + N_PEERS]
        recv_r = refs[2 * n + N_PEERS:2 * n + 2 * N_PEERS]
        for t in range(n):
            for o in range(1, N_CHIPS):
                cp = _scatter_copy(src[t], land[t], o, send_r[o - 1], recv_r[o - 1])
                cp.wait_send()
                cp.wait_recv()

    arrs = list(pbs) + list(lands)
    outs = pl.pallas_call(
        body, name=f"scatter_wait_{tag}",
        in_specs=[HBM] * (2 * n) + [SEM] * (2 * N_PEERS) + [pl.BlockSpec(memory_space=pl.ANY)],
        out_specs=[HBM] * (2 * n),
        out_shape=[pltpu.HBM(a.shape, a.dtype) for a in arrs],
        input_output_aliases={i: i for i in range(2 * n)},
        compiler_params=pltpu.CompilerParams(has_side_effects=EFFECT),
    )(*arrs, *send_sems, *recv_sems, after)
    return list(outs[n:])


def _pair_copies(srcs, lands, send_sem, recv_sem):
    mx, my, mc = _me()
    return [pltpu.make_async_remote_copy(
        src_ref=_half_at(src, (slice(None),) * (len(src.shape) - 2), 1 - mc), dst_ref=land,
        send_sem=send_sem, recv_sem=recv_sem, device_id=(mx, my, 1 - mc), device_id_type=MESH)
        for src, land in zip(srcs, lands)]


def pair_start(gs, tag, after):
    n = len(gs)
    lands = [lax.empty(g.shape[:-2] + _half_shape(*g.shape[-2:]), g.dtype) for g in gs]

    def body(*refs):
        send_sem, recv_sem = refs[2 * n + 1], refs[2 * n + 2]
        token = refs[-1]
        for cp in _pair_copies(refs[:n], refs[n:2 * n], send_sem, recv_sem):
            cp.start()
        token[...] = jnp.zeros_like(token)

    arrs = list(gs) + lands
    outs = pl.pallas_call(
        body, name=f"pair_start_{tag}",
        in_specs=[HBM] * (2 * n) + [pl.BlockSpec(memory_space=pl.ANY)],
        out_specs=[SEM, SEM] + [HBM] * (2 * n) + [pl.BlockSpec(memory_space=pltpu.VMEM)],
        out_shape=[DMA_SEM, DMA_SEM] + [pltpu.HBM(a.shape, a.dtype) for a in arrs] + [jax.ShapeDtypeStruct((8, LANES), F32)],
        input_output_aliases={i: i + 2 for i in range(2 * n)},
        compiler_params=pltpu.CompilerParams(has_side_effects=EFFECT),
    )(*[_hbm(a) for a in arrs], after)
    return outs[0], outs[1], list(outs[2:2 + n]), list(outs[2 + n:2 + 2 * n]), outs[-1]


def pair_wait(tag, send_sem, recv_sem, gs, lands, after):
    n = len(gs)

    def body(*refs):
        for cp in _pair_copies(refs[:n], refs[n:2 * n], refs[2 * n], refs[2 * n + 1]):
            cp.wait_send()
            cp.wait_recv()

    arrs = list(gs) + list(lands)
    outs = pl.pallas_call(
        body, name=f"pair_wait_{tag}",
        in_specs=[HBM] * (2 * n) + [SEM, SEM, pl.BlockSpec(memory_space=pl.ANY)],
        out_specs=[HBM] * (2 * n),
        out_shape=[pltpu.HBM(a.shape, a.dtype) for a in arrs],
        input_output_aliases={i: i for i in range(2 * n)},
        compiler_params=pltpu.CompilerParams(has_side_effects=EFFECT),
    )(*arrs, send_sem, recv_sem, after)
    return list(outs[:n]), list(outs[n:])


def _gather8_copy(x, land, o, send_sem, recv_sem, sending):
    mx, my, mc = _me()
    px, py, pc = _flip(mx, o & 4), _flip(my, o & 2), _flip(mc, o & 1)
    slot = 4 * mx + 2 * my + mc if sending else 4 * px + 2 * py + pc
    return pltpu.make_async_remote_copy(
        src_ref=x, dst_ref=land.at[slot], send_sem=send_sem, recv_sem=recv_sem,
        device_id=(px, py, pc), device_id_type=MESH)


def gather8_start(x, land, after, tag):
    n_peer = N_DEV - 1

    def body(x_ref, land_ref, after_ref, *rest):
        send_sems, recv_sems = rest[:n_peer], rest[n_peer:2 * n_peer]
        token = rest[-1]
        for o in range(1, N_DEV):
            _gather8_copy(x_ref, land_ref, o, send_sems[o - 1], recv_sems[o - 1], True).start()
        token[...] = jnp.zeros_like(token)

    outs = pl.pallas_call(
        body, name=f"gather8_start_{tag}",
        in_specs=[HBM, HBM, pl.BlockSpec(memory_space=pl.ANY)],
        out_specs=[SEM] * (2 * n_peer) + [HBM, HBM, pl.BlockSpec(memory_space=pltpu.VMEM)],
        out_shape=[DMA_SEM] * (2 * n_peer) + [pltpu.HBM(x.shape, x.dtype), pltpu.HBM(land.shape, land.dtype),
                                              jax.ShapeDtypeStruct((8, LANES), F32)],
        input_output_aliases={0: 2 * n_peer, 1: 2 * n_peer + 1},
        compiler_params=pltpu.CompilerParams(has_side_effects=EFFECT),
    )(_hbm(x), _hbm(land), after)
    return list(outs[:n_peer]), list(outs[n_peer:2 * n_peer]), outs[2 * n_peer], outs[2 * n_peer + 1], outs[-1]


def gather8_wait(tag, send_sems, recv_sems, x, land, after):
    n_peer = N_DEV - 1

    def body(x_ref, land_ref, *rest):
        send_r, recv_r = rest[:n_peer], rest[n_peer:2 * n_peer]
        for o in range(1, N_DEV):
            _gather8_copy(x_ref, land_ref, o, send_r[o - 1], recv_r[o - 1], True).wait_send()
            _gather8_copy(x_ref, land_ref, o, send_r[o - 1], recv_r[o - 1], False).wait_recv()

    return pl.pallas_call(
        body, name=f"gather8_wait_{tag}",
        in_specs=[HBM, HBM] + [SEM] * (2 * n_peer) + [pl.BlockSpec(memory_space=pl.ANY)],
        out_specs=[HBM, HBM],
        out_shape=[pltpu.HBM(x.shape, x.dtype), pltpu.HBM(land.shape, land.dtype)],
        input_output_aliases={0: 0, 1: 1},
        compiler_params=pltpu.CompilerParams(has_side_effects=EFFECT),
    )(x, land, *send_sems, *recv_sems, after)[1]


def _fill_copies(fs, send_sem, recv_sem, sending):
    mx, my, mc = _me()
    out = []
    for f in fs:
        region = _half_at(f, (slice(None),), mc if sending else 1 - mc)
        out.append(pltpu.make_async_remote_copy(
            src_ref=region, dst_ref=region, send_sem=send_sem, recv_sem=recv_sem,
            device_id=(mx, my, 1 - mc), device_id_type=MESH))
    return out


def fill_start(fs, tag, after):
    n = len(fs)

    def body(*refs):
        send_sem, recv_sem = refs[n + 1], refs[n + 2]
        token = refs[-1]
        for cp in _fill_copies(refs[:n], send_sem, recv_sem, True):
            cp.start()
        token[...] = jnp.zeros_like(token)

    outs = pl.pallas_call(
        body, name=f"fill_start_{tag}",
        in_specs=[HBM] * n + [pl.BlockSpec(memory_space=pl.ANY)],
        out_specs=[SEM, SEM] + [HBM] * n + [pl.BlockSpec(memory_space=pltpu.VMEM)],
        out_shape=[DMA_SEM, DMA_SEM] + [pltpu.HBM(f.shape, f.dtype) for f in fs] + [jax.ShapeDtypeStruct((8, LANES), F32)],
        input_output_aliases={i: i + 2 for i in range(n)},
        compiler_params=pltpu.CompilerParams(has_side_effects=EFFECT),
    )(*[_hbm(f) for f in fs], after)
    return outs[0], outs[1], list(outs[2:2 + n]), outs[-1]


def fill_wait(tag, send_sem, recv_sem, fs, after):
    n = len(fs)

    def body(*refs):
        for cp in _fill_copies(refs[:n], refs[n], refs[n + 1], True):
            cp.wait_send()
        for cp in _fill_copies(refs[:n], refs[n], refs[n + 1], False):
            cp.wait_recv()

    return list(pl.pallas_call(
        body, name=f"fill_wait_{tag}",
        in_specs=[HBM] * n + [SEM, SEM, pl.BlockSpec(memory_space=pl.ANY)],
        out_specs=[HBM] * n,
        out_shape=[pltpu.HBM(f.shape, f.dtype) for f in fs],
        input_output_aliases={i: i for i in range(n)},
        compiler_params=pltpu.CompilerParams(has_side_effects=EFFECT),
    )(*fs, send_sem, recv_sem, after))


def _pack_rows(parts, d):
    rows, spans = [], []
    at = 0
    for p in parts:
        flat = p.reshape(-1)
        n_rows = -(-flat.shape[0] // (8 * d)) * 8
        flat = jnp.pad(flat, (0, n_rows * d - flat.shape[0]))
        rows.append(flat.reshape(n_rows, d))
        spans.append((at, p.shape))
        at += n_rows
    return jnp.concatenate(rows, axis=0), spans


def _unpack_rows(packed, spans):
    lead, d = packed.shape[:-2], packed.shape[-1]
    out = []
    for at, shape in spans:
        n = math.prod(shape)
        n_rows = -(-n // d)
        out.append(packed[..., at:at + n_rows, :].reshape(lead + (-1,))[..., :n].reshape(lead + tuple(shape)))
    return out


def _rotate_half_matrix():
    half = QK_ROPE // 2
    idx = jnp.arange(QK_ROPE)
    src = jnp.where(idx < half, idx + half, idx - half)
    sign = jnp.where(idx < half, -1.0, 1.0)
    return (jnp.zeros((QK_ROPE, QK_ROPE), F32).at[src, idx].set(sign)).astype(BF16)


def kernel(x, c, positions, ada_w, ada_b, ffn1_norm, ffn1_w_gate, ffn1_w_up, ffn1_w_down, mix_norm, w_in, pool_w, pool_scale, q_a_norm, w_q_b, kv_a_norm, w_kv_b, w_out, ffn2_norm, ffn2_w_gate, ffn2_w_up, ffn2_w_down, final_norm, loss_target, m_ada_w, m_ada_b, m_ffn1_norm, m_ffn1_w_gate, m_ffn1_w_up, m_ffn1_w_down, m_mix_norm, m_w_in, m_pool_w, m_pool_scale, m_q_a_norm, m_w_q_b, m_kv_a_norm, m_w_kv_b, m_w_out, m_ffn2_norm, m_ffn2_w_gate, m_ffn2_w_up, m_ffn2_w_down, m_final_norm, v_ada_w, v_ada_b, v_ffn1_norm, v_ffn1_w_gate, v_ffn1_w_up, v_ffn1_w_down, v_mix_norm, v_w_in, v_pool_w, v_pool_scale, v_q_a_norm, v_w_q_b, v_kv_a_norm, v_w_kv_b, v_w_out, v_ffn2_norm, v_ffn2_w_gate, v_ffn2_w_up, v_ffn2_w_down, v_final_norm):
    mx, my, mc = _me()
    chip = 2 * mx + my
    half = jnp.reshape(mc, (1,)).astype(jnp.int32)
    chip1 = jnp.reshape(chip, (1,)).astype(jnp.int32)
    n_layers, d, ada_cols = ada_w.shape
    xt = x[0]
    tgt = loss_target[0]

    inv_freq = 1.0 / (ROPE_THETA ** (jnp.arange(0, QK_ROPE, 2, dtype=F32) / QK_ROPE))
    ang = positions[0].astype(F32)[:, None] * inv_freq
    ang = jnp.concatenate([ang, ang], axis=-1)
    cos, sin = jnp.cos(ang), jnp.sin(ang)
    rot = _rotate_half_matrix()
    rot_t = rot.T

    c_all = exchange8(c, True).reshape(N_DEV, d)
    c16 = jnp.pad(c_all, ((0, 8), (0, 0)))
    ada_b_loc = lax.dynamic_slice_in_dim(ada_b, chip * ada_cols, ada_cols, axis=1).reshape(n_layers, 1, ada_cols)
    mod_part = ada_fwd(c16, ada_w, ada_b_loc)[:, :N_DEV]
    mod_got = exchange8(jnp.transpose(mod_part, (1, 0, 2)), False)
    mod = jnp.transpose(mod_got.reshape(N_CHIPS, 2, n_layers, ada_cols)[:, 0], (1, 0, 2))
    mod = mod.reshape(n_layers, 9, 1, d)

    tr = lambda a: jnp.transpose(a, (0, 2, 1))
    local = [tr(ffn1_w_gate), tr(ffn1_w_up), ffn1_w_down, tr(w_in), tr(w_q_b), w_kv_b, w_out,
             tr(ffn2_w_gate), tr(ffn2_w_up), ffn2_w_down]
    ffn1_pos, mixer_pos, ffn2_pos = (0, 1, 2), (3, 4, 5, 6), (7, 8, 9)
    rest_pos = mixer_pos + ffn2_pos

    def cast_all(layers, after):
        by_shape = {}
        for t, w in enumerate(local):
            by_shape.setdefault(w.shape, []).append(t)
        out = [None] * len(local)
        for ts in by_shape.values():
            for t, per_layer in zip(ts, cast_place([local[t] for t in ts], chip1, layers, after)):
                out[t] = per_layer
        return out

    placed = cast_all((0,), mod)
    g_sems, lands_fly, g_token = gather_start([[p[0] for p in placed]], (ffn1_pos, mixer_pos, ffn2_pos), mod, "first")
    if n_layers > 1:
        later = tuple(range(1, n_layers))
        placed = cast_all(later, g_token)
        more_sems, more_fly, g_token = gather_start(
            [[p[j] for p in placed] for j in range(len(later))], (ffn1_pos, rest_pos), g_token, "rest")
        g_sems, lands_fly = g_sems + more_sems, lands_fly + more_fly
    gathered = []

    row = lambda a, l: a[l].reshape(1, -1)
    saved = []
    for l in range(n_layers):
        def fetch(tag, group, members, after, l=l):
            return gather_forward(gather_wait(tag, g_sems[l][group], [lands_fly[l][t] for t in members], after))

        g1, u1, d1 = fetch(f"{l}a", 0, ffn1_pos, xt if l else g_token)
        sv = dict(x0=xt)
        xt, sv["h1"], sv["a1"], sv["sl1"], sv["dsu1"], sv["y1"] = ffn_fwd(
            xt, row(ffn1_norm, l), mod[l, 0], mod[l, 1], mod[l, 2], g1, u1, d1)
        sv["x1"] = xt
        if l == 0:
            win, wq, wkv, wout = fetch("0b", 1, mixer_pos, xt)
        else:
            win, wq, wkv, wout, g2, u2, d2 = fetch(f"{l}b", 1, rest_pos, xt)
        win = win.reshape(-1, d)
        sv["h2"], u, cq, ckv, kr = mix_in_fwd(xt, row(mix_norm, l), mod[l, 3], mod[l, 4], win)
        sv["cq"], sv["ckv"] = cq, ckv
        yp, sv["diff"] = pool_fwd(u, pool_w[l], row(pool_scale, l))
        qh, kh, vh, sv["ql"], sv["kvl"] = mla_qkv_fwd(
            cq, ckv, kr, row(q_a_norm, l), row(kv_a_norm, l), wq, wkv, cos, sin, rot)
        sv["qkv"] = (qh, kh, vh)
        om = attn_fwd(qh, kh, vh)
        xt, sv["ycat"], sv["y2"] = out_proj_fwd(yp, om, wout, xt, mod[l, 5])
        sv["x2"] = xt
        if l == 0:
            g2, u2, d2 = fetch("0c", 2, ffn2_pos, xt)
        gathered.append([g1, u1, d1, win, wq, wkv, wout, g2, u2, d2])
        xt, sv["h3"], sv["a3"], sv["sl3"], sv["dsu3"], sv["y3"] = ffn_fwd(
            xt, row(ffn2_norm, l), mod[l, 6], mod[l, 7], mod[l, 8], g2, u2, d2)
        saved.append(sv)

    loss_vec, dx, d_final_norm = final_loss(xt, final_norm.reshape(1, d), tgt)
    loss = lax.psum(loss_vec[0, 0], ("x", "y", "c"))

    none = [None] * n_layers
    dmods, dnorm1, dnorm2, dnorm3 = list(none), list(none), list(none), list(none)
    dpw, dps, dqan_l, dkvan_l = list(none), list(none), list(none), list(none)
    reduced = [None] * len(local)
    stages = []
    sel_of = lambda l: jnp.stack([mc, chip, jnp.asarray(l, mc.dtype)]).astype(jnp.int32)

    def to_chips(job, after_wait, after_start):
        send, recv, g_fly, lands_p = job.pop("pair")
        g_fly, got = pair_wait(job["tag"], send, recv, g_fly, lands_p, after_wait)
        n_w = len(job["pos"])
        pbs, job["owns"] = pair_add(g_fly[:n_w], g_fly[n_w:], got[:n_w], got[n_w:], sel_of(job["l"]))
        job["scatter"] = scatter_start(pbs, job["tag"], after_start)
        return job["scatter"][4][0, 0]

    def finish(job, after):
        s_send, s_recv, pbs_fly, lands_j, _ = job.pop("scatter")
        parts = scatter_wait(job["tag"], s_send, s_recv, pbs_fly, lands_j, after)
        sums = chip_sum(job["owns"], parts, sel_of(job["l"]), [(n_layers,) + shp for shp in job["shapes"]],
                        [reduced[t] for t in job["pos"]])
        for t, total_t in zip(job["pos"], sums):
            reduced[t] = total_t

    def checkpoint(tag, l, positions, grads_, done, before_scatter=None):
        send, recv, g_fly, lands_p, tok = pair_start([g[0] for g in grads_] + [g[1] for g in grads_], tag, done)
        order = tok[0, 0]
        if stages:
            order = order + to_chips(stages[-1], done, done if before_scatter is None else before_scatter)
        if len(stages) >= 3:
            finish(stages[-3], done)
        stages.append(dict(tag=tag, l=l, pos=positions, shapes=[g[0].shape for g in grads_],
                           pair=(send, recv, g_fly, lands_p)))
        return order

    def small_gather(tag, parts, after):
        packed, spans = _pack_rows(parts, d)
        land = lax.dynamic_update_index_in_dim(lax.empty((N_DEV,) + packed.shape, F32), packed, 4 * mx + 2 * my + mc, 0)
        return gather8_start(packed, land, after, tag), spans

    order = None

    for l in reversed(range(n_layers)):
        sv = saved[l]
        g1, u1, d1, win, wq, wkv, wout, g2, u2, d2 = gathered[l]
        win = win.reshape(-1, d)
        gt3 = mod[l, 8] if order is None else mod[l, 8] + order
        dy, dgt, dup = ffn_bwd_act(dx, sv["sl3"], sv["dsu3"], gt3, d2)
        dx, dvec3 = ffn_bwd_in(dx, sv["x2"], sv["y3"], dgt, dup, row(ffn2_norm, l), mod[l, 7], g2, u2)
        g_g2, g_u2, g_d2 = tn_mm(dgt, sv["h3"][None], chip1), tn_mm(dup, sv["h3"][None], chip1), nn_mm(sv["a3"], dy, chip1)
        dy2, dyp, dom, dg2 = out_proj_bwd(dx, sv["y2"], mod[l, 5], wout)
        g_wout = nn_mm(sv["ycat"], dy2, chip1)
        qh, kh, vh = sv["qkv"]
        dqh, dkh, dvh = attn_bwd(qh, kh, vh, dom)
        dcq, dckv, dkr_in, gq, gkv, dqan_l[l], dkvan_l[l] = mla_qkv_bwd(
            dqh, dkh, dvh, sv["cq"], sv["ckv"], row(q_a_norm, l), row(kv_a_norm, l), wq, wkv, cos, sin, rot_t)
        g_wq, g_wkv = tn_mm(gq, sv["ql"][None], chip1), tn_mm(sv["kvl"][None], gkv, chip1)
        du, dpw[l], dps[l] = pool_bwd(dyp, sv["diff"], pool_w[l], row(pool_scale, l))
        dx, dz, dvec2 = mix_in_bwd(dx, du, dcq, dckv, dkr_in, sv["x1"], row(mix_norm, l), mod[l, 4], win)
        g_win = nn_mm(dz.reshape(N_CHIPS, -1, dz.shape[1]), sv["h2"], chip1)
        dnorm2[l], dnorm3[l] = dvec2[3], dvec3[3]
        dmod_rest = jnp.concatenate([dvec2[0:2], dg2, dvec3[0:3]], axis=0)
        if l == 0:
            early = small_gather("early", [jnp.stack(dmods[1:]), dmod_rest, jnp.stack(dnorm1[1:]), jnp.stack(dnorm2),
                                           jnp.stack(dnorm3), d_final_norm, jnp.stack(dps), jnp.stack(dqan_l),
                                           jnp.stack(dkvan_l), jnp.stack(dpw)], dx)
        order = checkpoint(f"{l}a", l, rest_pos, [g_win, g_wq, g_wkv, g_wout, g_g2, g_u2, g_d2], dx,
                           early[0][4] if l == 0 else None)
        dy, dgt, dup = ffn_bwd_act(dx, sv["sl1"], sv["dsu1"], mod[l, 2] + order, d1)
        dx, dvec1 = ffn_bwd_in(dx, sv["x0"], sv["y1"], dgt, dup, row(ffn1_norm, l), mod[l, 1], g1, u1)
        g_g1, g_u1, g_d1 = tn_mm(dgt, sv["h1"][None], chip1), tn_mm(dup, sv["h1"][None], chip1), nn_mm(sv["a1"], dy, chip1)
        dmods[l] = jnp.concatenate([dvec1[0:3], dmod_rest], axis=0)
        dnorm1[l] = dvec1[3]
        if l == 0:
            late = small_gather("late", [dvec1[0:3], dvec1[3]], dx)
        order = checkpoint(f"{l}b", l, ffn1_pos, [g_g1, g_u1, g_d1], dx, late[0][4] if l == 0 else None)

    to_chips(stages[-1], stages[-2]["scatter"][4], stages[-2]["scatter"][4])
    sent = stages[-1]["scatter"][4]
    got_early = gather8_wait("early", *early[0][:4], sent)
    got_late = gather8_wait("late", *late[0][:4], sent)
    each_rest, each0_rest = _unpack_rows(got_early, early[1])[:2]
    each0_first = _unpack_rows(got_late, late[1])[0]
    dmod_all = jnp.concatenate([jnp.concatenate([each0_first, each0_rest], axis=1)[:, None], each_rest], axis=1)
    dmod_all = dmod_all.reshape(N_DEV, n_layers, 9 * d)
    dmod_loc = lax.dynamic_slice_in_dim(dmod_all, chip * ada_cols, ada_cols, axis=2)
    dmod16 = jnp.pad(jnp.transpose(dmod_loc, (1, 0, 2)), ((0, 0), (0, 8), (0, 0)))

    weights = [ada_w, ada_b, ffn1_norm, ffn1_w_gate, ffn1_w_up, ffn1_w_down, mix_norm, w_in, pool_w, pool_scale,
               q_a_norm, w_q_b, kv_a_norm, w_kv_b, w_out, ffn2_norm, ffn2_w_gate, ffn2_w_up, ffn2_w_down, final_norm]
    ms = [m_ada_w, m_ada_b, m_ffn1_norm, m_ffn1_w_gate, m_ffn1_w_up, m_ffn1_w_down, m_mix_norm, m_w_in, m_pool_w,
          m_pool_scale, m_q_a_norm, m_w_q_b, m_kv_a_norm, m_w_kv_b, m_w_out, m_ffn2_norm, m_ffn2_w_gate, m_ffn2_w_up,
          m_ffn2_w_down, m_final_norm]
    vs = [v_ada_w, v_ada_b, v_ffn1_norm, v_ffn1_w_gate, v_ffn1_w_up, v_ffn1_w_down, v_mix_norm, v_w_in, v_pool_w,
          v_pool_scale, v_q_a_norm, v_w_q_b, v_kv_a_norm, v_w_kv_b, v_w_out, v_ffn2_norm, v_ffn2_w_gate, v_ffn2_w_up,
          v_ffn2_w_down, v_final_norm]
    transposed = (3, 4, 7, 11, 16, 17)
    outs = [None] * len(weights)

    outs[0] = adamw(ada_w, ada_bwd(c16, dmod16), m_ada_w, v_ada_w)
    for job in stages[-3:]:
        finish(job, outs[0][1])
    fill_a = fill_start([reduced[t] for t in rest_pos], "a", outs[0][1])
    fill_b = fill_start([reduced[t] for t in ffn1_pos], "b", fill_a[3])

    (g_dmod_rest, g_dmod0_rest, g_n1_rest, g_n2, g_n3, g_fn, g_ps, g_qan, g_kvan, g_pw) = _unpack_rows(
        sum_devices(got_early, fill_b[3]), early[1])
    g_dmod0_first, g_n1_first = _unpack_rows(sum_devices(got_late, fill_b[3]), late[1])
    g_ada_b = jnp.concatenate([jnp.concatenate([g_dmod0_first, g_dmod0_rest], axis=0)[None], g_dmod_rest], axis=0)
    g_n1 = jnp.concatenate([g_n1_first[None], g_n1_rest], axis=0)
    grads = [None, g_ada_b, g_n1, None, None, None, g_n2, None, g_pw, g_ps, g_qan, None, g_kvan, None, None, g_n3,
             None, None, None, g_fn]
    big = [i for i, g in enumerate(grads) if g is None and i > 0]
    for i, (w, g, m, v) in enumerate(zip(weights, grads, ms, vs)):
        if g is not None:
            outs[i] = adamw(w, g.reshape(w.shape), m, v)

    def update(positions, fly, after):
        filled = fill_wait(fly[0], fly[1], fly[2], fly[3], after)
        for t, g in zip(positions, filled):
            i = big[t]
            if i in transposed:
                outs[i] = tuple(tr(o) for o in adamw(tr(weights[i]), g, tr(ms[i]), tr(vs[i]), copy_g=True))
            else:
                outs[i] = adamw(weights[i], g, ms[i], vs[i], copy_g=True)

    update(rest_pos, ("a",) + tuple(fill_a[:3]), outs[8][1])
    update(ffn1_pos, ("b",) + tuple(fill_b[:3]), outs[big[rest_pos[-1]]][1])
    return (loss, dx.reshape(x.shape), *[t[0] for t in outs], *[t[1] for t in outs], *[t[2] for t in outs],
            *[t[3] for t in outs])
```

```python
import math

import jax
import jax.numpy as jnp
from jax import lax
from jax.experimental import pallas as pl
from jax.experimental.pallas import tpu as pltpu

F32 = jnp.float32
BF16 = jnp.bfloat16
MESH = pl.DeviceIdType.MESH

EPS = 1e-6
ROPE_THETA = 10000.0
N_HEADS = 4
QK_NOPE = 128
QK_ROPE = 64
V_HEAD = 128
POOL_WINDOWS = (2, 4, 8, 16)
POOL_GC = 128
POOL_WIDTH = POOL_GC * len(POOL_WINDOWS)
Q_LORA = 384
KV_LORA = 256
SOFTMAX_SCALE = 1.0 / math.sqrt(QK_NOPE + QK_ROPE)
N_CHIPS = 4
N_DEV = 8

ADAM_LR = 0.001
ADAM_B1 = 0.9
ADAM_B2 = 0.999
ADAM_EPS = 1e-08
ADAM_WD = 0.01
ADAM_STEP = 10

ROW_TILE = 512
ATT_TILE = 512
VMEM_LIMIT = 56 * 1024 * 1024
BF16_ROWS = 16
LANES = 128


def _params(sem=None, vmem=VMEM_LIMIT):
    return pltpu.CompilerParams(dimension_semantics=sem, vmem_limit_bytes=vmem)


def _dot(a, b):
    return jnp.dot(a, b, preferred_element_type=F32)


def _dot_nt(a, b):
    return lax.dot_general(a, b, (((1,), (1,)), ((), ())), preferred_element_type=F32)


def _dot_tn(a, b):
    return lax.dot_general(a, b, (((0,), (0,)), ((), ())), preferred_element_type=F32)


def _dot_exact(t, perm):
    t1 = t.astype(BF16)
    r1 = t - t1.astype(F32)
    t2 = r1.astype(BF16)
    t3 = (r1 - t2.astype(F32)).astype(BF16)
    return _dot(t1, perm) + _dot(t2, perm) + _dot(t3, perm)


def _sum0(a):
    return jnp.sum(a, axis=0, keepdims=True)


def _rms(xt):
    r = lax.rsqrt(jnp.mean(xt * xt, axis=-1, keepdims=True) + EPS)
    return xt * r, r


def _rms_bwd(dy, xt, g):
    xhat, r = _rms(xt)
    dxhat = dy * g
    dx = r * (dxhat - xhat * jnp.mean(dxhat * xhat, axis=-1, keepdims=True))
    return dx, _sum0(dy * xhat)


def _normmod_bwd(dh, xt, gn, sc):
    xhat, _ = _rms(xt)
    dn = dh * (1.0 + sc)
    dx, dgn = _rms_bwd(dn, xt, gn)
    return dx, _sum0(dh), _sum0(dh * (xhat * gn)), dgn


def _row_tile(s):
    return min(s, ROW_TILE)


def _full(shape):
    n = len(shape)
    return pl.BlockSpec(shape, lambda *_: (0,) * n)


def _resident(shape):
    n = len(shape)
    return pl.BlockSpec(shape, lambda *_: (0,) * n, pipeline_mode=pl.Buffered(1))


def ffn_fwd(x, gn, sh, sc, gt, wg, wu, wd):
    s, d = x.shape
    k_chunks, fs, _ = wg.shape
    tm = _row_tile(s)

    def body(x_ref, gn_ref, sh_ref, sc_ref, gt_ref, wg_ref, wu_ref, wd_ref,
             xo_ref, h_ref, a_ref, sl_ref, dsu_ref, y_ref):
        xt = x_ref[...]
        xhat, _ = _rms(xt)
        h = (xhat * gn_ref[...] * (1.0 + sc_ref[...]) + sh_ref[...]).astype(BF16)
        h_ref[...] = h
        y = jnp.zeros((tm, d), F32)
        for k in range(k_chunks):
            gate = _dot_nt(h, wg_ref[k])
            up = _dot_nt(h, wu_ref[k])
            sg = jax.nn.sigmoid(gate)
            sl = gate * sg
            a = (sl * up).astype(BF16)
            a_ref[k] = a.T
            sl_ref[k] = sl.astype(BF16)
            dsu_ref[k] = (up * (sg * (1.0 + gate * (1.0 - sg)))).astype(BF16)
            y += _dot(a, wd_ref[k])
        y_ref[...] = y.astype(BF16)
        xo_ref[...] = xt + 0.5 * gt_ref[...] * y

    row = pl.BlockSpec((tm, d), lambda i: (i, 0))
    vec = pl.BlockSpec((1, d), lambda i: (0, 0))
    act = pl.BlockSpec((k_chunks, tm, fs), lambda i: (0, i, 0))
    act_shape = jax.ShapeDtypeStruct((k_chunks, s, fs), BF16)
    return pl.pallas_call(
        body, name="ffn_fwd",
        grid=(s // tm,),
        in_specs=[row, vec, vec, vec, vec, _resident(wg.shape), _resident(wu.shape), _resident(wd.shape)],
        out_specs=[row, row, pl.BlockSpec((k_chunks, fs, tm), lambda i: (0, 0, i)), act, act, row],
        out_shape=[jax.ShapeDtypeStruct((s, d), F32), jax.ShapeDtypeStruct((s, d), BF16),
                   jax.ShapeDtypeStruct((k_chunks, fs, s), BF16), act_shape, act_shape,
                   jax.ShapeDtypeStruct((s, d), BF16)],
        compiler_params=_params(("arbitrary",)),
    )(x, gn, sh, sc, gt, wg, wu, wd)


def ffn_bwd_act(dxn, sl, dsu, gt, wd):
    s, d = dxn.shape
    k_chunks, fs, _ = wd.shape
    tm = _row_tile(s)

    def body(dxn_ref, sl_ref, dsu_ref, gt_ref, wd_ref, dy_ref, dgate_ref, dup_ref):
        dy = (0.5 * gt_ref[...] * dxn_ref[...]).astype(BF16)
        dy_ref[...] = dy
        for k in range(k_chunks):
            da = _dot_nt(dy, wd_ref[k])
            dgate_ref[k] = (da * dsu_ref[k].astype(F32)).astype(BF16)
            dup_ref[k] = (da * sl_ref[k].astype(F32)).astype(BF16)

    row = pl.BlockSpec((tm, d), lambda i: (i, 0))
    act = pl.BlockSpec((k_chunks, tm, fs), lambda i: (0, i, 0))
    act_shape = jax.ShapeDtypeStruct((k_chunks, s, fs), BF16)
    return pl.pallas_call(
        body, name="ffn_bwd_act",
        grid=(s // tm,),
        in_specs=[row, act, act, pl.BlockSpec((1, d), lambda i: (0, 0)), _resident(wd.shape)],
        out_specs=[row, act, act],
        out_shape=[jax.ShapeDtypeStruct((s, d), BF16), act_shape, act_shape],
        compiler_params=_params(("arbitrary",)),
    )(dxn, sl, dsu, gt, wd)


def ffn_bwd_in(dxn, x, y, dgate, dup, gn, sc, wg, wu):
    s, d = x.shape
    k_chunks, fs, _ = wg.shape
    tm = _row_tile(s)

    def body(dxn_ref, x_ref, y_ref, dgate_ref, dup_ref, gn_ref, sc_ref, wg_ref, wu_ref, dx_ref, dvec_ref):
        i = pl.program_id(0)

        @pl.when(i == 0)
        def _():
            dvec_ref[...] = jnp.zeros_like(dvec_ref)

        dh = jnp.zeros((tm, d), F32)
        for k in range(k_chunks):
            dh += _dot(dgate_ref[k], wg_ref[k]) + _dot(dup_ref[k], wu_ref[k])
        dxn_t = dxn_ref[...]
        dx, dsh, dsc, dgn = _normmod_bwd(dh, x_ref[...], gn_ref[...], sc_ref[...])
        dx_ref[...] = dx + dxn_t
        dvec_ref[0:1, :] += dsh
        dvec_ref[1:2, :] += dsc
        dvec_ref[2:3, :] += _sum0(0.5 * dxn_t * y_ref[...].astype(F32))
        dvec_ref[3:4, :] += dgn

    row = pl.BlockSpec((tm, d), lambda i: (i, 0))
    vec = pl.BlockSpec((1, d), lambda i: (0, 0))
    act = pl.BlockSpec((k_chunks, tm, fs), lambda i: (0, i, 0))
    return pl.pallas_call(
        body, name="ffn_bwd_in",
        grid=(s // tm,),
        in_specs=[row, row, row, act, act, vec, vec, _resident(wg.shape), _resident(wu.shape)],
        out_specs=[row, pl.BlockSpec((8, d), lambda i: (0, 0))],
        out_shape=[jax.ShapeDtypeStruct((s, d), F32), jax.ShapeDtypeStruct((8, d), F32)],
        compiler_params=_params(("arbitrary",)),
    )(dxn, x, y, dgate, dup, gn, sc, wg, wu)


def _grad_mm(dot, a, b, a_spec, b_spec, g, m, n, chip, name):
    def body(c_ref, a_ref, b_ref, own_ref, all_ref):
        res = dot(a_ref[...], b_ref[...])
        all_ref[...] = res.astype(BF16)

        @pl.when(pl.program_id(0) == c_ref[0])
        def _():
            own_ref[...] = res

    return pl.pallas_call(
        body, name=name,
        grid_spec=pltpu.PrefetchScalarGridSpec(
            num_scalar_prefetch=1, grid=(g,), in_specs=[a_spec, b_spec],
            out_specs=[pl.BlockSpec((m, n), lambda gi, c: (0, 0)), pl.BlockSpec((None, m, n), lambda gi, c: (gi, 0, 0))]),
        out_shape=[jax.ShapeDtypeStruct((m, n), F32), jax.ShapeDtypeStruct((g, m, n), BF16)],
        compiler_params=_params(("arbitrary",)),
    )(chip, a, b)


def tn_mm_pair(a1, a2, b, chip):
    g, s, m = a1.shape
    n = b.shape[1]

    def body(c_ref, a1_ref, a2_ref, b_ref, own1_ref, all1_ref, own2_ref, all2_ref):
        gi = pl.program_id(0)

        def one(a_ref, own_ref, all_ref, slot):
            res = _dot_tn(a_ref[...], b_ref[...])
            all_ref[...] = res.astype(BF16)

            @pl.when(slot == c_ref[0])
            def _():
                own_ref[...] = res

        @pl.when(gi < g)
        def _():
            one(a1_ref, own1_ref, all1_ref, gi)

        @pl.when(gi >= g)
        def _():
            one(a2_ref, own2_ref, all2_ref, gi - g)

    first = lambda gi, c: (jnp.minimum(gi, g - 1), 0, 0)
    second = lambda gi, c: (jnp.maximum(gi - g, 0), 0, 0)
    own = pl.BlockSpec((m, n), lambda gi, c: (0, 0))
    outs = pl.pallas_call(
        body, name="tn_mm_pair",
        grid_spec=pltpu.PrefetchScalarGridSpec(
            num_scalar_prefetch=1, grid=(2 * g,),
            in_specs=[pl.BlockSpec((None, s, m), first), pl.BlockSpec((None, s, m), second),
                      pl.BlockSpec((s, n), lambda gi, c: (0, 0))],
            out_specs=[own, pl.BlockSpec((None, m, n), first), own, pl.BlockSpec((None, m, n), second)]),
        out_shape=[jax.ShapeDtypeStruct((m, n), F32), jax.ShapeDtypeStruct((g, m, n), BF16)] * 2,
        compiler_params=_params(("arbitrary",)),
    )(chip, a1, a2, b)
    return (outs[0], outs[1]), (outs[2], outs[3])


def nn_mm(a_t, b, chip):
    g, m, s = a_t.shape
    n = b.shape[1]
    return _grad_mm(_dot, a_t, b, pl.BlockSpec((None, m, s), lambda gi, c: (gi, 0, 0)),
                    pl.BlockSpec((s, n), lambda gi, c: (0, 0)), g, m, n, chip, "nn_mm")


def tn_mm(a, b, chip):
    ga, s, m = a.shape
    gb, _, n = b.shape
    a_spec = pl.BlockSpec((None, s, m), (lambda gi, c: (gi, 0, 0)) if ga > 1 else (lambda gi, c: (0, 0, 0)))
    b_spec = pl.BlockSpec((None, s, n), (lambda gi, c: (gi, 0, 0)) if gb > 1 else (lambda gi, c: (0, 0, 0)))
    return _grad_mm(_dot_tn, a, b, a_spec, b_spec, max(ga, gb), m, n, chip, "tn_mm")


def mix_in_fwd(x, gn, sh, sc, w_in_t):
    s, d = x.shape
    tm = _row_tile(s)
    o1, o2, o3 = POOL_WIDTH, POOL_WIDTH + Q_LORA, POOL_WIDTH + Q_LORA + KV_LORA

    def body(x_ref, gn_ref, sh_ref, sc_ref, w_ref, h_ref, u_ref, cq_ref, ckv_ref, kr_ref):
        xhat, _ = _rms(x_ref[...])
        h = (xhat * gn_ref[...] * (1.0 + sc_ref[...]) + sh_ref[...]).astype(BF16)
        h_ref[...] = h
        z = _dot_nt(h, w_ref[0:o3, :])
        u_ref[...] = z[:, 0:o1]
        cq_ref[...] = z[:, o1:o2]
        ckv_ref[...] = z[:, o2:o3]
        kr_ref[...] = _dot_nt(h, w_ref[o3:, :])

    row = lambda w: pl.BlockSpec((tm, w), lambda i: (i, 0))
    vec = pl.BlockSpec((1, d), lambda i: (0, 0))
    return pl.pallas_call(
        body, name="mix_in_fwd",
        grid=(s // tm,),
        in_specs=[row(d), vec, vec, vec, _full(w_in_t.shape)],
        out_specs=[row(d), row(POOL_WIDTH), row(Q_LORA), row(KV_LORA), row(QK_ROPE)],
        out_shape=[jax.ShapeDtypeStruct((s, d), BF16), jax.ShapeDtypeStruct((s, POOL_WIDTH), F32),
                   jax.ShapeDtypeStruct((s, Q_LORA), F32), jax.ShapeDtypeStruct((s, KV_LORA), F32),
                   jax.ShapeDtypeStruct((s, QK_ROPE), F32)],
        compiler_params=_params(("arbitrary",)),
    )(x, gn, sh, sc, w_in_t)


def mix_in_bwd(dxn, du, dcq, dckv, dkr, x, gn, sc, w_in_t):
    s, d = x.shape
    tm = _row_tile(s)
    o1, o2, o3 = POOL_WIDTH, POOL_WIDTH + Q_LORA, POOL_WIDTH + Q_LORA + KV_LORA
    n_z = w_in_t.shape[0]

    def body(dxn_ref, du_ref, dcq_ref, dckv_ref, dkr_ref, x_ref, gn_ref, sc_ref, w_ref, dx_ref, dz_ref, dvec_ref):
        i = pl.program_id(0)

        @pl.when(i == 0)
        def _():
            dvec_ref[...] = jnp.zeros_like(dvec_ref)

        dub = du_ref[...].astype(BF16)
        dqb = dcq_ref[...].astype(BF16)
        dkb = dckv_ref[...].astype(BF16)
        drb = dkr_ref[...].astype(BF16)
        dz_ref[0:o1, :] = dub.T
        dz_ref[o1:o2, :] = dqb.T
        dz_ref[o2:o3, :] = dkb.T
        dz_ref[o3:, :] = drb.T
        dh = (_dot(dub, w_ref[0:o1, :]) + _dot(dqb, w_ref[o1:o2, :]) + _dot(dkb, w_ref[o2:o3, :])
              + _dot(drb, w_ref[o3:, :]))
        dx, dsh, dsc, dgn = _normmod_bwd(dh, x_ref[...], gn_ref[...], sc_ref[...])
        dx_ref[...] = dx + dxn_ref[...]
        dvec_ref[0:1, :] += dsh
        dvec_ref[1:2, :] += dsc
        dvec_ref[3:4, :] += dgn

    row = lambda w: pl.BlockSpec((tm, w), lambda i: (i, 0))
    vec = pl.BlockSpec((1, d), lambda i: (0, 0))
    return pl.pallas_call(
        body, name="mix_in_bwd",
        grid=(s // tm,),
        in_specs=[row(d), row(POOL_WIDTH), row(Q_LORA), row(KV_LORA), row(QK_ROPE), row(d), vec, vec,
                  _full(w_in_t.shape)],
        out_specs=[row(d), pl.BlockSpec((n_z, tm), lambda i: (0, i)), pl.BlockSpec((8, d), lambda i: (0, 0))],
        out_shape=[jax.ShapeDtypeStruct((s, d), F32), jax.ShapeDtypeStruct((n_z, s), BF16),
                   jax.ShapeDtypeStruct((8, d), F32)],
        compiler_params=_params(("arbitrary",)),
    )(dxn, du, dcq, dckv, dkr, x, gn, sc, w_in_t)


def _window_sum(a, w, rows, forward):
    s = a.shape[0]
    step = 1
    while step < w:
        if forward:
            shifted = jnp.where(rows < s - step, pltpu.roll(a, s - step, 0), 0.0)
        else:
            shifted = jnp.where(rows >= step, pltpu.roll(a, step, 0), 0.0)
        a = a + shifted
        step *= 2
    return a


def pool_fwd(u, pool_w, pool_scale):
    s = u.shape[0]

    def body(u_ref, w_ref, sc_ref, y_ref, diff_ref):
        rows = lax.broadcasted_iota(jnp.int32, (s, POOL_GC), 0)
        for g, w in enumerate(POOL_WINDOWS):
            cols = slice(g * POOL_GC, (g + 1) * POOL_GC)
            ug = u_ref[:, cols]
            cnt = jnp.minimum(rows + 1, w).astype(F32)
            diff = (_window_sum(ug, w, rows, False) / cnt - ug).astype(BF16)
            diff_ref[:, cols] = diff
            y_ref[:, cols] = _dot(diff, w_ref[g].astype(BF16)) * sc_ref[:, cols]

    return pl.pallas_call(
        body, name="pool_fwd",
        out_shape=[jax.ShapeDtypeStruct(u.shape, F32), jax.ShapeDtypeStruct(u.shape, BF16)],
        compiler_params=_params(),
    )(u, pool_w, pool_scale)


def pool_bwd(dy, diff, pool_w, pool_scale):
    s = dy.shape[0]

    def body(dy_ref, diff_ref, w_ref, sc_ref, du_ref, dw_ref, dsc_ref):
        rows = lax.broadcasted_iota(jnp.int32, (s, POOL_GC), 0)
        for g, w in enumerate(POOL_WINDOWS):
            cols = slice(g * POOL_GC, (g + 1) * POOL_GC)
            dyg = dy_ref[:, cols]
            diff = diff_ref[:, cols]
            wb = w_ref[g].astype(BF16)
            dsc_ref[:, cols] = _sum0(dyg * _dot(diff, wb))
            dys = (dyg * sc_ref[:, cols]).astype(BF16)
            dw_ref[g] = _dot_tn(diff, dys)
            ddiff = _dot_nt(dys, wb)
            cnt = jnp.minimum(rows + 1, w).astype(F32)
            du_ref[:, cols] = _window_sum(ddiff / cnt, w, rows, True) - ddiff

    return pl.pallas_call(
        body, name="pool_bwd",
        out_shape=[jax.ShapeDtypeStruct(dy.shape, F32), jax.ShapeDtypeStruct(pool_w.shape, F32),
                   jax.ShapeDtypeStruct(pool_scale.shape, F32)],
        compiler_params=_params(),
    )(dy, diff, pool_w, pool_scale)


def mla_qkv_fwd(cq, ckv, kr, qan, kvan, wq, wkv, cos, sin, rot):
    s = cq.shape[0]
    tm = _row_tile(s)

    def body(cq_ref, ckv_ref, kr_ref, qan_ref, kvan_ref, wq_ref, wkv_ref, cos_ref, sin_ref, rot_ref,
             q_ref, k_ref, v_ref, ql_ref, kvl_ref):
        cos_t = cos_ref[...]
        sin_t = sin_ref[...]
        perm = rot_ref[...]

        def rope(t):
            return t * cos_t + _dot_exact(t, perm) * sin_t

        qhat, _ = _rms(cq_ref[...])
        ql = (qhat * qan_ref[...]).astype(BF16)
        ql_ref[...] = ql
        khat, _ = _rms(ckv_ref[...])
        kvl = (khat * kvan_ref[...]).astype(BF16)
        kvl_ref[...] = kvl
        krr = rope(kr_ref[...]).astype(BF16)
        for h in range(N_HEADS):
            q = _dot_nt(ql, wq_ref[h])
            q_ref[h, :, 0:QK_NOPE] = q[:, 0:QK_NOPE].astype(BF16)
            q_ref[h, :, QK_NOPE:] = rope(q[:, QK_NOPE:]).astype(BF16)
            kv = _dot(kvl, wkv_ref[h])
            k_ref[h, :, 0:QK_NOPE] = kv[:, 0:QK_NOPE].astype(BF16)
            k_ref[h, :, QK_NOPE:] = krr
            v_ref[h] = kv[:, QK_NOPE:].astype(BF16)

    row = lambda w: pl.BlockSpec((tm, w), lambda i: (i, 0))
    hrow = lambda w: pl.BlockSpec((N_HEADS, tm, w), lambda i: (0, i, 0))
    qk = QK_NOPE + QK_ROPE
    return pl.pallas_call(
        body, name="mla_qkv_fwd",
        grid=(s // tm,),
        in_specs=[row(Q_LORA), row(KV_LORA), row(QK_ROPE), _full(qan.shape), _full(kvan.shape),
                  _full(wq.shape), _full(wkv.shape), row(QK_ROPE), row(QK_ROPE), _full(rot.shape)],
        out_specs=[hrow(qk), hrow(qk), hrow(V_HEAD), row(Q_LORA), row(KV_LORA)],
        out_shape=[jax.ShapeDtypeStruct((N_HEADS, s, qk), BF16), jax.ShapeDtypeStruct((N_HEADS, s, qk), BF16),
                   jax.ShapeDtypeStruct((N_HEADS, s, V_HEAD), BF16), jax.ShapeDtypeStruct((s, Q_LORA), BF16),
                   jax.ShapeDtypeStruct((s, KV_LORA), BF16)],
        compiler_params=_params(("arbitrary",)),
    )(cq, ckv, kr, qan, kvan, wq, wkv, cos, sin, rot)


def _attn_probs(q_ref, k_ref, qi, tq):
    n = (qi + 1) * tq
    rows = slice(qi * tq, n)
    sc = _dot_nt(q_ref[rows, :], k_ref[0:n, :]) * SOFTMAX_SCALE
    qpos = qi * tq + lax.broadcasted_iota(jnp.int32, (tq, n), 0)
    kpos = lax.broadcasted_iota(jnp.int32, (tq, n), 1)
    sc = jnp.where(qpos >= kpos, sc, -1e30)
    e = jnp.exp(sc - jnp.max(sc, axis=-1, keepdims=True))
    return e * (1.0 / jnp.sum(e, axis=-1, keepdims=True))


def attn_fwd(q, k, v):
    nh, s, qk = q.shape
    tq = min(s, ATT_TILE)

    def body(q_ref, k_ref, v_ref, o_ref):
        for qi in range(s // tq):
            n = (qi + 1) * tq
            p = _attn_probs(q_ref, k_ref, qi, tq).astype(BF16)
            o_ref[qi * tq:n, :] = _dot(p, v_ref[0:n, :])

    head = lambda w: pl.BlockSpec((None, s, w), lambda h: (h, 0, 0))
    return pl.pallas_call(
        body, name="attn_fwd",
        grid=(nh,),
        in_specs=[head(qk), head(qk), head(V_HEAD)],
        out_specs=pl.BlockSpec((s, V_HEAD), lambda h: (0, h)),
        out_shape=jax.ShapeDtypeStruct((s, nh * V_HEAD), F32),
        compiler_params=_params(("arbitrary",)),
    )(q, k, v)


def attn_bwd(q, k, v, do):
    nh, s, qk = q.shape
    tq = min(s, ATT_TILE)

    def body(q_ref, k_ref, v_ref, do_ref, dq_ref, dk_ref, dv_ref):
        dk_ref[...] = jnp.zeros_like(dk_ref)
        dv_ref[...] = jnp.zeros_like(dv_ref)
        for qi in range(s // tq):
            n = (qi + 1) * tq
            rows = slice(qi * tq, n)
            p = _attn_probs(q_ref, k_ref, qi, tq)
            dob = do_ref[rows, :].astype(BF16)
            dp = _dot_nt(dob, v_ref[0:n, :])
            ds = (p * (dp - jnp.sum(p * dp, axis=-1, keepdims=True)) * SOFTMAX_SCALE).astype(BF16)
            dq_ref[rows, :] = _dot(ds, k_ref[0:n, :])
            dk_ref[0:n, :] += _dot_tn(ds, q_ref[rows, :])
            dv_ref[0:n, :] += _dot_tn(p.astype(BF16), dob)

    head = lambda w: pl.BlockSpec((None, s, w), lambda h: (h, 0, 0))
    return pl.pallas_call(
        body, name="attn_bwd",
        grid=(nh,),
        in_specs=[head(qk), head(qk), head(V_HEAD), pl.BlockSpec((s, V_HEAD), lambda h: (0, h))],
        out_specs=[head(qk), head(qk), head(V_HEAD)],
        out_shape=[jax.ShapeDtypeStruct((nh, s, qk), F32), jax.ShapeDtypeStruct((nh, s, qk), F32),
                   jax.ShapeDtypeStruct((nh, s, V_HEAD), F32)],
        compiler_params=_params(("arbitrary",)),
    )(q, k, v, do)


def mla_qkv_bwd(dq, dk, dv, cq, ckv, qan, kvan, wq, wkv, cos, sin, rot_t):
    s = cq.shape[0]
    tm = _row_tile(s)

    def body(dq_ref, dk_ref, dv_ref, cq_ref, ckv_ref, qan_ref, kvan_ref,
             wq_ref, wkv_ref, cos_ref, sin_ref, rot_ref,
             dcq_ref, dckv_ref, dkro_ref, gq_ref, gkv_ref, dqan_ref, dkvan_ref):
        i = pl.program_id(0)

        @pl.when(i == 0)
        def _():
            dqan_ref[...] = jnp.zeros_like(dqan_ref)
            dkvan_ref[...] = jnp.zeros_like(dkvan_ref)

        cos_t = cos_ref[...]
        sin_t = sin_ref[...]
        perm_t = rot_ref[...]

        def unrope(t):
            return t * cos_t + _dot_exact(t * sin_t, perm_t)

        acc_q = jnp.zeros((tm, Q_LORA), F32)
        acc_kv = jnp.zeros((tm, KV_LORA), F32)
        dkr_sum = jnp.zeros((tm, QK_ROPE), F32)
        for h in range(N_HEADS):
            dq_h = dq_ref[h]
            a = dq_h[:, 0:QK_NOPE].astype(BF16)
            b = unrope(dq_h[:, QK_NOPE:]).astype(BF16)
            gq_ref[h, :, 0:QK_NOPE] = a
            gq_ref[h, :, QK_NOPE:] = b
            wq_h = wq_ref[h]
            acc_q += _dot(a, wq_h[0:QK_NOPE, :]) + _dot(b, wq_h[QK_NOPE:, :])
            dk_h = dk_ref[h]
            dk = dk_h[:, 0:QK_NOPE].astype(BF16)
            dvv = dv_ref[h].astype(BF16)
            gkv_ref[h, :, 0:QK_NOPE] = dk
            gkv_ref[h, :, QK_NOPE:] = dvv
            wkv_h = wkv_ref[h]
            acc_kv += _dot_nt(dk, wkv_h[:, 0:QK_NOPE]) + _dot_nt(dvv, wkv_h[:, QK_NOPE:])
            dkr_sum += dk_h[:, QK_NOPE:]
        dkro_ref[...] = unrope(dkr_sum)
        dcq, dqan = _rms_bwd(acc_q, cq_ref[...], qan_ref[...])
        dcq_ref[...] = dcq
        dqan_ref[...] += dqan
        dckv, dkvan = _rms_bwd(acc_kv, ckv_ref[...], kvan_ref[...])
        dckv_ref[...] = dckv
        dkvan_ref[...] += dkvan

    row = lambda w: pl.BlockSpec((tm, w), lambda i: (i, 0))
    hrow = lambda w: pl.BlockSpec((N_HEADS, tm, w), lambda i: (0, i, 0))
    return pl.pallas_call(
        body, name="mla_qkv_bwd",
        grid=(s // tm,),
        in_specs=[hrow(QK_NOPE + QK_ROPE), hrow(QK_NOPE + QK_ROPE), hrow(V_HEAD),
                  row(Q_LORA), row(KV_LORA), _full(qan.shape), _full(kvan.shape),
                  _full(wq.shape), _full(wkv.shape), row(QK_ROPE), row(QK_ROPE), _full(rot_t.shape)],
        out_specs=[row(Q_LORA), row(KV_LORA), row(QK_ROPE), hrow(QK_NOPE + QK_ROPE), hrow(QK_NOPE + V_HEAD),
                   _full(qan.shape), _full(kvan.shape)],
        out_shape=[jax.ShapeDtypeStruct((s, Q_LORA), F32), jax.ShapeDtypeStruct((s, KV_LORA), F32),
                   jax.ShapeDtypeStruct((s, QK_ROPE), F32),
                   jax.ShapeDtypeStruct((N_HEADS, s, QK_NOPE + QK_ROPE), BF16),
                   jax.ShapeDtypeStruct((N_HEADS, s, QK_NOPE + V_HEAD), BF16),
                   jax.ShapeDtypeStruct(qan.shape, F32), jax.ShapeDtypeStruct(kvan.shape, F32)],
        compiler_params=_params(("arbitrary",)),
    )(dq, dk, dv, cq, ckv, qan, kvan, wq, wkv, cos, sin, rot_t)


def out_proj_fwd(yp, om, w_out, x, gt):
    s, d = x.shape
    n_sh, rs, _ = w_out.shape
    tm = _row_tile(s)
    per = POOL_WIDTH // rs

    def body(yp_ref, om_ref, w_ref, x_ref, gt_ref, xo_ref, ycat_ref, y_ref):
        y = jnp.zeros((tm, d), F32)
        for j in range(n_sh):
            src = yp_ref if j < per else om_ref
            part = src[:, (j % per) * rs:(j % per + 1) * rs].astype(BF16)
            ycat_ref[j] = part.T
            y += _dot(part, w_ref[j])
        y_ref[...] = y.astype(BF16)
        xo_ref[...] = x_ref[...] + gt_ref[...] * y

    row = lambda w: pl.BlockSpec((tm, w), lambda i: (i, 0))
    return pl.pallas_call(
        body, name="out_proj_fwd",
        grid=(s // tm,),
        in_specs=[row(POOL_WIDTH), row(POOL_WIDTH), _full(w_out.shape), row(d), pl.BlockSpec((1, d), lambda i: (0, 0))],
        out_specs=[row(d), pl.BlockSpec((n_sh, rs, tm), lambda i: (0, 0, i)), row(d)],
        out_shape=[jax.ShapeDtypeStruct((s, d), F32), jax.ShapeDtypeStruct((n_sh, rs, s), BF16),
                   jax.ShapeDtypeStruct((s, d), BF16)],
        compiler_params=_params(("arbitrary",)),
    )(yp, om, w_out, x, gt)


def out_proj_bwd(dxn, y, gt, w_out):
    s, d = dxn.shape
    n_sh, rs, _ = w_out.shape
    tm = _row_tile(s)
    per = POOL_WIDTH // rs

    def body(dxn_ref, y_ref, gt_ref, w_ref, dy_ref, dyp_ref, dom_ref, dgt_ref):
        i = pl.program_id(0)

        @pl.when(i == 0)
        def _():
            dgt_ref[...] = jnp.zeros_like(dgt_ref)

        dxn_t = dxn_ref[...]
        dy = (gt_ref[...] * dxn_t).astype(BF16)
        dy_ref[...] = dy
        dgt_ref[...] += _sum0(dxn_t * y_ref[...].astype(F32))
        for j in range(n_sh):
            dst = dyp_ref if j < per else dom_ref
            dst[:, (j % per) * rs:(j % per + 1) * rs] = _dot_nt(dy, w_ref[j])

    row = lambda w: pl.BlockSpec((tm, w), lambda i: (i, 0))
    vec = pl.BlockSpec((1, d), lambda i: (0, 0))
    return pl.pallas_call(
        body, name="out_proj_bwd",
        grid=(s // tm,),
        in_specs=[row(d), row(d), vec, _full(w_out.shape)],
        out_specs=[row(d), row(POOL_WIDTH), row(POOL_WIDTH), vec],
        out_shape=[jax.ShapeDtypeStruct((s, d), BF16), jax.ShapeDtypeStruct((s, POOL_WIDTH), F32),
                   jax.ShapeDtypeStruct((s, POOL_WIDTH), F32), jax.ShapeDtypeStruct((1, d), F32)],
        compiler_params=_params(("arbitrary",)),
    )(dxn, y, gt, w_out)


def final_loss(x, gn, tgt):
    s, d = x.shape
    tm = _row_tile(s)

    def body(x_ref, gn_ref, t_ref, loss_ref, dx_ref, dgn_ref):
        i = pl.program_id(0)

        @pl.when(i == 0)
        def _():
            loss_ref[...] = jnp.zeros_like(loss_ref)
            dgn_ref[...] = jnp.zeros_like(dgn_ref)

        xt = x_ref[...]
        g = gn_ref[...]
        xhat, _ = _rms(xt)
        err = xhat * g - t_ref[...]
        per_tok = jnp.mean(err * err, axis=-1, keepdims=True)
        loss_ref[...] += jnp.broadcast_to(0.5 * _sum0(per_tok), loss_ref.shape)
        dx, dgn = _rms_bwd(err * (1.0 / d), xt, g)
        dx_ref[...] = dx
        dgn_ref[...] += dgn

    row = pl.BlockSpec((tm, d), lambda i: (i, 0))
    vec = pl.BlockSpec((1, d), lambda i: (0, 0))
    return pl.pallas_call(
        body, name="final_loss",
        grid=(s // tm,),
        in_specs=[row, vec, row],
        out_specs=[pl.BlockSpec((1, LANES), lambda i: (0, 0)), row, vec],
        out_shape=[jax.ShapeDtypeStruct((1, LANES), F32), jax.ShapeDtypeStruct((s, d), F32),
                   jax.ShapeDtypeStruct((1, d), F32)],
        compiler_params=_params(("arbitrary",)),
    )(x, gn, tgt)


def _col_tile(cols):
    return 768 if cols % 768 == 0 else cols


def ada_fwd(c16, ada_w, ada_b_loc):
    n_layers, d, cols = ada_w.shape
    tn = _col_tile(cols)

    def body(c_ref, w_ref, b_ref, o_ref):
        cv = c_ref[...]
        ca = (cv * jax.nn.sigmoid(cv)).astype(BF16)
        o_ref[...] = _dot(ca, w_ref[...].astype(BF16)) + b_ref[...]

    return pl.pallas_call(
        body, name="ada_fwd",
        grid=(n_layers, cols // tn),
        in_specs=[pl.BlockSpec((16, d), lambda l, j: (0, 0)), pl.BlockSpec((None, d, tn), lambda l, j: (l, 0, j)),
                  pl.BlockSpec((None, 1, tn), lambda l, j: (l, 0, j))],
        out_specs=pl.BlockSpec((None, 16, tn), lambda l, j: (l, 0, j)),
        out_shape=jax.ShapeDtypeStruct((n_layers, 16, cols), F32),
        compiler_params=_params(("arbitrary", "arbitrary")),
    )(c16, ada_w, ada_b_loc)


def ada_bwd(c16, dmod16):
    n_layers, _, cols = dmod16.shape
    d = c16.shape[1]
    tn = _col_tile(cols)

    def body(c_ref, g_ref, o_ref):
        cv = c_ref[...]
        ca = (cv * jax.nn.sigmoid(cv)).astype(BF16)
        o_ref[...] = _dot_tn(ca, g_ref[...].astype(BF16))

    return pl.pallas_call(
        body, name="ada_bwd",
        grid=(n_layers, cols // tn),
        in_specs=[pl.BlockSpec((16, d), lambda l, j: (0, 0)), pl.BlockSpec((None, 16, tn), lambda l, j: (l, 0, j))],
        out_specs=pl.BlockSpec((None, d, tn), lambda l, j: (l, 0, j)),
        out_shape=jax.ShapeDtypeStruct((n_layers, d, cols), F32),
        compiler_params=_params(("arbitrary", "arbitrary")),
    )(c16, dmod16)


def _as_rows(a):
    if a.ndim == 1:
        return a.reshape(1, a.shape[0])
    return a.reshape(-1, a.shape[-1])


def _rows_tile(r, c, itemsize=4, budget=2 * 1024 * 1024):
    if r * c * itemsize <= budget:
        return r
    best = None
    t = BF16_ROWS
    while t < r:
        if r % t == 0 and t * c * itemsize <= budget:
            best = t
        t += BF16_ROWS
    return best if best is not None else r


CAST_VMEM = 16 * 1024 * 1024


def cast_place(ws, chip, layers, after):
    _, r, c = ws[0].shape
    n_sel = len(layers)
    n_blk = len(ws) * n_sel
    tr = _rows_tile(r, c, budget=CAST_VMEM // (3 * n_blk))

    def body(chip_ref, *refs):
        for j in range(n_blk):
            refs[n_blk + 1 + j][...] = refs[j][...].astype(BF16)

    layer_spec = lambda l: pl.BlockSpec((None, tr, c), lambda i, ch: (l, i, 0))
    outs = pl.pallas_call(
        body, name="cast_place",
        grid_spec=pltpu.PrefetchScalarGridSpec(
            num_scalar_prefetch=1, grid=(r // tr,),
            in_specs=[layer_spec(l) for _ in ws for l in layers] + [pl.BlockSpec(memory_space=pl.ANY)],
            out_specs=[pl.BlockSpec((None, tr, c), lambda i, ch: (ch[0], i, 0))] * n_blk),
        out_shape=[jax.ShapeDtypeStruct((N_CHIPS, r, c), BF16)] * n_blk,
        compiler_params=_params(("arbitrary",)),
    )(chip, *[w for w in ws for _ in layers], after)
    return [list(outs[i * n_sel:(i + 1) * n_sel]) for i in range(len(ws))]


def adamw(w, g, m, v, copy_g=False):
    shape = w.shape
    w2, g2, m2, v2 = (_as_rows(t) for t in (w, g, m, v))
    r, c = w2.shape
    tr = _rows_tile(r, c, budget=3 * 1024 * 1024)
    c1 = 1.0 - ADAM_B1 ** ADAM_STEP
    c2 = 1.0 - ADAM_B2 ** ADAM_STEP

    def body(w_ref, g_ref, m_ref, v_ref, d_ref, mo_ref, vo_ref, *go_ref):
        gv = g_ref[...]
        if copy_g:
            go_ref[0][...] = gv
        mn = ADAM_B1 * m_ref[...] + (1.0 - ADAM_B1) * gv
        vn = ADAM_B2 * v_ref[...] + (1.0 - ADAM_B2) * (gv * gv)
        mo_ref[...] = mn
        vo_ref[...] = vn
        d_ref[...] = -ADAM_LR * ((mn / c1) / (jnp.sqrt(vn / c2) + ADAM_EPS) + ADAM_WD * w_ref[...])

    spec = pl.BlockSpec((tr, c), lambda i: (i, 0))
    n_out = 4 if copy_g else 3
    outs = pl.pallas_call(
        body, name="adamw", grid=(r // tr,), in_specs=[spec] * 4, out_specs=[spec] * n_out,
        out_shape=[jax.ShapeDtypeStruct((r, c), F32)] * n_out, compiler_params=_params(("arbitrary",)),
    )(w2, g2, m2, v2)
    g_out = outs[3] if copy_g else g2
    return tuple(o.reshape(shape) for o in (g_out,) + tuple(outs[:3]))


def sum_devices(a, after):
    n, r, c = a.shape
    tr = _rows_tile(r, c, budget=512 * 1024)

    def body(a_ref, after_ref, o_ref):
        acc = a_ref[0]
        for j in range(1, n):
            acc = acc + a_ref[j]
        o_ref[...] = acc

    return pl.pallas_call(
        body, name="sum_devices", grid=(r // tr,),
        in_specs=[pl.BlockSpec((n, tr, c), lambda i: (0, i, 0)), pl.BlockSpec(memory_space=pl.ANY)],
        out_specs=pl.BlockSpec((tr, c), lambda i: (i, 0)),
        out_shape=jax.ShapeDtypeStruct((r, c), F32), compiler_params=_params(("arbitrary",)),
    )(a, after)


def _split_axis(r, c):
    if (r // 2) % BF16_ROWS == 0 and r % 2 == 0:
        return 0
    assert c % (2 * LANES) == 0, (r, c)
    return 1


def _half_shape(r, c):
    return (r // 2, c) if _split_axis(r, c) == 0 else (r, c // 2)


def _half_at(ref, lead, which):
    r, c = ref.shape[-2:]
    if _split_axis(r, c) == 0:
        return ref.at[(*lead, pl.ds(which * (r // 2), r // 2), slice(None))]
    return ref.at[(*lead, slice(None), pl.ds(which * (c // 2), c // 2))]


def _half_spec(r, c, lead_block, imap):
    hr, hc = _half_shape(r, c)
    if _split_axis(r, c) == 0:
        return pl.BlockSpec((*lead_block, hr, hc), lambda *a: (*imap(*a)[0], imap(*a)[1], 0))
    return pl.BlockSpec((*lead_block, hr, hc), lambda *a: (*imap(*a)[0], 0, imap(*a)[1]))


def pair_add(owns, alls, ra_owns, ra_alls, sel):
    n = len(owns)
    n_sl = alls[0].shape[0]
    halves = [_half_shape(*g.shape) for g in owns]

    def body(s_ref, *refs):
        own_refs, all_refs, ra_own_refs, ra_all_refs, pb_refs, sum_refs = (refs[i * n:(i + 1) * n] for i in range(6))
        k = pl.program_id(0)
        for t in range(n):
            pb_refs[t][...] = (all_refs[t][...].astype(F32) + ra_all_refs[t][...].astype(F32)).astype(BF16)

            @pl.when(k == s_ref[1])
            def _(t=t):
                sum_refs[t][...] = own_refs[t][...] + ra_own_refs[t][...]

    slot = lambda hs: pl.BlockSpec((None,) + hs, lambda k, sr: (k, 0, 0))
    whole = lambda hs: pl.BlockSpec(hs, lambda k, sr: (0, 0))
    outs = pl.pallas_call(
        body, name="pair_add",
        grid_spec=pltpu.PrefetchScalarGridSpec(
            num_scalar_prefetch=1, grid=(n_sl,),
            in_specs=[_half_spec(*g.shape, (), lambda k, sr: ((), sr[0])) for g in owns]
            + [_half_spec(*g.shape[1:], (None,), lambda k, sr: ((k,), sr[0])) for g in alls]
            + [whole(hs) for hs in halves] + [slot(hs) for hs in halves],
            out_specs=[slot(hs) for hs in halves] + [whole(hs) for hs in halves]),
        out_shape=[jax.ShapeDtypeStruct((n_sl,) + hs, BF16) for hs in halves]
        + [jax.ShapeDtypeStruct(hs, F32) for hs in halves],
        compiler_params=_params(("arbitrary",)),
    )(sel, *owns, *alls, *ra_owns, *ra_alls)
    return list(outs[:n]), list(outs[n:])


def chip_sum(owns, rbs, sel, shapes, accs):
    n = len(owns)
    fresh = accs[0] is None

    def body(s_ref, *refs):
        own_refs, rb_refs, o_refs = refs[:n], refs[n:2 * n], refs[-n:]
        for t in range(n):
            acc_v = own_refs[t][...]
            for j in range(N_CHIPS - 1):
                acc_v = acc_v + rb_refs[t][j].astype(F32)
            o_refs[t][...] = acc_v

    in_specs = ([pl.BlockSpec(o.shape, lambda i, sr: (0, 0)) for o in owns]
                + [pl.BlockSpec(rb.shape, lambda i, sr: (0, 0, 0)) for rb in rbs])
    args = [sel, *owns, *rbs]
    aliases = {}
    if not fresh:
        in_specs += [pl.BlockSpec(memory_space=pl.ANY)] * n
        args += list(accs)
        aliases = {1 + 2 * n + t: t for t in range(n)}
    return list(pl.pallas_call(
        body, name="chip_sum",
        grid_spec=pltpu.PrefetchScalarGridSpec(
            num_scalar_prefetch=1, grid=(1,), in_specs=in_specs,
            out_specs=[_half_spec(*shp[1:], (None,), lambda i, sr: ((sr[2],), sr[0])) for shp in shapes]),
        out_shape=[jax.ShapeDtypeStruct(shp, F32) for shp in shapes],
        input_output_aliases=aliases,
        compiler_params=_params(("arbitrary",)),
    )(*args))


def _me():
    return lax.axis_index("x"), lax.axis_index("y"), lax.axis_index("c")


def _flip(v, bit):
    return 1 - v if bit else v


def exchange8(xs, bcast):
    blk = xs.shape if bcast else xs.shape[1:]

    def body(x_ref, o_ref, send_sems, recv_sems, loc_sem):
        mx, my, mc = _me()
        me = 4 * mx + 2 * my + mc
        src = (lambda j: x_ref) if bcast else (lambda j: x_ref.at[j])
        loc = pltpu.make_async_copy(src(me), o_ref.at[me], loc_sem)
        loc.start()
        copies = []
        for o in range(1, N_DEV):
            px, py, pc = _flip(mx, o & 4), _flip(my, o & 2), _flip(mc, o & 1)
            cp = pltpu.make_async_remote_copy(
                src_ref=src(4 * px + 2 * py + pc), dst_ref=o_ref.at[me],
                send_sem=send_sems.at[o - 1], recv_sem=recv_sems.at[o - 1],
                device_id=(px, py, pc), device_id_type=MESH)
            cp.start()
            copies.append(cp)
        for cp in copies:
            cp.wait()
        loc.wait()

    return pl.pallas_call(
        body, name="exchange8_gather" if bcast else "exchange8_a2a",
        in_specs=[pl.BlockSpec(memory_space=pltpu.VMEM)], out_specs=pl.BlockSpec(memory_space=pltpu.VMEM),
        out_shape=jax.ShapeDtypeStruct((N_DEV,) + tuple(blk), xs.dtype),
        scratch_shapes=[pltpu.SemaphoreType.DMA((N_DEV - 1,)), pltpu.SemaphoreType.DMA((N_DEV - 1,)), pltpu.SemaphoreType.DMA],
        compiler_params=_params(),
    )(xs)


HBM = pl.BlockSpec(memory_space=pltpu.HBM)
SEM = pl.BlockSpec(memory_space=pltpu.SEMAPHORE)
EFFECT = pltpu.SideEffectType.DATAFLOW_SIDE_EFFECTING


def _hbm(a):
    return pltpu.with_memory_space_constraint(a, pltpu.HBM)


def _ici_copy(land, o, send_sem, recv_sem, sending):
    mx, my, mc = _me()
    px, py = _flip(mx, o & 2), _flip(my, o & 1)
    mine = _half_at(land, (2 * mx + my,), mc)
    return pltpu.make_async_remote_copy(
        src_ref=mine, dst_ref=mine if sending else _half_at(land, (2 * px + py,), mc),
        send_sem=send_sem, recv_sem=recv_sem, device_id=(px, py, mc), device_id_type=MESH)


N_PEERS = N_CHIPS - 1
DMA_SEM = pltpu.SemaphoreType.DMA(())


def gather_start(lands, groups, after, tag):
    n_layers, n = len(lands), len(lands[0])
    flat = [a for layer in lands for a in layer]
    n_in = n * n_layers
    n_grp = len(groups)
    n_sem = 2 * n_layers * n_grp * N_PEERS
    first = lambda l, g, recv: ((l * n_grp + g) * 2 + recv) * N_PEERS

    def body(*refs):
        land = refs[:n_in]
        sems = refs[n_in + 1:n_in + 1 + n_sem]
        token = refs[-1]
        for l in range(n_layers):
            for g, members in enumerate(groups):
                for t in members:
                    for o in range(1, N_CHIPS):
                        _ici_copy(land[l * n + t], o, sems[first(l, g, 0) + o - 1], sems[first(l, g, 1) + o - 1],
                                  True).start()
        token[...] = jnp.zeros_like(token)

    outs = pl.pallas_call(
        body, name=f"gather_start_{tag}",
        in_specs=[HBM] * n_in + [pl.BlockSpec(memory_space=pl.ANY)],
        out_specs=[SEM] * n_sem + [HBM] * n_in + [pl.BlockSpec(memory_space=pltpu.VMEM)],
        out_shape=[DMA_SEM] * n_sem + [pltpu.HBM(a.shape, a.dtype) for a in flat]
        + [jax.ShapeDtypeStruct((8, LANES), F32)],
        input_output_aliases={i: i + n_sem for i in range(n_in)},
        compiler_params=pltpu.CompilerParams(has_side_effects=EFFECT),
    )(*[_hbm(a) for a in flat], after)
    sems = [[(list(outs[first(l, g, 0):first(l, g, 0) + N_PEERS]), list(outs[first(l, g, 1):first(l, g, 1) + N_PEERS]))
             for g in range(n_grp)] for l in range(n_layers)]
    lands_thru = [list(outs[n_sem + l * n:n_sem + (l + 1) * n]) for l in range(n_layers)]
    return sems, lands_thru, outs[-1]


def gather_wait(tag, sems, lands, after):
    n = len(lands)
    send_sems, recv_sems = sems

    def body(*refs):
        land = refs[:n]
        send_r = refs[n:n + N_PEERS]
        recv_r = refs[n + N_PEERS:n + 2 * N_PEERS]
        for t in range(n):
            for o in range(1, N_CHIPS):
                _ici_copy(land[t], o, send_r[o - 1], recv_r[o - 1], True).wait_send()
                _ici_copy(land[t], o, send_r[o - 1], recv_r[o - 1], False).wait_recv()

    return list(pl.pallas_call(
        body, name=f"gather_wait_{tag}",
        in_specs=[HBM] * n + [SEM] * (2 * N_PEERS) + [pl.BlockSpec(memory_space=pl.ANY)],
        out_specs=[HBM] * n,
        out_shape=[pltpu.HBM(a.shape, a.dtype) for a in lands],
        input_output_aliases={i: i for i in range(n)},
        compiler_params=pltpu.CompilerParams(has_side_effects=EFFECT),
    )(*lands, *send_sems, *recv_sems, after))


def gather_forward(lands):
    n = len(lands)

    def body(*refs):
        dst = refs[n:2 * n]
        send_sems, recv_sems = refs[2 * n:]
        mx, my, mc = _me()
        fwds = []
        for t in range(n):
            for o in range(1, N_CHIPS):
                slot = 2 * _flip(mx, o & 2) + _flip(my, o & 1)
                mine = _half_at(dst[t], (slot,), mc)
                theirs = _half_at(dst[t], (slot,), 1 - mc)
                cp = pltpu.make_async_remote_copy(
                    src_ref=mine, dst_ref=mine, send_sem=send_sems.at[t, o - 1], recv_sem=recv_sems.at[t, o - 1],
                    device_id=(mx, my, 1 - mc), device_id_type=MESH)
                cp.start()
                fwds.append((cp, pltpu.make_async_remote_copy(
                    src_ref=theirs, dst_ref=theirs, send_sem=send_sems.at[t, o - 1], recv_sem=recv_sems.at[t, o - 1],
                    device_id=(mx, my, 1 - mc), device_id_type=MESH)))
        for cp, arrival in fwds:
            cp.wait_send()
            arrival.wait_recv()

    any_spec = pl.BlockSpec(memory_space=pl.ANY)
    return list(pl.pallas_call(
        body, name="gather_forward",
        in_specs=[any_spec] * n, out_specs=[any_spec] * n,
        out_shape=[jax.ShapeDtypeStruct(a.shape, a.dtype) for a in lands],
        input_output_aliases={t: t for t in range(n)},
        scratch_shapes=[pltpu.SemaphoreType.DMA((n, N_CHIPS - 1)), pltpu.SemaphoreType.DMA((n, N_CHIPS - 1))],
        compiler_params=_params(),
    )(*lands))


def _scatter_copy(src, land, o, send_sem, recv_sem):
    mx, my, mc = _me()
    px, py = _flip(mx, o & 2), _flip(my, o & 1)
    return pltpu.make_async_remote_copy(
        src_ref=src.at[2 * px + py], dst_ref=land.at[o - 1],
        send_sem=send_sem, recv_sem=recv_sem, device_id=(px, py, mc), device_id_type=MESH)


def scatter_start(pbs, tag, after):
    n = len(pbs)
    lands = [lax.empty((N_CHIPS - 1,) + p.shape[1:], p.dtype) for p in pbs]

    def body(*refs):
        src = refs[:n]
        land = refs[n:2 * n]
        send_sems = refs[2 * n + 1:2 * n + 1 + N_PEERS]
        recv_sems = refs[2 * n + 1 + N_PEERS:2 * n + 1 + 2 * N_PEERS]
        token = refs[-1]
        for t in range(n):
            for o in range(1, N_CHIPS):
                _scatter_copy(src[t], land[t], o, send_sems[o - 1], recv_sems[o - 1]).start()
        token[...] = jnp.zeros_like(token)

    n_sem = 2 * N_PEERS
    arrs = list(pbs) + lands
    outs = pl.pallas_call(
        body, name=f"scatter_start_{tag}",
        in_specs=[HBM] * (2 * n) + [pl.BlockSpec(memory_space=pl.ANY)],
        out_specs=[SEM] * n_sem + [HBM] * (2 * n) + [pl.BlockSpec(memory_space=pltpu.VMEM)],
        out_shape=[DMA_SEM] * n_sem + [pltpu.HBM(a.shape, a.dtype) for a in arrs]
        + [jax.ShapeDtypeStruct((8, LANES), F32)],
        input_output_aliases={i: i + n_sem for i in range(2 * n)},
        compiler_params=pltpu.CompilerParams(has_side_effects=EFFECT),
    )(*[_hbm(a) for a in arrs], after)
    return (list(outs[:N_PEERS]), list(outs[N_PEERS:n_sem]), list(outs[n_sem:n_sem + n]),
            list(outs[n_sem + n:n_sem + 2 * n]), outs[-1])


def scatter_wait(tag, send_sems, recv_sems, pbs, lands, after):
    n = len(pbs)

    def body(*refs):
        src = refs[:n]
        land = refs[n:2 * n]
        send_r = refs[2 * n:2 * n + N_PEERS]
        recv_r = refs[2 * n + N_PEERS:2 * n + 2 * N_PEERS]
        for t in range(n):
            for o in range(1, N_CHIPS):
                cp = _scatter_copy(src[t], land[t], o, send_r[o - 1], recv_r[o - 1])
                cp.wait_send()
                cp.wait_recv()

    arrs = list(pbs) + list(lands)
    outs = pl.pallas_call(
        body, name=f"scatter_wait_{tag}",
        in_specs=[HBM] * (2 * n) + [SEM] * (2 * N_PEERS) + [pl.BlockSpec(memory_space=pl.ANY)],
        out_specs=[HBM] * (2 * n),
        out_shape=[pltpu.HBM(a.shape, a.dtype) for a in arrs],
        input_output_aliases={i: i for i in range(2 * n)},
        compiler_params=pltpu.CompilerParams(has_side_effects=EFFECT),
    )(*arrs, *send_sems, *recv_sems, after)
    return list(outs[n:])


def _pair_copies(srcs, lands, send_sem, recv_sem):
    mx, my, mc = _me()
    return [pltpu.make_async_remote_copy(
        src_ref=_half_at(src, (slice(None),) * (len(src.shape) - 2), 1 - mc), dst_ref=land,
        send_sem=send_sem, recv_sem=recv_sem, device_id=(mx, my, 1 - mc), device_id_type=MESH)
        for src, land in zip(srcs, lands)]


def pair_start(gs, tag, after):
    n = len(gs)
    lands = [lax.empty(g.shape[:-2] + _half_shape(*g.shape[-2:]), g.dtype) for g in gs]

    def body(*refs):
        send_sem, recv_sem = refs[2 * n + 1], refs[2 * n + 2]
        token = refs[-1]
        for cp in _pair_copies(refs[:n], refs[n:2 * n], send_sem, recv_sem):
            cp.start()
        token[...] = jnp.zeros_like(token)

    arrs = list(gs) + lands
    outs = pl.pallas_call(
        body, name=f"pair_start_{tag}",
        in_specs=[HBM] * (2 * n) + [pl.BlockSpec(memory_space=pl.ANY)],
        out_specs=[SEM, SEM] + [HBM] * (2 * n) + [pl.BlockSpec(memory_space=pltpu.VMEM)],
        out_shape=[DMA_SEM, DMA_SEM] + [pltpu.HBM(a.shape, a.dtype) for a in arrs] + [jax.ShapeDtypeStruct((8, LANES), F32)],
        input_output_aliases={i: i + 2 for i in range(2 * n)},
        compiler_params=pltpu.CompilerParams(has_side_effects=EFFECT),
    )(*[_hbm(a) for a in arrs], after)
    return outs[0], outs[1], list(outs[2:2 + n]), list(outs[2 + n:2 + 2 * n]), outs[-1]


def pair_wait(tag, send_sem, recv_sem, gs, lands, after):
    n = len(gs)

    def body(*refs):
        for cp in _pair_copies(refs[:n], refs[n:2 * n], refs[2 * n], refs[2 * n + 1]):
            cp.wait_send()
            cp.wait_recv()

    arrs = list(gs) + list(lands)
    outs = pl.pallas_call(
        body, name=f"pair_wait_{tag}",
        in_specs=[HBM] * (2 * n) + [SEM, SEM, pl.BlockSpec(memory_space=pl.ANY)],
        out_specs=[HBM] * (2 * n),
        out_shape=[pltpu.HBM(a.shape, a.dtype) for a in arrs],
        input_output_aliases={i: i for i in range(2 * n)},
        compiler_params=pltpu.CompilerParams(has_side_effects=EFFECT),
    )(*arrs, send_sem, recv_sem, after)
    return list(outs[:n]), list(outs[n:])


def _gather8_copy(x, land, o, send_sem, recv_sem, sending):
    mx, my, mc = _me()
    px, py, pc = _flip(mx, o & 4), _flip(my, o & 2), _flip(mc, o & 1)
    slot = 4 * mx + 2 * my + mc if sending else 4 * px + 2 * py + pc
    return pltpu.make_async_remote_copy(
        src_ref=x, dst_ref=land.at[slot], send_sem=send_sem, recv_sem=recv_sem,
        device_id=(px, py, pc), device_id_type=MESH)


def gather8_start(x, land, after, tag):
    n_peer = N_DEV - 1

    def body(x_ref, land_ref, after_ref, *rest):
        send_sems, recv_sems = rest[:n_peer], rest[n_peer:2 * n_peer]
        token = rest[-1]
        for o in range(1, N_DEV):
            _gather8_copy(x_ref, land_ref, o, send_sems[o - 1], recv_sems[o - 1], True).start()
        token[...] = jnp.zeros_like(token)

    outs = pl.pallas_call(
        body, name=f"gather8_start_{tag}",
        in_specs=[HBM, HBM, pl.BlockSpec(memory_space=pl.ANY)],
        out_specs=[SEM] * (2 * n_peer) + [HBM, HBM, pl.BlockSpec(memory_space=pltpu.VMEM)],
        out_shape=[DMA_SEM] * (2 * n_peer) + [pltpu.HBM(x.shape, x.dtype), pltpu.HBM(land.shape, land.dtype),
                                              jax.ShapeDtypeStruct((8, LANES), F32)],
        input_output_aliases={0: 2 * n_peer, 1: 2 * n_peer + 1},
        compiler_params=pltpu.CompilerParams(has_side_effects=EFFECT),
    )(_hbm(x), _hbm(land), after)
    return list(outs[:n_peer]), list(outs[n_peer:2 * n_peer]), outs[2 * n_peer], outs[2 * n_peer + 1], outs[-1]


def gather8_wait(tag, send_sems, recv_sems, x, land, after):
    n_peer = N_DEV - 1

    def body(x_ref, land_ref, *rest):
        send_r, recv_r = rest[:n_peer], rest[n_peer:2 * n_peer]
        for o in range(1, N_DEV):
            _gather8_copy(x_ref, land_ref, o, send_r[o - 1], recv_r[o - 1], True).wait_send()
            _gather8_copy(x_ref, land_ref, o, send_r[o - 1], recv_r[o - 1], False).wait_recv()

    return pl.pallas_call(
        body, name=f"gather8_wait_{tag}",
        in_specs=[HBM, HBM] + [SEM] * (2 * n_peer) + [pl.BlockSpec(memory_space=pl.ANY)],
        out_specs=[HBM, HBM],
        out_shape=[pltpu.HBM(x.shape, x.dtype), pltpu.HBM(land.shape, land.dtype)],
        input_output_aliases={0: 0, 1: 1},
        compiler_params=pltpu.CompilerParams(has_side_effects=EFFECT),
    )(x, land, *send_sems, *recv_sems, after)[1]


def _fill_copies(fs, send_sem, recv_sem, sending):
    mx, my, mc = _me()
    out = []
    for f in fs:
        region = _half_at(f, (slice(None),), mc if sending else 1 - mc)
        out.append(pltpu.make_async_remote_copy(
            src_ref=region, dst_ref=region, send_sem=send_sem, recv_sem=recv_sem,
            device_id=(mx, my, 1 - mc), device_id_type=MESH))
    return out


def fill_start(fs, tag, after):
    n = len(fs)

    def body(*refs):
        send_sem, recv_sem = refs[n + 1], refs[n + 2]
        token = refs[-1]
        for cp in _fill_copies(refs[:n], send_sem, recv_sem, True):
            cp.start()
        token[...] = jnp.zeros_like(token)

    outs = pl.pallas_call(
        body, name=f"fill_start_{tag}",
        in_specs=[HBM] * n + [pl.BlockSpec(memory_space=pl.ANY)],
        out_specs=[SEM, SEM] + [HBM] * n + [pl.BlockSpec(memory_space=pltpu.VMEM)],
        out_shape=[DMA_SEM, DMA_SEM] + [pltpu.HBM(f.shape, f.dtype) for f in fs] + [jax.ShapeDtypeStruct((8, LANES), F32)],
        input_output_aliases={i: i + 2 for i in range(n)},
        compiler_params=pltpu.CompilerParams(has_side_effects=EFFECT),
    )(*[_hbm(f) for f in fs], after)
    return outs[0], outs[1], list(outs[2:2 + n]), outs[-1]


def fill_wait(tag, send_sem, recv_sem, fs, after):
    n = len(fs)

    def body(*refs):
        for cp in _fill_copies(refs[:n], refs[n], refs[n + 1], True):
            cp.wait_send()
        for cp in _fill_copies(refs[:n], refs[n], refs[n + 1], False):
            cp.wait_recv()

    return list(pl.pallas_call(
        body, name=f"fill_wait_{tag}",
        in_specs=[HBM] * n + [SEM, SEM, pl.BlockSpec(memory_space=pl.ANY)],
        out_specs=[HBM] * n,
        out_shape=[pltpu.HBM(f.shape, f.dtype) for f in fs],
        input_output_aliases={i: i for i in range(n)},
        compiler_params=pltpu.CompilerParams(has_side_effects=EFFECT),
    )(*fs, send_sem, recv_sem, after))


def _pack_rows(parts, d):
    rows, spans = [], []
    at = 0
    for p in parts:
        flat = p.reshape(-1)
        n_rows = -(-flat.shape[0] // (8 * d)) * 8
        flat = jnp.pad(flat, (0, n_rows * d - flat.shape[0]))
        rows.append(flat.reshape(n_rows, d))
        spans.append((at, p.shape))
        at += n_rows
    return jnp.concatenate(rows, axis=0), spans


def _unpack_rows(packed, spans):
    lead, d = packed.shape[:-2], packed.shape[-1]
    out = []
    for at, shape in spans:
        n = math.prod(shape)
        n_rows = -(-n // d)
        out.append(packed[..., at:at + n_rows, :].reshape(lead + (-1,))[..., :n].reshape(lead + tuple(shape)))
    return out


def _rotate_half_matrix():
    half = QK_ROPE // 2
    idx = jnp.arange(QK_ROPE)
    src = jnp.where(idx < half, idx + half, idx - half)
    sign = jnp.where(idx < half, -1.0, 1.0)
    return (jnp.zeros((QK_ROPE, QK_ROPE), F32).at[src, idx].set(sign)).astype(BF16)


def kernel(x, c, positions, ada_w, ada_b, ffn1_norm, ffn1_w_gate, ffn1_w_up, ffn1_w_down, mix_norm, w_in, pool_w, pool_scale, q_a_norm, w_q_b, kv_a_norm, w_kv_b, w_out, ffn2_norm, ffn2_w_gate, ffn2_w_up, ffn2_w_down, final_norm, loss_target, m_ada_w, m_ada_b, m_ffn1_norm, m_ffn1_w_gate, m_ffn1_w_up, m_ffn1_w_down, m_mix_norm, m_w_in, m_pool_w, m_pool_scale, m_q_a_norm, m_w_q_b, m_kv_a_norm, m_w_kv_b, m_w_out, m_ffn2_norm, m_ffn2_w_gate, m_ffn2_w_up, m_ffn2_w_down, m_final_norm, v_ada_w, v_ada_b, v_ffn1_norm, v_ffn1_w_gate, v_ffn1_w_up, v_ffn1_w_down, v_mix_norm, v_w_in, v_pool_w, v_pool_scale, v_q_a_norm, v_w_q_b, v_kv_a_norm, v_w_kv_b, v_w_out, v_ffn2_norm, v_ffn2_w_gate, v_ffn2_w_up, v_ffn2_w_down, v_final_norm):
    mx, my, mc = _me()
    chip = 2 * mx + my
    half = jnp.reshape(mc, (1,)).astype(jnp.int32)
    chip1 = jnp.reshape(chip, (1,)).astype(jnp.int32)
    n_layers, d, ada_cols = ada_w.shape
    xt = x[0]
    tgt = loss_target[0]

    inv_freq = 1.0 / (ROPE_THETA ** (jnp.arange(0, QK_ROPE, 2, dtype=F32) / QK_ROPE))
    ang = positions[0].astype(F32)[:, None] * inv_freq
    ang = jnp.concatenate([ang, ang], axis=-1)
    cos, sin = jnp.cos(ang), jnp.sin(ang)
    rot = _rotate_half_matrix()
    rot_t = rot.T

    c_all = exchange8(c, True).reshape(N_DEV, d)
    c16 = jnp.pad(c_all, ((0, 8), (0, 0)))
    ada_b_loc = lax.dynamic_slice_in_dim(ada_b, chip * ada_cols, ada_cols, axis=1).reshape(n_layers, 1, ada_cols)
    mod_part = ada_fwd(c16, ada_w, ada_b_loc)[:, :N_DEV]
    mod_got = exchange8(jnp.transpose(mod_part, (1, 0, 2)), False)
    mod = jnp.transpose(mod_got.reshape(N_CHIPS, 2, n_layers, ada_cols)[:, 0], (1, 0, 2))
    mod = mod.reshape(n_layers, 9, 1, d)

    tr = lambda a: jnp.transpose(a, (0, 2, 1))
    local = [tr(ffn1_w_gate), tr(ffn1_w_up), ffn1_w_down, tr(w_in), tr(w_q_b), w_kv_b, w_out,
             tr(ffn2_w_gate), tr(ffn2_w_up), ffn2_w_down]
    ffn1_pos, mixer_pos, ffn2_pos = (0, 1, 2), (3, 4, 5, 6), (7, 8, 9)
    rest_pos = mixer_pos + ffn2_pos

    def cast_all(layers, after):
        by_shape = {}
        for t, w in enumerate(local):
            by_shape.setdefault(w.shape, []).append(t)
        out = [None] * len(local)
        for ts in by_shape.values():
            for t, per_layer in zip(ts, cast_place([local[t] for t in ts], chip1, layers, after)):
                out[t] = per_layer
        return out

    placed = cast_all((0,), mod)
    g_sems, lands_fly, g_token = gather_start([[p[0] for p in placed]], (ffn1_pos, mixer_pos, ffn2_pos), mod, "first")
    if n_layers > 1:
        later = tuple(range(1, n_layers))
        placed = cast_all(later, g_token)
        more_sems, more_fly, g_token = gather_start(
            [[p[j] for p in placed] for j in range(len(later))], (ffn1_pos, rest_pos), g_token, "rest")
        g_sems, lands_fly = g_sems + more_sems, lands_fly + more_fly
    gathered = []

    row = lambda a, l: a[l].reshape(1, -1)
    saved = []
    for l in range(n_layers):
        def fetch(tag, group, members, after, l=l):
            return gather_forward(gather_wait(tag, g_sems[l][group], [lands_fly[l][t] for t in members], after))

        g1, u1, d1 = fetch(f"{l}a", 0, ffn1_pos, xt if l else g_token)
        sv = dict(x0=xt)
        xt, sv["h1"], sv["a1"], sv["sl1"], sv["dsu1"], sv["y1"] = ffn_fwd(
            xt, row(ffn1_norm, l), mod[l, 0], mod[l, 1], mod[l, 2], g1, u1, d1)
        sv["x1"] = xt
        if l == 0:
            win, wq, wkv, wout = fetch("0b", 1, mixer_pos, xt)
        else:
            win, wq, wkv, wout, g2, u2, d2 = fetch(f"{l}b", 1, rest_pos, xt)
        win = win.reshape(-1, d)
        sv["h2"], u, cq, ckv, kr = mix_in_fwd(xt, row(mix_norm, l), mod[l, 3], mod[l, 4], win)
        sv["cq"], sv["ckv"] = cq, ckv
        yp, sv["diff"] = pool_fwd(u, pool_w[l], row(pool_scale, l))
        qh, kh, vh, sv["ql"], sv["kvl"] = mla_qkv_fwd(
            cq, ckv, kr, row(q_a_norm, l), row(kv_a_norm, l), wq, wkv, cos, sin, rot)
        sv["qkv"] = (qh, kh, vh)
        om = attn_fwd(qh, kh, vh)
        xt, sv["ycat"], sv["y2"] = out_proj_fwd(yp, om, wout, xt, mod[l, 5])
        sv["x2"] = xt
        if l == 0:
            g2, u2, d2 = fetch("0c", 2, ffn2_pos, xt)
        gathered.append([g1, u1, d1, win, wq, wkv, wout, g2, u2, d2])
        xt, sv["h3"], sv["a3"], sv["sl3"], sv["dsu3"], sv["y3"] = ffn_fwd(
            xt, row(ffn2_norm, l), mod[l, 6], mod[l, 7], mod[l, 8], g2, u2, d2)
        saved.append(sv)

    loss_vec, dx, d_final_norm = final_loss(xt, final_norm.reshape(1, d), tgt)
    loss = lax.psum(loss_vec[0, 0], ("x", "y", "c"))

    none = [None] * n_layers
    dmods, dnorm1, dnorm2, dnorm3 = list(none), list(none), list(none), list(none)
    dpw, dps, dqan_l, dkvan_l = list(none), list(none), list(none), list(none)
    reduced = [None] * len(local)
    stages = []
    sel_of = lambda l: jnp.stack([mc, chip, jnp.asarray(l, mc.dtype)]).astype(jnp.int32)

    def to_chips(job, after_wait, after_start):
        send, recv, g_fly, lands_p = job.pop("pair")
        g_fly, got = pair_wait(job["tag"], send, recv, g_fly, lands_p, after_wait)
        n_w = len(job["pos"])
        pbs, job["owns"] = pair_add(g_fly[:n_w], g_fly[n_w:], got[:n_w], got[n_w:], sel_of(job["l"]))
        job["scatter"] = scatter_start(pbs, job["tag"], after_start)
        return job["scatter"][4][0, 0]

    def finish(job, after):
        s_send, s_recv, pbs_fly, lands_j, _ = job.pop("scatter")
        parts = scatter_wait(job["tag"], s_send, s_recv, pbs_fly, lands_j, after)
        sums = chip_sum(job["owns"], parts, sel_of(job["l"]), [(n_layers,) + shp for shp in job["shapes"]],
                        [reduced[t] for t in job["pos"]])
        for t, total_t in zip(job["pos"], sums):
            reduced[t] = total_t

    def checkpoint(tag, l, positions, grads_, done, before_scatter=None):
        send, recv, g_fly, lands_p, tok = pair_start([g[0] for g in grads_] + [g[1] for g in grads_], tag, done)
        order = tok[0, 0]
        if stages:
            order = order + to_chips(stages[-1], done, done if before_scatter is None else before_scatter)
        if len(stages) >= 3:
            finish(stages[-3], done)
        stages.append(dict(tag=tag, l=l, pos=positions, shapes=[g[0].shape for g in grads_],
                           pair=(send, recv, g_fly, lands_p)))
        return order

    def small_gather(tag, parts, after):
        packed, spans = _pack_rows(parts, d)
        land = lax.dynamic_update_index_in_dim(lax.empty((N_DEV,) + packed.shape, F32), packed, 4 * mx + 2 * my + mc, 0)
        return gather8_start(packed, land, after, tag), spans

    order = None

    for l in reversed(range(n_layers)):
        sv = saved[l]
        g1, u1, d1, win, wq, wkv, wout, g2, u2, d2 = gathered[l]
        win = win.reshape(-1, d)
        gt3 = mod[l, 8] if order is None else mod[l, 8] + order
        dy, dgt, dup = ffn_bwd_act(dx, sv["sl3"], sv["dsu3"], gt3, d2)
        dx, dvec3 = ffn_bwd_in(dx, sv["x2"], sv["y3"], dgt, dup, row(ffn2_norm, l), mod[l, 7], g2, u2)
        (g_g2, g_u2), g_d2 = tn_mm_pair(dgt, dup, sv["h3"], chip1), nn_mm(sv["a3"], dy, chip1)
        dy2, dyp, dom, dg2 = out_proj_bwd(dx, sv["y2"], mod[l, 5], wout)
        g_wout = nn_mm(sv["ycat"], dy2, chip1)
        qh, kh, vh = sv["qkv"]
        dqh, dkh, dvh = attn_bwd(qh, kh, vh, dom)
        dcq, dckv, dkr_in, gq, gkv, dqan_l[l], dkvan_l[l] = mla_qkv_bwd(
            dqh, dkh, dvh, sv["cq"], sv["ckv"], row(q_a_norm, l), row(kv_a_norm, l), wq, wkv, cos, sin, rot_t)
        g_wq, g_wkv = tn_mm(gq, sv["ql"][None], chip1), tn_mm(sv["kvl"][None], gkv, chip1)
        du, dpw[l], dps[l] = pool_bwd(dyp, sv["diff"], pool_w[l], row(pool_scale, l))
        dx, dz, dvec2 = mix_in_bwd(dx, du, dcq, dckv, dkr_in, sv["x1"], row(mix_norm, l), mod[l, 4], win)
        g_win = nn_mm(dz.reshape(N_CHIPS, -1, dz.shape[1]), sv["h2"], chip1)
        dnorm2[l], dnorm3[l] = dvec2[3], dvec3[3]
        dmod_rest = jnp.concatenate([dvec2[0:2], dg2, dvec3[0:3]], axis=0)
        if l == 0:
            early = small_gather("early", [jnp.stack(dmods[1:]), dmod_rest, jnp.stack(dnorm1[1:]), jnp.stack(dnorm2),
                                           jnp.stack(dnorm3), d_final_norm, jnp.stack(dps), jnp.stack(dqan_l),
                                           jnp.stack(dkvan_l), jnp.stack(dpw)], dx)
        order = checkpoint(f"{l}a", l, rest_pos, [g_win, g_wq, g_wkv, g_wout, g_g2, g_u2, g_d2], dx,
                           early[0][4] if l == 0 else None)
        dy, dgt, dup = ffn_bwd_act(dx, sv["sl1"], sv["dsu1"], mod[l, 2] + order, d1)
        dx, dvec1 = ffn_bwd_in(dx, sv["x0"], sv["y1"], dgt, dup, row(ffn1_norm, l), mod[l, 1], g1, u1)
        (g_g1, g_u1), g_d1 = tn_mm_pair(dgt, dup, sv["h1"], chip1), nn_mm(sv["a1"], dy, chip1)
        dmods[l] = jnp.concatenate([dvec1[0:3], dmod_rest], axis=0)
        dnorm1[l] = dvec1[3]
        if l == 0:
            late = small_gather("late", [dvec1[0:3], dvec1[3]], dx)
        order = checkpoint(f"{l}b", l, ffn1_pos, [g_g1, g_u1, g_d1], dx, late[0][4] if l == 0 else None)

    to_chips(stages[-1], stages[-2]["scatter"][4], stages[-2]["scatter"][4])
    sent = stages[-1]["scatter"][4]
    got_early = gather8_wait("early", *early[0][:4], sent)
    got_late = gather8_wait("late", *late[0][:4], sent)
    each_rest, each0_rest = _unpack_rows(got_early, early[1])[:2]
    each0_first = _unpack_rows(got_late, late[1])[0]
    dmod_all = jnp.concatenate([jnp.concatenate([each0_first, each0_rest], axis=1)[:, None], each_rest], axis=1)
    dmod_all = dmod_all.reshape(N_DEV, n_layers, 9 * d)
    dmod_loc = lax.dynamic_slice_in_dim(dmod_all, chip * ada_cols, ada_cols, axis=2)
    dmod16 = jnp.pad(jnp.transpose(dmod_loc, (1, 0, 2)), ((0, 0), (0, 8), (0, 0)))

    weights = [ada_w, ada_b, ffn1_norm, ffn1_w_gate, ffn1_w_up, ffn1_w_down, mix_norm, w_in, pool_w, pool_scale,
               q_a_norm, w_q_b, kv_a_norm, w_kv_b, w_out, ffn2_norm, ffn2_w_gate, ffn2_w_up, ffn2_w_down, final_norm]
    ms = [m_ada_w, m_ada_b, m_ffn1_norm, m_ffn1_w_gate, m_ffn1_w_up, m_ffn1_w_down, m_mix_norm, m_w_in, m_pool_w,
          m_pool_scale, m_q_a_norm, m_w_q_b, m_kv_a_norm, m_w_kv_b, m_w_out, m_ffn2_norm, m_ffn2_w_gate, m_ffn2_w_up,
          m_ffn2_w_down, m_final_norm]
    vs = [v_ada_w, v_ada_b, v_ffn1_norm, v_ffn1_w_gate, v_ffn1_w_up, v_ffn1_w_down, v_mix_norm, v_w_in, v_pool_w,
          v_pool_scale, v_q_a_norm, v_w_q_b, v_kv_a_norm, v_w_kv_b, v_w_out, v_ffn2_norm, v_ffn2_w_gate, v_ffn2_w_up,
          v_ffn2_w_down, v_final_norm]
    transposed = (3, 4, 7, 11, 16, 17)
    outs = [None] * len(weights)

    outs[0] = adamw(ada_w, ada_bwd(c16, dmod16), m_ada_w, v_ada_w)
    for job in stages[-3:]:
        finish(job, outs[0][1])
    fill_a = fill_start([reduced[t] for t in rest_pos], "a", outs[0][1])
    fill_b = fill_start([reduced[t] for t in ffn1_pos], "b", fill_a[3])

    (g_dmod_rest, g_dmod0_rest, g_n1_rest, g_n2, g_n3, g_fn, g_ps, g_qan, g_kvan, g_pw) = _unpack_rows(
        sum_devices(got_early, fill_b[3]), early[1])
    g_dmod0_first, g_n1_first = _unpack_rows(sum_devices(got_late, fill_b[3]), late[1])
    g_ada_b = jnp.concatenate([jnp.concatenate([g_dmod0_first, g_dmod0_rest], axis=0)[None], g_dmod_rest], axis=0)
    g_n1 = jnp.concatenate([g_n1_first[None], g_n1_rest], axis=0)
    grads = [None, g_ada_b, g_n1, None, None, None, g_n2, None, g_pw, g_ps, g_qan, None, g_kvan, None, None, g_n3,
             None, None, None, g_fn]
    big = [i for i, g in enumerate(grads) if g is None and i > 0]
    for i, (w, g, m, v) in enumerate(zip(weights, grads, ms, vs)):
        if g is not None:
            outs[i] = adamw(w, g.reshape(w.shape), m, v)

    def update(positions, fly, after):
        filled = fill_wait(fly[0], fly[1], fly[2], fly[3], after)
        for t, g in zip(positions, filled):
            i = big[t]
            if i in transposed:
                outs[i] = tuple(tr(o) for o in adamw(tr(weights[i]), g, tr(ms[i]), tr(vs[i]), copy_g=True))
            else:
                outs[i] = adamw(weights[i], g, ms[i], vs[i], copy_g=True)

    update(rest_pos, ("a",) + tuple(fill_a[:3]), outs[8][1])
    update(ffn1_pos, ("b",) + tuple(fill_b[:3]), outs[big[rest_pos[-1]]][1])
    return (loss, dx.reshape(x.shape), *[t[0] for t in outs], *[t[1] for t in outs], *[t[2] for t in outs],
            *[t[3] for t in outs])
```

```python
import math

import jax
import jax.numpy as jnp
from jax import lax
from jax.experimental import pallas as pl
from jax.experimental.pallas import tpu as pltpu

F32 = jnp.float32
BF16 = jnp.bfloat16
MESH = pl.DeviceIdType.MESH

EPS = 1e-6
ROPE_THETA = 10000.0
N_HEADS = 4
QK_NOPE = 128
QK_ROPE = 64
V_HEAD = 128
POOL_WINDOWS = (2, 4, 8, 16)
POOL_GC = 128
POOL_WIDTH = POOL_GC * len(POOL_WINDOWS)
Q_LORA = 384
KV_LORA = 256
SOFTMAX_SCALE = 1.0 / math.sqrt(QK_NOPE + QK_ROPE)
N_CHIPS = 4
N_DEV = 8

ADAM_LR = 0.001
ADAM_B1 = 0.9
ADAM_B2 = 0.999
ADAM_EPS = 1e-08
ADAM_WD = 0.01
ADAM_STEP = 10

ROW_TILE = 512
LIGHT_TILE = 1024
ATT_TILE = 512
VMEM_LIMIT = 56 * 1024 * 1024
BF16_ROWS = 16
LANES = 128


def _params(sem=None, vmem=VMEM_LIMIT):
    return pltpu.CompilerParams(dimension_semantics=sem, vmem_limit_bytes=vmem)


def _dot(a, b):
    return jnp.dot(a, b, preferred_element_type=F32)


def _dot_nt(a, b):
    return lax.dot_general(a, b, (((1,), (1,)), ((), ())), preferred_element_type=F32)


def _dot_tn(a, b):
    return lax.dot_general(a, b, (((0,), (0,)), ((), ())), preferred_element_type=F32)


def _dot_exact(t, perm):
    t1 = t.astype(BF16)
    r1 = t - t1.astype(F32)
    t2 = r1.astype(BF16)
    t3 = (r1 - t2.astype(F32)).astype(BF16)
    return _dot(t1, perm) + _dot(t2, perm) + _dot(t3, perm)


def _sum0(a):
    return jnp.sum(a, axis=0, keepdims=True)


def _rms(xt):
    r = lax.rsqrt(jnp.mean(xt * xt, axis=-1, keepdims=True) + EPS)
    return xt * r, r


def _rms_bwd(dy, xt, g):
    xhat, r = _rms(xt)
    dxhat = dy * g
    dx = r * (dxhat - xhat * jnp.mean(dxhat * xhat, axis=-1, keepdims=True))
    return dx, _sum0(dy * xhat)


def _normmod_bwd(dh, xt, gn, sc):
    xhat, _ = _rms(xt)
    dn = dh * (1.0 + sc)
    dx, dgn = _rms_bwd(dn, xt, gn)
    return dx, _sum0(dh), _sum0(dh * (xhat * gn)), dgn


def _row_tile(s):
    return min(s, ROW_TILE)


def _light_tile(s):
    return min(s, LIGHT_TILE)


def _full(shape):
    n = len(shape)
    return pl.BlockSpec(shape, lambda *_: (0,) * n)


def _resident(shape):
    n = len(shape)
    return pl.BlockSpec(shape, lambda *_: (0,) * n, pipeline_mode=pl.Buffered(1))


def ffn_fwd(x, gn, sh, sc, gt, wg, wu, wd):
    s, d = x.shape
    k_chunks, fs, _ = wg.shape
    tm = _row_tile(s)

    def body(x_ref, gn_ref, sh_ref, sc_ref, gt_ref, wg_ref, wu_ref, wd_ref,
             xo_ref, h_ref, a_ref, sl_ref, dsu_ref, y_ref):
        xt = x_ref[...]
        xhat, _ = _rms(xt)
        h = (xhat * gn_ref[...] * (1.0 + sc_ref[...]) + sh_ref[...]).astype(BF16)
        h_ref[...] = h
        y = jnp.zeros((tm, d), F32)
        for k in range(k_chunks):
            gate = _dot_nt(h, wg_ref[k])
            up = _dot_nt(h, wu_ref[k])
            sg = jax.nn.sigmoid(gate)
            sl = gate * sg
            a = (sl * up).astype(BF16)
            a_ref[k] = a.T
            sl_ref[k] = sl.astype(BF16)
            dsu_ref[k] = (up * (sg * (1.0 + gate * (1.0 - sg)))).astype(BF16)
            y += _dot(a, wd_ref[k])
        y_ref[...] = y.astype(BF16)
        xo_ref[...] = xt + 0.5 * gt_ref[...] * y

    row = pl.BlockSpec((tm, d), lambda i: (i, 0))
    vec = pl.BlockSpec((1, d), lambda i: (0, 0))
    act = pl.BlockSpec((k_chunks, tm, fs), lambda i: (0, i, 0))
    act_shape = jax.ShapeDtypeStruct((k_chunks, s, fs), BF16)
    return pl.pallas_call(
        body, name="ffn_fwd",
        grid=(s // tm,),
        in_specs=[row, vec, vec, vec, vec, _resident(wg.shape), _resident(wu.shape), _resident(wd.shape)],
        out_specs=[row, row, pl.BlockSpec((k_chunks, fs, tm), lambda i: (0, 0, i)), act, act, row],
        out_shape=[jax.ShapeDtypeStruct((s, d), F32), jax.ShapeDtypeStruct((s, d), BF16),
                   jax.ShapeDtypeStruct((k_chunks, fs, s), BF16), act_shape, act_shape,
                   jax.ShapeDtypeStruct((s, d), BF16)],
        compiler_params=_params(("arbitrary",)),
    )(x, gn, sh, sc, gt, wg, wu, wd)


def ffn_bwd_act(dxn, sl, dsu, gt, wd):
    s, d = dxn.shape
    k_chunks, fs, _ = wd.shape
    tm = _row_tile(s)

    def body(dxn_ref, sl_ref, dsu_ref, gt_ref, wd_ref, dy_ref, dgate_ref, dup_ref):
        dy = (0.5 * gt_ref[...] * dxn_ref[...]).astype(BF16)
        dy_ref[...] = dy
        for k in range(k_chunks):
            da = _dot_nt(dy, wd_ref[k])
            dgate_ref[k] = (da * dsu_ref[k].astype(F32)).astype(BF16)
            dup_ref[k] = (da * sl_ref[k].astype(F32)).astype(BF16)

    row = pl.BlockSpec((tm, d), lambda i: (i, 0))
    act = pl.BlockSpec((k_chunks, tm, fs), lambda i: (0, i, 0))
    act_shape = jax.ShapeDtypeStruct((k_chunks, s, fs), BF16)
    return pl.pallas_call(
        body, name="ffn_bwd_act",
        grid=(s // tm,),
        in_specs=[row, act, act, pl.BlockSpec((1, d), lambda i: (0, 0)), _resident(wd.shape)],
        out_specs=[row, act, act],
        out_shape=[jax.ShapeDtypeStruct((s, d), BF16), act_shape, act_shape],
        compiler_params=_params(("arbitrary",)),
    )(dxn, sl, dsu, gt, wd)


def ffn_bwd_in(dxn, x, y, dgate, dup, gn, sc, wg, wu):
    s, d = x.shape
    k_chunks, fs, _ = wg.shape
    tm = _row_tile(s)

    def body(dxn_ref, x_ref, y_ref, dgate_ref, dup_ref, gn_ref, sc_ref, wg_ref, wu_ref, dx_ref, dvec_ref):
        i = pl.program_id(0)

        @pl.when(i == 0)
        def _():
            dvec_ref[...] = jnp.zeros_like(dvec_ref)

        dh = jnp.zeros((tm, d), F32)
        for k in range(k_chunks):
            dh += _dot(dgate_ref[k], wg_ref[k]) + _dot(dup_ref[k], wu_ref[k])
        dxn_t = dxn_ref[...]
        dx, dsh, dsc, dgn = _normmod_bwd(dh, x_ref[...], gn_ref[...], sc_ref[...])
        dx_ref[...] = dx + dxn_t
        dvec_ref[0:1, :] += dsh
        dvec_ref[1:2, :] += dsc
        dvec_ref[2:3, :] += _sum0(0.5 * dxn_t * y_ref[...].astype(F32))
        dvec_ref[3:4, :] += dgn

    row = pl.BlockSpec((tm, d), lambda i: (i, 0))
    vec = pl.BlockSpec((1, d), lambda i: (0, 0))
    act = pl.BlockSpec((k_chunks, tm, fs), lambda i: (0, i, 0))
    return pl.pallas_call(
        body, name="ffn_bwd_in",
        grid=(s // tm,),
        in_specs=[row, row, row, act, act, vec, vec, _resident(wg.shape), _resident(wu.shape)],
        out_specs=[row, pl.BlockSpec((8, d), lambda i: (0, 0))],
        out_shape=[jax.ShapeDtypeStruct((s, d), F32), jax.ShapeDtypeStruct((8, d), F32)],
        compiler_params=_params(("arbitrary",)),
    )(dxn, x, y, dgate, dup, gn, sc, wg, wu)


def _grad_mm(dot, a, b, a_spec, b_spec, g, m, n, chip, name):
    def body(c_ref, a_ref, b_ref, own_ref, all_ref):
        res = dot(a_ref[...], b_ref[...])
        all_ref[...] = res.astype(BF16)

        @pl.when(pl.program_id(0) == c_ref[0])
        def _():
            own_ref[...] = res

    return pl.pallas_call(
        body, name=name,
        grid_spec=pltpu.PrefetchScalarGridSpec(
            num_scalar_prefetch=1, grid=(g,), in_specs=[a_spec, b_spec],
            out_specs=[pl.BlockSpec((m, n), lambda gi, c: (0, 0)), pl.BlockSpec((None, m, n), lambda gi, c: (gi, 0, 0))]),
        out_shape=[jax.ShapeDtypeStruct((m, n), F32), jax.ShapeDtypeStruct((g, m, n), BF16)],
        compiler_params=_params(("arbitrary",)),
    )(chip, a, b)


def tn_mm_pair(a1, a2, b, chip):
    g, s, m = a1.shape
    n = b.shape[1]

    def body(c_ref, a1_ref, a2_ref, b_ref, own1_ref, all1_ref, own2_ref, all2_ref):
        gi = pl.program_id(0)

        def one(a_ref, own_ref, all_ref, slot):
            res = _dot_tn(a_ref[...], b_ref[...])
            all_ref[...] = res.astype(BF16)

            @pl.when(slot == c_ref[0])
            def _():
                own_ref[...] = res

        @pl.when(gi < g)
        def _():
            one(a1_ref, own1_ref, all1_ref, gi)

        @pl.when(gi >= g)
        def _():
            one(a2_ref, own2_ref, all2_ref, gi - g)

    first = lambda gi, c: (jnp.minimum(gi, g - 1), 0, 0)
    second = lambda gi, c: (jnp.maximum(gi - g, 0), 0, 0)
    own = pl.BlockSpec((m, n), lambda gi, c: (0, 0))
    outs = pl.pallas_call(
        body, name="tn_mm_pair",
        grid_spec=pltpu.PrefetchScalarGridSpec(
            num_scalar_prefetch=1, grid=(2 * g,),
            in_specs=[pl.BlockSpec((None, s, m), first), pl.BlockSpec((None, s, m), second),
                      pl.BlockSpec((s, n), lambda gi, c: (0, 0))],
            out_specs=[own, pl.BlockSpec((None, m, n), first), own, pl.BlockSpec((None, m, n), second)]),
        out_shape=[jax.ShapeDtypeStruct((m, n), F32), jax.ShapeDtypeStruct((g, m, n), BF16)] * 2,
        compiler_params=_params(("arbitrary",)),
    )(chip, a1, a2, b)
    return (outs[0], outs[1]), (outs[2], outs[3])


def nn_mm(a_t, b, chip):
    g, m, s = a_t.shape
    n = b.shape[1]
    return _grad_mm(_dot, a_t, b, pl.BlockSpec((None, m, s), lambda gi, c: (gi, 0, 0)),
                    pl.BlockSpec((s, n), lambda gi, c: (0, 0)), g, m, n, chip, "nn_mm")


def tn_mm(a, b, chip):
    ga, s, m = a.shape
    gb, _, n = b.shape
    a_spec = pl.BlockSpec((None, s, m), (lambda gi, c: (gi, 0, 0)) if ga > 1 else (lambda gi, c: (0, 0, 0)))
    b_spec = pl.BlockSpec((None, s, n), (lambda gi, c: (gi, 0, 0)) if gb > 1 else (lambda gi, c: (0, 0, 0)))
    return _grad_mm(_dot_tn, a, b, a_spec, b_spec, max(ga, gb), m, n, chip, "tn_mm")


def mix_in_fwd(x, gn, sh, sc, w_in_t):
    s, d = x.shape
    tm = _light_tile(s)
    o1, o2, o3 = POOL_WIDTH, POOL_WIDTH + Q_LORA, POOL_WIDTH + Q_LORA + KV_LORA

    def body(x_ref, gn_ref, sh_ref, sc_ref, w_ref, h_ref, u_ref, cq_ref, ckv_ref, kr_ref):
        xhat, _ = _rms(x_ref[...])
        h = (xhat * gn_ref[...] * (1.0 + sc_ref[...]) + sh_ref[...]).astype(BF16)
        h_ref[...] = h
        z = _dot_nt(h, w_ref[0:o3, :])
        u_ref[...] = z[:, 0:o1]
        cq_ref[...] = z[:, o1:o2]
        ckv_ref[...] = z[:, o2:o3]
        kr_ref[...] = _dot_nt(h, w_ref[o3:, :])

    row = lambda w: pl.BlockSpec((tm, w), lambda i: (i, 0))
    vec = pl.BlockSpec((1, d), lambda i: (0, 0))
    return pl.pallas_call(
        body, name="mix_in_fwd",
        grid=(s // tm,),
        in_specs=[row(d), vec, vec, vec, _full(w_in_t.shape)],
        out_specs=[row(d), row(POOL_WIDTH), row(Q_LORA), row(KV_LORA), row(QK_ROPE)],
        out_shape=[jax.ShapeDtypeStruct((s, d), BF16), jax.ShapeDtypeStruct((s, POOL_WIDTH), F32),
                   jax.ShapeDtypeStruct((s, Q_LORA), F32), jax.ShapeDtypeStruct((s, KV_LORA), F32),
                   jax.ShapeDtypeStruct((s, QK_ROPE), F32)],
        compiler_params=_params(("arbitrary",)),
    )(x, gn, sh, sc, w_in_t)


def mix_in_bwd(dxn, du, dcq, dckv, dkr, x, gn, sc, w_in_t):
    s, d = x.shape
    tm = _light_tile(s)
    o1, o2, o3 = POOL_WIDTH, POOL_WIDTH + Q_LORA, POOL_WIDTH + Q_LORA + KV_LORA
    n_z = w_in_t.shape[0]

    def body(dxn_ref, du_ref, dcq_ref, dckv_ref, dkr_ref, x_ref, gn_ref, sc_ref, w_ref, dx_ref, dz_ref, dvec_ref):
        i = pl.program_id(0)

        @pl.when(i == 0)
        def _():
            dvec_ref[...] = jnp.zeros_like(dvec_ref)

        dub = du_ref[...].astype(BF16)
        dqb = dcq_ref[...].astype(BF16)
        dkb = dckv_ref[...].astype(BF16)
        drb = dkr_ref[...].astype(BF16)
        dz_ref[0:o1, :] = dub.T
        dz_ref[o1:o2, :] = dqb.T
        dz_ref[o2:o3, :] = dkb.T
        dz_ref[o3:, :] = drb.T
        dh = (_dot(dub, w_ref[0:o1, :]) + _dot(dqb, w_ref[o1:o2, :]) + _dot(dkb, w_ref[o2:o3, :])
              + _dot(drb, w_ref[o3:, :]))
        dx, dsh, dsc, dgn = _normmod_bwd(dh, x_ref[...], gn_ref[...], sc_ref[...])
        dx_ref[...] = dx + dxn_ref[...]
        dvec_ref[0:1, :] += dsh
        dvec_ref[1:2, :] += dsc
        dvec_ref[3:4, :] += dgn

    row = lambda w: pl.BlockSpec((tm, w), lambda i: (i, 0))
    vec = pl.BlockSpec((1, d), lambda i: (0, 0))
    return pl.pallas_call(
        body, name="mix_in_bwd",
        grid=(s // tm,),
        in_specs=[row(d), row(POOL_WIDTH), row(Q_LORA), row(KV_LORA), row(QK_ROPE), row(d), vec, vec,
                  _full(w_in_t.shape)],
        out_specs=[row(d), pl.BlockSpec((n_z, tm), lambda i: (0, i)), pl.BlockSpec((8, d), lambda i: (0, 0))],
        out_shape=[jax.ShapeDtypeStruct((s, d), F32), jax.ShapeDtypeStruct((n_z, s), BF16),
                   jax.ShapeDtypeStruct((8, d), F32)],
        compiler_params=_params(("arbitrary",)),
    )(dxn, du, dcq, dckv, dkr, x, gn, sc, w_in_t)


def _window_sum(a, w, rows, forward):
    s = a.shape[0]
    step = 1
    while step < w:
        if forward:
            shifted = jnp.where(rows < s - step, pltpu.roll(a, s - step, 0), 0.0)
        else:
            shifted = jnp.where(rows >= step, pltpu.roll(a, step, 0), 0.0)
        a = a + shifted
        step *= 2
    return a


def pool_fwd(u, pool_w, pool_scale):
    s = u.shape[0]

    def body(u_ref, w_ref, sc_ref, y_ref, diff_ref):
        rows = lax.broadcasted_iota(jnp.int32, (s, POOL_GC), 0)
        for g, w in enumerate(POOL_WINDOWS):
            cols = slice(g * POOL_GC, (g + 1) * POOL_GC)
            ug = u_ref[:, cols]
            cnt = jnp.minimum(rows + 1, w).astype(F32)
            diff = (_window_sum(ug, w, rows, False) / cnt - ug).astype(BF16)
            diff_ref[:, cols] = diff
            y_ref[:, cols] = _dot(diff, w_ref[g].astype(BF16)) * sc_ref[:, cols]

    return pl.pallas_call(
        body, name="pool_fwd",
        out_shape=[jax.ShapeDtypeStruct(u.shape, F32), jax.ShapeDtypeStruct(u.shape, BF16)],
        compiler_params=_params(),
    )(u, pool_w, pool_scale)


def pool_bwd(dy, diff, pool_w, pool_scale):
    s = dy.shape[0]

    def body(dy_ref, diff_ref, w_ref, sc_ref, du_ref, dw_ref, dsc_ref):
        rows = lax.broadcasted_iota(jnp.int32, (s, POOL_GC), 0)
        for g, w in enumerate(POOL_WINDOWS):
            cols = slice(g * POOL_GC, (g + 1) * POOL_GC)
            dyg = dy_ref[:, cols]
            diff = diff_ref[:, cols]
            wb = w_ref[g].astype(BF16)
            dsc_ref[:, cols] = _sum0(dyg * _dot(diff, wb))
            dys = (dyg * sc_ref[:, cols]).astype(BF16)
            dw_ref[g] = _dot_tn(diff, dys)
            ddiff = _dot_nt(dys, wb)
            cnt = jnp.minimum(rows + 1, w).astype(F32)
            du_ref[:, cols] = _window_sum(ddiff / cnt, w, rows, True) - ddiff

    return pl.pallas_call(
        body, name="pool_bwd",
        out_shape=[jax.ShapeDtypeStruct(dy.shape, F32), jax.ShapeDtypeStruct(pool_w.shape, F32),
                   jax.ShapeDtypeStruct(pool_scale.shape, F32)],
        compiler_params=_params(),
    )(dy, diff, pool_w, pool_scale)


def mla_qkv_fwd(cq, ckv, kr, qan, kvan, wq, wkv, cos, sin, rot):
    s = cq.shape[0]
    tm = _light_tile(s)

    def body(cq_ref, ckv_ref, kr_ref, qan_ref, kvan_ref, wq_ref, wkv_ref, cos_ref, sin_ref, rot_ref,
             q_ref, k_ref, v_ref, ql_ref, kvl_ref):
        cos_t = cos_ref[...]
        sin_t = sin_ref[...]
        perm = rot_ref[...]

        def rope(t):
            return t * cos_t + _dot_exact(t, perm) * sin_t

        qhat, _ = _rms(cq_ref[...])
        ql = (qhat * qan_ref[...]).astype(BF16)
        ql_ref[...] = ql
        khat, _ = _rms(ckv_ref[...])
        kvl = (khat * kvan_ref[...]).astype(BF16)
        kvl_ref[...] = kvl
        krr = rope(kr_ref[...]).astype(BF16)
        for h in range(N_HEADS):
            q = _dot_nt(ql, wq_ref[h])
            q_ref[h, :, 0:QK_NOPE] = q[:, 0:QK_NOPE].astype(BF16)
            q_ref[h, :, QK_NOPE:] = rope(q[:, QK_NOPE:]).astype(BF16)
            kv = _dot(kvl, wkv_ref[h])
            k_ref[h, :, 0:QK_NOPE] = kv[:, 0:QK_NOPE].astype(BF16)
            k_ref[h, :, QK_NOPE:] = krr
            v_ref[h] = kv[:, QK_NOPE:].astype(BF16)

    row = lambda w: pl.BlockSpec((tm, w), lambda i: (i, 0))
    hrow = lambda w: pl.BlockSpec((N_HEADS, tm, w), lambda i: (0, i, 0))
    qk = QK_NOPE + QK_ROPE
    return pl.pallas_call(
        body, name="mla_qkv_fwd",
        grid=(s // tm,),
        in_specs=[row(Q_LORA), row(KV_LORA), row(QK_ROPE), _full(qan.shape), _full(kvan.shape),
                  _full(wq.shape), _full(wkv.shape), row(QK_ROPE), row(QK_ROPE), _full(rot.shape)],
        out_specs=[hrow(qk), hrow(qk), hrow(V_HEAD), row(Q_LORA), row(KV_LORA)],
        out_shape=[jax.ShapeDtypeStruct((N_HEADS, s, qk), BF16), jax.ShapeDtypeStruct((N_HEADS, s, qk), BF16),
                   jax.ShapeDtypeStruct((N_HEADS, s, V_HEAD), BF16), jax.ShapeDtypeStruct((s, Q_LORA), BF16),
                   jax.ShapeDtypeStruct((s, KV_LORA), BF16)],
        compiler_params=_params(("arbitrary",)),
    )(cq, ckv, kr, qan, kvan, wq, wkv, cos, sin, rot)


def _attn_probs(q_ref, k_ref, qi, tq):
    n = (qi + 1) * tq
    rows = slice(qi * tq, n)
    sc = _dot_nt(q_ref[rows, :], k_ref[0:n, :]) * SOFTMAX_SCALE
    qpos = qi * tq + lax.broadcasted_iota(jnp.int32, (tq, n), 0)
    kpos = lax.broadcasted_iota(jnp.int32, (tq, n), 1)
    sc = jnp.where(qpos >= kpos, sc, -1e30)
    e = jnp.exp(sc - jnp.max(sc, axis=-1, keepdims=True))
    return e * (1.0 / jnp.sum(e, axis=-1, keepdims=True))


def attn_fwd(q, k, v):
    nh, s, qk = q.shape
    tq = min(s, ATT_TILE)

    def body(q_ref, k_ref, v_ref, o_ref):
        for qi in range(s // tq):
            n = (qi + 1) * tq
            p = _attn_probs(q_ref, k_ref, qi, tq).astype(BF16)
            o_ref[qi * tq:n, :] = _dot(p, v_ref[0:n, :])

    head = lambda w: pl.BlockSpec((None, s, w), lambda h: (h, 0, 0))
    return pl.pallas_call(
        body, name="attn_fwd",
        grid=(nh,),
        in_specs=[head(qk), head(qk), head(V_HEAD)],
        out_specs=pl.BlockSpec((s, V_HEAD), lambda h: (0, h)),
        out_shape=jax.ShapeDtypeStruct((s, nh * V_HEAD), F32),
        compiler_params=_params(("arbitrary",)),
    )(q, k, v)


def attn_bwd(q, k, v, do):
    nh, s, qk = q.shape
    tq = min(s, ATT_TILE)

    def body(q_ref, k_ref, v_ref, do_ref, dq_ref, dk_ref, dv_ref):
        dk_ref[...] = jnp.zeros_like(dk_ref)
        dv_ref[...] = jnp.zeros_like(dv_ref)
        for qi in range(s // tq):
            n = (qi + 1) * tq
            rows = slice(qi * tq, n)
            p = _attn_probs(q_ref, k_ref, qi, tq)
            dob = do_ref[rows, :].astype(BF16)
            dp = _dot_nt(dob, v_ref[0:n, :])
            ds = (p * (dp - jnp.sum(p * dp, axis=-1, keepdims=True)) * SOFTMAX_SCALE).astype(BF16)
            dq_ref[rows, :] = _dot(ds, k_ref[0:n, :])
            dk_ref[0:n, :] += _dot_tn(ds, q_ref[rows, :])
            dv_ref[0:n, :] += _dot_tn(p.astype(BF16), dob)

    head = lambda w: pl.BlockSpec((None, s, w), lambda h: (h, 0, 0))
    return pl.pallas_call(
        body, name="attn_bwd",
        grid=(nh,),
        in_specs=[head(qk), head(qk), head(V_HEAD), pl.BlockSpec((s, V_HEAD), lambda h: (0, h))],
        out_specs=[head(qk), head(qk), head(V_HEAD)],
        out_shape=[jax.ShapeDtypeStruct((nh, s, qk), F32), jax.ShapeDtypeStruct((nh, s, qk), F32),
                   jax.ShapeDtypeStruct((nh, s, V_HEAD), F32)],
        compiler_params=_params(("arbitrary",)),
    )(q, k, v, do)


def mla_qkv_bwd(dq, dk, dv, cq, ckv, qan, kvan, wq, wkv, cos, sin, rot_t):
    s = cq.shape[0]
    tm = _light_tile(s)

    def body(dq_ref, dk_ref, dv_ref, cq_ref, ckv_ref, qan_ref, kvan_ref,
             wq_ref, wkv_ref, cos_ref, sin_ref, rot_ref,
             dcq_ref, dckv_ref, dkro_ref, gq_ref, gkv_ref, dqan_ref, dkvan_ref):
        i = pl.program_id(0)

        @pl.when(i == 0)
        def _():
            dqan_ref[...] = jnp.zeros_like(dqan_ref)
            dkvan_ref[...] = jnp.zeros_like(dkvan_ref)

        cos_t = cos_ref[...]
        sin_t = sin_ref[...]
        perm_t = rot_ref[...]

        def unrope(t):
            return t * cos_t + _dot_exact(t * sin_t, perm_t)

        acc_q = jnp.zeros((tm, Q_LORA), F32)
        acc_kv = jnp.zeros((tm, KV_LORA), F32)
        dkr_sum = jnp.zeros((tm, QK_ROPE), F32)
        for h in range(N_HEADS):
            dq_h = dq_ref[h]
            a = dq_h[:, 0:QK_NOPE].astype(BF16)
            b = unrope(dq_h[:, QK_NOPE:]).astype(BF16)
            gq_ref[h, :, 0:QK_NOPE] = a
            gq_ref[h, :, QK_NOPE:] = b
            wq_h = wq_ref[h]
            acc_q += _dot(a, wq_h[0:QK_NOPE, :]) + _dot(b, wq_h[QK_NOPE:, :])
            dk_h = dk_ref[h]
            dk = dk_h[:, 0:QK_NOPE].astype(BF16)
            dvv = dv_ref[h].astype(BF16)
            gkv_ref[h, :, 0:QK_NOPE] = dk
            gkv_ref[h, :, QK_NOPE:] = dvv
            wkv_h = wkv_ref[h]
            acc_kv += _dot_nt(dk, wkv_h[:, 0:QK_NOPE]) + _dot_nt(dvv, wkv_h[:, QK_NOPE:])
            dkr_sum += dk_h[:, QK_NOPE:]
        dkro_ref[...] = unrope(dkr_sum)
        dcq, dqan = _rms_bwd(acc_q, cq_ref[...], qan_ref[...])
        dcq_ref[...] = dcq
        dqan_ref[...] += dqan
        dckv, dkvan = _rms_bwd(acc_kv, ckv_ref[...], kvan_ref[...])
        dckv_ref[...] = dckv
        dkvan_ref[...] += dkvan

    row = lambda w: pl.BlockSpec((tm, w), lambda i: (i, 0))
    hrow = lambda w: pl.BlockSpec((N_HEADS, tm, w), lambda i: (0, i, 0))
    return pl.pallas_call(
        body, name="mla_qkv_bwd",
        grid=(s // tm,),
        in_specs=[hrow(QK_NOPE + QK_ROPE), hrow(QK_NOPE + QK_ROPE), hrow(V_HEAD),
                  row(Q_LORA), row(KV_LORA), _full(qan.shape), _full(kvan.shape),
                  _full(wq.shape), _full(wkv.shape), row(QK_ROPE), row(QK_ROPE), _full(rot_t.shape)],
        out_specs=[row(Q_LORA), row(KV_LORA), row(QK_ROPE), hrow(QK_NOPE + QK_ROPE), hrow(QK_NOPE + V_HEAD),
                   _full(qan.shape), _full(kvan.shape)],
        out_shape=[jax.ShapeDtypeStruct((s, Q_LORA), F32), jax.ShapeDtypeStruct((s, KV_LORA), F32),
                   jax.ShapeDtypeStruct((s, QK_ROPE), F32),
                   jax.ShapeDtypeStruct((N_HEADS, s, QK_NOPE + QK_ROPE), BF16),
                   jax.ShapeDtypeStruct((N_HEADS, s, QK_NOPE + V_HEAD), BF16),
                   jax.ShapeDtypeStruct(qan.shape, F32), jax.ShapeDtypeStruct(kvan.shape, F32)],
        compiler_params=_params(("arbitrary",)),
    )(dq, dk, dv, cq, ckv, qan, kvan, wq, wkv, cos, sin, rot_t)


def out_proj_fwd(yp, om, w_out, x, gt):
    s, d = x.shape
    n_sh, rs, _ = w_out.shape
    tm = _light_tile(s)
    per = POOL_WIDTH // rs

    def body(yp_ref, om_ref, w_ref, x_ref, gt_ref, xo_ref, ycat_ref, y_ref):
        y = jnp.zeros((tm, d), F32)
        for j in range(n_sh):
            src = yp_ref if j < per else om_ref
            part = src[:, (j % per) * rs:(j % per + 1) * rs].astype(BF16)
            ycat_ref[j] = part.T
            y += _dot(part, w_ref[j])
        y_ref[...] = y.astype(BF16)
        xo_ref[...] = x_ref[...] + gt_ref[...] * y

    row = lambda w: pl.BlockSpec((tm, w), lambda i: (i, 0))
    return pl.pallas_call(
        body, name="out_proj_fwd",
        grid=(s // tm,),
        in_specs=[row(POOL_WIDTH), row(POOL_WIDTH), _full(w_out.shape), row(d), pl.BlockSpec((1, d), lambda i: (0, 0))],
        out_specs=[row(d), pl.BlockSpec((n_sh, rs, tm), lambda i: (0, 0, i)), row(d)],
        out_shape=[jax.ShapeDtypeStruct((s, d), F32), jax.ShapeDtypeStruct((n_sh, rs, s), BF16),
                   jax.ShapeDtypeStruct((s, d), BF16)],
        compiler_params=_params(("arbitrary",)),
    )(yp, om, w_out, x, gt)


def out_proj_bwd(dxn, y, gt, w_out):
    s, d = dxn.shape
    n_sh, rs, _ = w_out.shape
    tm = _light_tile(s)
    per = POOL_WIDTH // rs

    def body(dxn_ref, y_ref, gt_ref, w_ref, dy_ref, dyp_ref, dom_ref, dgt_ref):
        i = pl.program_id(0)

        @pl.when(i == 0)
        def _():
            dgt_ref[...] = jnp.zeros_like(dgt_ref)

        dxn_t = dxn_ref[...]
        dy = (gt_ref[...] * dxn_t).astype(BF16)
        dy_ref[...] = dy
        dgt_ref[...] += _sum0(dxn_t * y_ref[...].astype(F32))
        for j in range(n_sh):
            dst = dyp_ref if j < per else dom_ref
            dst[:, (j % per) * rs:(j % per + 1) * rs] = _dot_nt(dy, w_ref[j])

    row = lambda w: pl.BlockSpec((tm, w), lambda i: (i, 0))
    vec = pl.BlockSpec((1, d), lambda i: (0, 0))
    return pl.pallas_call(
        body, name="out_proj_bwd",
        grid=(s // tm,),
        in_specs=[row(d), row(d), vec, _full(w_out.shape)],
        out_specs=[row(d), row(POOL_WIDTH), row(POOL_WIDTH), vec],
        out_shape=[jax.ShapeDtypeStruct((s, d), BF16), jax.ShapeDtypeStruct((s, POOL_WIDTH), F32),
                   jax.ShapeDtypeStruct((s, POOL_WIDTH), F32), jax.ShapeDtypeStruct((1, d), F32)],
        compiler_params=_params(("arbitrary",)),
    )(dxn, y, gt, w_out)


def final_loss(x, gn, tgt):
    s, d = x.shape
    tm = _light_tile(s)

    def body(x_ref, gn_ref, t_ref, loss_ref, dx_ref, dgn_ref):
        i = pl.program_id(0)

        @pl.when(i == 0)
        def _():
            loss_ref[...] = jnp.zeros_like(loss_ref)
            dgn_ref[...] = jnp.zeros_like(dgn_ref)

        xt = x_ref[...]
        g = gn_ref[...]
        xhat, _ = _rms(xt)
        err = xhat * g - t_ref[...]
        per_tok = jnp.mean(err * err, axis=-1, keepdims=True)
        loss_ref[...] += jnp.broadcast_to(0.5 * _sum0(per_tok), loss_ref.shape)
        dx, dgn = _rms_bwd(err * (1.0 / d), xt, g)
        dx_ref[...] = dx
        dgn_ref[...] += dgn

    row = pl.BlockSpec((tm, d), lambda i: (i, 0))
    vec = pl.BlockSpec((1, d), lambda i: (0, 0))
    return pl.pallas_call(
        body, name="final_loss",
        grid=(s // tm,),
        in_specs=[row, vec, row],
        out_specs=[pl.BlockSpec((1, LANES), lambda i: (0, 0)), row, vec],
        out_shape=[jax.ShapeDtypeStruct((1, LANES), F32), jax.ShapeDtypeStruct((s, d), F32),
                   jax.ShapeDtypeStruct((1, d), F32)],
        compiler_params=_params(("arbitrary",)),
    )(x, gn, tgt)


def _col_tile(cols):
    return 768 if cols % 768 == 0 else cols


def ada_fwd(c16, ada_w, ada_b_loc):
    n_layers, d, cols = ada_w.shape
    tn = _col_tile(cols)

    def body(c_ref, w_ref, b_ref, o_ref):
        cv = c_ref[...]
        ca = (cv * jax.nn.sigmoid(cv)).astype(BF16)
        o_ref[...] = _dot(ca, w_ref[...].astype(BF16)) + b_ref[...]

    return pl.pallas_call(
        body, name="ada_fwd",
        grid=(n_layers, cols // tn),
        in_specs=[pl.BlockSpec((16, d), lambda l, j: (0, 0)), pl.BlockSpec((None, d, tn), lambda l, j: (l, 0, j)),
                  pl.BlockSpec((None, 1, tn), lambda l, j: (l, 0, j))],
        out_specs=pl.BlockSpec((None, 16, tn), lambda l, j: (l, 0, j)),
        out_shape=jax.ShapeDtypeStruct((n_layers, 16, cols), F32),
        compiler_params=_params(("arbitrary", "arbitrary")),
    )(c16, ada_w, ada_b_loc)


def ada_bwd(c16, dmod16):
    n_layers, _, cols = dmod16.shape
    d = c16.shape[1]
    tn = _col_tile(cols)

    def body(c_ref, g_ref, o_ref):
        cv = c_ref[...]
        ca = (cv * jax.nn.sigmoid(cv)).astype(BF16)
        o_ref[...] = _dot_tn(ca, g_ref[...].astype(BF16))

    return pl.pallas_call(
        body, name="ada_bwd",
        grid=(n_layers, cols // tn),
        in_specs=[pl.BlockSpec((16, d), lambda l, j: (0, 0)), pl.BlockSpec((None, 16, tn), lambda l, j: (l, 0, j))],
        out_specs=pl.BlockSpec((None, d, tn), lambda l, j: (l, 0, j)),
        out_shape=jax.ShapeDtypeStruct((n_layers, d, cols), F32),
        compiler_params=_params(("arbitrary", "arbitrary")),
    )(c16, dmod16)


def _as_rows(a):
    if a.ndim == 1:
        return a.reshape(1, a.shape[0])
    return a.reshape(-1, a.shape[-1])


def _rows_tile(r, c, itemsize=4, budget=2 * 1024 * 1024):
    if r * c * itemsize <= budget:
        return r
    best = None
    t = BF16_ROWS
    while t < r:
        if r % t == 0 and t * c * itemsize <= budget:
            best = t
        t += BF16_ROWS
    return best if best is not None else r


CAST_VMEM = 16 * 1024 * 1024


def cast_place(ws, chip, layers, after):
    _, r, c = ws[0].shape
    n_sel = len(layers)
    n_blk = len(ws) * n_sel
    tr = _rows_tile(r, c, budget=CAST_VMEM // (3 * n_blk))

    def body(chip_ref, *refs):
        for j in range(n_blk):
            refs[n_blk + 1 + j][...] = refs[j][...].astype(BF16)

    layer_spec = lambda l: pl.BlockSpec((None, tr, c), lambda i, ch: (l, i, 0))
    outs = pl.pallas_call(
        body, name="cast_place",
        grid_spec=pltpu.PrefetchScalarGridSpec(
            num_scalar_prefetch=1, grid=(r // tr,),
            in_specs=[layer_spec(l) for _ in ws for l in layers] + [pl.BlockSpec(memory_space=pl.ANY)],
            out_specs=[pl.BlockSpec((None, tr, c), lambda i, ch: (ch[0], i, 0))] * n_blk),
        out_shape=[jax.ShapeDtypeStruct((N_CHIPS, r, c), BF16)] * n_blk,
        compiler_params=_params(("arbitrary",)),
    )(chip, *[w for w in ws for _ in layers], after)
    return [list(outs[i * n_sel:(i + 1) * n_sel]) for i in range(len(ws))]


def adamw(w, g, m, v, copy_g=False):
    shape = w.shape
    w2, g2, m2, v2 = (_as_rows(t) for t in (w, g, m, v))
    r, c = w2.shape
    tr = _rows_tile(r, c, budget=3 * 1024 * 1024)
    c1 = 1.0 - ADAM_B1 ** ADAM_STEP
    c2 = 1.0 - ADAM_B2 ** ADAM_STEP

    def body(w_ref, g_ref, m_ref, v_ref, d_ref, mo_ref, vo_ref, *go_ref):
        gv = g_ref[...]
        if copy_g:
            go_ref[0][...] = gv
        mn = ADAM_B1 * m_ref[...] + (1.0 - ADAM_B1) * gv
        vn = ADAM_B2 * v_ref[...] + (1.0 - ADAM_B2) * (gv * gv)
        mo_ref[...] = mn
        vo_ref[...] = vn
        d_ref[...] = -ADAM_LR * ((mn / c1) / (jnp.sqrt(vn / c2) + ADAM_EPS) + ADAM_WD * w_ref[...])

    spec = pl.BlockSpec((tr, c), lambda i: (i, 0))
    n_out = 4 if copy_g else 3
    outs = pl.pallas_call(
        body, name="adamw", grid=(r // tr,), in_specs=[spec] * 4, out_specs=[spec] * n_out,
        out_shape=[jax.ShapeDtypeStruct((r, c), F32)] * n_out, compiler_params=_params(("arbitrary",)),
    )(w2, g2, m2, v2)
    g_out = outs[3] if copy_g else g2
    return tuple(o.reshape(shape) for o in (g_out,) + tuple(outs[:3]))


def sum_devices(a, after):
    n, r, c = a.shape
    tr = _rows_tile(r, c, budget=512 * 1024)

    def body(a_ref, after_ref, o_ref):
        acc = a_ref[0]
        for j in range(1, n):
            acc = acc + a_ref[j]
        o_ref[...] = acc

    return pl.pallas_call(
        body, name="sum_devices", grid=(r // tr,),
        in_specs=[pl.BlockSpec((n, tr, c), lambda i: (0, i, 0)), pl.BlockSpec(memory_space=pl.ANY)],
        out_specs=pl.BlockSpec((tr, c), lambda i: (i, 0)),
        out_shape=jax.ShapeDtypeStruct((r, c), F32), compiler_params=_params(("arbitrary",)),
    )(a, after)


def _split_axis(r, c):
    if (r // 2) % BF16_ROWS == 0 and r % 2 == 0:
        return 0
    assert c % (2 * LANES) == 0, (r, c)
    return 1


def _half_shape(r, c):
    return (r // 2, c) if _split_axis(r, c) == 0 else (r, c // 2)


def _half_at(ref, lead, which):
    r, c = ref.shape[-2:]
    if _split_axis(r, c) == 0:
        return ref.at[(*lead, pl.ds(which * (r // 2), r // 2), slice(None))]
    return ref.at[(*lead, slice(None), pl.ds(which * (c // 2), c // 2))]


def _half_spec(r, c, lead_block, imap):
    hr, hc = _half_shape(r, c)
    if _split_axis(r, c) == 0:
        return pl.BlockSpec((*lead_block, hr, hc), lambda *a: (*imap(*a)[0], imap(*a)[1], 0))
    return pl.BlockSpec((*lead_block, hr, hc), lambda *a: (*imap(*a)[0], 0, imap(*a)[1]))


def pair_add(owns, alls, ra_owns, ra_alls, sel):
    n = len(owns)
    n_sl = alls[0].shape[0]
    halves = [_half_shape(*g.shape) for g in owns]

    def body(s_ref, *refs):
        own_refs, all_refs, ra_own_refs, ra_all_refs, pb_refs, sum_refs = (refs[i * n:(i + 1) * n] for i in range(6))
        k = pl.program_id(0)
        for t in range(n):
            pb_refs[t][...] = (all_refs[t][...].astype(F32) + ra_all_refs[t][...].astype(F32)).astype(BF16)

            @pl.when(k == s_ref[1])
            def _(t=t):
                sum_refs[t][...] = own_refs[t][...] + ra_own_refs[t][...]

    slot = lambda hs: pl.BlockSpec((None,) + hs, lambda k, sr: (k, 0, 0))
    whole = lambda hs: pl.BlockSpec(hs, lambda k, sr: (0, 0))
    outs = pl.pallas_call(
        body, name="pair_add",
        grid_spec=pltpu.PrefetchScalarGridSpec(
            num_scalar_prefetch=1, grid=(n_sl,),
            in_specs=[_half_spec(*g.shape, (), lambda k, sr: ((), sr[0])) for g in owns]
            + [_half_spec(*g.shape[1:], (None,), lambda k, sr: ((k,), sr[0])) for g in alls]
            + [whole(hs) for hs in halves] + [slot(hs) for hs in halves],
            out_specs=[slot(hs) for hs in halves] + [whole(hs) for hs in halves]),
        out_shape=[jax.ShapeDtypeStruct((n_sl,) + hs, BF16) for hs in halves]
        + [jax.ShapeDtypeStruct(hs, F32) for hs in halves],
        compiler_params=_params(("arbitrary",)),
    )(sel, *owns, *alls, *ra_owns, *ra_alls)
    return list(outs[:n]), list(outs[n:])


def chip_sum(owns, rbs, sel, shapes, accs):
    n = len(owns)
    fresh = accs[0] is None

    def body(s_ref, *refs):
        own_refs, rb_refs, o_refs = refs[:n], refs[n:2 * n], refs[-n:]
        for t in range(n):
            acc_v = own_refs[t][...]
            for j in range(N_CHIPS - 1):
                acc_v = acc_v + rb_refs[t][j].astype(F32)
            o_refs[t][...] = acc_v

    in_specs = ([pl.BlockSpec(o.shape, lambda i, sr: (0, 0)) for o in owns]
                + [pl.BlockSpec(rb.shape, lambda i, sr: (0, 0, 0)) for rb in rbs])
    args = [sel, *owns, *rbs]
    aliases = {}
    if not fresh:
        in_specs += [pl.BlockSpec(memory_space=pl.ANY)] * n
        args += list(accs)
        aliases = {1 + 2 * n + t: t for t in range(n)}
    return list(pl.pallas_call(
        body, name="chip_sum",
        grid_spec=pltpu.PrefetchScalarGridSpec(
            num_scalar_prefetch=1, grid=(1,), in_specs=in_specs,
            out_specs=[_half_spec(*shp[1:], (None,), lambda i, sr: ((sr[2],), sr[0])) for shp in shapes]),
        out_shape=[jax.ShapeDtypeStruct(shp, F32) for shp in shapes],
        input_output_aliases=aliases,
        compiler_params=_params(("arbitrary",)),
    )(*args))


def _me():
    return lax.axis_index("x"), lax.axis_index("y"), lax.axis_index("c")


def _flip(v, bit):
    return 1 - v if bit else v


def exchange8(xs, bcast):
    blk = xs.shape if bcast else xs.shape[1:]

    def body(x_ref, o_ref, send_sems, recv_sems, loc_sem):
        mx, my, mc = _me()
        me = 4 * mx + 2 * my + mc
        src = (lambda j: x_ref) if bcast else (lambda j: x_ref.at[j])
        loc = pltpu.make_async_copy(src(me), o_ref.at[me], loc_sem)
        loc.start()
        copies = []
        for o in range(1, N_DEV):
            px, py, pc = _flip(mx, o & 4), _flip(my, o & 2), _flip(mc, o & 1)
            cp = pltpu.make_async_remote_copy(
                src_ref=src(4 * px + 2 * py + pc), dst_ref=o_ref.at[me],
                send_sem=send_sems.at[o - 1], recv_sem=recv_sems.at[o - 1],
                device_id=(px, py, pc), device_id_type=MESH)
            cp.start()
            copies.append(cp)
        for cp in copies:
            cp.wait()
        loc.wait()

    return pl.pallas_call(
        body, name="exchange8_gather" if bcast else "exchange8_a2a",
        in_specs=[pl.BlockSpec(memory_space=pltpu.VMEM)], out_specs=pl.BlockSpec(memory_space=pltpu.VMEM),
        out_shape=jax.ShapeDtypeStruct((N_DEV,) + tuple(blk), xs.dtype),
        scratch_shapes=[pltpu.SemaphoreType.DMA((N_DEV - 1,)), pltpu.SemaphoreType.DMA((N_DEV - 1,)), pltpu.SemaphoreType.DMA],
        compiler_params=_params(),
    )(xs)


HBM = pl.BlockSpec(memory_space=pltpu.HBM)
SEM = pl.BlockSpec(memory_space=pltpu.SEMAPHORE)
EFFECT = pltpu.SideEffectType.DATAFLOW_SIDE_EFFECTING


def _hbm(a):
    return pltpu.with_memory_space_constraint(a, pltpu.HBM)


def _ici_copy(land, o, send_sem, recv_sem, sending):
    mx, my, mc = _me()
    px, py = _flip(mx, o & 2), _flip(my, o & 1)
    mine = _half_at(land, (2 * mx + my,), mc)
    return pltpu.make_async_remote_copy(
        src_ref=mine, dst_ref=mine if sending else _half_at(land, (2 * px + py,), mc),
        send_sem=send_sem, recv_sem=recv_sem, device_id=(px, py, mc), device_id_type=MESH)


N_PEERS = N_CHIPS - 1
DMA_SEM = pltpu.SemaphoreType.DMA(())


def gather_start(lands, groups, after, tag):
    n_layers, n = len(lands), len(lands[0])
    flat = [a for layer in lands for a in layer]
    n_in = n * n_layers
    n_grp = len(groups)
    n_sem = 2 * n_layers * n_grp * N_PEERS
    first = lambda l, g, recv: ((l * n_grp + g) * 2 + recv) * N_PEERS

    def body(*refs):
        land = refs[:n_in]
        sems = refs[n_in + 1:n_in + 1 + n_sem]
        token = refs[-1]
        for l in range(n_layers):
            for g, members in enumerate(groups):
                for t in members:
                    for o in range(1, N_CHIPS):
                        _ici_copy(land[l * n + t], o, sems[first(l, g, 0) + o - 1], sems[first(l, g, 1) + o - 1],
                                  True).start()
        token[...] = jnp.zeros_like(token)

    outs = pl.pallas_call(
        body, name=f"gather_start_{tag}",
        in_specs=[HBM] * n_in + [pl.BlockSpec(memory_space=pl.ANY)],
        out_specs=[SEM] * n_sem + [HBM] * n_in + [pl.BlockSpec(memory_space=pltpu.VMEM)],
        out_shape=[DMA_SEM] * n_sem + [pltpu.HBM(a.shape, a.dtype) for a in flat]
        + [jax.ShapeDtypeStruct((8, LANES), F32)],
        input_output_aliases={i: i + n_sem for i in range(n_in)},
        compiler_params=pltpu.CompilerParams(has_side_effects=EFFECT),
    )(*[_hbm(a) for a in flat], after)
    sems = [[(list(outs[first(l, g, 0):first(l, g, 0) + N_PEERS]), list(outs[first(l, g, 1):first(l, g, 1) + N_PEERS]))
             for g in range(n_grp)] for l in range(n_layers)]
    lands_thru = [list(outs[n_sem + l * n:n_sem + (l + 1) * n]) for l in range(n_layers)]
    return sems, lands_thru, outs[-1]


def gather_wait(tag, sems, lands, after):
    n = len(lands)
    send_sems, recv_sems = sems

    def body(*refs):
        land = refs[:n]
        send_r = refs[n:n + N_PEERS]
        recv_r = refs[n + N_PEERS:n + 2 * N_PEERS]
        for t in range(n):
            for o in range(1, N_CHIPS):
                _ici_copy(land[t], o, send_r[o - 1], recv_r[o - 1], True).wait_send()
                _ici_copy(land[t], o, send_r[o - 1], recv_r[o - 1], False).wait_recv()

    return list(pl.pallas_call(
        body, name=f"gather_wait_{tag}",
        in_specs=[HBM] * n + [SEM] * (2 * N_PEERS) + [pl.BlockSpec(memory_space=pl.ANY)],
        out_specs=[HBM] * n,
        out_shape=[pltpu.HBM(a.shape, a.dtype) for a in lands],
        input_output_aliases={i: i for i in range(n)},
        compiler_params=pltpu.CompilerParams(has_side_effects=EFFECT),
    )(*lands, *send_sems, *recv_sems, after))


def gather_forward(lands):
    n = len(lands)

    def body(*refs):
        dst = refs[n:2 * n]
        send_sems, recv_sems = refs[2 * n:]
        mx, my, mc = _me()
        fwds = []
        for t in range(n):
            for o in range(1, N_CHIPS):
                slot = 2 * _flip(mx, o & 2) + _flip(my, o & 1)
                mine = _half_at(dst[t], (slot,), mc)
                theirs = _half_at(dst[t], (slot,), 1 - mc)
                cp = pltpu.make_async_remote_copy(
                    src_ref=mine, dst_ref=mine, send_sem=send_sems.at[t, o - 1], recv_sem=recv_sems.at[t, o - 1],
                    device_id=(mx, my, 1 - mc), device_id_type=MESH)
                cp.start()
                fwds.append((cp, pltpu.make_async_remote_copy(
                    src_ref=theirs, dst_ref=theirs, send_sem=send_sems.at[t, o - 1], recv_sem=recv_sems.at[t, o - 1],
                    device_id=(mx, my, 1 - mc), device_id_type=MESH)))
        for cp, arrival in fwds:
            cp.wait_send()
            arrival.wait_recv()

    any_spec = pl.BlockSpec(memory_space=pl.ANY)
    return list(pl.pallas_call(
        body, name="gather_forward",
        in_specs=[any_spec] * n, out_specs=[any_spec] * n,
        out_shape=[jax.ShapeDtypeStruct(a.shape, a.dtype) for a in lands],
        input_output_aliases={t: t for t in range(n)},
        scratch_shapes=[pltpu.SemaphoreType.DMA((n, N_CHIPS - 1)), pltpu.SemaphoreType.DMA((n, N_CHIPS - 1))],
        compiler_params=_params(),
    )(*lands))


def _scatter_copy(src, land, o, send_sem, recv_sem):
    mx, my, mc = _me()
    px, py = _flip(mx, o & 2), _flip(my, o & 1)
    return pltpu.make_async_remote_copy(
        src_ref=src.at[2 * px + py], dst_ref=land.at[o - 1],
        send_sem=send_sem, recv_sem=recv_sem, device_id=(px, py, mc), device_id_type=MESH)


def scatter_start(pbs, tag, after):
    n = len(pbs)
    lands = [lax.empty((N_CHIPS - 1,) + p.shape[1:], p.dtype) for p in pbs]

    def body(*refs):
        src = refs[:n]
        land = refs[n:2 * n]
        send_sems = refs[2 * n + 1:2 * n + 1 + N_PEERS]
        recv_sems = refs[2 * n + 1 + N_PEERS:2 * n + 1 + 2 * N_PEERS]
        token = refs[-1]
        for t in range(n):
            for o in range(1, N_CHIPS):
                _scatter_copy(src[t], land[t], o, send_sems[o - 1], recv_sems[o - 1]).start()
        token[...] = jnp.zeros_like(token)

    n_sem = 2 * N_PEERS
    arrs = list(pbs) + lands
    outs = pl.pallas_call(
        body, name=f"scatter_start_{tag}",
        in_specs=[HBM] * (2 * n) + [pl.BlockSpec(memory_space=pl.ANY)],
        out_specs=[SEM] * n_sem + [HBM] * (2 * n) + [pl.BlockSpec(memory_space=pltpu.VMEM)],
        out_shape=[DMA_SEM] * n_sem + [pltpu.HBM(a.shape, a.dtype) for a in arrs]
        + [jax.ShapeDtypeStruct((8, LANES), F32)],
        input_output_aliases={i: i + n_sem for i in range(2 * n)},
        compiler_params=pltpu.CompilerParams(has_side_effects=EFFECT),
    )(*[_hbm(a) for a in arrs], after)
    return (list(outs[:N_PEERS]), list(outs[N_PEERS:n_sem]), list(outs[n_sem:n_sem + n]),
            list(outs[n_sem + n:n_sem + 2 * n]), outs[-1])


def scatter_wait(tag, send_sems, recv_sems, pbs, lands, after):
    n = len(pbs)

    def body(*refs):
        src = refs[:n]
        land = refs[n:2 * n]
        send_r = refs[2 * n:2 * n + N_PEERS]
        recv_r = refs[2 * n + N_PEERS:2 * n + 2 * N_PEERS]
        for t in range(n):
            for o in range(1, N_CHIPS):
                cp = _scatter_copy(src[t], land[t], o, send_r[o - 1], recv_r[o - 1])
                cp.wait_send()
                cp.wait_recv()

    arrs = list(pbs) + list(lands)
    outs = pl.pallas_call(
        body, name=f"scatter_wait_{tag}",
        in_specs=[HBM] * (2 * n) + [SEM] * (2 * N_PEERS) + [pl.BlockSpec(memory_space=pl.ANY)],
        out_specs=[HBM] * (2 * n),
        out_shape=[pltpu.HBM(a.shape, a.dtype) for a in arrs],
        input_output_aliases={i: i for i in range(2 * n)},
        compiler_params=pltpu.CompilerParams(has_side_effects=EFFECT),
    )(*arrs, *send_sems, *recv_sems, after)
    return list(outs[n:])


def _pair_copies(srcs, lands, send_sem, recv_sem):
    mx, my, mc = _me()
    return [pltpu.make_async_remote_copy(
        src_ref=_half_at(src, (slice(None),) * (len(src.shape) - 2), 1 - mc), dst_ref=land,
        send_sem=send_sem, recv_sem=recv_sem, device_id=(mx, my, 1 - mc), device_id_type=MESH)
        for src, land in zip(srcs, lands)]


def pair_start(gs, tag, after):
    n = len(gs)
    lands = [lax.empty(g.shape[:-2] + _half_shape(*g.shape[-2:]), g.dtype) for g in gs]

    def body(*refs):
        send_sem, recv_sem = refs[2 * n + 1], refs[2 * n + 2]
        token = refs[-1]
        for cp in _pair_copies(refs[:n], refs[n:2 * n], send_sem, recv_sem):
            cp.start()
        token[...] = jnp.zeros_like(token)

    arrs = list(gs) + lands
    outs = pl.pallas_call(
        body, name=f"pair_start_{tag}",
        in_specs=[HBM] * (2 * n) + [pl.BlockSpec(memory_space=pl.ANY)],
        out_specs=[SEM, SEM] + [HBM] * (2 * n) + [pl.BlockSpec(memory_space=pltpu.VMEM)],
        out_shape=[DMA_SEM, DMA_SEM] + [pltpu.HBM(a.shape, a.dtype) for a in arrs] + [jax.ShapeDtypeStruct((8, LANES), F32)],
        input_output_aliases={i: i + 2 for i in range(2 * n)},
        compiler_params=pltpu.CompilerParams(has_side_effects=EFFECT),
    )(*[_hbm(a) for a in arrs], after)
    return outs[0], outs[1], list(outs[2:2 + n]), list(outs[2 + n:2 + 2 * n]), outs[-1]


def pair_wait(tag, send_sem, recv_sem, gs, lands, after):
    n = len(gs)

    def body(*refs):
        for cp in _pair_copies(refs[:n], refs[n:2 * n], refs[2 * n], refs[2 * n + 1]):
            cp.wait_send()
            cp.wait_recv()

    arrs = list(gs) + list(lands)
    outs = pl.pallas_call(
        body, name=f"pair_wait_{tag}",
        in_specs=[HBM] * (2 * n) + [SEM, SEM, pl.BlockSpec(memory_space=pl.ANY)],
        out_specs=[HBM] * (2 * n),
        out_shape=[pltpu.HBM(a.shape, a.dtype) for a in arrs],
        input_output_aliases={i: i for i in range(2 * n)},
        compiler_params=pltpu.CompilerParams(has_side_effects=EFFECT),
    )(*arrs, send_sem, recv_sem, after)
    return list(outs[:n]), list(outs[n:])


def _gather8_copy(x, land, o, send_sem, recv_sem, sending):
    mx, my, mc = _me()
    px, py, pc = _flip(mx, o & 4), _flip(my, o & 2), _flip(mc, o & 1)
    slot = 4 * mx + 2 * my + mc if sending else 4 * px + 2 * py + pc
    return pltpu.make_async_remote_copy(
        src_ref=x, dst_ref=land.at[slot], send_sem=send_sem, recv_sem=recv_sem,
        device_id=(px, py, pc), device_id_type=MESH)


def gather8_start(x, land, after, tag):
    n_peer = N_DEV - 1

    def body(x_ref, land_ref, after_ref, *rest):
        send_sems, recv_sems = rest[:n_peer], rest[n_peer:2 * n_peer]
        token = rest[-1]
        for o in range(1, N_DEV):
            _gather8_copy(x_ref, land_ref, o, send_sems[o - 1], recv_sems[o - 1], True).start()
        token[...] = jnp.zeros_like(token)

    outs = pl.pallas_call(
        body, name=f"gather8_start_{tag}",
        in_specs=[HBM, HBM, pl.BlockSpec(memory_space=pl.ANY)],
        out_specs=[SEM] * (2 * n_peer) + [HBM, HBM, pl.BlockSpec(memory_space=pltpu.VMEM)],
        out_shape=[DMA_SEM] * (2 * n_peer) + [pltpu.HBM(x.shape, x.dtype), pltpu.HBM(land.shape, land.dtype),
                                              jax.ShapeDtypeStruct((8, LANES), F32)],
        input_output_aliases={0: 2 * n_peer, 1: 2 * n_peer + 1},
        compiler_params=pltpu.CompilerParams(has_side_effects=EFFECT),
    )(_hbm(x), _hbm(land), after)
    return list(outs[:n_peer]), list(outs[n_peer:2 * n_peer]), outs[2 * n_peer], outs[2 * n_peer + 1], outs[-1]


def gather8_wait(tag, send_sems, recv_sems, x, land, after):
    n_peer = N_DEV - 1

    def body(x_ref, land_ref, *rest):
        send_r, recv_r = rest[:n_peer], rest[n_peer:2 * n_peer]
        for o in range(1, N_DEV):
            _gather8_copy(x_ref, land_ref, o, send_r[o - 1], recv_r[o - 1], True).wait_send()
            _gather8_copy(x_ref, land_ref, o, send_r[o - 1], recv_r[o - 1], False).wait_recv()

    return pl.pallas_call(
        body, name=f"gather8_wait_{tag}",
        in_specs=[HBM, HBM] + [SEM] * (2 * n_peer) + [pl.BlockSpec(memory_space=pl.ANY)],
        out_specs=[HBM, HBM],
        out_shape=[pltpu.HBM(x.shape, x.dtype), pltpu.HBM(land.shape, land.dtype)],
        input_output_aliases={0: 0, 1: 1},
        compiler_params=pltpu.CompilerParams(has_side_effects=EFFECT),
    )(x, land, *send_sems, *recv_sems, after)[1]


def _fill_copies(fs, send_sem, recv_sem, sending):
    mx, my, mc = _me()
    out = []
    for f in fs:
        region = _half_at(f, (slice(None),), mc if sending else 1 - mc)
        out.append(pltpu.make_async_remote_copy(
            src_ref=region, dst_ref=region, send_sem=send_sem, recv_sem=recv_sem,
            device_id=(mx, my, 1 - mc), device_id_type=MESH))
    return out


def fill_start(fs, tag, after):
    n = len(fs)

    def body(*refs):
        send_sem, recv_sem = refs[n + 1], refs[n + 2]
        token = refs[-1]
        for cp in _fill_copies(refs[:n], send_sem, recv_sem, True):
            cp.start()
        token[...] = jnp.zeros_like(token)

    outs = pl.pallas_call(
        body, name=f"fill_start_{tag}",
        in_specs=[HBM] * n + [pl.BlockSpec(memory_space=pl.ANY)],
        out_specs=[SEM, SEM] + [HBM] * n + [pl.BlockSpec(memory_space=pltpu.VMEM)],
        out_shape=[DMA_SEM, DMA_SEM] + [pltpu.HBM(f.shape, f.dtype) for f in fs] + [jax.ShapeDtypeStruct((8, LANES), F32)],
        input_output_aliases={i: i + 2 for i in range(n)},
        compiler_params=pltpu.CompilerParams(has_side_effects=EFFECT),
    )(*[_hbm(f) for f in fs], after)
    return outs[0], outs[1], list(outs[2:2 + n]), outs[-1]


def fill_wait(tag, send_sem, recv_sem, fs, after):
    n = len(fs)

    def body(*refs):
        for cp in _fill_copies(refs[:n], refs[n], refs[n + 1], True):
            cp.wait_send()
        for cp in _fill_copies(refs[:n], refs[n], refs[n + 1], False):
            cp.wait_recv()

    return list(pl.pallas_call(
        body, name=f"fill_wait_{tag}",
        in_specs=[HBM] * n + [SEM, SEM, pl.BlockSpec(memory_space=pl.ANY)],
        out_specs=[HBM] * n,
        out_shape=[pltpu.HBM(f.shape, f.dtype) for f in fs],
        input_output_aliases={i: i for i in range(n)},
        compiler_params=pltpu.CompilerParams(has_side_effects=EFFECT),
    )(*fs, send_sem, recv_sem, after))


def _pack_rows(parts, d):
    rows, spans = [], []
    at = 0
    for p in parts:
        flat = p.reshape(-1)
        n_rows = -(-flat.shape[0] // (8 * d)) * 8
        flat = jnp.pad(flat, (0, n_rows * d - flat.shape[0]))
        rows.append(flat.reshape(n_rows, d))
        spans.append((at, p.shape))
        at += n_rows
    return jnp.concatenate(rows, axis=0), spans


def _unpack_rows(packed, spans):
    lead, d = packed.shape[:-2], packed.shape[-1]
    out = []
    for at, shape in spans:
        n = math.prod(shape)
        n_rows = -(-n // d)
        out.append(packed[..., at:at + n_rows, :].reshape(lead + (-1,))[..., :n].reshape(lead + tuple(shape)))
    return out


def _rotate_half_matrix():
    half = QK_ROPE // 2
    idx = jnp.arange(QK_ROPE)
    src = jnp.where(idx < half, idx + half, idx - half)
    sign = jnp.where(idx < half, -1.0, 1.0)
    return (jnp.zeros((QK_ROPE, QK_ROPE), F32).at[src, idx].set(sign)).astype(BF16)


def kernel(x, c, positions, ada_w, ada_b, ffn1_norm, ffn1_w_gate, ffn1_w_up, ffn1_w_down, mix_norm, w_in, pool_w, pool_scale, q_a_norm, w_q_b, kv_a_norm, w_kv_b, w_out, ffn2_norm, ffn2_w_gate, ffn2_w_up, ffn2_w_down, final_norm, loss_target, m_ada_w, m_ada_b, m_ffn1_norm, m_ffn1_w_gate, m_ffn1_w_up, m_ffn1_w_down, m_mix_norm, m_w_in, m_pool_w, m_pool_scale, m_q_a_norm, m_w_q_b, m_kv_a_norm, m_w_kv_b, m_w_out, m_ffn2_norm, m_ffn2_w_gate, m_ffn2_w_up, m_ffn2_w_down, m_final_norm, v_ada_w, v_ada_b, v_ffn1_norm, v_ffn1_w_gate, v_ffn1_w_up, v_ffn1_w_down, v_mix_norm, v_w_in, v_pool_w, v_pool_scale, v_q_a_norm, v_w_q_b, v_kv_a_norm, v_w_kv_b, v_w_out, v_ffn2_norm, v_ffn2_w_gate, v_ffn2_w_up, v_ffn2_w_down, v_final_norm):
    mx, my, mc = _me()
    chip = 2 * mx + my
    half = jnp.reshape(mc, (1,)).astype(jnp.int32)
    chip1 = jnp.reshape(chip, (1,)).astype(jnp.int32)
    n_layers, d, ada_cols = ada_w.shape
    xt = x[0]
    tgt = loss_target[0]

    inv_freq = 1.0 / (ROPE_THETA ** (jnp.arange(0, QK_ROPE, 2, dtype=F32) / QK_ROPE))
    ang = positions[0].astype(F32)[:, None] * inv_freq
    ang = jnp.concatenate([ang, ang], axis=-1)
    cos, sin = jnp.cos(ang), jnp.sin(ang)
    rot = _rotate_half_matrix()
    rot_t = rot.T

    c_all = exchange8(c, True).reshape(N_DEV, d)
    c16 = jnp.pad(c_all, ((0, 8), (0, 0)))
    ada_b_loc = lax.dynamic_slice_in_dim(ada_b, chip * ada_cols, ada_cols, axis=1).reshape(n_layers, 1, ada_cols)
    mod_part = ada_fwd(c16, ada_w, ada_b_loc)[:, :N_DEV]
    mod_got = exchange8(jnp.transpose(mod_part, (1, 0, 2)), False)
    mod = jnp.transpose(mod_got.reshape(N_CHIPS, 2, n_layers, ada_cols)[:, 0], (1, 0, 2))
    mod = mod.reshape(n_layers, 9, 1, d)

    tr = lambda a: jnp.transpose(a, (0, 2, 1))
    local = [tr(ffn1_w_gate), tr(ffn1_w_up), ffn1_w_down, tr(w_in), tr(w_q_b), w_kv_b, w_out,
             tr(ffn2_w_gate), tr(ffn2_w_up), ffn2_w_down]
    ffn1_pos, mixer_pos, ffn2_pos = (0, 1, 2), (3, 4, 5, 6), (7, 8, 9)
    rest_pos = mixer_pos + ffn2_pos

    def cast_all(layers, after):
        by_shape = {}
        for t, w in enumerate(local):
            by_shape.setdefault(w.shape, []).append(t)
        out = [None] * len(local)
        for ts in by_shape.values():
            for t, per_layer in zip(ts, cast_place([local[t] for t in ts], chip1, layers, after)):
                out[t] = per_layer
        return out

    placed = cast_all((0,), mod)
    g_sems, lands_fly, g_token = gather_start([[p[0] for p in placed]], (ffn1_pos, mixer_pos, ffn2_pos), mod, "first")
    if n_layers > 1:
        later = tuple(range(1, n_layers))
        placed = cast_all(later, g_token)
        more_sems, more_fly, g_token = gather_start(
            [[p[j] for p in placed] for j in range(len(later))], (ffn1_pos, rest_pos), g_token, "rest")
        g_sems, lands_fly = g_sems + more_sems, lands_fly + more_fly
    gathered = []

    row = lambda a, l: a[l].reshape(1, -1)
    saved = []
    for l in range(n_layers):
        def fetch(tag, group, members, after, l=l):
            return gather_forward(gather_wait(tag, g_sems[l][group], [lands_fly[l][t] for t in members], after))

        g1, u1, d1 = fetch(f"{l}a", 0, ffn1_pos, xt if l else g_token)
        sv = dict(x0=xt)
        xt, sv["h1"], sv["a1"], sv["sl1"], sv["dsu1"], sv["y1"] = ffn_fwd(
            xt, row(ffn1_norm, l), mod[l, 0], mod[l, 1], mod[l, 2], g1, u1, d1)
        sv["x1"] = xt
        if l == 0:
            win, wq, wkv, wout = fetch("0b", 1, mixer_pos, xt)
        else:
            win, wq, wkv, wout, g2, u2, d2 = fetch(f"{l}b", 1, rest_pos, xt)
        win = win.reshape(-1, d)
        sv["h2"], u, cq, ckv, kr = mix_in_fwd(xt, row(mix_norm, l), mod[l, 3], mod[l, 4], win)
        sv["cq"], sv["ckv"] = cq, ckv
        yp, sv["diff"] = pool_fwd(u, pool_w[l], row(pool_scale, l))
        qh, kh, vh, sv["ql"], sv["kvl"] = mla_qkv_fwd(
            cq, ckv, kr, row(q_a_norm, l), row(kv_a_norm, l), wq, wkv, cos, sin, rot)
        sv["qkv"] = (qh, kh, vh)
        om = attn_fwd(qh, kh, vh)
        xt, sv["ycat"], sv["y2"] = out_proj_fwd(yp, om, wout, xt, mod[l, 5])
        sv["x2"] = xt
        if l == 0:
            g2, u2, d2 = fetch("0c", 2, ffn2_pos, xt)
        gathered.append([g1, u1, d1, win, wq, wkv, wout, g2, u2, d2])
        xt, sv["h3"], sv["a3"], sv["sl3"], sv["dsu3"], sv["y3"] = ffn_fwd(
            xt, row(ffn2_norm, l), mod[l, 6], mod[l, 7], mod[l, 8], g2, u2, d2)
        saved.append(sv)

    loss_vec, dx, d_final_norm = final_loss(xt, final_norm.reshape(1, d), tgt)
    loss = lax.psum(loss_vec[0, 0], ("x", "y", "c"))

    none = [None] * n_layers
    dmods, dnorm1, dnorm2, dnorm3 = list(none), list(none), list(none), list(none)
    dpw, dps, dqan_l, dkvan_l = list(none), list(none), list(none), list(none)
    reduced = [None] * len(local)
    stages = []
    sel_of = lambda l: jnp.stack([mc, chip, jnp.asarray(l, mc.dtype)]).astype(jnp.int32)

    def to_chips(job, after_wait, after_start):
        send, recv, g_fly, lands_p = job.pop("pair")
        g_fly, got = pair_wait(job["tag"], send, recv, g_fly, lands_p, after_wait)
        n_w = len(job["pos"])
        pbs, job["owns"] = pair_add(g_fly[:n_w], g_fly[n_w:], got[:n_w], got[n_w:], sel_of(job["l"]))
        job["scatter"] = scatter_start(pbs, job["tag"], after_start)
        return job["scatter"][4][0, 0]

    def finish(job, after):
        s_send, s_recv, pbs_fly, lands_j, _ = job.pop("scatter")
        parts = scatter_wait(job["tag"], s_send, s_recv, pbs_fly, lands_j, after)
        sums = chip_sum(job["owns"], parts, sel_of(job["l"]), [(n_layers,) + shp for shp in job["shapes"]],
                        [reduced[t] for t in job["pos"]])
        for t, total_t in zip(job["pos"], sums):
            reduced[t] = total_t

    def checkpoint(tag, l, positions, grads_, done, before_scatter=None):
        send, recv, g_fly, lands_p, tok = pair_start([g[0] for g in grads_] + [g[1] for g in grads_], tag, done)
        order = tok[0, 0]
        if stages:
            order = order + to_chips(stages[-1], done, done if before_scatter is None else before_scatter)
        if len(stages) >= 3:
            finish(stages[-3], done)
        stages.append(dict(tag=tag, l=l, pos=positions, shapes=[g[0].shape for g in grads_],
                           pair=(send, recv, g_fly, lands_p)))
        return order

    def small_gather(tag, parts, after):
        packed, spans = _pack_rows(parts, d)
        land = lax.dynamic_update_index_in_dim(lax.empty((N_DEV,) + packed.shape, F32), packed, 4 * mx + 2 * my + mc, 0)
        return gather8_start(packed, land, after, tag), spans

    order = None

    for l in reversed(range(n_layers)):
        sv = saved[l]
        g1, u1, d1, win, wq, wkv, wout, g2, u2, d2 = gathered[l]
        win = win.reshape(-1, d)
        gt3 = mod[l, 8] if order is None else mod[l, 8] + order
        dy, dgt, dup = ffn_bwd_act(dx, sv["sl3"], sv["dsu3"], gt3, d2)
        dx, dvec3 = ffn_bwd_in(dx, sv["x2"], sv["y3"], dgt, dup, row(ffn2_norm, l), mod[l, 7], g2, u2)
        (g_g2, g_u2), g_d2 = tn_mm_pair(dgt, dup, sv["h3"], chip1), nn_mm(sv["a3"], dy, chip1)
        dy2, dyp, dom, dg2 = out_proj_bwd(dx, sv["y2"], mod[l, 5], wout)
        g_wout = nn_mm(sv["ycat"], dy2, chip1)
        qh, kh, vh = sv["qkv"]
        dqh, dkh, dvh = attn_bwd(qh, kh, vh, dom)
        dcq, dckv, dkr_in, gq, gkv, dqan_l[l], dkvan_l[l] = mla_qkv_bwd(
            dqh, dkh, dvh, sv["cq"], sv["ckv"], row(q_a_norm, l), row(kv_a_norm, l), wq, wkv, cos, sin, rot_t)
        g_wq, g_wkv = tn_mm(gq, sv["ql"][None], chip1), tn_mm(sv["kvl"][None], gkv, chip1)
        du, dpw[l], dps[l] = pool_bwd(dyp, sv["diff"], pool_w[l], row(pool_scale, l))
        dx, dz, dvec2 = mix_in_bwd(dx, du, dcq, dckv, dkr_in, sv["x1"], row(mix_norm, l), mod[l, 4], win)
        g_win = nn_mm(dz.reshape(N_CHIPS, -1, dz.shape[1]), sv["h2"], chip1)
        dnorm2[l], dnorm3[l] = dvec2[3], dvec3[3]
        dmod_rest = jnp.concatenate([dvec2[0:2], dg2, dvec3[0:3]], axis=0)
        if l == 0:
            early = small_gather("early", [jnp.stack(dmods[1:]), dmod_rest, jnp.stack(dnorm1[1:]), jnp.stack(dnorm2),
                                           jnp.stack(dnorm3), d_final_norm, jnp.stack(dps), jnp.stack(dqan_l),
                                           jnp.stack(dkvan_l), jnp.stack(dpw)], dx)
        order = checkpoint(f"{l}a", l, rest_pos, [g_win, g_wq, g_wkv, g_wout, g_g2, g_u2, g_d2], dx,
                           early[0][4] if l == 0 else None)
        dy, dgt, dup = ffn_bwd_act(dx, sv["sl1"], sv["dsu1"], mod[l, 2] + order, d1)
        dx, dvec1 = ffn_bwd_in(dx, sv["x0"], sv["y1"], dgt, dup, row(ffn1_norm, l), mod[l, 1], g1, u1)
        (g_g1, g_u1), g_d1 = tn_mm_pair(dgt, dup, sv["h1"], chip1), nn_mm(sv["a1"], dy, chip1)
        dmods[l] = jnp.concatenate([dvec1[0:3], dmod_rest], axis=0)
        dnorm1[l] = dvec1[3]
        if l == 0:
            late = small_gather("late", [dvec1[0:3], dvec1[3]], dx)
        order = checkpoint(f"{l}b", l, ffn1_pos, [g_g1, g_u1, g_d1], dx, late[0][4] if l == 0 else None)

    to_chips(stages[-1], stages[-2]["scatter"][4], stages[-2]["scatter"][4])
    sent = stages[-1]["scatter"][4]
    got_early = gather8_wait("early", *early[0][:4], sent)
    got_late = gather8_wait("late", *late[0][:4], sent)
    each_rest, each0_rest = _unpack_rows(got_early, early[1])[:2]
    each0_first = _unpack_rows(got_late, late[1])[0]
    dmod_all = jnp.concatenate([jnp.concatenate([each0_first, each0_rest], axis=1)[:, None], each_rest], axis=1)
    dmod_all = dmod_all.reshape(N_DEV, n_layers, 9 * d)
    dmod_loc = lax.dynamic_slice_in_dim(dmod_all, chip * ada_cols, ada_cols, axis=2)
    dmod16 = jnp.pad(jnp.transpose(dmod_loc, (1, 0, 2)), ((0, 0), (0, 8), (0, 0)))

    weights = [ada_w, ada_b, ffn1_norm, ffn1_w_gate, ffn1_w_up, ffn1_w_down, mix_norm, w_in, pool_w, pool_scale,
               q_a_norm, w_q_b, kv_a_norm, w_kv_b, w_out, ffn2_norm, ffn2_w_gate, ffn2_w_up, ffn2_w_down, final_norm]
    ms = [m_ada_w, m_ada_b, m_ffn1_norm, m_ffn1_w_gate, m_ffn1_w_up, m_ffn1_w_down, m_mix_norm, m_w_in, m_pool_w,
          m_pool_scale, m_q_a_norm, m_w_q_b, m_kv_a_norm, m_w_kv_b, m_w_out, m_ffn2_norm, m_ffn2_w_gate, m_ffn2_w_up,
          m_ffn2_w_down, m_final_norm]
    vs = [v_ada_w, v_ada_b, v_ffn1_norm, v_ffn1_w_gate, v_ffn1_w_up, v_ffn1_w_down, v_mix_norm, v_w_in, v_pool_w,
          v_pool_scale, v_q_a_norm, v_w_q_b, v_kv_a_norm, v_w_kv_b, v_w_out, v_ffn2_norm, v_ffn2_w_gate, v_ffn2_w_up,
          v_ffn2_w_down, v_final_norm]
    transposed = (3, 4, 7, 11, 16, 17)
    outs = [None] * len(weights)

    outs[0] = adamw(ada_w, ada_bwd(c16, dmod16), m_ada_w, v_ada_w)
    for job in stages[-3:]:
        finish(job, outs[0][1])
    fill_a = fill_start([reduced[t] for t in rest_pos], "a", outs[0][1])
    fill_b = fill_start([reduced[t] for t in ffn1_pos], "b", fill_a[3])

    (g_dmod_rest, g_dmod0_rest, g_n1_rest, g_n2, g_n3, g_fn, g_ps, g_qan, g_kvan, g_pw) = _unpack_rows(
        sum_devices(got_early, fill_b[3]), early[1])
    g_dmod0_first, g_n1_first = _unpack_rows(sum_devices(got_late, fill_b[3]), late[1])
    g_ada_b = jnp.concatenate([jnp.concatenate([g_dmod0_first, g_dmod0_rest], axis=0)[None], g_dmod_rest], axis=0)
    g_n1 = jnp.concatenate([g_n1_first[None], g_n1_rest], axis=0)
    grads = [None, g_ada_b, g_n1, None, None, None, g_n2, None, g_pw, g_ps, g_qan, None, g_kvan, None, None, g_n3,
             None, None, None, g_fn]
    big = [i for i, g in enumerate(grads) if g is None and i > 0]
    for i, (w, g, m, v) in enumerate(zip(weights, grads, ms, vs)):
        if g is not None:
            outs[i] = adamw(w, g.reshape(w.shape), m, v)

    def update(positions, fly, after):
        filled = fill_wait(fly[0], fly[1], fly[2], fly[3], after)
        for t, g in zip(positions, filled):
            i = big[t]
            if i in transposed:
                outs[i] = tuple(tr(o) for o in adamw(tr(weights[i]), g, tr(ms[i]), tr(vs[i]), copy_g=True))
            else:
                outs[i] = adamw(weights[i], g, ms[i], vs[i], copy_g=True)

    update(rest_pos, ("a",) + tuple(fill_a[:3]), outs[8][1])
    update(ffn1_pos, ("b",) + tuple(fill_b[:3]), outs[big[rest_pos[-1]]][1])
    return (loss, dx.reshape(x.shape), *[t[0] for t in outs], *[t[1] for t in outs], *[t[2] for t in outs],
            *[t[3] for t in outs])
```

```python
import math

import jax
import jax.numpy as jnp
from jax import lax
from jax.experimental import pallas as pl
from jax.experimental.pallas import tpu as pltpu

F32 = jnp.float32
BF16 = jnp.bfloat16
MESH = pl.DeviceIdType.MESH

EPS = 1e-6
ROPE_THETA = 10000.0
N_HEADS = 4
QK_NOPE = 128
QK_ROPE = 64
V_HEAD = 128
POOL_WINDOWS = (2, 4, 8, 16)
POOL_GC = 128
POOL_WIDTH = POOL_GC * len(POOL_WINDOWS)
Q_LORA = 384
KV_LORA = 256
SOFTMAX_SCALE = 1.0 / math.sqrt(QK_NOPE + QK_ROPE)
N_CHIPS = 4
N_DEV = 8

ADAM_LR = 0.001
ADAM_B1 = 0.9
ADAM_B2 = 0.999
ADAM_EPS = 1e-08
ADAM_WD = 0.01
ADAM_STEP = 10

ROW_TILE = 512
ATT_TILE = 512
VMEM_LIMIT = 56 * 1024 * 1024
BF16_ROWS = 16
LANES = 128


def _params(sem=None, vmem=VMEM_LIMIT):
    return pltpu.CompilerParams(dimension_semantics=sem, vmem_limit_bytes=vmem)


def _dot(a, b):
    return jnp.dot(a, b, preferred_element_type=F32)


def _dot_nt(a, b):
    return lax.dot_general(a, b, (((1,), (1,)), ((), ())), preferred_element_type=F32)


def _dot_tn(a, b):
    return lax.dot_general(a, b, (((0,), (0,)), ((), ())), preferred_element_type=F32)


def _dot_exact(t, perm):
    t1 = t.astype(BF16)
    r1 = t - t1.astype(F32)
    t2 = r1.astype(BF16)
    t3 = (r1 - t2.astype(F32)).astype(BF16)
    return _dot(t1, perm) + _dot(t2, perm) + _dot(t3, perm)


def _sum0(a):
    return jnp.sum(a, axis=0, keepdims=True)


def _rms(xt):
    r = lax.rsqrt(jnp.mean(xt * xt, axis=-1, keepdims=True) + EPS)
    return xt * r, r


def _rms_bwd(dy, xt, g):
    xhat, r = _rms(xt)
    dxhat = dy * g
    dx = r * (dxhat - xhat * jnp.mean(dxhat * xhat, axis=-1, keepdims=True))
    return dx, _sum0(dy * xhat)


def _normmod_bwd(dh, xt, gn, sc):
    xhat, _ = _rms(xt)
    dn = dh * (1.0 + sc)
    dx, dgn = _rms_bwd(dn, xt, gn)
    return dx, _sum0(dh), _sum0(dh * (xhat * gn)), dgn


def _row_tile(s):
    return min(s, ROW_TILE)


def _full(shape):
    n = len(shape)
    return pl.BlockSpec(shape, lambda *_: (0,) * n)


def _resident(shape):
    n = len(shape)
    return pl.BlockSpec(shape, lambda *_: (0,) * n, pipeline_mode=pl.Buffered(1))


def ffn_fwd(x, gn, sh, sc, gt, wg, wu, wd):
    s, d = x.shape
    k_chunks, fs, _ = wg.shape
    tm = _row_tile(s)

    def body(x_ref, gn_ref, sh_ref, sc_ref, gt_ref, wg_ref, wu_ref, wd_ref,
             xo_ref, h_ref, a_ref, sl_ref, dsu_ref, y_ref):
        xt = x_ref[...]
        xhat, _ = _rms(xt)
        h = (xhat * gn_ref[...] * (1.0 + sc_ref[...]) + sh_ref[...]).astype(BF16)
        h_ref[...] = h
        y = jnp.zeros((tm, d), F32)
        for k in range(k_chunks):
            gate = _dot_nt(h, wg_ref[k])
            up = _dot_nt(h, wu_ref[k])
            sg = jax.nn.sigmoid(gate)
            sl = gate * sg
            a = (sl * up).astype(BF16)
            a_ref[k] = a.T
            sl_ref[k] = sl.astype(BF16)
            dsu_ref[k] = (up * (sg * (1.0 + gate * (1.0 - sg)))).astype(BF16)
            y += _dot(a, wd_ref[k])
        y_ref[...] = y.astype(BF16)
        xo_ref[...] = xt + 0.5 * gt_ref[...] * y

    row = pl.BlockSpec((tm, d), lambda i: (i, 0))
    vec = pl.BlockSpec((1, d), lambda i: (0, 0))
    act = pl.BlockSpec((k_chunks, tm, fs), lambda i: (0, i, 0))
    act_shape = jax.ShapeDtypeStruct((k_chunks, s, fs), BF16)
    return pl.pallas_call(
        body, name="ffn_fwd",
        grid=(s // tm,),
        in_specs=[row, vec, vec, vec, vec, _resident(wg.shape), _resident(wu.shape), _resident(wd.shape)],
        out_specs=[row, row, pl.BlockSpec((k_chunks, fs, tm), lambda i: (0, 0, i)), act, act, row],
        out_shape=[jax.ShapeDtypeStruct((s, d), F32), jax.ShapeDtypeStruct((s, d), BF16),
                   jax.ShapeDtypeStruct((k_chunks, fs, s), BF16), act_shape, act_shape,
                   jax.ShapeDtypeStruct((s, d), BF16)],
        compiler_params=_params(("arbitrary",)),
    )(x, gn, sh, sc, gt, wg, wu, wd)


def ffn_bwd_act(dxn, sl, dsu, gt, wd):
    s, d = dxn.shape
    k_chunks, fs, _ = wd.shape
    tm = _row_tile(s)

    def body(dxn_ref, sl_ref, dsu_ref, gt_ref, wd_ref, dy_ref, dgate_ref, dup_ref):
        dy = (0.5 * gt_ref[...] * dxn_ref[...]).astype(BF16)
        dy_ref[...] = dy
        for k in range(k_chunks):
            da = _dot_nt(dy, wd_ref[k])
            dgate_ref[k] = (da * dsu_ref[k].astype(F32)).astype(BF16)
            dup_ref[k] = (da * sl_ref[k].astype(F32)).astype(BF16)

    row = pl.BlockSpec((tm, d), lambda i: (i, 0))
    act = pl.BlockSpec((k_chunks, tm, fs), lambda i: (0, i, 0))
    act_shape = jax.ShapeDtypeStruct((k_chunks, s, fs), BF16)
    return pl.pallas_call(
        body, name="ffn_bwd_act",
        grid=(s // tm,),
        in_specs=[row, act, act, pl.BlockSpec((1, d), lambda i: (0, 0)), _resident(wd.shape)],
        out_specs=[row, act, act],
        out_shape=[jax.ShapeDtypeStruct((s, d), BF16), act_shape, act_shape],
        compiler_params=_params(("arbitrary",)),
    )(dxn, sl, dsu, gt, wd)


def ffn_bwd_in(dxn, x, y, dgate, dup, gn, sc, wg, wu):
    s, d = x.shape
    k_chunks, fs, _ = wg.shape
    tm = _row_tile(s)

    def body(dxn_ref, x_ref, y_ref, dgate_ref, dup_ref, gn_ref, sc_ref, wg_ref, wu_ref, dx_ref, dvec_ref):
        i = pl.program_id(0)

        @pl.when(i == 0)
        def _():
            dvec_ref[...] = jnp.zeros_like(dvec_ref)

        dh = jnp.zeros((tm, d), F32)
        for k in range(k_chunks):
            dh += _dot(dgate_ref[k], wg_ref[k]) + _dot(dup_ref[k], wu_ref[k])
        dxn_t = dxn_ref[...]
        dx, dsh, dsc, dgn = _normmod_bwd(dh, x_ref[...], gn_ref[...], sc_ref[...])
        dx_ref[...] = dx + dxn_t
        dvec_ref[0:1, :] += dsh
        dvec_ref[1:2, :] += dsc
        dvec_ref[2:3, :] += _sum0(0.5 * dxn_t * y_ref[...].astype(F32))
        dvec_ref[3:4, :] += dgn

    row = pl.BlockSpec((tm, d), lambda i: (i, 0))
    vec = pl.BlockSpec((1, d), lambda i: (0, 0))
    act = pl.BlockSpec((k_chunks, tm, fs), lambda i: (0, i, 0))
    return pl.pallas_call(
        body, name="ffn_bwd_in",
        grid=(s // tm,),
        in_specs=[row, row, row, act, act, vec, vec, _resident(wg.shape), _resident(wu.shape)],
        out_specs=[row, pl.BlockSpec((8, d), lambda i: (0, 0))],
        out_shape=[jax.ShapeDtypeStruct((s, d), F32), jax.ShapeDtypeStruct((8, d), F32)],
        compiler_params=_params(("arbitrary",)),
    )(dxn, x, y, dgate, dup, gn, sc, wg, wu)


GRAD_COLS = 512


def _col_chunks(n):
    step = GRAD_COLS if n % GRAD_COLS == 0 else n
    return [slice(j, j + step) for j in range(0, n, step)]


def _grad_mm(dot, a, b, a_spec, b_spec, g, m, n, chip, name):
    def body(c_ref, a_ref, b_ref, own_ref, all_ref):
        a_v = a_ref[...]
        for cols in _col_chunks(n):
            res = dot(a_v, b_ref[:, cols])
            all_ref[:, cols] = res.astype(BF16)

            @pl.when(pl.program_id(0) == c_ref[0])
            def _(res=res, cols=cols):
                own_ref[:, cols] = res

    return pl.pallas_call(
        body, name=name,
        grid_spec=pltpu.PrefetchScalarGridSpec(
            num_scalar_prefetch=1, grid=(g,), in_specs=[a_spec, b_spec],
            out_specs=[pl.BlockSpec((m, n), lambda gi, c: (0, 0)), pl.BlockSpec((None, m, n), lambda gi, c: (gi, 0, 0))]),
        out_shape=[jax.ShapeDtypeStruct((m, n), F32), jax.ShapeDtypeStruct((g, m, n), BF16)],
        compiler_params=_params(("arbitrary",)),
    )(chip, a, b)


def tn_mm_pair(a1, a2, b, chip):
    g, s, m = a1.shape
    n = b.shape[1]

    def body(c_ref, a1_ref, a2_ref, b_ref, own1_ref, all1_ref, own2_ref, all2_ref):
        gi = pl.program_id(0)

        def one(a_ref, own_ref, all_ref, slot):
            a_v = a_ref[...]
            for cols in _col_chunks(n):
                res = _dot_tn(a_v, b_ref[:, cols])
                all_ref[:, cols] = res.astype(BF16)

                @pl.when(slot == c_ref[0])
                def _(res=res, cols=cols):
                    own_ref[:, cols] = res

        @pl.when(gi < g)
        def _():
            one(a1_ref, own1_ref, all1_ref, gi)

        @pl.when(gi >= g)
        def _():
            one(a2_ref, own2_ref, all2_ref, gi - g)

    first = lambda gi, c: (jnp.minimum(gi, g - 1), 0, 0)
    second = lambda gi, c: (jnp.maximum(gi - g, 0), 0, 0)
    own = pl.BlockSpec((m, n), lambda gi, c: (0, 0))
    outs = pl.pallas_call(
        body, name="tn_mm_pair",
        grid_spec=pltpu.PrefetchScalarGridSpec(
            num_scalar_prefetch=1, grid=(2 * g,),
            in_specs=[pl.BlockSpec((None, s, m), first), pl.BlockSpec((None, s, m), second),
                      pl.BlockSpec((s, n), lambda gi, c: (0, 0))],
            out_specs=[own, pl.BlockSpec((None, m, n), first), own, pl.BlockSpec((None, m, n), second)]),
        out_shape=[jax.ShapeDtypeStruct((m, n), F32), jax.ShapeDtypeStruct((g, m, n), BF16)] * 2,
        compiler_params=_params(("arbitrary",)),
    )(chip, a1, a2, b)
    return (outs[0], outs[1]), (outs[2], outs[3])


def nn_mm(a_t, b, chip):
    g, m, s = a_t.shape
    n = b.shape[1]
    return _grad_mm(_dot, a_t, b, pl.BlockSpec((None, m, s), lambda gi, c: (gi, 0, 0)),
                    pl.BlockSpec((s, n), lambda gi, c: (0, 0)), g, m, n, chip, "nn_mm")


def tn_mm(a, b, chip):
    ga, s, m = a.shape
    gb, _, n = b.shape
    a_spec = pl.BlockSpec((None, s, m), (lambda gi, c: (gi, 0, 0)) if ga > 1 else (lambda gi, c: (0, 0, 0)))
    b_spec = pl.BlockSpec((None, s, n), (lambda gi, c: (gi, 0, 0)) if gb > 1 else (lambda gi, c: (0, 0, 0)))
    return _grad_mm(_dot_tn, a, b, a_spec, b_spec, max(ga, gb), m, n, chip, "tn_mm")


def mix_in_fwd(x, gn, sh, sc, w_in_t):
    s, d = x.shape
    tm = _row_tile(s)
    o1, o2, o3 = POOL_WIDTH, POOL_WIDTH + Q_LORA, POOL_WIDTH + Q_LORA + KV_LORA

    def body(x_ref, gn_ref, sh_ref, sc_ref, w_ref, h_ref, u_ref, cq_ref, ckv_ref, kr_ref):
        xhat, _ = _rms(x_ref[...])
        h = (xhat * gn_ref[...] * (1.0 + sc_ref[...]) + sh_ref[...]).astype(BF16)
        h_ref[...] = h
        z = _dot_nt(h, w_ref[0:o3, :])
        u_ref[...] = z[:, 0:o1]
        cq_ref[...] = z[:, o1:o2]
        ckv_ref[...] = z[:, o2:o3]
        kr_ref[...] = _dot_nt(h, w_ref[o3:, :])

    row = lambda w: pl.BlockSpec((tm, w), lambda i: (i, 0))
    vec = pl.BlockSpec((1, d), lambda i: (0, 0))
    return pl.pallas_call(
        body, name="mix_in_fwd",
        grid=(s // tm,),
        in_specs=[row(d), vec, vec, vec, _full(w_in_t.shape)],
        out_specs=[row(d), row(POOL_WIDTH), row(Q_LORA), row(KV_LORA), row(QK_ROPE)],
        out_shape=[jax.ShapeDtypeStruct((s, d), BF16), jax.ShapeDtypeStruct((s, POOL_WIDTH), F32),
                   jax.ShapeDtypeStruct((s, Q_LORA), F32), jax.ShapeDtypeStruct((s, KV_LORA), F32),
                   jax.ShapeDtypeStruct((s, QK_ROPE), F32)],
        compiler_params=_params(("arbitrary",)),
    )(x, gn, sh, sc, w_in_t)


def mix_in_bwd(dxn, du, dcq, dckv, dkr, x, gn, sc, w_in_t):
    s, d = x.shape
    tm = _row_tile(s)
    o1, o2, o3 = POOL_WIDTH, POOL_WIDTH + Q_LORA, POOL_WIDTH + Q_LORA + KV_LORA
    n_z = w_in_t.shape[0]

    def body(dxn_ref, du_ref, dcq_ref, dckv_ref, dkr_ref, x_ref, gn_ref, sc_ref, w_ref, dx_ref, dz_ref, dvec_ref):
        i = pl.program_id(0)

        @pl.when(i == 0)
        def _():
            dvec_ref[...] = jnp.zeros_like(dvec_ref)

        dub = du_ref[...].astype(BF16)
        dqb = dcq_ref[...].astype(BF16)
        dkb = dckv_ref[...].astype(BF16)
        drb = dkr_ref[...].astype(BF16)
        dz_ref[0:o1, :] = dub.T
        dz_ref[o1:o2, :] = dqb.T
        dz_ref[o2:o3, :] = dkb.T
        dz_ref[o3:, :] = drb.T
        dh = (_dot(dub, w_ref[0:o1, :]) + _dot(dqb, w_ref[o1:o2, :]) + _dot(dkb, w_ref[o2:o3, :])
              + _dot(drb, w_ref[o3:, :]))
        dx, dsh, dsc, dgn = _normmod_bwd(dh, x_ref[...], gn_ref[...], sc_ref[...])
        dx_ref[...] = dx + dxn_ref[...]
        dvec_ref[0:1, :] += dsh
        dvec_ref[1:2, :] += dsc
        dvec_ref[3:4, :] += dgn

    row = lambda w: pl.BlockSpec((tm, w), lambda i: (i, 0))
    vec = pl.BlockSpec((1, d), lambda i: (0, 0))
    return pl.pallas_call(
        body, name="mix_in_bwd",
        grid=(s // tm,),
        in_specs=[row(d), row(POOL_WIDTH), row(Q_LORA), row(KV_LORA), row(QK_ROPE), row(d), vec, vec,
                  _full(w_in_t.shape)],
        out_specs=[row(d), pl.BlockSpec((n_z, tm), lambda i: (0, i)), pl.BlockSpec((8, d), lambda i: (0, 0))],
        out_shape=[jax.ShapeDtypeStruct((s, d), F32), jax.ShapeDtypeStruct((n_z, s), BF16),
                   jax.ShapeDtypeStruct((8, d), F32)],
        compiler_params=_params(("arbitrary",)),
    )(dxn, du, dcq, dckv, dkr, x, gn, sc, w_in_t)


def _window_sum(a, w, rows, forward):
    s = a.shape[0]
    step = 1
    while step < w:
        if forward:
            shifted = jnp.where(rows < s - step, pltpu.roll(a, s - step, 0), 0.0)
        else:
            shifted = jnp.where(rows >= step, pltpu.roll(a, step, 0), 0.0)
        a = a + shifted
        step *= 2
    return a


def pool_fwd(u, pool_w, pool_scale):
    s = u.shape[0]

    def body(u_ref, w_ref, sc_ref, y_ref, diff_ref):
        rows = lax.broadcasted_iota(jnp.int32, (s, POOL_GC), 0)
        for g, w in enumerate(POOL_WINDOWS):
            cols = slice(g * POOL_GC, (g + 1) * POOL_GC)
            ug = u_ref[:, cols]
            cnt = jnp.minimum(rows + 1, w).astype(F32)
            diff = (_window_sum(ug, w, rows, False) / cnt - ug).astype(BF16)
            diff_ref[:, cols] = diff
            y_ref[:, cols] = _dot(diff, w_ref[g].astype(BF16)) * sc_ref[:, cols]

    return pl.pallas_call(
        body, name="pool_fwd",
        out_shape=[jax.ShapeDtypeStruct(u.shape, F32), jax.ShapeDtypeStruct(u.shape, BF16)],
        compiler_params=_params(),
    )(u, pool_w, pool_scale)


def pool_bwd(dy, diff, pool_w, pool_scale):
    s = dy.shape[0]

    def body(dy_ref, diff_ref, w_ref, sc_ref, du_ref, dw_ref, dsc_ref):
        rows = lax.broadcasted_iota(jnp.int32, (s, POOL_GC), 0)
        for g, w in enumerate(POOL_WINDOWS):
            cols = slice(g * POOL_GC, (g + 1) * POOL_GC)
            dyg = dy_ref[:, cols]
            diff = diff_ref[:, cols]
            wb = w_ref[g].astype(BF16)
            dsc_ref[:, cols] = _sum0(dyg * _dot(diff, wb))
            dys = (dyg * sc_ref[:, cols]).astype(BF16)
            dw_ref[g] = _dot_tn(diff, dys)
            ddiff = _dot_nt(dys, wb)
            cnt = jnp.minimum(rows + 1, w).astype(F32)
            du_ref[:, cols] = _window_sum(ddiff / cnt, w, rows, True) - ddiff

    return pl.pallas_call(
        body, name="pool_bwd",
        out_shape=[jax.ShapeDtypeStruct(dy.shape, F32), jax.ShapeDtypeStruct(pool_w.shape, F32),
                   jax.ShapeDtypeStruct(pool_scale.shape, F32)],
        compiler_params=_params(),
    )(dy, diff, pool_w, pool_scale)


def mla_qkv_fwd(cq, ckv, kr, qan, kvan, wq, wkv, cos, sin, rot):
    s = cq.shape[0]
    tm = _row_tile(s)

    def body(cq_ref, ckv_ref, kr_ref, qan_ref, kvan_ref, wq_ref, wkv_ref, cos_ref, sin_ref, rot_ref,
             q_ref, k_ref, v_ref, ql_ref, kvl_ref):
        cos_t = cos_ref[...]
        sin_t = sin_ref[...]
        perm = rot_ref[...]

        def rope(t):
            return t * cos_t + _dot_exact(t, perm) * sin_t

        qhat, _ = _rms(cq_ref[...])
        ql = (qhat * qan_ref[...]).astype(BF16)
        ql_ref[...] = ql
        khat, _ = _rms(ckv_ref[...])
        kvl = (khat * kvan_ref[...]).astype(BF16)
        kvl_ref[...] = kvl
        krr = rope(kr_ref[...]).astype(BF16)
        for h in range(N_HEADS):
            q = _dot_nt(ql, wq_ref[h])
            q_ref[h, :, 0:QK_NOPE] = q[:, 0:QK_NOPE].astype(BF16)
            q_ref[h, :, QK_NOPE:] = rope(q[:, QK_NOPE:]).astype(BF16)
            kv = _dot(kvl, wkv_ref[h])
            k_ref[h, :, 0:QK_NOPE] = kv[:, 0:QK_NOPE].astype(BF16)
            k_ref[h, :, QK_NOPE:] = krr
            v_ref[h] = kv[:, QK_NOPE:].astype(BF16)

    row = lambda w: pl.BlockSpec((tm, w), lambda i: (i, 0))
    hrow = lambda w: pl.BlockSpec((N_HEADS, tm, w), lambda i: (0, i, 0))
    qk = QK_NOPE + QK_ROPE
    return pl.pallas_call(
        body, name="mla_qkv_fwd",
        grid=(s // tm,),
        in_specs=[row(Q_LORA), row(KV_LORA), row(QK_ROPE), _full(qan.shape), _full(kvan.shape),
                  _full(wq.shape), _full(wkv.shape), row(QK_ROPE), row(QK_ROPE), _full(rot.shape)],
        out_specs=[hrow(qk), hrow(qk), hrow(V_HEAD), row(Q_LORA), row(KV_LORA)],
        out_shape=[jax.ShapeDtypeStruct((N_HEADS, s, qk), BF16), jax.ShapeDtypeStruct((N_HEADS, s, qk), BF16),
                   jax.ShapeDtypeStruct((N_HEADS, s, V_HEAD), BF16), jax.ShapeDtypeStruct((s, Q_LORA), BF16),
                   jax.ShapeDtypeStruct((s, KV_LORA), BF16)],
        compiler_params=_params(("arbitrary",)),
    )(cq, ckv, kr, qan, kvan, wq, wkv, cos, sin, rot)


def _attn_probs(q_ref, k_ref, qi, tq):
    n = (qi + 1) * tq
    rows = slice(qi * tq, n)
    sc = _dot_nt(q_ref[rows, :], k_ref[0:n, :]) * SOFTMAX_SCALE
    qpos = qi * tq + lax.broadcasted_iota(jnp.int32, (tq, n), 0)
    kpos = lax.broadcasted_iota(jnp.int32, (tq, n), 1)
    sc = jnp.where(qpos >= kpos, sc, -1e30)
    e = jnp.exp(sc - jnp.max(sc, axis=-1, keepdims=True))
    return e * (1.0 / jnp.sum(e, axis=-1, keepdims=True))


def attn_fwd(q, k, v):
    nh, s, qk = q.shape
    tq = min(s, ATT_TILE)

    def body(q_ref, k_ref, v_ref, o_ref):
        for qi in range(s // tq):
            n = (qi + 1) * tq
            p = _attn_probs(q_ref, k_ref, qi, tq).astype(BF16)
            o_ref[qi * tq:n, :] = _dot(p, v_ref[0:n, :])

    head = lambda w: pl.BlockSpec((None, s, w), lambda h: (h, 0, 0))
    return pl.pallas_call(
        body, name="attn_fwd",
        grid=(nh,),
        in_specs=[head(qk), head(qk), head(V_HEAD)],
        out_specs=pl.BlockSpec((s, V_HEAD), lambda h: (0, h)),
        out_shape=jax.ShapeDtypeStruct((s, nh * V_HEAD), F32),
        compiler_params=_params(("arbitrary",)),
    )(q, k, v)


def attn_bwd(q, k, v, do):
    nh, s, qk = q.shape
    tq = min(s, ATT_TILE)

    def body(q_ref, k_ref, v_ref, do_ref, dq_ref, dk_ref, dv_ref):
        dk_ref[...] = jnp.zeros_like(dk_ref)
        dv_ref[...] = jnp.zeros_like(dv_ref)
        for qi in range(s // tq):
            n = (qi + 1) * tq
            rows = slice(qi * tq, n)
            p = _attn_probs(q_ref, k_ref, qi, tq)
            dob = do_ref[rows, :].astype(BF16)
            dp = _dot_nt(dob, v_ref[0:n, :])
            ds = (p * (dp - jnp.sum(p * dp, axis=-1, keepdims=True)) * SOFTMAX_SCALE).astype(BF16)
            dq_ref[rows, :] = _dot(ds, k_ref[0:n, :])
            dk_ref[0:n, :] += _dot_tn(ds, q_ref[rows, :])
            dv_ref[0:n, :] += _dot_tn(p.astype(BF16), dob)

    head = lambda w: pl.BlockSpec((None, s, w), lambda h: (h, 0, 0))
    return pl.pallas_call(
        body, name="attn_bwd",
        grid=(nh,),
        in_specs=[head(qk), head(qk), head(V_HEAD), pl.BlockSpec((s, V_HEAD), lambda h: (0, h))],
        out_specs=[head(qk), head(qk), head(V_HEAD)],
        out_shape=[jax.ShapeDtypeStruct((nh, s, qk), F32), jax.ShapeDtypeStruct((nh, s, qk), F32),
                   jax.ShapeDtypeStruct((nh, s, V_HEAD), F32)],
        compiler_params=_params(("arbitrary",)),
    )(q, k, v, do)


def mla_qkv_bwd(dq, dk, dv, cq, ckv, qan, kvan, wq, wkv, cos, sin, rot_t):
    s = cq.shape[0]
    tm = _row_tile(s)

    def body(dq_ref, dk_ref, dv_ref, cq_ref, ckv_ref, qan_ref, kvan_ref,
             wq_ref, wkv_ref, cos_ref, sin_ref, rot_ref,
             dcq_ref, dckv_ref, dkro_ref, gq_ref, gkv_ref, dqan_ref, dkvan_ref):
        i = pl.program_id(0)

        @pl.when(i == 0)
        def _():
            dqan_ref[...] = jnp.zeros_like(dqan_ref)
            dkvan_ref[...] = jnp.zeros_like(dkvan_ref)

        cos_t = cos_ref[...]
        sin_t = sin_ref[...]
        perm_t = rot_ref[...]

        def unrope(t):
            return t * cos_t + _dot_exact(t * sin_t, perm_t)

        acc_q = jnp.zeros((tm, Q_LORA), F32)
        acc_kv = jnp.zeros((tm, KV_LORA), F32)
        dkr_sum = jnp.zeros((tm, QK_ROPE), F32)
        for h in range(N_HEADS):
            dq_h = dq_ref[h]
            a = dq_h[:, 0:QK_NOPE].astype(BF16)
            b = unrope(dq_h[:, QK_NOPE:]).astype(BF16)
            gq_ref[h, :, 0:QK_NOPE] = a
            gq_ref[h, :, QK_NOPE:] = b
            wq_h = wq_ref[h]
            acc_q += _dot(a, wq_h[0:QK_NOPE, :]) + _dot(b, wq_h[QK_NOPE:, :])
            dk_h = dk_ref[h]
            dk = dk_h[:, 0:QK_NOPE].astype(BF16)
            dvv = dv_ref[h].astype(BF16)
            gkv_ref[h, :, 0:QK_NOPE] = dk
            gkv_ref[h, :, QK_NOPE:] = dvv
            wkv_h = wkv_ref[h]
            acc_kv += _dot_nt(dk, wkv_h[:, 0:QK_NOPE]) + _dot_nt(dvv, wkv_h[:, QK_NOPE:])
            dkr_sum += dk_h[:, QK_NOPE:]
        dkro_ref[...] = unrope(dkr_sum)
        dcq, dqan = _rms_bwd(acc_q, cq_ref[...], qan_ref[...])
        dcq_ref[...] = dcq
        dqan_ref[...] += dqan
        dckv, dkvan = _rms_bwd(acc_kv, ckv_ref[...], kvan_ref[...])
        dckv_ref[...] = dckv
        dkvan_ref[...] += dkvan

    row = lambda w: pl.BlockSpec((tm, w), lambda i: (i, 0))
    hrow = lambda w: pl.BlockSpec((N_HEADS, tm, w), lambda i: (0, i, 0))
    return pl.pallas_call(
        body, name="mla_qkv_bwd",
        grid=(s // tm,),
        in_specs=[hrow(QK_NOPE + QK_ROPE), hrow(QK_NOPE + QK_ROPE), hrow(V_HEAD),
                  row(Q_LORA), row(KV_LORA), _full(qan.shape), _full(kvan.shape),
                  _full(wq.shape), _full(wkv.shape), row(QK_ROPE), row(QK_ROPE), _full(rot_t.shape)],
        out_specs=[row(Q_LORA), row(KV_LORA), row(QK_ROPE), hrow(QK_NOPE + QK_ROPE), hrow(QK_NOPE + V_HEAD),
                   _full(qan.shape), _full(kvan.shape)],
        out_shape=[jax.ShapeDtypeStruct((s, Q_LORA), F32), jax.ShapeDtypeStruct((s, KV_LORA), F32),
                   jax.ShapeDtypeStruct((s, QK_ROPE), F32),
                   jax.ShapeDtypeStruct((N_HEADS, s, QK_NOPE + QK_ROPE), BF16),
                   jax.ShapeDtypeStruct((N_HEADS, s, QK_NOPE + V_HEAD), BF16),
                   jax.ShapeDtypeStruct(qan.shape, F32), jax.ShapeDtypeStruct(kvan.shape, F32)],
        compiler_params=_params(("arbitrary",)),
    )(dq, dk, dv, cq, ckv, qan, kvan, wq, wkv, cos, sin, rot_t)


def out_proj_fwd(yp, om, w_out, x, gt):
    s, d = x.shape
    n_sh, rs, _ = w_out.shape
    tm = _row_tile(s)
    per = POOL_WIDTH // rs

    def body(yp_ref, om_ref, w_ref, x_ref, gt_ref, xo_ref, ycat_ref, y_ref):
        y = jnp.zeros((tm, d), F32)
        for j in range(n_sh):
            src = yp_ref if j < per else om_ref
            part = src[:, (j % per) * rs:(j % per + 1) * rs].astype(BF16)
            ycat_ref[j] = part.T
            y += _dot(part, w_ref[j])
        y_ref[...] = y.astype(BF16)
        xo_ref[...] = x_ref[...] + gt_ref[...] * y

    row = lambda w: pl.BlockSpec((tm, w), lambda i: (i, 0))
    return pl.pallas_call(
        body, name="out_proj_fwd",
        grid=(s // tm,),
        in_specs=[row(POOL_WIDTH), row(POOL_WIDTH), _full(w_out.shape), row(d), pl.BlockSpec((1, d), lambda i: (0, 0))],
        out_specs=[row(d), pl.BlockSpec((n_sh, rs, tm), lambda i: (0, 0, i)), row(d)],
        out_shape=[jax.ShapeDtypeStruct((s, d), F32), jax.ShapeDtypeStruct((n_sh, rs, s), BF16),
                   jax.ShapeDtypeStruct((s, d), BF16)],
        compiler_params=_params(("arbitrary",)),
    )(yp, om, w_out, x, gt)


def out_proj_bwd(dxn, y, gt, w_out):
    s, d = dxn.shape
    n_sh, rs, _ = w_out.shape
    tm = _row_tile(s)
    per = POOL_WIDTH // rs

    def body(dxn_ref, y_ref, gt_ref, w_ref, dy_ref, dyp_ref, dom_ref, dgt_ref):
        i = pl.program_id(0)

        @pl.when(i == 0)
        def _():
            dgt_ref[...] = jnp.zeros_like(dgt_ref)

        dxn_t = dxn_ref[...]
        dy = (gt_ref[...] * dxn_t).astype(BF16)
        dy_ref[...] = dy
        dgt_ref[...] += _sum0(dxn_t * y_ref[...].astype(F32))
        for j in range(n_sh):
            dst = dyp_ref if j < per else dom_ref
            dst[:, (j % per) * rs:(j % per + 1) * rs] = _dot_nt(dy, w_ref[j])

    row = lambda w: pl.BlockSpec((tm, w), lambda i: (i, 0))
    vec = pl.BlockSpec((1, d), lambda i: (0, 0))
    return pl.pallas_call(
        body, name="out_proj_bwd",
        grid=(s // tm,),
        in_specs=[row(d), row(d), vec, _full(w_out.shape)],
        out_specs=[row(d), row(POOL_WIDTH), row(POOL_WIDTH), vec],
        out_shape=[jax.ShapeDtypeStruct((s, d), BF16), jax.ShapeDtypeStruct((s, POOL_WIDTH), F32),
                   jax.ShapeDtypeStruct((s, POOL_WIDTH), F32), jax.ShapeDtypeStruct((1, d), F32)],
        compiler_params=_params(("arbitrary",)),
    )(dxn, y, gt, w_out)


def final_loss(x, gn, tgt):
    s, d = x.shape
    tm = _row_tile(s)

    def body(x_ref, gn_ref, t_ref, loss_ref, dx_ref, dgn_ref):
        i = pl.program_id(0)

        @pl.when(i == 0)
        def _():
            loss_ref[...] = jnp.zeros_like(loss_ref)
            dgn_ref[...] = jnp.zeros_like(dgn_ref)

        xt = x_ref[...]
        g = gn_ref[...]
        xhat, _ = _rms(xt)
        err = xhat * g - t_ref[...]
        per_tok = jnp.mean(err * err, axis=-1, keepdims=True)
        loss_ref[...] += jnp.broadcast_to(0.5 * _sum0(per_tok), loss_ref.shape)
        dx, dgn = _rms_bwd(err * (1.0 / d), xt, g)
        dx_ref[...] = dx
        dgn_ref[...] += dgn

    row = pl.BlockSpec((tm, d), lambda i: (i, 0))
    vec = pl.BlockSpec((1, d), lambda i: (0, 0))
    return pl.pallas_call(
        body, name="final_loss",
        grid=(s // tm,),
        in_specs=[row, vec, row],
        out_specs=[pl.BlockSpec((1, LANES), lambda i: (0, 0)), row, vec],
        out_shape=[jax.ShapeDtypeStruct((1, LANES), F32), jax.ShapeDtypeStruct((s, d), F32),
                   jax.ShapeDtypeStruct((1, d), F32)],
        compiler_params=_params(("arbitrary",)),
    )(x, gn, tgt)


def _col_tile(cols):
    return 768 if cols % 768 == 0 else cols


def ada_fwd(c16, ada_w, ada_b_loc):
    n_layers, d, cols = ada_w.shape
    tn = _col_tile(cols)

    def body(c_ref, w_ref, b_ref, o_ref):
        cv = c_ref[...]
        ca = (cv * jax.nn.sigmoid(cv)).astype(BF16)
        o_ref[...] = _dot(ca, w_ref[...].astype(BF16)) + b_ref[...]

    return pl.pallas_call(
        body, name="ada_fwd",
        grid=(n_layers, cols // tn),
        in_specs=[pl.BlockSpec((16, d), lambda l, j: (0, 0)), pl.BlockSpec((None, d, tn), lambda l, j: (l, 0, j)),
                  pl.BlockSpec((None, 1, tn), lambda l, j: (l, 0, j))],
        out_specs=pl.BlockSpec((None, 16, tn), lambda l, j: (l, 0, j)),
        out_shape=jax.ShapeDtypeStruct((n_layers, 16, cols), F32),
        compiler_params=_params(("arbitrary", "arbitrary")),
    )(c16, ada_w, ada_b_loc)


def ada_bwd(c16, dmod16):
    n_layers, _, cols = dmod16.shape
    d = c16.shape[1]
    tn = _col_tile(cols)

    def body(c_ref, g_ref, o_ref):
        cv = c_ref[...]
        ca = (cv * jax.nn.sigmoid(cv)).astype(BF16)
        o_ref[...] = _dot_tn(ca, g_ref[...].astype(BF16))

    return pl.pallas_call(
        body, name="ada_bwd",
        grid=(n_layers, cols // tn),
        in_specs=[pl.BlockSpec((16, d), lambda l, j: (0, 0)), pl.BlockSpec((None, 16, tn), lambda l, j: (l, 0, j))],
        out_specs=pl.BlockSpec((None, d, tn), lambda l, j: (l, 0, j)),
        out_shape=jax.ShapeDtypeStruct((n_layers, d, cols), F32),
        compiler_params=_params(("arbitrary", "arbitrary")),
    )(c16, dmod16)


def _as_rows(a):
    if a.ndim == 1:
        return a.reshape(1, a.shape[0])
    return a.reshape(-1, a.shape[-1])


def _rows_tile(r, c, itemsize=4, budget=2 * 1024 * 1024):
    if r * c * itemsize <= budget:
        return r
    best = None
    t = BF16_ROWS
    while t < r:
        if r % t == 0 and t * c * itemsize <= budget:
            best = t
        t += BF16_ROWS
    return best if best is not None else r


CAST_VMEM = 16 * 1024 * 1024


def cast_place(ws, chip, layers, after):
    _, r, c = ws[0].shape
    n_sel = len(layers)
    n_blk = len(ws) * n_sel
    tr = _rows_tile(r, c, budget=CAST_VMEM // (3 * n_blk))

    def body(chip_ref, *refs):
        for j in range(n_blk):
            refs[n_blk + 1 + j][...] = refs[j][...].astype(BF16)

    layer_spec = lambda l: pl.BlockSpec((None, tr, c), lambda i, ch: (l, i, 0))
    outs = pl.pallas_call(
        body, name="cast_place",
        grid_spec=pltpu.PrefetchScalarGridSpec(
            num_scalar_prefetch=1, grid=(r // tr,),
            in_specs=[layer_spec(l) for _ in ws for l in layers] + [pl.BlockSpec(memory_space=pl.ANY)],
            out_specs=[pl.BlockSpec((None, tr, c), lambda i, ch: (ch[0], i, 0))] * n_blk),
        out_shape=[jax.ShapeDtypeStruct((N_CHIPS, r, c), BF16)] * n_blk,
        compiler_params=_params(("arbitrary",)),
    )(chip, *[w for w in ws for _ in layers], after)
    return [list(outs[i * n_sel:(i + 1) * n_sel]) for i in range(len(ws))]


def adamw(w, g, m, v, copy_g=False):
    shape = w.shape
    w2, g2, m2, v2 = (_as_rows(t) for t in (w, g, m, v))
    r, c = w2.shape
    tr = _rows_tile(r, c, budget=3 * 1024 * 1024)
    c1 = 1.0 - ADAM_B1 ** ADAM_STEP
    c2 = 1.0 - ADAM_B2 ** ADAM_STEP

    def body(w_ref, g_ref, m_ref, v_ref, d_ref, mo_ref, vo_ref, *go_ref):
        gv = g_ref[...]
        if copy_g:
            go_ref[0][...] = gv
        mn = ADAM_B1 * m_ref[...] + (1.0 - ADAM_B1) * gv
        vn = ADAM_B2 * v_ref[...] + (1.0 - ADAM_B2) * (gv * gv)
        mo_ref[...] = mn
        vo_ref[...] = vn
        d_ref[...] = -ADAM_LR * ((mn / c1) / (jnp.sqrt(vn / c2) + ADAM_EPS) + ADAM_WD * w_ref[...])

    spec = pl.BlockSpec((tr, c), lambda i: (i, 0))
    n_out = 4 if copy_g else 3
    outs = pl.pallas_call(
        body, name="adamw", grid=(r // tr,), in_specs=[spec] * 4, out_specs=[spec] * n_out,
        out_shape=[jax.ShapeDtypeStruct((r, c), F32)] * n_out, compiler_params=_params(("arbitrary",)),
    )(w2, g2, m2, v2)
    g_out = outs[3] if copy_g else g2
    return tuple(o.reshape(shape) for o in (g_out,) + tuple(outs[:3]))


def sum_devices(a, after):
    n, r, c = a.shape
    tr = _rows_tile(r, c, budget=512 * 1024)

    def body(a_ref, after_ref, o_ref):
        acc = a_ref[0]
        for j in range(1, n):
            acc = acc + a_ref[j]
        o_ref[...] = acc

    return pl.pallas_call(
        body, name="sum_devices", grid=(r // tr,),
        in_specs=[pl.BlockSpec((n, tr, c), lambda i: (0, i, 0)), pl.BlockSpec(memory_space=pl.ANY)],
        out_specs=pl.BlockSpec((tr, c), lambda i: (i, 0)),
        out_shape=jax.ShapeDtypeStruct((r, c), F32), compiler_params=_params(("arbitrary",)),
    )(a, after)


def _split_axis(r, c):
    if (r // 2) % BF16_ROWS == 0 and r % 2 == 0:
        return 0
    assert c % (2 * LANES) == 0, (r, c)
    return 1


def _half_shape(r, c):
    return (r // 2, c) if _split_axis(r, c) == 0 else (r, c // 2)


def _half_at(ref, lead, which):
    r, c = ref.shape[-2:]
    if _split_axis(r, c) == 0:
        return ref.at[(*lead, pl.ds(which * (r // 2), r // 2), slice(None))]
    return ref.at[(*lead, slice(None), pl.ds(which * (c // 2), c // 2))]


def _half_spec(r, c, lead_block, imap):
    hr, hc = _half_shape(r, c)
    if _split_axis(r, c) == 0:
        return pl.BlockSpec((*lead_block, hr, hc), lambda *a: (*imap(*a)[0], imap(*a)[1], 0))
    return pl.BlockSpec((*lead_block, hr, hc), lambda *a: (*imap(*a)[0], 0, imap(*a)[1]))


def pair_add(owns, alls, ra_owns, ra_alls, sel):
    n = len(owns)
    n_sl = alls[0].shape[0]
    halves = [_half_shape(*g.shape) for g in owns]

    def body(s_ref, *refs):
        own_refs, all_refs, ra_own_refs, ra_all_refs, pb_refs, sum_refs = (refs[i * n:(i + 1) * n] for i in range(6))
        k = pl.program_id(0)
        for t in range(n):
            pb_refs[t][...] = (all_refs[t][...].astype(F32) + ra_all_refs[t][...].astype(F32)).astype(BF16)

            @pl.when(k == s_ref[1])
            def _(t=t):
                sum_refs[t][...] = own_refs[t][...] + ra_own_refs[t][...]

    slot = lambda hs: pl.BlockSpec((None,) + hs, lambda k, sr: (k, 0, 0))
    whole = lambda hs: pl.BlockSpec(hs, lambda k, sr: (0, 0))
    outs = pl.pallas_call(
        body, name="pair_add",
        grid_spec=pltpu.PrefetchScalarGridSpec(
            num_scalar_prefetch=1, grid=(n_sl,),
            in_specs=[_half_spec(*g.shape, (), lambda k, sr: ((), sr[0])) for g in owns]
            + [_half_spec(*g.shape[1:], (None,), lambda k, sr: ((k,), sr[0])) for g in alls]
            + [whole(hs) for hs in halves] + [slot(hs) for hs in halves],
            out_specs=[slot(hs) for hs in halves] + [whole(hs) for hs in halves]),
        out_shape=[jax.ShapeDtypeStruct((n_sl,) + hs, BF16) for hs in halves]
        + [jax.ShapeDtypeStruct(hs, F32) for hs in halves],
        compiler_params=_params(("arbitrary",)),
    )(sel, *owns, *alls, *ra_owns, *ra_alls)
    return list(outs[:n]), list(outs[n:])


def chip_sum(owns, rbs, sel, shapes, accs):
    n = len(owns)
    fresh = accs[0] is None

    def body(s_ref, *refs):
        own_refs, rb_refs, o_refs = refs[:n], refs[n:2 * n], refs[-n:]
        for t in range(n):
            acc_v = own_refs[t][...]
            for j in range(N_CHIPS - 1):
                acc_v = acc_v + rb_refs[t][j].astype(F32)
            o_refs[t][...] = acc_v

    in_specs = ([pl.BlockSpec(o.shape, lambda i, sr: (0, 0)) for o in owns]
                + [pl.BlockSpec(rb.shape, lambda i, sr: (0, 0, 0)) for rb in rbs])
    args = [sel, *owns, *rbs]
    aliases = {}
    if not fresh:
        in_specs += [pl.BlockSpec(memory_space=pl.ANY)] * n
        args += list(accs)
        aliases = {1 + 2 * n + t: t for t in range(n)}
    return list(pl.pallas_call(
        body, name="chip_sum",
        grid_spec=pltpu.PrefetchScalarGridSpec(
            num_scalar_prefetch=1, grid=(1,), in_specs=in_specs,
            out_specs=[_half_spec(*shp[1:], (None,), lambda i, sr: ((sr[2],), sr[0])) for shp in shapes]),
        out_shape=[jax.ShapeDtypeStruct(shp, F32) for shp in shapes],
        input_output_aliases=aliases,
        compiler_params=_params(("arbitrary",)),
    )(*args))


def _me():
    return lax.axis_index("x"), lax.axis_index("y"), lax.axis_index("c")


def _flip(v, bit):
    return 1 - v if bit else v


def exchange8(xs, bcast):
    blk = xs.shape if bcast else xs.shape[1:]

    def body(x_ref, o_ref, send_sems, recv_sems, loc_sem):
        mx, my, mc = _me()
        me = 4 * mx + 2 * my + mc
        src = (lambda j: x_ref) if bcast else (lambda j: x_ref.at[j])
        loc = pltpu.make_async_copy(src(me), o_ref.at[me], loc_sem)
        loc.start()
        copies = []
        for o in range(1, N_DEV):
            px, py, pc = _flip(mx, o & 4), _flip(my, o & 2), _flip(mc, o & 1)
            cp = pltpu.make_async_remote_copy(
                src_ref=src(4 * px + 2 * py + pc), dst_ref=o_ref.at[me],
                send_sem=send_sems.at[o - 1], recv_sem=recv_sems.at[o - 1],
                device_id=(px, py, pc), device_id_type=MESH)
            cp.start()
            copies.append(cp)
        for cp in copies:
            cp.wait()
        loc.wait()

    return pl.pallas_call(
        body, name="exchange8_gather" if bcast else "exchange8_a2a",
        in_specs=[pl.BlockSpec(memory_space=pltpu.VMEM)], out_specs=pl.BlockSpec(memory_space=pltpu.VMEM),
        out_shape=jax.ShapeDtypeStruct((N_DEV,) + tuple(blk), xs.dtype),
        scratch_shapes=[pltpu.SemaphoreType.DMA((N_DEV - 1,)), pltpu.SemaphoreType.DMA((N_DEV - 1,)), pltpu.SemaphoreType.DMA],
        compiler_params=_params(),
    )(xs)


HBM = pl.BlockSpec(memory_space=pltpu.HBM)
SEM = pl.BlockSpec(memory_space=pltpu.SEMAPHORE)
EFFECT = pltpu.SideEffectType.DATAFLOW_SIDE_EFFECTING


def _hbm(a):
    return pltpu.with_memory_space_constraint(a, pltpu.HBM)


def _ici_copy(land, o, send_sem, recv_sem, sending):
    mx, my, mc = _me()
    px, py = _flip(mx, o & 2), _flip(my, o & 1)
    mine = _half_at(land, (2 * mx + my,), mc)
    return pltpu.make_async_remote_copy(
        src_ref=mine, dst_ref=mine if sending else _half_at(land, (2 * px + py,), mc),
        send_sem=send_sem, recv_sem=recv_sem, device_id=(px, py, mc), device_id_type=MESH)


N_PEERS = N_CHIPS - 1
DMA_SEM = pltpu.SemaphoreType.DMA(())


def gather_start(lands, groups, after, tag):
    n_layers, n = len(lands), len(lands[0])
    flat = [a for layer in lands for a in layer]
    n_in = n * n_layers
    n_grp = len(groups)
    n_sem = 2 * n_layers * n_grp * N_PEERS
    first = lambda l, g, recv: ((l * n_grp + g) * 2 + recv) * N_PEERS

    def body(*refs):
        land = refs[:n_in]
        sems = refs[n_in + 1:n_in + 1 + n_sem]
        token = refs[-1]
        for l in range(n_layers):
            for g, members in enumerate(groups):
                for t in members:
                    for o in range(1, N_CHIPS):
                        _ici_copy(land[l * n + t], o, sems[first(l, g, 0) + o - 1], sems[first(l, g, 1) + o - 1],
                                  True).start()
        token[...] = jnp.zeros_like(token)

    outs = pl.pallas_call(
        body, name=f"gather_start_{tag}",
        in_specs=[HBM] * n_in + [pl.BlockSpec(memory_space=pl.ANY)],
        out_specs=[SEM] * n_sem + [HBM] * n_in + [pl.BlockSpec(memory_space=pltpu.VMEM)],
        out_shape=[DMA_SEM] * n_sem + [pltpu.HBM(a.shape, a.dtype) for a in flat]
        + [jax.ShapeDtypeStruct((8, LANES), F32)],
        input_output_aliases={i: i + n_sem for i in range(n_in)},
        compiler_params=pltpu.CompilerParams(has_side_effects=EFFECT),
    )(*[_hbm(a) for a in flat], after)
    sems = [[(list(outs[first(l, g, 0):first(l, g, 0) + N_PEERS]), list(outs[first(l, g, 1):first(l, g, 1) + N_PEERS]))
             for g in range(n_grp)] for l in range(n_layers)]
    lands_thru = [list(outs[n_sem + l * n:n_sem + (l + 1) * n]) for l in range(n_layers)]
    return sems, lands_thru, outs[-1]


def gather_wait(tag, sems, lands, after):
    n = len(lands)
    send_sems, recv_sems = sems

    def body(*refs):
        land = refs[:n]
        send_r = refs[n:n + N_PEERS]
        recv_r = refs[n + N_PEERS:n + 2 * N_PEERS]
        for t in range(n):
            for o in range(1, N_CHIPS):
                _ici_copy(land[t], o, send_r[o - 1], recv_r[o - 1], True).wait_send()
                _ici_copy(land[t], o, send_r[o - 1], recv_r[o - 1], False).wait_recv()

    return list(pl.pallas_call(
        body, name=f"gather_wait_{tag}",
        in_specs=[HBM] * n + [SEM] * (2 * N_PEERS) + [pl.BlockSpec(memory_space=pl.ANY)],
        out_specs=[HBM] * n,
        out_shape=[pltpu.HBM(a.shape, a.dtype) for a in lands],
        input_output_aliases={i: i for i in range(n)},
        compiler_params=pltpu.CompilerParams(has_side_effects=EFFECT),
    )(*lands, *send_sems, *recv_sems, after))


def gather_forward(lands):
    n = len(lands)

    def body(*refs):
        dst = refs[n:2 * n]
        send_sems, recv_sems = refs[2 * n:]
        mx, my, mc = _me()
        fwds = []
        for t in range(n):
            for o in range(1, N_CHIPS):
                slot = 2 * _flip(mx, o & 2) + _flip(my, o & 1)
                mine = _half_at(dst[t], (slot,), mc)
                theirs = _half_at(dst[t], (slot,), 1 - mc)
                cp = pltpu.make_async_remote_copy(
                    src_ref=mine, dst_ref=mine, send_sem=send_sems.at[t, o - 1], recv_sem=recv_sems.at[t, o - 1],
                    device_id=(mx, my, 1 - mc), device_id_type=MESH)
                cp.start()
                fwds.append((cp, pltpu.make_async_remote_copy(
                    src_ref=theirs, dst_ref=theirs, send_sem=send_sems.at[t, o - 1], recv_sem=recv_sems.at[t, o - 1],
                    device_id=(mx, my, 1 - mc), device_id_type=MESH)))
        for cp, arrival in fwds:
            cp.wait_send()
            arrival.wait_recv()

    any_spec = pl.BlockSpec(memory_space=pl.ANY)
    return list(pl.pallas_call(
        body, name="gather_forward",
        in_specs=[any_spec] * n, out_specs=[any_spec] * n,
        out_shape=[jax.ShapeDtypeStruct(a.shape, a.dtype) for a in lands],
        input_output_aliases={t: t for t in range(n)},
        scratch_shapes=[pltpu.SemaphoreType.DMA((n, N_CHIPS - 1)), pltpu.SemaphoreType.DMA((n, N_CHIPS - 1))],
        compiler_params=_params(),
    )(*lands))


def _scatter_copy(src, land, o, send_sem, recv_sem):
    mx, my, mc = _me()
    px, py = _flip(mx, o & 2), _flip(my, o & 1)
    return pltpu.make_async_remote_copy(
        src_ref=src.at[2 * px + py], dst_ref=land.at[o - 1],
        send_sem=send_sem, recv_sem=recv_sem, device_id=(px, py, mc), device_id_type=MESH)


def scatter_start(pbs, tag, after):
    n = len(pbs)
    lands = [lax.empty((N_CHIPS - 1,) + p.shape[1:], p.dtype) for p in pbs]

    def body(*refs):
        src = refs[:n]
        land = refs[n:2 * n]
        send_sems = refs[2 * n + 1:2 * n + 1 + N_PEERS]
        recv_sems = refs[2 * n + 1 + N_PEERS:2 * n + 1 + 2 * N_PEERS]
        token = refs[-1]
        for t in range(n):
            for o in range(1, N_CHIPS):
                _scatter_copy(src[t], land[t], o, send_sems[o - 1], recv_sems[o - 1]).start()
        token[...] = jnp.zeros_like(token)

    n_sem = 2 * N_PEERS
    arrs = list(pbs) + lands
    outs = pl.pallas_call(
        body, name=f"scatter_start_{tag}",
        in_specs=[HBM] * (2 * n) + [pl.BlockSpec(memory_space=pl.ANY)],
        out_specs=[SEM] * n_sem + [HBM] * (2 * n) + [pl.BlockSpec(memory_space=pltpu.VMEM)],
        out_shape=[DMA_SEM] * n_sem + [pltpu.HBM(a.shape, a.dtype) for a in arrs]
        + [jax.ShapeDtypeStruct((8, LANES), F32)],
        input_output_aliases={i: i + n_sem for i in range(2 * n)},
        compiler_params=pltpu.CompilerParams(has_side_effects=EFFECT),
    )(*[_hbm(a) for a in arrs], after)
    return (list(outs[:N_PEERS]), list(outs[N_PEERS:n_sem]), list(outs[n_sem:n_sem + n]),
            list(outs[n_sem + n:n_sem + 2 * n]), outs[-1])


def scatter_wait(tag, send_sems, recv_sems, pbs, lands, after):
    n = len(pbs)

    def body(*refs):
        src = refs[:n]
        land = refs[n:2 * n]
        send_r = refs[2 * n:2 * n + N_PEERS]
        recv_r = refs[2 * n + N_PEERS:2 * n + 2 * N_PEERS]
        for t in range(n):
            for o in range(1, N_CHIPS):
                cp = _scatter_copy(src[t], land[t], o, send_r[o - 1], recv_r[o - 1])
                cp.wait_send()
                cp.wait_recv()

    arrs = list(pbs) + list(lands)
    outs = pl.pallas_call(
        body, name=f"scatter_wait_{tag}",
        in_specs=[HBM] * (2 * n) + [SEM] * (2 * N_PEERS) + [pl.BlockSpec(memory_space=pl.ANY)],
        out_specs=[HBM] * (2 * n),
        out_shape=[pltpu.HBM(a.shape, a.dtype) for a in arrs],
        input_output_aliases={i: i for i in range(2 * n)},
        compiler_params=pltpu.CompilerParams(has_side_effects=EFFECT),
    )(*arrs, *send_sems, *recv_sems, after)
    return list(outs[n:])


def _pair_copies(srcs, lands, send_sem, recv_sem):
    mx, my, mc = _me()
    return [pltpu.make_async_remote_copy(
        src_ref=_half_at(src, (slice(None),) * (len(src.shape) - 2), 1 - mc), dst_ref=land,
        send_sem=send_sem, recv_sem=recv_sem, device_id=(mx, my, 1 - mc), device_id_type=MESH)
        for src, land in zip(srcs, lands)]


def pair_start(gs, tag, after):
    n = len(gs)
    lands = [lax.empty(g.shape[:-2] + _half_shape(*g.shape[-2:]), g.dtype) for g in gs]

    def body(*refs):
        send_sem, recv_sem = refs[2 * n + 1], refs[2 * n + 2]
        token = refs[-1]
        for cp in _pair_copies(refs[:n], refs[n:2 * n], send_sem, recv_sem):
            cp.start()
        token[...] = jnp.zeros_like(token)

    arrs = list(gs) + lands
    outs = pl.pallas_call(
        body, name=f"pair_start_{tag}",
        in_specs=[HBM] * (2 * n) + [pl.BlockSpec(memory_space=pl.ANY)],
        out_specs=[SEM, SEM] + [HBM] * (2 * n) + [pl.BlockSpec(memory_space=pltpu.VMEM)],
        out_shape=[DMA_SEM, DMA_SEM] + [pltpu.HBM(a.shape, a.dtype) for a in arrs] + [jax.ShapeDtypeStruct((8, LANES), F32)],
        input_output_aliases={i: i + 2 for i in range(2 * n)},
        compiler_params=pltpu.CompilerParams(has_side_effects=EFFECT),
    )(*[_hbm(a) for a in arrs], after)
    return outs[0], outs[1], list(outs[2:2 + n]), list(outs[2 + n:2 + 2 * n]), outs[-1]


def pair_wait(tag, send_sem, recv_sem, gs, lands, after):
    n = len(gs)

    def body(*refs):
        for cp in _pair_copies(refs[:n], refs[n:2 * n], refs[2 * n], refs[2 * n + 1]):
            cp.wait_send()
            cp.wait_recv()

    arrs = list(gs) + list(lands)
    outs = pl.pallas_call(
        body, name=f"pair_wait_{tag}",
        in_specs=[HBM] * (2 * n) + [SEM, SEM, pl.BlockSpec(memory_space=pl.ANY)],
        out_specs=[HBM] * (2 * n),
        out_shape=[pltpu.HBM(a.shape, a.dtype) for a in arrs],
        input_output_aliases={i: i for i in range(2 * n)},
        compiler_params=pltpu.CompilerParams(has_side_effects=EFFECT),
    )(*arrs, send_sem, recv_sem, after)
    return list(outs[:n]), list(outs[n:])


def _gather8_copy(x, land, o, send_sem, recv_sem, sending):
    mx, my, mc = _me()
    px, py, pc = _flip(mx, o & 4), _flip(my, o & 2), _flip(mc, o & 1)
    slot = 4 * mx + 2 * my + mc if sending else 4 * px + 2 * py + pc
    return pltpu.make_async_remote_copy(
        src_ref=x, dst_ref=land.at[slot], send_sem=send_sem, recv_sem=recv_sem,
        device_id=(px, py, pc), device_id_type=MESH)


def gather8_start(x, land, after, tag):
    n_peer = N_DEV - 1

    def body(x_ref, land_ref, after_ref, *rest):
        send_sems, recv_sems = rest[:n_peer], rest[n_peer:2 * n_peer]
        token = rest[-1]
        for o in range(1, N_DEV):
            _gather8_copy(x_ref, land_ref, o, send_sems[o - 1], recv_sems[o - 1], True).start()
        token[...] = jnp.zeros_like(token)

    outs = pl.pallas_call(
        body, name=f"gather8_start_{tag}",
        in_specs=[HBM, HBM, pl.BlockSpec(memory_space=pl.ANY)],
        out_specs=[SEM] * (2 * n_peer) + [HBM, HBM, pl.BlockSpec(memory_space=pltpu.VMEM)],
        out_shape=[DMA_SEM] * (2 * n_peer) + [pltpu.HBM(x.shape, x.dtype), pltpu.HBM(land.shape, land.dtype),
                                              jax.ShapeDtypeStruct((8, LANES), F32)],
        input_output_aliases={0: 2 * n_peer, 1: 2 * n_peer + 1},
        compiler_params=pltpu.CompilerParams(has_side_effects=EFFECT),
    )(_hbm(x), _hbm(land), after)
    return list(outs[:n_peer]), list(outs[n_peer:2 * n_peer]), outs[2 * n_peer], outs[2 * n_peer + 1], outs[-1]


def gather8_wait(tag, send_sems, recv_sems, x, land, after):
    n_peer = N_DEV - 1

    def body(x_ref, land_ref, *rest):
        send_r, recv_r = rest[:n_peer], rest[n_peer:2 * n_peer]
        for o in range(1, N_DEV):
            _gather8_copy(x_ref, land_ref, o, send_r[o - 1], recv_r[o - 1], True).wait_send()
            _gather8_copy(x_ref, land_ref, o, send_r[o - 1], recv_r[o - 1], False).wait_recv()

    return pl.pallas_call(
        body, name=f"gather8_wait_{tag}",
        in_specs=[HBM, HBM] + [SEM] * (2 * n_peer) + [pl.BlockSpec(memory_space=pl.ANY)],
        out_specs=[HBM, HBM],
        out_shape=[pltpu.HBM(x.shape, x.dtype), pltpu.HBM(land.shape, land.dtype)],
        input_output_aliases={0: 0, 1: 1},
        compiler_params=pltpu.CompilerParams(has_side_effects=EFFECT),
    )(x, land, *send_sems, *recv_sems, after)[1]


def _fill_copies(fs, send_sem, recv_sem, sending):
    mx, my, mc = _me()
    out = []
    for f in fs:
        region = _half_at(f, (slice(None),), mc if sending else 1 - mc)
        out.append(pltpu.make_async_remote_copy(
            src_ref=region, dst_ref=region, send_sem=send_sem, recv_sem=recv_sem,
            device_id=(mx, my, 1 - mc), device_id_type=MESH))
    return out


def fill_start(fs, tag, after):
    n = len(fs)

    def body(*refs):
        send_sem, recv_sem = refs[n + 1], refs[n + 2]
        token = refs[-1]
        for cp in _fill_copies(refs[:n], send_sem, recv_sem, True):
            cp.start()
        token[...] = jnp.zeros_like(token)

    outs = pl.pallas_call(
        body, name=f"fill_start_{tag}",
        in_specs=[HBM] * n + [pl.BlockSpec(memory_space=pl.ANY)],
        out_specs=[SEM, SEM] + [HBM] * n + [pl.BlockSpec(memory_space=pltpu.VMEM)],
        out_shape=[DMA_SEM, DMA_SEM] + [pltpu.HBM(f.shape, f.dtype) for f in fs] + [jax.ShapeDtypeStruct((8, LANES), F32)],
        input_output_aliases={i: i + 2 for i in range(n)},
        compiler_params=pltpu.CompilerParams(has_side_effects=EFFECT),
    )(*[_hbm(f) for f in fs], after)
    return outs[0], outs[1], list(outs[2:2 + n]), outs[-1]


def fill_wait(tag, send_sem, recv_sem, fs, after):
    n = len(fs)

    def body(*refs):
        for cp in _fill_copies(refs[:n], refs[n], refs[n + 1], True):
            cp.wait_send()
        for cp in _fill_copies(refs[:n], refs[n], refs[n + 1], False):
            cp.wait_recv()

    return list(pl.pallas_call(
        body, name=f"fill_wait_{tag}",
        in_specs=[HBM] * n + [SEM, SEM, pl.BlockSpec(memory_space=pl.ANY)],
        out_specs=[HBM] * n,
        out_shape=[pltpu.HBM(f.shape, f.dtype) for f in fs],
        input_output_aliases={i: i for i in range(n)},
        compiler_params=pltpu.CompilerParams(has_side_effects=EFFECT),
    )(*fs, send_sem, recv_sem, after))


def _pack_rows(parts, d):
    rows, spans = [], []
    at = 0
    for p in parts:
        flat = p.reshape(-1)
        n_rows = -(-flat.shape[0] // (8 * d)) * 8
        flat = jnp.pad(flat, (0, n_rows * d - flat.shape[0]))
        rows.append(flat.reshape(n_rows, d))
        spans.append((at, p.shape))
        at += n_rows
    return jnp.concatenate(rows, axis=0), spans


def _unpack_rows(packed, spans):
    lead, d = packed.shape[:-2], packed.shape[-1]
    out = []
    for at, shape in spans:
        n = math.prod(shape)
        n_rows = -(-n // d)
        out.append(packed[..., at:at + n_rows, :].reshape(lead + (-1,))[..., :n].reshape(lead + tuple(shape)))
    return out


def _rotate_half_matrix():
    half = QK_ROPE // 2
    idx = jnp.arange(QK_ROPE)
    src = jnp.where(idx < half, idx + half, idx - half)
    sign = jnp.where(idx < half, -1.0, 1.0)
    return (jnp.zeros((QK_ROPE, QK_ROPE), F32).at[src, idx].set(sign)).astype(BF16)


def kernel(x, c, positions, ada_w, ada_b, ffn1_norm, ffn1_w_gate, ffn1_w_up, ffn1_w_down, mix_norm, w_in, pool_w, pool_scale, q_a_norm, w_q_b, kv_a_norm, w_kv_b, w_out, ffn2_norm, ffn2_w_gate, ffn2_w_up, ffn2_w_down, final_norm, loss_target, m_ada_w, m_ada_b, m_ffn1_norm, m_ffn1_w_gate, m_ffn1_w_up, m_ffn1_w_down, m_mix_norm, m_w_in, m_pool_w, m_pool_scale, m_q_a_norm, m_w_q_b, m_kv_a_norm, m_w_kv_b, m_w_out, m_ffn2_norm, m_ffn2_w_gate, m_ffn2_w_up, m_ffn2_w_down, m_final_norm, v_ada_w, v_ada_b, v_ffn1_norm, v_ffn1_w_gate, v_ffn1_w_up, v_ffn1_w_down, v_mix_norm, v_w_in, v_pool_w, v_pool_scale, v_q_a_norm, v_w_q_b, v_kv_a_norm, v_w_kv_b, v_w_out, v_ffn2_norm, v_ffn2_w_gate, v_ffn2_w_up, v_ffn2_w_down, v_final_norm):
    mx, my, mc = _me()
    chip = 2 * mx + my
    half = jnp.reshape(mc, (1,)).astype(jnp.int32)
    chip1 = jnp.reshape(chip, (1,)).astype(jnp.int32)
    n_layers, d, ada_cols = ada_w.shape
    xt = x[0]
    tgt = loss_target[0]

    inv_freq = 1.0 / (ROPE_THETA ** (jnp.arange(0, QK_ROPE, 2, dtype=F32) / QK_ROPE))
    ang = positions[0].astype(F32)[:, None] * inv_freq
    ang = jnp.concatenate([ang, ang], axis=-1)
    cos, sin = jnp.cos(ang), jnp.sin(ang)
    rot = _rotate_half_matrix()
    rot_t = rot.T

    c_all = exchange8(c, True).reshape(N_DEV, d)
    c16 = jnp.pad(c_all, ((0, 8), (0, 0)))
    ada_b_loc = lax.dynamic_slice_in_dim(ada_b, chip * ada_cols, ada_cols, axis=1).reshape(n_layers, 1, ada_cols)
    mod_part = ada_fwd(c16, ada_w, ada_b_loc)[:, :N_DEV]
    mod_got = exchange8(jnp.transpose(mod_part, (1, 0, 2)), False)
    mod = jnp.transpose(mod_got.reshape(N_CHIPS, 2, n_layers, ada_cols)[:, 0], (1, 0, 2))
    mod = mod.reshape(n_layers, 9, 1, d)

    tr = lambda a: jnp.transpose(a, (0, 2, 1))
    local = [tr(ffn1_w_gate), tr(ffn1_w_up), ffn1_w_down, tr(w_in), tr(w_q_b), w_kv_b, w_out,
             tr(ffn2_w_gate), tr(ffn2_w_up), ffn2_w_down]
    ffn1_pos, mixer_pos, ffn2_pos = (0, 1, 2), (3, 4, 5, 6), (7, 8, 9)
    rest_pos = mixer_pos + ffn2_pos

    def cast_all(layers, after):
        by_shape = {}
        for t, w in enumerate(local):
            by_shape.setdefault(w.shape, []).append(t)
        out = [None] * len(local)
        for ts in by_shape.values():
            for t, per_layer in zip(ts, cast_place([local[t] for t in ts], chip1, layers, after)):
                out[t] = per_layer
        return out

    placed = cast_all((0,), mod)
    g_sems, lands_fly, g_token = gather_start([[p[0] for p in placed]], (ffn1_pos, mixer_pos, ffn2_pos), mod, "first")
    if n_layers > 1:
        later = tuple(range(1, n_layers))
        placed = cast_all(later, g_token)
        more_sems, more_fly, g_token = gather_start(
            [[p[j] for p in placed] for j in range(len(later))], (ffn1_pos, rest_pos), g_token, "rest")
        g_sems, lands_fly = g_sems + more_sems, lands_fly + more_fly
    gathered = []

    row = lambda a, l: a[l].reshape(1, -1)
    saved = []
    for l in range(n_layers):
        def fetch(tag, group, members, after, l=l):
            return gather_forward(gather_wait(tag, g_sems[l][group], [lands_fly[l][t] for t in members], after))

        g1, u1, d1 = fetch(f"{l}a", 0, ffn1_pos, xt if l else g_token)
        sv = dict(x0=xt)
        xt, sv["h1"], sv["a1"], sv["sl1"], sv["dsu1"], sv["y1"] = ffn_fwd(
            xt, row(ffn1_norm, l), mod[l, 0], mod[l, 1], mod[l, 2], g1, u1, d1)
        sv["x1"] = xt
        if l == 0:
            win, wq, wkv, wout = fetch("0b", 1, mixer_pos, xt)
        else:
            win, wq, wkv, wout, g2, u2, d2 = fetch(f"{l}b", 1, rest_pos, xt)
        win = win.reshape(-1, d)
        sv["h2"], u, cq, ckv, kr = mix_in_fwd(xt, row(mix_norm, l), mod[l, 3], mod[l, 4], win)
        sv["cq"], sv["ckv"] = cq, ckv
        yp, sv["diff"] = pool_fwd(u, pool_w[l], row(pool_scale, l))
        qh, kh, vh, sv["ql"], sv["kvl"] = mla_qkv_fwd(
            cq, ckv, kr, row(q_a_norm, l), row(kv_a_norm, l), wq, wkv, cos, sin, rot)
        sv["qkv"] = (qh, kh, vh)
        om = attn_fwd(qh, kh, vh)
        xt, sv["ycat"], sv["y2"] = out_proj_fwd(yp, om, wout, xt, mod[l, 5])
        sv["x2"] = xt
        if l == 0:
            g2, u2, d2 = fetch("0c", 2, ffn2_pos, xt)
        gathered.append([g1, u1, d1, win, wq, wkv, wout, g2, u2, d2])
        xt, sv["h3"], sv["a3"], sv["sl3"], sv["dsu3"], sv["y3"] = ffn_fwd(
            xt, row(ffn2_norm, l), mod[l, 6], mod[l, 7], mod[l, 8], g2, u2, d2)
        saved.append(sv)

    loss_vec, dx, d_final_norm = final_loss(xt, final_norm.reshape(1, d), tgt)
    loss = lax.psum(loss_vec[0, 0], ("x", "y", "c"))

    none = [None] * n_layers
    dmods, dnorm1, dnorm2, dnorm3 = list(none), list(none), list(none), list(none)
    dpw, dps, dqan_l, dkvan_l = list(none), list(none), list(none), list(none)
    reduced = [None] * len(local)
    stages = []
    sel_of = lambda l: jnp.stack([mc, chip, jnp.asarray(l, mc.dtype)]).astype(jnp.int32)

    def to_chips(job, after_wait, after_start):
        send, recv, g_fly, lands_p = job.pop("pair")
        g_fly, got = pair_wait(job["tag"], send, recv, g_fly, lands_p, after_wait)
        n_w = len(job["pos"])
        pbs, job["owns"] = pair_add(g_fly[:n_w], g_fly[n_w:], got[:n_w], got[n_w:], sel_of(job["l"]))
        job["scatter"] = scatter_start(pbs, job["tag"], after_start)
        return job["scatter"][4][0, 0]

    def finish(job, after):
        s_send, s_recv, pbs_fly, lands_j, _ = job.pop("scatter")
        parts = scatter_wait(job["tag"], s_send, s_recv, pbs_fly, lands_j, after)
        sums = chip_sum(job["owns"], parts, sel_of(job["l"]), [(n_layers,) + shp for shp in job["shapes"]],
                        [reduced[t] for t in job["pos"]])
        for t, total_t in zip(job["pos"], sums):
            reduced[t] = total_t

    def checkpoint(tag, l, positions, grads_, done, before_scatter=None):
        send, recv, g_fly, lands_p, tok = pair_start([g[0] for g in grads_] + [g[1] for g in grads_], tag, done)
        order = tok[0, 0]
        if stages:
            order = order + to_chips(stages[-1], done, done if before_scatter is None else before_scatter)
        if len(stages) >= 3:
            finish(stages[-3], done)
        stages.append(dict(tag=tag, l=l, pos=positions, shapes=[g[0].shape for g in grads_],
                           pair=(send, recv, g_fly, lands_p)))
        return order

    def small_gather(tag, parts, after):
        packed, spans = _pack_rows(parts, d)
        land = lax.dynamic_update_index_in_dim(lax.empty((N_DEV,) + packed.shape, F32), packed, 4 * mx + 2 * my + mc, 0)
        return gather8_start(packed, land, after, tag), spans

    order = None

    for l in reversed(range(n_layers)):
        sv = saved[l]
        g1, u1, d1, win, wq, wkv, wout, g2, u2, d2 = gathered[l]
        win = win.reshape(-1, d)
        gt3 = mod[l, 8] if order is None else mod[l, 8] + order
        dy, dgt, dup = ffn_bwd_act(dx, sv["sl3"], sv["dsu3"], gt3, d2)
        dx, dvec3 = ffn_bwd_in(dx, sv["x2"], sv["y3"], dgt, dup, row(ffn2_norm, l), mod[l, 7], g2, u2)
        (g_g2, g_u2), g_d2 = tn_mm_pair(dgt, dup, sv["h3"], chip1), nn_mm(sv["a3"], dy, chip1)
        dy2, dyp, dom, dg2 = out_proj_bwd(dx, sv["y2"], mod[l, 5], wout)
        g_wout = nn_mm(sv["ycat"], dy2, chip1)
        qh, kh, vh = sv["qkv"]
        dqh, dkh, dvh = attn_bwd(qh, kh, vh, dom)
        dcq, dckv, dkr_in, gq, gkv, dqan_l[l], dkvan_l[l] = mla_qkv_bwd(
            dqh, dkh, dvh, sv["cq"], sv["ckv"], row(q_a_norm, l), row(kv_a_norm, l), wq, wkv, cos, sin, rot_t)
        g_wq, g_wkv = tn_mm(gq, sv["ql"][None], chip1), tn_mm(sv["kvl"][None], gkv, chip1)
        du, dpw[l], dps[l] = pool_bwd(dyp, sv["diff"], pool_w[l], row(pool_scale, l))
        dx, dz, dvec2 = mix_in_bwd(dx, du, dcq, dckv, dkr_in, sv["x1"], row(mix_norm, l), mod[l, 4], win)
        g_win = nn_mm(dz.reshape(N_CHIPS, -1, dz.shape[1]), sv["h2"], chip1)
        dnorm2[l], dnorm3[l] = dvec2[3], dvec3[3]
        dmod_rest = jnp.concatenate([dvec2[0:2], dg2, dvec3[0:3]], axis=0)
        if l == 0:
            early = small_gather("early", [jnp.stack(dmods[1:]), dmod_rest, jnp.stack(dnorm1[1:]), jnp.stack(dnorm2),
                                           jnp.stack(dnorm3), d_final_norm, jnp.stack(dps), jnp.stack(dqan_l),
                                           jnp.stack(dkvan_l), jnp.stack(dpw)], dx)
        order = checkpoint(f"{l}a", l, rest_pos, [g_win, g_wq, g_wkv, g_wout, g_g2, g_u2, g_d2], dx,
                           early[0][4] if l == 0 else None)
        dy, dgt, dup = ffn_bwd_act(dx, sv["sl1"], sv["dsu1"], mod[l, 2] + order, d1)
        dx, dvec1 = ffn_bwd_in(dx, sv["x0"], sv["y1"], dgt, dup, row(ffn1_norm, l), mod[l, 1], g1, u1)
        (g_g1, g_u1), g_d1 = tn_mm_pair(dgt, dup, sv["h1"], chip1), nn_mm(sv["a1"], dy, chip1)
        dmods[l] = jnp.concatenate([dvec1[0:3], dmod_rest], axis=0)
        dnorm1[l] = dvec1[3]
        if l == 0:
            late = small_gather("late", [dvec1[0:3], dvec1[3]], dx)
        order = checkpoint(f"{l}b", l, ffn1_pos, [g_g1, g_u1, g_d1], dx, late[0][4] if l == 0 else None)

    to_chips(stages[-1], stages[-2]["scatter"][4], stages[-2]["scatter"][4])
    sent = stages[-1]["scatter"][4]
    got_early = gather8_wait("early", *early[0][:4], sent)
    got_late = gather8_wait("late", *late[0][:4], sent)
    each_rest, each0_rest = _unpack_rows(got_early, early[1])[:2]
    each0_first = _unpack_rows(got_late, late[1])[0]
    dmod_all = jnp.concatenate([jnp.concatenate([each0_first, each0_rest], axis=1)[:, None], each_rest], axis=1)
    dmod_all = dmod_all.reshape(N_DEV, n_layers, 9 * d)
    dmod_loc = lax.dynamic_slice_in_dim(dmod_all, chip * ada_cols, ada_cols, axis=2)
    dmod16 = jnp.pad(jnp.transpose(dmod_loc, (1, 0, 2)), ((0, 0), (0, 8), (0, 0)))

    weights = [ada_w, ada_b, ffn1_norm, ffn1_w_gate, ffn1_w_up, ffn1_w_down, mix_norm, w_in, pool_w, pool_scale,
               q_a_norm, w_q_b, kv_a_norm, w_kv_b, w_out, ffn2_norm, ffn2_w_gate, ffn2_w_up, ffn2_w_down, final_norm]
    ms = [m_ada_w, m_ada_b, m_ffn1_norm, m_ffn1_w_gate, m_ffn1_w_up, m_ffn1_w_down, m_mix_norm, m_w_in, m_pool_w,
          m_pool_scale, m_q_a_norm, m_w_q_b, m_kv_a_norm, m_w_kv_b, m_w_out, m_ffn2_norm, m_ffn2_w_gate, m_ffn2_w_up,
          m_ffn2_w_down, m_final_norm]
    vs = [v_ada_w, v_ada_b, v_ffn1_norm, v_ffn1_w_gate, v_ffn1_w_up, v_ffn1_w_down, v_mix_norm, v_w_in, v_pool_w,
          v_pool_scale, v_q_a_norm, v_w_q_b, v_kv_a_norm, v_w_kv_b, v_w_out, v_ffn2_norm, v_ffn2_w_gate, v_ffn2_w_up,
          v_ffn2_w_down, v_final_norm]
    transposed = (3, 4, 7, 11, 16, 17)
    outs = [None] * len(weights)

    outs[0] = adamw(ada_w, ada_bwd(c16, dmod16), m_ada_w, v_ada_w)
    for job in stages[-3:]:
        finish(job, outs[0][1])
    fill_a = fill_start([reduced[t] for t in rest_pos], "a", outs[0][1])
    fill_b = fill_start([reduced[t] for t in ffn1_pos], "b", fill_a[3])

    (g_dmod_rest, g_dmod0_rest, g_n1_rest, g_n2, g_n3, g_fn, g_ps, g_qan, g_kvan, g_pw) = _unpack_rows(
        sum_devices(got_early, fill_b[3]), early[1])
    g_dmod0_first, g_n1_first = _unpack_rows(sum_devices(got_late, fill_b[3]), late[1])
    g_ada_b = jnp.concatenate([jnp.concatenate([g_dmod0_first, g_dmod0_rest], axis=0)[None], g_dmod_rest], axis=0)
    g_n1 = jnp.concatenate([g_n1_first[None], g_n1_rest], axis=0)
    grads = [None, g_ada_b, g_n1, None, None, None, g_n2, None, g_pw, g_ps, g_qan, None, g_kvan, None, None, g_n3,
             None, None, None, g_fn]
    big = [i for i, g in enumerate(grads) if g is None and i > 0]
    for i, (w, g, m, v) in enumerate(zip(weights, grads, ms, vs)):
        if g is not None:
            outs[i] = adamw(w, g.reshape(w.shape), m, v)

    def update(positions, fly, after):
        filled = fill_wait(fly[0], fly[1], fly[2], fly[3], after)
        for t, g in zip(positions, filled):
            i = big[t]
            if i in transposed:
                outs[i] = tuple(tr(o) for o in adamw(tr(weights[i]), g, tr(ms[i]), tr(vs[i]), copy_g=True))
            else:
                outs[i] = adamw(weights[i], g, ms[i], vs[i], copy_g=True)

    update(rest_pos, ("a",) + tuple(fill_a[:3]), outs[8][1])
    update(ffn1_pos, ("b",) + tuple(fill_b[:3]), outs[big[rest_pos[-1]]][1])
    return (loss, dx.reshape(x.shape), *[t[0] for t in outs], *[t[1] for t in outs], *[t[2] for t in outs],
            *[t[3] for t in outs])
```

```python
import math

import jax
import jax.numpy as jnp
from jax import lax
from jax.experimental import pallas as pl
from jax.experimental.pallas import tpu as pltpu

F32 = jnp.float32
BF16 = jnp.bfloat16
MESH = pl.DeviceIdType.MESH

EPS = 1e-6
ROPE_THETA = 10000.0
N_HEADS = 4
QK_NOPE = 128
QK_ROPE = 64
V_HEAD = 128
POOL_WINDOWS = (2, 4, 8, 16)
POOL_GC = 128
POOL_WIDTH = POOL_GC * len(POOL_WINDOWS)
Q_LORA = 384
KV_LORA = 256
SOFTMAX_SCALE = 1.0 / math.sqrt(QK_NOPE + QK_ROPE)
N_CHIPS = 4
N_DEV = 8

ADAM_LR = 0.001
ADAM_B1 = 0.9
ADAM_B2 = 0.999
ADAM_EPS = 1e-08
ADAM_WD = 0.01
ADAM_STEP = 10

ROW_TILE = 512
ATT_TILE = 1024
VMEM_LIMIT = 56 * 1024 * 1024
BF16_ROWS = 16
LANES = 128


def _params(sem=None, vmem=VMEM_LIMIT):
    return pltpu.CompilerParams(dimension_semantics=sem, vmem_limit_bytes=vmem)


def _dot(a, b):
    return jnp.dot(a, b, preferred_element_type=F32)


def _dot_nt(a, b):
    return lax.dot_general(a, b, (((1,), (1,)), ((), ())), preferred_element_type=F32)


def _dot_tn(a, b):
    return lax.dot_general(a, b, (((0,), (0,)), ((), ())), preferred_element_type=F32)


def _dot_exact(t, perm):
    t1 = t.astype(BF16)
    r1 = t - t1.astype(F32)
    t2 = r1.astype(BF16)
    t3 = (r1 - t2.astype(F32)).astype(BF16)
    return _dot(t1, perm) + _dot(t2, perm) + _dot(t3, perm)


def _sum0(a):
    return jnp.sum(a, axis=0, keepdims=True)


def _rms(xt):
    r = lax.rsqrt(jnp.mean(xt * xt, axis=-1, keepdims=True) + EPS)
    return xt * r, r


def _rms_bwd(dy, xt, g):
    xhat, r = _rms(xt)
    dxhat = dy * g
    dx = r * (dxhat - xhat * jnp.mean(dxhat * xhat, axis=-1, keepdims=True))
    return dx, _sum0(dy * xhat)


def _normmod_bwd(dh, xt, gn, sc):
    xhat, _ = _rms(xt)
    dn = dh * (1.0 + sc)
    dx, dgn = _rms_bwd(dn, xt, gn)
    return dx, _sum0(dh), _sum0(dh * (xhat * gn)), dgn


def _row_tile(s):
    return min(s, ROW_TILE)


def _full(shape):
    n = len(shape)
    return pl.BlockSpec(shape, lambda *_: (0,) * n)


def _resident(shape):
    n = len(shape)
    return pl.BlockSpec(shape, lambda *_: (0,) * n, pipeline_mode=pl.Buffered(1))


def ffn_fwd(x, gn, sh, sc, gt, wg, wu, wd):
    s, d = x.shape
    k_chunks, fs, _ = wg.shape
    tm = _row_tile(s)

    def body(x_ref, gn_ref, sh_ref, sc_ref, gt_ref, wg_ref, wu_ref, wd_ref,
             xo_ref, h_ref, a_ref, sl_ref, dsu_ref, y_ref):
        xt = x_ref[...]
        xhat, _ = _rms(xt)
        h = (xhat * gn_ref[...] * (1.0 + sc_ref[...]) + sh_ref[...]).astype(BF16)
        h_ref[...] = h
        y = jnp.zeros((tm, d), F32)
        for k in range(k_chunks):
            gate = _dot_nt(h, wg_ref[k])
            up = _dot_nt(h, wu_ref[k])
            sg = jax.nn.sigmoid(gate)
            sl = gate * sg
            a = (sl * up).astype(BF16)
            a_ref[k] = a.T
            sl_ref[k] = sl.astype(BF16)
            dsu_ref[k] = (up * (sg * (1.0 + gate * (1.0 - sg)))).astype(BF16)
            y += _dot(a, wd_ref[k])
        y_ref[...] = y.astype(BF16)
        xo_ref[...] = xt + 0.5 * gt_ref[...] * y

    row = pl.BlockSpec((tm, d), lambda i: (i, 0))
    vec = pl.BlockSpec((1, d), lambda i: (0, 0))
    act = pl.BlockSpec((k_chunks, tm, fs), lambda i: (0, i, 0))
    act_shape = jax.ShapeDtypeStruct((k_chunks, s, fs), BF16)
    return pl.pallas_call(
        body, name="ffn_fwd",
        grid=(s // tm,),
        in_specs=[row, vec, vec, vec, vec, _resident(wg.shape), _resident(wu.shape), _resident(wd.shape)],
        out_specs=[row, row, pl.BlockSpec((k_chunks, fs, tm), lambda i: (0, 0, i)), act, act, row],
        out_shape=[jax.ShapeDtypeStruct((s, d), F32), jax.ShapeDtypeStruct((s, d), BF16),
                   jax.ShapeDtypeStruct((k_chunks, fs, s), BF16), act_shape, act_shape,
                   jax.ShapeDtypeStruct((s, d), BF16)],
        compiler_params=_params(("arbitrary",)),
    )(x, gn, sh, sc, gt, wg, wu, wd)


def ffn_bwd_act(dxn, sl, dsu, gt, wd):
    s, d = dxn.shape
    k_chunks, fs, _ = wd.shape
    tm = _row_tile(s)

    def body(dxn_ref, sl_ref, dsu_ref, gt_ref, wd_ref, dy_ref, dgate_ref, dup_ref):
        dy = (0.5 * gt_ref[...] * dxn_ref[...]).astype(BF16)
        dy_ref[...] = dy
        for k in range(k_chunks):
            da = _dot_nt(dy, wd_ref[k])
            dgate_ref[k] = (da * dsu_ref[k].astype(F32)).astype(BF16)
            dup_ref[k] = (da * sl_ref[k].astype(F32)).astype(BF16)

    row = pl.BlockSpec((tm, d), lambda i: (i, 0))
    act = pl.BlockSpec((k_chunks, tm, fs), lambda i: (0, i, 0))
    act_shape = jax.ShapeDtypeStruct((k_chunks, s, fs), BF16)
    return pl.pallas_call(
        body, name="ffn_bwd_act",
        grid=(s // tm,),
        in_specs=[row, act, act, pl.BlockSpec((1, d), lambda i: (0, 0)), _resident(wd.shape)],
        out_specs=[row, act, act],
        out_shape=[jax.ShapeDtypeStruct((s, d), BF16), act_shape, act_shape],
        compiler_params=_params(("arbitrary",)),
    )(dxn, sl, dsu, gt, wd)


def ffn_bwd_in(dxn, x, y, dgate, dup, gn, sc, wg, wu):
    s, d = x.shape
    k_chunks, fs, _ = wg.shape
    tm = _row_tile(s)

    def body(dxn_ref, x_ref, y_ref, dgate_ref, dup_ref, gn_ref, sc_ref, wg_ref, wu_ref, dx_ref, dvec_ref):
        i = pl.program_id(0)

        @pl.when(i == 0)
        def _():
            dvec_ref[...] = jnp.zeros_like(dvec_ref)

        dh = jnp.zeros((tm, d), F32)
        for k in range(k_chunks):
            dh += _dot(dgate_ref[k], wg_ref[k]) + _dot(dup_ref[k], wu_ref[k])
        dxn_t = dxn_ref[...]
        dx, dsh, dsc, dgn = _normmod_bwd(dh, x_ref[...], gn_ref[...], sc_ref[...])
        dx_ref[...] = dx + dxn_t
        dvec_ref[0:1, :] += dsh
        dvec_ref[1:2, :] += dsc
        dvec_ref[2:3, :] += _sum0(0.5 * dxn_t * y_ref[...].astype(F32))
        dvec_ref[3:4, :] += dgn

    row = pl.BlockSpec((tm, d), lambda i: (i, 0))
    vec = pl.BlockSpec((1, d), lambda i: (0, 0))
    act = pl.BlockSpec((k_chunks, tm, fs), lambda i: (0, i, 0))
    return pl.pallas_call(
        body, name="ffn_bwd_in",
        grid=(s // tm,),
        in_specs=[row, row, row, act, act, vec, vec, _resident(wg.shape), _resident(wu.shape)],
        out_specs=[row, pl.BlockSpec((8, d), lambda i: (0, 0))],
        out_shape=[jax.ShapeDtypeStruct((s, d), F32), jax.ShapeDtypeStruct((8, d), F32)],
        compiler_params=_params(("arbitrary",)),
    )(dxn, x, y, dgate, dup, gn, sc, wg, wu)


def _grad_mm(dot, a, b, a_spec, b_spec, g, m, n, chip, name):
    def body(c_ref, a_ref, b_ref, own_ref, all_ref):
        res = dot(a_ref[...], b_ref[...])
        all_ref[...] = res.astype(BF16)

        @pl.when(pl.program_id(0) == c_ref[0])
        def _():
            own_ref[...] = res

    return pl.pallas_call(
        body, name=name,
        grid_spec=pltpu.PrefetchScalarGridSpec(
            num_scalar_prefetch=1, grid=(g,), in_specs=[a_spec, b_spec],
            out_specs=[pl.BlockSpec((m, n), lambda gi, c: (0, 0)), pl.BlockSpec((None, m, n), lambda gi, c: (gi, 0, 0))]),
        out_shape=[jax.ShapeDtypeStruct((m, n), F32), jax.ShapeDtypeStruct((g, m, n), BF16)],
        compiler_params=_params(("arbitrary",)),
    )(chip, a, b)


def tn_mm_pair(a1, a2, b, chip):
    g, s, m = a1.shape
    n = b.shape[1]

    def body(c_ref, a1_ref, a2_ref, b_ref, own1_ref, all1_ref, own2_ref, all2_ref):
        gi = pl.program_id(0)

        def one(a_ref, own_ref, all_ref, slot):
            res = _dot_tn(a_ref[...], b_ref[...])
            all_ref[...] = res.astype(BF16)

            @pl.when(slot == c_ref[0])
            def _():
                own_ref[...] = res

        @pl.when(gi < g)
        def _():
            one(a1_ref, own1_ref, all1_ref, gi)

        @pl.when(gi >= g)
        def _():
            one(a2_ref, own2_ref, all2_ref, gi - g)

    first = lambda gi, c: (jnp.minimum(gi, g - 1), 0, 0)
    second = lambda gi, c: (jnp.maximum(gi - g, 0), 0, 0)
    own = pl.BlockSpec((m, n), lambda gi, c: (0, 0))
    outs = pl.pallas_call(
        body, name="tn_mm_pair",
        grid_spec=pltpu.PrefetchScalarGridSpec(
            num_scalar_prefetch=1, grid=(2 * g,),
            in_specs=[pl.BlockSpec((None, s, m), first), pl.BlockSpec((None, s, m), second),
                      pl.BlockSpec((s, n), lambda gi, c: (0, 0))],
            out_specs=[own, pl.BlockSpec((None, m, n), first), own, pl.BlockSpec((None, m, n), second)]),
        out_shape=[jax.ShapeDtypeStruct((m, n), F32), jax.ShapeDtypeStruct((g, m, n), BF16)] * 2,
        compiler_params=_params(("arbitrary",)),
    )(chip, a1, a2, b)
    return (outs[0], outs[1]), (outs[2], outs[3])


def nn_mm(a_t, b, chip):
    g, m, s = a_t.shape
    n = b.shape[1]
    return _grad_mm(_dot, a_t, b, pl.BlockSpec((None, m, s), lambda gi, c: (gi, 0, 0)),
                    pl.BlockSpec((s, n), lambda gi, c: (0, 0)), g, m, n, chip, "nn_mm")


def tn_mm(a, b, chip):
    ga, s, m = a.shape
    gb, _, n = b.shape
    a_spec = pl.BlockSpec((None, s, m), (lambda gi, c: (gi, 0, 0)) if ga > 1 else (lambda gi, c: (0, 0, 0)))
    b_spec = pl.BlockSpec((None, s, n), (lambda gi, c: (gi, 0, 0)) if gb > 1 else (lambda gi, c: (0, 0, 0)))
    return _grad_mm(_dot_tn, a, b, a_spec, b_spec, max(ga, gb), m, n, chip, "tn_mm")


def mix_in_fwd(x, gn, sh, sc, w_in_t):
    s, d = x.shape
    tm = _row_tile(s)
    o1, o2, o3 = POOL_WIDTH, POOL_WIDTH + Q_LORA, POOL_WIDTH + Q_LORA + KV_LORA

    def body(x_ref, gn_ref, sh_ref, sc_ref, w_ref, h_ref, u_ref, cq_ref, ckv_ref, kr_ref):
        xhat, _ = _rms(x_ref[...])
        h = (xhat * gn_ref[...] * (1.0 + sc_ref[...]) + sh_ref[...]).astype(BF16)
        h_ref[...] = h
        z = _dot_nt(h, w_ref[0:o3, :])
        u_ref[...] = z[:, 0:o1]
        cq_ref[...] = z[:, o1:o2]
        ckv_ref[...] = z[:, o2:o3]
        kr_ref[...] = _dot_nt(h, w_ref[o3:, :])

    row = lambda w: pl.BlockSpec((tm, w), lambda i: (i, 0))
    vec = pl.BlockSpec((1, d), lambda i: (0, 0))
    return pl.pallas_call(
        body, name="mix_in_fwd",
        grid=(s // tm,),
        in_specs=[row(d), vec, vec, vec, _full(w_in_t.shape)],
        out_specs=[row(d), row(POOL_WIDTH), row(Q_LORA), row(KV_LORA), row(QK_ROPE)],
        out_shape=[jax.ShapeDtypeStruct((s, d), BF16), jax.ShapeDtypeStruct((s, POOL_WIDTH), F32),
                   jax.ShapeDtypeStruct((s, Q_LORA), F32), jax.ShapeDtypeStruct((s, KV_LORA), F32),
                   jax.ShapeDtypeStruct((s, QK_ROPE), F32)],
        compiler_params=_params(("arbitrary",)),
    )(x, gn, sh, sc, w_in_t)


def mix_in_bwd(dxn, du, dcq, dckv, dkr, x, gn, sc, w_in_t):
    s, d = x.shape
    tm = _row_tile(s)
    o1, o2, o3 = POOL_WIDTH, POOL_WIDTH + Q_LORA, POOL_WIDTH + Q_LORA + KV_LORA
    n_z = w_in_t.shape[0]

    def body(dxn_ref, du_ref, dcq_ref, dckv_ref, dkr_ref, x_ref, gn_ref, sc_ref, w_ref, dx_ref, dz_ref, dvec_ref):
        i = pl.program_id(0)

        @pl.when(i == 0)
        def _():
            dvec_ref[...] = jnp.zeros_like(dvec_ref)

        dub = du_ref[...].astype(BF16)
        dqb = dcq_ref[...].astype(BF16)
        dkb = dckv_ref[...].astype(BF16)
        drb = dkr_ref[...].astype(BF16)
        dz_ref[0:o1, :] = dub.T
        dz_ref[o1:o2, :] = dqb.T
        dz_ref[o2:o3, :] = dkb.T
        dz_ref[o3:, :] = drb.T
        dh = (_dot(dub, w_ref[0:o1, :]) + _dot(dqb, w_ref[o1:o2, :]) + _dot(dkb, w_ref[o2:o3, :])
              + _dot(drb, w_ref[o3:, :]))
        dx, dsh, dsc, dgn = _normmod_bwd(dh, x_ref[...], gn_ref[...], sc_ref[...])
        dx_ref[...] = dx + dxn_ref[...]
        dvec_ref[0:1, :] += dsh
        dvec_ref[1:2, :] += dsc
        dvec_ref[3:4, :] += dgn

    row = lambda w: pl.BlockSpec((tm, w), lambda i: (i, 0))
    vec = pl.BlockSpec((1, d), lambda i: (0, 0))
    return pl.pallas_call(
        body, name="mix_in_bwd",
        grid=(s // tm,),
        in_specs=[row(d), row(POOL_WIDTH), row(Q_LORA), row(KV_LORA), row(QK_ROPE), row(d), vec, vec,
                  _full(w_in_t.shape)],
        out_specs=[row(d), pl.BlockSpec((n_z, tm), lambda i: (0, i)), pl.BlockSpec((8, d), lambda i: (0, 0))],
        out_shape=[jax.ShapeDtypeStruct((s, d), F32), jax.ShapeDtypeStruct((n_z, s), BF16),
                   jax.ShapeDtypeStruct((8, d), F32)],
        compiler_params=_params(("arbitrary",)),
    )(dxn, du, dcq, dckv, dkr, x, gn, sc, w_in_t)


def _window_sum(a, w, rows, forward):
    s = a.shape[0]
    step = 1
    while step < w:
        if forward:
            shifted = jnp.where(rows < s - step, pltpu.roll(a, s - step, 0), 0.0)
        else:
            shifted = jnp.where(rows >= step, pltpu.roll(a, step, 0), 0.0)
        a = a + shifted
        step *= 2
    return a


def pool_fwd(u, pool_w, pool_scale):
    s = u.shape[0]

    def body(u_ref, w_ref, sc_ref, y_ref, diff_ref):
        rows = lax.broadcasted_iota(jnp.int32, (s, POOL_GC), 0)
        for g, w in enumerate(POOL_WINDOWS):
            cols = slice(g * POOL_GC, (g + 1) * POOL_GC)
            ug = u_ref[:, cols]
            cnt = jnp.minimum(rows + 1, w).astype(F32)
            diff = (_window_sum(ug, w, rows, False) / cnt - ug).astype(BF16)
            diff_ref[:, cols] = diff
            y_ref[:, cols] = _dot(diff, w_ref[g].astype(BF16)) * sc_ref[:, cols]

    return pl.pallas_call(
        body, name="pool_fwd",
        out_shape=[jax.ShapeDtypeStruct(u.shape, F32), jax.ShapeDtypeStruct(u.shape, BF16)],
        compiler_params=_params(),
    )(u, pool_w, pool_scale)


def pool_bwd(dy, diff, pool_w, pool_scale):
    s = dy.shape[0]

    def body(dy_ref, diff_ref, w_ref, sc_ref, du_ref, dw_ref, dsc_ref):
        rows = lax.broadcasted_iota(jnp.int32, (s, POOL_GC), 0)
        for g, w in enumerate(POOL_WINDOWS):
            cols = slice(g * POOL_GC, (g + 1) * POOL_GC)
            dyg = dy_ref[:, cols]
            diff = diff_ref[:, cols]
            wb = w_ref[g].astype(BF16)
            dsc_ref[:, cols] = _sum0(dyg * _dot(diff, wb))
            dys = (dyg * sc_ref[:, cols]).astype(BF16)
            dw_ref[g] = _dot_tn(diff, dys)
            ddiff = _dot_nt(dys, wb)
            cnt = jnp.minimum(rows + 1, w).astype(F32)
            du_ref[:, cols] = _window_sum(ddiff / cnt, w, rows, True) - ddiff

    return pl.pallas_call(
        body, name="pool_bwd",
        out_shape=[jax.ShapeDtypeStruct(dy.shape, F32), jax.ShapeDtypeStruct(pool_w.shape, F32),
                   jax.ShapeDtypeStruct(pool_scale.shape, F32)],
        compiler_params=_params(),
    )(dy, diff, pool_w, pool_scale)


def mla_qkv_fwd(cq, ckv, kr, qan, kvan, wq, wkv, cos, sin, rot):
    s = cq.shape[0]
    tm = _row_tile(s)

    def body(cq_ref, ckv_ref, kr_ref, qan_ref, kvan_ref, wq_ref, wkv_ref, cos_ref, sin_ref, rot_ref,
             q_ref, k_ref, v_ref, ql_ref, kvl_ref):
        cos_t = cos_ref[...]
        sin_t = sin_ref[...]
        perm = rot_ref[...]

        def rope(t):
            return t * cos_t + _dot_exact(t, perm) * sin_t

        qhat, _ = _rms(cq_ref[...])
        ql = (qhat * qan_ref[...]).astype(BF16)
        ql_ref[...] = ql
        khat, _ = _rms(ckv_ref[...])
        kvl = (khat * kvan_ref[...]).astype(BF16)
        kvl_ref[...] = kvl
        krr = rope(kr_ref[...]).astype(BF16)
        for h in range(N_HEADS):
            q = _dot_nt(ql, wq_ref[h])
            q_ref[h, :, 0:QK_NOPE] = q[:, 0:QK_NOPE].astype(BF16)
            q_ref[h, :, QK_NOPE:] = rope(q[:, QK_NOPE:]).astype(BF16)
            kv = _dot(kvl, wkv_ref[h])
            k_ref[h, :, 0:QK_NOPE] = kv[:, 0:QK_NOPE].astype(BF16)
            k_ref[h, :, QK_NOPE:] = krr
            v_ref[h] = kv[:, QK_NOPE:].astype(BF16)

    row = lambda w: pl.BlockSpec((tm, w), lambda i: (i, 0))
    hrow = lambda w: pl.BlockSpec((N_HEADS, tm, w), lambda i: (0, i, 0))
    qk = QK_NOPE + QK_ROPE
    return pl.pallas_call(
        body, name="mla_qkv_fwd",
        grid=(s // tm,),
        in_specs=[row(Q_LORA), row(KV_LORA), row(QK_ROPE), _full(qan.shape), _full(kvan.shape),
                  _full(wq.shape), _full(wkv.shape), row(QK_ROPE), row(QK_ROPE), _full(rot.shape)],
        out_specs=[hrow(qk), hrow(qk), hrow(V_HEAD), row(Q_LORA), row(KV_LORA)],
        out_shape=[jax.ShapeDtypeStruct((N_HEADS, s, qk), BF16), jax.ShapeDtypeStruct((N_HEADS, s, qk), BF16),
                   jax.ShapeDtypeStruct((N_HEADS, s, V_HEAD), BF16), jax.ShapeDtypeStruct((s, Q_LORA), BF16),
                   jax.ShapeDtypeStruct((s, KV_LORA), BF16)],
        compiler_params=_params(("arbitrary",)),
    )(cq, ckv, kr, qan, kvan, wq, wkv, cos, sin, rot)


def _attn_probs(q_ref, k_ref, qi, tq):
    n = (qi + 1) * tq
    rows = slice(qi * tq, n)
    sc = _dot_nt(q_ref[rows, :], k_ref[0:n, :]) * SOFTMAX_SCALE
    qpos = qi * tq + lax.broadcasted_iota(jnp.int32, (tq, n), 0)
    kpos = lax.broadcasted_iota(jnp.int32, (tq, n), 1)
    sc = jnp.where(qpos >= kpos, sc, -1e30)
    e = jnp.exp(sc - jnp.max(sc, axis=-1, keepdims=True))
    return e * (1.0 / jnp.sum(e, axis=-1, keepdims=True))


def attn_fwd(q, k, v):
    nh, s, qk = q.shape
    tq = min(s, ATT_TILE)

    def body(q_ref, k_ref, v_ref, o_ref):
        for qi in range(s // tq):
            n = (qi + 1) * tq
            p = _attn_probs(q_ref, k_ref, qi, tq).astype(BF16)
            o_ref[qi * tq:n, :] = _dot(p, v_ref[0:n, :])

    head = lambda w: pl.BlockSpec((None, s, w), lambda h: (h, 0, 0))
    return pl.pallas_call(
        body, name="attn_fwd",
        grid=(nh,),
        in_specs=[head(qk), head(qk), head(V_HEAD)],
        out_specs=pl.BlockSpec((s, V_HEAD), lambda h: (0, h)),
        out_shape=jax.ShapeDtypeStruct((s, nh * V_HEAD), F32),
        compiler_params=_params(("arbitrary",)),
    )(q, k, v)


def attn_bwd(q, k, v, do):
    nh, s, qk = q.shape
    tq = min(s, ATT_TILE)

    def body(q_ref, k_ref, v_ref, do_ref, dq_ref, dk_ref, dv_ref):
        dk_ref[...] = jnp.zeros_like(dk_ref)
        dv_ref[...] = jnp.zeros_like(dv_ref)
        for qi in range(s // tq):
            n = (qi + 1) * tq
            rows = slice(qi * tq, n)
            p = _attn_probs(q_ref, k_ref, qi, tq)
            dob = do_ref[rows, :].astype(BF16)
            dp = _dot_nt(dob, v_ref[0:n, :])
            ds = (p * (dp - jnp.sum(p * dp, axis=-1, keepdims=True)) * SOFTMAX_SCALE).astype(BF16)
            dq_ref[rows, :] = _dot(ds, k_ref[0:n, :])
            dk_ref[0:n, :] += _dot_tn(ds, q_ref[rows, :])
            dv_ref[0:n, :] += _dot_tn(p.astype(BF16), dob)

    head = lambda w: pl.BlockSpec((None, s, w), lambda h: (h, 0, 0))
    return pl.pallas_call(
        body, name="attn_bwd",
        grid=(nh,),
        in_specs=[head(qk), head(qk), head(V_HEAD), pl.BlockSpec((s, V_HEAD), lambda h: (0, h))],
        out_specs=[head(qk), head(qk), head(V_HEAD)],
        out_shape=[jax.ShapeDtypeStruct((nh, s, qk), F32), jax.ShapeDtypeStruct((nh, s, qk), F32),
                   jax.ShapeDtypeStruct((nh, s, V_HEAD), F32)],
        compiler_params=_params(("arbitrary",)),
    )(q, k, v, do)


def mla_qkv_bwd(dq, dk, dv, cq, ckv, qan, kvan, wq, wkv, cos, sin, rot_t):
    s = cq.shape[0]
    tm = _row_tile(s)

    def body(dq_ref, dk_ref, dv_ref, cq_ref, ckv_ref, qan_ref, kvan_ref,
             wq_ref, wkv_ref, cos_ref, sin_ref, rot_ref,
             dcq_ref, dckv_ref, dkro_ref, gq_ref, gkv_ref, dqan_ref, dkvan_ref):
        i = pl.program_id(0)

        @pl.when(i == 0)
        def _():
            dqan_ref[...] = jnp.zeros_like(dqan_ref)
            dkvan_ref[...] = jnp.zeros_like(dkvan_ref)

        cos_t = cos_ref[...]
        sin_t = sin_ref[...]
        perm_t = rot_ref[...]

        def unrope(t):
            return t * cos_t + _dot_exact(t * sin_t, perm_t)

        acc_q = jnp.zeros((tm, Q_LORA), F32)
        acc_kv = jnp.zeros((tm, KV_LORA), F32)
        dkr_sum = jnp.zeros((tm, QK_ROPE), F32)
        for h in range(N_HEADS):
            dq_h = dq_ref[h]
            a = dq_h[:, 0:QK_NOPE].astype(BF16)
            b = unrope(dq_h[:, QK_NOPE:]).astype(BF16)
            gq_ref[h, :, 0:QK_NOPE] = a
            gq_ref[h, :, QK_NOPE:] = b
            wq_h = wq_ref[h]
            acc_q += _dot(a, wq_h[0:QK_NOPE, :]) + _dot(b, wq_h[QK_NOPE:, :])
            dk_h = dk_ref[h]
            dk = dk_h[:, 0:QK_NOPE].astype(BF16)
            dvv = dv_ref[h].astype(BF16)
            gkv_ref[h, :, 0:QK_NOPE] = dk
            gkv_ref[h, :, QK_NOPE:] = dvv
            wkv_h = wkv_ref[h]
            acc_kv += _dot_nt(dk, wkv_h[:, 0:QK_NOPE]) + _dot_nt(dvv, wkv_h[:, QK_NOPE:])
            dkr_sum += dk_h[:, QK_NOPE:]
        dkro_ref[...] = unrope(dkr_sum)
        dcq, dqan = _rms_bwd(acc_q, cq_ref[...], qan_ref[...])
        dcq_ref[...] = dcq
        dqan_ref[...] += dqan
        dckv, dkvan = _rms_bwd(acc_kv, ckv_ref[...], kvan_ref[...])
        dckv_ref[...] = dckv
        dkvan_ref[...] += dkvan

    row = lambda w: pl.BlockSpec((tm, w), lambda i: (i, 0))
    hrow = lambda w: pl.BlockSpec((N_HEADS, tm, w), lambda i: (0, i, 0))
    return pl.pallas_call(
        body, name="mla_qkv_bwd",
        grid=(s // tm,),
        in_specs=[hrow(QK_NOPE + QK_ROPE), hrow(QK_NOPE + QK_ROPE), hrow(V_HEAD),
                  row(Q_LORA), row(KV_LORA), _full(qan.shape), _full(kvan.shape),
                  _full(wq.shape), _full(wkv.shape), row(QK_ROPE), row(QK_ROPE), _full(rot_t.shape)],
        out_specs=[row(Q_LORA), row(KV_LORA), row(QK_ROPE), hrow(QK_NOPE + QK_ROPE), hrow(QK_NOPE + V_HEAD),
                   _full(qan.shape), _full(kvan.shape)],
        out_shape=[jax.ShapeDtypeStruct((s, Q_LORA), F32), jax.ShapeDtypeStruct((s, KV_LORA), F32),
                   jax.ShapeDtypeStruct((s, QK_ROPE), F32),
                   jax.ShapeDtypeStruct((N_HEADS, s, QK_NOPE + QK_ROPE), BF16),
                   jax.ShapeDtypeStruct((N_HEADS, s, QK_NOPE + V_HEAD), BF16),
                   jax.ShapeDtypeStruct(qan.shape, F32), jax.ShapeDtypeStruct(kvan.shape, F32)],
        compiler_params=_params(("arbitrary",)),
    )(dq, dk, dv, cq, ckv, qan, kvan, wq, wkv, cos, sin, rot_t)


def out_proj_fwd(yp, om, w_out, x, gt):
    s, d = x.shape
    n_sh, rs, _ = w_out.shape
    tm = _row_tile(s)
    per = POOL_WIDTH // rs

    def body(yp_ref, om_ref, w_ref, x_ref, gt_ref, xo_ref, ycat_ref, y_ref):
        y = jnp.zeros((tm, d), F32)
        for j in range(n_sh):
            src = yp_ref if j < per else om_ref
            part = src[:, (j % per) * rs:(j % per + 1) * rs].astype(BF16)
            ycat_ref[j] = part.T
            y += _dot(part, w_ref[j])
        y_ref[...] = y.astype(BF16)
        xo_ref[...] = x_ref[...] + gt_ref[...] * y

    row = lambda w: pl.BlockSpec((tm, w), lambda i: (i, 0))
    return pl.pallas_call(
        body, name="out_proj_fwd",
        grid=(s // tm,),
        in_specs=[row(POOL_WIDTH), row(POOL_WIDTH), _full(w_out.shape), row(d), pl.BlockSpec((1, d), lambda i: (0, 0))],
        out_specs=[row(d), pl.BlockSpec((n_sh, rs, tm), lambda i: (0, 0, i)), row(d)],
        out_shape=[jax.ShapeDtypeStruct((s, d), F32), jax.ShapeDtypeStruct((n_sh, rs, s), BF16),
                   jax.ShapeDtypeStruct((s, d), BF16)],
        compiler_params=_params(("arbitrary",)),
    )(yp, om, w_out, x, gt)


def out_proj_bwd(dxn, y, gt, w_out):
    s, d = dxn.shape
    n_sh, rs, _ = w_out.shape
    tm = _row_tile(s)
    per = POOL_WIDTH // rs

    def body(dxn_ref, y_ref, gt_ref, w_ref, dy_ref, dyp_ref, dom_ref, dgt_ref):
        i = pl.program_id(0)

        @pl.when(i == 0)
        def _():
            dgt_ref[...] = jnp.zeros_like(dgt_ref)

        dxn_t = dxn_ref[...]
        dy = (gt_ref[...] * dxn_t).astype(BF16)
        dy_ref[...] = dy
        dgt_ref[...] += _sum0(dxn_t * y_ref[...].astype(F32))
        for j in range(n_sh):
            dst = dyp_ref if j < per else dom_ref
            dst[:, (j % per) * rs:(j % per + 1) * rs] = _dot_nt(dy, w_ref[j])

    row = lambda w: pl.BlockSpec((tm, w), lambda i: (i, 0))
    vec = pl.BlockSpec((1, d), lambda i: (0, 0))
    return pl.pallas_call(
        body, name="out_proj_bwd",
        grid=(s // tm,),
        in_specs=[row(d), row(d), vec, _full(w_out.shape)],
        out_specs=[row(d), row(POOL_WIDTH), row(POOL_WIDTH), vec],
        out_shape=[jax.ShapeDtypeStruct((s, d), BF16), jax.ShapeDtypeStruct((s, POOL_WIDTH), F32),
                   jax.ShapeDtypeStruct((s, POOL_WIDTH), F32), jax.ShapeDtypeStruct((1, d), F32)],
        compiler_params=_params(("arbitrary",)),
    )(dxn, y, gt, w_out)


def final_loss(x, gn, tgt):
    s, d = x.shape
    tm = _row_tile(s)

    def body(x_ref, gn_ref, t_ref, loss_ref, dx_ref, dgn_ref):
        i = pl.program_id(0)

        @pl.when(i == 0)
        def _():
            loss_ref[...] = jnp.zeros_like(loss_ref)
            dgn_ref[...] = jnp.zeros_like(dgn_ref)

        xt = x_ref[...]
        g = gn_ref[...]
        xhat, _ = _rms(xt)
        err = xhat * g - t_ref[...]
        per_tok = jnp.mean(err * err, axis=-1, keepdims=True)
        loss_ref[...] += jnp.broadcast_to(0.5 * _sum0(per_tok), loss_ref.shape)
        dx, dgn = _rms_bwd(err * (1.0 / d), xt, g)
        dx_ref[...] = dx
        dgn_ref[...] += dgn

    row = pl.BlockSpec((tm, d), lambda i: (i, 0))
    vec = pl.BlockSpec((1, d), lambda i: (0, 0))
    return pl.pallas_call(
        body, name="final_loss",
        grid=(s // tm,),
        in_specs=[row, vec, row],
        out_specs=[pl.BlockSpec((1, LANES), lambda i: (0, 0)), row, vec],
        out_shape=[jax.ShapeDtypeStruct((1, LANES), F32), jax.ShapeDtypeStruct((s, d), F32),
                   jax.ShapeDtypeStruct((1, d), F32)],
        compiler_params=_params(("arbitrary",)),
    )(x, gn, tgt)


def _col_tile(cols):
    return 768 if cols % 768 == 0 else cols


def ada_fwd(c16, ada_w, ada_b_loc):
    n_layers, d, cols = ada_w.shape
    tn = _col_tile(cols)

    def body(c_ref, w_ref, b_ref, o_ref):
        cv = c_ref[...]
        ca = (cv * jax.nn.sigmoid(cv)).astype(BF16)
        o_ref[...] = _dot(ca, w_ref[...].astype(BF16)) + b_ref[...]

    return pl.pallas_call(
        body, name="ada_fwd",
        grid=(n_layers, cols // tn),
        in_specs=[pl.BlockSpec((16, d), lambda l, j: (0, 0)), pl.BlockSpec((None, d, tn), lambda l, j: (l, 0, j)),
                  pl.BlockSpec((None, 1, tn), lambda l, j: (l, 0, j))],
        out_specs=pl.BlockSpec((None, 16, tn), lambda l, j: (l, 0, j)),
        out_shape=jax.ShapeDtypeStruct((n_layers, 16, cols), F32),
        compiler_params=_params(("arbitrary", "arbitrary")),
    )(c16, ada_w, ada_b_loc)


def ada_bwd(c16, dmod16):
    n_layers, _, cols = dmod16.shape
    d = c16.shape[1]
    tn = _col_tile(cols)

    def body(c_ref, g_ref, o_ref):
        cv = c_ref[...]
        ca = (cv * jax.nn.sigmoid(cv)).astype(BF16)
        o_ref[...] = _dot_tn(ca, g_ref[...].astype(BF16))

    return pl.pallas_call(
        body, name="ada_bwd",
        grid=(n_layers, cols // tn),
        in_specs=[pl.BlockSpec((16, d), lambda l, j: (0, 0)), pl.BlockSpec((None, 16, tn), lambda l, j: (l, 0, j))],
        out_specs=pl.BlockSpec((None, d, tn), lambda l, j: (l, 0, j)),
        out_shape=jax.ShapeDtypeStruct((n_layers, d, cols), F32),
        compiler_params=_params(("arbitrary", "arbitrary")),
    )(c16, dmod16)


def _as_rows(a):
    if a.ndim == 1:
        return a.reshape(1, a.shape[0])
    return a.reshape(-1, a.shape[-1])


def _rows_tile(r, c, itemsize=4, budget=2 * 1024 * 1024):
    if r * c * itemsize <= budget:
        return r
    best = None
    t = BF16_ROWS
    while t < r:
        if r % t == 0 and t * c * itemsize <= budget:
            best = t
        t += BF16_ROWS
    return best if best is not None else r


CAST_VMEM = 16 * 1024 * 1024


def cast_place(ws, chip, layers, after):
    _, r, c = ws[0].shape
    n_sel = len(layers)
    n_blk = len(ws) * n_sel
    tr = _rows_tile(r, c, budget=CAST_VMEM // (3 * n_blk))

    def body(chip_ref, *refs):
        for j in range(n_blk):
            refs[n_blk + 1 + j][...] = refs[j][...].astype(BF16)

    layer_spec = lambda l: pl.BlockSpec((None, tr, c), lambda i, ch: (l, i, 0))
    outs = pl.pallas_call(
        body, name="cast_place",
        grid_spec=pltpu.PrefetchScalarGridSpec(
            num_scalar_prefetch=1, grid=(r // tr,),
            in_specs=[layer_spec(l) for _ in ws for l in layers] + [pl.BlockSpec(memory_space=pl.ANY)],
            out_specs=[pl.BlockSpec((None, tr, c), lambda i, ch: (ch[0], i, 0))] * n_blk),
        out_shape=[jax.ShapeDtypeStruct((N_CHIPS, r, c), BF16)] * n_blk,
        compiler_params=_params(("arbitrary",)),
    )(chip, *[w for w in ws for _ in layers], after)
    return [list(outs[i * n_sel:(i + 1) * n_sel]) for i in range(len(ws))]


def adamw(w, g, m, v, copy_g=False):
    shape = w.shape
    w2, g2, m2, v2 = (_as_rows(t) for t in (w, g, m, v))
    r, c = w2.shape
    tr = _rows_tile(r, c, budget=3 * 1024 * 1024)
    c1 = 1.0 - ADAM_B1 ** ADAM_STEP
    c2 = 1.0 - ADAM_B2 ** ADAM_STEP

    def body(w_ref, g_ref, m_ref, v_ref, d_ref, mo_ref, vo_ref, *go_ref):
        gv = g_ref[...]
        if copy_g:
            go_ref[0][...] = gv
        mn = ADAM_B1 * m_ref[...] + (1.0 - ADAM_B1) * gv
        vn = ADAM_B2 * v_ref[...] + (1.0 - ADAM_B2) * (gv * gv)
        mo_ref[...] = mn
        vo_ref[...] = vn
        d_ref[...] = -ADAM_LR * ((mn / c1) / (jnp.sqrt(vn / c2) + ADAM_EPS) + ADAM_WD * w_ref[...])

    spec = pl.BlockSpec((tr, c), lambda i: (i, 0))
    n_out = 4 if copy_g else 3
    outs = pl.pallas_call(
        body, name="adamw", grid=(r // tr,), in_specs=[spec] * 4, out_specs=[spec] * n_out,
        out_shape=[jax.ShapeDtypeStruct((r, c), F32)] * n_out, compiler_params=_params(("arbitrary",)),
    )(w2, g2, m2, v2)
    g_out = outs[3] if copy_g else g2
    return tuple(o.reshape(shape) for o in (g_out,) + tuple(outs[:3]))


def sum_devices(a, after):
    n, r, c = a.shape
    tr = _rows_tile(r, c, budget=512 * 1024)

    def body(a_ref, after_ref, o_ref):
        acc = a_ref[0]
        for j in range(1, n):
            acc = acc + a_ref[j]
        o_ref[...] = acc

    return pl.pallas_call(
        body, name="sum_devices", grid=(r // tr,),
        in_specs=[pl.BlockSpec((n, tr, c), lambda i: (0, i, 0)), pl.BlockSpec(memory_space=pl.ANY)],
        out_specs=pl.BlockSpec((tr, c), lambda i: (i, 0)),
        out_shape=jax.ShapeDtypeStruct((r, c), F32), compiler_params=_params(("arbitrary",)),
    )(a, after)


def _split_axis(r, c):
    if (r // 2) % BF16_ROWS == 0 and r % 2 == 0:
        return 0
    assert c % (2 * LANES) == 0, (r, c)
    return 1


def _half_shape(r, c):
    return (r // 2, c) if _split_axis(r, c) == 0 else (r, c // 2)


def _half_at(ref, lead, which):
    r, c = ref.shape[-2:]
    if _split_axis(r, c) == 0:
        return ref.at[(*lead, pl.ds(which * (r // 2), r // 2), slice(None))]
    return ref.at[(*lead, slice(None), pl.ds(which * (c // 2), c // 2))]


def _half_spec(r, c, lead_block, imap):
    hr, hc = _half_shape(r, c)
    if _split_axis(r, c) == 0:
        return pl.BlockSpec((*lead_block, hr, hc), lambda *a: (*imap(*a)[0], imap(*a)[1], 0))
    return pl.BlockSpec((*lead_block, hr, hc), lambda *a: (*imap(*a)[0], 0, imap(*a)[1]))


def pair_add(owns, alls, ra_owns, ra_alls, sel):
    n = len(owns)
    n_sl = alls[0].shape[0]
    halves = [_half_shape(*g.shape) for g in owns]

    def body(s_ref, *refs):
        own_refs, all_refs, ra_own_refs, ra_all_refs, pb_refs, sum_refs = (refs[i * n:(i + 1) * n] for i in range(6))
        k = pl.program_id(0)
        for t in range(n):
            pb_refs[t][...] = (all_refs[t][...].astype(F32) + ra_all_refs[t][...].astype(F32)).astype(BF16)

            @pl.when(k == s_ref[1])
            def _(t=t):
                sum_refs[t][...] = own_refs[t][...] + ra_own_refs[t][...]

    slot = lambda hs: pl.BlockSpec((None,) + hs, lambda k, sr: (k, 0, 0))
    whole = lambda hs: pl.BlockSpec(hs, lambda k, sr: (0, 0))
    outs = pl.pallas_call(
        body, name="pair_add",
        grid_spec=pltpu.PrefetchScalarGridSpec(
            num_scalar_prefetch=1, grid=(n_sl,),
            in_specs=[_half_spec(*g.shape, (), lambda k, sr: ((), sr[0])) for g in owns]
            + [_half_spec(*g.shape[1:], (None,), lambda k, sr: ((k,), sr[0])) for g in alls]
            + [whole(hs) for hs in halves] + [slot(hs) for hs in halves],
            out_specs=[slot(hs) for hs in halves] + [whole(hs) for hs in halves]),
        out_shape=[jax.ShapeDtypeStruct((n_sl,) + hs, BF16) for hs in halves]
        + [jax.ShapeDtypeStruct(hs, F32) for hs in halves],
        compiler_params=_params(("arbitrary",)),
    )(sel, *owns, *alls, *ra_owns, *ra_alls)
    return list(outs[:n]), list(outs[n:])


def chip_sum(owns, rbs, sel, shapes, accs):
    n = len(owns)
    fresh = accs[0] is None

    def body(s_ref, *refs):
        own_refs, rb_refs, o_refs = refs[:n], refs[n:2 * n], refs[-n:]
        for t in range(n):
            acc_v = own_refs[t][...]
            for j in range(N_CHIPS - 1):
                acc_v = acc_v + rb_refs[t][j].astype(F32)
            o_refs[t][...] = acc_v

    in_specs = ([pl.BlockSpec(o.shape, lambda i, sr: (0, 0)) for o in owns]
                + [pl.BlockSpec(rb.shape, lambda i, sr: (0, 0, 0)) for rb in rbs])
    args = [sel, *owns, *rbs]
    aliases = {}
    if not fresh:
        in_specs += [pl.BlockSpec(memory_space=pl.ANY)] * n
        args += list(accs)
        aliases = {1 + 2 * n + t: t for t in range(n)}
    return list(pl.pallas_call(
        body, name="chip_sum",
        grid_spec=pltpu.PrefetchScalarGridSpec(
            num_scalar_prefetch=1, grid=(1,), in_specs=in_specs,
            out_specs=[_half_spec(*shp[1:], (None,), lambda i, sr: ((sr[2],), sr[0])) for shp in shapes]),
        out_shape=[jax.ShapeDtypeStruct(shp, F32) for shp in shapes],
        input_output_aliases=aliases,
        compiler_params=_params(("arbitrary",)),
    )(*args))


def _me():
    return lax.axis_index("x"), lax.axis_index("y"), lax.axis_index("c")


def _flip(v, bit):
    return 1 - v if bit else v


def exchange8(xs, bcast):
    blk = xs.shape if bcast else xs.shape[1:]

    def body(x_ref, o_ref, send_sems, recv_sems, loc_sem):
        mx, my, mc = _me()
        me = 4 * mx + 2 * my + mc
        src = (lambda j: x_ref) if bcast else (lambda j: x_ref.at[j])
        loc = pltpu.make_async_copy(src(me), o_ref.at[me], loc_sem)
        loc.start()
        copies = []
        for o in range(1, N_DEV):
            px, py, pc = _flip(mx, o & 4), _flip(my, o & 2), _flip(mc, o & 1)
            cp = pltpu.make_async_remote_copy(
                src_ref=src(4 * px + 2 * py + pc), dst_ref=o_ref.at[me],
                send_sem=send_sems.at[o - 1], recv_sem=recv_sems.at[o - 1],
                device_id=(px, py, pc), device_id_type=MESH)
            cp.start()
            copies.append(cp)
        for cp in copies:
            cp.wait()
        loc.wait()

    return pl.pallas_call(
        body, name="exchange8_gather" if bcast else "exchange8_a2a",
        in_specs=[pl.BlockSpec(memory_space=pltpu.VMEM)], out_specs=pl.BlockSpec(memory_space=pltpu.VMEM),
        out_shape=jax.ShapeDtypeStruct((N_DEV,) + tuple(blk), xs.dtype),
        scratch_shapes=[pltpu.SemaphoreType.DMA((N_DEV - 1,)), pltpu.SemaphoreType.DMA((N_DEV - 1,)), pltpu.SemaphoreType.DMA],
        compiler_params=_params(),
    )(xs)


HBM = pl.BlockSpec(memory_space=pltpu.HBM)
SEM = pl.BlockSpec(memory_space=pltpu.SEMAPHORE)
EFFECT = pltpu.SideEffectType.DATAFLOW_SIDE_EFFECTING


def _hbm(a):
    return pltpu.with_memory_space_constraint(a, pltpu.HBM)


def _ici_copy(land, o, send_sem, recv_sem, sending):
    mx, my, mc = _me()
    px, py = _flip(mx, o & 2), _flip(my, o & 1)
    mine = _half_at(land, (2 * mx + my,), mc)
    return pltpu.make_async_remote_copy(
        src_ref=mine, dst_ref=mine if sending else _half_at(land, (2 * px + py,), mc),
        send_sem=send_sem, recv_sem=recv_sem, device_id=(px, py, mc), device_id_type=MESH)


N_PEERS = N_CHIPS - 1
DMA_SEM = pltpu.SemaphoreType.DMA(())


def gather_start(lands, groups, after, tag):
    n_layers, n = len(lands), len(lands[0])
    flat = [a for layer in lands for a in layer]
    n_in = n * n_layers
    n_grp = len(groups)
    n_sem = 2 * n_layers * n_grp * N_PEERS
    first = lambda l, g, recv: ((l * n_grp + g) * 2 + recv) * N_PEERS

    def body(*refs):
        land = refs[:n_in]
        sems = refs[n_in + 1:n_in + 1 + n_sem]
        token = refs[-1]
        for l in range(n_layers):
            for g, members in enumerate(groups):
                for t in members:
                    for o in range(1, N_CHIPS):
                        _ici_copy(land[l * n + t], o, sems[first(l, g, 0) + o - 1], sems[first(l, g, 1) + o - 1],
                                  True).start()
        token[...] = jnp.zeros_like(token)

    outs = pl.pallas_call(
        body, name=f"gather_start_{tag}",
        in_specs=[HBM] * n_in + [pl.BlockSpec(memory_space=pl.ANY)],
        out_specs=[SEM] * n_sem + [HBM] * n_in + [pl.BlockSpec(memory_space=pltpu.VMEM)],
        out_shape=[DMA_SEM] * n_sem + [pltpu.HBM(a.shape, a.dtype) for a in flat]
        + [jax.ShapeDtypeStruct((8, LANES), F32)],
        input_output_aliases={i: i + n_sem for i in range(n_in)},
        compiler_params=pltpu.CompilerParams(has_side_effects=EFFECT),
    )(*[_hbm(a) for a in flat], after)
    sems = [[(list(outs[first(l, g, 0):first(l, g, 0) + N_PEERS]), list(outs[first(l, g, 1):first(l, g, 1) + N_PEERS]))
             for g in range(n_grp)] for l in range(n_layers)]
    lands_thru = [list(outs[n_sem + l * n:n_sem + (l + 1) * n]) for l in range(n_layers)]
    return sems, lands_thru, outs[-1]


def gather_wait(tag, sems, lands, after):
    n = len(lands)
    send_sems, recv_sems = sems

    def body(*refs):
        land = refs[:n]
        send_r = refs[n:n + N_PEERS]
        recv_r = refs[n + N_PEERS:n + 2 * N_PEERS]
        for t in range(n):
            for o in range(1, N_CHIPS):
                _ici_copy(land[t], o, send_r[o - 1], recv_r[o - 1], True).wait_send()
                _ici_copy(land[t], o, send_r[o - 1], recv_r[o - 1], False).wait_recv()

    return list(pl.pallas_call(
        body, name=f"gather_wait_{tag}",
        in_specs=[HBM] * n + [SEM] * (2 * N_PEERS) + [pl.BlockSpec(memory_space=pl.ANY)],
        out_specs=[HBM] * n,
        out_shape=[pltpu.HBM(a.shape, a.dtype) for a in lands],
        input_output_aliases={i: i for i in range(n)},
        compiler_params=pltpu.CompilerParams(has_side_effects=EFFECT),
    )(*lands, *send_sems, *recv_sems, after))


def gather_forward(lands):
    n = len(lands)

    def body(*refs):
        dst = refs[n:2 * n]
        send_sems, recv_sems = refs[2 * n:]
        mx, my, mc = _me()
        fwds = []
        for t in range(n):
            for o in range(1, N_CHIPS):
                slot = 2 * _flip(mx, o & 2) + _flip(my, o & 1)
                mine = _half_at(dst[t], (slot,), mc)
                theirs = _half_at(dst[t], (slot,), 1 - mc)
                cp = pltpu.make_async_remote_copy(
                    src_ref=mine, dst_ref=mine, send_sem=send_sems.at[t, o - 1], recv_sem=recv_sems.at[t, o - 1],
                    device_id=(mx, my, 1 - mc), device_id_type=MESH)
                cp.start()
                fwds.append((cp, pltpu.make_async_remote_copy(
                    src_ref=theirs, dst_ref=theirs, send_sem=send_sems.at[t, o - 1], recv_sem=recv_sems.at[t, o - 1],
                    device_id=(mx, my, 1 - mc), device_id_type=MESH)))
        for cp, arrival in fwds:
            cp.wait_send()
            arrival.wait_recv()

    any_spec = pl.BlockSpec(memory_space=pl.ANY)
    return list(pl.pallas_call(
        body, name="gather_forward",
        in_specs=[any_spec] * n, out_specs=[any_spec] * n,
        out_shape=[jax.ShapeDtypeStruct(a.shape, a.dtype) for a in lands],
        input_output_aliases={t: t for t in range(n)},
        scratch_shapes=[pltpu.SemaphoreType.DMA((n, N_CHIPS - 1)), pltpu.SemaphoreType.DMA((n, N_CHIPS - 1))],
        compiler_params=_params(),
    )(*lands))


def _scatter_copy(src, land, o, send_sem, recv_sem):
    mx, my, mc = _me()
    px, py = _flip(mx, o & 2), _flip(my, o & 1)
    return pltpu.make_async_remote_copy(
        src_ref=src.at[2 * px + py], dst_ref=land.at[o - 1],
        send_sem=send_sem, recv_sem=recv_sem, device_id=(px, py, mc), device_id_type=MESH)


def scatter_start(pbs, tag, after):
    n = len(pbs)
    lands = [lax.empty((N_CHIPS - 1,) + p.shape[1:], p.dtype) for p in pbs]

    def body(*refs):
        src = refs[:n]
        land = refs[n:2 * n]
        send_sems = refs[2 * n + 1:2 * n + 1 + N_PEERS]
        recv_sems = refs[2 * n + 1 + N_PEERS:2 * n + 1 + 2 * N_PEERS]
        token = refs[-1]
        for t in range(n):
            for o in range(1, N_CHIPS):
                _scatter_copy(src[t], land[t], o, send_sems[o - 1], recv_sems[o - 1]).start()
        token[...] = jnp.zeros_like(token)

    n_sem = 2 * N_PEERS
    arrs = list(pbs) + lands
    outs = pl.pallas_call(
        body, name=f"scatter_start_{tag}",
        in_specs=[HBM] * (2 * n) + [pl.BlockSpec(memory_space=pl.ANY)],
        out_specs=[SEM] * n_sem + [HBM] * (2 * n) + [pl.BlockSpec(memory_space=pltpu.VMEM)],
        out_shape=[DMA_SEM] * n_sem + [pltpu.HBM(a.shape, a.dtype) for a in arrs]
        + [jax.ShapeDtypeStruct((8, LANES), F32)],
        input_output_aliases={i: i + n_sem for i in range(2 * n)},
        compiler_params=pltpu.CompilerParams(has_side_effects=EFFECT),
    )(*[_hbm(a) for a in arrs], after)
    return (list(outs[:N_PEERS]), list(outs[N_PEERS:n_sem]), list(outs[n_sem:n_sem + n]),
            list(outs[n_sem + n:n_sem + 2 * n]), outs[-1])


def scatter_wait(tag, send_sems, recv_sems, pbs, lands, after):
    n = len(pbs)

    def body(*refs):
        src = refs[:n]
        land = refs[n:2 * n]
        send_r = refs[2 * n:2 * n + N_PEERS]
        recv_r = refs[2 * n + N_PEERS:2 * n + 2 * N_PEERS]
        for t in range(n):
            for o in range(1, N_CHIPS):
                cp = _scatter_copy(src[t], land[t], o, send_r[o - 1], recv_r[o - 1])
                cp.wait_send()
                cp.wait_recv()

    arrs = list(pbs) + list(lands)
    outs = pl.pallas_call(
        body, name=f"scatter_wait_{tag}",
        in_specs=[HBM] * (2 * n) + [SEM] * (2 * N_PEERS) + [pl.BlockSpec(memory_space=pl.ANY)],
        out_specs=[HBM] * (2 * n),
        out_shape=[pltpu.HBM(a.shape, a.dtype) for a in arrs],
        input_output_aliases={i: i for i in range(2 * n)},
        compiler_params=pltpu.CompilerParams(has_side_effects=EFFECT),
    )(*arrs, *send_sems, *recv_sems, after)
    return list(outs[n:])


def _pair_copies(srcs, lands, send_sem, recv_sem):
    mx, my, mc = _me()
    return [pltpu.make_async_remote_copy(
        src_ref=_half_at(src, (slice(None),) * (len(src.shape) - 2), 1 - mc), dst_ref=land,
        send_sem=send_sem, recv_sem=recv_sem, device_id=(mx, my, 1 - mc), device_id_type=MESH)
        for src, land in zip(srcs, lands)]


def pair_start(gs, tag, after):
    n = len(gs)
    lands = [lax.empty(g.shape[:-2] + _half_shape(*g.shape[-2:]), g.dtype) for g in gs]

    def body(*refs):
        send_sem, recv_sem = refs[2 * n + 1], refs[2 * n + 2]
        token = refs[-1]
        for cp in _pair_copies(refs[:n], refs[n:2 * n], send_sem, recv_sem):
            cp.start()
        token[...] = jnp.zeros_like(token)

    arrs = list(gs) + lands
    outs = pl.pallas_call(
        body, name=f"pair_start_{tag}",
        in_specs=[HBM] * (2 * n) + [pl.BlockSpec(memory_space=pl.ANY)],
        out_specs=[SEM, SEM] + [HBM] * (2 * n) + [pl.BlockSpec(memory_space=pltpu.VMEM)],
        out_shape=[DMA_SEM, DMA_SEM] + [pltpu.HBM(a.shape, a.dtype) for a in arrs] + [jax.ShapeDtypeStruct((8, LANES), F32)],
        input_output_aliases={i: i + 2 for i in range(2 * n)},
        compiler_params=pltpu.CompilerParams(has_side_effects=EFFECT),
    )(*[_hbm(a) for a in arrs], after)
    return outs[0], outs[1], list(outs[2:2 + n]), list(outs[2 + n:2 + 2 * n]), outs[-1]


def pair_wait(tag, send_sem, recv_sem, gs, lands, after):
    n = len(gs)

    def body(*refs):
        for cp in _pair_copies(refs[:n], refs[n:2 * n], refs[2 * n], refs[2 * n + 1]):
            cp.wait_send()
            cp.wait_recv()

    arrs = list(gs) + list(lands)
    outs = pl.pallas_call(
        body, name=f"pair_wait_{tag}",
        in_specs=[HBM] * (2 * n) + [SEM, SEM, pl.BlockSpec(memory_space=pl.ANY)],
        out_specs=[HBM] * (2 * n),
        out_shape=[pltpu.HBM(a.shape, a.dtype) for a in arrs],
        input_output_aliases={i: i for i in range(2 * n)},
        compiler_params=pltpu.CompilerParams(has_side_effects=EFFECT),
    )(*arrs, send_sem, recv_sem, after)
    return list(outs[:n]), list(outs[n:])


def _gather8_copy(x, land, o, send_sem, recv_sem, sending):
    mx, my, mc = _me()
    px, py, pc = _flip(mx, o & 4), _flip(my, o & 2), _flip(mc, o & 1)
    slot = 4 * mx + 2 * my + mc if sending else 4 * px + 2 * py + pc
    return pltpu.make_async_remote_copy(
        src_ref=x, dst_ref=land.at[slot], send_sem=send_sem, recv_sem=recv_sem,
        device_id=(px, py, pc), device_id_type=MESH)


def gather8_start(x, land, after, tag):
    n_peer = N_DEV - 1

    def body(x_ref, land_ref, after_ref, *rest):
        send_sems, recv_sems = rest[:n_peer], rest[n_peer:2 * n_peer]
        token = rest[-1]
        for o in range(1, N_DEV):
            _gather8_copy(x_ref, land_ref, o, send_sems[o - 1], recv_sems[o - 1], True).start()
        token[...] = jnp.zeros_like(token)

    outs = pl.pallas_call(
        body, name=f"gather8_start_{tag}",
        in_specs=[HBM, HBM, pl.BlockSpec(memory_space=pl.ANY)],
        out_specs=[SEM] * (2 * n_peer) + [HBM, HBM, pl.BlockSpec(memory_space=pltpu.VMEM)],
        out_shape=[DMA_SEM] * (2 * n_peer) + [pltpu.HBM(x.shape, x.dtype), pltpu.HBM(land.shape, land.dtype),
                                              jax.ShapeDtypeStruct((8, LANES), F32)],
        input_output_aliases={0: 2 * n_peer, 1: 2 * n_peer + 1},
        compiler_params=pltpu.CompilerParams(has_side_effects=EFFECT),
    )(_hbm(x), _hbm(land), after)
    return list(outs[:n_peer]), list(outs[n_peer:2 * n_peer]), outs[2 * n_peer], outs[2 * n_peer + 1], outs[-1]


def gather8_wait(tag, send_sems, recv_sems, x, land, after):
    n_peer = N_DEV - 1

    def body(x_ref, land_ref, *rest):
        send_r, recv_r = rest[:n_peer], rest[n_peer:2 * n_peer]
        for o in range(1, N_DEV):
            _gather8_copy(x_ref, land_ref, o, send_r[o - 1], recv_r[o - 1], True).wait_send()
            _gather8_copy(x_ref, land_ref, o, send_r[o - 1], recv_r[o - 1], False).wait_recv()

    return pl.pallas_call(
        body, name=f"gather8_wait_{tag}",
        in_specs=[HBM, HBM] + [SEM] * (2 * n_peer) + [pl.BlockSpec(memory_space=pl.ANY)],
        out_specs=[HBM, HBM],
        out_shape=[pltpu.HBM(x.shape, x.dtype), pltpu.HBM(land.shape, land.dtype)],
        input_output_aliases={0: 0, 1: 1},
        compiler_params=pltpu.CompilerParams(has_side_effects=EFFECT),
    )(x, land, *send_sems, *recv_sems, after)[1]


def _fill_copies(fs, send_sem, recv_sem, sending):
    mx, my, mc = _me()
    out = []
    for f in fs:
        region = _half_at(f, (slice(None),), mc if sending else 1 - mc)
        out.append(pltpu.make_async_remote_copy(
            src_ref=region, dst_ref=region, send_sem=send_sem, recv_sem=recv_sem,
            device_id=(mx, my, 1 - mc), device_id_type=MESH))
    return out


def fill_start(fs, tag, after):
    n = len(fs)

    def body(*refs):
        send_sem, recv_sem = refs[n + 1], refs[n + 2]
        token = refs[-1]
        for cp in _fill_copies(refs[:n], send_sem, recv_sem, True):
            cp.start()
        token[...] = jnp.zeros_like(token)

    outs = pl.pallas_call(
        body, name=f"fill_start_{tag}",
        in_specs=[HBM] * n + [pl.BlockSpec(memory_space=pl.ANY)],
        out_specs=[SEM, SEM] + [HBM] * n + [pl.BlockSpec(memory_space=pltpu.VMEM)],
        out_shape=[DMA_SEM, DMA_SEM] + [pltpu.HBM(f.shape, f.dtype) for f in fs] + [jax.ShapeDtypeStruct((8, LANES), F32)],
        input_output_aliases={i: i + 2 for i in range(n)},
        compiler_params=pltpu.CompilerParams(has_side_effects=EFFECT),
    )(*[_hbm(f) for f in fs], after)
    return outs[0], outs[1], list(outs[2:2 + n]), outs[-1]


def fill_wait(tag, send_sem, recv_sem, fs, after):
    n = len(fs)

    def body(*refs):
        for cp in _fill_copies(refs[:n], refs[n], refs[n + 1], True):
            cp.wait_send()
        for cp in _fill_copies(refs[:n], refs[n], refs[n + 1], False):
            cp.wait_recv()

    return list(pl.pallas_call(
        body, name=f"fill_wait_{tag}",
        in_specs=[HBM] * n + [SEM, SEM, pl.BlockSpec(memory_space=pl.ANY)],
        out_specs=[HBM] * n,
        out_shape=[pltpu.HBM(f.shape, f.dtype) for f in fs],
        input_output_aliases={i: i for i in range(n)},
        compiler_params=pltpu.CompilerParams(has_side_effects=EFFECT),
    )(*fs, send_sem, recv_sem, after))


def _pack_rows(parts, d):
    rows, spans = [], []
    at = 0
    for p in parts:
        flat = p.reshape(-1)
        n_rows = -(-flat.shape[0] // (8 * d)) * 8
        flat = jnp.pad(flat, (0, n_rows * d - flat.shape[0]))
        rows.append(flat.reshape(n_rows, d))
        spans.append((at, p.shape))
        at += n_rows
    return jnp.concatenate(rows, axis=0), spans


def _unpack_rows(packed, spans):
    lead, d = packed.shape[:-2], packed.shape[-1]
    out = []
    for at, shape in spans:
        n = math.prod(shape)
        n_rows = -(-n // d)
        out.append(packed[..., at:at + n_rows, :].reshape(lead + (-1,))[..., :n].reshape(lead + tuple(shape)))
    return out


def _rotate_half_matrix():
    half = QK_ROPE // 2
    idx = jnp.arange(QK_ROPE)
    src = jnp.where(idx < half, idx + half, idx - half)
    sign = jnp.where(idx < half, -1.0, 1.0)
    return (jnp.zeros((QK_ROPE, QK_ROPE), F32).at[src, idx].set(sign)).astype(BF16)


def kernel(x, c, positions, ada_w, ada_b, ffn1_norm, ffn1_w_gate, ffn1_w_up, ffn1_w_down, mix_norm, w_in, pool_w, pool_scale, q_a_norm, w_q_b, kv_a_norm, w_kv_b, w_out, ffn2_norm, ffn2_w_gate, ffn2_w_up, ffn2_w_down, final_norm, loss_target, m_ada_w, m_ada_b, m_ffn1_norm, m_ffn1_w_gate, m_ffn1_w_up, m_ffn1_w_down, m_mix_norm, m_w_in, m_pool_w, m_pool_scale, m_q_a_norm, m_w_q_b, m_kv_a_norm, m_w_kv_b, m_w_out, m_ffn2_norm, m_ffn2_w_gate, m_ffn2_w_up, m_ffn2_w_down, m_final_norm, v_ada_w, v_ada_b, v_ffn1_norm, v_ffn1_w_gate, v_ffn1_w_up, v_ffn1_w_down, v_mix_norm, v_w_in, v_pool_w, v_pool_scale, v_q_a_norm, v_w_q_b, v_kv_a_norm, v_w_kv_b, v_w_out, v_ffn2_norm, v_ffn2_w_gate, v_ffn2_w_up, v_ffn2_w_down, v_final_norm):
    mx, my, mc = _me()
    chip = 2 * mx + my
    half = jnp.reshape(mc, (1,)).astype(jnp.int32)
    chip1 = jnp.reshape(chip, (1,)).astype(jnp.int32)
    n_layers, d, ada_cols = ada_w.shape
    xt = x[0]
    tgt = loss_target[0]

    inv_freq = 1.0 / (ROPE_THETA ** (jnp.arange(0, QK_ROPE, 2, dtype=F32) / QK_ROPE))
    ang = positions[0].astype(F32)[:, None] * inv_freq
    ang = jnp.concatenate([ang, ang], axis=-1)
    cos, sin = jnp.cos(ang), jnp.sin(ang)
    rot = _rotate_half_matrix()
    rot_t = rot.T

    c_all = exchange8(c, True).reshape(N_DEV, d)
    c16 = jnp.pad(c_all, ((0, 8), (0, 0)))
    ada_b_loc = lax.dynamic_slice_in_dim(ada_b, chip * ada_cols, ada_cols, axis=1).reshape(n_layers, 1, ada_cols)
    mod_part = ada_fwd(c16, ada_w, ada_b_loc)[:, :N_DEV]
    mod_got = exchange8(jnp.transpose(mod_part, (1, 0, 2)), False)
    mod = jnp.transpose(mod_got.reshape(N_CHIPS, 2, n_layers, ada_cols)[:, 0], (1, 0, 2))
    mod = mod.reshape(n_layers, 9, 1, d)

    tr = lambda a: jnp.transpose(a, (0, 2, 1))
    local = [tr(ffn1_w_gate), tr(ffn1_w_up), ffn1_w_down, tr(w_in), tr(w_q_b), w_kv_b, w_out,
             tr(ffn2_w_gate), tr(ffn2_w_up), ffn2_w_down]
    ffn1_pos, mixer_pos, ffn2_pos = (0, 1, 2), (3, 4, 5, 6), (7, 8, 9)
    rest_pos = mixer_pos + ffn2_pos

    def cast_all(layers, after):
        by_shape = {}
        for t, w in enumerate(local):
            by_shape.setdefault(w.shape, []).append(t)
        out = [None] * len(local)
        for ts in by_shape.values():
            for t, per_layer in zip(ts, cast_place([local[t] for t in ts], chip1, layers, after)):
                out[t] = per_layer
        return out

    placed = cast_all((0,), mod)
    g_sems, lands_fly, g_token = gather_start([[p[0] for p in placed]], (ffn1_pos, mixer_pos, ffn2_pos), mod, "first")
    if n_layers > 1:
        later = tuple(range(1, n_layers))
        placed = cast_all(later, g_token)
        more_sems, more_fly, g_token = gather_start(
            [[p[j] for p in placed] for j in range(len(later))], (ffn1_pos, rest_pos), g_token, "rest")
        g_sems, lands_fly = g_sems + more_sems, lands_fly + more_fly
    gathered = []

    row = lambda a, l: a[l].reshape(1, -1)
    saved = []
    for l in range(n_layers):
        def fetch(tag, group, members, after, l=l):
            return gather_forward(gather_wait(tag, g_sems[l][group], [lands_fly[l][t] for t in members], after))

        g1, u1, d1 = fetch(f"{l}a", 0, ffn1_pos, xt if l else g_token)
        sv = dict(x0=xt)
        xt, sv["h1"], sv["a1"], sv["sl1"], sv["dsu1"], sv["y1"] = ffn_fwd(
            xt, row(ffn1_norm, l), mod[l, 0], mod[l, 1], mod[l, 2], g1, u1, d1)
        sv["x1"] = xt
        if l == 0:
            win, wq, wkv, wout = fetch("0b", 1, mixer_pos, xt)
        else:
            win, wq, wkv, wout, g2, u2, d2 = fetch(f"{l}b", 1, rest_pos, xt)
        win = win.reshape(-1, d)
        sv["h2"], u, cq, ckv, kr = mix_in_fwd(xt, row(mix_norm, l), mod[l, 3], mod[l, 4], win)
        sv["cq"], sv["ckv"] = cq, ckv
        yp, sv["diff"] = pool_fwd(u, pool_w[l], row(pool_scale, l))
        qh, kh, vh, sv["ql"], sv["kvl"] = mla_qkv_fwd(
            cq, ckv, kr, row(q_a_norm, l), row(kv_a_norm, l), wq, wkv, cos, sin, rot)
        sv["qkv"] = (qh, kh, vh)
        om = attn_fwd(qh, kh, vh)
        xt, sv["ycat"], sv["y2"] = out_proj_fwd(yp, om, wout, xt, mod[l, 5])
        sv["x2"] = xt
        if l == 0:
            g2, u2, d2 = fetch("0c", 2, ffn2_pos, xt)
        gathered.append([g1, u1, d1, win, wq, wkv, wout, g2, u2, d2])
        xt, sv["h3"], sv["a3"], sv["sl3"], sv["dsu3"], sv["y3"] = ffn_fwd(
            xt, row(ffn2_norm, l), mod[l, 6], mod[l, 7], mod[l, 8], g2, u2, d2)
        saved.append(sv)

    loss_vec, dx, d_final_norm = final_loss(xt, final_norm.reshape(1, d), tgt)
    loss = lax.psum(loss_vec[0, 0], ("x", "y", "c"))

    none = [None] * n_layers
    dmods, dnorm1, dnorm2, dnorm3 = list(none), list(none), list(none), list(none)
    dpw, dps, dqan_l, dkvan_l = list(none), list(none), list(none), list(none)
    reduced = [None] * len(local)
    stages = []
    sel_of = lambda l: jnp.stack([mc, chip, jnp.asarray(l, mc.dtype)]).astype(jnp.int32)

    def to_chips(job, after_wait, after_start):
        send, recv, g_fly, lands_p = job.pop("pair")
        g_fly, got = pair_wait(job["tag"], send, recv, g_fly, lands_p, after_wait)
        n_w = len(job["pos"])
        pbs, job["owns"] = pair_add(g_fly[:n_w], g_fly[n_w:], got[:n_w], got[n_w:], sel_of(job["l"]))
        job["scatter"] = scatter_start(pbs, job["tag"], after_start)
        return job["scatter"][4][0, 0]

    def finish(job, after):
        s_send, s_recv, pbs_fly, lands_j, _ = job.pop("scatter")
        parts = scatter_wait(job["tag"], s_send, s_recv, pbs_fly, lands_j, after)
        sums = chip_sum(job["owns"], parts, sel_of(job["l"]), [(n_layers,) + shp for shp in job["shapes"]],
                        [reduced[t] for t in job["pos"]])
        for t, total_t in zip(job["pos"], sums):
            reduced[t] = total_t

    def checkpoint(tag, l, positions, grads_, done, before_scatter=None):
        send, recv, g_fly, lands_p, tok = pair_start([g[0] for g in grads_] + [g[1] for g in grads_], tag, done)
        order = tok[0, 0]
        if stages:
            order = order + to_chips(stages[-1], done, done if before_scatter is None else before_scatter)
        if len(stages) >= 3:
            finish(stages[-3], done)
        stages.append(dict(tag=tag, l=l, pos=positions, shapes=[g[0].shape for g in grads_],
                           pair=(send, recv, g_fly, lands_p)))
        return order

    def small_gather(tag, parts, after):
        packed, spans = _pack_rows(parts, d)
        land = lax.dynamic_update_index_in_dim(lax.empty((N_DEV,) + packed.shape, F32), packed, 4 * mx + 2 * my + mc, 0)
        return gather8_start(packed, land, after, tag), spans

    order = None

    for l in reversed(range(n_layers)):
        sv = saved[l]
        g1, u1, d1, win, wq, wkv, wout, g2, u2, d2 = gathered[l]
        win = win.reshape(-1, d)
        gt3 = mod[l, 8] if order is None else mod[l, 8] + order
        dy, dgt, dup = ffn_bwd_act(dx, sv["sl3"], sv["dsu3"], gt3, d2)
        dx, dvec3 = ffn_bwd_in(dx, sv["x2"], sv["y3"], dgt, dup, row(ffn2_norm, l), mod[l, 7], g2, u2)
        (g_g2, g_u2), g_d2 = tn_mm_pair(dgt, dup, sv["h3"], chip1), nn_mm(sv["a3"], dy, chip1)
        dy2, dyp, dom, dg2 = out_proj_bwd(dx, sv["y2"], mod[l, 5], wout)
        g_wout = nn_mm(sv["ycat"], dy2, chip1)
        qh, kh, vh = sv["qkv"]
        dqh, dkh, dvh = attn_bwd(qh, kh, vh, dom)
        dcq, dckv, dkr_in, gq, gkv, dqan_l[l], dkvan_l[l] = mla_qkv_bwd(
            dqh, dkh, dvh, sv["cq"], sv["ckv"], row(q_a_norm, l), row(kv_a_norm, l), wq, wkv, cos, sin, rot_t)
        g_wq, g_wkv = tn_mm(gq, sv["ql"][None], chip1), tn_mm(sv["kvl"][None], gkv, chip1)
        du, dpw[l], dps[l] = pool_bwd(dyp, sv["diff"], pool_w[l], row(pool_scale, l))
        dx, dz, dvec2 = mix_in_bwd(dx, du, dcq, dckv, dkr_in, sv["x1"], row(mix_norm, l), mod[l, 4], win)
        g_win = nn_mm(dz.reshape(N_CHIPS, -1, dz.shape[1]), sv["h2"], chip1)
        dnorm2[l], dnorm3[l] = dvec2[3], dvec3[3]
        dmod_rest = jnp.concatenate([dvec2[0:2], dg2, dvec3[0:3]], axis=0)
        if l == 0:
            early = small_gather("early", [jnp.stack(dmods[1:]), dmod_rest, jnp.stack(dnorm1[1:]), jnp.stack(dnorm2),
                                           jnp.stack(dnorm3), d_final_norm, jnp.stack(dps), jnp.stack(dqan_l),
                                           jnp.stack(dkvan_l), jnp.stack(dpw)], dx)
        order = checkpoint(f"{l}a", l, rest_pos, [g_win, g_wq, g_wkv, g_wout, g_g2, g_u2, g_d2], dx,
                           early[0][4] if l == 0 else None)
        dy, dgt, dup = ffn_bwd_act(dx, sv["sl1"], sv["dsu1"], mod[l, 2] + order, d1)
        dx, dvec1 = ffn_bwd_in(dx, sv["x0"], sv["y1"], dgt, dup, row(ffn1_norm, l), mod[l, 1], g1, u1)
        (g_g1, g_u1), g_d1 = tn_mm_pair(dgt, dup, sv["h1"], chip1), nn_mm(sv["a1"], dy, chip1)
        dmods[l] = jnp.concatenate([dvec1[0:3], dmod_rest], axis=0)
        dnorm1[l] = dvec1[3]
        if l == 0:
            late = small_gather("late", [dvec1[0:3], dvec1[3]], dx)
        order = checkpoint(f"{l}b", l, ffn1_pos, [g_g1, g_u1, g_d1], dx, late[0][4] if l == 0 else None)

    to_chips(stages[-1], stages[-2]["scatter"][4], stages[-2]["scatter"][4])
    sent = stages[-1]["scatter"][4]
    got_early = gather8_wait("early", *early[0][:4], sent)
    got_late = gather8_wait("late", *late[0][:4], sent)
    each_rest, each0_rest = _unpack_rows(got_early, early[1])[:2]
    each0_first = _unpack_rows(got_late, late[1])[0]
    dmod_all = jnp.concatenate([jnp.concatenate([each0_first, each0_rest], axis=1)[:, None], each_rest], axis=1)
    dmod_all = dmod_all.reshape(N_DEV, n_layers, 9 * d)
    dmod_loc = lax.dynamic_slice_in_dim(dmod_all, chip * ada_cols, ada_cols, axis=2)
    dmod16 = jnp.pad(jnp.transpose(dmod_loc, (1, 0, 2)), ((0, 0), (0, 8), (0, 0)))

    weights = [ada_w, ada_b, ffn1_norm, ffn1_w_gate, ffn1_w_up, ffn1_w_down, mix_norm, w_in, pool_w, pool_scale,
               q_a_norm, w_q_b, kv_a_norm, w_kv_b, w_out, ffn2_norm, ffn2_w_gate, ffn2_w_up, ffn2_w_down, final_norm]
    ms = [m_ada_w, m_ada_b, m_ffn1_norm, m_ffn1_w_gate, m_ffn1_w_up, m_ffn1_w_down, m_mix_norm, m_w_in, m_pool_w,
          m_pool_scale, m_q_a_norm, m_w_q_b, m_kv_a_norm, m_w_kv_b, m_w_out, m_ffn2_norm, m_ffn2_w_gate, m_ffn2_w_up,
          m_ffn2_w_down, m_final_norm]
    vs = [v_ada_w, v_ada_b, v_ffn1_norm, v_ffn1_w_gate, v_ffn1_w_up, v_ffn1_w_down, v_mix_norm, v_w_in, v_pool_w,
          v_pool_scale, v_q_a_norm, v_w_q_b, v_kv_a_norm, v_w_kv_b, v_w_out, v_ffn2_norm, v_ffn2_w_gate, v_ffn2_w_up,
          v_ffn2_w_down, v_final_norm]
    transposed = (3, 4, 7, 11, 16, 17)
    outs = [None] * len(weights)

    outs[0] = adamw(ada_w, ada_bwd(c16, dmod16), m_ada_w, v_ada_w)
    for job in stages[-3:]:
        finish(job, outs[0][1])
    fill_a = fill_start([reduced[t] for t in rest_pos], "a", outs[0][1])
    fill_b = fill_start([reduced[t] for t in ffn1_pos], "b", fill_a[3])

    (g_dmod_rest, g_dmod0_rest, g_n1_rest, g_n2, g_n3, g_fn, g_ps, g_qan, g_kvan, g_pw) = _unpack_rows(
        sum_devices(got_early, fill_b[3]), early[1])
    g_dmod0_first, g_n1_first = _unpack_rows(sum_devices(got_late, fill_b[3]), late[1])
    g_ada_b = jnp.concatenate([jnp.concatenate([g_dmod0_first, g_dmod0_rest], axis=0)[None], g_dmod_rest], axis=0)
    g_n1 = jnp.concatenate([g_n1_first[None], g_n1_rest], axis=0)
    grads = [None, g_ada_b, g_n1, None, None, None, g_n2, None, g_pw, g_ps, g_qan, None, g_kvan, None, None, g_n3,
             None, None, None, g_fn]
    big = [i for i, g in enumerate(grads) if g is None and i > 0]
    for i, (w, g, m, v) in enumerate(zip(weights, grads, ms, vs)):
        if g is not None:
            outs[i] = adamw(w, g.reshape(w.shape), m, v)

    def update(positions, fly, after):
        filled = fill_wait(fly[0], fly[1], fly[2], fly[3], after)
        for t, g in zip(positions, filled):
            i = big[t]
            if i in transposed:
                outs[i] = tuple(tr(o) for o in adamw(tr(weights[i]), g, tr(ms[i]), tr(vs[i]), copy_g=True))
            else:
                outs[i] = adamw(weights[i], g, ms[i], vs[i], copy_g=True)

    update(rest_pos, ("a",) + tuple(fill_a[:3]), outs[8][1])
    update(ffn1_pos, ("b",) + tuple(fill_b[:3]), outs[big[rest_pos[-1]]][1])
    return (loss, dx.reshape(x.shape), *[t[0] for t in outs], *[t[1] for t in outs], *[t[2] for t in outs],
            *[t[3] for t in outs])
```
